```python
import math
import jax, jax.numpy as jnp
from jax import lax
import numpy as np

D_MODEL = 1024
BATCH = 8
SEQ = 2048
DEPTH = 4

CONV_WIDTH = D_MODEL // 2
CONV_HEAD_DIM = 64
N_CONV_HEADS = CONV_WIDTH // CONV_HEAD_DIM
CONV_K = 3
SSM_WIDTH = D_MODEL - CONV_WIDTH
SSM_GROUP = 16
SSM_GROUPS = SSM_WIDTH // SSM_GROUP
SSM_STATE = 64
MIX_WIDTH = CONV_WIDTH + SSM_WIDTH
IN_COLS = 3 * CONV_WIDTH + SSM_WIDTH
D_FF = ((8 * D_MODEL // 3 + 127) // 128) * 128
PLE_DIM = 256
EPS = 1e-6
DT_MIN = 1e-3
DT_MAX = 1e-1

kernel_name = "hymba_style_conv_s5_macaron_trunk"


def rmsnorm(x, g):
    xf = x.astype(jnp.float32)
    xf = xf * lax.rsqrt(jnp.mean(xf * xf, axis=-1, keepdims=True) + EPS)
    return (xf * g.astype(jnp.float32)).astype(x.dtype)


def swiglu(u, w_gate, w_up, w_down):
    return (jax.nn.silu(u @ w_gate) * (u @ w_up)) @ w_down


def short_conv_group(z_b, z_c, z_v, conv_w, conv_b):
    v = z_c * z_v
    rhs = conv_w.astype(v.dtype)[:, None, :]
    y = lax.conv_general_dilated(
        v, rhs, window_strides=(1,), padding=[(CONV_K - 1, 0)],
        dimension_numbers=("NWC", "WIO", "NWC"), feature_group_count=CONV_WIDTH)
    return z_b * (y + conv_b.astype(v.dtype))


def s5_group(u_s, A_re, A_im, B_re, B_im, C_re, C_im, D, log_dt, glu_w, glu_b):
    f32 = jnp.float32
    bsz, L, _ = u_s.shape
    u = u_s.astype(f32).reshape(bsz, L, SSM_GROUPS, SSM_GROUP)
    ar, ai = A_re.astype(f32), A_im.astype(f32)
    dt = jnp.exp(log_dt.astype(f32))[:, None]
    mag = jnp.exp(ar * dt)
    ph = ai * dt
    lb_re, lb_im = mag * jnp.cos(ph), mag * jnp.sin(ph)
    nr, ni = lb_re - 1.0, lb_im
    den = ar * ar + ai * ai
    f_re = (nr * ar + ni * ai) / den
    f_im = (ni * ar - nr * ai) / den
    br, bi = B_re.astype(f32), B_im.astype(f32)
    bb_re = f_re[..., None] * br - f_im[..., None] * bi
    bb_im = f_re[..., None] * bi + f_im[..., None] * br
    bu_re = jnp.einsum("blgh,gph->blgp", u, bb_re)
    bu_im = jnp.einsum("blgh,gph->blgp", u, bb_im)
    a_re = jnp.broadcast_to(lb_re[None, None], (1, L, SSM_GROUPS, SSM_STATE))
    a_im = jnp.broadcast_to(lb_im[None, None], (1, L, SSM_GROUPS, SSM_STATE))

    def combine(e1, e2):
        a1r, a1i, b1r, b1i = e1
        a2r, a2i, b2r, b2i = e2
        return (a1r * a2r - a1i * a2i,
                a1r * a2i + a1i * a2r,
                a2r * b1r - a2i * b1i + b2r,
                a2r * b1i + a2i * b1r + b2i)

    _, _, h_re, h_im = lax.associative_scan(combine, (a_re, a_im, bu_re, bu_im), axis=1)
    y = (jnp.einsum("blgp,ghp->blgh", h_re, C_re.astype(f32))
         - jnp.einsum("blgp,ghp->blgh", h_im, C_im.astype(f32))
         + D.astype(f32) * u)
    y = y.reshape(bsz, L, SSM_WIDTH)
    zg = jax.nn.gelu(y)
    out = zg * jax.nn.sigmoid(zg @ glu_w.astype(f32) + glu_b.astype(f32))
    return out.astype(u_s.dtype)


def _fwd_setup_inputs(seed: int = 0) -> dict:
    key = jax.random.key(seed)
    ks = iter(jax.random.split(key, 40))
    nrm = lambda shape, s: jax.random.normal(next(ks), shape, jnp.float32) * s
    gain = lambda shape: 1.0 + nrm(shape, 0.02)
    n = jnp.arange(SSM_STATE, dtype=jnp.float32)
    log_dt = jax.random.uniform(next(ks), (DEPTH, SSM_GROUPS), jnp.float32,
                                math.log(DT_MIN), math.log(DT_MAX))
    return {
        "x": nrm((BATCH, SEQ, D_MODEL), 1.0),
        "p": nrm((DEPTH, BATCH, SEQ, PLE_DIM), 1.0),
        "ffn1_norm": gain((DEPTH, D_MODEL)),
        "ffn1_w_gate": nrm((DEPTH, D_MODEL, D_FF), D_MODEL ** -0.5),
        "ffn1_w_up": nrm((DEPTH, D_MODEL, D_FF), D_MODEL ** -0.5),
        "ffn1_w_down": nrm((DEPTH, D_FF, D_MODEL), D_FF ** -0.5),
        "mix_norm": gain((DEPTH, D_MODEL)),
        "w_in": nrm((DEPTH, D_MODEL, IN_COLS), D_MODEL ** -0.5),
        "conv_w": nrm((DEPTH, CONV_K, CONV_WIDTH), CONV_K ** -0.5),
        "conv_b": nrm((DEPTH, CONV_WIDTH), 0.02),
        "ssm_A_re": -0.5 + nrm((DEPTH, SSM_GROUPS, SSM_STATE), 0.01),
        "ssm_A_im": math.pi * n + nrm((DEPTH, SSM_GROUPS, SSM_STATE), 0.01),
        "ssm_B_re": nrm((DEPTH, SSM_GROUPS, SSM_STATE, SSM_GROUP), (2 * SSM_GROUP) ** -0.5),
        "ssm_B_im": nrm((DEPTH, SSM_GROUPS, SSM_STATE, SSM_GROUP), (2 * SSM_GROUP) ** -0.5),
        "ssm_C_re": nrm((DEPTH, SSM_GROUPS, SSM_GROUP, SSM_STATE), (2 * SSM_STATE) ** -0.5),
        "ssm_C_im": nrm((DEPTH, SSM_GROUPS, SSM_GROUP, SSM_STATE), (2 * SSM_STATE) ** -0.5),
        "ssm_D": nrm((DEPTH, SSM_GROUPS, SSM_GROUP), 1.0),
        "ssm_log_dt": log_dt,
        "glu_w": nrm((DEPTH, SSM_WIDTH, SSM_WIDTH), SSM_WIDTH ** -0.5),
        "glu_b": nrm((DEPTH, SSM_WIDTH), 0.02),
        "conv_out_norm": gain((DEPTH, CONV_WIDTH)),
        "ssm_out_norm": gain((DEPTH, SSM_WIDTH)),
        "w_out": nrm((DEPTH, MIX_WIDTH, D_MODEL), MIX_WIDTH ** -0.5),
        "ffn2_norm": gain((DEPTH, D_MODEL)),
        "ffn2_w_gate": nrm((DEPTH, D_MODEL, D_FF), D_MODEL ** -0.5),
        "ffn2_w_up": nrm((DEPTH, D_MODEL, D_FF), D_MODEL ** -0.5),
        "ffn2_w_down": nrm((DEPTH, D_FF, D_MODEL), D_FF ** -0.5),
        "ple_norm": gain((DEPTH, D_MODEL)),
        "ple_w_gate": nrm((DEPTH, D_MODEL, D_MODEL), D_MODEL ** -0.5),
        "ple_w_proj": nrm((DEPTH, PLE_DIM, D_MODEL), PLE_DIM ** -0.5),
        "final_norm": gain((D_MODEL,)),
    }


def _fwd_reference(x, p, ffn1_norm, ffn1_w_gate, ffn1_w_up, ffn1_w_down, mix_norm, w_in,
              conv_w, conv_b, ssm_A_re, ssm_A_im, ssm_B_re, ssm_B_im, ssm_C_re, ssm_C_im,
              ssm_D, ssm_log_dt, glu_w, glu_b, conv_out_norm, ssm_out_norm, w_out,
              ffn2_norm, ffn2_w_gate, ffn2_w_up, ffn2_w_down, ple_norm, ple_w_gate,
              ple_w_proj, final_norm):
    h = x
    for i in range(DEPTH):
        h = h + 0.5 * swiglu(rmsnorm(h, ffn1_norm[i]), ffn1_w_gate[i], ffn1_w_up[i], ffn1_w_down[i])
        z = rmsnorm(h, mix_norm[i]) @ w_in[i]
        z_b = z[..., :CONV_WIDTH]
        z_c = z[..., CONV_WIDTH:2 * CONV_WIDTH]
        z_v = z[..., 2 * CONV_WIDTH:3 * CONV_WIDTH]
        z_s = z[..., 3 * CONV_WIDTH:]
        y_a = short_conv_group(z_b, z_c, z_v, conv_w[i], conv_b[i])
        y_s = s5_group(z_s, ssm_A_re[i], ssm_A_im[i], ssm_B_re[i], ssm_B_im[i],
                       ssm_C_re[i], ssm_C_im[i], ssm_D[i], ssm_log_dt[i], glu_w[i], glu_b[i])
        y = jnp.concatenate([rmsnorm(y_a, conv_out_norm[i]), rmsnorm(y_s, ssm_out_norm[i])], axis=-1)
        h = h + y @ w_out[i]
        h = h + 0.5 * swiglu(rmsnorm(h, ffn2_norm[i]), ffn2_w_gate[i], ffn2_w_up[i], ffn2_w_down[i])
        gate = jax.nn.sigmoid(rmsnorm(h, ple_norm[i]) @ ple_w_gate[i])
        h = h + (p[i] @ ple_w_proj[i]) * gate
    return rmsnorm(h, final_norm)


import jax as _jax
import jax.numpy as _jnp

TWIN_FORMAT = 'train_step'
FWD_PARAMS = ['x', 'p', 'ffn1_norm', 'ffn1_w_gate', 'ffn1_w_up', 'ffn1_w_down', 'mix_norm', 'w_in', 'conv_w', 'conv_b', 'ssm_A_re', 'ssm_A_im', 'ssm_B_re', 'ssm_B_im', 'ssm_C_re', 'ssm_C_im', 'ssm_D', 'ssm_log_dt', 'glu_w', 'glu_b', 'conv_out_norm', 'ssm_out_norm', 'w_out', 'ffn2_norm', 'ffn2_w_gate', 'ffn2_w_up', 'ffn2_w_down', 'ple_norm', 'ple_w_gate', 'ple_w_proj', 'final_norm']
TWIN_WEIGHTS = ['ffn1_norm', 'ffn1_w_gate', 'ffn1_w_up', 'ffn1_w_down', 'mix_norm', 'w_in', 'conv_w', 'conv_b', 'ssm_A_re', 'ssm_A_im', 'ssm_B_re', 'ssm_B_im', 'ssm_C_re', 'ssm_C_im', 'ssm_D', 'ssm_log_dt', 'glu_w', 'glu_b', 'conv_out_norm', 'ssm_out_norm', 'w_out', 'ffn2_norm', 'ffn2_w_gate', 'ffn2_w_up', 'ffn2_w_down', 'ple_norm', 'ple_w_gate', 'ple_w_proj', 'final_norm']
TWIN_DIFF_INPUT = 'x'
TWIN_INPUTS = ['x', 'p', 'ffn1_norm', 'ffn1_w_gate', 'ffn1_w_up', 'ffn1_w_down', 'mix_norm', 'w_in', 'conv_w', 'conv_b', 'ssm_A_re', 'ssm_A_im', 'ssm_B_re', 'ssm_B_im', 'ssm_C_re', 'ssm_C_im', 'ssm_D', 'ssm_log_dt', 'glu_w', 'glu_b', 'conv_out_norm', 'ssm_out_norm', 'w_out', 'ffn2_norm', 'ffn2_w_gate', 'ffn2_w_up', 'ffn2_w_down', 'ple_norm', 'ple_w_gate', 'ple_w_proj', 'final_norm', 'loss_target', 'm_ffn1_norm', 'm_ffn1_w_gate', 'm_ffn1_w_up', 'm_ffn1_w_down', 'm_mix_norm', 'm_w_in', 'm_conv_w', 'm_conv_b', 'm_ssm_A_re', 'm_ssm_A_im', 'm_ssm_B_re', 'm_ssm_B_im', 'm_ssm_C_re', 'm_ssm_C_im', 'm_ssm_D', 'm_ssm_log_dt', 'm_glu_w', 'm_glu_b', 'm_conv_out_norm', 'm_ssm_out_norm', 'm_w_out', 'm_ffn2_norm', 'm_ffn2_w_gate', 'm_ffn2_w_up', 'm_ffn2_w_down', 'm_ple_norm', 'm_ple_w_gate', 'm_ple_w_proj', 'm_final_norm', 'v_ffn1_norm', 'v_ffn1_w_gate', 'v_ffn1_w_up', 'v_ffn1_w_down', 'v_mix_norm', 'v_w_in', 'v_conv_w', 'v_conv_b', 'v_ssm_A_re', 'v_ssm_A_im', 'v_ssm_B_re', 'v_ssm_B_im', 'v_ssm_C_re', 'v_ssm_C_im', 'v_ssm_D', 'v_ssm_log_dt', 'v_glu_w', 'v_glu_b', 'v_conv_out_norm', 'v_ssm_out_norm', 'v_w_out', 'v_ffn2_norm', 'v_ffn2_w_gate', 'v_ffn2_w_up', 'v_ffn2_w_down', 'v_ple_norm', 'v_ple_w_gate', 'v_ple_w_proj', 'v_final_norm']
TWIN_OUTPUTS = ['loss', 'grad_x', 'grad_ffn1_norm', 'grad_ffn1_w_gate', 'grad_ffn1_w_up', 'grad_ffn1_w_down', 'grad_mix_norm', 'grad_w_in', 'grad_conv_w', 'grad_conv_b', 'grad_ssm_A_re', 'grad_ssm_A_im', 'grad_ssm_B_re', 'grad_ssm_B_im', 'grad_ssm_C_re', 'grad_ssm_C_im', 'grad_ssm_D', 'grad_ssm_log_dt', 'grad_glu_w', 'grad_glu_b', 'grad_conv_out_norm', 'grad_ssm_out_norm', 'grad_w_out', 'grad_ffn2_norm', 'grad_ffn2_w_gate', 'grad_ffn2_w_up', 'grad_ffn2_w_down', 'grad_ple_norm', 'grad_ple_w_gate', 'grad_ple_w_proj', 'grad_final_norm', 'delta_ffn1_norm', 'delta_ffn1_w_gate', 'delta_ffn1_w_up', 'delta_ffn1_w_down', 'delta_mix_norm', 'delta_w_in', 'delta_conv_w', 'delta_conv_b', 'delta_ssm_A_re', 'delta_ssm_A_im', 'delta_ssm_B_re', 'delta_ssm_B_im', 'delta_ssm_C_re', 'delta_ssm_C_im', 'delta_ssm_D', 'delta_ssm_log_dt', 'delta_glu_w', 'delta_glu_b', 'delta_conv_out_norm', 'delta_ssm_out_norm', 'delta_w_out', 'delta_ffn2_norm', 'delta_ffn2_w_gate', 'delta_ffn2_w_up', 'delta_ffn2_w_down', 'delta_ple_norm', 'delta_ple_w_gate', 'delta_ple_w_proj', 'delta_final_norm', 'new_m_ffn1_norm', 'new_m_ffn1_w_gate', 'new_m_ffn1_w_up', 'new_m_ffn1_w_down', 'new_m_mix_norm', 'new_m_w_in', 'new_m_conv_w', 'new_m_conv_b', 'new_m_ssm_A_re', 'new_m_ssm_A_im', 'new_m_ssm_B_re', 'new_m_ssm_B_im', 'new_m_ssm_C_re', 'new_m_ssm_C_im', 'new_m_ssm_D', 'new_m_ssm_log_dt', 'new_m_glu_w', 'new_m_glu_b', 'new_m_conv_out_norm', 'new_m_ssm_out_norm', 'new_m_w_out', 'new_m_ffn2_norm', 'new_m_ffn2_w_gate', 'new_m_ffn2_w_up', 'new_m_ffn2_w_down', 'new_m_ple_norm', 'new_m_ple_w_gate', 'new_m_ple_w_proj', 'new_m_final_norm', 'new_v_ffn1_norm', 'new_v_ffn1_w_gate', 'new_v_ffn1_w_up', 'new_v_ffn1_w_down', 'new_v_mix_norm', 'new_v_w_in', 'new_v_conv_w', 'new_v_conv_b', 'new_v_ssm_A_re', 'new_v_ssm_A_im', 'new_v_ssm_B_re', 'new_v_ssm_B_im', 'new_v_ssm_C_re', 'new_v_ssm_C_im', 'new_v_ssm_D', 'new_v_ssm_log_dt', 'new_v_glu_w', 'new_v_glu_b', 'new_v_conv_out_norm', 'new_v_ssm_out_norm', 'new_v_w_out', 'new_v_ffn2_norm', 'new_v_ffn2_w_gate', 'new_v_ffn2_w_up', 'new_v_ffn2_w_down', 'new_v_ple_norm', 'new_v_ple_w_gate', 'new_v_ple_w_proj', 'new_v_final_norm']
TWIN_LEAF_KINDS = {'loss': 'loss', 'grad_x': 'grad_x', 'grad_ffn1_norm': 'grad_w', 'grad_ffn1_w_gate': 'grad_w', 'grad_ffn1_w_up': 'grad_w', 'grad_ffn1_w_down': 'grad_w', 'grad_mix_norm': 'grad_w', 'grad_w_in': 'grad_w', 'grad_conv_w': 'grad_w', 'grad_conv_b': 'grad_w', 'grad_ssm_A_re': 'grad_w', 'grad_ssm_A_im': 'grad_w', 'grad_ssm_B_re': 'grad_w', 'grad_ssm_B_im': 'grad_w', 'grad_ssm_C_re': 'grad_w', 'grad_ssm_C_im': 'grad_w', 'grad_ssm_D': 'grad_w', 'grad_ssm_log_dt': 'grad_w', 'grad_glu_w': 'grad_w', 'grad_glu_b': 'grad_w', 'grad_conv_out_norm': 'grad_w', 'grad_ssm_out_norm': 'grad_w', 'grad_w_out': 'grad_w', 'grad_ffn2_norm': 'grad_w', 'grad_ffn2_w_gate': 'grad_w', 'grad_ffn2_w_up': 'grad_w', 'grad_ffn2_w_down': 'grad_w', 'grad_ple_norm': 'grad_w', 'grad_ple_w_gate': 'grad_w', 'grad_ple_w_proj': 'grad_w', 'grad_final_norm': 'grad_w', 'delta_ffn1_norm': 'delta_w', 'delta_ffn1_w_gate': 'delta_w', 'delta_ffn1_w_up': 'delta_w', 'delta_ffn1_w_down': 'delta_w', 'delta_mix_norm': 'delta_w', 'delta_w_in': 'delta_w', 'delta_conv_w': 'delta_w', 'delta_conv_b': 'delta_w', 'delta_ssm_A_re': 'delta_w', 'delta_ssm_A_im': 'delta_w', 'delta_ssm_B_re': 'delta_w', 'delta_ssm_B_im': 'delta_w', 'delta_ssm_C_re': 'delta_w', 'delta_ssm_C_im': 'delta_w', 'delta_ssm_D': 'delta_w', 'delta_ssm_log_dt': 'delta_w', 'delta_glu_w': 'delta_w', 'delta_glu_b': 'delta_w', 'delta_conv_out_norm': 'delta_w', 'delta_ssm_out_norm': 'delta_w', 'delta_w_out': 'delta_w', 'delta_ffn2_norm': 'delta_w', 'delta_ffn2_w_gate': 'delta_w', 'delta_ffn2_w_up': 'delta_w', 'delta_ffn2_w_down': 'delta_w', 'delta_ple_norm': 'delta_w', 'delta_ple_w_gate': 'delta_w', 'delta_ple_w_proj': 'delta_w', 'delta_final_norm': 'delta_w', 'new_m_ffn1_norm': 'new_m', 'new_m_ffn1_w_gate': 'new_m', 'new_m_ffn1_w_up': 'new_m', 'new_m_ffn1_w_down': 'new_m', 'new_m_mix_norm': 'new_m', 'new_m_w_in': 'new_m', 'new_m_conv_w': 'new_m', 'new_m_conv_b': 'new_m', 'new_m_ssm_A_re': 'new_m', 'new_m_ssm_A_im': 'new_m', 'new_m_ssm_B_re': 'new_m', 'new_m_ssm_B_im': 'new_m', 'new_m_ssm_C_re': 'new_m', 'new_m_ssm_C_im': 'new_m', 'new_m_ssm_D': 'new_m', 'new_m_ssm_log_dt': 'new_m', 'new_m_glu_w': 'new_m', 'new_m_glu_b': 'new_m', 'new_m_conv_out_norm': 'new_m', 'new_m_ssm_out_norm': 'new_m', 'new_m_w_out': 'new_m', 'new_m_ffn2_norm': 'new_m', 'new_m_ffn2_w_gate': 'new_m', 'new_m_ffn2_w_up': 'new_m', 'new_m_ffn2_w_down': 'new_m', 'new_m_ple_norm': 'new_m', 'new_m_ple_w_gate': 'new_m', 'new_m_ple_w_proj': 'new_m', 'new_m_final_norm': 'new_m', 'new_v_ffn1_norm': 'new_v', 'new_v_ffn1_w_gate': 'new_v', 'new_v_ffn1_w_up': 'new_v', 'new_v_ffn1_w_down': 'new_v', 'new_v_mix_norm': 'new_v', 'new_v_w_in': 'new_v', 'new_v_conv_w': 'new_v', 'new_v_conv_b': 'new_v', 'new_v_ssm_A_re': 'new_v', 'new_v_ssm_A_im': 'new_v', 'new_v_ssm_B_re': 'new_v', 'new_v_ssm_B_im': 'new_v', 'new_v_ssm_C_re': 'new_v', 'new_v_ssm_C_im': 'new_v', 'new_v_ssm_D': 'new_v', 'new_v_ssm_log_dt': 'new_v', 'new_v_glu_w': 'new_v', 'new_v_glu_b': 'new_v', 'new_v_conv_out_norm': 'new_v', 'new_v_ssm_out_norm': 'new_v', 'new_v_w_out': 'new_v', 'new_v_ffn2_norm': 'new_v', 'new_v_ffn2_w_gate': 'new_v', 'new_v_ffn2_w_up': 'new_v', 'new_v_ffn2_w_down': 'new_v', 'new_v_ple_norm': 'new_v', 'new_v_ple_w_gate': 'new_v', 'new_v_ple_w_proj': 'new_v', 'new_v_final_norm': 'new_v'}


def _forward(args):
    return _fwd_reference(*[args[k] for k in FWD_PARAMS])


def _output_shape():
    out = _jax.eval_shape(lambda: _forward(_fwd_setup_inputs(0)))
    return out.shape, out.dtype

N_MICROBATCH = 1
ADAM_LR = 0.001
ADAM_B1 = 0.9
ADAM_B2 = 0.999
ADAM_EPS = 1e-08
ADAM_WD = 0.01
ADAM_STEP = 10
PER_EXAMPLE_BATCH_AXIS = {'x': 0, 'p': 1, 'loss_target': 0}
SHARED_INPUTS = []
_WEIGHT_DTYPES = {'ffn1_norm': _jnp.float32, 'ffn1_w_gate': _jnp.float32, 'ffn1_w_up': _jnp.float32, 'ffn1_w_down': _jnp.float32, 'mix_norm': _jnp.float32, 'w_in': _jnp.float32, 'conv_w': _jnp.float32, 'conv_b': _jnp.float32, 'ssm_A_re': _jnp.float32, 'ssm_A_im': _jnp.float32, 'ssm_B_re': _jnp.float32, 'ssm_B_im': _jnp.float32, 'ssm_C_re': _jnp.float32, 'ssm_C_im': _jnp.float32, 'ssm_D': _jnp.float32, 'ssm_log_dt': _jnp.float32, 'glu_w': _jnp.float32, 'glu_b': _jnp.float32, 'conv_out_norm': _jnp.float32, 'ssm_out_norm': _jnp.float32, 'w_out': _jnp.float32, 'ffn2_norm': _jnp.float32, 'ffn2_w_gate': _jnp.float32, 'ffn2_w_up': _jnp.float32, 'ffn2_w_down': _jnp.float32, 'ple_norm': _jnp.float32, 'ple_w_gate': _jnp.float32, 'ple_w_proj': _jnp.float32, 'final_norm': _jnp.float32}
MOMENT_SCALE = {'ffn1_norm': 5.341553e-02, 'ffn1_w_gate': 2.233660e-02, 'ffn1_w_up': 2.164371e-02, 'ffn1_w_down': 3.594695e-02, 'mix_norm': 1.156391e-01, 'w_in': 8.138173e-02, 'conv_w': 8.292888e-02, 'conv_b': 8.010270e-02, 'ssm_A_re': 4.132154e-03, 'ssm_A_im': 4.977589e-03, 'ssm_B_re': 3.037880e-03, 'ssm_B_im': 2.979699e-03, 'ssm_C_re': 6.009949e-03, 'ssm_C_im': 5.796994e-03, 'ssm_D': 9.036333e-02, 'ssm_log_dt': 3.240608e+00, 'glu_w': 2.254702e-02, 'glu_b': 3.461528e-02, 'conv_out_norm': 8.175854e-02, 'ssm_out_norm': 7.929542e-02, 'w_out': 8.047326e-02, 'ffn2_norm': 3.319895e-02, 'ffn2_w_gate': 1.420873e-02, 'ffn2_w_up': 1.380138e-02, 'ffn2_w_down': 2.281794e-02, 'ple_norm': 1.599623e-02, 'ple_w_gate': 1.614133e-02, 'ple_w_proj': 4.123174e-02, 'final_norm': 1.608359e+01}


def _to_microbatches(a, axis):
    t = _jnp.moveaxis(a, axis, 0)
    t = t.reshape((N_MICROBATCH, t.shape[0] // N_MICROBATCH) + t.shape[1:])
    return _jnp.moveaxis(t, 1, axis + 1)


def setup_inputs(seed: int = 0) -> dict:
    inp = _fwd_setup_inputs(seed)
    key = _jax.random.fold_in(_jax.random.key(seed), 7919)
    shape, _ = _output_shape()
    out = dict(inp)
    out["loss_target"] = _jax.random.normal(_jax.random.fold_in(key, 0), shape, _jnp.float32)
    for i, name in enumerate(TWIN_WEIGHTS):
        w = inp[name].astype(_jnp.float32)
        if MOMENT_SCALE is None:
            s = _jnp.sqrt(_jnp.mean(_jnp.square(w)) + 1e-30)
        else:
            s = MOMENT_SCALE[name]
        km, kv = _jax.random.split(_jax.random.fold_in(key, i + 1))
        out[name] = w
        out["m_" + name] = s * _jax.random.normal(km, w.shape, _jnp.float32)
        out["v_" + name] = (s * s) * _jax.random.uniform(kv, w.shape, _jnp.float32, 0.5, 1.5)
    if N_MICROBATCH > 1:
        for name, axis in PER_EXAMPLE_BATCH_AXIS.items():
            out[name] = _to_microbatches(out[name], axis)
    return {'x': out['x'], 'p': out['p'], 'ffn1_norm': out['ffn1_norm'], 'ffn1_w_gate': out['ffn1_w_gate'], 'ffn1_w_up': out['ffn1_w_up'], 'ffn1_w_down': out['ffn1_w_down'], 'mix_norm': out['mix_norm'], 'w_in': out['w_in'], 'conv_w': out['conv_w'], 'conv_b': out['conv_b'], 'ssm_A_re': out['ssm_A_re'], 'ssm_A_im': out['ssm_A_im'], 'ssm_B_re': out['ssm_B_re'], 'ssm_B_im': out['ssm_B_im'], 'ssm_C_re': out['ssm_C_re'], 'ssm_C_im': out['ssm_C_im'], 'ssm_D': out['ssm_D'], 'ssm_log_dt': out['ssm_log_dt'], 'glu_w': out['glu_w'], 'glu_b': out['glu_b'], 'conv_out_norm': out['conv_out_norm'], 'ssm_out_norm': out['ssm_out_norm'], 'w_out': out['w_out'], 'ffn2_norm': out['ffn2_norm'], 'ffn2_w_gate': out['ffn2_w_gate'], 'ffn2_w_up': out['ffn2_w_up'], 'ffn2_w_down': out['ffn2_w_down'], 'ple_norm': out['ple_norm'], 'ple_w_gate': out['ple_w_gate'], 'ple_w_proj': out['ple_w_proj'], 'final_norm': out['final_norm'], 'loss_target': out['loss_target'], 'm_ffn1_norm': out['m_ffn1_norm'], 'm_ffn1_w_gate': out['m_ffn1_w_gate'], 'm_ffn1_w_up': out['m_ffn1_w_up'], 'm_ffn1_w_down': out['m_ffn1_w_down'], 'm_mix_norm': out['m_mix_norm'], 'm_w_in': out['m_w_in'], 'm_conv_w': out['m_conv_w'], 'm_conv_b': out['m_conv_b'], 'm_ssm_A_re': out['m_ssm_A_re'], 'm_ssm_A_im': out['m_ssm_A_im'], 'm_ssm_B_re': out['m_ssm_B_re'], 'm_ssm_B_im': out['m_ssm_B_im'], 'm_ssm_C_re': out['m_ssm_C_re'], 'm_ssm_C_im': out['m_ssm_C_im'], 'm_ssm_D': out['m_ssm_D'], 'm_ssm_log_dt': out['m_ssm_log_dt'], 'm_glu_w': out['m_glu_w'], 'm_glu_b': out['m_glu_b'], 'm_conv_out_norm': out['m_conv_out_norm'], 'm_ssm_out_norm': out['m_ssm_out_norm'], 'm_w_out': out['m_w_out'], 'm_ffn2_norm': out['m_ffn2_norm'], 'm_ffn2_w_gate': out['m_ffn2_w_gate'], 'm_ffn2_w_up': out['m_ffn2_w_up'], 'm_ffn2_w_down': out['m_ffn2_w_down'], 'm_ple_norm': out['m_ple_norm'], 'm_ple_w_gate': out['m_ple_w_gate'], 'm_ple_w_proj': out['m_ple_w_proj'], 'm_final_norm': out['m_final_norm'], 'v_ffn1_norm': out['v_ffn1_norm'], 'v_ffn1_w_gate': out['v_ffn1_w_gate'], 'v_ffn1_w_up': out['v_ffn1_w_up'], 'v_ffn1_w_down': out['v_ffn1_w_down'], 'v_mix_norm': out['v_mix_norm'], 'v_w_in': out['v_w_in'], 'v_conv_w': out['v_conv_w'], 'v_conv_b': out['v_conv_b'], 'v_ssm_A_re': out['v_ssm_A_re'], 'v_ssm_A_im': out['v_ssm_A_im'], 'v_ssm_B_re': out['v_ssm_B_re'], 'v_ssm_B_im': out['v_ssm_B_im'], 'v_ssm_C_re': out['v_ssm_C_re'], 'v_ssm_C_im': out['v_ssm_C_im'], 'v_ssm_D': out['v_ssm_D'], 'v_ssm_log_dt': out['v_ssm_log_dt'], 'v_glu_w': out['v_glu_w'], 'v_glu_b': out['v_glu_b'], 'v_conv_out_norm': out['v_conv_out_norm'], 'v_ssm_out_norm': out['v_ssm_out_norm'], 'v_w_out': out['v_w_out'], 'v_ffn2_norm': out['v_ffn2_norm'], 'v_ffn2_w_gate': out['v_ffn2_w_gate'], 'v_ffn2_w_up': out['v_ffn2_w_up'], 'v_ffn2_w_down': out['v_ffn2_w_down'], 'v_ple_norm': out['v_ple_norm'], 'v_ple_w_gate': out['v_ple_w_gate'], 'v_ple_w_proj': out['v_ple_w_proj'], 'v_final_norm': out['v_final_norm']}


def _loss(weights, diff, rest, loss_target):
    with _jax.named_scope("forward"):
        args = {**rest, TWIN_DIFF_INPUT: diff, **{k: w.astype(_WEIGHT_DTYPES[k]) for k, w in weights.items()}}
        y = _forward(args)
    with _jax.named_scope("loss_head"):
        err = _jnp.square(y.astype(_jnp.float32) - loss_target)
        return 0.5 * _jnp.sum(_jnp.mean(err, axis=-1)) if err.ndim else 0.5 * err


def _adamw(w, g, m, v):
    m = ADAM_B1 * m + (1.0 - ADAM_B1) * g
    v = ADAM_B2 * v + (1.0 - ADAM_B2) * _jnp.square(g)
    m_hat = m / (1.0 - ADAM_B1 ** ADAM_STEP)
    v_hat = v / (1.0 - ADAM_B2 ** ADAM_STEP)
    delta = -ADAM_LR * (m_hat / (_jnp.sqrt(v_hat) + ADAM_EPS) + ADAM_WD * w)
    return delta, m, v


def reference(x, p, ffn1_norm, ffn1_w_gate, ffn1_w_up, ffn1_w_down, mix_norm, w_in, conv_w, conv_b, ssm_A_re, ssm_A_im, ssm_B_re, ssm_B_im, ssm_C_re, ssm_C_im, ssm_D, ssm_log_dt, glu_w, glu_b, conv_out_norm, ssm_out_norm, w_out, ffn2_norm, ffn2_w_gate, ffn2_w_up, ffn2_w_down, ple_norm, ple_w_gate, ple_w_proj, final_norm, loss_target, m_ffn1_norm, m_ffn1_w_gate, m_ffn1_w_up, m_ffn1_w_down, m_mix_norm, m_w_in, m_conv_w, m_conv_b, m_ssm_A_re, m_ssm_A_im, m_ssm_B_re, m_ssm_B_im, m_ssm_C_re, m_ssm_C_im, m_ssm_D, m_ssm_log_dt, m_glu_w, m_glu_b, m_conv_out_norm, m_ssm_out_norm, m_w_out, m_ffn2_norm, m_ffn2_w_gate, m_ffn2_w_up, m_ffn2_w_down, m_ple_norm, m_ple_w_gate, m_ple_w_proj, m_final_norm, v_ffn1_norm, v_ffn1_w_gate, v_ffn1_w_up, v_ffn1_w_down, v_mix_norm, v_w_in, v_conv_w, v_conv_b, v_ssm_A_re, v_ssm_A_im, v_ssm_B_re, v_ssm_B_im, v_ssm_C_re, v_ssm_C_im, v_ssm_D, v_ssm_log_dt, v_glu_w, v_glu_b, v_conv_out_norm, v_ssm_out_norm, v_w_out, v_ffn2_norm, v_ffn2_w_gate, v_ffn2_w_up, v_ffn2_w_down, v_ple_norm, v_ple_w_gate, v_ple_w_proj, v_final_norm):
    given = dict(x=x, p=p, ffn1_norm=ffn1_norm, ffn1_w_gate=ffn1_w_gate, ffn1_w_up=ffn1_w_up, ffn1_w_down=ffn1_w_down, mix_norm=mix_norm, w_in=w_in, conv_w=conv_w, conv_b=conv_b, ssm_A_re=ssm_A_re, ssm_A_im=ssm_A_im, ssm_B_re=ssm_B_re, ssm_B_im=ssm_B_im, ssm_C_re=ssm_C_re, ssm_C_im=ssm_C_im, ssm_D=ssm_D, ssm_log_dt=ssm_log_dt, glu_w=glu_w, glu_b=glu_b, conv_out_norm=conv_out_norm, ssm_out_norm=ssm_out_norm, w_out=w_out, ffn2_norm=ffn2_norm, ffn2_w_gate=ffn2_w_gate, ffn2_w_up=ffn2_w_up, ffn2_w_down=ffn2_w_down, ple_norm=ple_norm, ple_w_gate=ple_w_gate, ple_w_proj=ple_w_proj, final_norm=final_norm, loss_target=loss_target, m_ffn1_norm=m_ffn1_norm, m_ffn1_w_gate=m_ffn1_w_gate, m_ffn1_w_up=m_ffn1_w_up, m_ffn1_w_down=m_ffn1_w_down, m_mix_norm=m_mix_norm, m_w_in=m_w_in, m_conv_w=m_conv_w, m_conv_b=m_conv_b, m_ssm_A_re=m_ssm_A_re, m_ssm_A_im=m_ssm_A_im, m_ssm_B_re=m_ssm_B_re, m_ssm_B_im=m_ssm_B_im, m_ssm_C_re=m_ssm_C_re, m_ssm_C_im=m_ssm_C_im, m_ssm_D=m_ssm_D, m_ssm_log_dt=m_ssm_log_dt, m_glu_w=m_glu_w, m_glu_b=m_glu_b, m_conv_out_norm=m_conv_out_norm, m_ssm_out_norm=m_ssm_out_norm, m_w_out=m_w_out, m_ffn2_norm=m_ffn2_norm, m_ffn2_w_gate=m_ffn2_w_gate, m_ffn2_w_up=m_ffn2_w_up, m_ffn2_w_down=m_ffn2_w_down, m_ple_norm=m_ple_norm, m_ple_w_gate=m_ple_w_gate, m_ple_w_proj=m_ple_w_proj, m_final_norm=m_final_norm, v_ffn1_norm=v_ffn1_norm, v_ffn1_w_gate=v_ffn1_w_gate, v_ffn1_w_up=v_ffn1_w_up, v_ffn1_w_down=v_ffn1_w_down, v_mix_norm=v_mix_norm, v_w_in=v_w_in, v_conv_w=v_conv_w, v_conv_b=v_conv_b, v_ssm_A_re=v_ssm_A_re, v_ssm_A_im=v_ssm_A_im, v_ssm_B_re=v_ssm_B_re, v_ssm_B_im=v_ssm_B_im, v_ssm_C_re=v_ssm_C_re, v_ssm_C_im=v_ssm_C_im, v_ssm_D=v_ssm_D, v_ssm_log_dt=v_ssm_log_dt, v_glu_w=v_glu_w, v_glu_b=v_glu_b, v_conv_out_norm=v_conv_out_norm, v_ssm_out_norm=v_ssm_out_norm, v_w_out=v_w_out, v_ffn2_norm=v_ffn2_norm, v_ffn2_w_gate=v_ffn2_w_gate, v_ffn2_w_up=v_ffn2_w_up, v_ffn2_w_down=v_ffn2_w_down, v_ple_norm=v_ple_norm, v_ple_w_gate=v_ple_w_gate, v_ple_w_proj=v_ple_w_proj, v_final_norm=v_final_norm)
    weights = {n: given[n] for n in TWIN_WEIGHTS}
    shared = {n: given[n] for n in SHARED_INPUTS}
    per_example = {n: given[n] for n in ['x', 'p']}
    grad_fn = _jax.value_and_grad(_loss, argnums=(0, 1))

    def one_microbatch(ex, loss_target):
        ex = dict(ex)
        diff = ex.pop(TWIN_DIFF_INPUT)
        return grad_fn(weights, diff, {**shared, **ex}, loss_target)

    if N_MICROBATCH == 1:
        loss, (grad_w, grad_x) = one_microbatch(per_example, given["loss_target"])
    else:
        def body(carry, xs):
            loss_sum, grad_sum = carry
            l_k, (gw_k, gx_k) = one_microbatch(xs[0], xs[1])
            with _jax.named_scope("update"):
                return (loss_sum + l_k, _jax.tree.map(_jnp.add, grad_sum, gw_k)), gx_k

        init = (_jnp.zeros((), _jnp.float32), _jax.tree.map(_jnp.zeros_like, weights))
        (loss, grad_w), grad_x = _jax.lax.scan(body, init, (per_example, given["loss_target"]))
    with _jax.named_scope("update"):
        delta_w, new_m, new_v = {}, {}, {}
        for n in TWIN_WEIGHTS:
            delta_w[n], new_m[n], new_v[n] = _adamw(weights[n], grad_w[n], given["m_" + n], given["v_" + n])
    return (loss, grad_x, *[grad_w[n] for n in TWIN_WEIGHTS], *[delta_w[n] for n in TWIN_WEIGHTS],
            *[new_m[n] for n in TWIN_WEIGHTS], *[new_v[n] for n in TWIN_WEIGHTS])
```

```python
import functools
import math

import jax
import jax.numpy as jnp
from jax import lax
from jax.experimental import pallas as pl
from jax.experimental.pallas import tpu as pltpu

F32, BF16 = jnp.float32, jnp.bfloat16
S = jax.ShapeDtypeStruct
EPS = 1e-6
N_SEG = 8
N_SHARD = 4
N_DEV = 8
VMEM_LIMIT_BYTES = 56 * 1024 * 1024
ADAM_LR, ADAM_B1, ADAM_B2, ADAM_EPS, ADAM_WD, ADAM_STEP = 0.001, 0.9, 0.999, 1e-08, 0.01, 10
MESH = pl.DeviceIdType.MESH


def _params(*sem):
    return pltpu.CompilerParams(dimension_semantics=sem if sem else None, vmem_limit_bytes=VMEM_LIMIT_BYTES)


def _dot(a, b, ca, cb):
    return lax.dot_general(a, b, (((ca,), (cb,)), ((), ())), preferred_element_type=F32)


def _sigmoid(x):
    return 1.0 / (1.0 + jnp.exp(-x))


def _rstd(x):
    return lax.rsqrt(jnp.mean(x * x, axis=-1, keepdims=True) + EPS)


def _rms_bwd(x, g, dy):
    r = _rstd(x)
    xh = x * r
    dxh = dy * g
    dx = r * (dxh - xh * jnp.mean(dxh * xh, axis=-1, keepdims=True))
    return dx, jnp.sum(dy * xh, axis=0, keepdims=True)


def _tile(n, want):
    return want if n % want == 0 else n


def rmsnorm_fwd(h, g):
    L, D = h.shape
    tm = _tile(L, 512)

    def body(h_ref, g_ref, o_ref):
        x = h_ref[...]
        o_ref[...] = (x * _rstd(x) * g_ref[...]).astype(BF16)

    return pl.pallas_call(
        body, name="rmsnorm_fwd", grid=(L // tm,),
        in_specs=[pl.BlockSpec((tm, D), lambda m: (m, 0)), pl.BlockSpec((1, D), lambda m: (0, 0))],
        out_specs=pl.BlockSpec((tm, D), lambda m: (m, 0)),
        out_shape=S((L, D), BF16), compiler_params=_params("parallel"),
    )(h, g.reshape(1, D))


def ffn_up(u, wg, wu):
    L, D = u.shape
    ns, _, F = wg.shape
    tm = _tile(L, 512)

    def body(u_ref, wg_ref, wu_ref, a_ref, b_ref, s_ref):
        x = u_ref[...]
        a = _dot(x, wg_ref[0], 1, 0)
        b = _dot(x, wu_ref[0], 1, 0)
        a_ref[0] = a.astype(BF16)
        b_ref[0] = b.astype(BF16)
        s_ref[0] = (a * _sigmoid(a) * b).astype(BF16)

    w_spec = pl.BlockSpec((1, D, F), lambda s, m: (s, 0, 0))
    o_spec = pl.BlockSpec((1, tm, F), lambda s, m: (s, m, 0))
    return pl.pallas_call(
        body, name="ffn_up", grid=(ns, L // tm),
        in_specs=[pl.BlockSpec((tm, D), lambda s, m: (m, 0)), w_spec, w_spec],
        out_specs=[o_spec, o_spec, o_spec],
        out_shape=[S((ns, L, F), BF16)] * 3, compiler_params=_params("parallel", "parallel"),
    )(u, wg, wu)


def mm_shard_n(u, w3, out_dtype):
    L, K = u.shape
    ns, _, N = w3.shape
    tm = _tile(L, 512)

    def body(u_ref, w_ref, o_ref):
        o_ref[0] = _dot(u_ref[...], w_ref[0], 1, 0).astype(out_dtype)

    return pl.pallas_call(
        body, name="mm_shard_n", grid=(ns, L // tm),
        in_specs=[pl.BlockSpec((tm, K), lambda s, m: (m, 0)), pl.BlockSpec((1, K, N), lambda s, m: (s, 0, 0))],
        out_specs=pl.BlockSpec((1, tm, N), lambda s, m: (s, m, 0)),
        out_shape=S((ns, L, N), out_dtype), compiler_params=_params("parallel", "parallel"),
    )(u, w3)


def mm_shard_k(a3, w3, res, scale):
    nk, L, Kc = a3.shape
    N = w3.shape[2]
    tm = _tile(L, 512)

    def body(a_ref, w_ref, r_ref, o_ref, acc):
        k = pl.program_id(1)

        @pl.when(k == 0)
        def _():
            acc[...] = jnp.zeros_like(acc)

        acc[...] += _dot(a_ref[0], w_ref[0], 1, 0)

        @pl.when(k == nk - 1)
        def _():
            o_ref[...] = r_ref[...] + scale * acc[...]

    return pl.pallas_call(
        body, name="mm_shard_k", grid=(L // tm, nk),
        in_specs=[pl.BlockSpec((1, tm, Kc), lambda m, k: (k, m, 0)), pl.BlockSpec((1, Kc, N), lambda m, k: (k, 0, 0)),
                  pl.BlockSpec((tm, N), lambda m, k: (m, 0))],
        out_specs=pl.BlockSpec((tm, N), lambda m, k: (m, 0)),
        out_shape=S((L, N), F32), scratch_shapes=[pltpu.VMEM((tm, N), F32)],
        compiler_params=_params("parallel", "arbitrary"),
    )(a3, w3, res)


CONV_HALO = 8


def _conv_specs(L, tm, C, shard):
    nb = L // CONV_HALO
    per = tm // CONV_HALO
    main = pl.BlockSpec((1, tm, C), lambda m: (shard, m, 0))
    prev = pl.BlockSpec((1, CONV_HALO, C), lambda m: (shard, jnp.maximum(m * per - 1, 0), 0))
    nxt = pl.BlockSpec((1, CONV_HALO, C), lambda m: (shard, jnp.minimum((m + 1) * per, nb - 1), 0))
    return main, prev, nxt


def _conv_core(zb, zc, zv, w_ref, bias, grow, L):
    valid = (grow >= 0) & (grow < L)
    v = jnp.where(valid, zc * zv, 0.0)
    v1 = pltpu.roll(v, 1, 0)
    v2 = pltpu.roll(v, 2, 0)
    cb = w_ref[0:1, :] * v2 + w_ref[1:2, :] * v1 + w_ref[2:3, :] * v + bias
    return valid, v, v1, v2, cb, zb * cb


def conv_fwd(z, conv_w, conv_b, gnorm):
    _, L, C = z.shape
    tm = _tile(L, 256)
    H = CONV_HALO

    def body(zb_ref, zc_ref, zcp_ref, zv_ref, zvp_ref, w_ref, b_ref, g_ref, o_ref):
        m = pl.program_id(0)
        zc = jnp.concatenate([zcp_ref[0], zc_ref[0]], axis=0)
        zv = jnp.concatenate([zvp_ref[0], zv_ref[0]], axis=0)
        grow = m * tm - H + lax.broadcasted_iota(jnp.int32, (tm + H, C), 0)
        valid = grow >= 0
        v = jnp.where(valid, zc * zv, 0.0)
        v1 = pltpu.roll(v, 1, 0)
        v2 = pltpu.roll(v, 2, 0)
        cb = (w_ref[0:1, :] * v2 + w_ref[1:2, :] * v1 + w_ref[2:3, :] * v + b_ref[...])[H:, :]
        ya = zb_ref[0] * cb
        o_ref[...] = (ya * _rstd(ya) * g_ref[...]).astype(BF16)

    zb_m, _, _ = _conv_specs(L, tm, C, 0)
    zc_m, zc_p, _ = _conv_specs(L, tm, C, 1)
    zv_m, zv_p, _ = _conv_specs(L, tm, C, 2)
    row = lambda r: pl.BlockSpec((r, C), lambda m: (0, 0))
    return pl.pallas_call(
        body, name="conv_fwd", grid=(L // tm,),
        in_specs=[zb_m, zc_m, zc_p, zv_m, zv_p, row(3), row(1), row(1)],
        out_specs=pl.BlockSpec((tm, C), lambda m: (m, 0)),
        out_shape=S((L, C), BF16), compiler_params=_params("parallel"),
    )(z, z, z, z, z, conv_w, conv_b.reshape(1, C), gnorm.reshape(1, C))


def _cmul(ar, ai, br, bi):
    return ar * br - ai * bi, ar * bi + ai * br


def _scan_fwd(hr_ref, hi_ref, lr, li, n_steps):
    W = hr_ref.shape[1]
    zero = jnp.zeros((N_SEG, W), F32)

    def local(t, c):
        r = pl.multiple_of(t * N_SEG, N_SEG)
        pr, pi = _cmul(lr, li, c[0], c[1])
        nr = pr + hr_ref[pl.ds(r, N_SEG), :]
        ni = pi + hi_ref[pl.ds(r, N_SEG), :]
        hr_ref[pl.ds(r, N_SEG), :] = nr
        hi_ref[pl.ds(r, N_SEG), :] = ni
        return nr, ni

    fr, fi = lax.fori_loop(0, n_steps, local, (zero, zero))
    qr, qi = _cpow(lr, li, n_steps)
    row = lax.broadcasted_iota(jnp.int32, (N_SEG, W), 0)
    cr, ci = zero, zero
    for seg in range(1, N_SEG):
        tr, ti = _cmul(qr, qi, cr, ci)
        sr = pltpu.roll(fr + tr, 1, 0)
        si = pltpu.roll(fi + ti, 1, 0)
        cr = jnp.where(row == seg, sr, cr)
        ci = jnp.where(row == seg, si, ci)

    def fix(t, c):
        r = pl.multiple_of(t * N_SEG, N_SEG)
        pr, pi = _cmul(lr, li, c[0], c[1])
        ar, ai = _cmul(pr, pi, cr, ci)
        hr_ref[pl.ds(r, N_SEG), :] += ar
        hi_ref[pl.ds(r, N_SEG), :] += ai
        return pr, pi

    lax.fori_loop(0, n_steps, fix, (jnp.ones((N_SEG, W), F32), zero))


def _cpow(lr, li, n):
    rr, ri = None, None
    br, bi = lr, li
    while n:
        if n & 1:
            rr, ri = (br, bi) if rr is None else _cmul(rr, ri, br, bi)
        n >>= 1
        if n:
            br, bi = _cmul(br, bi, br, bi)
    return rr, ri


def _ssm_specs(L):
    col = lambda w: pl.BlockSpec((L, w), lambda j: (0, j))
    return dict(
        u=col(128), lam=pl.BlockSpec((2, 512), lambda j: (0, j)),
        bmat=pl.BlockSpec((1, 128, 512), lambda j: (j, 0, 0)), cmat=pl.BlockSpec((1, 512, 128), lambda j: (j, 0, 0)),
        d=pl.BlockSpec((1, 128), lambda j: (0, j)))


def ssm_fwd(us, lam, bre, bim, cre, cim, dvec):
    L = us.shape[0]
    n_steps = L // N_SEG
    sp = _ssm_specs(L)

    def body(u_ref, lam_ref, bre_ref, bim_ref, cre_ref, cim_ref, d_ref, y_ref, hr, hi):
        u = u_ref[...]
        ub = u.astype(BF16)
        hr[...] = _dot(ub, bre_ref[0], 1, 0)
        hi[...] = _dot(ub, bim_ref[0], 1, 0)
        lr = jnp.broadcast_to(lam_ref[0:1, :], (N_SEG, 512))
        li = jnp.broadcast_to(lam_ref[1:2, :], (N_SEG, 512))
        _scan_fwd(hr, hi, lr, li, n_steps)
        y_ref[...] = (_dot(hr[...].astype(BF16), cre_ref[0], 1, 0) - _dot(hi[...].astype(BF16), cim_ref[0], 1, 0)
                      + d_ref[...] * u)

    return pl.pallas_call(
        body, name="ssm_fwd", grid=(4,),
        in_specs=[sp["u"], sp["lam"], sp["bmat"], sp["bmat"], sp["cmat"], sp["cmat"], sp["d"]],
        out_specs=sp["u"], out_shape=S((L, 512), F32),
        scratch_shapes=[pltpu.VMEM((L, 512), F32), pltpu.VMEM((L, 512), F32)],
        compiler_params=_params("parallel"),
    )(us, lam, bre, bim, cre, cim, dvec)


_GELU_C = math.sqrt(2.0 / math.pi)


def _gelu(y):
    t = jnp.tanh(_GELU_C * (y + 0.044715 * y * y * y))
    return 0.5 * y * (1.0 + t), t


def glu_fwd(y, w, b, gnorm):
    L, C = y.shape
    tm = _tile(L, 512)

    def body(y_ref, w_ref, b_ref, g_ref, o_ref):
        zg, _ = _gelu(y_ref[...])
        out = zg * _sigmoid(_dot(zg.astype(BF16), w_ref[...], 1, 0) + b_ref[...])
        o_ref[...] = (out * _rstd(out) * g_ref[...]).astype(BF16)

    row = pl.BlockSpec((1, C), lambda m: (0, 0))
    return pl.pallas_call(
        body, name="glu_fwd", grid=(L // tm,),
        in_specs=[pl.BlockSpec((tm, C), lambda m: (m, 0)), pl.BlockSpec((C, C), lambda m: (0, 0)), row, row],
        out_specs=pl.BlockSpec((tm, C), lambda m: (m, 0)),
        out_shape=S((L, C), BF16), compiler_params=_params("parallel"),
    )(y, w, b.reshape(1, C), gnorm.reshape(1, C))


def _ple_specs(L, D, P, tm, nb):
    return [pl.BlockSpec((tm, D), lambda n, m: (m, 0)), pl.BlockSpec((tm, P), lambda n, m: (m, 0)),
            pl.BlockSpec((D, nb), lambda n, m: (0, n)), pl.BlockSpec((1, P, nb), lambda n, m: (n, 0, 0)),
            pl.BlockSpec((tm, nb), lambda n, m: (m, n))]


def ple_fwd(un, pb, wpg, wpp, h):
    L, D = un.shape
    ns, P, nb = wpp.shape
    tm = _tile(L, 512)

    def body(un_ref, p_ref, wg_ref, wp_ref, h_ref, o_ref):
        gate = _sigmoid(_dot(un_ref[...], wg_ref[...], 1, 0))
        o_ref[...] = h_ref[...] + _dot(p_ref[...], wp_ref[0], 1, 0) * gate

    return pl.pallas_call(
        body, name="ple_fwd", grid=(ns, L // tm),
        in_specs=_ple_specs(L, D, P, tm, nb),
        out_specs=pl.BlockSpec((tm, nb), lambda n, m: (m, n)),
        out_shape=S((L, D), F32), compiler_params=_params("parallel", "parallel"),
    )(un, pb, wpg, wpp, h)


def loss_head(h, g, target):
    L, D = h.shape
    tm = _tile(L, 256)

    def body(h_ref, g_ref, t_ref, loss_ref, dh_ref, dg_ref):
        m = pl.program_id(0)
        x = h_ref[...]
        gg = g_ref[...]
        e = x * _rstd(x) * gg - t_ref[...]
        dx, dg = _rms_bwd(x, gg, e * (1.0 / D))
        dh_ref[...] = dx
        part = jnp.full((8, 128), 0.5 / D, F32) * jnp.sum(e * e)

        @pl.when(m == 0)
        def _():
            loss_ref[...] = part
            dg_ref[...] = dg

        @pl.when(m > 0)
        def _():
            loss_ref[...] += part
            dg_ref[...] += dg

    return pl.pallas_call(
        body, name="loss_head", grid=(L // tm,),
        in_specs=[pl.BlockSpec((tm, D), lambda m: (m, 0)), pl.BlockSpec((1, D), lambda m: (0, 0)),
                  pl.BlockSpec((tm, D), lambda m: (m, 0))],
        out_specs=[pl.BlockSpec((8, 128), lambda m: (0, 0)), pl.BlockSpec((tm, D), lambda m: (m, 0)),
                   pl.BlockSpec((1, D), lambda m: (0, 0))],
        out_shape=[S((8, 128), F32), S((L, D), F32), S((1, D), F32)],
        compiler_params=_params("arbitrary"),
    )(h, g.reshape(1, D), target)


def scale_cast(x, scale):
    L, D = x.shape
    tm = _tile(L, 512)

    def body(x_ref, o_ref):
        o_ref[...] = (scale * x_ref[...]).astype(BF16)

    return pl.pallas_call(
        body, name="scale_cast", grid=(L // tm,),
        in_specs=[pl.BlockSpec((tm, D), lambda m: (m, 0))], out_specs=pl.BlockSpec((tm, D), lambda m: (m, 0)),
        out_shape=S((L, D), BF16), compiler_params=_params("parallel"),
    )(x)


def ple_bwd(un, pb, wpg, wpp, dh):
    L, D = un.shape
    ns, P, nb = wpp.shape
    tm = _tile(L, 512)

    def body(un_ref, p_ref, wg_ref, wp_ref, dh_ref, dpre_ref, dpp_ref):
        gate = _sigmoid(_dot(un_ref[...], wg_ref[...], 1, 0))
        pp = _dot(p_ref[...], wp_ref[0], 1, 0)
        d = dh_ref[...]
        dpp_ref[0] = (d * gate).astype(BF16)
        dpre_ref[...] = (d * pp * gate * (1.0 - gate)).astype(BF16)

    return pl.pallas_call(
        body, name="ple_bwd", grid=(ns, L // tm),
        in_specs=_ple_specs(L, D, P, tm, nb),
        out_specs=[pl.BlockSpec((tm, nb), lambda n, m: (m, n)), pl.BlockSpec((1, tm, nb), lambda n, m: (n, m, 0))],
        out_shape=[S((L, D), BF16), S((ns, L, nb), BF16)], compiler_params=_params("parallel", "parallel"),
    )(un, pb, wpg, wpp, dh)


def wgrad(a, b):
    a3 = a if a.ndim == 3 else a[None]
    b3 = b if b.ndim == 3 else b[None]
    ns = max(a3.shape[0], b3.shape[0])
    _, L, Ka = a3.shape
    N = b3.shape[2]
    tk = _tile(L, 512)
    nk = L // tk
    a_map = (lambda s, k: (s, k, 0)) if a3.shape[0] > 1 else (lambda s, k: (0, k, 0))
    b_map = (lambda s, k: (s, k, 0)) if b3.shape[0] > 1 else (lambda s, k: (0, k, 0))

    def body(a_ref, b_ref, o_ref, acc):
        k = pl.program_id(1)

        @pl.when(k == 0)
        def _():
            acc[...] = jnp.zeros_like(acc)

        acc[...] += _dot(a_ref[0], b_ref[0], 0, 0)

        @pl.when(k == nk - 1)
        def _():
            o_ref[0] = acc[...]

    return pl.pallas_call(
        body, name="wgrad", grid=(ns, nk),
        in_specs=[pl.BlockSpec((1, tk, Ka), a_map), pl.BlockSpec((1, tk, N), b_map)],
        out_specs=pl.BlockSpec((1, Ka, N), lambda s, k: (s, 0, 0)),
        out_shape=S((ns, Ka, N), F32), scratch_shapes=[pltpu.VMEM((Ka, N), F32)],
        compiler_params=_params("parallel", "arbitrary"),
    )(a3, b3)


def dx_rms(pairs, h, g, dh_in):
    L, D = h.shape
    nk = pairs[0][0].shape[0]
    n_pairs = len(pairs)
    tm = _tile(L, 256)
    n_m = L // tm

    def body(*refs):
        ins, (h_ref, g_ref, dhi_ref, dho_ref, dg_ref, acc) = refs[:2 * n_pairs], refs[2 * n_pairs:]
        m, k = pl.program_id(0), pl.program_id(1)

        @pl.when(k == 0)
        def _():
            acc[...] = jnp.zeros_like(acc)

        for i in range(n_pairs):
            acc[...] += _dot(ins[2 * i][0], ins[2 * i + 1][0], 1, 1)

        @pl.when(k == nk - 1)
        def _():
            dx, dg = _rms_bwd(h_ref[...], g_ref[...], acc[...])
            dho_ref[...] = dhi_ref[...] + dx

            @pl.when(m == 0)
            def _():
                dg_ref[...] = dg

            @pl.when(m > 0)
            def _():
                dg_ref[...] += dg

    in_specs, args = [], []
    for a3, w3 in pairs:
        Kc = a3.shape[2]
        in_specs += [pl.BlockSpec((1, tm, Kc), lambda m, k: (k, m, 0)), pl.BlockSpec((1, D, Kc), lambda m, k: (k, 0, 0))]
        args += [a3, w3]
    tile = pl.BlockSpec((tm, D), lambda m, k: (m, 0))
    row = pl.BlockSpec((1, D), lambda m, k: (0, 0))
    return pl.pallas_call(
        body, name="dx_rms", grid=(n_m, nk),
        in_specs=in_specs + [tile, row, tile], out_specs=[tile, row],
        out_shape=[S((L, D), F32), S((1, D), F32)], scratch_shapes=[pltpu.VMEM((tm, D), F32)],
        compiler_params=_params("arbitrary", "arbitrary"),
    )(*args, h, g.reshape(1, D), dh_in)


def dact_plain(dhb, w3):
    L, D = dhb.shape
    ns, N, _ = w3.shape
    tm = _tile(L, 512)

    def body(d_ref, w_ref, o_ref):
        o_ref[0] = _dot(d_ref[...], w_ref[0], 1, 1)

    return pl.pallas_call(
        body, name="dact_plain", grid=(ns, L // tm),
        in_specs=[pl.BlockSpec((tm, D), lambda s, m: (m, 0)), pl.BlockSpec((1, N, D), lambda s, m: (s, 0, 0))],
        out_specs=pl.BlockSpec((1, tm, N), lambda s, m: (s, m, 0)),
        out_shape=S((ns, L, N), F32), compiler_params=_params("parallel", "parallel"),
    )(dhb, w3)


def dact_swiglu(dhb, wd, a3, b3):
    L, D = dhb.shape
    ns, F, _ = wd.shape
    tm = _tile(L, 512)

    def body(d_ref, w_ref, a_ref, b_ref, da_ref, db_ref):
        ds = _dot(d_ref[...], w_ref[0], 1, 1)
        a = a_ref[0].astype(F32)
        b = b_ref[0].astype(F32)
        sg = _sigmoid(a)
        da_ref[0] = (ds * b * (sg * (1.0 + a * (1.0 - sg)))).astype(BF16)
        db_ref[0] = (ds * (a * sg)).astype(BF16)

    t_spec = pl.BlockSpec((1, tm, F), lambda s, m: (s, m, 0))
    return pl.pallas_call(
        body, name="dact_swiglu", grid=(ns, L // tm),
        in_specs=[pl.BlockSpec((tm, D), lambda s, m: (m, 0)), pl.BlockSpec((1, F, D), lambda s, m: (s, 0, 0)), t_spec, t_spec],
        out_specs=[t_spec, t_spec], out_shape=[S((ns, L, F), BF16)] * 2,
        compiler_params=_params("parallel", "parallel"),
    )(dhb, wd, a3, b3)


def conv_bwd(z, conv_w, conv_b, gnorm, dyn):
    _, L, C = z.shape
    tm = _tile(L, 256)
    H = CONV_HALO
    T = tm + 2 * H

    def body(zb_ref, zbp_ref, zbn_ref, zc_ref, zcp_ref, zcn_ref, zv_ref, zvp_ref, zvn_ref, d_ref, dp_ref, dn_ref,
             w_ref, b_ref, g_ref, dz_ref, dw_ref, db_ref, dg_ref):
        m = pl.program_id(0)
        cat = lambda p, c, n: jnp.concatenate([p[0], c[0], n[0]], axis=0)
        zb, zc, zv, d = cat(zbp_ref, zb_ref, zbn_ref), cat(zcp_ref, zc_ref, zcn_ref), cat(zvp_ref, zv_ref, zvn_ref), cat(dp_ref, d_ref, dn_ref)
        grow = m * tm - H + lax.broadcasted_iota(jnp.int32, (T, C), 0)
        valid, v, v1, v2, cb, ya = _conv_core(zb, zc, zv, w_ref, b_ref[...], grow, L)
        dya, _ = _rms_bwd(ya, g_ref[...], d)
        dc = jnp.where(valid, dya * zb, 0.0)
        dv = w_ref[2:3, :] * dc + w_ref[1:2, :] * pltpu.roll(dc, T - 1, 0) + w_ref[0:1, :] * pltpu.roll(dc, T - 2, 0)
        dz_ref[0] = (dya * cb)[H:H + tm, :].astype(BF16)
        dz_ref[1] = (dv * zv)[H:H + tm, :].astype(BF16)
        dz_ref[2] = (dv * zc)[H:H + tm, :].astype(BF16)
        rs = lambda x: jnp.sum(x[H:H + tm, :], axis=0, keepdims=True)
        yh = ya * _rstd(ya)
        dw = jnp.concatenate([rs(dc * v2), rs(dc * v1), rs(dc * v)], axis=0)
        dbias, dg = rs(dc), rs(d * yh)

        @pl.when(m == 0)
        def _():
            dw_ref[...] = dw
            db_ref[...] = dbias
            dg_ref[...] = dg

        @pl.when(m > 0)
        def _():
            dw_ref[...] += dw
            db_ref[...] += dbias
            dg_ref[...] += dg

    row = lambda r: pl.BlockSpec((r, C), lambda m: (0, 0))
    specs = [*_conv_specs(L, tm, C, 0), *_conv_specs(L, tm, C, 1), *_conv_specs(L, tm, C, 2), *_conv_specs(L, tm, C, 0)]
    return pl.pallas_call(
        body, name="conv_bwd", grid=(L // tm,),
        in_specs=specs + [row(3), row(1), row(1)],
        out_specs=[pl.BlockSpec((3, tm, C), lambda m: (0, m, 0)), row(3), row(1), row(1)],
        out_shape=[S((3, L, C), BF16), S((3, C), F32), S((1, C), F32), S((1, C), F32)],
        compiler_params=_params("arbitrary"),
    )(z, z, z, z, z, z, z, z, z, dyn, dyn, dyn, conv_w, conv_b.reshape(1, C), gnorm.reshape(1, C))


def glu_bwd(y, w, b, gnorm, dn):
    L, C = y.shape
    tm = _tile(L, 256)

    def body(y_ref, w_ref, b_ref, g_ref, d_ref, dy_ref, dpre_ref, zg_ref, db_ref, dg_ref):
        m = pl.program_id(0)
        yv = y_ref[...]
        zg, t = _gelu(yv)
        zgb = zg.astype(BF16)
        sg = _sigmoid(_dot(zgb, w_ref[...], 1, 0) + b_ref[...])
        out = zg * sg
        dout, dg = _rms_bwd(out, g_ref[...], d_ref[...])
        dpre = dout * zg * sg * (1.0 - sg)
        dpre_b = dpre.astype(BF16)
        dzg = dout * sg + _dot(dpre_b, w_ref[...], 1, 1)
        dt = (1.0 - t * t) * _GELU_C * (1.0 + 3.0 * 0.044715 * yv * yv)
        dy_ref[...] = dzg * (0.5 * (1.0 + t) + 0.5 * yv * dt)
        dpre_ref[...] = dpre_b
        zg_ref[...] = zgb
        dbias = jnp.sum(dpre, axis=0, keepdims=True)

        @pl.when(m == 0)
        def _():
            db_ref[...] = dbias
            dg_ref[...] = dg

        @pl.when(m > 0)
        def _():
            db_ref[...] += dbias
            dg_ref[...] += dg

    tile = pl.BlockSpec((tm, C), lambda m: (m, 0))
    row = pl.BlockSpec((1, C), lambda m: (0, 0))
    return pl.pallas_call(
        body, name="glu_bwd", grid=(L // tm,),
        in_specs=[tile, pl.BlockSpec((C, C), lambda m: (0, 0)), row, row, tile],
        out_specs=[tile, tile, tile, row, row],
        out_shape=[S((L, C), F32), S((L, C), BF16), S((L, C), BF16), S((1, C), F32), S((1, C), F32)],
        compiler_params=_params("arbitrary"),
    )(y, w, b.reshape(1, C), gnorm.reshape(1, C), dn)


def _scan_bwd(gr_ref, gi_ref, hr_ref, hi_ref, lr, li, n_steps):
    W = gr_ref.shape[1]
    zero = jnp.zeros((N_SEG, W), F32)
    lic = -li

    def local(i, c):
        r = pl.multiple_of((n_steps - 1 - i) * N_SEG, N_SEG)
        pr, pi = _cmul(lr, lic, c[0], c[1])
        nr = pr + gr_ref[pl.ds(r, N_SEG), :]
        ni = pi + gi_ref[pl.ds(r, N_SEG), :]
        gr_ref[pl.ds(r, N_SEG), :] = nr
        gi_ref[pl.ds(r, N_SEG), :] = ni
        return nr, ni

    fr, fi = lax.fori_loop(0, n_steps, local, (zero, zero))
    qr, qi = _cpow(lr, lic, n_steps)
    row = lax.broadcasted_iota(jnp.int32, (N_SEG, W), 0)
    cr, ci = zero, zero
    for seg in range(N_SEG - 2, -1, -1):
        tr, ti = _cmul(qr, qi, cr, ci)
        sr = pltpu.roll(fr + tr, N_SEG - 1, 0)
        si = pltpu.roll(fi + ti, N_SEG - 1, 0)
        cr = jnp.where(row == seg, sr, cr)
        ci = jnp.where(row == seg, si, ci)

    def fix(i, c):
        pwr, pwi, ar, ai = c
        t = n_steps - 1 - i
        r = pl.multiple_of(t * N_SEG, N_SEG)
        pwr, pwi = _cmul(lr, lic, pwr, pwi)
        xr, xi = _cmul(pwr, pwi, cr, ci)
        g_r = gr_ref[pl.ds(r, N_SEG), :] + xr
        g_i = gi_ref[pl.ds(r, N_SEG), :] + xi
        gr_ref[pl.ds(r, N_SEG), :] = g_r
        gi_ref[pl.ds(r, N_SEG), :] = g_i
        rp = pl.multiple_of(jnp.maximum(t - 1, 0) * N_SEG, N_SEG)
        hpr = hr_ref[pl.ds(rp, N_SEG), :]
        hpi = hi_ref[pl.ds(rp, N_SEG), :]
        live = t > 0
        ar = ar + jnp.where(live, hpr * g_r + hpi * g_i, 0.0)
        ai = ai + jnp.where(live, hpr * g_i - hpi * g_r, 0.0)
        return pwr, pwi, ar, ai

    _, _, ar, ai = lax.fori_loop(0, n_steps, fix, (jnp.ones((N_SEG, W), F32), zero, zero, zero))
    last = pl.ds((n_steps - 1) * N_SEG, N_SEG)
    hpr = jnp.where(row == 0, 0.0, pltpu.roll(hr_ref[last, :], 1, 0))
    hpi = jnp.where(row == 0, 0.0, pltpu.roll(hi_ref[last, :], 1, 0))
    g_r, g_i = gr_ref[pl.ds(0, N_SEG), :], gi_ref[pl.ds(0, N_SEG), :]
    ar = ar + hpr * g_r + hpi * g_i
    ai = ai + hpr * g_i - hpi * g_r
    return jnp.sum(ar, axis=0, keepdims=True), jnp.sum(ai, axis=0, keepdims=True)


def ssm_bwd(us, dy, lam, bre, bim, cre, cim, dvec):
    L = us.shape[0]
    n_steps = L // N_SEG
    sp = _ssm_specs(L)

    def body(u_ref, dy_ref, lam_ref, bre_ref, bim_ref, cre_ref, cim_ref, d_ref,
             du_ref, dlam_ref, dbre_ref, dbim_ref, dcre_ref, dcim_ref, dd_ref, hr, hi, gr, gi):
        u = u_ref[...]
        ub = u.astype(BF16)
        dyv = dy_ref[...]
        dyb = dyv.astype(BF16)
        hr[...] = _dot(ub, bre_ref[0], 1, 0)
        hi[...] = _dot(ub, bim_ref[0], 1, 0)
        lr = jnp.broadcast_to(lam_ref[0:1, :], (N_SEG, 512))
        li = jnp.broadcast_to(lam_ref[1:2, :], (N_SEG, 512))
        _scan_fwd(hr, hi, lr, li, n_steps)
        dcre_ref[0] = _dot(hr[...].astype(BF16), dyb, 0, 0)
        dcim_ref[0] = -_dot(hi[...].astype(BF16), dyb, 0, 0)
        gr[...] = _dot(dyb, cre_ref[0], 1, 1)
        gi[...] = -_dot(dyb, cim_ref[0], 1, 1)
        dlr, dli = _scan_bwd(gr, gi, hr, hi, lr, li, n_steps)
        dlam_ref[...] = jnp.concatenate([dlr, dli], axis=0)
        grb, gib = gr[...].astype(BF16), gi[...].astype(BF16)
        du_ref[...] = _dot(grb, bre_ref[0], 1, 1) + _dot(gib, bim_ref[0], 1, 1) + d_ref[...] * dyv
        dbre_ref[0] = _dot(ub, grb, 0, 0)
        dbim_ref[0] = _dot(ub, gib, 0, 0)
        dd_ref[...] = jnp.sum(dyv * u, axis=0, keepdims=True)

    big = pltpu.VMEM((L, 512), F32)
    return pl.pallas_call(
        body, name="ssm_bwd", grid=(4,),
        in_specs=[sp["u"], sp["u"], sp["lam"], sp["bmat"], sp["bmat"], sp["cmat"], sp["cmat"], sp["d"]],
        out_specs=[sp["u"], sp["lam"], sp["bmat"], sp["bmat"], sp["cmat"], sp["cmat"], sp["d"]],
        out_shape=[S((L, 512), F32), S((2, 2048), F32), S((4, 128, 512), F32), S((4, 128, 512), F32),
                   S((4, 512, 128), F32), S((4, 512, 128), F32), S((1, 512), F32)],
        scratch_shapes=[big, big, big, big], compiler_params=_params("parallel"),
    )(us, dy, lam, bre, bim, cre, cim, dvec)


def _discretize(ar, ai, log_dt, br, bi):
    dt = jnp.exp(log_dt)
    mag = jnp.exp(ar * dt)
    ph = ai * dt
    lr, li = mag * jnp.cos(ph), mag * jnp.sin(ph)
    nr, ni = lr - 1.0, li
    den = ar * ar + ai * ai
    fr = (nr * ar + ni * ai) / den
    fi = (ni * ar - nr * ai) / den
    return lr, li, fr[..., None] * br - fi[..., None] * bi, fr[..., None] * bi + fi[..., None] * br


def ssm_prep(ar, ai, log_dt, br, bi):
    G, P, H = br.shape

    def body(ar_ref, ai_ref, dt_ref, br_ref, bi_ref, lr_ref, li_ref, bbr_ref, bbi_ref):
        lr_ref[...], li_ref[...], bbr_ref[...], bbi_ref[...] = _discretize(
            ar_ref[...], ai_ref[...], dt_ref[...], br_ref[...], bi_ref[...])

    return pl.pallas_call(
        body, name="ssm_prep",
        out_shape=[S((G, P), F32), S((G, P), F32), S((G, P, H), F32), S((G, P, H), F32)],
    )(ar, ai, log_dt.reshape(G, 1), br, bi)


def ssm_prep_bwd(ar, ai, log_dt, br, bi, dlr, dli, dbbr, dbbi):
    G, P, H = br.shape

    def body(ar_ref, ai_ref, dt_ref, br_ref, bi_ref, dlr_ref, dli_ref, dbbr_ref, dbbi_ref,
             dar_ref, dai_ref, ddt_ref, dbr_ref, dbi_ref):
        _, vjp = jax.vjp(_discretize, ar_ref[...], ai_ref[...], dt_ref[...], br_ref[...], bi_ref[...])
        dar_ref[...], dai_ref[...], ddt_ref[...], dbr_ref[...], dbi_ref[...] = vjp(
            (dlr_ref[...], dli_ref[...], dbbr_ref[...], dbbi_ref[...]))

    return pl.pallas_call(
        body, name="ssm_prep_bwd",
        out_shape=[S((G, P), F32), S((G, P), F32), S((G, 1), F32), S((G, P, H), F32), S((G, P, H), F32)],
    )(ar, ai, log_dt.reshape(G, 1), br, bi, dlr, dli, dbbr, dbbi)


def _block_diag(x):
    j, n, R, C = x.shape
    eye = jnp.eye(n, dtype=x.dtype)
    return (x[:, :, :, None, :] * eye[None, :, None, :, None]).reshape(j, n * R, n * C)


def _block_diag_take(x, R, C):
    j = x.shape[0]
    n = x.shape[1] // R
    x5 = x.reshape(j, n, R, n, C)
    return jnp.stack([x5[:, i, :, i, :] for i in range(n)], axis=1)


def _to_segments(x):
    L, C = x.shape
    return x.reshape(N_SEG, L // N_SEG, C).transpose(1, 0, 2).reshape(L, C)


def _from_segments(x):
    L, C = x.shape
    return x.reshape(L // N_SEG, N_SEG, C).transpose(1, 0, 2).reshape(L, C)


BIG = ("ffn1_w_gate", "ffn1_w_up", "ffn1_w_down", "w_in", "glu_w", "w_out",
       "ffn2_w_gate", "ffn2_w_up", "ffn2_w_down", "ple_w_gate", "ple_w_proj")
SMALL = ("ffn1_norm", "mix_norm", "conv_w", "conv_b", "ssm_A_re", "ssm_A_im", "ssm_B_re", "ssm_B_im", "ssm_C_re", "ssm_C_im",
         "ssm_D", "ssm_log_dt", "glu_b", "conv_out_norm", "ssm_out_norm", "ffn2_norm", "ple_norm")


def _ssm_mats(w):
    G, P, H = w["ssm_B_re"].shape
    lr, li, bbr, bbi = ssm_prep(w["ssm_A_re"], w["ssm_A_im"], w["ssm_log_dt"], w["ssm_B_re"], w["ssm_B_im"])
    lam = jnp.stack([lr.reshape(G * P), li.reshape(G * P)])
    bmat = lambda bb: _block_diag(bb.reshape(4, G // 4, P, H).transpose(0, 1, 3, 2)).astype(BF16)
    cmat = lambda c: _block_diag(c.reshape(4, G // 4, H, P).transpose(0, 1, 3, 2)).astype(BF16)
    return lam, bmat(bbr), bmat(bbi), cmat(w["ssm_C_re"]), cmat(w["ssm_C_im"]), w["ssm_D"].reshape(1, G * H)


def layer_fwd(h0, pb, w):
    L, D = h0.shape
    u1 = rmsnorm_fwd(h0, w["ffn1_norm"])
    a1, b1, s1 = ffn_up(u1, w["ffn1_w_gate"], w["ffn1_w_up"])
    h1 = mm_shard_k(s1, w["ffn1_w_down"], h0, 0.5)
    u2 = rmsnorm_fwd(h1, w["mix_norm"])
    z = mm_shard_n(u2, w["w_in"], F32)
    ya_n = conv_fwd(z, w["conv_w"], w["conv_b"], w["conv_out_norm"])
    us = _to_segments(z[3])
    mats = _ssm_mats(w)
    y = ssm_fwd(us, *mats)
    ys_n = glu_fwd(y, w["glu_w"], w["glu_b"], w["ssm_out_norm"])
    ycat = jnp.stack([ya_n, _from_segments(ys_n)])
    h2 = mm_shard_k(ycat, w["w_out"], h1, 1.0)
    u3 = rmsnorm_fwd(h2, w["ffn2_norm"])
    a2, b2, s2 = ffn_up(u3, w["ffn2_w_gate"], w["ffn2_w_up"])
    h3 = mm_shard_k(s2, w["ffn2_w_down"], h2, 0.5)
    un = rmsnorm_fwd(h3, w["ple_norm"])
    h4 = ple_fwd(un, pb, w["ple_w_gate"], w["ple_w_proj"], h3)
    saved = dict(h0=h0, u1=u1, a1=a1, b1=b1, s1=s1, h1=h1, u2=u2, z=z, us=us, mats=mats, y=y, ycat=ycat,
                 h2=h2, u3=u3, a2=a2, b2=b2, s2=s2, h3=h3, un=un)
    return h4, saved


def _ffn_bwd(dh, h_in, u, a, b, s, wg, wu, wd, gnorm):
    dhb = scale_cast(dh, 0.5)
    da, db = dact_swiglu(dhb, wd, a, b)
    g_wd = wgrad(s, dhb)
    g_wg = wgrad(u, da)
    g_wu = wgrad(u, db)
    dh_in, g_norm = dx_rms([(da, wg), (db, wu)], h_in, gnorm, dh)
    return dh_in, g_wg, g_wu, g_wd, g_norm


def layer_bwd(dh, pb, w, sv):
    L, D = dh.shape
    G, P, H = w["ssm_B_re"].shape
    dpre, dpp3 = ple_bwd(sv["un"], pb, w["ple_w_gate"], w["ple_w_proj"], dh)
    g_wpg = wgrad(sv["un"], dpre).reshape(N_SHARD, D // N_SHARD, D)
    g_wpp = wgrad(pb, dpp3)
    dh3, g_nple = dx_rms([(dpre[None], w["ple_w_gate"][None])], sv["h3"], w["ple_norm"], dh)
    dh2, g_wg2, g_wu2, g_wd2, g_nffn2 = _ffn_bwd(dh3, sv["h2"], sv["u3"], sv["a2"], sv["b2"], sv["s2"],
                                                 w["ffn2_w_gate"], w["ffn2_w_up"], w["ffn2_w_down"], w["ffn2_norm"])
    dhb = scale_cast(dh2, 1.0)
    dyn = dact_plain(dhb, w["w_out"])
    g_wout = wgrad(sv["ycat"], dhb).reshape(N_SHARD, -1, D)
    dz_abc, g_convw, g_convb, g_nconv = conv_bwd(sv["z"], w["conv_w"], w["conv_b"], w["conv_out_norm"], dyn)
    dy, dpre_g, zg, g_glub, g_nssm = glu_bwd(sv["y"], w["glu_w"], w["glu_b"], w["ssm_out_norm"], _to_segments(dyn[1]))
    C = zg.shape[1]
    g_gluw = wgrad(zg, dpre_g).reshape(N_SHARD, C // N_SHARD, C)
    dus, dlam, dbre, dbim, dcre, dcim, dd = ssm_bwd(sv["us"], dy, *sv["mats"])
    take_b = lambda m: _block_diag_take(m, H, P).transpose(0, 1, 3, 2).reshape(G, P, H)
    take_c = lambda m: _block_diag_take(m, P, H).transpose(0, 1, 3, 2).reshape(G, H, P)
    g_ar, g_ai, g_dt, g_br, g_bi = ssm_prep_bwd(
        w["ssm_A_re"], w["ssm_A_im"], w["ssm_log_dt"], w["ssm_B_re"], w["ssm_B_im"],
        dlam[0].reshape(G, P), dlam[1].reshape(G, P), take_b(dbre), take_b(dbim))
    dz3 = jnp.concatenate([dz_abc, _from_segments(dus).astype(BF16)[None]], axis=0)
    g_win = wgrad(sv["u2"], dz3)
    dh1, g_nmix = dx_rms([(dz3, w["w_in"])], sv["h1"], w["mix_norm"], dh2)
    dh0, g_wg1, g_wu1, g_wd1, g_nffn1 = _ffn_bwd(dh1, sv["h0"], sv["u1"], sv["a1"], sv["b1"], sv["s1"],
                                                 w["ffn1_w_gate"], w["ffn1_w_up"], w["ffn1_w_down"], w["ffn1_norm"])
    big = [g_wg1, g_wu1, g_wd1, g_win, g_gluw, g_wout, g_wg2, g_wu2, g_wd2, g_wpg, g_wpp]
    small = dict(ffn1_norm=g_nffn1, mix_norm=g_nmix, conv_w=g_convw, conv_b=g_convb, ssm_A_re=g_ar, ssm_A_im=g_ai,
                 ssm_B_re=g_br, ssm_B_im=g_bi, ssm_C_re=take_c(dcre), ssm_C_im=take_c(dcim), ssm_D=dd,
                 ssm_log_dt=g_dt, glu_b=g_glub, conv_out_norm=g_nconv, ssm_out_norm=g_nssm, ffn2_norm=g_nffn2,
                 ple_norm=g_nple)
    return dh0, big, small


def local_step(x, p, target, layers, final_norm):
    h = x
    saved = []
    pbs = []
    for i, w in enumerate(layers):
        pb = p[i].astype(BF16)
        h, sv = layer_fwd(h, pb, w)
        saved.append(sv)
        pbs.append(pb)
    loss_part, dh, g_final = loss_head(h, final_norm, target)
    bigs, smalls = [None] * len(layers), [None] * len(layers)
    for i in reversed(range(len(layers))):
        dh, bigs[i], smalls[i] = layer_bwd(dh, pbs[i], layers[i], saved[i])
    return loss_part, dh, bigs, smalls, g_final


def _row_tile(rows):
    for t in (512, 352, 256, 128, 64, 32, 16):
        if rows % t == 0:
            return t
    return rows


def elementwise(fn, ins, out_dtypes, name):
    rows, cols = ins[0].shape
    tr = _row_tile(rows)
    n_in = len(ins)

    def body(*refs):
        outs = fn(*[r[...] for r in refs[:n_in]])
        for o_ref, o in zip(refs[n_in:], outs):
            o_ref[...] = o.astype(o_ref.dtype)

    spec = pl.BlockSpec((tr, cols), lambda i: (i, 0))
    return pl.pallas_call(
        body, name=name, grid=(rows // tr,), in_specs=[spec] * n_in, out_specs=[spec] * len(out_dtypes),
        out_shape=[S((rows, cols), d) for d in out_dtypes], compiler_params=_params("parallel"),
    )(*ins)


def _rows(a):
    return a.reshape(-1, a.shape[-1])


def _adamw(w, g, m, v):
    m = ADAM_B1 * m + (1.0 - ADAM_B1) * g
    v = ADAM_B2 * v + (1.0 - ADAM_B2) * (g * g)
    m_hat = m / (1.0 - ADAM_B1 ** ADAM_STEP)
    v_hat = v / (1.0 - ADAM_B2 ** ADAM_STEP)
    delta = -ADAM_LR * (m_hat / (jnp.sqrt(v_hat) + ADAM_EPS) + ADAM_WD * w)
    return delta, m, v


ANY = pl.BlockSpec(memory_space=pl.ANY)


def _mesh_pos():
    return lax.axis_index("x"), lax.axis_index("y"), lax.axis_index("c")


def _other_chips(x, y):
    return [(1 - x, y), (x, 1 - y), (1 - x, 1 - y)]


def _remote(src, dst, send_sem, recv_sem, device):
    return pltpu.make_async_remote_copy(src_ref=src, dst_ref=dst, send_sem=send_sem, recv_sem=recv_sem,
                                        device_id=device, device_id_type=MESH)


def gather_weights(ws):
    n = len(ws)

    def body(*refs):
        ins, outs = refs[:n], refs[n:2 * n]
        send_sems, recv_sems, local_sems = refs[2 * n:]
        x, y, c = _mesh_pos()
        me_s = 2 * x + y
        sibling = (x, y, 1 - c)
        chips = _other_chips(x, y)
        n_half = ins[0].shape[0] // 2
        mine, other = pl.ds(c * n_half, n_half), pl.ds((1 - c) * n_half, n_half)
        local = [pltpu.make_async_copy(ins[t], outs[t].at[:, me_s], local_sems.at[t]) for t in range(n)]
        for cp in local:
            cp.start()
        sent = []
        for t in range(n):
            for j, (cx, cy) in enumerate(chips):
                cp = _remote(ins[t].at[mine], outs[t].at[mine, me_s], send_sems.at[t, j], recv_sems.at[t, j], (cx, cy, c))
                cp.start()
                sent.append(cp)
        for j, (cx, cy) in enumerate(chips):
            for t in range(n):
                blk = outs[t].at[mine, 2 * cx + cy]
                _remote(blk, blk, send_sems.at[t, j], recv_sems.at[t, j], (cx, cy, c)).wait_recv()
                cp = _remote(blk, blk, send_sems.at[t, 3 + j], recv_sems.at[t, 3 + j], sibling)
                cp.start()
                sent.append(cp)
        for j, (cx, cy) in enumerate(chips):
            for t in range(n):
                blk = outs[t].at[other, 2 * cx + cy]
                _remote(blk, blk, send_sems.at[t, 3 + j], recv_sems.at[t, 3 + j], sibling).wait_recv()
        for cp in sent:
            cp.wait_send()
        for cp in local:
            cp.wait()

    return pl.pallas_call(
        body, name="gather_weights", in_specs=[ANY] * n, out_specs=[ANY] * n,
        out_shape=[S((w.shape[0], N_SHARD) + w.shape[1:], w.dtype) for w in ws],
        scratch_shapes=[pltpu.SemaphoreType.DMA((n, 6)), pltpu.SemaphoreType.DMA((n, 6)), pltpu.SemaphoreType.DMA((n,))],
    )(*ws)


def reduce_pair(gs):
    n = len(gs)

    def body(*refs):
        ins, kept, got = refs[:n], refs[n:2 * n], refs[2 * n:3 * n]
        send_sems, recv_sems, local_sems = refs[3 * n:]
        x, y, c = _mesh_pos()
        copies = []
        for t in range(n):
            r2 = ins[t].shape[1] // 2
            keep = pl.ds(pl.multiple_of(c * r2, 8), r2)
            give = pl.ds(pl.multiple_of((1 - c) * r2, 8), r2)
            lc = pltpu.make_async_copy(ins[t].at[:, keep], kept[t], local_sems.at[t])
            rc = _remote(ins[t].at[:, give], got[t], send_sems.at[t], recv_sems.at[t], (x, y, 1 - c))
            lc.start()
            rc.start()
            copies += [lc, rc]
        for cp in copies:
            cp.wait()

    half = [S((g.shape[0], g.shape[1] // 2, g.shape[2]), g.dtype) for g in gs]
    outs = pl.pallas_call(
        body, name="reduce_pair", in_specs=[ANY] * n, out_specs=[ANY] * (2 * n), out_shape=half + half,
        scratch_shapes=[pltpu.SemaphoreType.DMA((n,)), pltpu.SemaphoreType.DMA((n,)), pltpu.SemaphoreType.DMA((n,))],
    )(*gs)
    return outs[:n], outs[n:]


def reduce_chips(sums, kept, got):
    n = len(sums)

    def body(*refs):
        a, kp, gt = refs[:n], refs[n:2 * n], refs[2 * n:3 * n]
        p2, own_k, own_g = refs[3 * n:4 * n], refs[4 * n:5 * n], refs[5 * n:6 * n]
        send_sems, recv_sems, local_sems = refs[6 * n:]
        x, y, c = _mesh_pos()
        me_s = 2 * x + y
        copies = []
        for t in range(n):
            for j, (cx, cy) in enumerate(_other_chips(x, y)):
                cp = _remote(a[t].at[2 * cx + cy], p2[t].at[j], send_sems.at[t, j], recv_sems.at[t, j], (cx, cy, c))
                cp.start()
                copies.append(cp)
            for k, (src, dst) in enumerate(((kp[t], own_k[t]), (gt[t], own_g[t]))):
                cp = pltpu.make_async_copy(src.at[me_s], dst, local_sems.at[t, k])
                cp.start()
                copies.append(cp)
        for cp in copies:
            cp.wait()

    outs = pl.pallas_call(
        body, name="reduce_chips", in_specs=[ANY] * (3 * n), out_specs=[ANY] * (3 * n),
        out_shape=([S((3,) + s.shape[1:], s.dtype) for s in sums] + [S(k.shape[1:], k.dtype) for k in kept]
                   + [S(g.shape[1:], g.dtype) for g in got]),
        scratch_shapes=[pltpu.SemaphoreType.DMA((n, 3)), pltpu.SemaphoreType.DMA((n, 3)), pltpu.SemaphoreType.DMA((n, 2))],
    )(*sums, *kept, *got)
    return outs[:n], outs[n:2 * n], outs[2 * n:]


def exchange_halves(rh):
    n = len(rh)

    def body(*refs):
        ins, outs = refs[:n], refs[n:2 * n]
        send_sems, recv_sems, local_sems = refs[2 * n:]
        x, y, c = _mesh_pos()
        copies = []
        for t in range(n):
            r2 = ins[t].shape[0]
            rows = pl.ds(pl.multiple_of(c * r2, 8), r2)
            lc = pltpu.make_async_copy(ins[t], outs[t].at[rows], local_sems.at[t])
            rc = _remote(ins[t], outs[t].at[rows], send_sems.at[t], recv_sems.at[t], (x, y, 1 - c))
            lc.start()
            rc.start()
            copies += [lc, rc]
        for cp in copies:
            cp.wait()

    return pl.pallas_call(
        body, name="exchange_halves", in_specs=[ANY] * n, out_specs=[ANY] * n,
        out_shape=[S((2 * r.shape[0], r.shape[1]), r.dtype) for r in rh],
        scratch_shapes=[pltpu.SemaphoreType.DMA((n,)), pltpu.SemaphoreType.DMA((n,)), pltpu.SemaphoreType.DMA((n,))],
    )(*rh)


def allreduce_small(vec):
    R = vec.shape[0]

    def body(x_ref, o_ref, buf, send_sems, recv_sems):
        x, y, c = _mesh_pos()
        me = 4 * x + 2 * y + c
        buf[me] = x_ref[...]
        copies = []
        for k in range(1, N_DEV):
            peer = (1 - x if k & 4 else x, 1 - y if k & 2 else y, 1 - c if k & 1 else c)
            cp = _remote(x_ref, buf.at[me], send_sems.at[k - 1], recv_sems.at[k - 1], peer)
            cp.start()
            copies.append(cp)
        for cp in copies:
            cp.wait()
        acc = buf[0]
        for d in range(1, N_DEV):
            acc = acc + buf[d]
        o_ref[...] = acc

    vm = pl.BlockSpec(memory_space=pltpu.VMEM)
    return pl.pallas_call(
        body, name="allreduce_small", in_specs=[vm], out_specs=vm, out_shape=S((R, 128), F32),
        scratch_shapes=[pltpu.VMEM((N_DEV, R, 128), F32), pltpu.SemaphoreType.DMA((N_DEV - 1,)),
                        pltpu.SemaphoreType.DMA((N_DEV - 1,))],
        compiler_params=pltpu.CompilerParams(vmem_limit_bytes=VMEM_LIMIT_BYTES),
    )(vec)


def reduce_layer(gs):
    kept, got = reduce_pair(gs)
    sums = [elementwise(lambda a, b: (a + b,), [_rows(k), _rows(g)], [BF16], "pair_sum")[0].reshape(k.shape)
            for k, g in zip(kept, got)]
    from_chips, own_k, own_g = reduce_chips(sums, kept, got)
    halves = [elementwise(lambda k, g, p0, p1, p2: (k + g + p0.astype(F32) + p1.astype(F32) + p2.astype(F32),),
                          [k, g, p[0], p[1], p[2]], [F32], "chip_sum")[0]
              for k, g, p in zip(own_k, own_g, from_chips)]
    return exchange_halves(halves)


W_NAMES = ("ffn1_norm", "ffn1_w_gate", "ffn1_w_up", "ffn1_w_down", "mix_norm", "w_in", "conv_w", "conv_b", "ssm_A_re", "ssm_A_im",
           "ssm_B_re", "ssm_B_im", "ssm_C_re", "ssm_C_im", "ssm_D", "ssm_log_dt", "glu_w", "glu_b", "conv_out_norm", "ssm_out_norm",
           "w_out", "ffn2_norm", "ffn2_w_gate", "ffn2_w_up", "ffn2_w_down", "ple_norm", "ple_w_gate", "ple_w_proj", "final_norm")
SMALL_ALL = SMALL + ("final_norm",)
PACK = 8 * 128


def _pack(parts):
    flat = jnp.concatenate([p.reshape(-1) for p in parts])
    pad = (-flat.shape[0]) % PACK
    return jnp.pad(flat, (0, pad)).reshape(-1, 128)


def _unpack(vec, shapes):
    flat = vec.reshape(-1)
    out, off = [], 0
    for shp in shapes:
        size = math.prod(shp)
        out.append(flat[off:off + size].reshape(shp))
        off += size
    return out


def _step(a):
    x, p, target = a["x"][0], a["p"][:, 0], a["loss_target"][0]
    depth = p.shape[0]
    L, D = x.shape
    me_s = 2 * lax.axis_index("x") + lax.axis_index("y")

    local_bf = [elementwise(lambda v: (v,), [_rows(a[n])], [BF16], "cast_bf16")[0].reshape(a[n].shape) for n in BIG]
    gathered = gather_weights(local_bf + [a["conv_w"]])

    def layer_weights(l):
        w = {n: a[n][l] for n in SMALL if n != "conv_w"}
        w.update({n: gathered[i][l] for i, n in enumerate(BIG)})
        C = w["glu_w"].shape[-1]
        w["glu_w"] = w["glu_w"].reshape(C, C)
        w["w_out"] = w["w_out"].reshape(2, -1, D)
        w["ple_w_gate"] = w["ple_w_gate"].reshape(D, D)
        w["conv_w"] = gathered[-1][l].transpose(1, 0, 2).reshape(3, -1)
        return w

    loss_part, dx, bigs, smalls, g_final = local_step(x, p, target, [layer_weights(l) for l in range(depth)], a["final_norm"])

    reduced = [reduce_layer(bigs[l]) for l in range(depth)]
    big_out = {}
    for i, n in enumerate(BIG):
        g = jnp.stack([reduced[l][i] for l in range(depth)])
        delta, new_m, new_v = elementwise(_adamw, [_rows(a[n]), _rows(g), _rows(a["m_" + n]), _rows(a["v_" + n])],
                                          [F32, F32, F32], "adamw")
        big_out[n] = (g, delta.reshape(g.shape), new_m.reshape(g.shape), new_v.reshape(g.shape))

    small_shapes = [(depth,) + smalls[0][n].shape for n in SMALL] + [g_final.shape, (1,)]
    parts = [smalls[l][n] for n in SMALL for l in range(depth)] + [g_final, loss_part[0, 0:1]]
    summed = _unpack(allreduce_small(_pack(parts)), small_shapes)
    g_small = dict(zip(SMALL_ALL, summed[:-1]))
    loss = summed[-1][0]
    n_conv = a["conv_w"].shape[-1]
    g_small["conv_w"] = lax.dynamic_slice_in_dim(g_small["conv_w"], me_s * n_conv, n_conv, axis=2)
    g_small = {n: g_small[n].reshape(a[n].shape) for n in SMALL_ALL}
    packed = [_pack([src[n] for n in SMALL_ALL]) for src in
              ({n: a[n] for n in SMALL_ALL}, g_small, {n: a["m_" + n] for n in SMALL_ALL}, {n: a["v_" + n] for n in SMALL_ALL})]
    shapes = [a[n].shape for n in SMALL_ALL]
    d_s, m_s, v_s = [dict(zip(SMALL_ALL, _unpack(o, shapes))) for o in elementwise(_adamw, packed, [F32, F32, F32], "adamw_small")]

    outs = {n: big_out[n] if n in big_out else (g_small[n], d_s[n], m_s[n], v_s[n]) for n in W_NAMES}
    return (loss, dx[None], *[outs[n][0] for n in W_NAMES], *[outs[n][1] for n in W_NAMES],
            *[outs[n][2] for n in W_NAMES], *[outs[n][3] for n in W_NAMES])


def kernel(x, p, ffn1_norm, ffn1_w_gate, ffn1_w_up, ffn1_w_down, mix_norm, w_in, conv_w, conv_b, ssm_A_re, ssm_A_im, ssm_B_re, ssm_B_im, ssm_C_re, ssm_C_im, ssm_D, ssm_log_dt, glu_w, glu_b, conv_out_norm, ssm_out_norm, w_out, ffn2_norm, ffn2_w_gate, ffn2_w_up, ffn2_w_down, ple_norm, ple_w_gate, ple_w_proj, final_norm, loss_target, m_ffn1_norm, m_ffn1_w_gate, m_ffn1_w_up, m_ffn1_w_down, m_mix_norm, m_w_in, m_conv_w, m_conv_b, m_ssm_A_re, m_ssm_A_im, m_ssm_B_re, m_ssm_B_im, m_ssm_C_re, m_ssm_C_im, m_ssm_D, m_ssm_log_dt, m_glu_w, m_glu_b, m_conv_out_norm, m_ssm_out_norm, m_w_out, m_ffn2_norm, m_ffn2_w_gate, m_ffn2_w_up, m_ffn2_w_down, m_ple_norm, m_ple_w_gate, m_ple_w_proj, m_final_norm, v_ffn1_norm, v_ffn1_w_gate, v_ffn1_w_up, v_ffn1_w_down, v_mix_norm, v_w_in, v_conv_w, v_conv_b, v_ssm_A_re, v_ssm_A_im, v_ssm_B_re, v_ssm_B_im, v_ssm_C_re, v_ssm_C_im, v_ssm_D, v_ssm_log_dt, v_glu_w, v_glu_b, v_conv_out_norm, v_ssm_out_norm, v_w_out, v_ffn2_norm, v_ffn2_w_gate, v_ffn2_w_up, v_ffn2_w_down, v_ple_norm, v_ple_w_gate, v_ple_w_proj, v_final_norm):
    return _step(dict(locals()))
```

```python
import functools
import math

import jax
import jax.numpy as jnp
from jax import lax
from jax.experimental import pallas as pl
from jax.experimental.pallas import tpu as pltpu

F32, BF16 = jnp.float32, jnp.bfloat16
S = jax.ShapeDtypeStruct
EPS = 1e-6
N_SEG = 8
N_SHARD = 4
N_DEV = 8
VMEM_LIMIT_BYTES = 56 * 1024 * 1024
ADAM_LR, ADAM_B1, ADAM_B2, ADAM_EPS, ADAM_WD, ADAM_STEP = 0.001, 0.9, 0.999, 1e-08, 0.01, 10
MESH = pl.DeviceIdType.MESH


def _params(*sem):
    return pltpu.CompilerParams(dimension_semantics=sem if sem else None, vmem_limit_bytes=VMEM_LIMIT_BYTES)


def _dot(a, b, ca, cb):
    return lax.dot_general(a, b, (((ca,), (cb,)), ((), ())), preferred_element_type=F32)


def _sigmoid(x):
    return 1.0 / (1.0 + jnp.exp(-x))


def _rstd(x):
    return lax.rsqrt(jnp.mean(x * x, axis=-1, keepdims=True) + EPS)


def _rms_bwd(x, g, dy):
    r = _rstd(x)
    xh = x * r
    dxh = dy * g
    dx = r * (dxh - xh * jnp.mean(dxh * xh, axis=-1, keepdims=True))
    return dx, jnp.sum(dy * xh, axis=0, keepdims=True)


def _tile(n, want):
    return want if n % want == 0 else n


def rmsnorm_fwd(h, g):
    L, D = h.shape
    tm = _tile(L, 512)

    def body(h_ref, g_ref, o_ref):
        x = h_ref[...]
        o_ref[...] = (x * _rstd(x) * g_ref[...]).astype(BF16)

    return pl.pallas_call(
        body, name="rmsnorm_fwd", grid=(L // tm,),
        in_specs=[pl.BlockSpec((tm, D), lambda m: (m, 0)), pl.BlockSpec((1, D), lambda m: (0, 0))],
        out_specs=pl.BlockSpec((tm, D), lambda m: (m, 0)),
        out_shape=S((L, D), BF16), compiler_params=_params("parallel"),
    )(h, g.reshape(1, D))


def ffn_up(u, wg, wu):
    L, D = u.shape
    ns, _, F = wg.shape
    tm = _tile(L, 512)

    def body(u_ref, wg_ref, wu_ref, a_ref, b_ref, s_ref):
        x = u_ref[...]
        a = _dot(x, wg_ref[0], 1, 0)
        b = _dot(x, wu_ref[0], 1, 0)
        a_ref[0] = a.astype(BF16)
        b_ref[0] = b.astype(BF16)
        s_ref[0] = (a * _sigmoid(a) * b).astype(BF16)

    w_spec = pl.BlockSpec((1, D, F), lambda s, m: (s, 0, 0))
    o_spec = pl.BlockSpec((1, tm, F), lambda s, m: (s, m, 0))
    return pl.pallas_call(
        body, name="ffn_up", grid=(ns, L // tm),
        in_specs=[pl.BlockSpec((tm, D), lambda s, m: (m, 0)), w_spec, w_spec],
        out_specs=[o_spec, o_spec, o_spec],
        out_shape=[S((ns, L, F), BF16)] * 3, compiler_params=_params("parallel", "parallel"),
    )(u, wg, wu)


def mm_shard_n(u, w3, out_dtype):
    L, K = u.shape
    ns, _, N = w3.shape
    tm = _tile(L, 512)

    def body(u_ref, w_ref, o_ref):
        o_ref[0] = _dot(u_ref[...], w_ref[0], 1, 0).astype(out_dtype)

    return pl.pallas_call(
        body, name="mm_shard_n", grid=(ns, L // tm),
        in_specs=[pl.BlockSpec((tm, K), lambda s, m: (m, 0)), pl.BlockSpec((1, K, N), lambda s, m: (s, 0, 0))],
        out_specs=pl.BlockSpec((1, tm, N), lambda s, m: (s, m, 0)),
        out_shape=S((ns, L, N), out_dtype), compiler_params=_params("parallel", "parallel"),
    )(u, w3)


def mm_shard_k(a3, w3, res, scale):
    nk, L, Kc = a3.shape
    N = w3.shape[2]
    tm = _tile(L, 512)

    def body(a_ref, w_ref, r_ref, o_ref, acc):
        k = pl.program_id(1)

        @pl.when(k == 0)
        def _():
            acc[...] = jnp.zeros_like(acc)

        acc[...] += _dot(a_ref[0], w_ref[0], 1, 0)

        @pl.when(k == nk - 1)
        def _():
            o_ref[...] = r_ref[...] + scale * acc[...]

    return pl.pallas_call(
        body, name="mm_shard_k", grid=(L // tm, nk),
        in_specs=[pl.BlockSpec((1, tm, Kc), lambda m, k: (k, m, 0)), pl.BlockSpec((1, Kc, N), lambda m, k: (k, 0, 0)),
                  pl.BlockSpec((tm, N), lambda m, k: (m, 0))],
        out_specs=pl.BlockSpec((tm, N), lambda m, k: (m, 0)),
        out_shape=S((L, N), F32), scratch_shapes=[pltpu.VMEM((tm, N), F32)],
        compiler_params=_params("parallel", "arbitrary"),
    )(a3, w3, res)


CONV_HALO = 8


def _conv_specs(L, tm, C, shard):
    nb = L // CONV_HALO
    per = tm // CONV_HALO
    main = pl.BlockSpec((1, tm, C), lambda m: (shard, m, 0))
    prev = pl.BlockSpec((1, CONV_HALO, C), lambda m: (shard, jnp.maximum(m * per - 1, 0), 0))
    nxt = pl.BlockSpec((1, CONV_HALO, C), lambda m: (shard, jnp.minimum((m + 1) * per, nb - 1), 0))
    return main, prev, nxt


def _conv_core(zb, zc, zv, w_ref, bias, grow, L):
    valid = (grow >= 0) & (grow < L)
    v = jnp.where(valid, zc * zv, 0.0)
    v1 = pltpu.roll(v, 1, 0)
    v2 = pltpu.roll(v, 2, 0)
    cb = w_ref[0:1, :] * v2 + w_ref[1:2, :] * v1 + w_ref[2:3, :] * v + bias
    return valid, v, v1, v2, cb, zb * cb


def conv_fwd(z, conv_w, conv_b, gnorm):
    _, L, C = z.shape
    tm = _tile(L, 256)
    H = CONV_HALO

    def body(zb_ref, zc_ref, zcp_ref, zv_ref, zvp_ref, w_ref, b_ref, g_ref, o_ref):
        m = pl.program_id(0)
        zc = jnp.concatenate([zcp_ref[0], zc_ref[0]], axis=0)
        zv = jnp.concatenate([zvp_ref[0], zv_ref[0]], axis=0)
        grow = m * tm - H + lax.broadcasted_iota(jnp.int32, (tm + H, C), 0)
        valid = grow >= 0
        v = jnp.where(valid, zc * zv, 0.0)
        v1 = pltpu.roll(v, 1, 0)
        v2 = pltpu.roll(v, 2, 0)
        cb = (w_ref[0:1, :] * v2 + w_ref[1:2, :] * v1 + w_ref[2:3, :] * v + b_ref[...])[H:, :]
        ya = zb_ref[0] * cb
        o_ref[...] = (ya * _rstd(ya) * g_ref[...]).astype(BF16)

    zb_m, _, _ = _conv_specs(L, tm, C, 0)
    zc_m, zc_p, _ = _conv_specs(L, tm, C, 1)
    zv_m, zv_p, _ = _conv_specs(L, tm, C, 2)
    row = lambda r: pl.BlockSpec((r, C), lambda m: (0, 0))
    return pl.pallas_call(
        body, name="conv_fwd", grid=(L // tm,),
        in_specs=[zb_m, zc_m, zc_p, zv_m, zv_p, row(3), row(1), row(1)],
        out_specs=pl.BlockSpec((tm, C), lambda m: (m, 0)),
        out_shape=S((L, C), BF16), compiler_params=_params("parallel"),
    )(z, z, z, z, z, conv_w, conv_b.reshape(1, C), gnorm.reshape(1, C))


def _cmul(ar, ai, br, bi):
    return ar * br - ai * bi, ar * bi + ai * br


def _scan_fwd(hr_ref, hi_ref, lr, li, n_steps):
    W = hr_ref.shape[1]
    zero = jnp.zeros((N_SEG, W), F32)

    def local(t, c):
        r = pl.multiple_of(t * N_SEG, N_SEG)
        pr, pi = _cmul(lr, li, c[0], c[1])
        nr = pr + hr_ref[pl.ds(r, N_SEG), :]
        ni = pi + hi_ref[pl.ds(r, N_SEG), :]
        hr_ref[pl.ds(r, N_SEG), :] = nr
        hi_ref[pl.ds(r, N_SEG), :] = ni
        return nr, ni

    fr, fi = lax.fori_loop(0, n_steps, local, (zero, zero))
    qr, qi = _cpow(lr, li, n_steps)
    row = lax.broadcasted_iota(jnp.int32, (N_SEG, W), 0)
    cr, ci = zero, zero
    for seg in range(1, N_SEG):
        tr, ti = _cmul(qr, qi, cr, ci)
        sr = pltpu.roll(fr + tr, 1, 0)
        si = pltpu.roll(fi + ti, 1, 0)
        cr = jnp.where(row == seg, sr, cr)
        ci = jnp.where(row == seg, si, ci)

    def fix(t, c):
        r = pl.multiple_of(t * N_SEG, N_SEG)
        pr, pi = _cmul(lr, li, c[0], c[1])
        ar, ai = _cmul(pr, pi, cr, ci)
        hr_ref[pl.ds(r, N_SEG), :] += ar
        hi_ref[pl.ds(r, N_SEG), :] += ai
        return pr, pi

    lax.fori_loop(0, n_steps, fix, (jnp.ones((N_SEG, W), F32), zero))


def _cpow(lr, li, n):
    rr, ri = None, None
    br, bi = lr, li
    while n:
        if n & 1:
            rr, ri = (br, bi) if rr is None else _cmul(rr, ri, br, bi)
        n >>= 1
        if n:
            br, bi = _cmul(br, bi, br, bi)
    return rr, ri


def _ssm_specs(L):
    col = lambda w: pl.BlockSpec((L, w), lambda j: (0, j))
    return dict(
        u=col(128), lam=pl.BlockSpec((2, 512), lambda j: (0, j)),
        bmat=pl.BlockSpec((1, 128, 512), lambda j: (j, 0, 0)), cmat=pl.BlockSpec((1, 512, 128), lambda j: (j, 0, 0)),
        d=pl.BlockSpec((1, 128), lambda j: (0, j)))


def ssm_fwd(us, lam, bre, bim, cre, cim, dvec):
    L = us.shape[0]
    n_steps = L // N_SEG
    sp = _ssm_specs(L)

    def body(u_ref, lam_ref, bre_ref, bim_ref, cre_ref, cim_ref, d_ref, y_ref, hr, hi):
        u = u_ref[...]
        ub = u.astype(BF16)
        hr[...] = _dot(ub, bre_ref[0], 1, 0)
        hi[...] = _dot(ub, bim_ref[0], 1, 0)
        lr = jnp.broadcast_to(lam_ref[0:1, :], (N_SEG, 512))
        li = jnp.broadcast_to(lam_ref[1:2, :], (N_SEG, 512))
        _scan_fwd(hr, hi, lr, li, n_steps)
        y_ref[...] = (_dot(hr[...].astype(BF16), cre_ref[0], 1, 0) - _dot(hi[...].astype(BF16), cim_ref[0], 1, 0)
                      + d_ref[...] * u)

    return pl.pallas_call(
        body, name="ssm_fwd", grid=(4,),
        in_specs=[sp["u"], sp["lam"], sp["bmat"], sp["bmat"], sp["cmat"], sp["cmat"], sp["d"]],
        out_specs=sp["u"], out_shape=S((L, 512), F32),
        scratch_shapes=[pltpu.VMEM((L, 512), F32), pltpu.VMEM((L, 512), F32)],
        compiler_params=_params("parallel"),
    )(us, lam, bre, bim, cre, cim, dvec)


_GELU_C = math.sqrt(2.0 / math.pi)


def _gelu(y):
    t = jnp.tanh(_GELU_C * (y + 0.044715 * y * y * y))
    return 0.5 * y * (1.0 + t), t


def glu_fwd(y, w, b, gnorm):
    L, C = y.shape
    tm = _tile(L, 512)

    def body(y_ref, w_ref, b_ref, g_ref, o_ref):
        zg, _ = _gelu(y_ref[...])
        out = zg * _sigmoid(_dot(zg.astype(BF16), w_ref[...], 1, 0) + b_ref[...])
        o_ref[...] = (out * _rstd(out) * g_ref[...]).astype(BF16)

    row = pl.BlockSpec((1, C), lambda m: (0, 0))
    return pl.pallas_call(
        body, name="glu_fwd", grid=(L // tm,),
        in_specs=[pl.BlockSpec((tm, C), lambda m: (m, 0)), pl.BlockSpec((C, C), lambda m: (0, 0)), row, row],
        out_specs=pl.BlockSpec((tm, C), lambda m: (m, 0)),
        out_shape=S((L, C), BF16), compiler_params=_params("parallel"),
    )(y, w, b.reshape(1, C), gnorm.reshape(1, C))


def _ple_specs(L, D, P, tm, nb):
    return [pl.BlockSpec((tm, D), lambda n, m: (m, 0)), pl.BlockSpec((tm, P), lambda n, m: (m, 0)),
            pl.BlockSpec((D, nb), lambda n, m: (0, n)), pl.BlockSpec((1, P, nb), lambda n, m: (n, 0, 0)),
            pl.BlockSpec((tm, nb), lambda n, m: (m, n))]


def ple_fwd(un, pb, wpg, wpp, h):
    L, D = un.shape
    ns, P, nb = wpp.shape
    tm = _tile(L, 512)

    def body(un_ref, p_ref, wg_ref, wp_ref, h_ref, o_ref):
        gate = _sigmoid(_dot(un_ref[...], wg_ref[...], 1, 0))
        o_ref[...] = h_ref[...] + _dot(p_ref[...], wp_ref[0], 1, 0) * gate

    return pl.pallas_call(
        body, name="ple_fwd", grid=(ns, L // tm),
        in_specs=_ple_specs(L, D, P, tm, nb),
        out_specs=pl.BlockSpec((tm, nb), lambda n, m: (m, n)),
        out_shape=S((L, D), F32), compiler_params=_params("parallel", "parallel"),
    )(un, pb, wpg, wpp, h)


def loss_head(h, g, target):
    L, D = h.shape
    tm = _tile(L, 256)

    def body(h_ref, g_ref, t_ref, loss_ref, dh_ref, dg_ref):
        m = pl.program_id(0)
        x = h_ref[...]
        gg = g_ref[...]
        e = x * _rstd(x) * gg - t_ref[...]
        dx, dg = _rms_bwd(x, gg, e * (1.0 / D))
        dh_ref[...] = dx
        part = jnp.full((8, 128), 0.5 / D, F32) * jnp.sum(e * e)

        @pl.when(m == 0)
        def _():
            loss_ref[...] = part
            dg_ref[...] = dg

        @pl.when(m > 0)
        def _():
            loss_ref[...] += part
            dg_ref[...] += dg

    return pl.pallas_call(
        body, name="loss_head", grid=(L // tm,),
        in_specs=[pl.BlockSpec((tm, D), lambda m: (m, 0)), pl.BlockSpec((1, D), lambda m: (0, 0)),
                  pl.BlockSpec((tm, D), lambda m: (m, 0))],
        out_specs=[pl.BlockSpec((8, 128), lambda m: (0, 0)), pl.BlockSpec((tm, D), lambda m: (m, 0)),
                   pl.BlockSpec((1, D), lambda m: (0, 0))],
        out_shape=[S((8, 128), F32), S((L, D), F32), S((1, D), F32)],
        compiler_params=_params("arbitrary"),
    )(h, g.reshape(1, D), target)


def scale_cast(x, scale):
    L, D = x.shape
    tm = _tile(L, 512)

    def body(x_ref, o_ref):
        o_ref[...] = (scale * x_ref[...]).astype(BF16)

    return pl.pallas_call(
        body, name="scale_cast", grid=(L // tm,),
        in_specs=[pl.BlockSpec((tm, D), lambda m: (m, 0))], out_specs=pl.BlockSpec((tm, D), lambda m: (m, 0)),
        out_shape=S((L, D), BF16), compiler_params=_params("parallel"),
    )(x)


def ple_bwd(un, pb, wpg, wpp, dh):
    L, D = un.shape
    ns, P, nb = wpp.shape
    tm = _tile(L, 512)

    def body(un_ref, p_ref, wg_ref, wp_ref, dh_ref, dpre_ref, dpp_ref):
        gate = _sigmoid(_dot(un_ref[...], wg_ref[...], 1, 0))
        pp = _dot(p_ref[...], wp_ref[0], 1, 0)
        d = dh_ref[...]
        dpp_ref[0] = (d * gate).astype(BF16)
        dpre_ref[...] = (d * pp * gate * (1.0 - gate)).astype(BF16)

    return pl.pallas_call(
        body, name="ple_bwd", grid=(ns, L // tm),
        in_specs=_ple_specs(L, D, P, tm, nb),
        out_specs=[pl.BlockSpec((tm, nb), lambda n, m: (m, n)), pl.BlockSpec((1, tm, nb), lambda n, m: (n, m, 0))],
        out_shape=[S((L, D), BF16), S((ns, L, nb), BF16)], compiler_params=_params("parallel", "parallel"),
    )(un, pb, wpg, wpp, dh)


def wgrad(a, b):
    a3 = a if a.ndim == 3 else a[None]
    b3 = b if b.ndim == 3 else b[None]
    ns = max(a3.shape[0], b3.shape[0])
    _, L, Ka = a3.shape
    N = b3.shape[2]
    tk = _tile(L, 512)
    nk = L // tk
    a_map = (lambda s, k: (s, k, 0)) if a3.shape[0] > 1 else (lambda s, k: (0, k, 0))
    b_map = (lambda s, k: (s, k, 0)) if b3.shape[0] > 1 else (lambda s, k: (0, k, 0))

    def body(a_ref, b_ref, o_ref, acc):
        k = pl.program_id(1)

        @pl.when(k == 0)
        def _():
            acc[...] = jnp.zeros_like(acc)

        acc[...] += _dot(a_ref[0], b_ref[0], 0, 0)

        @pl.when(k == nk - 1)
        def _():
            o_ref[0] = acc[...]

    return pl.pallas_call(
        body, name="wgrad", grid=(ns, nk),
        in_specs=[pl.BlockSpec((1, tk, Ka), a_map), pl.BlockSpec((1, tk, N), b_map)],
        out_specs=pl.BlockSpec((1, Ka, N), lambda s, k: (s, 0, 0)),
        out_shape=S((ns, Ka, N), F32), scratch_shapes=[pltpu.VMEM((Ka, N), F32)],
        compiler_params=_params("parallel", "arbitrary"),
    )(a3, b3)


def dx_rms(pairs, h, g, dh_in):
    L, D = h.shape
    nk = pairs[0][0].shape[0]
    n_pairs = len(pairs)
    tm = _tile(L, 256)
    n_m = L // tm

    def body(*refs):
        ins, (h_ref, g_ref, dhi_ref, dho_ref, dg_ref, acc) = refs[:2 * n_pairs], refs[2 * n_pairs:]
        m, k = pl.program_id(0), pl.program_id(1)

        @pl.when(k == 0)
        def _():
            acc[...] = jnp.zeros_like(acc)

        for i in range(n_pairs):
            acc[...] += _dot(ins[2 * i][0], ins[2 * i + 1][0], 1, 1)

        @pl.when(k == nk - 1)
        def _():
            dx, dg = _rms_bwd(h_ref[...], g_ref[...], acc[...])
            dho_ref[...] = dhi_ref[...] + dx

            @pl.when(m == 0)
            def _():
                dg_ref[...] = dg

            @pl.when(m > 0)
            def _():
                dg_ref[...] += dg

    in_specs, args = [], []
    for a3, w3 in pairs:
        Kc = a3.shape[2]
        in_specs += [pl.BlockSpec((1, tm, Kc), lambda m, k: (k, m, 0)), pl.BlockSpec((1, D, Kc), lambda m, k: (k, 0, 0))]
        args += [a3, w3]
    tile = pl.BlockSpec((tm, D), lambda m, k: (m, 0))
    row = pl.BlockSpec((1, D), lambda m, k: (0, 0))
    return pl.pallas_call(
        body, name="dx_rms", grid=(n_m, nk),
        in_specs=in_specs + [tile, row, tile], out_specs=[tile, row],
        out_shape=[S((L, D), F32), S((1, D), F32)], scratch_shapes=[pltpu.VMEM((tm, D), F32)],
        compiler_params=_params("arbitrary", "arbitrary"),
    )(*args, h, g.reshape(1, D), dh_in)


def dact_plain(dhb, w3):
    L, D = dhb.shape
    ns, N, _ = w3.shape
    tm = _tile(L, 512)

    def body(d_ref, w_ref, o_ref):
        o_ref[0] = _dot(d_ref[...], w_ref[0], 1, 1)

    return pl.pallas_call(
        body, name="dact_plain", grid=(ns, L // tm),
        in_specs=[pl.BlockSpec((tm, D), lambda s, m: (m, 0)), pl.BlockSpec((1, N, D), lambda s, m: (s, 0, 0))],
        out_specs=pl.BlockSpec((1, tm, N), lambda s, m: (s, m, 0)),
        out_shape=S((ns, L, N), F32), compiler_params=_params("parallel", "parallel"),
    )(dhb, w3)


def dact_swiglu(dhb, wd, a3, b3):
    L, D = dhb.shape
    ns, F, _ = wd.shape
    tm = _tile(L, 512)

    def body(d_ref, w_ref, a_ref, b_ref, da_ref, db_ref):
        ds = _dot(d_ref[...], w_ref[0], 1, 1)
        a = a_ref[0].astype(F32)
        b = b_ref[0].astype(F32)
        sg = _sigmoid(a)
        da_ref[0] = (ds * b * (sg * (1.0 + a * (1.0 - sg)))).astype(BF16)
        db_ref[0] = (ds * (a * sg)).astype(BF16)

    t_spec = pl.BlockSpec((1, tm, F), lambda s, m: (s, m, 0))
    return pl.pallas_call(
        body, name="dact_swiglu", grid=(ns, L // tm),
        in_specs=[pl.BlockSpec((tm, D), lambda s, m: (m, 0)), pl.BlockSpec((1, F, D), lambda s, m: (s, 0, 0)), t_spec, t_spec],
        out_specs=[t_spec, t_spec], out_shape=[S((ns, L, F), BF16)] * 2,
        compiler_params=_params("parallel", "parallel"),
    )(dhb, wd, a3, b3)


def conv_bwd(z, conv_w, conv_b, gnorm, dyn):
    _, L, C = z.shape
    tm = _tile(L, 256)
    H = CONV_HALO
    T = tm + 2 * H

    def body(zb_ref, zbp_ref, zbn_ref, zc_ref, zcp_ref, zcn_ref, zv_ref, zvp_ref, zvn_ref, d_ref, dp_ref, dn_ref,
             w_ref, b_ref, g_ref, dz_ref, dw_ref, db_ref, dg_ref):
        m = pl.program_id(0)
        cat = lambda p, c, n: jnp.concatenate([p[0], c[0], n[0]], axis=0)
        zb, zc, zv, d = cat(zbp_ref, zb_ref, zbn_ref), cat(zcp_ref, zc_ref, zcn_ref), cat(zvp_ref, zv_ref, zvn_ref), cat(dp_ref, d_ref, dn_ref)
        grow = m * tm - H + lax.broadcasted_iota(jnp.int32, (T, C), 0)
        valid, v, v1, v2, cb, ya = _conv_core(zb, zc, zv, w_ref, b_ref[...], grow, L)
        dya, _ = _rms_bwd(ya, g_ref[...], d)
        dc = jnp.where(valid, dya * zb, 0.0)
        dv = w_ref[2:3, :] * dc + w_ref[1:2, :] * pltpu.roll(dc, T - 1, 0) + w_ref[0:1, :] * pltpu.roll(dc, T - 2, 0)
        dz_ref[0] = (dya * cb)[H:H + tm, :].astype(BF16)
        dz_ref[1] = (dv * zv)[H:H + tm, :].astype(BF16)
        dz_ref[2] = (dv * zc)[H:H + tm, :].astype(BF16)
        rs = lambda x: jnp.sum(x[H:H + tm, :], axis=0, keepdims=True)
        yh = ya * _rstd(ya)
        dw = jnp.concatenate([rs(dc * v2), rs(dc * v1), rs(dc * v)], axis=0)
        dbias, dg = rs(dc), rs(d * yh)

        @pl.when(m == 0)
        def _():
            dw_ref[...] = dw
            db_ref[...] = dbias
            dg_ref[...] = dg

        @pl.when(m > 0)
        def _():
            dw_ref[...] += dw
            db_ref[...] += dbias
            dg_ref[...] += dg

    row = lambda r: pl.BlockSpec((r, C), lambda m: (0, 0))
    specs = [*_conv_specs(L, tm, C, 0), *_conv_specs(L, tm, C, 1), *_conv_specs(L, tm, C, 2), *_conv_specs(L, tm, C, 0)]
    return pl.pallas_call(
        body, name="conv_bwd", grid=(L // tm,),
        in_specs=specs + [row(3), row(1), row(1)],
        out_specs=[pl.BlockSpec((3, tm, C), lambda m: (0, m, 0)), row(3), row(1), row(1)],
        out_shape=[S((3, L, C), BF16), S((3, C), F32), S((1, C), F32), S((1, C), F32)],
        compiler_params=_params("arbitrary"),
    )(z, z, z, z, z, z, z, z, z, dyn, dyn, dyn, conv_w, conv_b.reshape(1, C), gnorm.reshape(1, C))


def glu_bwd(y, w, b, gnorm, dn):
    L, C = y.shape
    tm = _tile(L, 256)

    def body(y_ref, w_ref, b_ref, g_ref, d_ref, dy_ref, dpre_ref, zg_ref, db_ref, dg_ref):
        m = pl.program_id(0)
        yv = y_ref[...]
        zg, t = _gelu(yv)
        zgb = zg.astype(BF16)
        sg = _sigmoid(_dot(zgb, w_ref[...], 1, 0) + b_ref[...])
        out = zg * sg
        dout, dg = _rms_bwd(out, g_ref[...], d_ref[...])
        dpre = dout * zg * sg * (1.0 - sg)
        dpre_b = dpre.astype(BF16)
        dzg = dout * sg + _dot(dpre_b, w_ref[...], 1, 1)
        dt = (1.0 - t * t) * _GELU_C * (1.0 + 3.0 * 0.044715 * yv * yv)
        dy_ref[...] = dzg * (0.5 * (1.0 + t) + 0.5 * yv * dt)
        dpre_ref[...] = dpre_b
        zg_ref[...] = zgb
        dbias = jnp.sum(dpre, axis=0, keepdims=True)

        @pl.when(m == 0)
        def _():
            db_ref[...] = dbias
            dg_ref[...] = dg

        @pl.when(m > 0)
        def _():
            db_ref[...] += dbias
            dg_ref[...] += dg

    tile = pl.BlockSpec((tm, C), lambda m: (m, 0))
    row = pl.BlockSpec((1, C), lambda m: (0, 0))
    return pl.pallas_call(
        body, name="glu_bwd", grid=(L // tm,),
        in_specs=[tile, pl.BlockSpec((C, C), lambda m: (0, 0)), row, row, tile],
        out_specs=[tile, tile, tile, row, row],
        out_shape=[S((L, C), F32), S((L, C), BF16), S((L, C), BF16), S((1, C), F32), S((1, C), F32)],
        compiler_params=_params("arbitrary"),
    )(y, w, b.reshape(1, C), gnorm.reshape(1, C), dn)


def _scan_bwd(gr_ref, gi_ref, hr_ref, hi_ref, lr, li, n_steps):
    W = gr_ref.shape[1]
    zero = jnp.zeros((N_SEG, W), F32)
    lic = -li

    def local(i, c):
        r = pl.multiple_of((n_steps - 1 - i) * N_SEG, N_SEG)
        pr, pi = _cmul(lr, lic, c[0], c[1])
        nr = pr + gr_ref[pl.ds(r, N_SEG), :]
        ni = pi + gi_ref[pl.ds(r, N_SEG), :]
        gr_ref[pl.ds(r, N_SEG), :] = nr
        gi_ref[pl.ds(r, N_SEG), :] = ni
        return nr, ni

    fr, fi = lax.fori_loop(0, n_steps, local, (zero, zero))
    qr, qi = _cpow(lr, lic, n_steps)
    row = lax.broadcasted_iota(jnp.int32, (N_SEG, W), 0)
    cr, ci = zero, zero
    for seg in range(N_SEG - 2, -1, -1):
        tr, ti = _cmul(qr, qi, cr, ci)
        sr = pltpu.roll(fr + tr, N_SEG - 1, 0)
        si = pltpu.roll(fi + ti, N_SEG - 1, 0)
        cr = jnp.where(row == seg, sr, cr)
        ci = jnp.where(row == seg, si, ci)

    def fix(i, c):
        pwr, pwi, ar, ai = c
        t = n_steps - 1 - i
        r = pl.multiple_of(t * N_SEG, N_SEG)
        pwr, pwi = _cmul(lr, lic, pwr, pwi)
        xr, xi = _cmul(pwr, pwi, cr, ci)
        g_r = gr_ref[pl.ds(r, N_SEG), :] + xr
        g_i = gi_ref[pl.ds(r, N_SEG), :] + xi
        gr_ref[pl.ds(r, N_SEG), :] = g_r
        gi_ref[pl.ds(r, N_SEG), :] = g_i
        rp = pl.multiple_of(jnp.maximum(t - 1, 0) * N_SEG, N_SEG)
        hpr = hr_ref[pl.ds(rp, N_SEG), :]
        hpi = hi_ref[pl.ds(rp, N_SEG), :]
        live = t > 0
        ar = ar + jnp.where(live, hpr * g_r + hpi * g_i, 0.0)
        ai = ai + jnp.where(live, hpr * g_i - hpi * g_r, 0.0)
        return pwr, pwi, ar, ai

    _, _, ar, ai = lax.fori_loop(0, n_steps, fix, (jnp.ones((N_SEG, W), F32), zero, zero, zero))
    last = pl.ds((n_steps - 1) * N_SEG, N_SEG)
    hpr = jnp.where(row == 0, 0.0, pltpu.roll(hr_ref[last, :], 1, 0))
    hpi = jnp.where(row == 0, 0.0, pltpu.roll(hi_ref[last, :], 1, 0))
    g_r, g_i = gr_ref[pl.ds(0, N_SEG), :], gi_ref[pl.ds(0, N_SEG), :]
    ar = ar + hpr * g_r + hpi * g_i
    ai = ai + hpr * g_i - hpi * g_r
    return jnp.sum(ar, axis=0, keepdims=True), jnp.sum(ai, axis=0, keepdims=True)


def ssm_bwd(us, dy, lam, bre, bim, cre, cim, dvec):
    L = us.shape[0]
    n_steps = L // N_SEG
    sp = _ssm_specs(L)

    def body(u_ref, dy_ref, lam_ref, bre_ref, bim_ref, cre_ref, cim_ref, d_ref,
             du_ref, dlam_ref, dbre_ref, dbim_ref, dcre_ref, dcim_ref, dd_ref, hr, hi, gr, gi):
        u = u_ref[...]
        ub = u.astype(BF16)
        dyv = dy_ref[...]
        dyb = dyv.astype(BF16)
        hr[...] = _dot(ub, bre_ref[0], 1, 0)
        hi[...] = _dot(ub, bim_ref[0], 1, 0)
        lr = jnp.broadcast_to(lam_ref[0:1, :], (N_SEG, 512))
        li = jnp.broadcast_to(lam_ref[1:2, :], (N_SEG, 512))
        _scan_fwd(hr, hi, lr, li, n_steps)
        dcre_ref[0] = _dot(hr[...].astype(BF16), dyb, 0, 0)
        dcim_ref[0] = -_dot(hi[...].astype(BF16), dyb, 0, 0)
        gr[...] = _dot(dyb, cre_ref[0], 1, 1)
        gi[...] = -_dot(dyb, cim_ref[0], 1, 1)
        dlr, dli = _scan_bwd(gr, gi, hr, hi, lr, li, n_steps)
        dlam_ref[...] = jnp.concatenate([dlr, dli], axis=0)
        grb, gib = gr[...].astype(BF16), gi[...].astype(BF16)
        du_ref[...] = _dot(grb, bre_ref[0], 1, 1) + _dot(gib, bim_ref[0], 1, 1) + d_ref[...] * dyv
        dbre_ref[0] = _dot(ub, grb, 0, 0)
        dbim_ref[0] = _dot(ub, gib, 0, 0)
        dd_ref[...] = jnp.sum(dyv * u, axis=0, keepdims=True)

    big = pltpu.VMEM((L, 512), F32)
    return pl.pallas_call(
        body, name="ssm_bwd", grid=(4,),
        in_specs=[sp["u"], sp["u"], sp["lam"], sp["bmat"], sp["bmat"], sp["cmat"], sp["cmat"], sp["d"]],
        out_specs=[sp["u"], sp["lam"], sp["bmat"], sp["bmat"], sp["cmat"], sp["cmat"], sp["d"]],
        out_shape=[S((L, 512), F32), S((2, 2048), F32), S((4, 128, 512), F32), S((4, 128, 512), F32),
                   S((4, 512, 128), F32), S((4, 512, 128), F32), S((1, 512), F32)],
        scratch_shapes=[big, big, big, big], compiler_params=_params("parallel"),
    )(us, dy, lam, bre, bim, cre, cim, dvec)


def _discretize(ar, ai, log_dt, br, bi):
    dt = jnp.exp(log_dt)
    mag = jnp.exp(ar * dt)
    ph = ai * dt
    lr, li = mag * jnp.cos(ph), mag * jnp.sin(ph)
    nr, ni = lr - 1.0, li
    den = ar * ar + ai * ai
    fr = (nr * ar + ni * ai) / den
    fi = (ni * ar - nr * ai) / den
    return lr, li, fr[..., None] * br - fi[..., None] * bi, fr[..., None] * bi + fi[..., None] * br


def ssm_prep(ar, ai, log_dt, br, bi):
    G, P, H = br.shape

    def body(ar_ref, ai_ref, dt_ref, br_ref, bi_ref, lr_ref, li_ref, bbr_ref, bbi_ref):
        lr_ref[...], li_ref[...], bbr_ref[...], bbi_ref[...] = _discretize(
            ar_ref[...], ai_ref[...], dt_ref[...], br_ref[...], bi_ref[...])

    return pl.pallas_call(
        body, name="ssm_prep",
        out_shape=[S((G, P), F32), S((G, P), F32), S((G, P, H), F32), S((G, P, H), F32)],
    )(ar, ai, log_dt.reshape(G, 1), br, bi)


def ssm_prep_bwd(ar, ai, log_dt, br, bi, dlr, dli, dbbr, dbbi):
    G, P, H = br.shape

    def body(ar_ref, ai_ref, dt_ref, br_ref, bi_ref, dlr_ref, dli_ref, dbbr_ref, dbbi_ref,
             dar_ref, dai_ref, ddt_ref, dbr_ref, dbi_ref):
        _, vjp = jax.vjp(_discretize, ar_ref[...], ai_ref[...], dt_ref[...], br_ref[...], bi_ref[...])
        dar_ref[...], dai_ref[...], ddt_ref[...], dbr_ref[...], dbi_ref[...] = vjp(
            (dlr_ref[...], dli_ref[...], dbbr_ref[...], dbbi_ref[...]))

    return pl.pallas_call(
        body, name="ssm_prep_bwd",
        out_shape=[S((G, P), F32), S((G, P), F32), S((G, 1), F32), S((G, P, H), F32), S((G, P, H), F32)],
    )(ar, ai, log_dt.reshape(G, 1), br, bi, dlr, dli, dbbr, dbbi)


def _block_diag(x):
    j, n, R, C = x.shape
    eye = jnp.eye(n, dtype=x.dtype)
    return (x[:, :, :, None, :] * eye[None, :, None, :, None]).reshape(j, n * R, n * C)


def _block_diag_take(x, R, C):
    j = x.shape[0]
    n = x.shape[1] // R
    x5 = x.reshape(j, n, R, n, C)
    return jnp.stack([x5[:, i, :, i, :] for i in range(n)], axis=1)


def _to_segments(x):
    L, C = x.shape
    return x.reshape(N_SEG, L // N_SEG, C).transpose(1, 0, 2).reshape(L, C)


def _from_segments(x):
    L, C = x.shape
    return x.reshape(L // N_SEG, N_SEG, C).transpose(1, 0, 2).reshape(L, C)


BIG = ("ffn1_w_gate", "ffn1_w_up", "ffn1_w_down", "w_in", "glu_w", "w_out",
       "ffn2_w_gate", "ffn2_w_up", "ffn2_w_down", "ple_w_gate", "ple_w_proj")
SMALL = ("ffn1_norm", "mix_norm", "conv_w", "conv_b", "ssm_A_re", "ssm_A_im", "ssm_B_re", "ssm_B_im", "ssm_C_re", "ssm_C_im",
         "ssm_D", "ssm_log_dt", "glu_b", "conv_out_norm", "ssm_out_norm", "ffn2_norm", "ple_norm")


def _ssm_mats(w):
    G, P, H = w["ssm_B_re"].shape
    lr, li, bbr, bbi = ssm_prep(w["ssm_A_re"], w["ssm_A_im"], w["ssm_log_dt"], w["ssm_B_re"], w["ssm_B_im"])
    lam = jnp.stack([lr.reshape(G * P), li.reshape(G * P)])
    bmat = lambda bb: _block_diag(bb.reshape(4, G // 4, P, H).transpose(0, 1, 3, 2)).astype(BF16)
    cmat = lambda c: _block_diag(c.reshape(4, G // 4, H, P).transpose(0, 1, 3, 2)).astype(BF16)
    return lam, bmat(bbr), bmat(bbi), cmat(w["ssm_C_re"]), cmat(w["ssm_C_im"]), w["ssm_D"].reshape(1, G * H)


def layer_fwd(h0, pb, w):
    L, D = h0.shape
    u1 = rmsnorm_fwd(h0, w["ffn1_norm"])
    a1, b1, s1 = ffn_up(u1, w["ffn1_w_gate"], w["ffn1_w_up"])
    h1 = mm_shard_k(s1, w["ffn1_w_down"], h0, 0.5)
    u2 = rmsnorm_fwd(h1, w["mix_norm"])
    z = mm_shard_n(u2, w["w_in"], F32)
    ya_n = conv_fwd(z, w["conv_w"], w["conv_b"], w["conv_out_norm"])
    us = _to_segments(z[3])
    mats = _ssm_mats(w)
    y = ssm_fwd(us, *mats)
    ys_n = glu_fwd(y, w["glu_w"], w["glu_b"], w["ssm_out_norm"])
    ycat = jnp.stack([ya_n, _from_segments(ys_n)])
    h2 = mm_shard_k(ycat, w["w_out"], h1, 1.0)
    u3 = rmsnorm_fwd(h2, w["ffn2_norm"])
    a2, b2, s2 = ffn_up(u3, w["ffn2_w_gate"], w["ffn2_w_up"])
    h3 = mm_shard_k(s2, w["ffn2_w_down"], h2, 0.5)
    un = rmsnorm_fwd(h3, w["ple_norm"])
    h4 = ple_fwd(un, pb, w["ple_w_gate"], w["ple_w_proj"], h3)
    saved = dict(h0=h0, u1=u1, a1=a1, b1=b1, s1=s1, h1=h1, u2=u2, z=z, us=us, mats=mats, y=y, ycat=ycat,
                 h2=h2, u3=u3, a2=a2, b2=b2, s2=s2, h3=h3, un=un)
    return h4, saved


def _ffn_bwd(dh, h_in, u, a, b, s, wg, wu, wd, gnorm):
    dhb = scale_cast(dh, 0.5)
    da, db = dact_swiglu(dhb, wd, a, b)
    g_wd = wgrad(s, dhb)
    g_wg = wgrad(u, da)
    g_wu = wgrad(u, db)
    dh_in, g_norm = dx_rms([(da, wg), (db, wu)], h_in, gnorm, dh)
    return dh_in, g_wg, g_wu, g_wd, g_norm


def layer_bwd(dh, pb, w, sv):
    L, D = dh.shape
    G, P, H = w["ssm_B_re"].shape
    dpre, dpp3 = ple_bwd(sv["un"], pb, w["ple_w_gate"], w["ple_w_proj"], dh)
    g_wpg = wgrad(sv["un"], dpre).reshape(N_SHARD, D // N_SHARD, D)
    g_wpp = wgrad(pb, dpp3)
    dh3, g_nple = dx_rms([(dpre[None], w["ple_w_gate"][None])], sv["h3"], w["ple_norm"], dh)
    dh2, g_wg2, g_wu2, g_wd2, g_nffn2 = _ffn_bwd(dh3, sv["h2"], sv["u3"], sv["a2"], sv["b2"], sv["s2"],
                                                 w["ffn2_w_gate"], w["ffn2_w_up"], w["ffn2_w_down"], w["ffn2_norm"])
    dhb = scale_cast(dh2, 1.0)
    dyn = dact_plain(dhb, w["w_out"])
    g_wout = wgrad(sv["ycat"], dhb).reshape(N_SHARD, -1, D)
    dz_abc, g_convw, g_convb, g_nconv = conv_bwd(sv["z"], w["conv_w"], w["conv_b"], w["conv_out_norm"], dyn)
    dy, dpre_g, zg, g_glub, g_nssm = glu_bwd(sv["y"], w["glu_w"], w["glu_b"], w["ssm_out_norm"], _to_segments(dyn[1]))
    C = zg.shape[1]
    g_gluw = wgrad(zg, dpre_g).reshape(N_SHARD, C // N_SHARD, C)
    dus, dlam, dbre, dbim, dcre, dcim, dd = ssm_bwd(sv["us"], dy, *sv["mats"])
    take_b = lambda m: _block_diag_take(m, H, P).transpose(0, 1, 3, 2).reshape(G, P, H)
    take_c = lambda m: _block_diag_take(m, P, H).transpose(0, 1, 3, 2).reshape(G, H, P)
    g_ar, g_ai, g_dt, g_br, g_bi = ssm_prep_bwd(
        w["ssm_A_re"], w["ssm_A_im"], w["ssm_log_dt"], w["ssm_B_re"], w["ssm_B_im"],
        dlam[0].reshape(G, P), dlam[1].reshape(G, P), take_b(dbre), take_b(dbim))
    dz3 = jnp.concatenate([dz_abc, _from_segments(dus).astype(BF16)[None]], axis=0)
    g_win = wgrad(sv["u2"], dz3)
    dh1, g_nmix = dx_rms([(dz3, w["w_in"])], sv["h1"], w["mix_norm"], dh2)
    dh0, g_wg1, g_wu1, g_wd1, g_nffn1 = _ffn_bwd(dh1, sv["h0"], sv["u1"], sv["a1"], sv["b1"], sv["s1"],
                                                 w["ffn1_w_gate"], w["ffn1_w_up"], w["ffn1_w_down"], w["ffn1_norm"])
    big = [g_wg1, g_wu1, g_wd1, g_win, g_gluw, g_wout, g_wg2, g_wu2, g_wd2, g_wpg, g_wpp]
    small = dict(ffn1_norm=g_nffn1, mix_norm=g_nmix, conv_w=g_convw, conv_b=g_convb, ssm_A_re=g_ar, ssm_A_im=g_ai,
                 ssm_B_re=g_br, ssm_B_im=g_bi, ssm_C_re=take_c(dcre), ssm_C_im=take_c(dcim), ssm_D=dd,
                 ssm_log_dt=g_dt, glu_b=g_glub, conv_out_norm=g_nconv, ssm_out_norm=g_nssm, ffn2_norm=g_nffn2,
                 ple_norm=g_nple)
    return dh0, big, small


def local_step(x, p, target, layers, final_norm):
    h = x
    saved = []
    pbs = []
    for i, w in enumerate(layers):
        pb = p[i].astype(BF16)
        h, sv = layer_fwd(h, pb, w)
        saved.append(sv)
        pbs.append(pb)
    loss_part, dh, g_final = loss_head(h, final_norm, target)
    bigs, smalls = [None] * len(layers), [None] * len(layers)
    for i in reversed(range(len(layers))):
        dh, bigs[i], smalls[i] = layer_bwd(dh, pbs[i], layers[i], saved[i])
    return loss_part, dh, bigs, smalls, g_final


def _row_tile(rows):
    for t in (512, 352, 256, 128, 64, 32, 16):
        if rows % t == 0:
            return t
    return rows


def elementwise(fn, ins, out_dtypes, name):
    rows, cols = ins[0].shape
    tr = _row_tile(rows)
    n_in = len(ins)

    def body(*refs):
        outs = fn(*[r[...] for r in refs[:n_in]])
        for o_ref, o in zip(refs[n_in:], outs):
            o_ref[...] = o.astype(o_ref.dtype)

    spec = pl.BlockSpec((tr, cols), lambda i: (i, 0))
    return pl.pallas_call(
        body, name=name, grid=(rows // tr,), in_specs=[spec] * n_in, out_specs=[spec] * len(out_dtypes),
        out_shape=[S((rows, cols), d) for d in out_dtypes], compiler_params=_params("parallel"),
    )(*ins)


def _rows(a):
    return a.reshape(-1, a.shape[-1])


def _adamw(w, g, m, v):
    m = ADAM_B1 * m + (1.0 - ADAM_B1) * g
    v = ADAM_B2 * v + (1.0 - ADAM_B2) * (g * g)
    m_hat = m / (1.0 - ADAM_B1 ** ADAM_STEP)
    v_hat = v / (1.0 - ADAM_B2 ** ADAM_STEP)
    delta = -ADAM_LR * (m_hat / (jnp.sqrt(v_hat) + ADAM_EPS) + ADAM_WD * w)
    return delta, m, v


ANY = pl.BlockSpec(memory_space=pl.ANY)


def _mesh_pos():
    return lax.axis_index("x"), lax.axis_index("y"), lax.axis_index("c")


def _other_chips(x, y):
    return [(1 - x, y), (x, 1 - y), (1 - x, 1 - y)]


def _remote(src, dst, send_sem, recv_sem, device):
    return pltpu.make_async_remote_copy(src_ref=src, dst_ref=dst, send_sem=send_sem, recv_sem=recv_sem,
                                        device_id=device, device_id_type=MESH)


def gather_weights(ws):
    n = len(ws)

    def body(*refs):
        outs = refs[n:2 * n]
        send_sems, recv_sems = refs[2 * n:]
        x, y, c = _mesh_pos()
        me_s = 2 * x + y
        sibling = (x, y, 1 - c)
        chips = _other_chips(x, y)
        n_half = outs[0].shape[0] // 2
        mine, other = pl.ds(c * n_half, n_half), pl.ds((1 - c) * n_half, n_half)
        sent = []
        for t in range(n):
            for j, (cx, cy) in enumerate(chips):
                blk = outs[t].at[mine, me_s]
                cp = _remote(blk, blk, send_sems.at[t, j], recv_sems.at[t, j], (cx, cy, c))
                cp.start()
                sent.append(cp)
        for j, (cx, cy) in enumerate(chips):
            for t in range(n):
                blk = outs[t].at[mine, 2 * cx + cy]
                _remote(blk, blk, send_sems.at[t, j], recv_sems.at[t, j], (cx, cy, c)).wait_recv()
                cp = _remote(blk, blk, send_sems.at[t, 3 + j], recv_sems.at[t, 3 + j], sibling)
                cp.start()
                sent.append(cp)
        for j, (cx, cy) in enumerate(chips):
            for t in range(n):
                blk = outs[t].at[other, 2 * cx + cy]
                _remote(blk, blk, send_sems.at[t, 3 + j], recv_sems.at[t, 3 + j], sibling).wait_recv()
        for cp in sent:
            cp.wait_send()

    return pl.pallas_call(
        body, name="gather_weights", in_specs=[ANY] * n, out_specs=[ANY] * n,
        out_shape=[S(w.shape, w.dtype) for w in ws], input_output_aliases={t: t for t in range(n)},
        scratch_shapes=[pltpu.SemaphoreType.DMA((n, 6)), pltpu.SemaphoreType.DMA((n, 6))],
    )(*ws)


def cast_place(w, pos, dtype):
    layers, r, c = w.shape
    tr = _row_tile(r)

    def body(pos_ref, w_ref, o_ref):
        o_ref[0, 0] = w_ref[0].astype(dtype)

    return pl.pallas_call(
        body, name="cast_place",
        grid_spec=pltpu.PrefetchScalarGridSpec(
            num_scalar_prefetch=1, grid=(layers, r // tr),
            in_specs=[pl.BlockSpec((1, tr, c), lambda l, i, pos: (l, i, 0))],
            out_specs=pl.BlockSpec((1, 1, tr, c), lambda l, i, pos: (l, pos[1], i, 0))),
        out_shape=S((layers, N_SHARD, r, c), dtype), compiler_params=_params("parallel", "parallel"),
    )(pos, w)


def reduce_pair(gs):
    n = len(gs)

    def body(*refs):
        ins, got = refs[:n], refs[n:2 * n]
        send_sems, recv_sems = refs[2 * n:]
        x, y, c = _mesh_pos()
        copies = []
        for t in range(n):
            r2 = ins[t].shape[1] // 2
            give = pl.ds(pl.multiple_of((1 - c) * r2, 8), r2)
            cp = _remote(ins[t].at[:, give], got[t], send_sems.at[t], recv_sems.at[t], (x, y, 1 - c))
            cp.start()
            copies.append(cp)
        for cp in copies:
            cp.wait()

    return pl.pallas_call(
        body, name="reduce_pair", in_specs=[ANY] * n, out_specs=[ANY] * n,
        out_shape=[S((g.shape[0], g.shape[1] // 2, g.shape[2]), g.dtype) for g in gs],
        scratch_shapes=[pltpu.SemaphoreType.DMA((n,)), pltpu.SemaphoreType.DMA((n,))],
    )(*gs)


def pair_sum(g, got, pos):
    ns, r2, c = got.shape
    tr = _row_tile(r2)
    n_i = r2 // tr

    def body(pos_ref, g_ref, got_ref, sum_ref, own_ref):
        s = pl.program_id(1)
        v = g_ref[0] + got_ref[0]
        sum_ref[0] = v.astype(BF16)

        @pl.when(s == pos_ref[1])
        def _():
            own_ref[...] = v

    return pl.pallas_call(
        body, name="pair_sum",
        grid_spec=pltpu.PrefetchScalarGridSpec(
            num_scalar_prefetch=1, grid=(n_i, ns),
            in_specs=[pl.BlockSpec((1, tr, c), lambda i, s, pos: (s, pos[0] * n_i + i, 0)),
                      pl.BlockSpec((1, tr, c), lambda i, s, pos: (s, i, 0))],
            out_specs=[pl.BlockSpec((1, tr, c), lambda i, s, pos: (s, i, 0)), pl.BlockSpec((tr, c), lambda i, s, pos: (i, 0))]),
        out_shape=[S((ns, r2, c), BF16), S((r2, c), F32)], compiler_params=_params("parallel", "arbitrary"),
    )(pos, g, got)


def reduce_chips(sums):
    n = len(sums)

    def body(*refs):
        a, p2 = refs[:n], refs[n:2 * n]
        send_sems, recv_sems = refs[2 * n:]
        x, y, c = _mesh_pos()
        copies = []
        for t in range(n):
            for j, (cx, cy) in enumerate(_other_chips(x, y)):
                cp = _remote(a[t].at[2 * cx + cy], p2[t].at[j], send_sems.at[t, j], recv_sems.at[t, j], (cx, cy, c))
                cp.start()
                copies.append(cp)
        for cp in copies:
            cp.wait()

    return pl.pallas_call(
        body, name="reduce_chips", in_specs=[ANY] * n, out_specs=[ANY] * n,
        out_shape=[S((3,) + s.shape[1:], s.dtype) for s in sums],
        scratch_shapes=[pltpu.SemaphoreType.DMA((n, 3)), pltpu.SemaphoreType.DMA((n, 3))],
    )(*sums)


def chip_sum(own, p2, pos):
    r2, c = own.shape
    tr = _row_tile(r2)
    n_i = r2 // tr

    def body(pos_ref, own_ref, a_ref, b_ref, c_ref, o_ref):
        o_ref[...] = own_ref[...] + a_ref[0].astype(F32) + b_ref[0].astype(F32) + c_ref[0].astype(F32)

    peer = lambda j: pl.BlockSpec((1, tr, c), lambda i, pos: (j, i, 0))
    return pl.pallas_call(
        body, name="chip_sum",
        grid_spec=pltpu.PrefetchScalarGridSpec(
            num_scalar_prefetch=1, grid=(n_i,),
            in_specs=[pl.BlockSpec((tr, c), lambda i, pos: (i, 0)), peer(0), peer(1), peer(2)],
            out_specs=pl.BlockSpec((tr, c), lambda i, pos: (pos[0] * n_i + i, 0))),
        out_shape=S((2 * r2, c), F32), compiler_params=_params("parallel"),
    )(pos, own, p2, p2, p2)


def exchange_halves(rs):
    n = len(rs)

    def body(*refs):
        outs = refs[n:2 * n]
        send_sems, recv_sems = refs[2 * n:]
        x, y, c = _mesh_pos()
        copies = []
        for t in range(n):
            r2 = outs[t].shape[0] // 2
            rows = outs[t].at[pl.ds(pl.multiple_of(c * r2, 8), r2)]
            cp = _remote(rows, rows, send_sems.at[t], recv_sems.at[t], (x, y, 1 - c))
            cp.start()
            copies.append(cp)
        for cp in copies:
            cp.wait()

    return pl.pallas_call(
        body, name="exchange_halves", in_specs=[ANY] * n, out_specs=[ANY] * n,
        out_shape=[S(r.shape, r.dtype) for r in rs], input_output_aliases={t: t for t in range(n)},
        scratch_shapes=[pltpu.SemaphoreType.DMA((n,)), pltpu.SemaphoreType.DMA((n,))],
    )(*rs)


def allreduce_small(vec):
    R = vec.shape[0]

    def body(x_ref, o_ref, buf, send_sems, recv_sems):
        x, y, c = _mesh_pos()
        me = 4 * x + 2 * y + c
        buf[me] = x_ref[...]
        copies = []
        for k in range(1, N_DEV):
            peer = (1 - x if k & 4 else x, 1 - y if k & 2 else y, 1 - c if k & 1 else c)
            cp = _remote(x_ref, buf.at[me], send_sems.at[k - 1], recv_sems.at[k - 1], peer)
            cp.start()
            copies.append(cp)
        for cp in copies:
            cp.wait()
        acc = buf[0]
        for d in range(1, N_DEV):
            acc = acc + buf[d]
        o_ref[...] = acc

    vm = pl.BlockSpec(memory_space=pltpu.VMEM)
    return pl.pallas_call(
        body, name="allreduce_small", in_specs=[vm], out_specs=vm, out_shape=S((R, 128), F32),
        scratch_shapes=[pltpu.VMEM((N_DEV, R, 128), F32), pltpu.SemaphoreType.DMA((N_DEV - 1,)),
                        pltpu.SemaphoreType.DMA((N_DEV - 1,))],
        compiler_params=pltpu.CompilerParams(vmem_limit_bytes=VMEM_LIMIT_BYTES),
    )(vec)


def reduce_layer(gs, pos):
    got = reduce_pair(gs)
    sums, own = zip(*[pair_sum(g, o, pos) for g, o in zip(gs, got)])
    from_chips = reduce_chips(list(sums))
    return exchange_halves([chip_sum(o, p, pos) for o, p in zip(own, from_chips)])


W_NAMES = ("ffn1_norm", "ffn1_w_gate", "ffn1_w_up", "ffn1_w_down", "mix_norm", "w_in", "conv_w", "conv_b", "ssm_A_re", "ssm_A_im",
           "ssm_B_re", "ssm_B_im", "ssm_C_re", "ssm_C_im", "ssm_D", "ssm_log_dt", "glu_w", "glu_b", "conv_out_norm", "ssm_out_norm",
           "w_out", "ffn2_norm", "ffn2_w_gate", "ffn2_w_up", "ffn2_w_down", "ple_norm", "ple_w_gate", "ple_w_proj", "final_norm")
SMALL_ALL = SMALL + ("final_norm",)
PACK = 8 * 128


def _pack(parts):
    flat = jnp.concatenate([p.reshape(-1) for p in parts])
    pad = (-flat.shape[0]) % PACK
    return jnp.pad(flat, (0, pad)).reshape(-1, 128)


def _unpack(vec, shapes):
    flat = vec.reshape(-1)
    out, off = [], 0
    for shp in shapes:
        size = math.prod(shp)
        out.append(flat[off:off + size].reshape(shp))
        off += size
    return out


def _step(a):
    x, p, target = a["x"][0], a["p"][:, 0], a["loss_target"][0]
    depth = p.shape[0]
    L, D = x.shape
    me_s = 2 * lax.axis_index("x") + lax.axis_index("y")

    pos = jnp.stack([lax.axis_index("c"), me_s]).astype(jnp.int32)
    gathered = gather_weights([cast_place(a[n], pos, BF16) for n in BIG] + [cast_place(a["conv_w"], pos, F32)])

    def layer_weights(l):
        w = {n: a[n][l] for n in SMALL if n != "conv_w"}
        w.update({n: gathered[i][l] for i, n in enumerate(BIG)})
        C = w["glu_w"].shape[-1]
        w["glu_w"] = w["glu_w"].reshape(C, C)
        w["w_out"] = w["w_out"].reshape(2, -1, D)
        w["ple_w_gate"] = w["ple_w_gate"].reshape(D, D)
        w["conv_w"] = gathered[-1][l].transpose(1, 0, 2).reshape(3, -1)
        return w

    loss_part, dx, bigs, smalls, g_final = local_step(x, p, target, [layer_weights(l) for l in range(depth)], a["final_norm"])

    reduced = [reduce_layer(bigs[l], pos) for l in range(depth)]
    big_out = {}
    for i, n in enumerate(BIG):
        g = jnp.stack([reduced[l][i] for l in range(depth)])
        delta, new_m, new_v = elementwise(_adamw, [_rows(a[n]), _rows(g), _rows(a["m_" + n]), _rows(a["v_" + n])],
                                          [F32, F32, F32], "adamw")
        big_out[n] = (g, delta.reshape(g.shape), new_m.reshape(g.shape), new_v.reshape(g.shape))

    small_shapes = [(depth,) + smalls[0][n].shape for n in SMALL] + [g_final.shape, (1,)]
    parts = [smalls[l][n] for n in SMALL for l in range(depth)] + [g_final, loss_part[0, 0:1]]
    summed = _unpack(allreduce_small(_pack(parts)), small_shapes)
    g_small = dict(zip(SMALL_ALL, summed[:-1]))
    loss = summed[-1][0]
    n_conv = a["conv_w"].shape[-1]
    g_small["conv_w"] = lax.dynamic_slice_in_dim(g_small["conv_w"], me_s * n_conv, n_conv, axis=2)
    g_small = {n: g_small[n].reshape(a[n].shape) for n in SMALL_ALL}
    packed = [_pack([src[n] for n in SMALL_ALL]) for src in
              ({n: a[n] for n in SMALL_ALL}, g_small, {n: a["m_" + n] for n in SMALL_ALL}, {n: a["v_" + n] for n in SMALL_ALL})]
    shapes = [a[n].shape for n in SMALL_ALL]
    d_s, m_s, v_s = [dict(zip(SMALL_ALL, _unpack(o, shapes))) for o in elementwise(_adamw, packed, [F32, F32, F32], "adamw_small")]

    outs = {n: big_out[n] if n in big_out else (g_small[n], d_s[n], m_s[n], v_s[n]) for n in W_NAMES}
    return (loss, dx[None], *[outs[n][0] for n in W_NAMES], *[outs[n][1] for n in W_NAMES],
            *[outs[n][2] for n in W_NAMES], *[outs[n][3] for n in W_NAMES])


def kernel(x, p, ffn1_norm, ffn1_w_gate, ffn1_w_up, ffn1_w_down, mix_norm, w_in, conv_w, conv_b, ssm_A_re, ssm_A_im, ssm_B_re, ssm_B_im, ssm_C_re, ssm_C_im, ssm_D, ssm_log_dt, glu_w, glu_b, conv_out_norm, ssm_out_norm, w_out, ffn2_norm, ffn2_w_gate, ffn2_w_up, ffn2_w_down, ple_norm, ple_w_gate, ple_w_proj, final_norm, loss_target, m_ffn1_norm, m_ffn1_w_gate, m_ffn1_w_up, m_ffn1_w_down, m_mix_norm, m_w_in, m_conv_w, m_conv_b, m_ssm_A_re, m_ssm_A_im, m_ssm_B_re, m_ssm_B_im, m_ssm_C_re, m_ssm_C_im, m_ssm_D, m_ssm_log_dt, m_glu_w, m_glu_b, m_conv_out_norm, m_ssm_out_norm, m_w_out, m_ffn2_norm, m_ffn2_w_gate, m_ffn2_w_up, m_ffn2_w_down, m_ple_norm, m_ple_w_gate, m_ple_w_proj, m_final_norm, v_ffn1_norm, v_ffn1_w_gate, v_ffn1_w_up, v_ffn1_w_down, v_mix_norm, v_w_in, v_conv_w, v_conv_b, v_ssm_A_re, v_ssm_A_im, v_ssm_B_re, v_ssm_B_im, v_ssm_C_re, v_ssm_C_im, v_ssm_D, v_ssm_log_dt, v_glu_w, v_glu_b, v_conv_out_norm, v_ssm_out_norm, v_w_out, v_ffn2_norm, v_ffn2_w_gate, v_ffn2_w_up, v_ffn2_w_down, v_ple_norm, v_ple_w_gate, v_ple_w_proj, v_final_norm):
    return _step(dict(locals()))
```

```python
import functools
import math

import jax
import jax.numpy as jnp
from jax import lax
from jax.experimental import pallas as pl
from jax.experimental.pallas import tpu as pltpu

F32, BF16 = jnp.float32, jnp.bfloat16
S = jax.ShapeDtypeStruct
EPS = 1e-6
N_SEG = 8
N_SHARD = 4
N_DEV = 8
VMEM_LIMIT_BYTES = 56 * 1024 * 1024
ADAM_LR, ADAM_B1, ADAM_B2, ADAM_EPS, ADAM_WD, ADAM_STEP = 0.001, 0.9, 0.999, 1e-08, 0.01, 10
MESH = pl.DeviceIdType.MESH


def _params(*sem):
    return pltpu.CompilerParams(dimension_semantics=sem if sem else None, vmem_limit_bytes=VMEM_LIMIT_BYTES)


def _dot(a, b, ca, cb):
    return lax.dot_general(a, b, (((ca,), (cb,)), ((), ())), preferred_element_type=F32)


def _sigmoid(x):
    return 1.0 / (1.0 + jnp.exp(-x))


def _rstd(x):
    return lax.rsqrt(jnp.mean(x * x, axis=-1, keepdims=True) + EPS)


def _rms_bwd(x, g, dy):
    r = _rstd(x)
    xh = x * r
    dxh = dy * g
    dx = r * (dxh - xh * jnp.mean(dxh * xh, axis=-1, keepdims=True))
    return dx, jnp.sum(dy * xh, axis=0, keepdims=True)


def _tile(n, want):
    return want if n % want == 0 else n


def rmsnorm_fwd(h, g):
    L, D = h.shape
    tm = _tile(L, 512)

    def body(h_ref, g_ref, o_ref):
        x = h_ref[...]
        o_ref[...] = (x * _rstd(x) * g_ref[...]).astype(BF16)

    return pl.pallas_call(
        body, name="rmsnorm_fwd", grid=(L // tm,),
        in_specs=[pl.BlockSpec((tm, D), lambda m: (m, 0)), pl.BlockSpec((1, D), lambda m: (0, 0))],
        out_specs=pl.BlockSpec((tm, D), lambda m: (m, 0)),
        out_shape=S((L, D), BF16), compiler_params=_params("parallel"),
    )(h, g.reshape(1, D))


def ffn_up(u, wg, wu):
    L, D = u.shape
    ns, _, F = wg.shape
    tm = _tile(L, 512)

    def body(u_ref, wg_ref, wu_ref, a_ref, b_ref, s_ref):
        x = u_ref[...]
        a = _dot(x, wg_ref[0], 1, 0)
        b = _dot(x, wu_ref[0], 1, 0)
        a_ref[0] = a.astype(BF16)
        b_ref[0] = b.astype(BF16)
        s_ref[0] = (a * _sigmoid(a) * b).astype(BF16)

    w_spec = pl.BlockSpec((1, D, F), lambda s, m: (s, 0, 0))
    o_spec = pl.BlockSpec((1, tm, F), lambda s, m: (s, m, 0))
    return pl.pallas_call(
        body, name="ffn_up", grid=(ns, L // tm),
        in_specs=[pl.BlockSpec((tm, D), lambda s, m: (m, 0)), w_spec, w_spec],
        out_specs=[o_spec, o_spec, o_spec],
        out_shape=[S((ns, L, F), BF16)] * 3, compiler_params=_params("parallel", "parallel"),
    )(u, wg, wu)


def mm_shard_n(u, w3, out_dtype):
    L, K = u.shape
    ns, _, N = w3.shape
    tm = _tile(L, 512)

    def body(u_ref, w_ref, o_ref):
        o_ref[0] = _dot(u_ref[...], w_ref[0], 1, 0).astype(out_dtype)

    return pl.pallas_call(
        body, name="mm_shard_n", grid=(ns, L // tm),
        in_specs=[pl.BlockSpec((tm, K), lambda s, m: (m, 0)), pl.BlockSpec((1, K, N), lambda s, m: (s, 0, 0))],
        out_specs=pl.BlockSpec((1, tm, N), lambda s, m: (s, m, 0)),
        out_shape=S((ns, L, N), out_dtype), compiler_params=_params("parallel", "parallel"),
    )(u, w3)


def mm_shard_k(a3, w3, res, scale, g_next):
    nk, L, Kc = a3.shape
    N = w3.shape[2]
    tm = _tile(L, 512)

    def body(a_ref, w_ref, r_ref, g_ref, o_ref, u_ref, acc):
        k = pl.program_id(1)
        part = _dot(a_ref[0], w_ref[0], 1, 0)

        @pl.when(k == 0)
        def _():
            acc[...] = part

        @pl.when(k > 0)
        def _():
            acc[...] += part

        @pl.when(k == nk - 1)
        def _():
            h = r_ref[...] + scale * acc[...]
            o_ref[...] = h
            u_ref[...] = (h * _rstd(h) * g_ref[...]).astype(BF16)

    tile = pl.BlockSpec((tm, N), lambda m, k: (m, 0))
    return pl.pallas_call(
        body, name="mm_shard_k", grid=(L // tm, nk),
        in_specs=[pl.BlockSpec((1, tm, Kc), lambda m, k: (k, m, 0)), pl.BlockSpec((1, Kc, N), lambda m, k: (k, 0, 0)),
                  tile, pl.BlockSpec((1, N), lambda m, k: (0, 0))],
        out_specs=[tile, tile],
        out_shape=[S((L, N), F32), S((L, N), BF16)], scratch_shapes=[pltpu.VMEM((tm, N), F32)],
        compiler_params=_params("parallel", "arbitrary"),
    )(a3, w3, res, g_next.reshape(1, N))


CONV_HALO = 8


def _conv_specs(L, tm, C, shard):
    nb = L // CONV_HALO
    per = tm // CONV_HALO
    main = pl.BlockSpec((1, tm, C), lambda m: (shard, m, 0))
    prev = pl.BlockSpec((1, CONV_HALO, C), lambda m: (shard, jnp.maximum(m * per - 1, 0), 0))
    nxt = pl.BlockSpec((1, CONV_HALO, C), lambda m: (shard, jnp.minimum((m + 1) * per, nb - 1), 0))
    return main, prev, nxt


def _conv_core(zb, zc, zv, w_ref, bias, grow, L):
    valid = (grow >= 0) & (grow < L)
    v = jnp.where(valid, zc * zv, 0.0)
    v1 = pltpu.roll(v, 1, 0)
    v2 = pltpu.roll(v, 2, 0)
    cb = w_ref[0:1, :] * v2 + w_ref[1:2, :] * v1 + w_ref[2:3, :] * v + bias
    return valid, v, v1, v2, cb, zb * cb


def conv_fwd(z, conv_w, conv_b, gnorm):
    _, L, C = z.shape
    tm = _tile(L, 256)
    H = CONV_HALO

    def body(zb_ref, zc_ref, zcp_ref, zv_ref, zvp_ref, w_ref, b_ref, g_ref, o_ref):
        m = pl.program_id(0)
        zc = jnp.concatenate([zcp_ref[0], zc_ref[0]], axis=0)
        zv = jnp.concatenate([zvp_ref[0], zv_ref[0]], axis=0)
        grow = m * tm - H + lax.broadcasted_iota(jnp.int32, (tm + H, C), 0)
        valid = grow >= 0
        v = jnp.where(valid, zc * zv, 0.0)
        v1 = pltpu.roll(v, 1, 0)
        v2 = pltpu.roll(v, 2, 0)
        cb = (w_ref[0:1, :] * v2 + w_ref[1:2, :] * v1 + w_ref[2:3, :] * v + b_ref[...])[H:, :]
        ya = zb_ref[0] * cb
        o_ref[...] = (ya * _rstd(ya) * g_ref[...]).astype(BF16)

    zb_m, _, _ = _conv_specs(L, tm, C, 0)
    zc_m, zc_p, _ = _conv_specs(L, tm, C, 1)
    zv_m, zv_p, _ = _conv_specs(L, tm, C, 2)
    row = lambda r: pl.BlockSpec((r, C), lambda m: (0, 0))
    return pl.pallas_call(
        body, name="conv_fwd", grid=(L // tm,),
        in_specs=[zb_m, zc_m, zc_p, zv_m, zv_p, row(3), row(1), row(1)],
        out_specs=pl.BlockSpec((tm, C), lambda m: (m, 0)),
        out_shape=S((L, C), BF16), compiler_params=_params("parallel"),
    )(z, z, z, z, z, conv_w, conv_b.reshape(1, C), gnorm.reshape(1, C))


def _cmul(ar, ai, br, bi):
    return ar * br - ai * bi, ar * bi + ai * br


def _scan_fwd(hr_ref, hi_ref, lr, li, n_steps):
    W = hr_ref.shape[1]
    zero = jnp.zeros((N_SEG, W), F32)

    def local(t, c):
        r = pl.multiple_of(t * N_SEG, N_SEG)
        pr, pi = _cmul(lr, li, c[0], c[1])
        nr = pr + hr_ref[pl.ds(r, N_SEG), :]
        ni = pi + hi_ref[pl.ds(r, N_SEG), :]
        hr_ref[pl.ds(r, N_SEG), :] = nr
        hi_ref[pl.ds(r, N_SEG), :] = ni
        return nr, ni

    fr, fi = lax.fori_loop(0, n_steps, local, (zero, zero))
    qr, qi = _cpow(lr, li, n_steps)
    row = lax.broadcasted_iota(jnp.int32, (N_SEG, W), 0)
    cr, ci = zero, zero
    for seg in range(1, N_SEG):
        tr, ti = _cmul(qr, qi, cr, ci)
        sr = pltpu.roll(fr + tr, 1, 0)
        si = pltpu.roll(fi + ti, 1, 0)
        cr = jnp.where(row == seg, sr, cr)
        ci = jnp.where(row == seg, si, ci)

    def fix(t, c):
        r = pl.multiple_of(t * N_SEG, N_SEG)
        pr, pi = _cmul(lr, li, c[0], c[1])
        ar, ai = _cmul(pr, pi, cr, ci)
        hr_ref[pl.ds(r, N_SEG), :] += ar
        hi_ref[pl.ds(r, N_SEG), :] += ai
        return pr, pi

    lax.fori_loop(0, n_steps, fix, (jnp.ones((N_SEG, W), F32), zero))


def _cpow(lr, li, n):
    rr, ri = None, None
    br, bi = lr, li
    while n:
        if n & 1:
            rr, ri = (br, bi) if rr is None else _cmul(rr, ri, br, bi)
        n >>= 1
        if n:
            br, bi = _cmul(br, bi, br, bi)
    return rr, ri


def _ssm_specs(L):
    col = lambda w: pl.BlockSpec((L, w), lambda j: (0, j))
    return dict(
        u=col(128), lam=pl.BlockSpec((2, 512), lambda j: (0, j)),
        bmat=pl.BlockSpec((1, 128, 512), lambda j: (j, 0, 0)), cmat=pl.BlockSpec((1, 512, 128), lambda j: (j, 0, 0)),
        d=pl.BlockSpec((1, 128), lambda j: (0, j)))


def ssm_fwd(us, lam, bre, bim, cre, cim, dvec):
    L = us.shape[0]
    n_steps = L // N_SEG
    sp = _ssm_specs(L)

    def body(u_ref, lam_ref, bre_ref, bim_ref, cre_ref, cim_ref, d_ref, y_ref, hr, hi):
        u = u_ref[...]
        ub = u.astype(BF16)
        hr[...] = _dot(ub, bre_ref[0], 1, 0)
        hi[...] = _dot(ub, bim_ref[0], 1, 0)
        lr = jnp.broadcast_to(lam_ref[0:1, :], (N_SEG, 512))
        li = jnp.broadcast_to(lam_ref[1:2, :], (N_SEG, 512))
        _scan_fwd(hr, hi, lr, li, n_steps)
        y_ref[...] = (_dot(hr[...].astype(BF16), cre_ref[0], 1, 0) - _dot(hi[...].astype(BF16), cim_ref[0], 1, 0)
                      + d_ref[...] * u)

    return pl.pallas_call(
        body, name="ssm_fwd", grid=(4,),
        in_specs=[sp["u"], sp["lam"], sp["bmat"], sp["bmat"], sp["cmat"], sp["cmat"], sp["d"]],
        out_specs=sp["u"], out_shape=S((L, 512), F32),
        scratch_shapes=[pltpu.VMEM((L, 512), F32), pltpu.VMEM((L, 512), F32)],
        compiler_params=_params("parallel"),
    )(us, lam, bre, bim, cre, cim, dvec)


_GELU_C = math.sqrt(2.0 / math.pi)


def _gelu(y):
    t = jnp.tanh(_GELU_C * (y + 0.044715 * y * y * y))
    return 0.5 * y * (1.0 + t), t


def glu_fwd(y, w, b, gnorm):
    L, C = y.shape
    tm = _tile(L, 512)

    def body(y_ref, w_ref, b_ref, g_ref, o_ref):
        zg, _ = _gelu(y_ref[...])
        out = zg * _sigmoid(_dot(zg.astype(BF16), w_ref[...], 1, 0) + b_ref[...])
        o_ref[...] = (out * _rstd(out) * g_ref[...]).astype(BF16)

    row = pl.BlockSpec((1, C), lambda m: (0, 0))
    return pl.pallas_call(
        body, name="glu_fwd", grid=(L // tm,),
        in_specs=[pl.BlockSpec((tm, C), lambda m: (m, 0)), pl.BlockSpec((C, C), lambda m: (0, 0)), row, row],
        out_specs=pl.BlockSpec((tm, C), lambda m: (m, 0)),
        out_shape=S((L, C), BF16), compiler_params=_params("parallel"),
    )(y, w, b.reshape(1, C), gnorm.reshape(1, C))


def _ple_specs(L, D, P, tm, nb):
    return [pl.BlockSpec((tm, D), lambda n, m: (m, 0)), pl.BlockSpec((tm, P), lambda n, m: (m, 0)),
            pl.BlockSpec((D, nb), lambda n, m: (0, n)), pl.BlockSpec((1, P, nb), lambda n, m: (n, 0, 0)),
            pl.BlockSpec((tm, nb), lambda n, m: (m, n))]


def ple_fwd(un, pb, wpg, wpp, h):
    L, D = un.shape
    ns, P, nb = wpp.shape
    tm = _tile(L, 512)

    def body(un_ref, p_ref, wg_ref, wp_ref, h_ref, o_ref):
        gate = _sigmoid(_dot(un_ref[...], wg_ref[...], 1, 0))
        o_ref[...] = h_ref[...] + _dot(p_ref[...], wp_ref[0], 1, 0) * gate

    return pl.pallas_call(
        body, name="ple_fwd", grid=(ns, L // tm),
        in_specs=_ple_specs(L, D, P, tm, nb),
        out_specs=pl.BlockSpec((tm, nb), lambda n, m: (m, n)),
        out_shape=S((L, D), F32), compiler_params=_params("parallel", "parallel"),
    )(un, pb, wpg, wpp, h)


def loss_head(h, g, target):
    L, D = h.shape
    tm = _tile(L, 256)

    def body(h_ref, g_ref, t_ref, loss_ref, dh_ref, dg_ref):
        m = pl.program_id(0)
        x = h_ref[...]
        gg = g_ref[...]
        e = x * _rstd(x) * gg - t_ref[...]
        dx, dg = _rms_bwd(x, gg, e * (1.0 / D))
        dh_ref[...] = dx
        part = jnp.full((8, 128), 0.5 / D, F32) * jnp.sum(e * e)

        @pl.when(m == 0)
        def _():
            loss_ref[...] = part
            dg_ref[...] = dg

        @pl.when(m > 0)
        def _():
            loss_ref[...] += part
            dg_ref[...] += dg

    return pl.pallas_call(
        body, name="loss_head", grid=(L // tm,),
        in_specs=[pl.BlockSpec((tm, D), lambda m: (m, 0)), pl.BlockSpec((1, D), lambda m: (0, 0)),
                  pl.BlockSpec((tm, D), lambda m: (m, 0))],
        out_specs=[pl.BlockSpec((8, 128), lambda m: (0, 0)), pl.BlockSpec((tm, D), lambda m: (m, 0)),
                   pl.BlockSpec((1, D), lambda m: (0, 0))],
        out_shape=[S((8, 128), F32), S((L, D), F32), S((1, D), F32)],
        compiler_params=_params("arbitrary"),
    )(h, g.reshape(1, D), target)


def ple_bwd(un, pb, wpg, wpp, dh):
    L, D = un.shape
    ns, P, nb = wpp.shape
    tm = _tile(L, 512)

    def body(un_ref, p_ref, wg_ref, wp_ref, dh_ref, dpre_ref, dpp_ref):
        gate = _sigmoid(_dot(un_ref[...], wg_ref[...], 1, 0))
        pp = _dot(p_ref[...], wp_ref[0], 1, 0)
        d = dh_ref[...]
        dpp_ref[0] = (d * gate).astype(BF16)
        dpre_ref[...] = (d * pp * gate * (1.0 - gate)).astype(BF16)

    return pl.pallas_call(
        body, name="ple_bwd", grid=(ns, L // tm),
        in_specs=_ple_specs(L, D, P, tm, nb),
        out_specs=[pl.BlockSpec((tm, nb), lambda n, m: (m, n)), pl.BlockSpec((1, tm, nb), lambda n, m: (n, m, 0))],
        out_shape=[S((L, D), BF16), S((ns, L, nb), BF16)], compiler_params=_params("parallel", "parallel"),
    )(un, pb, wpg, wpp, dh)


def wgrad(a, b):
    a3 = a if a.ndim == 3 else a[None]
    b3 = b if b.ndim == 3 else b[None]
    ns = max(a3.shape[0], b3.shape[0])
    _, L, Ka = a3.shape
    N = b3.shape[2]
    a_map = (lambda s: (s, 0, 0)) if a3.shape[0] > 1 else (lambda s: (0, 0, 0))
    b_map = (lambda s: (s, 0, 0)) if b3.shape[0] > 1 else (lambda s: (0, 0, 0))

    def body(a_ref, b_ref, o_ref):
        o_ref[0] = _dot(a_ref[0], b_ref[0], 0, 0)

    return pl.pallas_call(
        body, name="wgrad", grid=(ns,),
        in_specs=[pl.BlockSpec((1, L, Ka), a_map), pl.BlockSpec((1, L, N), b_map)],
        out_specs=pl.BlockSpec((1, Ka, N), lambda s: (s, 0, 0)),
        out_shape=S((ns, Ka, N), F32), compiler_params=_params("parallel"),
    )(a3, b3)


def dx_rms(pairs, h, g, dh_in, cast_scale):
    L, D = h.shape
    nk = pairs[0][0].shape[0]
    n_pairs = len(pairs)
    tm = _tile(L, 512)
    n_m = L // tm

    def body(*refs):
        ins, (h_ref, g_ref, dhi_ref, dho_ref, dhb_ref, dg_ref, acc) = refs[:2 * n_pairs], refs[2 * n_pairs:]
        m, k = pl.program_id(0), pl.program_id(1)
        part = _dot(ins[0][0], ins[1][0], 1, 1)
        for i in range(1, n_pairs):
            part += _dot(ins[2 * i][0], ins[2 * i + 1][0], 1, 1)

        @pl.when(k == 0)
        def _():
            acc[...] = part

        @pl.when(k > 0)
        def _():
            acc[...] += part

        @pl.when(k == nk - 1)
        def _():
            dx, dg = _rms_bwd(h_ref[...], g_ref[...], acc[...])
            dh_out = dhi_ref[...] + dx
            dho_ref[...] = dh_out
            dhb_ref[...] = (cast_scale * dh_out).astype(BF16)

            @pl.when(m == 0)
            def _():
                dg_ref[...] = dg

            @pl.when(m > 0)
            def _():
                dg_ref[...] += dg

    in_specs, args = [], []
    for a3, w3 in pairs:
        Kc = a3.shape[2]
        in_specs += [pl.BlockSpec((1, tm, Kc), lambda m, k: (k, m, 0)), pl.BlockSpec((1, D, Kc), lambda m, k: (k, 0, 0))]
        args += [a3, w3]
    tile = pl.BlockSpec((tm, D), lambda m, k: (m, 0))
    row = pl.BlockSpec((1, D), lambda m, k: (0, 0))
    return pl.pallas_call(
        body, name="dx_rms", grid=(n_m, nk),
        in_specs=in_specs + [tile, row, tile], out_specs=[tile, tile, row],
        out_shape=[S((L, D), F32), S((L, D), BF16), S((1, D), F32)], scratch_shapes=[pltpu.VMEM((tm, D), F32)],
        compiler_params=_params("arbitrary", "arbitrary"),
    )(*args, h, g.reshape(1, D), dh_in)


def dact_plain(dhb, w3):
    L, D = dhb.shape
    ns, N, _ = w3.shape
    tm = _tile(L, 512)

    def body(d_ref, w_ref, o_ref):
        o_ref[0] = _dot(d_ref[...], w_ref[0], 1, 1)

    return pl.pallas_call(
        body, name="dact_plain", grid=(ns, L // tm),
        in_specs=[pl.BlockSpec((tm, D), lambda s, m: (m, 0)), pl.BlockSpec((1, N, D), lambda s, m: (s, 0, 0))],
        out_specs=pl.BlockSpec((1, tm, N), lambda s, m: (s, m, 0)),
        out_shape=S((ns, L, N), F32), compiler_params=_params("parallel", "parallel"),
    )(dhb, w3)


def dact_swiglu(dhb, wd, a3, b3):
    L, D = dhb.shape
    ns, F, _ = wd.shape
    tm = _tile(L, 512)

    def body(d_ref, w_ref, a_ref, b_ref, da_ref, db_ref):
        ds = _dot(d_ref[...], w_ref[0], 1, 1)
        a = a_ref[0].astype(F32)
        b = b_ref[0].astype(F32)
        sg = _sigmoid(a)
        da_ref[0] = (ds * b * (sg * (1.0 + a * (1.0 - sg)))).astype(BF16)
        db_ref[0] = (ds * (a * sg)).astype(BF16)

    t_spec = pl.BlockSpec((1, tm, F), lambda s, m: (s, m, 0))
    return pl.pallas_call(
        body, name="dact_swiglu", grid=(ns, L // tm),
        in_specs=[pl.BlockSpec((tm, D), lambda s, m: (m, 0)), pl.BlockSpec((1, F, D), lambda s, m: (s, 0, 0)), t_spec, t_spec],
        out_specs=[t_spec, t_spec], out_shape=[S((ns, L, F), BF16)] * 2,
        compiler_params=_params("parallel", "parallel"),
    )(dhb, wd, a3, b3)


def conv_bwd(z, conv_w, conv_b, gnorm, dyn):
    _, L, C = z.shape
    tm = _tile(L, 256)
    H = CONV_HALO
    T = tm + 2 * H

    def body(zb_ref, zbp_ref, zbn_ref, zc_ref, zcp_ref, zcn_ref, zv_ref, zvp_ref, zvn_ref, d_ref, dp_ref, dn_ref,
             w_ref, b_ref, g_ref, dz_ref, dw_ref, db_ref, dg_ref):
        m = pl.program_id(0)
        cat = lambda p, c, n: jnp.concatenate([p[0], c[0], n[0]], axis=0)
        zb, zc, zv, d = cat(zbp_ref, zb_ref, zbn_ref), cat(zcp_ref, zc_ref, zcn_ref), cat(zvp_ref, zv_ref, zvn_ref), cat(dp_ref, d_ref, dn_ref)
        grow = m * tm - H + lax.broadcasted_iota(jnp.int32, (T, C), 0)
        valid, v, v1, v2, cb, ya = _conv_core(zb, zc, zv, w_ref, b_ref[...], grow, L)
        dya, _ = _rms_bwd(ya, g_ref[...], d)
        dc = jnp.where(valid, dya * zb, 0.0)
        dv = w_ref[2:3, :] * dc + w_ref[1:2, :] * pltpu.roll(dc, T - 1, 0) + w_ref[0:1, :] * pltpu.roll(dc, T - 2, 0)
        dz_ref[0] = (dya * cb)[H:H + tm, :].astype(BF16)
        dz_ref[1] = (dv * zv)[H:H + tm, :].astype(BF16)
        dz_ref[2] = (dv * zc)[H:H + tm, :].astype(BF16)
        rs = lambda x: jnp.sum(x[H:H + tm, :], axis=0, keepdims=True)
        yh = ya * _rstd(ya)
        dw = jnp.concatenate([rs(dc * v2), rs(dc * v1), rs(dc * v)], axis=0)
        dbias, dg = rs(dc), rs(d * yh)

        @pl.when(m == 0)
        def _():
            dw_ref[...] = dw
            db_ref[...] = dbias
            dg_ref[...] = dg

        @pl.when(m > 0)
        def _():
            dw_ref[...] += dw
            db_ref[...] += dbias
            dg_ref[...] += dg

    row = lambda r: pl.BlockSpec((r, C), lambda m: (0, 0))
    specs = [*_conv_specs(L, tm, C, 0), *_conv_specs(L, tm, C, 1), *_conv_specs(L, tm, C, 2), *_conv_specs(L, tm, C, 0)]
    return pl.pallas_call(
        body, name="conv_bwd", grid=(L // tm,),
        in_specs=specs + [row(3), row(1), row(1)],
        out_specs=[pl.BlockSpec((3, tm, C), lambda m: (0, m, 0)), row(3), row(1), row(1)],
        out_shape=[S((3, L, C), BF16), S((3, C), F32), S((1, C), F32), S((1, C), F32)],
        compiler_params=_params("arbitrary"),
    )(z, z, z, z, z, z, z, z, z, dyn, dyn, dyn, conv_w, conv_b.reshape(1, C), gnorm.reshape(1, C))


def glu_bwd(y, w, b, gnorm, dn):
    L, C = y.shape
    tm = _tile(L, 256)

    def body(y_ref, w_ref, b_ref, g_ref, d_ref, dy_ref, dpre_ref, zg_ref, db_ref, dg_ref):
        m = pl.program_id(0)
        yv = y_ref[...]
        zg, t = _gelu(yv)
        zgb = zg.astype(BF16)
        sg = _sigmoid(_dot(zgb, w_ref[...], 1, 0) + b_ref[...])
        out = zg * sg
        dout, dg = _rms_bwd(out, g_ref[...], d_ref[...])
        dpre = dout * zg * sg * (1.0 - sg)
        dpre_b = dpre.astype(BF16)
        dzg = dout * sg + _dot(dpre_b, w_ref[...], 1, 1)
        dt = (1.0 - t * t) * _GELU_C * (1.0 + 3.0 * 0.044715 * yv * yv)
        dy_ref[...] = dzg * (0.5 * (1.0 + t) + 0.5 * yv * dt)
        dpre_ref[...] = dpre_b
        zg_ref[...] = zgb
        dbias = jnp.sum(dpre, axis=0, keepdims=True)

        @pl.when(m == 0)
        def _():
            db_ref[...] = dbias
            dg_ref[...] = dg

        @pl.when(m > 0)
        def _():
            db_ref[...] += dbias
            dg_ref[...] += dg

    tile = pl.BlockSpec((tm, C), lambda m: (m, 0))
    row = pl.BlockSpec((1, C), lambda m: (0, 0))
    return pl.pallas_call(
        body, name="glu_bwd", grid=(L // tm,),
        in_specs=[tile, pl.BlockSpec((C, C), lambda m: (0, 0)), row, row, tile],
        out_specs=[tile, tile, tile, row, row],
        out_shape=[S((L, C), F32), S((L, C), BF16), S((L, C), BF16), S((1, C), F32), S((1, C), F32)],
        compiler_params=_params("arbitrary"),
    )(y, w, b.reshape(1, C), gnorm.reshape(1, C), dn)


def _scan_bwd(gr_ref, gi_ref, hr_ref, hi_ref, lr, li, n_steps):
    W = gr_ref.shape[1]
    zero = jnp.zeros((N_SEG, W), F32)
    lic = -li

    def local(i, c):
        r = pl.multiple_of((n_steps - 1 - i) * N_SEG, N_SEG)
        pr, pi = _cmul(lr, lic, c[0], c[1])
        nr = pr + gr_ref[pl.ds(r, N_SEG), :]
        ni = pi + gi_ref[pl.ds(r, N_SEG), :]
        gr_ref[pl.ds(r, N_SEG), :] = nr
        gi_ref[pl.ds(r, N_SEG), :] = ni
        return nr, ni

    fr, fi = lax.fori_loop(0, n_steps, local, (zero, zero))
    qr, qi = _cpow(lr, lic, n_steps)
    row = lax.broadcasted_iota(jnp.int32, (N_SEG, W), 0)
    cr, ci = zero, zero
    for seg in range(N_SEG - 2, -1, -1):
        tr, ti = _cmul(qr, qi, cr, ci)
        sr = pltpu.roll(fr + tr, N_SEG - 1, 0)
        si = pltpu.roll(fi + ti, N_SEG - 1, 0)
        cr = jnp.where(row == seg, sr, cr)
        ci = jnp.where(row == seg, si, ci)

    def fix(i, c):
        pwr, pwi, ar, ai = c
        t = n_steps - 1 - i
        r = pl.multiple_of(t * N_SEG, N_SEG)
        pwr, pwi = _cmul(lr, lic, pwr, pwi)
        xr, xi = _cmul(pwr, pwi, cr, ci)
        g_r = gr_ref[pl.ds(r, N_SEG), :] + xr
        g_i = gi_ref[pl.ds(r, N_SEG), :] + xi
        gr_ref[pl.ds(r, N_SEG), :] = g_r
        gi_ref[pl.ds(r, N_SEG), :] = g_i
        rp = pl.multiple_of(jnp.maximum(t - 1, 0) * N_SEG, N_SEG)
        hpr = hr_ref[pl.ds(rp, N_SEG), :]
        hpi = hi_ref[pl.ds(rp, N_SEG), :]
        live = t > 0
        ar = ar + jnp.where(live, hpr * g_r + hpi * g_i, 0.0)
        ai = ai + jnp.where(live, hpr * g_i - hpi * g_r, 0.0)
        return pwr, pwi, ar, ai

    _, _, ar, ai = lax.fori_loop(0, n_steps, fix, (jnp.ones((N_SEG, W), F32), zero, zero, zero))
    last = pl.ds((n_steps - 1) * N_SEG, N_SEG)
    hpr = jnp.where(row == 0, 0.0, pltpu.roll(hr_ref[last, :], 1, 0))
    hpi = jnp.where(row == 0, 0.0, pltpu.roll(hi_ref[last, :], 1, 0))
    g_r, g_i = gr_ref[pl.ds(0, N_SEG), :], gi_ref[pl.ds(0, N_SEG), :]
    ar = ar + hpr * g_r + hpi * g_i
    ai = ai + hpr * g_i - hpi * g_r
    return jnp.sum(ar, axis=0, keepdims=True), jnp.sum(ai, axis=0, keepdims=True)


def ssm_bwd(us, dy, lam, bre, bim, cre, cim, dvec):
    L = us.shape[0]
    n_steps = L // N_SEG
    sp = _ssm_specs(L)

    def body(u_ref, dy_ref, lam_ref, bre_ref, bim_ref, cre_ref, cim_ref, d_ref,
             du_ref, dlam_ref, dbre_ref, dbim_ref, dcre_ref, dcim_ref, dd_ref, hr, hi, gr, gi):
        u = u_ref[...]
        ub = u.astype(BF16)
        dyv = dy_ref[...]
        dyb = dyv.astype(BF16)
        hr[...] = _dot(ub, bre_ref[0], 1, 0)
        hi[...] = _dot(ub, bim_ref[0], 1, 0)
        lr = jnp.broadcast_to(lam_ref[0:1, :], (N_SEG, 512))
        li = jnp.broadcast_to(lam_ref[1:2, :], (N_SEG, 512))
        _scan_fwd(hr, hi, lr, li, n_steps)
        dcre_ref[0] = _dot(hr[...].astype(BF16), dyb, 0, 0)
        dcim_ref[0] = -_dot(hi[...].astype(BF16), dyb, 0, 0)
        gr[...] = _dot(dyb, cre_ref[0], 1, 1)
        gi[...] = -_dot(dyb, cim_ref[0], 1, 1)
        dlr, dli = _scan_bwd(gr, gi, hr, hi, lr, li, n_steps)
        dlam_ref[...] = jnp.concatenate([dlr, dli], axis=0)
        grb, gib = gr[...].astype(BF16), gi[...].astype(BF16)
        du_ref[...] = _dot(grb, bre_ref[0], 1, 1) + _dot(gib, bim_ref[0], 1, 1) + d_ref[...] * dyv
        dbre_ref[0] = _dot(ub, grb, 0, 0)
        dbim_ref[0] = _dot(ub, gib, 0, 0)
        dd_ref[...] = jnp.sum(dyv * u, axis=0, keepdims=True)

    big = pltpu.VMEM((L, 512), F32)
    return pl.pallas_call(
        body, name="ssm_bwd", grid=(4,),
        in_specs=[sp["u"], sp["u"], sp["lam"], sp["bmat"], sp["bmat"], sp["cmat"], sp["cmat"], sp["d"]],
        out_specs=[sp["u"], sp["lam"], sp["bmat"], sp["bmat"], sp["cmat"], sp["cmat"], sp["d"]],
        out_shape=[S((L, 512), F32), S((2, 2048), F32), S((4, 128, 512), F32), S((4, 128, 512), F32),
                   S((4, 512, 128), F32), S((4, 512, 128), F32), S((1, 512), F32)],
        scratch_shapes=[big, big, big, big], compiler_params=_params("parallel"),
    )(us, dy, lam, bre, bim, cre, cim, dvec)


def _discretize(ar, ai, log_dt, br, bi):
    dt = jnp.exp(log_dt)
    mag = jnp.exp(ar * dt)
    ph = ai * dt
    lr, li = mag * jnp.cos(ph), mag * jnp.sin(ph)
    nr, ni = lr - 1.0, li
    den = ar * ar + ai * ai
    fr = (nr * ar + ni * ai) / den
    fi = (ni * ar - nr * ai) / den
    return lr, li, fr[..., None] * br - fi[..., None] * bi, fr[..., None] * bi + fi[..., None] * br


def ssm_prep(ar, ai, log_dt, br, bi):
    G, P, H = br.shape

    def body(ar_ref, ai_ref, dt_ref, br_ref, bi_ref, lr_ref, li_ref, bbr_ref, bbi_ref):
        lr_ref[...], li_ref[...], bbr_ref[...], bbi_ref[...] = _discretize(
            ar_ref[...], ai_ref[...], dt_ref[...], br_ref[...], bi_ref[...])

    return pl.pallas_call(
        body, name="ssm_prep",
        out_shape=[S((G, P), F32), S((G, P), F32), S((G, P, H), F32), S((G, P, H), F32)],
    )(ar, ai, log_dt.reshape(G, 1), br, bi)


def ssm_prep_bwd(ar, ai, log_dt, br, bi, dlr, dli, dbbr, dbbi):
    G, P, H = br.shape

    def body(ar_ref, ai_ref, dt_ref, br_ref, bi_ref, dlr_ref, dli_ref, dbbr_ref, dbbi_ref,
             dar_ref, dai_ref, ddt_ref, dbr_ref, dbi_ref):
        _, vjp = jax.vjp(_discretize, ar_ref[...], ai_ref[...], dt_ref[...], br_ref[...], bi_ref[...])
        dar_ref[...], dai_ref[...], ddt_ref[...], dbr_ref[...], dbi_ref[...] = vjp(
            (dlr_ref[...], dli_ref[...], dbbr_ref[...], dbbi_ref[...]))

    return pl.pallas_call(
        body, name="ssm_prep_bwd",
        out_shape=[S((G, P), F32), S((G, P), F32), S((G, 1), F32), S((G, P, H), F32), S((G, P, H), F32)],
    )(ar, ai, log_dt.reshape(G, 1), br, bi, dlr, dli, dbbr, dbbi)


def _block_diag(x):
    j, n, R, C = x.shape
    eye = jnp.eye(n, dtype=x.dtype)
    return (x[:, :, :, None, :] * eye[None, :, None, :, None]).reshape(j, n * R, n * C)


def _block_diag_take(x, R, C):
    j = x.shape[0]
    n = x.shape[1] // R
    x5 = x.reshape(j, n, R, n, C)
    return jnp.stack([x5[:, i, :, i, :] for i in range(n)], axis=1)


def _to_segments(x):
    L, C = x.shape
    return x.reshape(N_SEG, L // N_SEG, C).transpose(1, 0, 2).reshape(L, C)


def _from_segments(x):
    L, C = x.shape
    return x.reshape(L // N_SEG, N_SEG, C).transpose(1, 0, 2).reshape(L, C)


BIG = ("ffn1_w_gate", "ffn1_w_up", "ffn1_w_down", "w_in", "glu_w", "w_out",
       "ffn2_w_gate", "ffn2_w_up", "ffn2_w_down", "ple_w_gate", "ple_w_proj")
SMALL = ("ffn1_norm", "mix_norm", "conv_w", "conv_b", "ssm_A_re", "ssm_A_im", "ssm_B_re", "ssm_B_im", "ssm_C_re", "ssm_C_im",
         "ssm_D", "ssm_log_dt", "glu_b", "conv_out_norm", "ssm_out_norm", "ffn2_norm", "ple_norm")


def _ssm_mats(w):
    G, P, H = w["ssm_B_re"].shape
    lr, li, bbr, bbi = ssm_prep(w["ssm_A_re"], w["ssm_A_im"], w["ssm_log_dt"], w["ssm_B_re"], w["ssm_B_im"])
    lam = jnp.stack([lr.reshape(G * P), li.reshape(G * P)])
    bmat = lambda bb: _block_diag(bb.reshape(4, G // 4, P, H).transpose(0, 1, 3, 2)).astype(BF16)
    cmat = lambda c: _block_diag(c.reshape(4, G // 4, H, P).transpose(0, 1, 3, 2)).astype(BF16)
    return lam, bmat(bbr), bmat(bbi), cmat(w["ssm_C_re"]), cmat(w["ssm_C_im"]), w["ssm_D"].reshape(1, G * H)


def layer_fwd(h0, pb, w):
    L, D = h0.shape
    u1 = rmsnorm_fwd(h0, w["ffn1_norm"])
    a1, b1, s1 = ffn_up(u1, w["ffn1_w_gate"], w["ffn1_w_up"])
    h1, u2 = mm_shard_k(s1, w["ffn1_w_down"], h0, 0.5, w["mix_norm"])
    z = mm_shard_n(u2, w["w_in"], F32)
    ya_n = conv_fwd(z, w["conv_w"], w["conv_b"], w["conv_out_norm"])
    us = _to_segments(z[3])
    mats = _ssm_mats(w)
    y = ssm_fwd(us, *mats)
    ys_n = glu_fwd(y, w["glu_w"], w["glu_b"], w["ssm_out_norm"])
    ycat = jnp.stack([ya_n, _from_segments(ys_n)])
    h2, u3 = mm_shard_k(ycat, w["w_out"], h1, 1.0, w["ffn2_norm"])
    a2, b2, s2 = ffn_up(u3, w["ffn2_w_gate"], w["ffn2_w_up"])
    h3, un = mm_shard_k(s2, w["ffn2_w_down"], h2, 0.5, w["ple_norm"])
    h4 = ple_fwd(un, pb, w["ple_w_gate"], w["ple_w_proj"], h3)
    saved = dict(h0=h0, u1=u1, a1=a1, b1=b1, s1=s1, h1=h1, u2=u2, z=z, us=us, mats=mats, y=y, ycat=ycat,
                 h2=h2, u3=u3, a2=a2, b2=b2, s2=s2, h3=h3, un=un)
    return h4, saved


def _ffn_bwd(dh, dhb, h_in, u, a, b, s, wg, wu, wd, gnorm, cast_scale):
    da, db = dact_swiglu(dhb, wd, a, b)
    g_wd = wgrad(s, dhb)
    g_wg = wgrad(u, da)
    g_wu = wgrad(u, db)
    dh_in, dhb_in, g_norm = dx_rms([(da, wg), (db, wu)], h_in, gnorm, dh, cast_scale)
    return dh_in, dhb_in, g_wg, g_wu, g_wd, g_norm


def layer_bwd(dh, pb, w, sv):
    L, D = dh.shape
    G, P, H = w["ssm_B_re"].shape
    dpre, dpp3 = ple_bwd(sv["un"], pb, w["ple_w_gate"], w["ple_w_proj"], dh)
    g_wpg = wgrad(sv["un"], dpre).reshape(N_SHARD, D // N_SHARD, D)
    g_wpp = wgrad(pb, dpp3)
    dh3, dhb3, g_nple = dx_rms([(dpre[None], w["ple_w_gate"][None])], sv["h3"], w["ple_norm"], dh, 0.5)
    dh2, dhb, g_wg2, g_wu2, g_wd2, g_nffn2 = _ffn_bwd(dh3, dhb3, sv["h2"], sv["u3"], sv["a2"], sv["b2"], sv["s2"],
                                                      w["ffn2_w_gate"], w["ffn2_w_up"], w["ffn2_w_down"], w["ffn2_norm"], 1.0)
    dyn = dact_plain(dhb, w["w_out"])
    g_wout = wgrad(sv["ycat"], dhb).reshape(N_SHARD, -1, D)
    dz_abc, g_convw, g_convb, g_nconv = conv_bwd(sv["z"], w["conv_w"], w["conv_b"], w["conv_out_norm"], dyn)
    dy, dpre_g, zg, g_glub, g_nssm = glu_bwd(sv["y"], w["glu_w"], w["glu_b"], w["ssm_out_norm"], _to_segments(dyn[1]))
    C = zg.shape[1]
    g_gluw = wgrad(zg, dpre_g).reshape(N_SHARD, C // N_SHARD, C)
    dus, dlam, dbre, dbim, dcre, dcim, dd = ssm_bwd(sv["us"], dy, *sv["mats"])
    take_b = lambda m: _block_diag_take(m, H, P).transpose(0, 1, 3, 2).reshape(G, P, H)
    take_c = lambda m: _block_diag_take(m, P, H).transpose(0, 1, 3, 2).reshape(G, H, P)
    g_ar, g_ai, g_dt, g_br, g_bi = ssm_prep_bwd(
        w["ssm_A_re"], w["ssm_A_im"], w["ssm_log_dt"], w["ssm_B_re"], w["ssm_B_im"],
        dlam[0].reshape(G, P), dlam[1].reshape(G, P), take_b(dbre), take_b(dbim))
    dz3 = jnp.concatenate([dz_abc, _from_segments(dus).astype(BF16)[None]], axis=0)
    g_win = wgrad(sv["u2"], dz3)
    dh1, dhb1, g_nmix = dx_rms([(dz3, w["w_in"])], sv["h1"], w["mix_norm"], dh2, 0.5)
    dh0, _, g_wg1, g_wu1, g_wd1, g_nffn1 = _ffn_bwd(dh1, dhb1, sv["h0"], sv["u1"], sv["a1"], sv["b1"], sv["s1"],
                                                    w["ffn1_w_gate"], w["ffn1_w_up"], w["ffn1_w_down"], w["ffn1_norm"], 1.0)
    big = [g_wg1, g_wu1, g_wd1, g_win, g_gluw, g_wout, g_wg2, g_wu2, g_wd2, g_wpg, g_wpp]
    small = dict(ffn1_norm=g_nffn1, mix_norm=g_nmix, conv_w=g_convw, conv_b=g_convb, ssm_A_re=g_ar, ssm_A_im=g_ai,
                 ssm_B_re=g_br, ssm_B_im=g_bi, ssm_C_re=take_c(dcre), ssm_C_im=take_c(dcim), ssm_D=dd,
                 ssm_log_dt=g_dt, glu_b=g_glub, conv_out_norm=g_nconv, ssm_out_norm=g_nssm, ffn2_norm=g_nffn2,
                 ple_norm=g_nple)
    return dh0, big, small


def local_step(x, p, target, layers, final_norm):
    h = x
    saved = []
    pbs = []
    for i, w in enumerate(layers):
        pb = p[i].astype(BF16)
        h, sv = layer_fwd(h, pb, w)
        saved.append(sv)
        pbs.append(pb)
    loss_part, dh, g_final = loss_head(h, final_norm, target)
    bigs, smalls = [None] * len(layers), [None] * len(layers)
    for i in reversed(range(len(layers))):
        dh, bigs[i], smalls[i] = layer_bwd(dh, pbs[i], layers[i], saved[i])
    return loss_part, dh, bigs, smalls, g_final


def _row_tile(rows):
    for t in (512, 352, 256, 128, 64, 32, 16):
        if rows % t == 0:
            return t
    return rows


def elementwise(fn, ins, out_dtypes, name):
    rows, cols = ins[0].shape
    tr = _row_tile(rows)
    n_in = len(ins)

    def body(*refs):
        outs = fn(*[r[...] for r in refs[:n_in]])
        for o_ref, o in zip(refs[n_in:], outs):
            o_ref[...] = o.astype(o_ref.dtype)

    spec = pl.BlockSpec((tr, cols), lambda i: (i, 0))
    return pl.pallas_call(
        body, name=name, grid=(rows // tr,), in_specs=[spec] * n_in, out_specs=[spec] * len(out_dtypes),
        out_shape=[S((rows, cols), d) for d in out_dtypes], compiler_params=_params("parallel"),
    )(*ins)


def _rows(a):
    return a.reshape(-1, a.shape[-1])


def _adamw(w, g, m, v):
    m = ADAM_B1 * m + (1.0 - ADAM_B1) * g
    v = ADAM_B2 * v + (1.0 - ADAM_B2) * (g * g)
    m_hat = m / (1.0 - ADAM_B1 ** ADAM_STEP)
    v_hat = v / (1.0 - ADAM_B2 ** ADAM_STEP)
    delta = -ADAM_LR * (m_hat / (jnp.sqrt(v_hat) + ADAM_EPS) + ADAM_WD * w)
    return delta, m, v


ANY = pl.BlockSpec(memory_space=pl.ANY)


def _mesh_pos():
    return lax.axis_index("x"), lax.axis_index("y"), lax.axis_index("c")


def _other_chips(x, y):
    return [(1 - x, y), (x, 1 - y), (1 - x, 1 - y)]


def _remote(src, dst, send_sem, recv_sem, device):
    return pltpu.make_async_remote_copy(src_ref=src, dst_ref=dst, send_sem=send_sem, recv_sem=recv_sem,
                                        device_id=device, device_id_type=MESH)


def gather_weights(ws):
    n = len(ws)

    def body(*refs):
        outs = refs[n:2 * n]
        send_sems, recv_sems = refs[2 * n:]
        x, y, c = _mesh_pos()
        me_s = 2 * x + y
        sibling = (x, y, 1 - c)
        chips = _other_chips(x, y)
        n_half = outs[0].shape[0] // 2
        mine, other = pl.ds(c * n_half, n_half), pl.ds((1 - c) * n_half, n_half)
        sent = []
        for t in range(n):
            for j, (cx, cy) in enumerate(chips):
                blk = outs[t].at[mine, me_s]
                cp = _remote(blk, blk, send_sems.at[t, j], recv_sems.at[t, j], (cx, cy, c))
                cp.start()
                sent.append(cp)
        for j, (cx, cy) in enumerate(chips):
            for t in range(n):
                blk = outs[t].at[mine, 2 * cx + cy]
                _remote(blk, blk, send_sems.at[t, j], recv_sems.at[t, j], (cx, cy, c)).wait_recv()
                cp = _remote(blk, blk, send_sems.at[t, 3 + j], recv_sems.at[t, 3 + j], sibling)
                cp.start()
                sent.append(cp)
        for j, (cx, cy) in enumerate(chips):
            for t in range(n):
                blk = outs[t].at[other, 2 * cx + cy]
                _remote(blk, blk, send_sems.at[t, 3 + j], recv_sems.at[t, 3 + j], sibling).wait_recv()
        for cp in sent:
            cp.wait_send()

    return pl.pallas_call(
        body, name="gather_weights", in_specs=[ANY] * n, out_specs=[ANY] * n,
        out_shape=[S(w.shape, w.dtype) for w in ws], input_output_aliases={t: t for t in range(n)},
        scratch_shapes=[pltpu.SemaphoreType.DMA((n, 6)), pltpu.SemaphoreType.DMA((n, 6))],
    )(*ws)


def cast_place(w, pos, dtype):
    layers, r, c = w.shape
    tr = _row_tile(r)

    def body(pos_ref, w_ref, o_ref):
        o_ref[0, 0] = w_ref[0].astype(dtype)

    return pl.pallas_call(
        body, name="cast_place",
        grid_spec=pltpu.PrefetchScalarGridSpec(
            num_scalar_prefetch=1, grid=(layers, r // tr),
            in_specs=[pl.BlockSpec((1, tr, c), lambda l, i, pos: (l, i, 0))],
            out_specs=pl.BlockSpec((1, 1, tr, c), lambda l, i, pos: (l, pos[1], i, 0))),
        out_shape=S((layers, N_SHARD, r, c), dtype), compiler_params=_params("parallel", "parallel"),
    )(pos, w)


def reduce_pair(gs):
    n = len(gs)

    def body(*refs):
        ins, got = refs[:n], refs[n:2 * n]
        send_sems, recv_sems = refs[2 * n:]
        x, y, c = _mesh_pos()
        copies = []
        for t in range(n):
            r2 = ins[t].shape[1] // 2
            give = pl.ds(pl.multiple_of((1 - c) * r2, 8), r2)
            cp = _remote(ins[t].at[:, give], got[t], send_sems.at[t], recv_sems.at[t], (x, y, 1 - c))
            cp.start()
            copies.append(cp)
        for cp in copies:
            cp.wait()

    return pl.pallas_call(
        body, name="reduce_pair", in_specs=[ANY] * n, out_specs=[ANY] * n,
        out_shape=[S((g.shape[0], g.shape[1] // 2, g.shape[2]), g.dtype) for g in gs],
        scratch_shapes=[pltpu.SemaphoreType.DMA((n,)), pltpu.SemaphoreType.DMA((n,))],
    )(*gs)


def pair_sum(g, got, pos):
    ns, r2, c = got.shape
    tr = _row_tile(r2)
    n_i = r2 // tr

    def body(pos_ref, g_ref, got_ref, sum_ref, own_ref):
        s = pl.program_id(1)
        v = g_ref[0] + got_ref[0]
        sum_ref[0] = v.astype(BF16)

        @pl.when(s == pos_ref[1])
        def _():
            own_ref[...] = v

    return pl.pallas_call(
        body, name="pair_sum",
        grid_spec=pltpu.PrefetchScalarGridSpec(
            num_scalar_prefetch=1, grid=(n_i, ns),
            in_specs=[pl.BlockSpec((1, tr, c), lambda i, s, pos: (s, pos[0] * n_i + i, 0)),
                      pl.BlockSpec((1, tr, c), lambda i, s, pos: (s, i, 0))],
            out_specs=[pl.BlockSpec((1, tr, c), lambda i, s, pos: (s, i, 0)), pl.BlockSpec((tr, c), lambda i, s, pos: (i, 0))]),
        out_shape=[S((ns, r2, c), BF16), S((r2, c), F32)], compiler_params=_params("parallel", "arbitrary"),
    )(pos, g, got)


def reduce_chips(sums):
    n = len(sums)

    def body(*refs):
        a, p2 = refs[:n], refs[n:2 * n]
        send_sems, recv_sems = refs[2 * n:]
        x, y, c = _mesh_pos()
        copies = []
        for t in range(n):
            for j, (cx, cy) in enumerate(_other_chips(x, y)):
                cp = _remote(a[t].at[2 * cx + cy], p2[t].at[j], send_sems.at[t, j], recv_sems.at[t, j], (cx, cy, c))
                cp.start()
                copies.append(cp)
        for cp in copies:
            cp.wait()

    return pl.pallas_call(
        body, name="reduce_chips", in_specs=[ANY] * n, out_specs=[ANY] * n,
        out_shape=[S((3,) + s.shape[1:], s.dtype) for s in sums],
        scratch_shapes=[pltpu.SemaphoreType.DMA((n, 3)), pltpu.SemaphoreType.DMA((n, 3))],
    )(*sums)


def chip_sum(own, p2, pos):
    r2, c = own.shape
    tr = _row_tile(r2)
    n_i = r2 // tr

    def body(pos_ref, own_ref, a_ref, b_ref, c_ref, o_ref):
        o_ref[...] = own_ref[...] + a_ref[0].astype(F32) + b_ref[0].astype(F32) + c_ref[0].astype(F32)

    peer = lambda j: pl.BlockSpec((1, tr, c), lambda i, pos: (j, i, 0))
    return pl.pallas_call(
        body, name="chip_sum",
        grid_spec=pltpu.PrefetchScalarGridSpec(
            num_scalar_prefetch=1, grid=(n_i,),
            in_specs=[pl.BlockSpec((tr, c), lambda i, pos: (i, 0)), peer(0), peer(1), peer(2)],
            out_specs=pl.BlockSpec((tr, c), lambda i, pos: (pos[0] * n_i + i, 0))),
        out_shape=S((2 * r2, c), F32), compiler_params=_params("parallel"),
    )(pos, own, p2, p2, p2)


def exchange_halves(rs):
    n = len(rs)

    def body(*refs):
        outs = refs[n:2 * n]
        send_sems, recv_sems = refs[2 * n:]
        x, y, c = _mesh_pos()
        copies = []
        for t in range(n):
            r2 = outs[t].shape[0] // 2
            rows = outs[t].at[pl.ds(pl.multiple_of(c * r2, 8), r2)]
            cp = _remote(rows, rows, send_sems.at[t], recv_sems.at[t], (x, y, 1 - c))
            cp.start()
            copies.append(cp)
        for cp in copies:
            cp.wait()

    return pl.pallas_call(
        body, name="exchange_halves", in_specs=[ANY] * n, out_specs=[ANY] * n,
        out_shape=[S(r.shape, r.dtype) for r in rs], input_output_aliases={t: t for t in range(n)},
        scratch_shapes=[pltpu.SemaphoreType.DMA((n,)), pltpu.SemaphoreType.DMA((n,))],
    )(*rs)


def allreduce_small(vec):
    R = vec.shape[0]

    def body(x_ref, o_ref, buf, send_sems, recv_sems):
        x, y, c = _mesh_pos()
        me = 4 * x + 2 * y + c
        buf[me] = x_ref[...]
        copies = []
        for k in range(1, N_DEV):
            peer = (1 - x if k & 4 else x, 1 - y if k & 2 else y, 1 - c if k & 1 else c)
            cp = _remote(x_ref, buf.at[me], send_sems.at[k - 1], recv_sems.at[k - 1], peer)
            cp.start()
            copies.append(cp)
        for cp in copies:
            cp.wait()
        acc = buf[0]
        for d in range(1, N_DEV):
            acc = acc + buf[d]
        o_ref[...] = acc

    vm = pl.BlockSpec(memory_space=pltpu.VMEM)
    return pl.pallas_call(
        body, name="allreduce_small", in_specs=[vm], out_specs=vm, out_shape=S((R, 128), F32),
        scratch_shapes=[pltpu.VMEM((N_DEV, R, 128), F32), pltpu.SemaphoreType.DMA((N_DEV - 1,)),
                        pltpu.SemaphoreType.DMA((N_DEV - 1,))],
        compiler_params=pltpu.CompilerParams(vmem_limit_bytes=VMEM_LIMIT_BYTES),
    )(vec)


def reduce_layer(gs, pos):
    got = reduce_pair(gs)
    sums, own = zip(*[pair_sum(g, o, pos) for g, o in zip(gs, got)])
    from_chips = reduce_chips(list(sums))
    return exchange_halves([chip_sum(o, p, pos) for o, p in zip(own, from_chips)])


W_NAMES = ("ffn1_norm", "ffn1_w_gate", "ffn1_w_up", "ffn1_w_down", "mix_norm", "w_in", "conv_w", "conv_b", "ssm_A_re", "ssm_A_im",
           "ssm_B_re", "ssm_B_im", "ssm_C_re", "ssm_C_im", "ssm_D", "ssm_log_dt", "glu_w", "glu_b", "conv_out_norm", "ssm_out_norm",
           "w_out", "ffn2_norm", "ffn2_w_gate", "ffn2_w_up", "ffn2_w_down", "ple_norm", "ple_w_gate", "ple_w_proj", "final_norm")
SMALL_ALL = SMALL + ("final_norm",)
PACK = 8 * 128


def _pack(parts):
    flat = jnp.concatenate([p.reshape(-1) for p in parts])
    pad = (-flat.shape[0]) % PACK
    return jnp.pad(flat, (0, pad)).reshape(-1, 128)


def _unpack(vec, shapes):
    flat = vec.reshape(-1)
    out, off = [], 0
    for shp in shapes:
        size = math.prod(shp)
        out.append(flat[off:off + size].reshape(shp))
        off += size
    return out


def _step(a):
    x, p, target = a["x"][0], a["p"][:, 0], a["loss_target"][0]
    depth = p.shape[0]
    L, D = x.shape
    me_s = 2 * lax.axis_index("x") + lax.axis_index("y")

    pos = jnp.stack([lax.axis_index("c"), me_s]).astype(jnp.int32)
    gathered = gather_weights([cast_place(a[n], pos, BF16) for n in BIG] + [cast_place(a["conv_w"], pos, F32)])

    def layer_weights(l):
        w = {n: a[n][l] for n in SMALL if n != "conv_w"}
        w.update({n: gathered[i][l] for i, n in enumerate(BIG)})
        C = w["glu_w"].shape[-1]
        w["glu_w"] = w["glu_w"].reshape(C, C)
        w["w_out"] = w["w_out"].reshape(2, -1, D)
        w["ple_w_gate"] = w["ple_w_gate"].reshape(D, D)
        w["conv_w"] = gathered[-1][l].transpose(1, 0, 2).reshape(3, -1)
        return w

    loss_part, dx, bigs, smalls, g_final = local_step(x, p, target, [layer_weights(l) for l in range(depth)], a["final_norm"])

    reduced = [reduce_layer(bigs[l], pos) for l in range(depth)]
    big_out = {}
    for i, n in enumerate(BIG):
        g = jnp.stack([reduced[l][i] for l in range(depth)])
        delta, new_m, new_v = elementwise(_adamw, [_rows(a[n]), _rows(g), _rows(a["m_" + n]), _rows(a["v_" + n])],
                                          [F32, F32, F32], "adamw")
        big_out[n] = (g, delta.reshape(g.shape), new_m.reshape(g.shape), new_v.reshape(g.shape))

    small_shapes = [(depth,) + smalls[0][n].shape for n in SMALL] + [g_final.shape, (1,)]
    parts = [smalls[l][n] for n in SMALL for l in range(depth)] + [g_final, loss_part[0, 0:1]]
    summed = _unpack(allreduce_small(_pack(parts)), small_shapes)
    g_small = dict(zip(SMALL_ALL, summed[:-1]))
    loss = summed[-1][0]
    n_conv = a["conv_w"].shape[-1]
    g_small["conv_w"] = lax.dynamic_slice_in_dim(g_small["conv_w"], me_s * n_conv, n_conv, axis=2)
    g_small = {n: g_small[n].reshape(a[n].shape) for n in SMALL_ALL}
    packed = [_pack([src[n] for n in SMALL_ALL]) for src in
              ({n: a[n] for n in SMALL_ALL}, g_small, {n: a["m_" + n] for n in SMALL_ALL}, {n: a["v_" + n] for n in SMALL_ALL})]
    shapes = [a[n].shape for n in SMALL_ALL]
    d_s, m_s, v_s = [dict(zip(SMALL_ALL, _unpack(o, shapes))) for o in elementwise(_adamw, packed, [F32, F32, F32], "adamw_small")]

    outs = {n: big_out[n] if n in big_out else (g_small[n], d_s[n], m_s[n], v_s[n]) for n in W_NAMES}
    return (loss, dx[None], *[outs[n][0] for n in W_NAMES], *[outs[n][1] for n in W_NAMES],
            *[outs[n][2] for n in W_NAMES], *[outs[n][3] for n in W_NAMES])


def kernel(x, p, ffn1_norm, ffn1_w_gate, ffn1_w_up, ffn1_w_down, mix_norm, w_in, conv_w, conv_b, ssm_A_re, ssm_A_im, ssm_B_re, ssm_B_im, ssm_C_re, ssm_C_im, ssm_D, ssm_log_dt, glu_w, glu_b, conv_out_norm, ssm_out_norm, w_out, ffn2_norm, ffn2_w_gate, ffn2_w_up, ffn2_w_down, ple_norm, ple_w_gate, ple_w_proj, final_norm, loss_target, m_ffn1_norm, m_ffn1_w_gate, m_ffn1_w_up, m_ffn1_w_down, m_mix_norm, m_w_in, m_conv_w, m_conv_b, m_ssm_A_re, m_ssm_A_im, m_ssm_B_re, m_ssm_B_im, m_ssm_C_re, m_ssm_C_im, m_ssm_D, m_ssm_log_dt, m_glu_w, m_glu_b, m_conv_out_norm, m_ssm_out_norm, m_w_out, m_ffn2_norm, m_ffn2_w_gate, m_ffn2_w_up, m_ffn2_w_down, m_ple_norm, m_ple_w_gate, m_ple_w_proj, m_final_norm, v_ffn1_norm, v_ffn1_w_gate, v_ffn1_w_up, v_ffn1_w_down, v_mix_norm, v_w_in, v_conv_w, v_conv_b, v_ssm_A_re, v_ssm_A_im, v_ssm_B_re, v_ssm_B_im, v_ssm_C_re, v_ssm_C_im, v_ssm_D, v_ssm_log_dt, v_glu_w, v_glu_b, v_conv_out_norm, v_ssm_out_norm, v_w_out, v_ffn2_norm, v_ffn2_w_gate, v_ffn2_w_up, v_ffn2_w_down, v_ple_norm, v_ple_w_gate, v_ple_w_proj, v_final_norm):
    return _step(dict(locals()))
```

```python
import functools
import math

import jax
import jax.numpy as jnp
from jax import lax
from jax.experimental import pallas as pl
from jax.experimental.pallas import tpu as pltpu

F32, BF16 = jnp.float32, jnp.bfloat16
S = jax.ShapeDtypeStruct
EPS = 1e-6
N_SEG = 8
N_SHARD = 4
N_DEV = 8
VMEM_LIMIT_BYTES = 56 * 1024 * 1024
ADAM_LR, ADAM_B1, ADAM_B2, ADAM_EPS, ADAM_WD, ADAM_STEP = 0.001, 0.9, 0.999, 1e-08, 0.01, 10
MESH = pl.DeviceIdType.MESH


def _params(*sem):
    return pltpu.CompilerParams(dimension_semantics=sem if sem else None, vmem_limit_bytes=VMEM_LIMIT_BYTES)


def _dot(a, b, ca, cb):
    return lax.dot_general(a, b, (((ca,), (cb,)), ((), ())), preferred_element_type=F32)


def _sigmoid(x):
    return 1.0 / (1.0 + jnp.exp(-x))


def _rstd(x):
    return lax.rsqrt(jnp.mean(x * x, axis=-1, keepdims=True) + EPS)


def _rms_bwd(x, g, dy):
    r = _rstd(x)
    xh = x * r
    dxh = dy * g
    dx = r * (dxh - xh * jnp.mean(dxh * xh, axis=-1, keepdims=True))
    return dx, jnp.sum(dy * xh, axis=0, keepdims=True)


def _tile(n, want):
    return want if n % want == 0 else n


def rmsnorm_fwd(h, g):
    L, D = h.shape
    tm = _tile(L, 512)

    def body(h_ref, g_ref, o_ref):
        x = h_ref[...]
        o_ref[...] = (x * _rstd(x) * g_ref[...]).astype(BF16)

    return pl.pallas_call(
        body, name="rmsnorm_fwd", grid=(L // tm,),
        in_specs=[pl.BlockSpec((tm, D), lambda m: (m, 0)), pl.BlockSpec((1, D), lambda m: (0, 0))],
        out_specs=pl.BlockSpec((tm, D), lambda m: (m, 0)),
        out_shape=S((L, D), BF16), compiler_params=_params("parallel"),
    )(h, g.reshape(1, D))


def ffn_up(u, wg, wu):
    L, D = u.shape
    ns, _, F = wg.shape
    tm = _tile(L, 512)

    def body(u_ref, wg_ref, wu_ref, a_ref, b_ref, s_ref):
        x = u_ref[...]
        a = _dot(x, wg_ref[0], 1, 0)
        b = _dot(x, wu_ref[0], 1, 0)
        a_ref[0] = a.astype(BF16)
        b_ref[0] = b.astype(BF16)
        s_ref[0] = (a * _sigmoid(a) * b).astype(BF16)

    w_spec = pl.BlockSpec((1, D, F), lambda s, m: (s, 0, 0))
    o_spec = pl.BlockSpec((1, tm, F), lambda s, m: (s, m, 0))
    return pl.pallas_call(
        body, name="ffn_up", grid=(ns, L // tm),
        in_specs=[pl.BlockSpec((tm, D), lambda s, m: (m, 0)), w_spec, w_spec],
        out_specs=[o_spec, o_spec, o_spec],
        out_shape=[S((ns, L, F), BF16)] * 3, compiler_params=_params("parallel", "parallel"),
    )(u, wg, wu)


def mm_shard_n(u, w3, out_dtype):
    L, K = u.shape
    ns, _, N = w3.shape
    tm = _tile(L, 512)

    def body(u_ref, w_ref, o_ref):
        o_ref[0] = _dot(u_ref[...], w_ref[0], 1, 0).astype(out_dtype)

    return pl.pallas_call(
        body, name="mm_shard_n", grid=(ns, L // tm),
        in_specs=[pl.BlockSpec((tm, K), lambda s, m: (m, 0)), pl.BlockSpec((1, K, N), lambda s, m: (s, 0, 0))],
        out_specs=pl.BlockSpec((1, tm, N), lambda s, m: (s, m, 0)),
        out_shape=S((ns, L, N), out_dtype), compiler_params=_params("parallel", "parallel"),
    )(u, w3)


def mm_shard_k(a3, w3, res, scale, g_next):
    nk, L, Kc = a3.shape
    N = w3.shape[2]
    tm = _tile(L, 512)

    def body(a_ref, w_ref, r_ref, g_ref, o_ref, u_ref, acc):
        k = pl.program_id(1)
        part = _dot(a_ref[0], w_ref[0], 1, 0)

        @pl.when(k == 0)
        def _():
            acc[...] = part

        @pl.when(k > 0)
        def _():
            acc[...] += part

        @pl.when(k == nk - 1)
        def _():
            h = r_ref[...] + scale * acc[...]
            o_ref[...] = h
            u_ref[...] = (h * _rstd(h) * g_ref[...]).astype(BF16)

    tile = pl.BlockSpec((tm, N), lambda m, k: (m, 0))
    return pl.pallas_call(
        body, name="mm_shard_k", grid=(L // tm, nk),
        in_specs=[pl.BlockSpec((1, tm, Kc), lambda m, k: (k, m, 0)), pl.BlockSpec((1, Kc, N), lambda m, k: (k, 0, 0)),
                  tile, pl.BlockSpec((1, N), lambda m, k: (0, 0))],
        out_specs=[tile, tile],
        out_shape=[S((L, N), F32), S((L, N), BF16)], scratch_shapes=[pltpu.VMEM((tm, N), F32)],
        compiler_params=_params("parallel", "arbitrary"),
    )(a3, w3, res, g_next.reshape(1, N))


CONV_HALO = 8


def _conv_specs(L, tm, C, shard):
    nb = L // CONV_HALO
    per = tm // CONV_HALO
    main = pl.BlockSpec((1, tm, C), lambda m: (shard, m, 0))
    prev = pl.BlockSpec((1, CONV_HALO, C), lambda m: (shard, jnp.maximum(m * per - 1, 0), 0))
    nxt = pl.BlockSpec((1, CONV_HALO, C), lambda m: (shard, jnp.minimum((m + 1) * per, nb - 1), 0))
    return main, prev, nxt


def _conv_core(zb, zc, zv, w_ref, bias, grow, L):
    valid = (grow >= 0) & (grow < L)
    v = jnp.where(valid, zc * zv, 0.0)
    v1 = pltpu.roll(v, 1, 0)
    v2 = pltpu.roll(v, 2, 0)
    cb = w_ref[0:1, :] * v2 + w_ref[1:2, :] * v1 + w_ref[2:3, :] * v + bias
    return valid, v, v1, v2, cb, zb * cb


def conv_fwd(z, conv_w, conv_b, gnorm):
    _, L, C = z.shape
    tm = _tile(L, 256)
    H = CONV_HALO

    def body(zb_ref, zc_ref, zcp_ref, zv_ref, zvp_ref, w_ref, b_ref, g_ref, o_ref):
        m = pl.program_id(0)
        zc = jnp.concatenate([zcp_ref[0], zc_ref[0]], axis=0)
        zv = jnp.concatenate([zvp_ref[0], zv_ref[0]], axis=0)
        grow = m * tm - H + lax.broadcasted_iota(jnp.int32, (tm + H, C), 0)
        valid = grow >= 0
        v = jnp.where(valid, zc * zv, 0.0)
        v1 = pltpu.roll(v, 1, 0)
        v2 = pltpu.roll(v, 2, 0)
        cb = (w_ref[0:1, :] * v2 + w_ref[1:2, :] * v1 + w_ref[2:3, :] * v + b_ref[...])[H:, :]
        ya = zb_ref[0] * cb
        o_ref[...] = (ya * _rstd(ya) * g_ref[...]).astype(BF16)

    zb_m, _, _ = _conv_specs(L, tm, C, 0)
    zc_m, zc_p, _ = _conv_specs(L, tm, C, 1)
    zv_m, zv_p, _ = _conv_specs(L, tm, C, 2)
    row = lambda r: pl.BlockSpec((r, C), lambda m: (0, 0))
    return pl.pallas_call(
        body, name="conv_fwd", grid=(L // tm,),
        in_specs=[zb_m, zc_m, zc_p, zv_m, zv_p, row(3), row(1), row(1)],
        out_specs=pl.BlockSpec((tm, C), lambda m: (m, 0)),
        out_shape=S((L, C), BF16), compiler_params=_params("parallel"),
    )(z, z, z, z, z, conv_w, conv_b.reshape(1, C), gnorm.reshape(1, C))


def _cmul(ar, ai, br, bi):
    return ar * br - ai * bi, ar * bi + ai * br


def _scan_fwd(hr_ref, hi_ref, lr, li, n_steps):
    W = hr_ref.shape[1]
    zero = jnp.zeros((N_SEG, W), F32)

    def local(t, c):
        r = pl.multiple_of(t * N_SEG, N_SEG)
        pr, pi = _cmul(lr, li, c[0], c[1])
        nr = pr + hr_ref[pl.ds(r, N_SEG), :]
        ni = pi + hi_ref[pl.ds(r, N_SEG), :]
        hr_ref[pl.ds(r, N_SEG), :] = nr
        hi_ref[pl.ds(r, N_SEG), :] = ni
        return nr, ni

    fr, fi = lax.fori_loop(0, n_steps, local, (zero, zero))
    qr, qi = _cpow(lr, li, n_steps)
    row = lax.broadcasted_iota(jnp.int32, (N_SEG, W), 0)
    cr, ci = zero, zero
    for seg in range(1, N_SEG):
        tr, ti = _cmul(qr, qi, cr, ci)
        sr = pltpu.roll(fr + tr, 1, 0)
        si = pltpu.roll(fi + ti, 1, 0)
        cr = jnp.where(row == seg, sr, cr)
        ci = jnp.where(row == seg, si, ci)

    def fix(t, c):
        r = pl.multiple_of(t * N_SEG, N_SEG)
        pr, pi = _cmul(lr, li, c[0], c[1])
        ar, ai = _cmul(pr, pi, cr, ci)
        hr_ref[pl.ds(r, N_SEG), :] += ar
        hi_ref[pl.ds(r, N_SEG), :] += ai
        return pr, pi

    lax.fori_loop(0, n_steps, fix, (jnp.ones((N_SEG, W), F32), zero))


def _cpow(lr, li, n):
    rr, ri = None, None
    br, bi = lr, li
    while n:
        if n & 1:
            rr, ri = (br, bi) if rr is None else _cmul(rr, ri, br, bi)
        n >>= 1
        if n:
            br, bi = _cmul(br, bi, br, bi)
    return rr, ri


def _ssm_specs(L):
    col = lambda w: pl.BlockSpec((L, w), lambda j: (0, j))
    return dict(
        u=col(128), lam=pl.BlockSpec((2, 512), lambda j: (0, j)),
        bmat=pl.BlockSpec((1, 128, 512), lambda j: (j, 0, 0)), cmat=pl.BlockSpec((1, 512, 128), lambda j: (j, 0, 0)),
        d=pl.BlockSpec((1, 128), lambda j: (0, j)))


def ssm_fwd(us, lam, bre, bim, cre, cim, dvec):
    L = us.shape[0]
    n_steps = L // N_SEG
    sp = _ssm_specs(L)

    def body(u_ref, lam_ref, bre_ref, bim_ref, cre_ref, cim_ref, d_ref, y_ref, hr, hi):
        u = u_ref[...]
        ub = u.astype(BF16)
        hr[...] = _dot(ub, bre_ref[0], 1, 0)
        hi[...] = _dot(ub, bim_ref[0], 1, 0)
        lr = jnp.broadcast_to(lam_ref[0:1, :], (N_SEG, 512))
        li = jnp.broadcast_to(lam_ref[1:2, :], (N_SEG, 512))
        _scan_fwd(hr, hi, lr, li, n_steps)
        y_ref[...] = (_dot(hr[...].astype(BF16), cre_ref[0], 1, 0) - _dot(hi[...].astype(BF16), cim_ref[0], 1, 0)
                      + d_ref[...] * u)

    return pl.pallas_call(
        body, name="ssm_fwd", grid=(4,),
        in_specs=[sp["u"], sp["lam"], sp["bmat"], sp["bmat"], sp["cmat"], sp["cmat"], sp["d"]],
        out_specs=sp["u"], out_shape=S((L, 512), F32),
        scratch_shapes=[pltpu.VMEM((L, 512), F32), pltpu.VMEM((L, 512), F32)],
        compiler_params=_params("parallel"),
    )(us, lam, bre, bim, cre, cim, dvec)


_GELU_C = math.sqrt(2.0 / math.pi)


def _gelu(y):
    t = jnp.tanh(_GELU_C * (y + 0.044715 * y * y * y))
    return 0.5 * y * (1.0 + t), t


def glu_fwd(y, w, b, gnorm):
    L, C = y.shape
    tm = _tile(L, 512)

    def body(y_ref, w_ref, b_ref, g_ref, o_ref):
        zg, _ = _gelu(y_ref[...])
        out = zg * _sigmoid(_dot(zg.astype(BF16), w_ref[...], 1, 0) + b_ref[...])
        o_ref[...] = (out * _rstd(out) * g_ref[...]).astype(BF16)

    row = pl.BlockSpec((1, C), lambda m: (0, 0))
    return pl.pallas_call(
        body, name="glu_fwd", grid=(L // tm,),
        in_specs=[pl.BlockSpec((tm, C), lambda m: (m, 0)), pl.BlockSpec((C, C), lambda m: (0, 0)), row, row],
        out_specs=pl.BlockSpec((tm, C), lambda m: (m, 0)),
        out_shape=S((L, C), BF16), compiler_params=_params("parallel"),
    )(y, w, b.reshape(1, C), gnorm.reshape(1, C))


def _ple_specs(L, D, P, tm, nb):
    return [pl.BlockSpec((tm, D), lambda n, m: (m, 0)), pl.BlockSpec((tm, P), lambda n, m: (m, 0)),
            pl.BlockSpec((D, nb), lambda n, m: (0, n)), pl.BlockSpec((1, P, nb), lambda n, m: (n, 0, 0)),
            pl.BlockSpec((tm, nb), lambda n, m: (m, n))]


def ple_fwd(un, pb, wpg, wpp, h):
    L, D = un.shape
    ns, P, nb = wpp.shape
    tm = _tile(L, 512)

    def body(un_ref, p_ref, wg_ref, wp_ref, h_ref, o_ref):
        gate = _sigmoid(_dot(un_ref[...], wg_ref[...], 1, 0))
        o_ref[...] = h_ref[...] + _dot(p_ref[...], wp_ref[0], 1, 0) * gate

    return pl.pallas_call(
        body, name="ple_fwd", grid=(ns, L // tm),
        in_specs=_ple_specs(L, D, P, tm, nb),
        out_specs=pl.BlockSpec((tm, nb), lambda n, m: (m, n)),
        out_shape=S((L, D), F32), compiler_params=_params("parallel", "parallel"),
    )(un, pb, wpg, wpp, h)


def loss_head(h, g, target):
    L, D = h.shape
    tm = _tile(L, 256)

    def body(h_ref, g_ref, t_ref, loss_ref, dh_ref, dg_ref):
        m = pl.program_id(0)
        x = h_ref[...]
        gg = g_ref[...]
        e = x * _rstd(x) * gg - t_ref[...]
        dx, dg = _rms_bwd(x, gg, e * (1.0 / D))
        dh_ref[...] = dx
        part = jnp.full((8, 128), 0.5 / D, F32) * jnp.sum(e * e)

        @pl.when(m == 0)
        def _():
            loss_ref[...] = part
            dg_ref[...] = dg

        @pl.when(m > 0)
        def _():
            loss_ref[...] += part
            dg_ref[...] += dg

    return pl.pallas_call(
        body, name="loss_head", grid=(L // tm,),
        in_specs=[pl.BlockSpec((tm, D), lambda m: (m, 0)), pl.BlockSpec((1, D), lambda m: (0, 0)),
                  pl.BlockSpec((tm, D), lambda m: (m, 0))],
        out_specs=[pl.BlockSpec((8, 128), lambda m: (0, 0)), pl.BlockSpec((tm, D), lambda m: (m, 0)),
                   pl.BlockSpec((1, D), lambda m: (0, 0))],
        out_shape=[S((8, 128), F32), S((L, D), F32), S((1, D), F32)],
        compiler_params=_params("arbitrary"),
    )(h, g.reshape(1, D), target)


def ple_bwd(un, pb, wpg, wpp, dh, token):
    L, D = un.shape
    ns, P, nb = wpp.shape
    tm = _tile(L, 512)

    def body(un_ref, p_ref, wg_ref, wp_ref, dh_ref, tok_ref, dpre_ref, dpp_ref):
        gate = _sigmoid(_dot(un_ref[...], wg_ref[...], 1, 0))
        pp = _dot(p_ref[...], wp_ref[0], 1, 0)
        d = dh_ref[...] + tok_ref[0:1, 0:1]
        dpp_ref[0] = (d * gate).astype(BF16)
        dpre_ref[...] = (d * pp * gate * (1.0 - gate)).astype(BF16)

    return pl.pallas_call(
        body, name="ple_bwd", grid=(ns, L // tm),
        in_specs=_ple_specs(L, D, P, tm, nb) + [pl.BlockSpec((8, 128), lambda n, m: (0, 0))],
        out_specs=[pl.BlockSpec((tm, nb), lambda n, m: (m, n)), pl.BlockSpec((1, tm, nb), lambda n, m: (n, m, 0))],
        out_shape=[S((L, D), BF16), S((ns, L, nb), BF16)], compiler_params=_params("parallel", "parallel"),
    )(un, pb, wpg, wpp, dh, token)


def wgrad(a, b):
    a3 = a if a.ndim == 3 else a[None]
    b3 = b if b.ndim == 3 else b[None]
    ns = max(a3.shape[0], b3.shape[0])
    _, L, Ka = a3.shape
    N = b3.shape[2]
    a_map = (lambda s: (s, 0, 0)) if a3.shape[0] > 1 else (lambda s: (0, 0, 0))
    b_map = (lambda s: (s, 0, 0)) if b3.shape[0] > 1 else (lambda s: (0, 0, 0))

    def body(a_ref, b_ref, o_ref):
        o_ref[0] = _dot(a_ref[0], b_ref[0], 0, 0)

    return pl.pallas_call(
        body, name="wgrad", grid=(ns,),
        in_specs=[pl.BlockSpec((1, L, Ka), a_map), pl.BlockSpec((1, L, N), b_map)],
        out_specs=pl.BlockSpec((1, Ka, N), lambda s: (s, 0, 0)),
        out_shape=S((ns, Ka, N), F32), compiler_params=_params("parallel"),
    )(a3, b3)


def dx_rms(pairs, h, g, dh_in, cast_scale):
    L, D = h.shape
    nk = pairs[0][0].shape[0]
    n_pairs = len(pairs)
    tm = _tile(L, 512)
    n_m = L // tm

    def body(*refs):
        ins, (h_ref, g_ref, dhi_ref, dho_ref, dhb_ref, dg_ref, acc) = refs[:2 * n_pairs], refs[2 * n_pairs:]
        m, k = pl.program_id(0), pl.program_id(1)
        part = _dot(ins[0][0], ins[1][0], 1, 1)
        for i in range(1, n_pairs):
            part += _dot(ins[2 * i][0], ins[2 * i + 1][0], 1, 1)

        @pl.when(k == 0)
        def _():
            acc[...] = part

        @pl.when(k > 0)
        def _():
            acc[...] += part

        @pl.when(k == nk - 1)
        def _():
            dx, dg = _rms_bwd(h_ref[...], g_ref[...], acc[...])
            dh_out = dhi_ref[...] + dx
            dho_ref[...] = dh_out
            dhb_ref[...] = (cast_scale * dh_out).astype(BF16)

            @pl.when(m == 0)
            def _():
                dg_ref[...] = dg

            @pl.when(m > 0)
            def _():
                dg_ref[...] += dg

    in_specs, args = [], []
    for a3, w3 in pairs:
        Kc = a3.shape[2]
        in_specs += [pl.BlockSpec((1, tm, Kc), lambda m, k: (k, m, 0)), pl.BlockSpec((1, D, Kc), lambda m, k: (k, 0, 0))]
        args += [a3, w3]
    tile = pl.BlockSpec((tm, D), lambda m, k: (m, 0))
    row = pl.BlockSpec((1, D), lambda m, k: (0, 0))
    return pl.pallas_call(
        body, name="dx_rms", grid=(n_m, nk),
        in_specs=in_specs + [tile, row, tile], out_specs=[tile, tile, row],
        out_shape=[S((L, D), F32), S((L, D), BF16), S((1, D), F32)], scratch_shapes=[pltpu.VMEM((tm, D), F32)],
        compiler_params=_params("arbitrary", "arbitrary"),
    )(*args, h, g.reshape(1, D), dh_in)


def dact_plain(dhb, w3):
    L, D = dhb.shape
    ns, N, _ = w3.shape
    tm = _tile(L, 512)

    def body(d_ref, w_ref, o_ref):
        o_ref[0] = _dot(d_ref[...], w_ref[0], 1, 1)

    return pl.pallas_call(
        body, name="dact_plain", grid=(ns, L // tm),
        in_specs=[pl.BlockSpec((tm, D), lambda s, m: (m, 0)), pl.BlockSpec((1, N, D), lambda s, m: (s, 0, 0))],
        out_specs=pl.BlockSpec((1, tm, N), lambda s, m: (s, m, 0)),
        out_shape=S((ns, L, N), F32), compiler_params=_params("parallel", "parallel"),
    )(dhb, w3)


def dact_swiglu(dhb, wd, a3, b3):
    L, D = dhb.shape
    ns, F, _ = wd.shape
    tm = _tile(L, 512)

    def body(d_ref, w_ref, a_ref, b_ref, da_ref, db_ref):
        ds = _dot(d_ref[...], w_ref[0], 1, 1)
        a = a_ref[0].astype(F32)
        b = b_ref[0].astype(F32)
        sg = _sigmoid(a)
        da_ref[0] = (ds * b * (sg * (1.0 + a * (1.0 - sg)))).astype(BF16)
        db_ref[0] = (ds * (a * sg)).astype(BF16)

    t_spec = pl.BlockSpec((1, tm, F), lambda s, m: (s, m, 0))
    return pl.pallas_call(
        body, name="dact_swiglu", grid=(ns, L // tm),
        in_specs=[pl.BlockSpec((tm, D), lambda s, m: (m, 0)), pl.BlockSpec((1, F, D), lambda s, m: (s, 0, 0)), t_spec, t_spec],
        out_specs=[t_spec, t_spec], out_shape=[S((ns, L, F), BF16)] * 2,
        compiler_params=_params("parallel", "parallel"),
    )(dhb, wd, a3, b3)


def conv_bwd(z, conv_w, conv_b, gnorm, dyn):
    _, L, C = z.shape
    tm = _tile(L, 256)
    H = CONV_HALO
    T = tm + 2 * H

    def body(zb_ref, zbp_ref, zbn_ref, zc_ref, zcp_ref, zcn_ref, zv_ref, zvp_ref, zvn_ref, d_ref, dp_ref, dn_ref,
             w_ref, b_ref, g_ref, dz_ref, dw_ref, db_ref, dg_ref):
        m = pl.program_id(0)
        cat = lambda p, c, n: jnp.concatenate([p[0], c[0], n[0]], axis=0)
        zb, zc, zv, d = cat(zbp_ref, zb_ref, zbn_ref), cat(zcp_ref, zc_ref, zcn_ref), cat(zvp_ref, zv_ref, zvn_ref), cat(dp_ref, d_ref, dn_ref)
        grow = m * tm - H + lax.broadcasted_iota(jnp.int32, (T, C), 0)
        valid, v, v1, v2, cb, ya = _conv_core(zb, zc, zv, w_ref, b_ref[...], grow, L)
        dya, _ = _rms_bwd(ya, g_ref[...], d)
        dc = jnp.where(valid, dya * zb, 0.0)
        dv = w_ref[2:3, :] * dc + w_ref[1:2, :] * pltpu.roll(dc, T - 1, 0) + w_ref[0:1, :] * pltpu.roll(dc, T - 2, 0)
        dz_ref[0] = (dya * cb)[H:H + tm, :].astype(BF16)
        dz_ref[1] = (dv * zv)[H:H + tm, :].astype(BF16)
        dz_ref[2] = (dv * zc)[H:H + tm, :].astype(BF16)
        rs = lambda x: jnp.sum(x[H:H + tm, :], axis=0, keepdims=True)
        yh = ya * _rstd(ya)
        dw = jnp.concatenate([rs(dc * v2), rs(dc * v1), rs(dc * v)], axis=0)
        dbias, dg = rs(dc), rs(d * yh)

        @pl.when(m == 0)
        def _():
            dw_ref[...] = dw
            db_ref[...] = dbias
            dg_ref[...] = dg

        @pl.when(m > 0)
        def _():
            dw_ref[...] += dw
            db_ref[...] += dbias
            dg_ref[...] += dg

    row = lambda r: pl.BlockSpec((r, C), lambda m: (0, 0))
    specs = [*_conv_specs(L, tm, C, 0), *_conv_specs(L, tm, C, 1), *_conv_specs(L, tm, C, 2), *_conv_specs(L, tm, C, 0)]
    return pl.pallas_call(
        body, name="conv_bwd", grid=(L // tm,),
        in_specs=specs + [row(3), row(1), row(1)],
        out_specs=[pl.BlockSpec((3, tm, C), lambda m: (0, m, 0)), row(3), row(1), row(1)],
        out_shape=[S((3, L, C), BF16), S((3, C), F32), S((1, C), F32), S((1, C), F32)],
        compiler_params=_params("arbitrary"),
    )(z, z, z, z, z, z, z, z, z, dyn, dyn, dyn, conv_w, conv_b.reshape(1, C), gnorm.reshape(1, C))


def glu_bwd(y, w, b, gnorm, dn):
    L, C = y.shape
    tm = _tile(L, 256)

    def body(y_ref, w_ref, b_ref, g_ref, d_ref, dy_ref, dpre_ref, zg_ref, db_ref, dg_ref):
        m = pl.program_id(0)
        yv = y_ref[...]
        zg, t = _gelu(yv)
        zgb = zg.astype(BF16)
        sg = _sigmoid(_dot(zgb, w_ref[...], 1, 0) + b_ref[...])
        out = zg * sg
        dout, dg = _rms_bwd(out, g_ref[...], d_ref[...])
        dpre = dout * zg * sg * (1.0 - sg)
        dpre_b = dpre.astype(BF16)
        dzg = dout * sg + _dot(dpre_b, w_ref[...], 1, 1)
        dt = (1.0 - t * t) * _GELU_C * (1.0 + 3.0 * 0.044715 * yv * yv)
        dy_ref[...] = dzg * (0.5 * (1.0 + t) + 0.5 * yv * dt)
        dpre_ref[...] = dpre_b
        zg_ref[...] = zgb
        dbias = jnp.sum(dpre, axis=0, keepdims=True)

        @pl.when(m == 0)
        def _():
            db_ref[...] = dbias
            dg_ref[...] = dg

        @pl.when(m > 0)
        def _():
            db_ref[...] += dbias
            dg_ref[...] += dg

    tile = pl.BlockSpec((tm, C), lambda m: (m, 0))
    row = pl.BlockSpec((1, C), lambda m: (0, 0))
    return pl.pallas_call(
        body, name="glu_bwd", grid=(L // tm,),
        in_specs=[tile, pl.BlockSpec((C, C), lambda m: (0, 0)), row, row, tile],
        out_specs=[tile, tile, tile, row, row],
        out_shape=[S((L, C), F32), S((L, C), BF16), S((L, C), BF16), S((1, C), F32), S((1, C), F32)],
        compiler_params=_params("arbitrary"),
    )(y, w, b.reshape(1, C), gnorm.reshape(1, C), dn)


def _scan_bwd(gr_ref, gi_ref, hr_ref, hi_ref, lr, li, n_steps):
    W = gr_ref.shape[1]
    zero = jnp.zeros((N_SEG, W), F32)
    lic = -li

    def local(i, c):
        r = pl.multiple_of((n_steps - 1 - i) * N_SEG, N_SEG)
        pr, pi = _cmul(lr, lic, c[0], c[1])
        nr = pr + gr_ref[pl.ds(r, N_SEG), :]
        ni = pi + gi_ref[pl.ds(r, N_SEG), :]
        gr_ref[pl.ds(r, N_SEG), :] = nr
        gi_ref[pl.ds(r, N_SEG), :] = ni
        return nr, ni

    fr, fi = lax.fori_loop(0, n_steps, local, (zero, zero))
    qr, qi = _cpow(lr, lic, n_steps)
    row = lax.broadcasted_iota(jnp.int32, (N_SEG, W), 0)
    cr, ci = zero, zero
    for seg in range(N_SEG - 2, -1, -1):
        tr, ti = _cmul(qr, qi, cr, ci)
        sr = pltpu.roll(fr + tr, N_SEG - 1, 0)
        si = pltpu.roll(fi + ti, N_SEG - 1, 0)
        cr = jnp.where(row == seg, sr, cr)
        ci = jnp.where(row == seg, si, ci)

    def fix(i, c):
        pwr, pwi, ar, ai = c
        t = n_steps - 1 - i
        r = pl.multiple_of(t * N_SEG, N_SEG)
        pwr, pwi = _cmul(lr, lic, pwr, pwi)
        xr, xi = _cmul(pwr, pwi, cr, ci)
        g_r = gr_ref[pl.ds(r, N_SEG), :] + xr
        g_i = gi_ref[pl.ds(r, N_SEG), :] + xi
        gr_ref[pl.ds(r, N_SEG), :] = g_r
        gi_ref[pl.ds(r, N_SEG), :] = g_i
        rp = pl.multiple_of(jnp.maximum(t - 1, 0) * N_SEG, N_SEG)
        hpr = hr_ref[pl.ds(rp, N_SEG), :]
        hpi = hi_ref[pl.ds(rp, N_SEG), :]
        live = t > 0
        ar = ar + jnp.where(live, hpr * g_r + hpi * g_i, 0.0)
        ai = ai + jnp.where(live, hpr * g_i - hpi * g_r, 0.0)
        return pwr, pwi, ar, ai

    _, _, ar, ai = lax.fori_loop(0, n_steps, fix, (jnp.ones((N_SEG, W), F32), zero, zero, zero))
    last = pl.ds((n_steps - 1) * N_SEG, N_SEG)
    hpr = jnp.where(row == 0, 0.0, pltpu.roll(hr_ref[last, :], 1, 0))
    hpi = jnp.where(row == 0, 0.0, pltpu.roll(hi_ref[last, :], 1, 0))
    g_r, g_i = gr_ref[pl.ds(0, N_SEG), :], gi_ref[pl.ds(0, N_SEG), :]
    ar = ar + hpr * g_r + hpi * g_i
    ai = ai + hpr * g_i - hpi * g_r
    return jnp.sum(ar, axis=0, keepdims=True), jnp.sum(ai, axis=0, keepdims=True)


def ssm_bwd(us, dy, lam, bre, bim, cre, cim, dvec):
    L = us.shape[0]
    n_steps = L // N_SEG
    sp = _ssm_specs(L)

    def body(u_ref, dy_ref, lam_ref, bre_ref, bim_ref, cre_ref, cim_ref, d_ref,
             du_ref, dlam_ref, dbre_ref, dbim_ref, dcre_ref, dcim_ref, dd_ref, hr, hi, gr, gi):
        u = u_ref[...]
        ub = u.astype(BF16)
        dyv = dy_ref[...]
        dyb = dyv.astype(BF16)
        hr[...] = _dot(ub, bre_ref[0], 1, 0)
        hi[...] = _dot(ub, bim_ref[0], 1, 0)
        lr = jnp.broadcast_to(lam_ref[0:1, :], (N_SEG, 512))
        li = jnp.broadcast_to(lam_ref[1:2, :], (N_SEG, 512))
        _scan_fwd(hr, hi, lr, li, n_steps)
        dcre_ref[0] = _dot(hr[...].astype(BF16), dyb, 0, 0)
        dcim_ref[0] = -_dot(hi[...].astype(BF16), dyb, 0, 0)
        gr[...] = _dot(dyb, cre_ref[0], 1, 1)
        gi[...] = -_dot(dyb, cim_ref[0], 1, 1)
        dlr, dli = _scan_bwd(gr, gi, hr, hi, lr, li, n_steps)
        dlam_ref[...] = jnp.concatenate([dlr, dli], axis=0)
        grb, gib = gr[...].astype(BF16), gi[...].astype(BF16)
        du_ref[...] = _dot(grb, bre_ref[0], 1, 1) + _dot(gib, bim_ref[0], 1, 1) + d_ref[...] * dyv
        dbre_ref[0] = _dot(ub, grb, 0, 0)
        dbim_ref[0] = _dot(ub, gib, 0, 0)
        dd_ref[...] = jnp.sum(dyv * u, axis=0, keepdims=True)

    big = pltpu.VMEM((L, 512), F32)
    return pl.pallas_call(
        body, name="ssm_bwd", grid=(4,),
        in_specs=[sp["u"], sp["u"], sp["lam"], sp["bmat"], sp["bmat"], sp["cmat"], sp["cmat"], sp["d"]],
        out_specs=[sp["u"], sp["lam"], sp["bmat"], sp["bmat"], sp["cmat"], sp["cmat"], sp["d"]],
        out_shape=[S((L, 512), F32), S((2, 2048), F32), S((4, 128, 512), F32), S((4, 128, 512), F32),
                   S((4, 512, 128), F32), S((4, 512, 128), F32), S((1, 512), F32)],
        scratch_shapes=[big, big, big, big], compiler_params=_params("parallel"),
    )(us, dy, lam, bre, bim, cre, cim, dvec)


def _discretize(ar, ai, log_dt, br, bi):
    dt = jnp.exp(log_dt)
    mag = jnp.exp(ar * dt)
    ph = ai * dt
    lr, li = mag * jnp.cos(ph), mag * jnp.sin(ph)
    nr, ni = lr - 1.0, li
    den = ar * ar + ai * ai
    fr = (nr * ar + ni * ai) / den
    fi = (ni * ar - nr * ai) / den
    return lr, li, fr[..., None] * br - fi[..., None] * bi, fr[..., None] * bi + fi[..., None] * br


def ssm_prep(ar, ai, log_dt, br, bi):
    G, P, H = br.shape

    def body(ar_ref, ai_ref, dt_ref, br_ref, bi_ref, lr_ref, li_ref, bbr_ref, bbi_ref):
        lr_ref[...], li_ref[...], bbr_ref[...], bbi_ref[...] = _discretize(
            ar_ref[...], ai_ref[...], dt_ref[...], br_ref[...], bi_ref[...])

    return pl.pallas_call(
        body, name="ssm_prep",
        out_shape=[S((G, P), F32), S((G, P), F32), S((G, P, H), F32), S((G, P, H), F32)],
    )(ar, ai, log_dt.reshape(G, 1), br, bi)


def ssm_prep_bwd(ar, ai, log_dt, br, bi, dlr, dli, dbbr, dbbi):
    G, P, H = br.shape

    def body(ar_ref, ai_ref, dt_ref, br_ref, bi_ref, dlr_ref, dli_ref, dbbr_ref, dbbi_ref,
             dar_ref, dai_ref, ddt_ref, dbr_ref, dbi_ref):
        _, vjp = jax.vjp(_discretize, ar_ref[...], ai_ref[...], dt_ref[...], br_ref[...], bi_ref[...])
        dar_ref[...], dai_ref[...], ddt_ref[...], dbr_ref[...], dbi_ref[...] = vjp(
            (dlr_ref[...], dli_ref[...], dbbr_ref[...], dbbi_ref[...]))

    return pl.pallas_call(
        body, name="ssm_prep_bwd",
        out_shape=[S((G, P), F32), S((G, P), F32), S((G, 1), F32), S((G, P, H), F32), S((G, P, H), F32)],
    )(ar, ai, log_dt.reshape(G, 1), br, bi, dlr, dli, dbbr, dbbi)


def _block_diag(x):
    j, n, R, C = x.shape
    eye = jnp.eye(n, dtype=x.dtype)
    return (x[:, :, :, None, :] * eye[None, :, None, :, None]).reshape(j, n * R, n * C)


def _block_diag_take(x, R, C):
    j = x.shape[0]
    n = x.shape[1] // R
    x5 = x.reshape(j, n, R, n, C)
    return jnp.stack([x5[:, i, :, i, :] for i in range(n)], axis=1)


def _to_segments(x):
    L, C = x.shape
    return x.reshape(N_SEG, L // N_SEG, C).transpose(1, 0, 2).reshape(L, C)


def _from_segments(x):
    L, C = x.shape
    return x.reshape(L // N_SEG, N_SEG, C).transpose(1, 0, 2).reshape(L, C)


BIG = ("ffn1_w_gate", "ffn1_w_up", "ffn1_w_down", "w_in", "glu_w", "w_out",
       "ffn2_w_gate", "ffn2_w_up", "ffn2_w_down", "ple_w_gate", "ple_w_proj")
SMALL = ("ffn1_norm", "mix_norm", "conv_w", "conv_b", "ssm_A_re", "ssm_A_im", "ssm_B_re", "ssm_B_im", "ssm_C_re", "ssm_C_im",
         "ssm_D", "ssm_log_dt", "glu_b", "conv_out_norm", "ssm_out_norm", "ffn2_norm", "ple_norm")


def _ssm_mats(w):
    G, P, H = w["ssm_B_re"].shape
    lr, li, bbr, bbi = ssm_prep(w["ssm_A_re"], w["ssm_A_im"], w["ssm_log_dt"], w["ssm_B_re"], w["ssm_B_im"])
    lam = jnp.stack([lr.reshape(G * P), li.reshape(G * P)])
    bmat = lambda bb: _block_diag(bb.reshape(4, G // 4, P, H).transpose(0, 1, 3, 2)).astype(BF16)
    cmat = lambda c: _block_diag(c.reshape(4, G // 4, H, P).transpose(0, 1, 3, 2)).astype(BF16)
    return lam, bmat(bbr), bmat(bbi), cmat(w["ssm_C_re"]), cmat(w["ssm_C_im"]), w["ssm_D"].reshape(1, G * H)


def layer_fwd(h0, pb, w):
    L, D = h0.shape
    u1 = rmsnorm_fwd(h0, w["ffn1_norm"])
    a1, b1, s1 = ffn_up(u1, w["ffn1_w_gate"], w["ffn1_w_up"])
    h1, u2 = mm_shard_k(s1, w["ffn1_w_down"], h0, 0.5, w["mix_norm"])
    z = mm_shard_n(u2, w["w_in"], F32)
    ya_n = conv_fwd(z, w["conv_w"], w["conv_b"], w["conv_out_norm"])
    us = _to_segments(z[3])
    mats = _ssm_mats(w)
    y = ssm_fwd(us, *mats)
    ys_n = glu_fwd(y, w["glu_w"], w["glu_b"], w["ssm_out_norm"])
    ycat = jnp.stack([ya_n, _from_segments(ys_n)])
    h2, u3 = mm_shard_k(ycat, w["w_out"], h1, 1.0, w["ffn2_norm"])
    a2, b2, s2 = ffn_up(u3, w["ffn2_w_gate"], w["ffn2_w_up"])
    h3, un = mm_shard_k(s2, w["ffn2_w_down"], h2, 0.5, w["ple_norm"])
    h4 = ple_fwd(un, pb, w["ple_w_gate"], w["ple_w_proj"], h3)
    saved = dict(h0=h0, u1=u1, a1=a1, b1=b1, s1=s1, h1=h1, u2=u2, z=z, us=us, mats=mats, y=y, ycat=ycat,
                 h2=h2, u3=u3, a2=a2, b2=b2, s2=s2, h3=h3, un=un)
    return h4, saved


def _ffn_bwd(dh, dhb, h_in, u, a, b, s, wg, wu, wd, gnorm, cast_scale):
    da, db = dact_swiglu(dhb, wd, a, b)
    g_wd = wgrad(s, dhb)
    g_wg = wgrad(u, da)
    g_wu = wgrad(u, db)
    dh_in, dhb_in, g_norm = dx_rms([(da, wg), (db, wu)], h_in, gnorm, dh, cast_scale)
    return dh_in, dhb_in, g_wg, g_wu, g_wd, g_norm


def layer_bwd(dh, pb, w, sv, token):
    L, D = dh.shape
    G, P, H = w["ssm_B_re"].shape
    dpre, dpp3 = ple_bwd(sv["un"], pb, w["ple_w_gate"], w["ple_w_proj"], dh, token)
    g_wpg = wgrad(sv["un"], dpre).reshape(N_SHARD, D // N_SHARD, D)
    g_wpp = wgrad(pb, dpp3)
    dh3, dhb3, g_nple = dx_rms([(dpre[None], w["ple_w_gate"][None])], sv["h3"], w["ple_norm"], dh, 0.5)
    dh2, dhb, g_wg2, g_wu2, g_wd2, g_nffn2 = _ffn_bwd(dh3, dhb3, sv["h2"], sv["u3"], sv["a2"], sv["b2"], sv["s2"],
                                                      w["ffn2_w_gate"], w["ffn2_w_up"], w["ffn2_w_down"], w["ffn2_norm"], 1.0)
    dyn = dact_plain(dhb, w["w_out"])
    g_wout = wgrad(sv["ycat"], dhb).reshape(N_SHARD, -1, D)
    dz_abc, g_convw, g_convb, g_nconv = conv_bwd(sv["z"], w["conv_w"], w["conv_b"], w["conv_out_norm"], dyn)
    dy, dpre_g, zg, g_glub, g_nssm = glu_bwd(sv["y"], w["glu_w"], w["glu_b"], w["ssm_out_norm"], _to_segments(dyn[1]))
    C = zg.shape[1]
    g_gluw = wgrad(zg, dpre_g).reshape(N_SHARD, C // N_SHARD, C)
    dus, dlam, dbre, dbim, dcre, dcim, dd = ssm_bwd(sv["us"], dy, *sv["mats"])
    take_b = lambda m: _block_diag_take(m, H, P).transpose(0, 1, 3, 2).reshape(G, P, H)
    take_c = lambda m: _block_diag_take(m, P, H).transpose(0, 1, 3, 2).reshape(G, H, P)
    g_ar, g_ai, g_dt, g_br, g_bi = ssm_prep_bwd(
        w["ssm_A_re"], w["ssm_A_im"], w["ssm_log_dt"], w["ssm_B_re"], w["ssm_B_im"],
        dlam[0].reshape(G, P), dlam[1].reshape(G, P), take_b(dbre), take_b(dbim))
    dz3 = jnp.concatenate([dz_abc, _from_segments(dus).astype(BF16)[None]], axis=0)
    g_win = wgrad(sv["u2"], dz3)
    dh1, dhb1, g_nmix = dx_rms([(dz3, w["w_in"])], sv["h1"], w["mix_norm"], dh2, 0.5)
    dh0, _, g_wg1, g_wu1, g_wd1, g_nffn1 = _ffn_bwd(dh1, dhb1, sv["h0"], sv["u1"], sv["a1"], sv["b1"], sv["s1"],
                                                    w["ffn1_w_gate"], w["ffn1_w_up"], w["ffn1_w_down"], w["ffn1_norm"], 1.0)
    big = [g_wg1, g_wu1, g_wd1, g_win, g_gluw, g_wout, g_wg2, g_wu2, g_wd2, g_wpg, g_wpp]
    small = dict(ffn1_norm=g_nffn1, mix_norm=g_nmix, conv_w=g_convw, conv_b=g_convb, ssm_A_re=g_ar, ssm_A_im=g_ai,
                 ssm_B_re=g_br, ssm_B_im=g_bi, ssm_C_re=take_c(dcre), ssm_C_im=take_c(dcim), ssm_D=dd,
                 ssm_log_dt=g_dt, glu_b=g_glub, conv_out_norm=g_nconv, ssm_out_norm=g_nssm, ffn2_norm=g_nffn2,
                 ple_norm=g_nple)
    return dh0, big, small


def local_step(x, p, target, final_norm, weights_of, on_grads):
    depth = p.shape[0]
    h = x
    layers, saved, pbs = [], [], []
    for i in range(depth):
        w = weights_of(i, h)
        pb = p[i].astype(BF16)
        h, sv = layer_fwd(h, pb, w)
        layers.append(w)
        saved.append(sv)
        pbs.append(pb)
    loss_part, dh, g_final = loss_head(h, final_norm, target)
    smalls = [None] * depth
    token = jnp.zeros((8, 128), F32)
    for i in reversed(range(depth)):
        dh, big, smalls[i] = layer_bwd(dh, pbs[i], layers[i], saved[i], token)
        token = on_grads(i, big, dh)
    return loss_part, dh, smalls, g_final


def _row_tile(rows):
    for t in (512, 352, 256, 128, 64, 32, 16):
        if rows % t == 0:
            return t
    return rows


def elementwise(fn, ins, out_dtypes, name):
    rows, cols = ins[0].shape
    tr = _row_tile(rows)
    n_in = len(ins)

    def body(*refs):
        outs = fn(*[r[...] for r in refs[:n_in]])
        for o_ref, o in zip(refs[n_in:], outs):
            o_ref[...] = o.astype(o_ref.dtype)

    spec = pl.BlockSpec((tr, cols), lambda i: (i, 0))
    return pl.pallas_call(
        body, name=name, grid=(rows // tr,), in_specs=[spec] * n_in, out_specs=[spec] * len(out_dtypes),
        out_shape=[S((rows, cols), d) for d in out_dtypes], compiler_params=_params("parallel"),
    )(*ins)


def _rows(a):
    return a.reshape(-1, a.shape[-1])


def _adamw(w, g, m, v):
    m = ADAM_B1 * m + (1.0 - ADAM_B1) * g
    v = ADAM_B2 * v + (1.0 - ADAM_B2) * (g * g)
    m_hat = m / (1.0 - ADAM_B1 ** ADAM_STEP)
    v_hat = v / (1.0 - ADAM_B2 ** ADAM_STEP)
    delta = -ADAM_LR * (m_hat / (jnp.sqrt(v_hat) + ADAM_EPS) + ADAM_WD * w)
    return delta, m, v


ANY = pl.BlockSpec(memory_space=pl.ANY)


def _mesh_pos():
    return lax.axis_index("x"), lax.axis_index("y"), lax.axis_index("c")


def _other_chips(x, y):
    return [(1 - x, y), (x, 1 - y), (1 - x, 1 - y)]


def _remote(src, dst, send_sem, recv_sem, device):
    return pltpu.make_async_remote_copy(src_ref=src, dst_ref=dst, send_sem=send_sem, recv_sem=recv_sem,
                                        device_id=device, device_id_type=MESH)


def gather_weights(ws):
    n = len(ws)

    def body(*refs):
        outs = refs[n:2 * n]
        send_sems, recv_sems = refs[2 * n:]
        x, y, c = _mesh_pos()
        me_s = 2 * x + y
        sibling = (x, y, 1 - c)
        chips = _other_chips(x, y)
        n_half = outs[0].shape[0] // 2
        mine, other = pl.ds(c * n_half, n_half), pl.ds((1 - c) * n_half, n_half)
        sent = []
        for t in range(n):
            for j, (cx, cy) in enumerate(chips):
                blk = outs[t].at[mine, me_s]
                cp = _remote(blk, blk, send_sems.at[t, j], recv_sems.at[t, j], (cx, cy, c))
                cp.start()
                sent.append(cp)
        for j, (cx, cy) in enumerate(chips):
            for t in range(n):
                blk = outs[t].at[mine, 2 * cx + cy]
                _remote(blk, blk, send_sems.at[t, j], recv_sems.at[t, j], (cx, cy, c)).wait_recv()
                cp = _remote(blk, blk, send_sems.at[t, 3 + j], recv_sems.at[t, 3 + j], sibling)
                cp.start()
                sent.append(cp)
        for j, (cx, cy) in enumerate(chips):
            for t in range(n):
                blk = outs[t].at[other, 2 * cx + cy]
                _remote(blk, blk, send_sems.at[t, 3 + j], recv_sems.at[t, 3 + j], sibling).wait_recv()
        for cp in sent:
            cp.wait_send()

    return pl.pallas_call(
        body, name="gather_weights", in_specs=[ANY] * n, out_specs=[ANY] * n,
        out_shape=[S(w.shape, w.dtype) for w in ws], input_output_aliases={t: t for t in range(n)},
        scratch_shapes=[pltpu.SemaphoreType.DMA((n, 6)), pltpu.SemaphoreType.DMA((n, 6))],
    )(*ws)


HBM = pl.BlockSpec(memory_space=pltpu.HBM)
SEM = pl.BlockSpec(memory_space=pltpu.SEMAPHORE)
VMEM_WHOLE = pl.BlockSpec(memory_space=pltpu.VMEM)
SPLIT_COPY = pltpu.CompilerParams(has_side_effects=pltpu.SideEffectType.DATAFLOW_SIDE_EFFECTING)


def _hbm(x):
    return pltpu.with_memory_space_constraint(x, pltpu.HBM)


def _half_rows(ref, c):
    r2 = ref.shape[1] // 2
    return pl.ds(pl.multiple_of(c * r2, 8), r2)


def gather_start(bufs, layer):
    n = len(bufs)

    def body(*refs):
        ins, send_sems, recv_sems, token = refs[:n], refs[n], refs[n + 1], refs[2 * n + 2]
        x, y, c = _mesh_pos()
        me_s = 2 * x + y
        for t in range(n):
            blk = ins[t].at[me_s, _half_rows(ins[t], c)]
            for j, (cx, cy) in enumerate(_other_chips(x, y)):
                _remote(blk, blk, send_sems.at[3 * t + j], recv_sems.at[3 * t + j], (cx, cy, c)).start()
        token[...] = jnp.zeros_like(token)

    outs = pl.pallas_call(
        body, name=f"gather_start_{layer}", in_specs=[HBM] * n, out_specs=[SEM, SEM] + [HBM] * n + [VMEM_WHOLE],
        out_shape=[pltpu.SemaphoreType.DMA((3 * n,)), pltpu.SemaphoreType.DMA((3 * n,))]
        + [pltpu.HBM(b.shape, b.dtype) for b in bufs] + [S((8, 128), F32)],
        input_output_aliases={t: t + 2 for t in range(n)}, compiler_params=SPLIT_COPY,
    )(*[_hbm(b) for b in bufs])
    return outs[0], outs[1], list(outs[2:2 + n]), outs[2 + n]


def gather_wait(bufs, send_sems, recv_sems, after, layer):
    n, n_after = len(bufs), len(after)

    def body(*refs):
        ins, send_ref, recv_ref = refs[:n], refs[n], refs[n + 1]
        x, y, c = _mesh_pos()
        me_s = 2 * x + y
        for t in range(n):
            rows = _half_rows(ins[t], c)
            for j, (cx, cy) in enumerate(_other_chips(x, y)):
                cp = _remote(ins[t].at[me_s, rows], ins[t].at[2 * cx + cy, rows], send_ref.at[3 * t + j], recv_ref.at[3 * t + j],
                             (cx, cy, c))
                cp.wait_send()
                cp.wait_recv()

    outs = pl.pallas_call(
        body, name=f"gather_wait_{layer}", in_specs=[HBM] * n + [SEM, SEM] + [ANY] * n_after, out_specs=[HBM] * n,
        out_shape=[pltpu.HBM(b.shape, b.dtype) for b in bufs],
        input_output_aliases={t: t for t in range(n)}, compiler_params=SPLIT_COPY,
    )(*bufs, send_sems, recv_sems, *after)
    return list(outs)


def gather_forward(bufs):
    n = len(bufs)

    def body(*refs):
        outs = refs[n:2 * n]
        send_sems, recv_sems = refs[2 * n:]
        x, y, c = _mesh_pos()
        copies = []
        for t in range(n):
            rows = _half_rows(outs[t], c)
            for j, (cx, cy) in enumerate(_other_chips(x, y)):
                blk = outs[t].at[2 * cx + cy, rows]
                cp = _remote(blk, blk, send_sems.at[t, j], recv_sems.at[t, j], (x, y, 1 - c))
                cp.start()
                copies.append(cp)
        for cp in copies:
            cp.wait()

    return pl.pallas_call(
        body, name="gather_forward", in_specs=[ANY] * n, out_specs=[ANY] * n,
        out_shape=[S(b.shape, b.dtype) for b in bufs], input_output_aliases={t: t for t in range(n)},
        scratch_shapes=[pltpu.SemaphoreType.DMA((n, 3)), pltpu.SemaphoreType.DMA((n, 3))],
    )(*bufs)


def chips_start(sums, layer):
    n = len(sums)
    lands = [lax.empty((3,) + s.shape[1:], s.dtype) for s in sums]

    def body(*refs):
        a, land, send_sems, recv_sems, token = refs[:n], refs[n:2 * n], refs[2 * n], refs[2 * n + 1], refs[4 * n + 2]
        x, y, c = _mesh_pos()
        for t in range(n):
            for j, (cx, cy) in enumerate(_other_chips(x, y)):
                _remote(a[t].at[2 * cx + cy], land[t].at[j], send_sems.at[3 * t + j], recv_sems.at[3 * t + j], (cx, cy, c)).start()
        token[...] = jnp.zeros_like(token)

    outs = pl.pallas_call(
        body, name=f"chips_start_{layer}", in_specs=[HBM] * (2 * n), out_specs=[SEM, SEM] + [HBM] * (2 * n) + [VMEM_WHOLE],
        out_shape=[pltpu.SemaphoreType.DMA((3 * n,)), pltpu.SemaphoreType.DMA((3 * n,))]
        + [pltpu.HBM(b.shape, b.dtype) for b in sums + lands] + [S((8, 128), F32)],
        input_output_aliases={t: t + 2 for t in range(2 * n)}, compiler_params=SPLIT_COPY,
    )(*[_hbm(b) for b in sums + lands])
    return outs[0], outs[1], list(outs[2:2 + n]), list(outs[2 + n:2 + 2 * n]), outs[2 + 2 * n]


def chips_wait(sums, lands, send_sems, recv_sems, after, layer):
    n, n_after = len(sums), len(after)

    def body(*refs):
        a, land, send_ref, recv_ref = refs[:n], refs[n:2 * n], refs[2 * n], refs[2 * n + 1]
        x, y, c = _mesh_pos()
        for t in range(n):
            for j, (cx, cy) in enumerate(_other_chips(x, y)):
                cp = _remote(a[t].at[2 * cx + cy], land[t].at[j], send_ref.at[3 * t + j], recv_ref.at[3 * t + j], (cx, cy, c))
                cp.wait_send()
                cp.wait_recv()

    outs = pl.pallas_call(
        body, name=f"chips_wait_{layer}", in_specs=[HBM] * (2 * n) + [SEM, SEM] + [ANY] * n_after, out_specs=[HBM] * (2 * n),
        out_shape=[pltpu.HBM(b.shape, b.dtype) for b in sums + lands],
        input_output_aliases={t: t for t in range(2 * n)}, compiler_params=SPLIT_COPY,
    )(*sums, *lands, send_sems, recv_sems, *after)
    return list(outs[n:])


def cast_place_layer(w, layer, pos, dtype):
    _, r, c = w.shape
    tr = _row_tile(r)

    def body(pos_ref, w_ref, o_ref):
        o_ref[0] = w_ref[0].astype(dtype)

    return pl.pallas_call(
        body, name="cast_place_layer",
        grid_spec=pltpu.PrefetchScalarGridSpec(
            num_scalar_prefetch=1, grid=(r // tr,),
            in_specs=[pl.BlockSpec((1, tr, c), lambda i, pos: (layer, i, 0))],
            out_specs=pl.BlockSpec((1, tr, c), lambda i, pos: (pos[1], i, 0))),
        out_shape=S((N_SHARD, r, c), dtype), compiler_params=_params("parallel"),
    )(pos, w)


def cast_place(w, pos, dtype):
    layers, r, c = w.shape
    tr = _row_tile(r)

    def body(pos_ref, w_ref, o_ref):
        o_ref[0, 0] = w_ref[0].astype(dtype)

    return pl.pallas_call(
        body, name="cast_place",
        grid_spec=pltpu.PrefetchScalarGridSpec(
            num_scalar_prefetch=1, grid=(layers, r // tr),
            in_specs=[pl.BlockSpec((1, tr, c), lambda l, i, pos: (l, i, 0))],
            out_specs=pl.BlockSpec((1, 1, tr, c), lambda l, i, pos: (l, pos[1], i, 0))),
        out_shape=S((layers, N_SHARD, r, c), dtype), compiler_params=_params("parallel", "parallel"),
    )(pos, w)


def reduce_pair(gs):
    n = len(gs)

    def body(*refs):
        ins, got = refs[:n], refs[n:2 * n]
        send_sems, recv_sems = refs[2 * n:]
        x, y, c = _mesh_pos()
        copies = []
        for t in range(n):
            r2 = ins[t].shape[1] // 2
            give = pl.ds(pl.multiple_of((1 - c) * r2, 8), r2)
            cp = _remote(ins[t].at[:, give], got[t], send_sems.at[t], recv_sems.at[t], (x, y, 1 - c))
            cp.start()
            copies.append(cp)
        for cp in copies:
            cp.wait()

    return pl.pallas_call(
        body, name="reduce_pair", in_specs=[ANY] * n, out_specs=[ANY] * n,
        out_shape=[S((g.shape[0], g.shape[1] // 2, g.shape[2]), g.dtype) for g in gs],
        scratch_shapes=[pltpu.SemaphoreType.DMA((n,)), pltpu.SemaphoreType.DMA((n,))],
    )(*gs)


def pair_sum(g, got, pos):
    ns, r2, c = got.shape
    tr = _row_tile(r2)
    n_i = r2 // tr

    def body(pos_ref, g_ref, got_ref, sum_ref, own_ref):
        s = pl.program_id(1)
        v = g_ref[0] + got_ref[0]
        sum_ref[0] = v.astype(BF16)

        @pl.when(s == pos_ref[1])
        def _():
            own_ref[...] = v

    return pl.pallas_call(
        body, name="pair_sum",
        grid_spec=pltpu.PrefetchScalarGridSpec(
            num_scalar_prefetch=1, grid=(n_i, ns),
            in_specs=[pl.BlockSpec((1, tr, c), lambda i, s, pos: (s, pos[0] * n_i + i, 0)),
                      pl.BlockSpec((1, tr, c), lambda i, s, pos: (s, i, 0))],
            out_specs=[pl.BlockSpec((1, tr, c), lambda i, s, pos: (s, i, 0)), pl.BlockSpec((tr, c), lambda i, s, pos: (i, 0))]),
        out_shape=[S((ns, r2, c), BF16), S((r2, c), F32)], compiler_params=_params("parallel", "arbitrary"),
    )(pos, g, got)


def chip_sum(own, p2, pos):
    r2, c = own.shape
    tr = _row_tile(r2)
    n_i = r2 // tr

    def body(pos_ref, own_ref, a_ref, b_ref, c_ref, o_ref):
        o_ref[...] = own_ref[...] + a_ref[0].astype(F32) + b_ref[0].astype(F32) + c_ref[0].astype(F32)

    peer = lambda j: pl.BlockSpec((1, tr, c), lambda i, pos: (j, i, 0))
    return pl.pallas_call(
        body, name="chip_sum",
        grid_spec=pltpu.PrefetchScalarGridSpec(
            num_scalar_prefetch=1, grid=(n_i,),
            in_specs=[pl.BlockSpec((tr, c), lambda i, pos: (i, 0)), peer(0), peer(1), peer(2)],
            out_specs=pl.BlockSpec((tr, c), lambda i, pos: (pos[0] * n_i + i, 0))),
        out_shape=S((2 * r2, c), F32), compiler_params=_params("parallel"),
    )(pos, own, p2, p2, p2)


def exchange_halves(rs):
    n = len(rs)

    def body(*refs):
        outs = refs[n:2 * n]
        send_sems, recv_sems = refs[2 * n:]
        x, y, c = _mesh_pos()
        copies = []
        for t in range(n):
            r2 = outs[t].shape[0] // 2
            rows = outs[t].at[pl.ds(pl.multiple_of(c * r2, 8), r2)]
            cp = _remote(rows, rows, send_sems.at[t], recv_sems.at[t], (x, y, 1 - c))
            cp.start()
            copies.append(cp)
        for cp in copies:
            cp.wait()

    return pl.pallas_call(
        body, name="exchange_halves", in_specs=[ANY] * n, out_specs=[ANY] * n,
        out_shape=[S(r.shape, r.dtype) for r in rs], input_output_aliases={t: t for t in range(n)},
        scratch_shapes=[pltpu.SemaphoreType.DMA((n,)), pltpu.SemaphoreType.DMA((n,))],
    )(*rs)


def allreduce_small(vec):
    R = vec.shape[0]

    def body(x_ref, o_ref, buf, send_sems, recv_sems):
        x, y, c = _mesh_pos()
        me = 4 * x + 2 * y + c
        buf[me] = x_ref[...]
        copies = []
        for k in range(1, N_DEV):
            peer = (1 - x if k & 4 else x, 1 - y if k & 2 else y, 1 - c if k & 1 else c)
            cp = _remote(x_ref, buf.at[me], send_sems.at[k - 1], recv_sems.at[k - 1], peer)
            cp.start()
            copies.append(cp)
        for cp in copies:
            cp.wait()
        acc = buf[0]
        for d in range(1, N_DEV):
            acc = acc + buf[d]
        o_ref[...] = acc

    vm = pl.BlockSpec(memory_space=pltpu.VMEM)
    return pl.pallas_call(
        body, name="allreduce_small", in_specs=[vm], out_specs=vm, out_shape=S((R, 128), F32),
        scratch_shapes=[pltpu.VMEM((N_DEV, R, 128), F32), pltpu.SemaphoreType.DMA((N_DEV - 1,)),
                        pltpu.SemaphoreType.DMA((N_DEV - 1,))],
        compiler_params=pltpu.CompilerParams(vmem_limit_bytes=VMEM_LIMIT_BYTES),
    )(vec)


def reduce_begin(gs, pos, layer):
    got = reduce_pair(gs)
    sums, own = zip(*[pair_sum(g, o, pos) for g, o in zip(gs, got)])
    send_sems, recv_sems, sums, lands, token = chips_start(list(sums), layer)
    return dict(own=own, sums=sums, lands=lands, sems=(send_sems, recv_sems), token=token, layer=layer)


def reduce_end(pending, pos, after):
    lands = chips_wait(pending["sums"], pending["lands"], *pending["sems"], after, pending["layer"])
    return exchange_halves([chip_sum(o, p, pos) for o, p in zip(pending["own"], lands)])


W_NAMES = ("ffn1_norm", "ffn1_w_gate", "ffn1_w_up", "ffn1_w_down", "mix_norm", "w_in", "conv_w", "conv_b", "ssm_A_re", "ssm_A_im",
           "ssm_B_re", "ssm_B_im", "ssm_C_re", "ssm_C_im", "ssm_D", "ssm_log_dt", "glu_w", "glu_b", "conv_out_norm", "ssm_out_norm",
           "w_out", "ffn2_norm", "ffn2_w_gate", "ffn2_w_up", "ffn2_w_down", "ple_norm", "ple_w_gate", "ple_w_proj", "final_norm")
SMALL_ALL = SMALL + ("final_norm",)
PACK = 8 * 128


def _pack(parts):
    flat = jnp.concatenate([p.reshape(-1) for p in parts])
    pad = (-flat.shape[0]) % PACK
    return jnp.pad(flat, (0, pad)).reshape(-1, 128)


def _unpack(vec, shapes):
    flat = vec.reshape(-1)
    out, off = [], 0
    for shp in shapes:
        size = math.prod(shp)
        out.append(flat[off:off + size].reshape(shp))
        off += size
    return out


def _step(a):
    x, p, target = a["x"][0], a["p"][:, 0], a["loss_target"][0]
    depth = p.shape[0]
    L, D = x.shape
    me_s = 2 * lax.axis_index("x") + lax.axis_index("y")

    pos = jnp.stack([lax.axis_index("c"), me_s]).astype(jnp.int32)
    conv_w = gather_weights([cast_place(a["conv_w"], pos, F32)])[0]
    started = [gather_start([cast_place_layer(a[n], l, pos, BF16) for n in BIG], l) for l in range(depth)]

    def weights_of(l, h):
        send_sems, recv_sems, bufs, _ = started[l]
        after = [h] if l else [s[3] for s in started]
        full = gather_forward(gather_wait(bufs, send_sems, recv_sems, after, l))
        w = {n: a[n][l] for n in SMALL if n != "conv_w"}
        w.update(dict(zip(BIG, full)))
        C = w["glu_w"].shape[-1]
        w["glu_w"] = w["glu_w"].reshape(C, C)
        w["w_out"] = w["w_out"].reshape(2, -1, D)
        w["ple_w_gate"] = w["ple_w_gate"].reshape(D, D)
        w["conv_w"] = conv_w[l].transpose(1, 0, 2).reshape(3, -1)
        return w

    reduced = [None] * depth
    pending = []

    def on_grads(l, big, dh):
        if pending:
            prev = pending.pop()
            reduced[prev["layer"]] = reduce_end(prev, pos, [dh])
        pending.append(reduce_begin(big, pos, l))
        return pending[-1]["token"]

    loss_part, dx, smalls, g_final = local_step(x, p, target, a["final_norm"], weights_of, on_grads)
    last = pending.pop()
    small_shapes = [(depth,) + smalls[0][n].shape for n in SMALL] + [g_final.shape, (1,)]
    parts = [smalls[l][n] for n in SMALL for l in range(depth)] + [g_final, loss_part[0, 0:1] + last["token"][0, 0:1]]
    summed_vec = allreduce_small(_pack(parts))
    reduced[last["layer"]] = reduce_end(last, pos, [summed_vec])
    big_out = {}
    for i, n in enumerate(BIG):
        g = jnp.stack([reduced[l][i] for l in range(depth)])
        delta, new_m, new_v = elementwise(_adamw, [_rows(a[n]), _rows(g), _rows(a["m_" + n]), _rows(a["v_" + n])],
                                          [F32, F32, F32], "adamw")
        big_out[n] = (g, delta.reshape(g.shape), new_m.reshape(g.shape), new_v.reshape(g.shape))

    summed = _unpack(summed_vec, small_shapes)
    g_small = dict(zip(SMALL_ALL, summed[:-1]))
    loss = summed[-1][0]
    n_conv = a["conv_w"].shape[-1]
    g_small["conv_w"] = lax.dynamic_slice_in_dim(g_small["conv_w"], me_s * n_conv, n_conv, axis=2)
    g_small = {n: g_small[n].reshape(a[n].shape) for n in SMALL_ALL}
    packed = [_pack([src[n] for n in SMALL_ALL]) for src in
              ({n: a[n] for n in SMALL_ALL}, g_small, {n: a["m_" + n] for n in SMALL_ALL}, {n: a["v_" + n] for n in SMALL_ALL})]
    shapes = [a[n].shape for n in SMALL_ALL]
    d_s, m_s, v_s = [dict(zip(SMALL_ALL, _unpack(o, shapes))) for o in elementwise(_adamw, packed, [F32, F32, F32], "adamw_small")]

    outs = {n: big_out[n] if n in big_out else (g_small[n], d_s[n], m_s[n], v_s[n]) for n in W_NAMES}
    return (loss, dx[None], *[outs[n][0] for n in W_NAMES], *[outs[n][1] for n in W_NAMES],
            *[outs[n][2] for n in W_NAMES], *[outs[n][3] for n in W_NAMES])


def kernel(x, p, ffn1_norm, ffn1_w_gate, ffn1_w_up, ffn1_w_down, mix_norm, w_in, conv_w, conv_b, ssm_A_re, ssm_A_im, ssm_B_re, ssm_B_im, ssm_C_re, ssm_C_im, ssm_D, ssm_log_dt, glu_w, glu_b, conv_out_norm, ssm_out_norm, w_out, ffn2_norm, ffn2_w_gate, ffn2_w_up, ffn2_w_down, ple_norm, ple_w_gate, ple_w_proj, final_norm, loss_target, m_ffn1_norm, m_ffn1_w_gate, m_ffn1_w_up, m_ffn1_w_down, m_mix_norm, m_w_in, m_conv_w, m_conv_b, m_ssm_A_re, m_ssm_A_im, m_ssm_B_re, m_ssm_B_im, m_ssm_C_re, m_ssm_C_im, m_ssm_D, m_ssm_log_dt, m_glu_w, m_glu_b, m_conv_out_norm, m_ssm_out_norm, m_w_out, m_ffn2_norm, m_ffn2_w_gate, m_ffn2_w_up, m_ffn2_w_down, m_ple_norm, m_ple_w_gate, m_ple_w_proj, m_final_norm, v_ffn1_norm, v_ffn1_w_gate, v_ffn1_w_up, v_ffn1_w_down, v_mix_norm, v_w_in, v_conv_w, v_conv_b, v_ssm_A_re, v_ssm_A_im, v_ssm_B_re, v_ssm_B_im, v_ssm_C_re, v_ssm_C_im, v_ssm_D, v_ssm_log_dt, v_glu_w, v_glu_b, v_conv_out_norm, v_ssm_out_norm, v_w_out, v_ffn2_norm, v_ffn2_w_gate, v_ffn2_w_up, v_ffn2_w_down, v_ple_norm, v_ple_w_gate, v_ple_w_proj, v_final_norm):
    return _step(dict(locals()))
```

```python
import functools
import math

import jax
import jax.numpy as jnp
from jax import lax
from jax.experimental import pallas as pl
from jax.experimental.pallas import tpu as pltpu

F32, BF16 = jnp.float32, jnp.bfloat16
S = jax.ShapeDtypeStruct
EPS = 1e-6
N_SEG = 8
N_SHARD = 4
N_DEV = 8
VMEM_LIMIT_BYTES = 56 * 1024 * 1024
ADAM_LR, ADAM_B1, ADAM_B2, ADAM_EPS, ADAM_WD, ADAM_STEP = 0.001, 0.9, 0.999, 1e-08, 0.01, 10
MESH = pl.DeviceIdType.MESH


def _params(*sem):
    return pltpu.CompilerParams(dimension_semantics=sem if sem else None, vmem_limit_bytes=VMEM_LIMIT_BYTES)


def _dot(a, b, ca, cb):
    return lax.dot_general(a, b, (((ca,), (cb,)), ((), ())), preferred_element_type=F32)


def _sigmoid(x):
    return 1.0 / (1.0 + jnp.exp(-x))


def _rstd(x):
    return lax.rsqrt(jnp.mean(x * x, axis=-1, keepdims=True) + EPS)


def _rms_bwd(x, g, dy):
    r = _rstd(x)
    xh = x * r
    dxh = dy * g
    dx = r * (dxh - xh * jnp.mean(dxh * xh, axis=-1, keepdims=True))
    return dx, jnp.sum(dy * xh, axis=0, keepdims=True)


def _tile(n, want):
    return want if n % want == 0 else n


def rmsnorm_fwd(h, g):
    L, D = h.shape
    tm = _tile(L, 512)

    def body(h_ref, g_ref, o_ref):
        x = h_ref[...]
        o_ref[...] = (x * _rstd(x) * g_ref[...]).astype(BF16)

    return pl.pallas_call(
        body, name="rmsnorm_fwd", grid=(L // tm,),
        in_specs=[pl.BlockSpec((tm, D), lambda m: (m, 0)), pl.BlockSpec((1, D), lambda m: (0, 0))],
        out_specs=pl.BlockSpec((tm, D), lambda m: (m, 0)),
        out_shape=S((L, D), BF16), compiler_params=_params("parallel"),
    )(h, g.reshape(1, D))


def ffn_up(u, wg, wu):
    L, D = u.shape
    ns, F, _ = wg.shape
    tm = _tile(L, 512)

    def body(u_ref, wg_ref, wu_ref, a_ref, b_ref, s_ref):
        x = u_ref[...]
        a = _dot(x, wg_ref[0], 1, 1)
        b = _dot(x, wu_ref[0], 1, 1)
        a_ref[0] = a.astype(BF16)
        b_ref[0] = b.astype(BF16)
        s_ref[0] = (a * _sigmoid(a) * b).astype(BF16)

    w_spec = pl.BlockSpec((1, F, D), lambda s, m: (s, 0, 0))
    o_spec = pl.BlockSpec((1, tm, F), lambda s, m: (s, m, 0))
    return pl.pallas_call(
        body, name="ffn_up", grid=(ns, L // tm),
        in_specs=[pl.BlockSpec((tm, D), lambda s, m: (m, 0)), w_spec, w_spec],
        out_specs=[o_spec, o_spec, o_spec],
        out_shape=[S((ns, L, F), BF16)] * 3, compiler_params=_params("parallel", "parallel"),
    )(u, wg, wu)


def mm_shard_n(u, w3, out_dtype):
    L, K = u.shape
    ns, _, N = w3.shape
    tm = _tile(L, 512)

    def body(u_ref, w_ref, o_ref):
        o_ref[0] = _dot(u_ref[...], w_ref[0], 1, 0).astype(out_dtype)

    return pl.pallas_call(
        body, name="mm_shard_n", grid=(ns, L // tm),
        in_specs=[pl.BlockSpec((tm, K), lambda s, m: (m, 0)), pl.BlockSpec((1, K, N), lambda s, m: (s, 0, 0))],
        out_specs=pl.BlockSpec((1, tm, N), lambda s, m: (s, m, 0)),
        out_shape=S((ns, L, N), out_dtype), compiler_params=_params("parallel", "parallel"),
    )(u, w3)


def mm_shard_k(a3, w3, res, scale, g_next):
    nk, L, Kc = a3.shape
    N = w3.shape[2]
    tm = _tile(L, 512)

    def body(a_ref, w_ref, r_ref, g_ref, o_ref, u_ref, acc):
        k = pl.program_id(1)
        part = _dot(a_ref[0], w_ref[0], 1, 0)

        @pl.when(k == 0)
        def _():
            acc[...] = part

        @pl.when(k > 0)
        def _():
            acc[...] += part

        @pl.when(k == nk - 1)
        def _():
            h = r_ref[...] + scale * acc[...]
            o_ref[...] = h
            u_ref[...] = (h * _rstd(h) * g_ref[...]).astype(BF16)

    tile = pl.BlockSpec((tm, N), lambda m, k: (m, 0))
    return pl.pallas_call(
        body, name="mm_shard_k", grid=(L // tm, nk),
        in_specs=[pl.BlockSpec((1, tm, Kc), lambda m, k: (k, m, 0)), pl.BlockSpec((1, Kc, N), lambda m, k: (k, 0, 0)),
                  tile, pl.BlockSpec((1, N), lambda m, k: (0, 0))],
        out_specs=[tile, tile],
        out_shape=[S((L, N), F32), S((L, N), BF16)], scratch_shapes=[pltpu.VMEM((tm, N), F32)],
        compiler_params=_params("parallel", "arbitrary"),
    )(a3, w3, res, g_next.reshape(1, N))


CONV_HALO = 8


def _conv_specs(L, tm, C, shard):
    nb = L // CONV_HALO
    per = tm // CONV_HALO
    main = pl.BlockSpec((1, tm, C), lambda m: (shard, m, 0))
    prev = pl.BlockSpec((1, CONV_HALO, C), lambda m: (shard, jnp.maximum(m * per - 1, 0), 0))
    nxt = pl.BlockSpec((1, CONV_HALO, C), lambda m: (shard, jnp.minimum((m + 1) * per, nb - 1), 0))
    return main, prev, nxt


def _conv_core(zb, zc, zv, w_ref, bias, grow, L):
    valid = (grow >= 0) & (grow < L)
    v = jnp.where(valid, zc * zv, 0.0)
    v1 = pltpu.roll(v, 1, 0)
    v2 = pltpu.roll(v, 2, 0)
    cb = w_ref[0:1, :] * v2 + w_ref[1:2, :] * v1 + w_ref[2:3, :] * v + bias
    return valid, v, v1, v2, cb, zb * cb


def conv_fwd(z, conv_w, conv_b, gnorm):
    _, L, C = z.shape
    tm = _tile(L, 256)
    H = CONV_HALO

    def body(zb_ref, zc_ref, zcp_ref, zv_ref, zvp_ref, w_ref, b_ref, g_ref, o_ref):
        m = pl.program_id(0)
        zc = jnp.concatenate([zcp_ref[0], zc_ref[0]], axis=0)
        zv = jnp.concatenate([zvp_ref[0], zv_ref[0]], axis=0)
        grow = m * tm - H + lax.broadcasted_iota(jnp.int32, (tm + H, C), 0)
        valid = grow >= 0
        v = jnp.where(valid, zc * zv, 0.0)
        v1 = pltpu.roll(v, 1, 0)
        v2 = pltpu.roll(v, 2, 0)
        cb = (w_ref[0:1, :] * v2 + w_ref[1:2, :] * v1 + w_ref[2:3, :] * v + b_ref[...])[H:, :]
        ya = zb_ref[0] * cb
        o_ref[...] = (ya * _rstd(ya) * g_ref[...]).astype(BF16)

    zb_m, _, _ = _conv_specs(L, tm, C, 0)
    zc_m, zc_p, _ = _conv_specs(L, tm, C, 1)
    zv_m, zv_p, _ = _conv_specs(L, tm, C, 2)
    row = lambda r: pl.BlockSpec((r, C), lambda m: (0, 0))
    return pl.pallas_call(
        body, name="conv_fwd", grid=(L // tm,),
        in_specs=[zb_m, zc_m, zc_p, zv_m, zv_p, row(3), row(1), row(1)],
        out_specs=pl.BlockSpec((tm, C), lambda m: (m, 0)),
        out_shape=S((L, C), BF16), compiler_params=_params("parallel"),
    )(z, z, z, z, z, conv_w, conv_b.reshape(1, C), gnorm.reshape(1, C))


def _cmul(ar, ai, br, bi):
    return ar * br - ai * bi, ar * bi + ai * br


def _scan_fwd(hr_ref, hi_ref, lr, li, n_steps):
    W = hr_ref.shape[1]
    zero = jnp.zeros((N_SEG, W), F32)

    def local(t, c):
        r = pl.multiple_of(t * N_SEG, N_SEG)
        pr, pi = _cmul(lr, li, c[0], c[1])
        nr = pr + hr_ref[pl.ds(r, N_SEG), :]
        ni = pi + hi_ref[pl.ds(r, N_SEG), :]
        hr_ref[pl.ds(r, N_SEG), :] = nr
        hi_ref[pl.ds(r, N_SEG), :] = ni
        return nr, ni

    fr, fi = lax.fori_loop(0, n_steps, local, (zero, zero))
    qr, qi = _cpow(lr, li, n_steps)
    row = lax.broadcasted_iota(jnp.int32, (N_SEG, W), 0)
    cr, ci = zero, zero
    for seg in range(1, N_SEG):
        tr, ti = _cmul(qr, qi, cr, ci)
        sr = pltpu.roll(fr + tr, 1, 0)
        si = pltpu.roll(fi + ti, 1, 0)
        cr = jnp.where(row == seg, sr, cr)
        ci = jnp.where(row == seg, si, ci)

    def fix(t, c):
        r = pl.multiple_of(t * N_SEG, N_SEG)
        pr, pi = _cmul(lr, li, c[0], c[1])
        ar, ai = _cmul(pr, pi, cr, ci)
        hr_ref[pl.ds(r, N_SEG), :] += ar
        hi_ref[pl.ds(r, N_SEG), :] += ai
        return pr, pi

    lax.fori_loop(0, n_steps, fix, (jnp.ones((N_SEG, W), F32), zero))


def _cpow(lr, li, n):
    rr, ri = None, None
    br, bi = lr, li
    while n:
        if n & 1:
            rr, ri = (br, bi) if rr is None else _cmul(rr, ri, br, bi)
        n >>= 1
        if n:
            br, bi = _cmul(br, bi, br, bi)
    return rr, ri


def _ssm_specs(L):
    col = lambda w: pl.BlockSpec((L, w), lambda j: (0, j))
    return dict(
        u=col(128), lam=pl.BlockSpec((2, 512), lambda j: (0, j)),
        bmat=pl.BlockSpec((1, 128, 512), lambda j: (j, 0, 0)), cmat=pl.BlockSpec((1, 512, 128), lambda j: (j, 0, 0)),
        d=pl.BlockSpec((1, 128), lambda j: (0, j)))


def ssm_fwd(us, lam, bre, bim, cre, cim, dvec):
    L = us.shape[0]
    n_steps = L // N_SEG
    sp = _ssm_specs(L)

    def body(u_ref, lam_ref, bre_ref, bim_ref, cre_ref, cim_ref, d_ref, y_ref, hr, hi):
        u = u_ref[...]
        ub = u.astype(BF16)
        hr[...] = _dot(ub, bre_ref[0], 1, 0)
        hi[...] = _dot(ub, bim_ref[0], 1, 0)
        lr = jnp.broadcast_to(lam_ref[0:1, :], (N_SEG, 512))
        li = jnp.broadcast_to(lam_ref[1:2, :], (N_SEG, 512))
        _scan_fwd(hr, hi, lr, li, n_steps)
        y_ref[...] = (_dot(hr[...].astype(BF16), cre_ref[0], 1, 0) - _dot(hi[...].astype(BF16), cim_ref[0], 1, 0)
                      + d_ref[...] * u)

    return pl.pallas_call(
        body, name="ssm_fwd", grid=(4,),
        in_specs=[sp["u"], sp["lam"], sp["bmat"], sp["bmat"], sp["cmat"], sp["cmat"], sp["d"]],
        out_specs=sp["u"], out_shape=S((L, 512), F32),
        scratch_shapes=[pltpu.VMEM((L, 512), F32), pltpu.VMEM((L, 512), F32)],
        compiler_params=_params("parallel"),
    )(us, lam, bre, bim, cre, cim, dvec)


_GELU_C = math.sqrt(2.0 / math.pi)


def _gelu(y):
    t = jnp.tanh(_GELU_C * (y + 0.044715 * y * y * y))
    return 0.5 * y * (1.0 + t), t


def glu_fwd(y, w, b, gnorm):
    L, C = y.shape
    tm = _tile(L, 512)

    def body(y_ref, w_ref, b_ref, g_ref, o_ref):
        zg, _ = _gelu(y_ref[...])
        out = zg * _sigmoid(_dot(zg.astype(BF16), w_ref[...], 1, 0) + b_ref[...])
        o_ref[...] = (out * _rstd(out) * g_ref[...]).astype(BF16)

    row = pl.BlockSpec((1, C), lambda m: (0, 0))
    return pl.pallas_call(
        body, name="glu_fwd", grid=(L // tm,),
        in_specs=[pl.BlockSpec((tm, C), lambda m: (m, 0)), pl.BlockSpec((C, C), lambda m: (0, 0)), row, row],
        out_specs=pl.BlockSpec((tm, C), lambda m: (m, 0)),
        out_shape=S((L, C), BF16), compiler_params=_params("parallel"),
    )(y, w, b.reshape(1, C), gnorm.reshape(1, C))


def _ple_specs(L, D, P, tm, nb):
    return [pl.BlockSpec((tm, D), lambda n, m: (m, 0)), pl.BlockSpec((tm, P), lambda n, m: (m, 0)),
            pl.BlockSpec((D, nb), lambda n, m: (0, n)), pl.BlockSpec((1, P, nb), lambda n, m: (n, 0, 0)),
            pl.BlockSpec((tm, nb), lambda n, m: (m, n))]


def ple_fwd(un, pb, wpg, wpp, h):
    L, D = un.shape
    ns, P, nb = wpp.shape
    tm = _tile(L, 512)

    def body(un_ref, p_ref, wg_ref, wp_ref, h_ref, o_ref):
        gate = _sigmoid(_dot(un_ref[...], wg_ref[...], 1, 0))
        o_ref[...] = h_ref[...] + _dot(p_ref[...], wp_ref[0], 1, 0) * gate

    return pl.pallas_call(
        body, name="ple_fwd", grid=(ns, L // tm),
        in_specs=_ple_specs(L, D, P, tm, nb),
        out_specs=pl.BlockSpec((tm, nb), lambda n, m: (m, n)),
        out_shape=S((L, D), F32), compiler_params=_params("parallel", "parallel"),
    )(un, pb, wpg, wpp, h)


def loss_head(h, g, target):
    L, D = h.shape
    tm = _tile(L, 256)

    def body(h_ref, g_ref, t_ref, loss_ref, dh_ref, dg_ref):
        m = pl.program_id(0)
        x = h_ref[...]
        gg = g_ref[...]
        e = x * _rstd(x) * gg - t_ref[...]
        dx, dg = _rms_bwd(x, gg, e * (1.0 / D))
        dh_ref[...] = dx
        part = jnp.full((8, 128), 0.5 / D, F32) * jnp.sum(e * e)

        @pl.when(m == 0)
        def _():
            loss_ref[...] = part
            dg_ref[...] = dg

        @pl.when(m > 0)
        def _():
            loss_ref[...] += part
            dg_ref[...] += dg

    return pl.pallas_call(
        body, name="loss_head", grid=(L // tm,),
        in_specs=[pl.BlockSpec((tm, D), lambda m: (m, 0)), pl.BlockSpec((1, D), lambda m: (0, 0)),
                  pl.BlockSpec((tm, D), lambda m: (m, 0))],
        out_specs=[pl.BlockSpec((8, 128), lambda m: (0, 0)), pl.BlockSpec((tm, D), lambda m: (m, 0)),
                   pl.BlockSpec((1, D), lambda m: (0, 0))],
        out_shape=[S((8, 128), F32), S((L, D), F32), S((1, D), F32)],
        compiler_params=_params("arbitrary"),
    )(h, g.reshape(1, D), target)


def ple_bwd(un, pb, wpg, wpp, dh, token):
    L, D = un.shape
    ns, P, nb = wpp.shape
    tm = _tile(L, 512)

    def body(un_ref, p_ref, wg_ref, wp_ref, dh_ref, tok_ref, dpre_ref, dpp_ref):
        gate = _sigmoid(_dot(un_ref[...], wg_ref[...], 1, 0))
        pp = _dot(p_ref[...], wp_ref[0], 1, 0)
        d = dh_ref[...] + tok_ref[0:1, 0:1]
        dpp_ref[0] = (d * gate).astype(BF16)
        dpre_ref[...] = (d * pp * gate * (1.0 - gate)).astype(BF16)

    return pl.pallas_call(
        body, name="ple_bwd", grid=(ns, L // tm),
        in_specs=_ple_specs(L, D, P, tm, nb) + [pl.BlockSpec((8, 128), lambda n, m: (0, 0))],
        out_specs=[pl.BlockSpec((tm, nb), lambda n, m: (m, n)), pl.BlockSpec((1, tm, nb), lambda n, m: (n, m, 0))],
        out_shape=[S((L, D), BF16), S((ns, L, nb), BF16)], compiler_params=_params("parallel", "parallel"),
    )(un, pb, wpg, wpp, dh, token)


def wgrad(a, b):
    a3 = a if a.ndim == 3 else a[None]
    b3 = b if b.ndim == 3 else b[None]
    ns = max(a3.shape[0], b3.shape[0])
    _, L, Ka = a3.shape
    N = b3.shape[2]
    a_map = (lambda s: (s, 0, 0)) if a3.shape[0] > 1 else (lambda s: (0, 0, 0))
    b_map = (lambda s: (s, 0, 0)) if b3.shape[0] > 1 else (lambda s: (0, 0, 0))

    def body(a_ref, b_ref, o_ref):
        o_ref[0] = _dot(a_ref[0], b_ref[0], 0, 0)

    return pl.pallas_call(
        body, name="wgrad", grid=(ns,),
        in_specs=[pl.BlockSpec((1, L, Ka), a_map), pl.BlockSpec((1, L, N), b_map)],
        out_specs=pl.BlockSpec((1, Ka, N), lambda s: (s, 0, 0)),
        out_shape=S((ns, Ka, N), F32), compiler_params=_params("parallel"),
    )(a3, b3)


def dx_rms(pairs, h, g, dh_in, cast_scale):
    L, D = h.shape
    nk = pairs[0][0].shape[0]
    n_pairs = len(pairs)
    tm = _tile(L, 512)
    n_m = L // tm
    w_dims = [0 if transposed else 1 for _, _, transposed in pairs]

    def body(*refs):
        ins, (h_ref, g_ref, dhi_ref, dho_ref, dhb_ref, dg_ref, acc) = refs[:2 * n_pairs], refs[2 * n_pairs:]
        m, k = pl.program_id(0), pl.program_id(1)
        part = _dot(ins[0][0], ins[1][0], 1, w_dims[0])
        for i in range(1, n_pairs):
            part += _dot(ins[2 * i][0], ins[2 * i + 1][0], 1, w_dims[i])

        @pl.when(k == 0)
        def _():
            acc[...] = part

        @pl.when(k > 0)
        def _():
            acc[...] += part

        @pl.when(k == nk - 1)
        def _():
            dx, dg = _rms_bwd(h_ref[...], g_ref[...], acc[...])
            dh_out = dhi_ref[...] + dx
            dho_ref[...] = dh_out
            dhb_ref[...] = (cast_scale * dh_out).astype(BF16)

            @pl.when(m == 0)
            def _():
                dg_ref[...] = dg

            @pl.when(m > 0)
            def _():
                dg_ref[...] += dg

    in_specs, args = [], []
    for a3, w3, _ in pairs:
        Kc = a3.shape[2]
        in_specs += [pl.BlockSpec((1, tm, Kc), lambda m, k: (k, m, 0)), pl.BlockSpec((1,) + w3.shape[1:], lambda m, k: (k, 0, 0))]
        args += [a3, w3]
    tile = pl.BlockSpec((tm, D), lambda m, k: (m, 0))
    row = pl.BlockSpec((1, D), lambda m, k: (0, 0))
    return pl.pallas_call(
        body, name="dx_rms", grid=(n_m, nk),
        in_specs=in_specs + [tile, row, tile], out_specs=[tile, tile, row],
        out_shape=[S((L, D), F32), S((L, D), BF16), S((1, D), F32)], scratch_shapes=[pltpu.VMEM((tm, D), F32)],
        compiler_params=_params("arbitrary", "arbitrary"),
    )(*args, h, g.reshape(1, D), dh_in)


def dact_plain(dhb, w3):
    L, D = dhb.shape
    ns, N, _ = w3.shape
    tm = _tile(L, 512)

    def body(d_ref, w_ref, o_ref):
        o_ref[0] = _dot(d_ref[...], w_ref[0], 1, 1)

    return pl.pallas_call(
        body, name="dact_plain", grid=(ns, L // tm),
        in_specs=[pl.BlockSpec((tm, D), lambda s, m: (m, 0)), pl.BlockSpec((1, N, D), lambda s, m: (s, 0, 0))],
        out_specs=pl.BlockSpec((1, tm, N), lambda s, m: (s, m, 0)),
        out_shape=S((ns, L, N), F32), compiler_params=_params("parallel", "parallel"),
    )(dhb, w3)


def dact_swiglu(dhb, wd, a3, b3):
    L, D = dhb.shape
    ns, F, _ = wd.shape
    tm = _tile(L, 512)

    def body(d_ref, w_ref, a_ref, b_ref, da_ref, db_ref):
        ds = _dot(d_ref[...], w_ref[0], 1, 1)
        a = a_ref[0].astype(F32)
        b = b_ref[0].astype(F32)
        sg = _sigmoid(a)
        da_ref[0] = (ds * b * (sg * (1.0 + a * (1.0 - sg)))).astype(BF16)
        db_ref[0] = (ds * (a * sg)).astype(BF16)

    t_spec = pl.BlockSpec((1, tm, F), lambda s, m: (s, m, 0))
    return pl.pallas_call(
        body, name="dact_swiglu", grid=(ns, L // tm),
        in_specs=[pl.BlockSpec((tm, D), lambda s, m: (m, 0)), pl.BlockSpec((1, F, D), lambda s, m: (s, 0, 0)), t_spec, t_spec],
        out_specs=[t_spec, t_spec], out_shape=[S((ns, L, F), BF16)] * 2,
        compiler_params=_params("parallel", "parallel"),
    )(dhb, wd, a3, b3)


def conv_bwd(z, conv_w, conv_b, gnorm, dyn):
    _, L, C = z.shape
    tm = _tile(L, 256)
    H = CONV_HALO
    T = tm + 2 * H

    def body(zb_ref, zbp_ref, zbn_ref, zc_ref, zcp_ref, zcn_ref, zv_ref, zvp_ref, zvn_ref, d_ref, dp_ref, dn_ref,
             w_ref, b_ref, g_ref, dz_ref, dw_ref, db_ref, dg_ref):
        m = pl.program_id(0)
        cat = lambda p, c, n: jnp.concatenate([p[0], c[0], n[0]], axis=0)
        zb, zc, zv, d = cat(zbp_ref, zb_ref, zbn_ref), cat(zcp_ref, zc_ref, zcn_ref), cat(zvp_ref, zv_ref, zvn_ref), cat(dp_ref, d_ref, dn_ref)
        grow = m * tm - H + lax.broadcasted_iota(jnp.int32, (T, C), 0)
        valid, v, v1, v2, cb, ya = _conv_core(zb, zc, zv, w_ref, b_ref[...], grow, L)
        dya, _ = _rms_bwd(ya, g_ref[...], d)
        dc = jnp.where(valid, dya * zb, 0.0)
        dv = w_ref[2:3, :] * dc + w_ref[1:2, :] * pltpu.roll(dc, T - 1, 0) + w_ref[0:1, :] * pltpu.roll(dc, T - 2, 0)
        dz_ref[0] = (dya * cb)[H:H + tm, :].astype(BF16)
        dz_ref[1] = (dv * zv)[H:H + tm, :].astype(BF16)
        dz_ref[2] = (dv * zc)[H:H + tm, :].astype(BF16)
        rs = lambda x: jnp.sum(x[H:H + tm, :], axis=0, keepdims=True)
        yh = ya * _rstd(ya)
        dw = jnp.concatenate([rs(dc * v2), rs(dc * v1), rs(dc * v)], axis=0)
        dbias, dg = rs(dc), rs(d * yh)

        @pl.when(m == 0)
        def _():
            dw_ref[...] = dw
            db_ref[...] = dbias
            dg_ref[...] = dg

        @pl.when(m > 0)
        def _():
            dw_ref[...] += dw
            db_ref[...] += dbias
            dg_ref[...] += dg

    row = lambda r: pl.BlockSpec((r, C), lambda m: (0, 0))
    specs = [*_conv_specs(L, tm, C, 0), *_conv_specs(L, tm, C, 1), *_conv_specs(L, tm, C, 2), *_conv_specs(L, tm, C, 0)]
    return pl.pallas_call(
        body, name="conv_bwd", grid=(L // tm,),
        in_specs=specs + [row(3), row(1), row(1)],
        out_specs=[pl.BlockSpec((3, tm, C), lambda m: (0, m, 0)), row(3), row(1), row(1)],
        out_shape=[S((3, L, C), BF16), S((3, C), F32), S((1, C), F32), S((1, C), F32)],
        compiler_params=_params("arbitrary"),
    )(z, z, z, z, z, z, z, z, z, dyn, dyn, dyn, conv_w, conv_b.reshape(1, C), gnorm.reshape(1, C))


def glu_bwd(y, w, b, gnorm, dn):
    L, C = y.shape
    tm = _tile(L, 256)

    def body(y_ref, w_ref, b_ref, g_ref, d_ref, dy_ref, dpre_ref, zg_ref, db_ref, dg_ref):
        m = pl.program_id(0)
        yv = y_ref[...]
        zg, t = _gelu(yv)
        zgb = zg.astype(BF16)
        sg = _sigmoid(_dot(zgb, w_ref[...], 1, 0) + b_ref[...])
        out = zg * sg
        dout, dg = _rms_bwd(out, g_ref[...], d_ref[...])
        dpre = dout * zg * sg * (1.0 - sg)
        dpre_b = dpre.astype(BF16)
        dzg = dout * sg + _dot(dpre_b, w_ref[...], 1, 1)
        dt = (1.0 - t * t) * _GELU_C * (1.0 + 3.0 * 0.044715 * yv * yv)
        dy_ref[...] = dzg * (0.5 * (1.0 + t) + 0.5 * yv * dt)
        dpre_ref[...] = dpre_b
        zg_ref[...] = zgb
        dbias = jnp.sum(dpre, axis=0, keepdims=True)

        @pl.when(m == 0)
        def _():
            db_ref[...] = dbias
            dg_ref[...] = dg

        @pl.when(m > 0)
        def _():
            db_ref[...] += dbias
            dg_ref[...] += dg

    tile = pl.BlockSpec((tm, C), lambda m: (m, 0))
    row = pl.BlockSpec((1, C), lambda m: (0, 0))
    return pl.pallas_call(
        body, name="glu_bwd", grid=(L // tm,),
        in_specs=[tile, pl.BlockSpec((C, C), lambda m: (0, 0)), row, row, tile],
        out_specs=[tile, tile, tile, row, row],
        out_shape=[S((L, C), F32), S((L, C), BF16), S((L, C), BF16), S((1, C), F32), S((1, C), F32)],
        compiler_params=_params("arbitrary"),
    )(y, w, b.reshape(1, C), gnorm.reshape(1, C), dn)


def _scan_bwd(gr_ref, gi_ref, hr_ref, hi_ref, lr, li, n_steps):
    W = gr_ref.shape[1]
    zero = jnp.zeros((N_SEG, W), F32)
    lic = -li

    def local(i, c):
        r = pl.multiple_of((n_steps - 1 - i) * N_SEG, N_SEG)
        pr, pi = _cmul(lr, lic, c[0], c[1])
        nr = pr + gr_ref[pl.ds(r, N_SEG), :]
        ni = pi + gi_ref[pl.ds(r, N_SEG), :]
        gr_ref[pl.ds(r, N_SEG), :] = nr
        gi_ref[pl.ds(r, N_SEG), :] = ni
        return nr, ni

    fr, fi = lax.fori_loop(0, n_steps, local, (zero, zero))
    qr, qi = _cpow(lr, lic, n_steps)
    row = lax.broadcasted_iota(jnp.int32, (N_SEG, W), 0)
    cr, ci = zero, zero
    for seg in range(N_SEG - 2, -1, -1):
        tr, ti = _cmul(qr, qi, cr, ci)
        sr = pltpu.roll(fr + tr, N_SEG - 1, 0)
        si = pltpu.roll(fi + ti, N_SEG - 1, 0)
        cr = jnp.where(row == seg, sr, cr)
        ci = jnp.where(row == seg, si, ci)

    def fix(i, c):
        pwr, pwi, ar, ai = c
        t = n_steps - 1 - i
        r = pl.multiple_of(t * N_SEG, N_SEG)
        pwr, pwi = _cmul(lr, lic, pwr, pwi)
        xr, xi = _cmul(pwr, pwi, cr, ci)
        g_r = gr_ref[pl.ds(r, N_SEG), :] + xr
        g_i = gi_ref[pl.ds(r, N_SEG), :] + xi
        gr_ref[pl.ds(r, N_SEG), :] = g_r
        gi_ref[pl.ds(r, N_SEG), :] = g_i
        rp = pl.multiple_of(jnp.maximum(t - 1, 0) * N_SEG, N_SEG)
        hpr = hr_ref[pl.ds(rp, N_SEG), :]
        hpi = hi_ref[pl.ds(rp, N_SEG), :]
        live = t > 0
        ar = ar + jnp.where(live, hpr * g_r + hpi * g_i, 0.0)
        ai = ai + jnp.where(live, hpr * g_i - hpi * g_r, 0.0)
        return pwr, pwi, ar, ai

    _, _, ar, ai = lax.fori_loop(0, n_steps, fix, (jnp.ones((N_SEG, W), F32), zero, zero, zero))
    last = pl.ds((n_steps - 1) * N_SEG, N_SEG)
    hpr = jnp.where(row == 0, 0.0, pltpu.roll(hr_ref[last, :], 1, 0))
    hpi = jnp.where(row == 0, 0.0, pltpu.roll(hi_ref[last, :], 1, 0))
    g_r, g_i = gr_ref[pl.ds(0, N_SEG), :], gi_ref[pl.ds(0, N_SEG), :]
    ar = ar + hpr * g_r + hpi * g_i
    ai = ai + hpr * g_i - hpi * g_r
    return jnp.sum(ar, axis=0, keepdims=True), jnp.sum(ai, axis=0, keepdims=True)


def ssm_bwd(us, dy, lam, bre, bim, cre, cim, dvec):
    L = us.shape[0]
    n_steps = L // N_SEG
    sp = _ssm_specs(L)

    def body(u_ref, dy_ref, lam_ref, bre_ref, bim_ref, cre_ref, cim_ref, d_ref,
             du_ref, dlam_ref, dbre_ref, dbim_ref, dcre_ref, dcim_ref, dd_ref, hr, hi, gr, gi):
        u = u_ref[...]
        ub = u.astype(BF16)
        dyv = dy_ref[...]
        dyb = dyv.astype(BF16)
        hr[...] = _dot(ub, bre_ref[0], 1, 0)
        hi[...] = _dot(ub, bim_ref[0], 1, 0)
        lr = jnp.broadcast_to(lam_ref[0:1, :], (N_SEG, 512))
        li = jnp.broadcast_to(lam_ref[1:2, :], (N_SEG, 512))
        _scan_fwd(hr, hi, lr, li, n_steps)
        dcre_ref[0] = _dot(hr[...].astype(BF16), dyb, 0, 0)
        dcim_ref[0] = -_dot(hi[...].astype(BF16), dyb, 0, 0)
        gr[...] = _dot(dyb, cre_ref[0], 1, 1)
        gi[...] = -_dot(dyb, cim_ref[0], 1, 1)
        dlr, dli = _scan_bwd(gr, gi, hr, hi, lr, li, n_steps)
        dlam_ref[...] = jnp.concatenate([dlr, dli], axis=0)
        grb, gib = gr[...].astype(BF16), gi[...].astype(BF16)
        du_ref[...] = _dot(grb, bre_ref[0], 1, 1) + _dot(gib, bim_ref[0], 1, 1) + d_ref[...] * dyv
        dbre_ref[0] = _dot(ub, grb, 0, 0)
        dbim_ref[0] = _dot(ub, gib, 0, 0)
        dd_ref[...] = jnp.sum(dyv * u, axis=0, keepdims=True)

    big = pltpu.VMEM((L, 512), F32)
    return pl.pallas_call(
        body, name="ssm_bwd", grid=(4,),
        in_specs=[sp["u"], sp["u"], sp["lam"], sp["bmat"], sp["bmat"], sp["cmat"], sp["cmat"], sp["d"]],
        out_specs=[sp["u"], sp["lam"], sp["bmat"], sp["bmat"], sp["cmat"], sp["cmat"], sp["d"]],
        out_shape=[S((L, 512), F32), S((2, 2048), F32), S((4, 128, 512), F32), S((4, 128, 512), F32),
                   S((4, 512, 128), F32), S((4, 512, 128), F32), S((1, 512), F32)],
        scratch_shapes=[big, big, big, big], compiler_params=_params("parallel"),
    )(us, dy, lam, bre, bim, cre, cim, dvec)


def _discretize(ar, ai, log_dt, br, bi):
    dt = jnp.exp(log_dt)
    mag = jnp.exp(ar * dt)
    ph = ai * dt
    lr, li = mag * jnp.cos(ph), mag * jnp.sin(ph)
    nr, ni = lr - 1.0, li
    den = ar * ar + ai * ai
    fr = (nr * ar + ni * ai) / den
    fi = (ni * ar - nr * ai) / den
    return lr, li, fr[..., None] * br - fi[..., None] * bi, fr[..., None] * bi + fi[..., None] * br


def ssm_prep(ar, ai, log_dt, br, bi):
    G, P, H = br.shape

    def body(ar_ref, ai_ref, dt_ref, br_ref, bi_ref, lr_ref, li_ref, bbr_ref, bbi_ref):
        lr_ref[...], li_ref[...], bbr_ref[...], bbi_ref[...] = _discretize(
            ar_ref[...], ai_ref[...], dt_ref[...], br_ref[...], bi_ref[...])

    return pl.pallas_call(
        body, name="ssm_prep",
        out_shape=[S((G, P), F32), S((G, P), F32), S((G, P, H), F32), S((G, P, H), F32)],
    )(ar, ai, log_dt.reshape(G, 1), br, bi)


def ssm_prep_bwd(ar, ai, log_dt, br, bi, dlr, dli, dbbr, dbbi):
    G, P, H = br.shape

    def body(ar_ref, ai_ref, dt_ref, br_ref, bi_ref, dlr_ref, dli_ref, dbbr_ref, dbbi_ref,
             dar_ref, dai_ref, ddt_ref, dbr_ref, dbi_ref):
        _, vjp = jax.vjp(_discretize, ar_ref[...], ai_ref[...], dt_ref[...], br_ref[...], bi_ref[...])
        dar_ref[...], dai_ref[...], ddt_ref[...], dbr_ref[...], dbi_ref[...] = vjp(
            (dlr_ref[...], dli_ref[...], dbbr_ref[...], dbbi_ref[...]))

    return pl.pallas_call(
        body, name="ssm_prep_bwd",
        out_shape=[S((G, P), F32), S((G, P), F32), S((G, 1), F32), S((G, P, H), F32), S((G, P, H), F32)],
    )(ar, ai, log_dt.reshape(G, 1), br, bi, dlr, dli, dbbr, dbbi)


def _block_diag(x):
    j, n, R, C = x.shape
    eye = jnp.eye(n, dtype=x.dtype)
    return (x[:, :, :, None, :] * eye[None, :, None, :, None]).reshape(j, n * R, n * C)


def _block_diag_take(x, R, C):
    j = x.shape[0]
    n = x.shape[1] // R
    x5 = x.reshape(j, n, R, n, C)
    return jnp.stack([x5[:, i, :, i, :] for i in range(n)], axis=1)


def _to_segments(x):
    L, C = x.shape
    return x.reshape(N_SEG, L // N_SEG, C).transpose(1, 0, 2).reshape(L, C)


def _from_segments(x):
    L, C = x.shape
    return x.reshape(L // N_SEG, N_SEG, C).transpose(1, 0, 2).reshape(L, C)


BIG = ("ffn1_w_gate", "ffn1_w_up", "ffn1_w_down", "w_in", "glu_w", "w_out",
       "ffn2_w_gate", "ffn2_w_up", "ffn2_w_down", "ple_w_gate", "ple_w_proj")
SMALL = ("ffn1_norm", "mix_norm", "conv_w", "conv_b", "ssm_A_re", "ssm_A_im", "ssm_B_re", "ssm_B_im", "ssm_C_re", "ssm_C_im",
         "ssm_D", "ssm_log_dt", "glu_b", "conv_out_norm", "ssm_out_norm", "ffn2_norm", "ple_norm")


def _ssm_mats(w):
    G, P, H = w["ssm_B_re"].shape
    lr, li, bbr, bbi = ssm_prep(w["ssm_A_re"], w["ssm_A_im"], w["ssm_log_dt"], w["ssm_B_re"], w["ssm_B_im"])
    lam = jnp.stack([lr.reshape(G * P), li.reshape(G * P)])
    bmat = lambda bb: _block_diag(bb.reshape(4, G // 4, P, H).transpose(0, 1, 3, 2)).astype(BF16)
    cmat = lambda c: _block_diag(c.reshape(4, G // 4, H, P).transpose(0, 1, 3, 2)).astype(BF16)
    return lam, bmat(bbr), bmat(bbi), cmat(w["ssm_C_re"]), cmat(w["ssm_C_im"]), w["ssm_D"].reshape(1, G * H)


def layer_fwd(h0, pb, w):
    L, D = h0.shape
    u1 = rmsnorm_fwd(h0, w["ffn1_norm"])
    a1, b1, s1 = ffn_up(u1, w["ffn1_w_gate"], w["ffn1_w_up"])
    h1, u2 = mm_shard_k(s1, w["ffn1_w_down"], h0, 0.5, w["mix_norm"])
    z = mm_shard_n(u2, w["w_in"], F32)
    ya_n = conv_fwd(z, w["conv_w"], w["conv_b"], w["conv_out_norm"])
    us = _to_segments(z[3])
    mats = _ssm_mats(w)
    y = ssm_fwd(us, *mats)
    ys_n = glu_fwd(y, w["glu_w"], w["glu_b"], w["ssm_out_norm"])
    ycat = jnp.stack([ya_n, _from_segments(ys_n)])
    h2, u3 = mm_shard_k(ycat, w["w_out"], h1, 1.0, w["ffn2_norm"])
    a2, b2, s2 = ffn_up(u3, w["ffn2_w_gate"], w["ffn2_w_up"])
    h3, un = mm_shard_k(s2, w["ffn2_w_down"], h2, 0.5, w["ple_norm"])
    h4 = ple_fwd(un, pb, w["ple_w_gate"], w["ple_w_proj"], h3)
    saved = dict(h0=h0, u1=u1, a1=a1, b1=b1, s1=s1, h1=h1, u2=u2, z=z, us=us, mats=mats, y=y, ycat=ycat,
                 h2=h2, u3=u3, a2=a2, b2=b2, s2=s2, h3=h3, un=un)
    return h4, saved


def _ffn_bwd(dh, dhb, h_in, u, a, b, s, wg, wu, wd, gnorm, cast_scale):
    da, db = dact_swiglu(dhb, wd, a, b)
    g_wd = wgrad(s, dhb)
    g_wg = wgrad(da, u)
    g_wu = wgrad(db, u)
    dh_in, dhb_in, g_norm = dx_rms([(da, wg, True), (db, wu, True)], h_in, gnorm, dh, cast_scale)
    return dh_in, dhb_in, g_wg, g_wu, g_wd, g_norm


def layer_bwd(dh, pb, w, sv, token):
    L, D = dh.shape
    G, P, H = w["ssm_B_re"].shape
    dpre, dpp3 = ple_bwd(sv["un"], pb, w["ple_w_gate"], w["ple_w_proj"], dh, token)
    g_wpg = wgrad(sv["un"], dpre).reshape(N_SHARD, D // N_SHARD, D)
    g_wpp = wgrad(pb, dpp3)
    dh3, dhb3, g_nple = dx_rms([(dpre[None], w["ple_w_gate"][None], False)], sv["h3"], w["ple_norm"], dh, 0.5)
    dh2, dhb, g_wg2, g_wu2, g_wd2, g_nffn2 = _ffn_bwd(dh3, dhb3, sv["h2"], sv["u3"], sv["a2"], sv["b2"], sv["s2"],
                                                      w["ffn2_w_gate"], w["ffn2_w_up"], w["ffn2_w_down"], w["ffn2_norm"], 1.0)
    dyn = dact_plain(dhb, w["w_out"])
    g_wout = wgrad(sv["ycat"], dhb).reshape(N_SHARD, -1, D)
    dz_abc, g_convw, g_convb, g_nconv = conv_bwd(sv["z"], w["conv_w"], w["conv_b"], w["conv_out_norm"], dyn)
    dy, dpre_g, zg, g_glub, g_nssm = glu_bwd(sv["y"], w["glu_w"], w["glu_b"], w["ssm_out_norm"], _to_segments(dyn[1]))
    C = zg.shape[1]
    g_gluw = wgrad(zg, dpre_g).reshape(N_SHARD, C // N_SHARD, C)
    dus, dlam, dbre, dbim, dcre, dcim, dd = ssm_bwd(sv["us"], dy, *sv["mats"])
    take_b = lambda m: _block_diag_take(m, H, P).transpose(0, 1, 3, 2).reshape(G, P, H)
    take_c = lambda m: _block_diag_take(m, P, H).transpose(0, 1, 3, 2).reshape(G, H, P)
    g_ar, g_ai, g_dt, g_br, g_bi = ssm_prep_bwd(
        w["ssm_A_re"], w["ssm_A_im"], w["ssm_log_dt"], w["ssm_B_re"], w["ssm_B_im"],
        dlam[0].reshape(G, P), dlam[1].reshape(G, P), take_b(dbre), take_b(dbim))
    dz3 = jnp.concatenate([dz_abc, _from_segments(dus).astype(BF16)[None]], axis=0)
    g_win = wgrad(sv["u2"], dz3)
    dh1, dhb1, g_nmix = dx_rms([(dz3, w["w_in"], False)], sv["h1"], w["mix_norm"], dh2, 0.5)
    dh0, _, g_wg1, g_wu1, g_wd1, g_nffn1 = _ffn_bwd(dh1, dhb1, sv["h0"], sv["u1"], sv["a1"], sv["b1"], sv["s1"],
                                                    w["ffn1_w_gate"], w["ffn1_w_up"], w["ffn1_w_down"], w["ffn1_norm"], 1.0)
    big = [g_wg1, g_wu1, g_wd1, g_win, g_gluw, g_wout, g_wg2, g_wu2, g_wd2, g_wpg, g_wpp]
    small = dict(ffn1_norm=g_nffn1, mix_norm=g_nmix, conv_w=g_convw, conv_b=g_convb, ssm_A_re=g_ar, ssm_A_im=g_ai,
                 ssm_B_re=g_br, ssm_B_im=g_bi, ssm_C_re=take_c(dcre), ssm_C_im=take_c(dcim), ssm_D=dd,
                 ssm_log_dt=g_dt, glu_b=g_glub, conv_out_norm=g_nconv, ssm_out_norm=g_nssm, ffn2_norm=g_nffn2,
                 ple_norm=g_nple)
    return dh0, big, small


def local_step(x, p, target, final_norm, weights_of, on_grads):
    depth = p.shape[0]
    h = x
    layers, saved, pbs = [], [], []
    for i in range(depth):
        w = weights_of(i, h)
        pb = p[i].astype(BF16)
        h, sv = layer_fwd(h, pb, w)
        layers.append(w)
        saved.append(sv)
        pbs.append(pb)
    loss_part, dh, g_final = loss_head(h, final_norm, target)
    smalls = [None] * depth
    token = jnp.zeros((8, 128), F32)
    for i in reversed(range(depth)):
        dh, big, smalls[i] = layer_bwd(dh, pbs[i], layers[i], saved[i], token)
        token = on_grads(i, big, dh)
    return loss_part, dh, smalls, g_final


def _row_tile(rows):
    for t in (512, 352, 256, 128, 64, 32, 16):
        if rows % t == 0:
            return t
    return rows


def elementwise(fn, ins, out_dtypes, name):
    rows, cols = ins[0].shape
    tr = _row_tile(rows)
    n_in = len(ins)

    def body(*refs):
        outs = fn(*[r[...] for r in refs[:n_in]])
        for o_ref, o in zip(refs[n_in:], outs):
            o_ref[...] = o.astype(o_ref.dtype)

    spec = pl.BlockSpec((tr, cols), lambda i: (i, 0))
    return pl.pallas_call(
        body, name=name, grid=(rows // tr,), in_specs=[spec] * n_in, out_specs=[spec] * len(out_dtypes),
        out_shape=[S((rows, cols), d) for d in out_dtypes], compiler_params=_params("parallel"),
    )(*ins)


def _adamw(w, g, m, v):
    m = ADAM_B1 * m + (1.0 - ADAM_B1) * g
    v = ADAM_B2 * v + (1.0 - ADAM_B2) * (g * g)
    m_hat = m / (1.0 - ADAM_B1 ** ADAM_STEP)
    v_hat = v / (1.0 - ADAM_B2 ** ADAM_STEP)
    delta = -ADAM_LR * (m_hat / (jnp.sqrt(v_hat) + ADAM_EPS) + ADAM_WD * w)
    return delta, m, v


ANY = pl.BlockSpec(memory_space=pl.ANY)


def _mesh_pos():
    return lax.axis_index("x"), lax.axis_index("y"), lax.axis_index("c")


def _other_chips(x, y):
    return [(1 - x, y), (x, 1 - y), (1 - x, 1 - y)]


def _remote(src, dst, send_sem, recv_sem, device):
    return pltpu.make_async_remote_copy(src_ref=src, dst_ref=dst, send_sem=send_sem, recv_sem=recv_sem,
                                        device_id=device, device_id_type=MESH)


def gather_weights(ws):
    n = len(ws)

    def body(*refs):
        outs = refs[n:2 * n]
        send_sems, recv_sems = refs[2 * n:]
        x, y, c = _mesh_pos()
        me_s = 2 * x + y
        sibling = (x, y, 1 - c)
        chips = _other_chips(x, y)
        n_half = outs[0].shape[0] // 2
        mine, other = pl.ds(c * n_half, n_half), pl.ds((1 - c) * n_half, n_half)
        sent = []
        for t in range(n):
            for j, (cx, cy) in enumerate(chips):
                blk = outs[t].at[mine, me_s]
                cp = _remote(blk, blk, send_sems.at[t, j], recv_sems.at[t, j], (cx, cy, c))
                cp.start()
                sent.append(cp)
        for j, (cx, cy) in enumerate(chips):
            for t in range(n):
                blk = outs[t].at[mine, 2 * cx + cy]
                _remote(blk, blk, send_sems.at[t, j], recv_sems.at[t, j], (cx, cy, c)).wait_recv()
                cp = _remote(blk, blk, send_sems.at[t, 3 + j], recv_sems.at[t, 3 + j], sibling)
                cp.start()
                sent.append(cp)
        for j, (cx, cy) in enumerate(chips):
            for t in range(n):
                blk = outs[t].at[other, 2 * cx + cy]
                _remote(blk, blk, send_sems.at[t, 3 + j], recv_sems.at[t, 3 + j], sibling).wait_recv()
        for cp in sent:
            cp.wait_send()

    return pl.pallas_call(
        body, name="gather_weights", in_specs=[ANY] * n, out_specs=[ANY] * n,
        out_shape=[S(w.shape, w.dtype) for w in ws], input_output_aliases={t: t for t in range(n)},
        scratch_shapes=[pltpu.SemaphoreType.DMA((n, 6)), pltpu.SemaphoreType.DMA((n, 6))],
    )(*ws)


HBM = pl.BlockSpec(memory_space=pltpu.HBM)
SEM = pl.BlockSpec(memory_space=pltpu.SEMAPHORE)
VMEM_WHOLE = pl.BlockSpec(memory_space=pltpu.VMEM)
SPLIT_COPY = pltpu.CompilerParams(has_side_effects=pltpu.SideEffectType.DATAFLOW_SIDE_EFFECTING)


def _hbm(x):
    return pltpu.with_memory_space_constraint(x, pltpu.HBM)


def _half_rows(ref, c):
    r2 = ref.shape[1] // 2
    return pl.ds(pl.multiple_of(c * r2, 8), r2)


def gather_start(bufs, layer, after):
    n, k = len(bufs), len(after)

    def body(*refs):
        ins, send_sems, recv_sems, token = refs[:n], refs[n + k], refs[n + k + 1], refs[2 * n + k + 2]
        x, y, c = _mesh_pos()
        me_s = 2 * x + y
        for t in range(n):
            blk = ins[t].at[me_s, _half_rows(ins[t], c)]
            for j, (cx, cy) in enumerate(_other_chips(x, y)):
                _remote(blk, blk, send_sems.at[3 * t + j], recv_sems.at[3 * t + j], (cx, cy, c)).start()
        token[...] = jnp.zeros_like(token)

    outs = pl.pallas_call(
        body, name=f"gather_start_{layer}", in_specs=[HBM] * n + [ANY] * k, out_specs=[SEM, SEM] + [HBM] * n + [VMEM_WHOLE],
        out_shape=[pltpu.SemaphoreType.DMA((3 * n,)), pltpu.SemaphoreType.DMA((3 * n,))]
        + [pltpu.HBM(b.shape, b.dtype) for b in bufs] + [S((8, 128), F32)],
        input_output_aliases={t: t + 2 for t in range(n)}, compiler_params=SPLIT_COPY,
    )(*[_hbm(b) for b in bufs], *after)
    return outs[0], outs[1], list(outs[2:2 + n]), outs[2 + n]


def gather_wait(bufs, send_sems, recv_sems, after, layer):
    n, n_after = len(bufs), len(after)

    def body(*refs):
        ins, send_ref, recv_ref = refs[:n], refs[n], refs[n + 1]
        x, y, c = _mesh_pos()
        me_s = 2 * x + y
        for t in range(n):
            rows = _half_rows(ins[t], c)
            for j, (cx, cy) in enumerate(_other_chips(x, y)):
                cp = _remote(ins[t].at[me_s, rows], ins[t].at[2 * cx + cy, rows], send_ref.at[3 * t + j], recv_ref.at[3 * t + j],
                             (cx, cy, c))
                cp.wait_send()
                cp.wait_recv()

    outs = pl.pallas_call(
        body, name=f"gather_wait_{layer}", in_specs=[HBM] * n + [SEM, SEM] + [ANY] * n_after, out_specs=[HBM] * n,
        out_shape=[pltpu.HBM(b.shape, b.dtype) for b in bufs],
        input_output_aliases={t: t for t in range(n)}, compiler_params=SPLIT_COPY,
    )(*bufs, send_sems, recv_sems, *after)
    return list(outs)


def gather_forward(bufs):
    n = len(bufs)

    def body(*refs):
        outs = refs[n:2 * n]
        send_sems, recv_sems = refs[2 * n:]
        x, y, c = _mesh_pos()
        copies = []
        for t in range(n):
            rows = _half_rows(outs[t], c)
            for j, (cx, cy) in enumerate(_other_chips(x, y)):
                blk = outs[t].at[2 * cx + cy, rows]
                cp = _remote(blk, blk, send_sems.at[t, j], recv_sems.at[t, j], (x, y, 1 - c))
                cp.start()
                copies.append(cp)
        for cp in copies:
            cp.wait()

    return pl.pallas_call(
        body, name="gather_forward", in_specs=[ANY] * n, out_specs=[ANY] * n,
        out_shape=[S(b.shape, b.dtype) for b in bufs], input_output_aliases={t: t for t in range(n)},
        scratch_shapes=[pltpu.SemaphoreType.DMA((n, 3)), pltpu.SemaphoreType.DMA((n, 3))],
    )(*bufs)


def chips_start(sums, layer):
    n = len(sums)
    lands = [lax.empty((3,) + s.shape[1:], s.dtype) for s in sums]

    def body(*refs):
        a, land, send_sems, recv_sems, token = refs[:n], refs[n:2 * n], refs[2 * n], refs[2 * n + 1], refs[4 * n + 2]
        x, y, c = _mesh_pos()
        for t in range(n):
            for j, (cx, cy) in enumerate(_other_chips(x, y)):
                _remote(a[t].at[2 * cx + cy], land[t].at[j], send_sems.at[3 * t + j], recv_sems.at[3 * t + j], (cx, cy, c)).start()
        token[...] = jnp.zeros_like(token)

    outs = pl.pallas_call(
        body, name=f"chips_start_{layer}", in_specs=[HBM] * (2 * n), out_specs=[SEM, SEM] + [HBM] * (2 * n) + [VMEM_WHOLE],
        out_shape=[pltpu.SemaphoreType.DMA((3 * n,)), pltpu.SemaphoreType.DMA((3 * n,))]
        + [pltpu.HBM(b.shape, b.dtype) for b in sums + lands] + [S((8, 128), F32)],
        input_output_aliases={t: t + 2 for t in range(2 * n)}, compiler_params=SPLIT_COPY,
    )(*[_hbm(b) for b in sums + lands])
    return outs[0], outs[1], list(outs[2:2 + n]), list(outs[2 + n:2 + 2 * n]), outs[2 + 2 * n]


def chips_wait(sums, lands, send_sems, recv_sems, after, layer):
    n, n_after = len(sums), len(after)

    def body(*refs):
        a, land, send_ref, recv_ref = refs[:n], refs[n:2 * n], refs[2 * n], refs[2 * n + 1]
        x, y, c = _mesh_pos()
        for t in range(n):
            for j, (cx, cy) in enumerate(_other_chips(x, y)):
                cp = _remote(a[t].at[2 * cx + cy], land[t].at[j], send_ref.at[3 * t + j], recv_ref.at[3 * t + j], (cx, cy, c))
                cp.wait_send()
                cp.wait_recv()

    outs = pl.pallas_call(
        body, name=f"chips_wait_{layer}", in_specs=[HBM] * (2 * n) + [SEM, SEM] + [ANY] * n_after, out_specs=[HBM] * (2 * n),
        out_shape=[pltpu.HBM(b.shape, b.dtype) for b in sums + lands],
        input_output_aliases={t: t for t in range(2 * n)}, compiler_params=SPLIT_COPY,
    )(*sums, *lands, send_sems, recv_sems, *after)
    return list(outs[n:])


def cast_place_layer(w, layer, pos, dtype):
    _, r, c = w.shape
    tr = _row_tile(r)

    def body(pos_ref, w_ref, o_ref):
        o_ref[0] = w_ref[0].astype(dtype)

    return pl.pallas_call(
        body, name="cast_place_layer",
        grid_spec=pltpu.PrefetchScalarGridSpec(
            num_scalar_prefetch=1, grid=(r // tr,),
            in_specs=[pl.BlockSpec((1, tr, c), lambda i, pos: (layer, i, 0))],
            out_specs=pl.BlockSpec((1, tr, c), lambda i, pos: (pos[1], i, 0))),
        out_shape=S((N_SHARD, r, c), dtype), compiler_params=_params("parallel"),
    )(pos, w)


def cast_place(w, pos, dtype):
    layers, r, c = w.shape
    tr = _row_tile(r)

    def body(pos_ref, w_ref, o_ref):
        o_ref[0, 0] = w_ref[0].astype(dtype)

    return pl.pallas_call(
        body, name="cast_place",
        grid_spec=pltpu.PrefetchScalarGridSpec(
            num_scalar_prefetch=1, grid=(layers, r // tr),
            in_specs=[pl.BlockSpec((1, tr, c), lambda l, i, pos: (l, i, 0))],
            out_specs=pl.BlockSpec((1, 1, tr, c), lambda l, i, pos: (l, pos[1], i, 0))),
        out_shape=S((layers, N_SHARD, r, c), dtype), compiler_params=_params("parallel", "parallel"),
    )(pos, w)


def reduce_pair(gs, after):
    n, k = len(gs), len(after)

    def body(*refs):
        ins, got = refs[:n], refs[n + k:2 * n + k]
        send_sems, recv_sems = refs[2 * n + k:]
        x, y, c = _mesh_pos()
        copies = []
        for t in range(n):
            r2 = ins[t].shape[1] // 2
            give = pl.ds(pl.multiple_of((1 - c) * r2, 8), r2)
            cp = _remote(ins[t].at[:, give], got[t], send_sems.at[t], recv_sems.at[t], (x, y, 1 - c))
            cp.start()
            copies.append(cp)
        for cp in copies:
            cp.wait()

    return pl.pallas_call(
        body, name="reduce_pair", in_specs=[ANY] * (n + k), out_specs=[ANY] * n,
        out_shape=[S((g.shape[0], g.shape[1] // 2, g.shape[2]), g.dtype) for g in gs],
        scratch_shapes=[pltpu.SemaphoreType.DMA((n,)), pltpu.SemaphoreType.DMA((n,))],
    )(*gs, *after)


def pair_sum(g, got, pos):
    ns, r2, c = got.shape
    tr = _row_tile(r2)
    n_i = r2 // tr

    def body(pos_ref, g_ref, got_ref, sum_ref, own_ref):
        s = pl.program_id(1)
        v = g_ref[0] + got_ref[0]
        sum_ref[0] = v.astype(BF16)

        @pl.when(s == pos_ref[1])
        def _():
            own_ref[...] = v

    return pl.pallas_call(
        body, name="pair_sum",
        grid_spec=pltpu.PrefetchScalarGridSpec(
            num_scalar_prefetch=1, grid=(n_i, ns),
            in_specs=[pl.BlockSpec((1, tr, c), lambda i, s, pos: (s, pos[0] * n_i + i, 0)),
                      pl.BlockSpec((1, tr, c), lambda i, s, pos: (s, i, 0))],
            out_specs=[pl.BlockSpec((1, tr, c), lambda i, s, pos: (s, i, 0)), pl.BlockSpec((tr, c), lambda i, s, pos: (i, 0))]),
        out_shape=[S((ns, r2, c), BF16), S((r2, c), F32)], compiler_params=_params("parallel", "arbitrary"),
    )(pos, g, got)


def chip_sum(own, p2, pos):
    r2, c = own.shape
    tr = _row_tile(r2)
    n_i = r2 // tr

    def body(pos_ref, own_ref, a_ref, b_ref, c_ref, o_ref):
        o_ref[...] = own_ref[...] + a_ref[0].astype(F32) + b_ref[0].astype(F32) + c_ref[0].astype(F32)

    peer = lambda j: pl.BlockSpec((1, tr, c), lambda i, pos: (j, i, 0))
    return pl.pallas_call(
        body, name="chip_sum",
        grid_spec=pltpu.PrefetchScalarGridSpec(
            num_scalar_prefetch=1, grid=(n_i,),
            in_specs=[pl.BlockSpec((tr, c), lambda i, pos: (i, 0)), peer(0), peer(1), peer(2)],
            out_specs=pl.BlockSpec((tr, c), lambda i, pos: (pos[0] * n_i + i, 0))),
        out_shape=S((2 * r2, c), F32), compiler_params=_params("parallel"),
    )(pos, own, p2, p2, p2)


def exchange_halves(rs):
    n = len(rs)

    def body(*refs):
        outs = refs[n:2 * n]
        send_sems, recv_sems = refs[2 * n:]
        x, y, c = _mesh_pos()
        copies = []
        for t in range(n):
            r2 = outs[t].shape[0] // 2
            rows = outs[t].at[pl.ds(pl.multiple_of(c * r2, 8), r2)]
            cp = _remote(rows, rows, send_sems.at[t], recv_sems.at[t], (x, y, 1 - c))
            cp.start()
            copies.append(cp)
        for cp in copies:
            cp.wait()

    return pl.pallas_call(
        body, name="exchange_halves", in_specs=[ANY] * n, out_specs=[ANY] * n,
        out_shape=[S(r.shape, r.dtype) for r in rs], input_output_aliases={t: t for t in range(n)},
        scratch_shapes=[pltpu.SemaphoreType.DMA((n,)), pltpu.SemaphoreType.DMA((n,))],
    )(*rs)


def allreduce_small(vec):
    R = vec.shape[0]

    def body(x_ref, o_ref, buf, send_sems, recv_sems):
        x, y, c = _mesh_pos()
        me = 4 * x + 2 * y + c
        buf[me] = x_ref[...]
        copies = []
        for k in range(1, N_DEV):
            peer = (1 - x if k & 4 else x, 1 - y if k & 2 else y, 1 - c if k & 1 else c)
            cp = _remote(x_ref, buf.at[me], send_sems.at[k - 1], recv_sems.at[k - 1], peer)
            cp.start()
            copies.append(cp)
        for cp in copies:
            cp.wait()
        acc = buf[0]
        for d in range(1, N_DEV):
            acc = acc + buf[d]
        o_ref[...] = acc

    vm = pl.BlockSpec(memory_space=pltpu.VMEM)
    return pl.pallas_call(
        body, name="allreduce_small", in_specs=[vm], out_specs=vm, out_shape=S((R, 128), F32),
        scratch_shapes=[pltpu.VMEM((N_DEV, R, 128), F32), pltpu.SemaphoreType.DMA((N_DEV - 1,)),
                        pltpu.SemaphoreType.DMA((N_DEV - 1,))],
        compiler_params=pltpu.CompilerParams(vmem_limit_bytes=VMEM_LIMIT_BYTES),
    )(vec)


def adamw_layer(w, g, m, v, layer, prev):
    _, r, c = w.shape
    tr = _row_tile(r)

    def body(w_ref, g_ref, m_ref, v_ref, *rest):
        outs = rest[-4:]
        g_val = g_ref[...]
        outs[0][0] = g_val
        outs[1][0], outs[2][0], outs[3][0] = _adamw(w_ref[0], g_val, m_ref[0], v_ref[0])

    lay = pl.BlockSpec((1, tr, c), lambda i: (layer, i, 0))
    prev = list(prev) if prev else []
    return pl.pallas_call(
        body, name="adamw_layer", grid=(r // tr,),
        in_specs=[lay, pl.BlockSpec((tr, c), lambda i: (i, 0)), lay, lay] + [ANY] * len(prev),
        out_specs=[lay] * 4, out_shape=[S(w.shape, F32)] * 4,
        input_output_aliases={4 + k: k for k in range(len(prev))}, compiler_params=_params("parallel"),
    )(w, g, m, v, *prev)


def reduce_begin(gs, pos, layer, after):
    got = reduce_pair(gs, after)
    sums, own = zip(*[pair_sum(g, o, pos) for g, o in zip(gs, got)])
    send_sems, recv_sems, sums, lands, token = chips_start(list(sums), layer)
    return dict(own=own, sums=sums, lands=lands, sems=(send_sems, recv_sems), token=token, layer=layer)


def reduce_end(pending, pos, after):
    lands = chips_wait(pending["sums"], pending["lands"], *pending["sems"], after, pending["layer"])
    return exchange_halves([chip_sum(o, p, pos) for o, p in zip(pending["own"], lands)])


W_NAMES = ("ffn1_norm", "ffn1_w_gate", "ffn1_w_up", "ffn1_w_down", "mix_norm", "w_in", "conv_w", "conv_b", "ssm_A_re", "ssm_A_im",
           "ssm_B_re", "ssm_B_im", "ssm_C_re", "ssm_C_im", "ssm_D", "ssm_log_dt", "glu_w", "glu_b", "conv_out_norm", "ssm_out_norm",
           "w_out", "ffn2_norm", "ffn2_w_gate", "ffn2_w_up", "ffn2_w_down", "ple_norm", "ple_w_gate", "ple_w_proj", "final_norm")
SMALL_ALL = SMALL + ("final_norm",)
TRANSPOSED = ("ffn1_w_gate", "ffn1_w_up", "ffn2_w_gate", "ffn2_w_up")
PACK = 8 * 128


def _pack(parts):
    flat = jnp.concatenate([p.reshape(-1) for p in parts])
    pad = (-flat.shape[0]) % PACK
    return jnp.pad(flat, (0, pad)).reshape(-1, 128)


def _unpack(vec, shapes):
    flat = vec.reshape(-1)
    out, off = [], 0
    for shp in shapes:
        size = math.prod(shp)
        out.append(flat[off:off + size].reshape(shp))
        off += size
    return out


def _step(a):
    a = {k: jnp.swapaxes(v, 1, 2) if k.removeprefix("m_").removeprefix("v_") in TRANSPOSED else v for k, v in a.items()}
    x, p, target = a["x"][0], a["p"][:, 0], a["loss_target"][0]
    depth = p.shape[0]
    L, D = x.shape
    me_s = 2 * lax.axis_index("x") + lax.axis_index("y")

    pos = jnp.stack([lax.axis_index("c"), me_s]).astype(jnp.int32)
    conv_w = gather_weights([cast_place(a["conv_w"], pos, F32)])[0]
    started = [gather_start([cast_place_layer(a[n], l, pos, BF16) for n in BIG], l, [conv_w]) for l in range(depth)]

    def weights_of(l, h):
        send_sems, recv_sems, bufs, _ = started[l]
        after = [h] if l else [s[3] for s in started]
        full = gather_forward(gather_wait(bufs, send_sems, recv_sems, after, l))
        w = {n: a[n][l] for n in SMALL if n != "conv_w"}
        w.update(dict(zip(BIG, full)))
        C = w["glu_w"].shape[-1]
        w["glu_w"] = w["glu_w"].reshape(C, C)
        w["w_out"] = w["w_out"].reshape(2, -1, D)
        w["ple_w_gate"] = w["ple_w_gate"].reshape(D, D)
        w["conv_w"] = conv_w[l].transpose(1, 0, 2).reshape(3, -1)
        return w

    pending, first_layer_grads = {}, []

    def on_grads(l, big, dh):
        if l == 0:
            first_layer_grads.extend(big)
            return None
        pending[l] = reduce_begin(big, pos, l, [])
        return pending[l]["token"]

    loss_part, dx, smalls, g_final = local_step(x, p, target, a["final_norm"], weights_of, on_grads)
    small_shapes = [(depth,) + smalls[0][n].shape for n in SMALL] + [g_final.shape, (1,)]
    parts = [smalls[l][n] for n in SMALL for l in range(depth)] + [g_final, loss_part[0, 0:1]]
    summed_vec = allreduce_small(_pack(parts))
    pending[0] = reduce_begin(first_layer_grads, pos, 0, [summed_vec])
    stacked = [None] * len(BIG)
    for l in reversed(range(depth)):
        after = [pending[0]["token"]] if l else [s[3] for s in stacked]
        reduced = reduce_end(pending[l], pos, after)
        stacked = [adamw_layer(a[n], reduced[i], a["m_" + n], a["v_" + n], l, stacked[i]) for i, n in enumerate(BIG)]
    big_out = {n: [jnp.swapaxes(o, 1, 2) for o in outs] if n in TRANSPOSED else outs for n, outs in zip(BIG, stacked)}

    summed = _unpack(summed_vec, small_shapes)
    g_small = dict(zip(SMALL_ALL, summed[:-1]))
    loss = summed[-1][0]
    n_conv = a["conv_w"].shape[-1]
    g_small["conv_w"] = lax.dynamic_slice_in_dim(g_small["conv_w"], me_s * n_conv, n_conv, axis=2)
    g_small = {n: g_small[n].reshape(a[n].shape) for n in SMALL_ALL}
    packed = [_pack([src[n] for n in SMALL_ALL]) for src in
              ({n: a[n] for n in SMALL_ALL}, g_small, {n: a["m_" + n] for n in SMALL_ALL}, {n: a["v_" + n] for n in SMALL_ALL})]
    shapes = [a[n].shape for n in SMALL_ALL]
    d_s, m_s, v_s = [dict(zip(SMALL_ALL, _unpack(o, shapes))) for o in elementwise(_adamw, packed, [F32, F32, F32], "adamw_small")]

    outs = {n: big_out[n] if n in big_out else (g_small[n], d_s[n], m_s[n], v_s[n]) for n in W_NAMES}
    return (loss, dx[None], *[outs[n][0] for n in W_NAMES], *[outs[n][1] for n in W_NAMES],
            *[outs[n][2] for n in W_NAMES], *[outs[n][3] for n in W_NAMES])


def kernel(x, p, ffn1_norm, ffn1_w_gate, ffn1_w_up, ffn1_w_down, mix_norm, w_in, conv_w, conv_b, ssm_A_re, ssm_A_im, ssm_B_re, ssm_B_im, ssm_C_re, ssm_C_im, ssm_D, ssm_log_dt, glu_w, glu_b, conv_out_norm, ssm_out_norm, w_out, ffn2_norm, ffn2_w_gate, ffn2_w_up, ffn2_w_down, ple_norm, ple_w_gate, ple_w_proj, final_norm, loss_target, m_ffn1_norm, m_ffn1_w_gate, m_ffn1_w_up, m_ffn1_w_down, m_mix_norm, m_w_in, m_conv_w, m_conv_b, m_ssm_A_re, m_ssm_A_im, m_ssm_B_re, m_ssm_B_im, m_ssm_C_re, m_ssm_C_im, m_ssm_D, m_ssm_log_dt, m_glu_w, m_glu_b, m_conv_out_norm, m_ssm_out_norm, m_w_out, m_ffn2_norm, m_ffn2_w_gate, m_ffn2_w_up, m_ffn2_w_down, m_ple_norm, m_ple_w_gate, m_ple_w_proj, m_final_norm, v_ffn1_norm, v_ffn1_w_gate, v_ffn1_w_up, v_ffn1_w_down, v_mix_norm, v_w_in, v_conv_w, v_conv_b, v_ssm_A_re, v_ssm_A_im, v_ssm_B_re, v_ssm_B_im, v_ssm_C_re, v_ssm_C_im, v_ssm_D, v_ssm_log_dt, v_glu_w, v_glu_b, v_conv_out_norm, v_ssm_out_norm, v_w_out, v_ffn2_norm, v_ffn2_w_gate, v_ffn2_w_up, v_ffn2_w_down, v_ple_norm, v_ple_w_gate, v_ple_w_proj, v_final_norm):
    return _step(dict(locals()))
```

```python
import functools
import math

import jax
import jax.numpy as jnp
from jax import lax
from jax.experimental import pallas as pl
from jax.experimental.pallas import tpu as pltpu

F32, BF16 = jnp.float32, jnp.bfloat16
S = jax.ShapeDtypeStruct
EPS = 1e-6
N_SEG = 8
N_SHARD = 4
N_DEV = 8
VMEM_LIMIT_BYTES = 56 * 1024 * 1024
ADAM_LR, ADAM_B1, ADAM_B2, ADAM_EPS, ADAM_WD, ADAM_STEP = 0.001, 0.9, 0.999, 1e-08, 0.01, 10
MESH = pl.DeviceIdType.MESH


def _params(*sem):
    return pltpu.CompilerParams(dimension_semantics=sem if sem else None, vmem_limit_bytes=VMEM_LIMIT_BYTES)


def _dot(a, b, ca, cb):
    return lax.dot_general(a, b, (((ca,), (cb,)), ((), ())), preferred_element_type=F32)


def _sigmoid(x):
    return 1.0 / (1.0 + jnp.exp(-x))


def _rstd(x):
    return lax.rsqrt(jnp.mean(x * x, axis=-1, keepdims=True) + EPS)


def _rms_bwd(x, g, dy):
    r = _rstd(x)
    xh = x * r
    dxh = dy * g
    dx = r * (dxh - xh * jnp.mean(dxh * xh, axis=-1, keepdims=True))
    return dx, jnp.sum(dy * xh, axis=0, keepdims=True)


def _tile(n, want):
    return want if n % want == 0 else n


def rmsnorm_fwd(h, g):
    L, D = h.shape
    tm = _tile(L, 512)

    def body(h_ref, g_ref, o_ref):
        x = h_ref[...]
        o_ref[...] = (x * _rstd(x) * g_ref[...]).astype(BF16)

    return pl.pallas_call(
        body, name="rmsnorm_fwd", grid=(L // tm,),
        in_specs=[pl.BlockSpec((tm, D), lambda m: (m, 0)), pl.BlockSpec((1, D), lambda m: (0, 0))],
        out_specs=pl.BlockSpec((tm, D), lambda m: (m, 0)),
        out_shape=S((L, D), BF16), compiler_params=_params("parallel"),
    )(h, g.reshape(1, D))


def ffn_up(u, wg, wu):
    L, D = u.shape
    ns, F, _ = wg.shape
    tm = _tile(L, 512)

    def body(u_ref, wg_ref, wu_ref, a_ref, b_ref, s_ref):
        x = u_ref[...]
        a = _dot(x, wg_ref[0], 1, 1)
        b = _dot(x, wu_ref[0], 1, 1)
        a_ref[0] = a.astype(BF16)
        b_ref[0] = b.astype(BF16)
        s_ref[0] = (a * _sigmoid(a) * b).astype(BF16)

    w_spec = pl.BlockSpec((1, F, D), lambda s, m: (s, 0, 0))
    o_spec = pl.BlockSpec((1, tm, F), lambda s, m: (s, m, 0))
    return pl.pallas_call(
        body, name="ffn_up", grid=(ns, L // tm),
        in_specs=[pl.BlockSpec((tm, D), lambda s, m: (m, 0)), w_spec, w_spec],
        out_specs=[o_spec, o_spec, o_spec],
        out_shape=[S((ns, L, F), BF16)] * 3, compiler_params=_params("parallel", "parallel"),
    )(u, wg, wu)


def mm_shard_n(u, w3, out_dtype):
    L, K = u.shape
    ns, _, N = w3.shape
    tm = _tile(L, 512)

    def body(u_ref, w_ref, o_ref):
        o_ref[0] = _dot(u_ref[...], w_ref[0], 1, 0).astype(out_dtype)

    return pl.pallas_call(
        body, name="mm_shard_n", grid=(ns, L // tm),
        in_specs=[pl.BlockSpec((tm, K), lambda s, m: (m, 0)), pl.BlockSpec((1, K, N), lambda s, m: (s, 0, 0))],
        out_specs=pl.BlockSpec((1, tm, N), lambda s, m: (s, m, 0)),
        out_shape=S((ns, L, N), out_dtype), compiler_params=_params("parallel", "parallel"),
    )(u, w3)


def mm_shard_k(a3, w3, res, scale, g_next):
    nk, L, Kc = a3.shape
    N = w3.shape[2]
    tm = _tile(L, 512)

    def body(a_ref, w_ref, r_ref, g_ref, o_ref, u_ref, acc):
        k = pl.program_id(1)
        part = _dot(a_ref[0], w_ref[0], 1, 0)

        @pl.when(k == 0)
        def _():
            acc[...] = part

        @pl.when(k > 0)
        def _():
            acc[...] += part

        @pl.when(k == nk - 1)
        def _():
            h = r_ref[...] + scale * acc[...]
            o_ref[...] = h
            u_ref[...] = (h * _rstd(h) * g_ref[...]).astype(BF16)

    tile = pl.BlockSpec((tm, N), lambda m, k: (m, 0))
    return pl.pallas_call(
        body, name="mm_shard_k", grid=(L // tm, nk),
        in_specs=[pl.BlockSpec((1, tm, Kc), lambda m, k: (k, m, 0)), pl.BlockSpec((1, Kc, N), lambda m, k: (k, 0, 0)),
                  tile, pl.BlockSpec((1, N), lambda m, k: (0, 0))],
        out_specs=[tile, tile],
        out_shape=[S((L, N), F32), S((L, N), BF16)], scratch_shapes=[pltpu.VMEM((tm, N), F32)],
        compiler_params=_params("parallel", "arbitrary"),
    )(a3, w3, res, g_next.reshape(1, N))


CONV_HALO = 8


def _conv_specs(L, tm, C, shard):
    nb = L // CONV_HALO
    per = tm // CONV_HALO
    main = pl.BlockSpec((1, tm, C), lambda m: (shard, m, 0))
    prev = pl.BlockSpec((1, CONV_HALO, C), lambda m: (shard, jnp.maximum(m * per - 1, 0), 0))
    nxt = pl.BlockSpec((1, CONV_HALO, C), lambda m: (shard, jnp.minimum((m + 1) * per, nb - 1), 0))
    return main, prev, nxt


def _conv_core(zb, zc, zv, w_ref, bias, grow, L):
    valid = (grow >= 0) & (grow < L)
    v = jnp.where(valid, zc * zv, 0.0)
    v1 = pltpu.roll(v, 1, 0)
    v2 = pltpu.roll(v, 2, 0)
    cb = w_ref[0:1, :] * v2 + w_ref[1:2, :] * v1 + w_ref[2:3, :] * v + bias
    return valid, v, v1, v2, cb, zb * cb


def conv_fwd(z, conv_w, conv_b, gnorm):
    _, L, C = z.shape
    tm = _tile(L, 256)
    H = CONV_HALO

    def body(zb_ref, zc_ref, zcp_ref, zv_ref, zvp_ref, w_ref, b_ref, g_ref, o_ref):
        m = pl.program_id(0)
        zc = jnp.concatenate([zcp_ref[0], zc_ref[0]], axis=0)
        zv = jnp.concatenate([zvp_ref[0], zv_ref[0]], axis=0)
        grow = m * tm - H + lax.broadcasted_iota(jnp.int32, (tm + H, C), 0)
        valid = grow >= 0
        v = jnp.where(valid, zc * zv, 0.0)
        v1 = pltpu.roll(v, 1, 0)
        v2 = pltpu.roll(v, 2, 0)
        cb = (w_ref[0:1, :] * v2 + w_ref[1:2, :] * v1 + w_ref[2:3, :] * v + b_ref[...])[H:, :]
        ya = zb_ref[0] * cb
        o_ref[...] = (ya * _rstd(ya) * g_ref[...]).astype(BF16)

    zb_m, _, _ = _conv_specs(L, tm, C, 0)
    zc_m, zc_p, _ = _conv_specs(L, tm, C, 1)
    zv_m, zv_p, _ = _conv_specs(L, tm, C, 2)
    row = lambda r: pl.BlockSpec((r, C), lambda m: (0, 0))
    return pl.pallas_call(
        body, name="conv_fwd", grid=(L // tm,),
        in_specs=[zb_m, zc_m, zc_p, zv_m, zv_p, row(3), row(1), row(1)],
        out_specs=pl.BlockSpec((tm, C), lambda m: (m, 0)),
        out_shape=S((L, C), BF16), compiler_params=_params("parallel"),
    )(z, z, z, z, z, conv_w, conv_b.reshape(1, C), gnorm.reshape(1, C))


def _cmul(ar, ai, br, bi):
    return ar * br - ai * bi, ar * bi + ai * br


def _scan_fwd(hr_ref, hi_ref, lr, li, n_steps):
    W = hr_ref.shape[1]
    zero = jnp.zeros((N_SEG, W), F32)

    def local(t, c):
        r = pl.multiple_of(t * N_SEG, N_SEG)
        pr, pi = _cmul(lr, li, c[0], c[1])
        nr = pr + hr_ref[pl.ds(r, N_SEG), :]
        ni = pi + hi_ref[pl.ds(r, N_SEG), :]
        hr_ref[pl.ds(r, N_SEG), :] = nr
        hi_ref[pl.ds(r, N_SEG), :] = ni
        return nr, ni

    fr, fi = lax.fori_loop(0, n_steps, local, (zero, zero))
    qr, qi = _cpow(lr, li, n_steps)
    row = lax.broadcasted_iota(jnp.int32, (N_SEG, W), 0)
    cr, ci = zero, zero
    for seg in range(1, N_SEG):
        tr, ti = _cmul(qr, qi, cr, ci)
        sr = pltpu.roll(fr + tr, 1, 0)
        si = pltpu.roll(fi + ti, 1, 0)
        cr = jnp.where(row == seg, sr, cr)
        ci = jnp.where(row == seg, si, ci)

    def fix(t, c):
        r = pl.multiple_of(t * N_SEG, N_SEG)
        pr, pi = _cmul(lr, li, c[0], c[1])
        ar, ai = _cmul(pr, pi, cr, ci)
        hr_ref[pl.ds(r, N_SEG), :] += ar
        hi_ref[pl.ds(r, N_SEG), :] += ai
        return pr, pi

    lax.fori_loop(0, n_steps, fix, (jnp.ones((N_SEG, W), F32), zero))


def _cpow(lr, li, n):
    rr, ri = None, None
    br, bi = lr, li
    while n:
        if n & 1:
            rr, ri = (br, bi) if rr is None else _cmul(rr, ri, br, bi)
        n >>= 1
        if n:
            br, bi = _cmul(br, bi, br, bi)
    return rr, ri


def _ssm_specs(L):
    col = lambda w: pl.BlockSpec((L, w), lambda j: (0, j))
    return dict(
        u=col(128), lam=pl.BlockSpec((2, 512), lambda j: (0, j)),
        bmat=pl.BlockSpec((1, 128, 512), lambda j: (j, 0, 0)), cmat=pl.BlockSpec((1, 512, 128), lambda j: (j, 0, 0)),
        d=pl.BlockSpec((1, 128), lambda j: (0, j)))


def ssm_fwd(us, lam, bre, bim, cre, cim, dvec):
    L = us.shape[0]
    n_steps = L // N_SEG
    sp = _ssm_specs(L)

    def body(u_ref, lam_ref, bre_ref, bim_ref, cre_ref, cim_ref, d_ref, y_ref, hr, hi):
        u = u_ref[...]
        ub = u.astype(BF16)
        hr[...] = _dot(ub, bre_ref[0], 1, 0)
        hi[...] = _dot(ub, bim_ref[0], 1, 0)
        lr = jnp.broadcast_to(lam_ref[0:1, :], (N_SEG, 512))
        li = jnp.broadcast_to(lam_ref[1:2, :], (N_SEG, 512))
        _scan_fwd(hr, hi, lr, li, n_steps)
        y_ref[...] = (_dot(hr[...].astype(BF16), cre_ref[0], 1, 0) - _dot(hi[...].astype(BF16), cim_ref[0], 1, 0)
                      + d_ref[...] * u)

    return pl.pallas_call(
        body, name="ssm_fwd", grid=(4,),
        in_specs=[sp["u"], sp["lam"], sp["bmat"], sp["bmat"], sp["cmat"], sp["cmat"], sp["d"]],
        out_specs=sp["u"], out_shape=S((L, 512), F32),
        scratch_shapes=[pltpu.VMEM((L, 512), F32), pltpu.VMEM((L, 512), F32)],
        compiler_params=_params("parallel"),
    )(us, lam, bre, bim, cre, cim, dvec)


_GELU_C = math.sqrt(2.0 / math.pi)


def _gelu(y):
    t = jnp.tanh(_GELU_C * (y + 0.044715 * y * y * y))
    return 0.5 * y * (1.0 + t), t


def glu_fwd(y, w, b, gnorm):
    L, C = y.shape
    tm = _tile(L, 512)

    def body(y_ref, w_ref, b_ref, g_ref, o_ref):
        zg, _ = _gelu(y_ref[...])
        out = zg * _sigmoid(_dot(zg.astype(BF16), w_ref[...], 1, 0) + b_ref[...])
        o_ref[...] = (out * _rstd(out) * g_ref[...]).astype(BF16)

    row = pl.BlockSpec((1, C), lambda m: (0, 0))
    return pl.pallas_call(
        body, name="glu_fwd", grid=(L // tm,),
        in_specs=[pl.BlockSpec((tm, C), lambda m: (m, 0)), pl.BlockSpec((C, C), lambda m: (0, 0)), row, row],
        out_specs=pl.BlockSpec((tm, C), lambda m: (m, 0)),
        out_shape=S((L, C), BF16), compiler_params=_params("parallel"),
    )(y, w, b.reshape(1, C), gnorm.reshape(1, C))


def _ple_specs(L, D, P, tm, nb):
    return [pl.BlockSpec((tm, D), lambda n, m: (m, 0)), pl.BlockSpec((tm, P), lambda n, m: (m, 0)),
            pl.BlockSpec((D, nb), lambda n, m: (0, n)), pl.BlockSpec((1, P, nb), lambda n, m: (n, 0, 0)),
            pl.BlockSpec((tm, nb), lambda n, m: (m, n))]


def ple_fwd(un, pb, wpg, wpp, h):
    L, D = un.shape
    ns, P, nb = wpp.shape
    tm = _tile(L, 512)

    def body(un_ref, p_ref, wg_ref, wp_ref, h_ref, o_ref):
        gate = _sigmoid(_dot(un_ref[...], wg_ref[...], 1, 0))
        o_ref[...] = h_ref[...] + _dot(p_ref[...], wp_ref[0], 1, 0) * gate

    return pl.pallas_call(
        body, name="ple_fwd", grid=(ns, L // tm),
        in_specs=_ple_specs(L, D, P, tm, nb),
        out_specs=pl.BlockSpec((tm, nb), lambda n, m: (m, n)),
        out_shape=S((L, D), F32), compiler_params=_params("parallel", "parallel"),
    )(un, pb, wpg, wpp, h)


def loss_head(h, g, target):
    L, D = h.shape
    tm = _tile(L, 256)

    def body(h_ref, g_ref, t_ref, loss_ref, dh_ref, dg_ref):
        m = pl.program_id(0)
        x = h_ref[...]
        gg = g_ref[...]
        e = x * _rstd(x) * gg - t_ref[...]
        dx, dg = _rms_bwd(x, gg, e * (1.0 / D))
        dh_ref[...] = dx
        part = jnp.full((8, 128), 0.5 / D, F32) * jnp.sum(e * e)

        @pl.when(m == 0)
        def _():
            loss_ref[...] = part
            dg_ref[...] = dg

        @pl.when(m > 0)
        def _():
            loss_ref[...] += part
            dg_ref[...] += dg

    return pl.pallas_call(
        body, name="loss_head", grid=(L // tm,),
        in_specs=[pl.BlockSpec((tm, D), lambda m: (m, 0)), pl.BlockSpec((1, D), lambda m: (0, 0)),
                  pl.BlockSpec((tm, D), lambda m: (m, 0))],
        out_specs=[pl.BlockSpec((8, 128), lambda m: (0, 0)), pl.BlockSpec((tm, D), lambda m: (m, 0)),
                   pl.BlockSpec((1, D), lambda m: (0, 0))],
        out_shape=[S((8, 128), F32), S((L, D), F32), S((1, D), F32)],
        compiler_params=_params("arbitrary"),
    )(h, g.reshape(1, D), target)


def ple_bwd(un, pb, wpg, wpp, dh, token):
    L, D = un.shape
    ns, P, nb = wpp.shape
    tm = _tile(L, 512)

    def body(un_ref, p_ref, wg_ref, wp_ref, dh_ref, tok_ref, dpre_ref, dpp_ref):
        gate = _sigmoid(_dot(un_ref[...], wg_ref[...], 1, 0))
        pp = _dot(p_ref[...], wp_ref[0], 1, 0)
        d = dh_ref[...] + tok_ref[0:1, 0:1]
        dpp_ref[0] = (d * gate).astype(BF16)
        dpre_ref[...] = (d * pp * gate * (1.0 - gate)).astype(BF16)

    return pl.pallas_call(
        body, name="ple_bwd", grid=(ns, L // tm),
        in_specs=_ple_specs(L, D, P, tm, nb) + [pl.BlockSpec((8, 128), lambda n, m: (0, 0))],
        out_specs=[pl.BlockSpec((tm, nb), lambda n, m: (m, n)), pl.BlockSpec((1, tm, nb), lambda n, m: (n, m, 0))],
        out_shape=[S((L, D), BF16), S((ns, L, nb), BF16)], compiler_params=_params("parallel", "parallel"),
    )(un, pb, wpg, wpp, dh, token)


def wgrad(a, b):
    a3 = a if a.ndim == 3 else a[None]
    b3 = b if b.ndim == 3 else b[None]
    ns = max(a3.shape[0], b3.shape[0])
    _, L, Ka = a3.shape
    N = b3.shape[2]
    a_map = (lambda s: (s, 0, 0)) if a3.shape[0] > 1 else (lambda s: (0, 0, 0))
    b_map = (lambda s: (s, 0, 0)) if b3.shape[0] > 1 else (lambda s: (0, 0, 0))

    def body(a_ref, b_ref, o_ref):
        o_ref[0] = _dot(a_ref[0], b_ref[0], 0, 0).astype(BF16)

    return pl.pallas_call(
        body, name="wgrad", grid=(ns,),
        in_specs=[pl.BlockSpec((1, L, Ka), a_map), pl.BlockSpec((1, L, N), b_map)],
        out_specs=pl.BlockSpec((1, Ka, N), lambda s: (s, 0, 0)),
        out_shape=S((ns, Ka, N), BF16), compiler_params=_params("parallel"),
    )(a3, b3)


def dx_rms(pairs, h, g, dh_in, cast_scale):
    L, D = h.shape
    nk = pairs[0][0].shape[0]
    n_pairs = len(pairs)
    tm = _tile(L, 512)
    n_m = L // tm
    w_dims = [0 if transposed else 1 for _, _, transposed in pairs]

    def body(*refs):
        ins, (h_ref, g_ref, dhi_ref, dho_ref, dhb_ref, dg_ref, acc) = refs[:2 * n_pairs], refs[2 * n_pairs:]
        m, k = pl.program_id(0), pl.program_id(1)
        part = _dot(ins[0][0], ins[1][0], 1, w_dims[0])
        for i in range(1, n_pairs):
            part += _dot(ins[2 * i][0], ins[2 * i + 1][0], 1, w_dims[i])

        @pl.when(k == 0)
        def _():
            acc[...] = part

        @pl.when(k > 0)
        def _():
            acc[...] += part

        @pl.when(k == nk - 1)
        def _():
            dx, dg = _rms_bwd(h_ref[...], g_ref[...], acc[...])
            dh_out = dhi_ref[...] + dx
            dho_ref[...] = dh_out
            dhb_ref[...] = (cast_scale * dh_out).astype(BF16)

            @pl.when(m == 0)
            def _():
                dg_ref[...] = dg

            @pl.when(m > 0)
            def _():
                dg_ref[...] += dg

    in_specs, args = [], []
    for a3, w3, _ in pairs:
        Kc = a3.shape[2]
        in_specs += [pl.BlockSpec((1, tm, Kc), lambda m, k: (k, m, 0)), pl.BlockSpec((1,) + w3.shape[1:], lambda m, k: (k, 0, 0))]
        args += [a3, w3]
    tile = pl.BlockSpec((tm, D), lambda m, k: (m, 0))
    row = pl.BlockSpec((1, D), lambda m, k: (0, 0))
    return pl.pallas_call(
        body, name="dx_rms", grid=(n_m, nk),
        in_specs=in_specs + [tile, row, tile], out_specs=[tile, tile, row],
        out_shape=[S((L, D), F32), S((L, D), BF16), S((1, D), F32)], scratch_shapes=[pltpu.VMEM((tm, D), F32)],
        compiler_params=_params("arbitrary", "arbitrary"),
    )(*args, h, g.reshape(1, D), dh_in)


def dact_plain(dhb, w3):
    L, D = dhb.shape
    ns, N, _ = w3.shape
    tm = _tile(L, 512)

    def body(d_ref, w_ref, o_ref):
        o_ref[0] = _dot(d_ref[...], w_ref[0], 1, 1)

    return pl.pallas_call(
        body, name="dact_plain", grid=(ns, L // tm),
        in_specs=[pl.BlockSpec((tm, D), lambda s, m: (m, 0)), pl.BlockSpec((1, N, D), lambda s, m: (s, 0, 0))],
        out_specs=pl.BlockSpec((1, tm, N), lambda s, m: (s, m, 0)),
        out_shape=S((ns, L, N), F32), compiler_params=_params("parallel", "parallel"),
    )(dhb, w3)


def dact_swiglu(dhb, wd, a3, b3):
    L, D = dhb.shape
    ns, F, _ = wd.shape
    tm = _tile(L, 512)

    def body(d_ref, w_ref, a_ref, b_ref, da_ref, db_ref):
        ds = _dot(d_ref[...], w_ref[0], 1, 1)
        a = a_ref[0].astype(F32)
        b = b_ref[0].astype(F32)
        sg = _sigmoid(a)
        da_ref[0] = (ds * b * (sg * (1.0 + a * (1.0 - sg)))).astype(BF16)
        db_ref[0] = (ds * (a * sg)).astype(BF16)

    t_spec = pl.BlockSpec((1, tm, F), lambda s, m: (s, m, 0))
    return pl.pallas_call(
        body, name="dact_swiglu", grid=(ns, L // tm),
        in_specs=[pl.BlockSpec((tm, D), lambda s, m: (m, 0)), pl.BlockSpec((1, F, D), lambda s, m: (s, 0, 0)), t_spec, t_spec],
        out_specs=[t_spec, t_spec], out_shape=[S((ns, L, F), BF16)] * 2,
        compiler_params=_params("parallel", "parallel"),
    )(dhb, wd, a3, b3)


def conv_bwd(z, conv_w, conv_b, gnorm, dyn):
    _, L, C = z.shape
    tm = _tile(L, 256)
    H = CONV_HALO
    T = tm + 2 * H

    def body(zb_ref, zbp_ref, zbn_ref, zc_ref, zcp_ref, zcn_ref, zv_ref, zvp_ref, zvn_ref, d_ref, dp_ref, dn_ref,
             w_ref, b_ref, g_ref, dz_ref, dw_ref, db_ref, dg_ref):
        m = pl.program_id(0)
        cat = lambda p, c, n: jnp.concatenate([p[0], c[0], n[0]], axis=0)
        zb, zc, zv, d = cat(zbp_ref, zb_ref, zbn_ref), cat(zcp_ref, zc_ref, zcn_ref), cat(zvp_ref, zv_ref, zvn_ref), cat(dp_ref, d_ref, dn_ref)
        grow = m * tm - H + lax.broadcasted_iota(jnp.int32, (T, C), 0)
        valid, v, v1, v2, cb, ya = _conv_core(zb, zc, zv, w_ref, b_ref[...], grow, L)
        dya, _ = _rms_bwd(ya, g_ref[...], d)
        dc = jnp.where(valid, dya * zb, 0.0)
        dv = w_ref[2:3, :] * dc + w_ref[1:2, :] * pltpu.roll(dc, T - 1, 0) + w_ref[0:1, :] * pltpu.roll(dc, T - 2, 0)
        dz_ref[0] = (dya * cb)[H:H + tm, :].astype(BF16)
        dz_ref[1] = (dv * zv)[H:H + tm, :].astype(BF16)
        dz_ref[2] = (dv * zc)[H:H + tm, :].astype(BF16)
        rs = lambda x: jnp.sum(x[H:H + tm, :], axis=0, keepdims=True)
        yh = ya * _rstd(ya)
        dw = jnp.concatenate([rs(dc * v2), rs(dc * v1), rs(dc * v)], axis=0)
        dbias, dg = rs(dc), rs(d * yh)

        @pl.when(m == 0)
        def _():
            dw_ref[...] = dw
            db_ref[...] = dbias
            dg_ref[...] = dg

        @pl.when(m > 0)
        def _():
            dw_ref[...] += dw
            db_ref[...] += dbias
            dg_ref[...] += dg

    row = lambda r: pl.BlockSpec((r, C), lambda m: (0, 0))
    specs = [*_conv_specs(L, tm, C, 0), *_conv_specs(L, tm, C, 1), *_conv_specs(L, tm, C, 2), *_conv_specs(L, tm, C, 0)]
    return pl.pallas_call(
        body, name="conv_bwd", grid=(L // tm,),
        in_specs=specs + [row(3), row(1), row(1)],
        out_specs=[pl.BlockSpec((3, tm, C), lambda m: (0, m, 0)), row(3), row(1), row(1)],
        out_shape=[S((3, L, C), BF16), S((3, C), F32), S((1, C), F32), S((1, C), F32)],
        compiler_params=_params("arbitrary"),
    )(z, z, z, z, z, z, z, z, z, dyn, dyn, dyn, conv_w, conv_b.reshape(1, C), gnorm.reshape(1, C))


def glu_bwd(y, w, b, gnorm, dn):
    L, C = y.shape
    tm = _tile(L, 256)

    def body(y_ref, w_ref, b_ref, g_ref, d_ref, dy_ref, dpre_ref, zg_ref, db_ref, dg_ref):
        m = pl.program_id(0)
        yv = y_ref[...]
        zg, t = _gelu(yv)
        zgb = zg.astype(BF16)
        sg = _sigmoid(_dot(zgb, w_ref[...], 1, 0) + b_ref[...])
        out = zg * sg
        dout, dg = _rms_bwd(out, g_ref[...], d_ref[...])
        dpre = dout * zg * sg * (1.0 - sg)
        dpre_b = dpre.astype(BF16)
        dzg = dout * sg + _dot(dpre_b, w_ref[...], 1, 1)
        dt = (1.0 - t * t) * _GELU_C * (1.0 + 3.0 * 0.044715 * yv * yv)
        dy_ref[...] = dzg * (0.5 * (1.0 + t) + 0.5 * yv * dt)
        dpre_ref[...] = dpre_b
        zg_ref[...] = zgb
        dbias = jnp.sum(dpre, axis=0, keepdims=True)

        @pl.when(m == 0)
        def _():
            db_ref[...] = dbias
            dg_ref[...] = dg

        @pl.when(m > 0)
        def _():
            db_ref[...] += dbias
            dg_ref[...] += dg

    tile = pl.BlockSpec((tm, C), lambda m: (m, 0))
    row = pl.BlockSpec((1, C), lambda m: (0, 0))
    return pl.pallas_call(
        body, name="glu_bwd", grid=(L // tm,),
        in_specs=[tile, pl.BlockSpec((C, C), lambda m: (0, 0)), row, row, tile],
        out_specs=[tile, tile, tile, row, row],
        out_shape=[S((L, C), F32), S((L, C), BF16), S((L, C), BF16), S((1, C), F32), S((1, C), F32)],
        compiler_params=_params("arbitrary"),
    )(y, w, b.reshape(1, C), gnorm.reshape(1, C), dn)


def _scan_bwd(gr_ref, gi_ref, hr_ref, hi_ref, lr, li, n_steps):
    W = gr_ref.shape[1]
    zero = jnp.zeros((N_SEG, W), F32)
    lic = -li

    def local(i, c):
        r = pl.multiple_of((n_steps - 1 - i) * N_SEG, N_SEG)
        pr, pi = _cmul(lr, lic, c[0], c[1])
        nr = pr + gr_ref[pl.ds(r, N_SEG), :]
        ni = pi + gi_ref[pl.ds(r, N_SEG), :]
        gr_ref[pl.ds(r, N_SEG), :] = nr
        gi_ref[pl.ds(r, N_SEG), :] = ni
        return nr, ni

    fr, fi = lax.fori_loop(0, n_steps, local, (zero, zero))
    qr, qi = _cpow(lr, lic, n_steps)
    row = lax.broadcasted_iota(jnp.int32, (N_SEG, W), 0)
    cr, ci = zero, zero
    for seg in range(N_SEG - 2, -1, -1):
        tr, ti = _cmul(qr, qi, cr, ci)
        sr = pltpu.roll(fr + tr, N_SEG - 1, 0)
        si = pltpu.roll(fi + ti, N_SEG - 1, 0)
        cr = jnp.where(row == seg, sr, cr)
        ci = jnp.where(row == seg, si, ci)

    def fix(i, c):
        pwr, pwi, ar, ai = c
        t = n_steps - 1 - i
        r = pl.multiple_of(t * N_SEG, N_SEG)
        pwr, pwi = _cmul(lr, lic, pwr, pwi)
        xr, xi = _cmul(pwr, pwi, cr, ci)
        g_r = gr_ref[pl.ds(r, N_SEG), :] + xr
        g_i = gi_ref[pl.ds(r, N_SEG), :] + xi
        gr_ref[pl.ds(r, N_SEG), :] = g_r
        gi_ref[pl.ds(r, N_SEG), :] = g_i
        rp = pl.multiple_of(jnp.maximum(t - 1, 0) * N_SEG, N_SEG)
        hpr = hr_ref[pl.ds(rp, N_SEG), :]
        hpi = hi_ref[pl.ds(rp, N_SEG), :]
        live = t > 0
        ar = ar + jnp.where(live, hpr * g_r + hpi * g_i, 0.0)
        ai = ai + jnp.where(live, hpr * g_i - hpi * g_r, 0.0)
        return pwr, pwi, ar, ai

    _, _, ar, ai = lax.fori_loop(0, n_steps, fix, (jnp.ones((N_SEG, W), F32), zero, zero, zero))
    last = pl.ds((n_steps - 1) * N_SEG, N_SEG)
    hpr = jnp.where(row == 0, 0.0, pltpu.roll(hr_ref[last, :], 1, 0))
    hpi = jnp.where(row == 0, 0.0, pltpu.roll(hi_ref[last, :], 1, 0))
    g_r, g_i = gr_ref[pl.ds(0, N_SEG), :], gi_ref[pl.ds(0, N_SEG), :]
    ar = ar + hpr * g_r + hpi * g_i
    ai = ai + hpr * g_i - hpi * g_r
    return jnp.sum(ar, axis=0, keepdims=True), jnp.sum(ai, axis=0, keepdims=True)


def ssm_bwd(us, dy, lam, bre, bim, cre, cim, dvec):
    L = us.shape[0]
    n_steps = L // N_SEG
    sp = _ssm_specs(L)

    def body(u_ref, dy_ref, lam_ref, bre_ref, bim_ref, cre_ref, cim_ref, d_ref,
             du_ref, dlam_ref, dbre_ref, dbim_ref, dcre_ref, dcim_ref, dd_ref, hr, hi, gr, gi):
        u = u_ref[...]
        ub = u.astype(BF16)
        dyv = dy_ref[...]
        dyb = dyv.astype(BF16)
        hr[...] = _dot(ub, bre_ref[0], 1, 0)
        hi[...] = _dot(ub, bim_ref[0], 1, 0)
        lr = jnp.broadcast_to(lam_ref[0:1, :], (N_SEG, 512))
        li = jnp.broadcast_to(lam_ref[1:2, :], (N_SEG, 512))
        _scan_fwd(hr, hi, lr, li, n_steps)
        dcre_ref[0] = _dot(hr[...].astype(BF16), dyb, 0, 0)
        dcim_ref[0] = -_dot(hi[...].astype(BF16), dyb, 0, 0)
        gr[...] = _dot(dyb, cre_ref[0], 1, 1)
        gi[...] = -_dot(dyb, cim_ref[0], 1, 1)
        dlr, dli = _scan_bwd(gr, gi, hr, hi, lr, li, n_steps)
        dlam_ref[...] = jnp.concatenate([dlr, dli], axis=0)
        grb, gib = gr[...].astype(BF16), gi[...].astype(BF16)
        du_ref[...] = _dot(grb, bre_ref[0], 1, 1) + _dot(gib, bim_ref[0], 1, 1) + d_ref[...] * dyv
        dbre_ref[0] = _dot(ub, grb, 0, 0)
        dbim_ref[0] = _dot(ub, gib, 0, 0)
        dd_ref[...] = jnp.sum(dyv * u, axis=0, keepdims=True)

    big = pltpu.VMEM((L, 512), F32)
    return pl.pallas_call(
        body, name="ssm_bwd", grid=(4,),
        in_specs=[sp["u"], sp["u"], sp["lam"], sp["bmat"], sp["bmat"], sp["cmat"], sp["cmat"], sp["d"]],
        out_specs=[sp["u"], sp["lam"], sp["bmat"], sp["bmat"], sp["cmat"], sp["cmat"], sp["d"]],
        out_shape=[S((L, 512), F32), S((2, 2048), F32), S((4, 128, 512), F32), S((4, 128, 512), F32),
                   S((4, 512, 128), F32), S((4, 512, 128), F32), S((1, 512), F32)],
        scratch_shapes=[big, big, big, big], compiler_params=_params("parallel"),
    )(us, dy, lam, bre, bim, cre, cim, dvec)


def _discretize(ar, ai, log_dt, br, bi):
    dt = jnp.exp(log_dt)
    mag = jnp.exp(ar * dt)
    ph = ai * dt
    lr, li = mag * jnp.cos(ph), mag * jnp.sin(ph)
    nr, ni = lr - 1.0, li
    den = ar * ar + ai * ai
    fr = (nr * ar + ni * ai) / den
    fi = (ni * ar - nr * ai) / den
    return lr, li, fr[..., None] * br - fi[..., None] * bi, fr[..., None] * bi + fi[..., None] * br


def ssm_prep(ar, ai, log_dt, br, bi):
    G, P, H = br.shape

    def body(ar_ref, ai_ref, dt_ref, br_ref, bi_ref, lr_ref, li_ref, bbr_ref, bbi_ref):
        lr_ref[...], li_ref[...], bbr_ref[...], bbi_ref[...] = _discretize(
            ar_ref[...], ai_ref[...], dt_ref[...], br_ref[...], bi_ref[...])

    return pl.pallas_call(
        body, name="ssm_prep",
        out_shape=[S((G, P), F32), S((G, P), F32), S((G, P, H), F32), S((G, P, H), F32)],
    )(ar, ai, log_dt.reshape(G, 1), br, bi)


def ssm_prep_bwd(ar, ai, log_dt, br, bi, dlr, dli, dbbr, dbbi):
    G, P, H = br.shape

    def body(ar_ref, ai_ref, dt_ref, br_ref, bi_ref, dlr_ref, dli_ref, dbbr_ref, dbbi_ref,
             dar_ref, dai_ref, ddt_ref, dbr_ref, dbi_ref):
        _, vjp = jax.vjp(_discretize, ar_ref[...], ai_ref[...], dt_ref[...], br_ref[...], bi_ref[...])
        dar_ref[...], dai_ref[...], ddt_ref[...], dbr_ref[...], dbi_ref[...] = vjp(
            (dlr_ref[...], dli_ref[...], dbbr_ref[...], dbbi_ref[...]))

    return pl.pallas_call(
        body, name="ssm_prep_bwd",
        out_shape=[S((G, P), F32), S((G, P), F32), S((G, 1), F32), S((G, P, H), F32), S((G, P, H), F32)],
    )(ar, ai, log_dt.reshape(G, 1), br, bi, dlr, dli, dbbr, dbbi)


def _block_diag(x):
    j, n, R, C = x.shape
    eye = jnp.eye(n, dtype=x.dtype)
    return (x[:, :, :, None, :] * eye[None, :, None, :, None]).reshape(j, n * R, n * C)


def _block_diag_take(x, R, C):
    j = x.shape[0]
    n = x.shape[1] // R
    x5 = x.reshape(j, n, R, n, C)
    return jnp.stack([x5[:, i, :, i, :] for i in range(n)], axis=1)


def _to_segments(x):
    L, C = x.shape
    return x.reshape(N_SEG, L // N_SEG, C).transpose(1, 0, 2).reshape(L, C)


def _from_segments(x):
    L, C = x.shape
    return x.reshape(L // N_SEG, N_SEG, C).transpose(1, 0, 2).reshape(L, C)


BIG = ("ffn1_w_gate", "ffn1_w_up", "ffn1_w_down", "w_in", "glu_w", "w_out",
       "ffn2_w_gate", "ffn2_w_up", "ffn2_w_down", "ple_w_gate", "ple_w_proj")
SMALL = ("ffn1_norm", "mix_norm", "conv_w", "conv_b", "ssm_A_re", "ssm_A_im", "ssm_B_re", "ssm_B_im", "ssm_C_re", "ssm_C_im",
         "ssm_D", "ssm_log_dt", "glu_b", "conv_out_norm", "ssm_out_norm", "ffn2_norm", "ple_norm")


def _ssm_mats(w):
    G, P, H = w["ssm_B_re"].shape
    lr, li, bbr, bbi = ssm_prep(w["ssm_A_re"], w["ssm_A_im"], w["ssm_log_dt"], w["ssm_B_re"], w["ssm_B_im"])
    lam = jnp.stack([lr.reshape(G * P), li.reshape(G * P)])
    bmat = lambda bb: _block_diag(bb.reshape(4, G // 4, P, H).transpose(0, 1, 3, 2)).astype(BF16)
    cmat = lambda c: _block_diag(c.reshape(4, G // 4, H, P).transpose(0, 1, 3, 2)).astype(BF16)
    return lam, bmat(bbr), bmat(bbi), cmat(w["ssm_C_re"]), cmat(w["ssm_C_im"]), w["ssm_D"].reshape(1, G * H)


def layer_fwd(h0, pb, w):
    L, D = h0.shape
    u1 = rmsnorm_fwd(h0, w["ffn1_norm"])
    a1, b1, s1 = ffn_up(u1, w["ffn1_w_gate"], w["ffn1_w_up"])
    h1, u2 = mm_shard_k(s1, w["ffn1_w_down"], h0, 0.5, w["mix_norm"])
    z = mm_shard_n(u2, w["w_in"], F32)
    ya_n = conv_fwd(z, w["conv_w"], w["conv_b"], w["conv_out_norm"])
    us = _to_segments(z[3])
    mats = _ssm_mats(w)
    y = ssm_fwd(us, *mats)
    ys_n = glu_fwd(y, w["glu_w"], w["glu_b"], w["ssm_out_norm"])
    ycat = jnp.stack([ya_n, _from_segments(ys_n)])
    h2, u3 = mm_shard_k(ycat, w["w_out"], h1, 1.0, w["ffn2_norm"])
    a2, b2, s2 = ffn_up(u3, w["ffn2_w_gate"], w["ffn2_w_up"])
    h3, un = mm_shard_k(s2, w["ffn2_w_down"], h2, 0.5, w["ple_norm"])
    h4 = ple_fwd(un, pb, w["ple_w_gate"], w["ple_w_proj"], h3)
    saved = dict(h0=h0, u1=u1, a1=a1, b1=b1, s1=s1, h1=h1, u2=u2, z=z, us=us, mats=mats, y=y, ycat=ycat,
                 h2=h2, u3=u3, a2=a2, b2=b2, s2=s2, h3=h3, un=un)
    return h4, saved


def _ffn_bwd(dh, dhb, h_in, u, a, b, s, wg, wu, wd, gnorm, cast_scale):
    da, db = dact_swiglu(dhb, wd, a, b)
    g_wd = wgrad(s, dhb)
    g_wg = wgrad(da, u)
    g_wu = wgrad(db, u)
    dh_in, dhb_in, g_norm = dx_rms([(da, wg, True), (db, wu, True)], h_in, gnorm, dh, cast_scale)
    return dh_in, dhb_in, g_wg, g_wu, g_wd, g_norm


def layer_bwd(dh, pb, w, sv, token):
    L, D = dh.shape
    G, P, H = w["ssm_B_re"].shape
    dpre, dpp3 = ple_bwd(sv["un"], pb, w["ple_w_gate"], w["ple_w_proj"], dh, token)
    g_wpg = wgrad(sv["un"], dpre).reshape(N_SHARD, D // N_SHARD, D)
    g_wpp = wgrad(pb, dpp3)
    dh3, dhb3, g_nple = dx_rms([(dpre[None], w["ple_w_gate"][None], False)], sv["h3"], w["ple_norm"], dh, 0.5)
    dh2, dhb, g_wg2, g_wu2, g_wd2, g_nffn2 = _ffn_bwd(dh3, dhb3, sv["h2"], sv["u3"], sv["a2"], sv["b2"], sv["s2"],
                                                      w["ffn2_w_gate"], w["ffn2_w_up"], w["ffn2_w_down"], w["ffn2_norm"], 1.0)
    dyn = dact_plain(dhb, w["w_out"])
    g_wout = wgrad(sv["ycat"], dhb).reshape(N_SHARD, -1, D)
    dz_abc, g_convw, g_convb, g_nconv = conv_bwd(sv["z"], w["conv_w"], w["conv_b"], w["conv_out_norm"], dyn)
    dy, dpre_g, zg, g_glub, g_nssm = glu_bwd(sv["y"], w["glu_w"], w["glu_b"], w["ssm_out_norm"], _to_segments(dyn[1]))
    C = zg.shape[1]
    g_gluw = wgrad(zg, dpre_g).reshape(N_SHARD, C // N_SHARD, C)
    dus, dlam, dbre, dbim, dcre, dcim, dd = ssm_bwd(sv["us"], dy, *sv["mats"])
    take_b = lambda m: _block_diag_take(m, H, P).transpose(0, 1, 3, 2).reshape(G, P, H)
    take_c = lambda m: _block_diag_take(m, P, H).transpose(0, 1, 3, 2).reshape(G, H, P)
    g_ar, g_ai, g_dt, g_br, g_bi = ssm_prep_bwd(
        w["ssm_A_re"], w["ssm_A_im"], w["ssm_log_dt"], w["ssm_B_re"], w["ssm_B_im"],
        dlam[0].reshape(G, P), dlam[1].reshape(G, P), take_b(dbre), take_b(dbim))
    dz3 = jnp.concatenate([dz_abc, _from_segments(dus).astype(BF16)[None]], axis=0)
    g_win = wgrad(sv["u2"], dz3)
    dh1, dhb1, g_nmix = dx_rms([(dz3, w["w_in"], False)], sv["h1"], w["mix_norm"], dh2, 0.5)
    dh0, _, g_wg1, g_wu1, g_wd1, g_nffn1 = _ffn_bwd(dh1, dhb1, sv["h0"], sv["u1"], sv["a1"], sv["b1"], sv["s1"],
                                                    w["ffn1_w_gate"], w["ffn1_w_up"], w["ffn1_w_down"], w["ffn1_norm"], 1.0)
    big = [g_wg1, g_wu1, g_wd1, g_win, g_gluw, g_wout, g_wg2, g_wu2, g_wd2, g_wpg, g_wpp]
    small = dict(ffn1_norm=g_nffn1, mix_norm=g_nmix, conv_w=g_convw, conv_b=g_convb, ssm_A_re=g_ar, ssm_A_im=g_ai,
                 ssm_B_re=g_br, ssm_B_im=g_bi, ssm_C_re=take_c(dcre), ssm_C_im=take_c(dcim), ssm_D=dd,
                 ssm_log_dt=g_dt, glu_b=g_glub, conv_out_norm=g_nconv, ssm_out_norm=g_nssm, ffn2_norm=g_nffn2,
                 ple_norm=g_nple)
    return dh0, big, small


def local_step(x, p, target, final_norm, weights_of, on_grads):
    depth = p.shape[0]
    h = x
    layers, saved, pbs = [], [], []
    for i in range(depth):
        w = weights_of(i, h)
        pb = p[i].astype(BF16)
        h, sv = layer_fwd(h, pb, w)
        layers.append(w)
        saved.append(sv)
        pbs.append(pb)
    loss_part, dh, g_final = loss_head(h, final_norm, target)
    smalls = [None] * depth
    token = jnp.zeros((8, 128), F32)
    for i in reversed(range(depth)):
        dh, big, smalls[i] = layer_bwd(dh, pbs[i], layers[i], saved[i], token)
        token = on_grads(i, big, dh)
    return loss_part, dh, smalls, g_final


def _row_tile(rows):
    for t in (512, 352, 256, 128, 64, 32, 16):
        if rows % t == 0:
            return t
    return rows


def elementwise(fn, ins, out_dtypes, name):
    rows, cols = ins[0].shape
    tr = _row_tile(rows)
    n_in = len(ins)

    def body(*refs):
        outs = fn(*[r[...] for r in refs[:n_in]])
        for o_ref, o in zip(refs[n_in:], outs):
            o_ref[...] = o.astype(o_ref.dtype)

    spec = pl.BlockSpec((tr, cols), lambda i: (i, 0))
    return pl.pallas_call(
        body, name=name, grid=(rows // tr,), in_specs=[spec] * n_in, out_specs=[spec] * len(out_dtypes),
        out_shape=[S((rows, cols), d) for d in out_dtypes], compiler_params=_params("parallel"),
    )(*ins)


def _adamw(w, g, m, v):
    m = ADAM_B1 * m + (1.0 - ADAM_B1) * g
    v = ADAM_B2 * v + (1.0 - ADAM_B2) * (g * g)
    m_hat = m / (1.0 - ADAM_B1 ** ADAM_STEP)
    v_hat = v / (1.0 - ADAM_B2 ** ADAM_STEP)
    delta = -ADAM_LR * (m_hat / (jnp.sqrt(v_hat) + ADAM_EPS) + ADAM_WD * w)
    return delta, m, v


ANY = pl.BlockSpec(memory_space=pl.ANY)


def _mesh_pos():
    return lax.axis_index("x"), lax.axis_index("y"), lax.axis_index("c")


def _other_chips(x, y):
    return [(1 - x, y), (x, 1 - y), (1 - x, 1 - y)]


def _remote(src, dst, send_sem, recv_sem, device):
    return pltpu.make_async_remote_copy(src_ref=src, dst_ref=dst, send_sem=send_sem, recv_sem=recv_sem,
                                        device_id=device, device_id_type=MESH)


def gather_weights(ws):
    n = len(ws)

    def body(*refs):
        outs = refs[n:2 * n]
        send_sems, recv_sems = refs[2 * n:]
        x, y, c = _mesh_pos()
        me_s = 2 * x + y
        sibling = (x, y, 1 - c)
        chips = _other_chips(x, y)
        n_half = outs[0].shape[0] // 2
        mine, other = pl.ds(c * n_half, n_half), pl.ds((1 - c) * n_half, n_half)
        sent = []
        for t in range(n):
            for j, (cx, cy) in enumerate(chips):
                blk = outs[t].at[mine, me_s]
                cp = _remote(blk, blk, send_sems.at[t, j], recv_sems.at[t, j], (cx, cy, c))
                cp.start()
                sent.append(cp)
        for j, (cx, cy) in enumerate(chips):
            for t in range(n):
                blk = outs[t].at[mine, 2 * cx + cy]
                _remote(blk, blk, send_sems.at[t, j], recv_sems.at[t, j], (cx, cy, c)).wait_recv()
                cp = _remote(blk, blk, send_sems.at[t, 3 + j], recv_sems.at[t, 3 + j], sibling)
                cp.start()
                sent.append(cp)
        for j, (cx, cy) in enumerate(chips):
            for t in range(n):
                blk = outs[t].at[other, 2 * cx + cy]
                _remote(blk, blk, send_sems.at[t, 3 + j], recv_sems.at[t, 3 + j], sibling).wait_recv()
        for cp in sent:
            cp.wait_send()

    return pl.pallas_call(
        body, name="gather_weights", in_specs=[ANY] * n, out_specs=[ANY] * n,
        out_shape=[S(w.shape, w.dtype) for w in ws], input_output_aliases={t: t for t in range(n)},
        scratch_shapes=[pltpu.SemaphoreType.DMA((n, 6)), pltpu.SemaphoreType.DMA((n, 6))],
    )(*ws)


HBM = pl.BlockSpec(memory_space=pltpu.HBM)
SEM = pl.BlockSpec(memory_space=pltpu.SEMAPHORE)
VMEM_WHOLE = pl.BlockSpec(memory_space=pltpu.VMEM)
SPLIT_COPY = pltpu.CompilerParams(has_side_effects=pltpu.SideEffectType.DATAFLOW_SIDE_EFFECTING)


def _hbm(x):
    return pltpu.with_memory_space_constraint(x, pltpu.HBM)


def _half_rows(ref, c):
    r2 = ref.shape[1] // 2
    return pl.ds(pl.multiple_of(c * r2, 8), r2)


def gather_start(bufs, layer, after):
    n, k = len(bufs), len(after)

    def body(*refs):
        ins, send_sems, recv_sems, token = refs[:n], refs[n + k], refs[n + k + 1], refs[2 * n + k + 2]
        x, y, c = _mesh_pos()
        me_s = 2 * x + y
        for t in range(n):
            blk = ins[t].at[me_s, _half_rows(ins[t], c)]
            for j, (cx, cy) in enumerate(_other_chips(x, y)):
                _remote(blk, blk, send_sems.at[3 * t + j], recv_sems.at[3 * t + j], (cx, cy, c)).start()
        token[...] = jnp.zeros_like(token)

    outs = pl.pallas_call(
        body, name=f"gather_start_{layer}", in_specs=[HBM] * n + [ANY] * k, out_specs=[SEM, SEM] + [HBM] * n + [VMEM_WHOLE],
        out_shape=[pltpu.SemaphoreType.DMA((3 * n,)), pltpu.SemaphoreType.DMA((3 * n,))]
        + [pltpu.HBM(b.shape, b.dtype) for b in bufs] + [S((8, 128), F32)],
        input_output_aliases={t: t + 2 for t in range(n)}, compiler_params=SPLIT_COPY,
    )(*[_hbm(b) for b in bufs], *after)
    return outs[0], outs[1], list(outs[2:2 + n]), outs[2 + n]


def gather_wait(bufs, send_sems, recv_sems, after, layer):
    n, n_after = len(bufs), len(after)

    def body(*refs):
        ins, send_ref, recv_ref = refs[:n], refs[n], refs[n + 1]
        x, y, c = _mesh_pos()
        me_s = 2 * x + y
        for t in range(n):
            rows = _half_rows(ins[t], c)
            for j, (cx, cy) in enumerate(_other_chips(x, y)):
                cp = _remote(ins[t].at[me_s, rows], ins[t].at[2 * cx + cy, rows], send_ref.at[3 * t + j], recv_ref.at[3 * t + j],
                             (cx, cy, c))
                cp.wait_send()
                cp.wait_recv()

    outs = pl.pallas_call(
        body, name=f"gather_wait_{layer}", in_specs=[HBM] * n + [SEM, SEM] + [ANY] * n_after, out_specs=[HBM] * n,
        out_shape=[pltpu.HBM(b.shape, b.dtype) for b in bufs],
        input_output_aliases={t: t for t in range(n)}, compiler_params=SPLIT_COPY,
    )(*bufs, send_sems, recv_sems, *after)
    return list(outs)


def gather_forward(bufs):
    n = len(bufs)

    def body(*refs):
        outs = refs[n:2 * n]
        send_sems, recv_sems = refs[2 * n:]
        x, y, c = _mesh_pos()
        copies = []
        for t in range(n):
            rows = _half_rows(outs[t], c)
            for j, (cx, cy) in enumerate(_other_chips(x, y)):
                blk = outs[t].at[2 * cx + cy, rows]
                cp = _remote(blk, blk, send_sems.at[t, j], recv_sems.at[t, j], (x, y, 1 - c))
                cp.start()
                copies.append(cp)
        for cp in copies:
            cp.wait()

    return pl.pallas_call(
        body, name="gather_forward", in_specs=[ANY] * n, out_specs=[ANY] * n,
        out_shape=[S(b.shape, b.dtype) for b in bufs], input_output_aliases={t: t for t in range(n)},
        scratch_shapes=[pltpu.SemaphoreType.DMA((n, 3)), pltpu.SemaphoreType.DMA((n, 3))],
    )(*bufs)


def chips_start(sums, layer):
    n = len(sums)
    lands = [lax.empty((3,) + s.shape[1:], s.dtype) for s in sums]

    def body(*refs):
        a, land, send_sems, recv_sems, token = refs[:n], refs[n:2 * n], refs[2 * n], refs[2 * n + 1], refs[4 * n + 2]
        x, y, c = _mesh_pos()
        for t in range(n):
            for j, (cx, cy) in enumerate(_other_chips(x, y)):
                _remote(a[t].at[2 * cx + cy], land[t].at[j], send_sems.at[3 * t + j], recv_sems.at[3 * t + j], (cx, cy, c)).start()
        token[...] = jnp.zeros_like(token)

    outs = pl.pallas_call(
        body, name=f"chips_start_{layer}", in_specs=[HBM] * (2 * n), out_specs=[SEM, SEM] + [HBM] * (2 * n) + [VMEM_WHOLE],
        out_shape=[pltpu.SemaphoreType.DMA((3 * n,)), pltpu.SemaphoreType.DMA((3 * n,))]
        + [pltpu.HBM(b.shape, b.dtype) for b in sums + lands] + [S((8, 128), F32)],
        input_output_aliases={t: t + 2 for t in range(2 * n)}, compiler_params=SPLIT_COPY,
    )(*[_hbm(b) for b in sums + lands])
    return outs[0], outs[1], list(outs[2:2 + n]), list(outs[2 + n:2 + 2 * n]), outs[2 + 2 * n]


def chips_wait(sums, lands, send_sems, recv_sems, after, layer):
    n, n_after = len(sums), len(after)

    def body(*refs):
        a, land, send_ref, recv_ref = refs[:n], refs[n:2 * n], refs[2 * n], refs[2 * n + 1]
        x, y, c = _mesh_pos()
        for t in range(n):
            for j, (cx, cy) in enumerate(_other_chips(x, y)):
                cp = _remote(a[t].at[2 * cx + cy], land[t].at[j], send_ref.at[3 * t + j], recv_ref.at[3 * t + j], (cx, cy, c))
                cp.wait_send()
                cp.wait_recv()

    outs = pl.pallas_call(
        body, name=f"chips_wait_{layer}", in_specs=[HBM] * (2 * n) + [SEM, SEM] + [ANY] * n_after, out_specs=[HBM] * (2 * n),
        out_shape=[pltpu.HBM(b.shape, b.dtype) for b in sums + lands],
        input_output_aliases={t: t for t in range(2 * n)}, compiler_params=SPLIT_COPY,
    )(*sums, *lands, send_sems, recv_sems, *after)
    return list(outs[n:])


def cast_place_layer(w, layer, pos, dtype):
    _, r, c = w.shape
    tr = _row_tile(r)

    def body(pos_ref, w_ref, o_ref):
        o_ref[0] = w_ref[0].astype(dtype)

    return pl.pallas_call(
        body, name="cast_place_layer",
        grid_spec=pltpu.PrefetchScalarGridSpec(
            num_scalar_prefetch=1, grid=(r // tr,),
            in_specs=[pl.BlockSpec((1, tr, c), lambda i, pos: (layer, i, 0))],
            out_specs=pl.BlockSpec((1, tr, c), lambda i, pos: (pos[1], i, 0))),
        out_shape=S((N_SHARD, r, c), dtype), compiler_params=_params("parallel"),
    )(pos, w)


def cast_place(w, pos, dtype):
    layers, r, c = w.shape
    tr = _row_tile(r)

    def body(pos_ref, w_ref, o_ref):
        o_ref[0, 0] = w_ref[0].astype(dtype)

    return pl.pallas_call(
        body, name="cast_place",
        grid_spec=pltpu.PrefetchScalarGridSpec(
            num_scalar_prefetch=1, grid=(layers, r // tr),
            in_specs=[pl.BlockSpec((1, tr, c), lambda l, i, pos: (l, i, 0))],
            out_specs=pl.BlockSpec((1, 1, tr, c), lambda l, i, pos: (l, pos[1], i, 0))),
        out_shape=S((layers, N_SHARD, r, c), dtype), compiler_params=_params("parallel", "parallel"),
    )(pos, w)


def _pair_copy(g_ref, got_ref, send_sem, recv_sem):
    x, y, c = _mesh_pos()
    r2 = g_ref.shape[1] // 2
    give = pl.ds(pl.multiple_of((1 - c) * r2, 8), r2)
    return _remote(g_ref.at[:, give], got_ref, send_sem, recv_sem, (x, y, 1 - c))


def reduce_pair(gs, after):
    n, k = len(gs), len(after)

    def body(*refs):
        ins, got = refs[:n], refs[n + k:2 * n + k]
        send_sems, recv_sems = refs[2 * n + k:]
        copies = [_pair_copy(ins[t], got[t], send_sems.at[t], recv_sems.at[t]) for t in range(n)]
        for cp in copies:
            cp.start()
        for cp in copies:
            cp.wait()

    return pl.pallas_call(
        body, name="reduce_pair", in_specs=[ANY] * (n + k), out_specs=[ANY] * n,
        out_shape=[S((g.shape[0], g.shape[1] // 2, g.shape[2]), g.dtype) for g in gs],
        scratch_shapes=[pltpu.SemaphoreType.DMA((n,)), pltpu.SemaphoreType.DMA((n,))],
    )(*gs, *after)


def pair_sum(g, got, pos):
    ns, r2, c = got.shape
    tr = _row_tile(r2)
    n_i = r2 // tr

    def body(pos_ref, g_ref, got_ref, sum_ref, own_ref):
        s = pl.program_id(1)
        v = g_ref[0].astype(F32) + got_ref[0].astype(F32)
        sum_ref[0] = v.astype(BF16)

        @pl.when(s == pos_ref[1])
        def _():
            own_ref[...] = v

    return pl.pallas_call(
        body, name="pair_sum",
        grid_spec=pltpu.PrefetchScalarGridSpec(
            num_scalar_prefetch=1, grid=(n_i, ns),
            in_specs=[pl.BlockSpec((1, tr, c), lambda i, s, pos: (s, pos[0] * n_i + i, 0)),
                      pl.BlockSpec((1, tr, c), lambda i, s, pos: (s, i, 0))],
            out_specs=[pl.BlockSpec((1, tr, c), lambda i, s, pos: (s, i, 0)), pl.BlockSpec((tr, c), lambda i, s, pos: (i, 0))]),
        out_shape=[S((ns, r2, c), BF16), S((r2, c), F32)], compiler_params=_params("parallel", "arbitrary"),
    )(pos, g, got)


def chip_sum(own, p2, pos):
    r2, c = own.shape
    tr = _row_tile(r2)
    n_i = r2 // tr

    def body(pos_ref, own_ref, a_ref, b_ref, c_ref, o_ref):
        o_ref[...] = own_ref[...] + a_ref[0].astype(F32) + b_ref[0].astype(F32) + c_ref[0].astype(F32)

    peer = lambda j: pl.BlockSpec((1, tr, c), lambda i, pos: (j, i, 0))
    return pl.pallas_call(
        body, name="chip_sum",
        grid_spec=pltpu.PrefetchScalarGridSpec(
            num_scalar_prefetch=1, grid=(n_i,),
            in_specs=[pl.BlockSpec((tr, c), lambda i, pos: (i, 0)), peer(0), peer(1), peer(2)],
            out_specs=pl.BlockSpec((tr, c), lambda i, pos: (pos[0] * n_i + i, 0))),
        out_shape=S((2 * r2, c), F32), compiler_params=_params("parallel"),
    )(pos, own, p2, p2, p2)


def exchange_halves(rs):
    n = len(rs)

    def body(*refs):
        outs = refs[n:2 * n]
        send_sems, recv_sems = refs[2 * n:]
        x, y, c = _mesh_pos()
        copies = []
        for t in range(n):
            r2 = outs[t].shape[0] // 2
            rows = outs[t].at[pl.ds(pl.multiple_of(c * r2, 8), r2)]
            cp = _remote(rows, rows, send_sems.at[t], recv_sems.at[t], (x, y, 1 - c))
            cp.start()
            copies.append(cp)
        for cp in copies:
            cp.wait()

    return pl.pallas_call(
        body, name="exchange_halves", in_specs=[ANY] * n, out_specs=[ANY] * n,
        out_shape=[S(r.shape, r.dtype) for r in rs], input_output_aliases={t: t for t in range(n)},
        scratch_shapes=[pltpu.SemaphoreType.DMA((n,)), pltpu.SemaphoreType.DMA((n,))],
    )(*rs)


def allreduce_small(vec):
    R = vec.shape[0]
    H = R // 2

    def body(x_ref, o_ref, pair_buf, chip_buf, send_sems, recv_sems):
        x, y, c = _mesh_pos()
        me_s = 2 * x + y
        sibling = (x, y, 1 - c)
        mine = pl.ds(pl.multiple_of(c * H, 8), H)
        give = pl.ds(pl.multiple_of((1 - c) * H, 8), H)
        cp = _remote(x_ref.at[give], pair_buf, send_sems.at[0], recv_sems.at[0], sibling)
        cp.start()
        cp.wait()
        chip_buf[me_s] = x_ref[mine, :] + pair_buf[...]
        copies = []
        for j, (cx, cy) in enumerate(_other_chips(x, y)):
            cp = _remote(chip_buf.at[me_s], chip_buf.at[me_s], send_sems.at[1 + j], recv_sems.at[1 + j], (cx, cy, c))
            cp.start()
            copies.append(cp)
        for cp in copies:
            cp.wait()
        o_ref[mine, :] = (chip_buf[0] + chip_buf[1]) + (chip_buf[2] + chip_buf[3])
        cp = _remote(o_ref.at[mine], o_ref.at[mine], send_sems.at[4], recv_sems.at[4], sibling)
        cp.start()
        cp.wait()

    vm = pl.BlockSpec(memory_space=pltpu.VMEM)
    return pl.pallas_call(
        body, name="allreduce_small", in_specs=[vm], out_specs=vm, out_shape=S((R, 128), F32),
        scratch_shapes=[pltpu.VMEM((H, 128), F32), pltpu.VMEM((N_SHARD, H, 128), F32),
                        pltpu.SemaphoreType.DMA((5,)), pltpu.SemaphoreType.DMA((5,))],
        compiler_params=pltpu.CompilerParams(vmem_limit_bytes=VMEM_LIMIT_BYTES),
    )(vec)


def adamw_layer(w, g, m, v, layer, prev):
    _, r, c = w.shape
    tr = _row_tile(r)

    def body(w_ref, g_ref, m_ref, v_ref, *rest):
        outs = rest[-4:]
        g_val = g_ref[...]
        outs[0][0] = g_val
        outs[1][0], outs[2][0], outs[3][0] = _adamw(w_ref[0], g_val, m_ref[0], v_ref[0])

    lay = pl.BlockSpec((1, tr, c), lambda i: (layer, i, 0))
    prev = list(prev) if prev else []
    return pl.pallas_call(
        body, name="adamw_layer", grid=(r // tr,),
        in_specs=[lay, pl.BlockSpec((tr, c), lambda i: (i, 0)), lay, lay] + [ANY] * len(prev),
        out_specs=[lay] * 4, out_shape=[S(w.shape, F32)] * 4,
        input_output_aliases={4 + k: k for k in range(len(prev))}, compiler_params=_params("parallel"),
    )(w, g, m, v, *prev)


def reduce_begin(gs, pos, layer, after):
    got = reduce_pair(gs, after)
    sums, own = zip(*[pair_sum(g, o, pos) for g, o in zip(gs, got)])
    send_sems, recv_sems, sums, lands, token = chips_start(list(sums), layer)
    return dict(own=own, sums=sums, lands=lands, sems=(send_sems, recv_sems), token=token, layer=layer)


def reduce_end(pending, pos, after):
    lands = chips_wait(pending["sums"], pending["lands"], *pending["sems"], after, pending["layer"])
    return exchange_halves([chip_sum(o, p, pos) for o, p in zip(pending["own"], lands)])


W_NAMES = ("ffn1_norm", "ffn1_w_gate", "ffn1_w_up", "ffn1_w_down", "mix_norm", "w_in", "conv_w", "conv_b", "ssm_A_re", "ssm_A_im",
           "ssm_B_re", "ssm_B_im", "ssm_C_re", "ssm_C_im", "ssm_D", "ssm_log_dt", "glu_w", "glu_b", "conv_out_norm", "ssm_out_norm",
           "w_out", "ffn2_norm", "ffn2_w_gate", "ffn2_w_up", "ffn2_w_down", "ple_norm", "ple_w_gate", "ple_w_proj", "final_norm")
SMALL_ALL = SMALL + ("final_norm",)
TRANSPOSED = ("ffn1_w_gate", "ffn1_w_up", "ffn2_w_gate", "ffn2_w_up")
PACK = 2 * 8 * 128


def _pack(parts):
    flat = jnp.concatenate([p.reshape(-1) for p in parts])
    pad = (-flat.shape[0]) % PACK
    return jnp.pad(flat, (0, pad)).reshape(-1, 128)


def _unpack(vec, shapes):
    flat = vec.reshape(-1)
    out, off = [], 0
    for shp in shapes:
        size = math.prod(shp)
        out.append(flat[off:off + size].reshape(shp))
        off += size
    return out


def _step(a):
    a = {k: jnp.swapaxes(v, 1, 2) if k.removeprefix("m_").removeprefix("v_") in TRANSPOSED else v for k, v in a.items()}
    x, p, target = a["x"][0], a["p"][:, 0], a["loss_target"][0]
    depth = p.shape[0]
    L, D = x.shape
    me_s = 2 * lax.axis_index("x") + lax.axis_index("y")

    pos = jnp.stack([lax.axis_index("c"), me_s]).astype(jnp.int32)
    conv_w = gather_weights([cast_place(a["conv_w"], pos, F32)])[0]
    started = [gather_start([cast_place_layer(a[n], l, pos, BF16) for n in BIG], l, [conv_w]) for l in range(depth)]

    def weights_of(l, h):
        send_sems, recv_sems, bufs, _ = started[l]
        after = [h] if l else [s[3] for s in started]
        full = gather_forward(gather_wait(bufs, send_sems, recv_sems, after, l))
        w = {n: a[n][l] for n in SMALL if n != "conv_w"}
        w.update(dict(zip(BIG, full)))
        C = w["glu_w"].shape[-1]
        w["glu_w"] = w["glu_w"].reshape(C, C)
        w["w_out"] = w["w_out"].reshape(2, -1, D)
        w["ple_w_gate"] = w["ple_w_gate"].reshape(D, D)
        w["conv_w"] = conv_w[l].transpose(1, 0, 2).reshape(3, -1)
        return w

    pending, first_layer_grads = {}, []

    def on_grads(l, big, dh):
        if l == 0:
            first_layer_grads.extend(big)
            return None
        pending[l] = reduce_begin(big, pos, l, [])
        return pending[l]["token"]

    loss_part, dx, smalls, g_final = local_step(x, p, target, a["final_norm"], weights_of, on_grads)
    small_shapes = [(depth,) + smalls[0][n].shape for n in SMALL] + [g_final.shape, (1,)]
    parts = [smalls[l][n] for n in SMALL for l in range(depth)] + [g_final, loss_part[0, 0:1]]
    summed_vec = allreduce_small(_pack(parts))
    pending[0] = reduce_begin(first_layer_grads, pos, 0, [summed_vec])
    stacked = [None] * len(BIG)
    for l in reversed(range(depth)):
        after = [pending[0]["token"]] if l else [s[3] for s in stacked]
        reduced = reduce_end(pending[l], pos, after)
        stacked = [adamw_layer(a[n], reduced[i], a["m_" + n], a["v_" + n], l, stacked[i]) for i, n in enumerate(BIG)]
    big_out = {n: [jnp.swapaxes(o, 1, 2) for o in outs] if n in TRANSPOSED else outs for n, outs in zip(BIG, stacked)}

    summed = _unpack(summed_vec, small_shapes)
    g_small = dict(zip(SMALL_ALL, summed[:-1]))
    loss = summed[-1][0]
    n_conv = a["conv_w"].shape[-1]
    g_small["conv_w"] = lax.dynamic_slice_in_dim(g_small["conv_w"], me_s * n_conv, n_conv, axis=2)
    g_small = {n: g_small[n].reshape(a[n].shape) for n in SMALL_ALL}
    packed = [_pack([src[n] for n in SMALL_ALL]) for src in
              ({n: a[n] for n in SMALL_ALL}, g_small, {n: a["m_" + n] for n in SMALL_ALL}, {n: a["v_" + n] for n in SMALL_ALL})]
    shapes = [a[n].shape for n in SMALL_ALL]
    d_s, m_s, v_s = [dict(zip(SMALL_ALL, _unpack(o, shapes))) for o in elementwise(_adamw, packed, [F32, F32, F32], "adamw_small")]

    outs = {n: big_out[n] if n in big_out else (g_small[n], d_s[n], m_s[n], v_s[n]) for n in W_NAMES}
    return (loss, dx[None], *[outs[n][0] for n in W_NAMES], *[outs[n][1] for n in W_NAMES],
            *[outs[n][2] for n in W_NAMES], *[outs[n][3] for n in W_NAMES])


def kernel(x, p, ffn1_norm, ffn1_w_gate, ffn1_w_up, ffn1_w_down, mix_norm, w_in, conv_w, conv_b, ssm_A_re, ssm_A_im, ssm_B_re, ssm_B_im, ssm_C_re, ssm_C_im, ssm_D, ssm_log_dt, glu_w, glu_b, conv_out_norm, ssm_out_norm, w_out, ffn2_norm, ffn2_w_gate, ffn2_w_up, ffn2_w_down, ple_norm, ple_w_gate, ple_w_proj, final_norm, loss_target, m_ffn1_norm, m_ffn1_w_gate, m_ffn1_w_up, m_ffn1_w_down, m_mix_norm, m_w_in, m_conv_w, m_conv_b, m_ssm_A_re, m_ssm_A_im, m_ssm_B_re, m_ssm_B_im, m_ssm_C_re, m_ssm_C_im, m_ssm_D, m_ssm_log_dt, m_glu_w, m_glu_b, m_conv_out_norm, m_ssm_out_norm, m_w_out, m_ffn2_norm, m_ffn2_w_gate, m_ffn2_w_up, m_ffn2_w_down, m_ple_norm, m_ple_w_gate, m_ple_w_proj, m_final_norm, v_ffn1_norm, v_ffn1_w_gate, v_ffn1_w_up, v_ffn1_w_down, v_mix_norm, v_w_in, v_conv_w, v_conv_b, v_ssm_A_re, v_ssm_A_im, v_ssm_B_re, v_ssm_B_im, v_ssm_C_re, v_ssm_C_im, v_ssm_D, v_ssm_log_dt, v_glu_w, v_glu_b, v_conv_out_norm, v_ssm_out_norm, v_w_out, v_ffn2_norm, v_ffn2_w_gate, v_ffn2_w_up, v_ffn2_w_down, v_ple_norm, v_ple_w_gate, v_ple_w_proj, v_final_norm):
    return _step(dict(locals()))
```

```python
import functools
import math

import jax
import jax.numpy as jnp
from jax import lax
from jax.experimental import pallas as pl
from jax.experimental.pallas import tpu as pltpu

F32, BF16 = jnp.float32, jnp.bfloat16
S = jax.ShapeDtypeStruct
EPS = 1e-6
N_SEG = 8
N_SHARD = 4
N_DEV = 8
VMEM_LIMIT_BYTES = 56 * 1024 * 1024
ADAM_LR, ADAM_B1, ADAM_B2, ADAM_EPS, ADAM_WD, ADAM_STEP = 0.001, 0.9, 0.999, 1e-08, 0.01, 10
MESH = pl.DeviceIdType.MESH


def _params(*sem):
    return pltpu.CompilerParams(dimension_semantics=sem if sem else None, vmem_limit_bytes=VMEM_LIMIT_BYTES)


def _dot(a, b, ca, cb):
    return lax.dot_general(a, b, (((ca,), (cb,)), ((), ())), preferred_element_type=F32)


def _sigmoid(x):
    return 1.0 / (1.0 + jnp.exp(-x))


def _rstd(x):
    return lax.rsqrt(jnp.mean(x * x, axis=-1, keepdims=True) + EPS)


def _rms_bwd(x, g, dy):
    r = _rstd(x)
    xh = x * r
    dxh = dy * g
    dx = r * (dxh - xh * jnp.mean(dxh * xh, axis=-1, keepdims=True))
    return dx, jnp.sum(dy * xh, axis=0, keepdims=True)


def _tile(n, want):
    return want if n % want == 0 else n


def rmsnorm_fwd(h, g):
    L, D = h.shape
    tm = _tile(L, 512)

    def body(h_ref, g_ref, o_ref):
        x = h_ref[...]
        o_ref[...] = (x * _rstd(x) * g_ref[...]).astype(BF16)

    return pl.pallas_call(
        body, name="rmsnorm_fwd", grid=(L // tm,),
        in_specs=[pl.BlockSpec((tm, D), lambda m: (m, 0)), pl.BlockSpec((1, D), lambda m: (0, 0))],
        out_specs=pl.BlockSpec((tm, D), lambda m: (m, 0)),
        out_shape=S((L, D), BF16), compiler_params=_params("parallel"),
    )(h, g.reshape(1, D))


def ffn_up(u, wg, wu):
    L, D = u.shape
    ns, F, _ = wg.shape
    tm = _tile(L, 512)

    def body(u_ref, wg_ref, wu_ref, a_ref, b_ref, s_ref):
        x = u_ref[...]
        a = _dot(x, wg_ref[0], 1, 1)
        b = _dot(x, wu_ref[0], 1, 1)
        a_ref[0] = a.astype(BF16)
        b_ref[0] = b.astype(BF16)
        s_ref[0] = (a * _sigmoid(a) * b).astype(BF16)

    w_spec = pl.BlockSpec((1, F, D), lambda s, m: (s, 0, 0))
    o_spec = pl.BlockSpec((1, tm, F), lambda s, m: (s, m, 0))
    return pl.pallas_call(
        body, name="ffn_up", grid=(ns, L // tm),
        in_specs=[pl.BlockSpec((tm, D), lambda s, m: (m, 0)), w_spec, w_spec],
        out_specs=[o_spec, o_spec, o_spec],
        out_shape=[S((ns, L, F), BF16)] * 3, compiler_params=_params("parallel", "parallel"),
    )(u, wg, wu)


def mm_shard_n(u, w3, out_dtype):
    L, K = u.shape
    ns, _, N = w3.shape
    tm = _tile(L, 512)

    def body(u_ref, w_ref, o_ref):
        o_ref[0] = _dot(u_ref[...], w_ref[0], 1, 0).astype(out_dtype)

    return pl.pallas_call(
        body, name="mm_shard_n", grid=(ns, L // tm),
        in_specs=[pl.BlockSpec((tm, K), lambda s, m: (m, 0)), pl.BlockSpec((1, K, N), lambda s, m: (s, 0, 0))],
        out_specs=pl.BlockSpec((1, tm, N), lambda s, m: (s, m, 0)),
        out_shape=S((ns, L, N), out_dtype), compiler_params=_params("parallel", "parallel"),
    )(u, w3)


def mm_shard_k(a3, w3, res, scale, g_next):
    nk, L, Kc = a3.shape
    N = w3.shape[2]
    tm = _tile(L, 512)

    def body(a_ref, w_ref, r_ref, g_ref, o_ref, u_ref):
        acc = _dot(a_ref[0], w_ref[0], 1, 0)
        for k in range(1, nk):
            acc += _dot(a_ref[k], w_ref[k], 1, 0)
        h = r_ref[...] + scale * acc
        o_ref[...] = h
        u_ref[...] = (h * _rstd(h) * g_ref[...]).astype(BF16)

    tile = pl.BlockSpec((tm, N), lambda m: (m, 0))
    return pl.pallas_call(
        body, name="mm_shard_k", grid=(L // tm,),
        in_specs=[pl.BlockSpec((nk, tm, Kc), lambda m: (0, m, 0)), pl.BlockSpec((nk, Kc, N), lambda m: (0, 0, 0)),
                  tile, pl.BlockSpec((1, N), lambda m: (0, 0))],
        out_specs=[tile, tile],
        out_shape=[S((L, N), F32), S((L, N), BF16)], compiler_params=_params("parallel"),
    )(a3, w3, res, g_next.reshape(1, N))


CONV_HALO = 8


def _conv_specs(L, tm, C, shard):
    nb = L // CONV_HALO
    per = tm // CONV_HALO
    main = pl.BlockSpec((1, tm, C), lambda m: (shard, m, 0))
    prev = pl.BlockSpec((1, CONV_HALO, C), lambda m: (shard, jnp.maximum(m * per - 1, 0), 0))
    nxt = pl.BlockSpec((1, CONV_HALO, C), lambda m: (shard, jnp.minimum((m + 1) * per, nb - 1), 0))
    return main, prev, nxt


def _conv_core(zb, zc, zv, w_ref, bias, grow, L):
    valid = (grow >= 0) & (grow < L)
    v = jnp.where(valid, zc * zv, 0.0)
    v1 = pltpu.roll(v, 1, 0)
    v2 = pltpu.roll(v, 2, 0)
    cb = w_ref[0:1, :] * v2 + w_ref[1:2, :] * v1 + w_ref[2:3, :] * v + bias
    return valid, v, v1, v2, cb, zb * cb


def conv_fwd(z, conv_w, conv_b, gnorm):
    _, L, C = z.shape
    tm = _tile(L, 256)
    H = CONV_HALO

    def body(zb_ref, zc_ref, zcp_ref, zv_ref, zvp_ref, w_ref, b_ref, g_ref, o_ref):
        m = pl.program_id(0)
        zc = jnp.concatenate([zcp_ref[0], zc_ref[0]], axis=0)
        zv = jnp.concatenate([zvp_ref[0], zv_ref[0]], axis=0)
        grow = m * tm - H + lax.broadcasted_iota(jnp.int32, (tm + H, C), 0)
        valid = grow >= 0
        v = jnp.where(valid, zc * zv, 0.0)
        v1 = pltpu.roll(v, 1, 0)
        v2 = pltpu.roll(v, 2, 0)
        cb = (w_ref[0:1, :] * v2 + w_ref[1:2, :] * v1 + w_ref[2:3, :] * v + b_ref[...])[H:, :]
        ya = zb_ref[0] * cb
        o_ref[...] = (ya * _rstd(ya) * g_ref[...]).astype(BF16)

    zb_m, _, _ = _conv_specs(L, tm, C, 0)
    zc_m, zc_p, _ = _conv_specs(L, tm, C, 1)
    zv_m, zv_p, _ = _conv_specs(L, tm, C, 2)
    row = lambda r: pl.BlockSpec((r, C), lambda m: (0, 0))
    return pl.pallas_call(
        body, name="conv_fwd", grid=(L // tm,),
        in_specs=[zb_m, zc_m, zc_p, zv_m, zv_p, row(3), row(1), row(1)],
        out_specs=pl.BlockSpec((tm, C), lambda m: (m, 0)),
        out_shape=S((L, C), BF16), compiler_params=_params("parallel"),
    )(z, z, z, z, z, conv_w, conv_b.reshape(1, C), gnorm.reshape(1, C))


def _cmul(ar, ai, br, bi):
    return ar * br - ai * bi, ar * bi + ai * br


def _scan_fwd(hr_ref, hi_ref, lr, li, n_steps):
    W = hr_ref.shape[1]
    zero = jnp.zeros((N_SEG, W), F32)

    def local(t, c):
        r = pl.multiple_of(t * N_SEG, N_SEG)
        pr, pi = _cmul(lr, li, c[0], c[1])
        nr = pr + hr_ref[pl.ds(r, N_SEG), :]
        ni = pi + hi_ref[pl.ds(r, N_SEG), :]
        hr_ref[pl.ds(r, N_SEG), :] = nr
        hi_ref[pl.ds(r, N_SEG), :] = ni
        return nr, ni

    fr, fi = lax.fori_loop(0, n_steps, local, (zero, zero))
    qr, qi = _cpow(lr, li, n_steps)
    row = lax.broadcasted_iota(jnp.int32, (N_SEG, W), 0)
    cr, ci = zero, zero
    for seg in range(1, N_SEG):
        tr, ti = _cmul(qr, qi, cr, ci)
        sr = pltpu.roll(fr + tr, 1, 0)
        si = pltpu.roll(fi + ti, 1, 0)
        cr = jnp.where(row == seg, sr, cr)
        ci = jnp.where(row == seg, si, ci)

    def fix(t, c):
        r = pl.multiple_of(t * N_SEG, N_SEG)
        pr, pi = _cmul(lr, li, c[0], c[1])
        ar, ai = _cmul(pr, pi, cr, ci)
        hr_ref[pl.ds(r, N_SEG), :] += ar
        hi_ref[pl.ds(r, N_SEG), :] += ai
        return pr, pi

    lax.fori_loop(0, n_steps, fix, (jnp.ones((N_SEG, W), F32), zero))


def _cpow(lr, li, n):
    rr, ri = None, None
    br, bi = lr, li
    while n:
        if n & 1:
            rr, ri = (br, bi) if rr is None else _cmul(rr, ri, br, bi)
        n >>= 1
        if n:
            br, bi = _cmul(br, bi, br, bi)
    return rr, ri


def _ssm_specs(L):
    col = lambda w: pl.BlockSpec((L, w), lambda j: (0, j))
    return dict(
        u=col(128), lam=pl.BlockSpec((2, 512), lambda j: (0, j)),
        bmat=pl.BlockSpec((1, 128, 512), lambda j: (j, 0, 0)), cmat=pl.BlockSpec((1, 512, 128), lambda j: (j, 0, 0)),
        d=pl.BlockSpec((1, 128), lambda j: (0, j)))


def ssm_fwd(us, lam, bre, bim, cre, cim, dvec):
    L = us.shape[0]
    n_steps = L // N_SEG
    sp = _ssm_specs(L)

    def body(u_ref, lam_ref, bre_ref, bim_ref, cre_ref, cim_ref, d_ref, y_ref, hr, hi):
        u = u_ref[...]
        ub = u.astype(BF16)
        hr[...] = _dot(ub, bre_ref[0], 1, 0)
        hi[...] = _dot(ub, bim_ref[0], 1, 0)
        lr = jnp.broadcast_to(lam_ref[0:1, :], (N_SEG, 512))
        li = jnp.broadcast_to(lam_ref[1:2, :], (N_SEG, 512))
        _scan_fwd(hr, hi, lr, li, n_steps)
        y_ref[...] = (_dot(hr[...].astype(BF16), cre_ref[0], 1, 0) - _dot(hi[...].astype(BF16), cim_ref[0], 1, 0)
                      + d_ref[...] * u)

    return pl.pallas_call(
        body, name="ssm_fwd", grid=(4,),
        in_specs=[sp["u"], sp["lam"], sp["bmat"], sp["bmat"], sp["cmat"], sp["cmat"], sp["d"]],
        out_specs=sp["u"], out_shape=S((L, 512), F32),
        scratch_shapes=[pltpu.VMEM((L, 512), F32), pltpu.VMEM((L, 512), F32)],
        compiler_params=_params("parallel"),
    )(us, lam, bre, bim, cre, cim, dvec)


_GELU_C = math.sqrt(2.0 / math.pi)


def _gelu(y):
    t = jnp.tanh(_GELU_C * (y + 0.044715 * y * y * y))
    return 0.5 * y * (1.0 + t), t


def glu_fwd(y, w, b, gnorm):
    L, C = y.shape
    tm = _tile(L, 512)

    def body(y_ref, w_ref, b_ref, g_ref, o_ref):
        zg, _ = _gelu(y_ref[...])
        out = zg * _sigmoid(_dot(zg.astype(BF16), w_ref[...], 1, 0) + b_ref[...])
        o_ref[...] = (out * _rstd(out) * g_ref[...]).astype(BF16)

    row = pl.BlockSpec((1, C), lambda m: (0, 0))
    return pl.pallas_call(
        body, name="glu_fwd", grid=(L // tm,),
        in_specs=[pl.BlockSpec((tm, C), lambda m: (m, 0)), pl.BlockSpec((C, C), lambda m: (0, 0)), row, row],
        out_specs=pl.BlockSpec((tm, C), lambda m: (m, 0)),
        out_shape=S((L, C), BF16), compiler_params=_params("parallel"),
    )(y, w, b.reshape(1, C), gnorm.reshape(1, C))


def _ple_specs(L, D, P, tm, nb):
    return [pl.BlockSpec((tm, D), lambda n, m: (m, 0)), pl.BlockSpec((tm, P), lambda n, m: (m, 0)),
            pl.BlockSpec((D, nb), lambda n, m: (0, n)), pl.BlockSpec((1, P, nb), lambda n, m: (n, 0, 0)),
            pl.BlockSpec((tm, nb), lambda n, m: (m, n))]


def ple_fwd(un, pb, wpg, wpp, h):
    L, D = un.shape
    ns, P, nb = wpp.shape
    tm = _tile(L, 512)

    def body(un_ref, p_ref, wg_ref, wp_ref, h_ref, o_ref):
        gate = _sigmoid(_dot(un_ref[...], wg_ref[...], 1, 0))
        o_ref[...] = h_ref[...] + _dot(p_ref[...], wp_ref[0], 1, 0) * gate

    return pl.pallas_call(
        body, name="ple_fwd", grid=(ns, L // tm),
        in_specs=_ple_specs(L, D, P, tm, nb),
        out_specs=pl.BlockSpec((tm, nb), lambda n, m: (m, n)),
        out_shape=S((L, D), F32), compiler_params=_params("parallel", "parallel"),
    )(un, pb, wpg, wpp, h)


def loss_head(h, g, target):
    L, D = h.shape
    tm = _tile(L, 256)

    def body(h_ref, g_ref, t_ref, loss_ref, dh_ref, dg_ref):
        m = pl.program_id(0)
        x = h_ref[...]
        gg = g_ref[...]
        e = x * _rstd(x) * gg - t_ref[...]
        dx, dg = _rms_bwd(x, gg, e * (1.0 / D))
        dh_ref[...] = dx
        part = jnp.full((8, 128), 0.5 / D, F32) * jnp.sum(e * e)

        @pl.when(m == 0)
        def _():
            loss_ref[...] = part
            dg_ref[...] = dg

        @pl.when(m > 0)
        def _():
            loss_ref[...] += part
            dg_ref[...] += dg

    return pl.pallas_call(
        body, name="loss_head", grid=(L // tm,),
        in_specs=[pl.BlockSpec((tm, D), lambda m: (m, 0)), pl.BlockSpec((1, D), lambda m: (0, 0)),
                  pl.BlockSpec((tm, D), lambda m: (m, 0))],
        out_specs=[pl.BlockSpec((8, 128), lambda m: (0, 0)), pl.BlockSpec((tm, D), lambda m: (m, 0)),
                   pl.BlockSpec((1, D), lambda m: (0, 0))],
        out_shape=[S((8, 128), F32), S((L, D), F32), S((1, D), F32)],
        compiler_params=_params("arbitrary"),
    )(h, g.reshape(1, D), target)


def ple_bwd(un, pb, wpg, wpp, dh, token):
    L, D = un.shape
    ns, P, nb = wpp.shape
    tm = _tile(L, 512)

    def body(un_ref, p_ref, wg_ref, wp_ref, dh_ref, tok_ref, dpre_ref, dpp_ref):
        gate = _sigmoid(_dot(un_ref[...], wg_ref[...], 1, 0))
        pp = _dot(p_ref[...], wp_ref[0], 1, 0)
        d = dh_ref[...] + tok_ref[0:1, 0:1]
        dpp_ref[0] = (d * gate).astype(BF16)
        dpre_ref[...] = (d * pp * gate * (1.0 - gate)).astype(BF16)

    return pl.pallas_call(
        body, name="ple_bwd", grid=(ns, L // tm),
        in_specs=_ple_specs(L, D, P, tm, nb) + [pl.BlockSpec((8, 128), lambda n, m: (0, 0))],
        out_specs=[pl.BlockSpec((tm, nb), lambda n, m: (m, n)), pl.BlockSpec((1, tm, nb), lambda n, m: (n, m, 0))],
        out_shape=[S((L, D), BF16), S((ns, L, nb), BF16)], compiler_params=_params("parallel", "parallel"),
    )(un, pb, wpg, wpp, dh, token)


def wgrad(a, b):
    a3 = a if a.ndim == 3 else a[None]
    b3 = b if b.ndim == 3 else b[None]
    ns = max(a3.shape[0], b3.shape[0])
    _, L, Ka = a3.shape
    N = b3.shape[2]
    a_map = (lambda s: (s, 0, 0)) if a3.shape[0] > 1 else (lambda s: (0, 0, 0))
    b_map = (lambda s: (s, 0, 0)) if b3.shape[0] > 1 else (lambda s: (0, 0, 0))

    def body(a_ref, b_ref, o_ref):
        o_ref[0] = _dot(a_ref[0], b_ref[0], 0, 0).astype(BF16)

    return pl.pallas_call(
        body, name="wgrad", grid=(ns,),
        in_specs=[pl.BlockSpec((1, L, Ka), a_map), pl.BlockSpec((1, L, N), b_map)],
        out_specs=pl.BlockSpec((1, Ka, N), lambda s: (s, 0, 0)),
        out_shape=S((ns, Ka, N), BF16), compiler_params=_params("parallel"),
    )(a3, b3)


def dx_rms(pairs, h, g, dh_in, cast_scale):
    L, D = h.shape
    nk = pairs[0][0].shape[0]
    n_pairs = len(pairs)
    tm = _tile(L, 512)
    n_m = L // tm
    w_dims = [0 if transposed else 1 for _, _, transposed in pairs]

    def body(*refs):
        ins, (h_ref, g_ref, dhi_ref, dho_ref, dhb_ref, dg_ref, acc) = refs[:2 * n_pairs], refs[2 * n_pairs:]
        m, k = pl.program_id(0), pl.program_id(1)
        part = _dot(ins[0][0], ins[1][0], 1, w_dims[0])
        for i in range(1, n_pairs):
            part += _dot(ins[2 * i][0], ins[2 * i + 1][0], 1, w_dims[i])

        @pl.when(k == 0)
        def _():
            acc[...] = part

        @pl.when(k > 0)
        def _():
            acc[...] += part

        @pl.when(k == nk - 1)
        def _():
            dx, dg = _rms_bwd(h_ref[...], g_ref[...], acc[...])
            dh_out = dhi_ref[...] + dx
            dho_ref[...] = dh_out
            dhb_ref[...] = (cast_scale * dh_out).astype(BF16)

            @pl.when(m == 0)
            def _():
                dg_ref[...] = dg

            @pl.when(m > 0)
            def _():
                dg_ref[...] += dg

    in_specs, args = [], []
    for a3, w3, _ in pairs:
        Kc = a3.shape[2]
        in_specs += [pl.BlockSpec((1, tm, Kc), lambda m, k: (k, m, 0)), pl.BlockSpec((1,) + w3.shape[1:], lambda m, k: (k, 0, 0))]
        args += [a3, w3]
    tile = pl.BlockSpec((tm, D), lambda m, k: (m, 0))
    row = pl.BlockSpec((1, D), lambda m, k: (0, 0))
    return pl.pallas_call(
        body, name="dx_rms", grid=(n_m, nk),
        in_specs=in_specs + [tile, row, tile], out_specs=[tile, tile, row],
        out_shape=[S((L, D), F32), S((L, D), BF16), S((1, D), F32)], scratch_shapes=[pltpu.VMEM((tm, D), F32)],
        compiler_params=_params("arbitrary", "arbitrary"),
    )(*args, h, g.reshape(1, D), dh_in)


def dact_plain(dhb, w3):
    L, D = dhb.shape
    ns, N, _ = w3.shape
    tm = _tile(L, 512)

    def body(d_ref, w_ref, o_ref):
        o_ref[0] = _dot(d_ref[...], w_ref[0], 1, 1)

    return pl.pallas_call(
        body, name="dact_plain", grid=(ns, L // tm),
        in_specs=[pl.BlockSpec((tm, D), lambda s, m: (m, 0)), pl.BlockSpec((1, N, D), lambda s, m: (s, 0, 0))],
        out_specs=pl.BlockSpec((1, tm, N), lambda s, m: (s, m, 0)),
        out_shape=S((ns, L, N), F32), compiler_params=_params("parallel", "parallel"),
    )(dhb, w3)


def dact_swiglu(dhb, wd, a3, b3):
    L, D = dhb.shape
    ns, F, _ = wd.shape
    tm = _tile(L, 512)

    def body(d_ref, w_ref, a_ref, b_ref, da_ref, db_ref):
        ds = _dot(d_ref[...], w_ref[0], 1, 1)
        a = a_ref[0].astype(F32)
        b = b_ref[0].astype(F32)
        sg = _sigmoid(a)
        da_ref[0] = (ds * b * (sg * (1.0 + a * (1.0 - sg)))).astype(BF16)
        db_ref[0] = (ds * (a * sg)).astype(BF16)

    t_spec = pl.BlockSpec((1, tm, F), lambda s, m: (s, m, 0))
    return pl.pallas_call(
        body, name="dact_swiglu", grid=(ns, L // tm),
        in_specs=[pl.BlockSpec((tm, D), lambda s, m: (m, 0)), pl.BlockSpec((1, F, D), lambda s, m: (s, 0, 0)), t_spec, t_spec],
        out_specs=[t_spec, t_spec], out_shape=[S((ns, L, F), BF16)] * 2,
        compiler_params=_params("parallel", "parallel"),
    )(dhb, wd, a3, b3)


def conv_bwd(z, conv_w, conv_b, gnorm, dyn):
    _, L, C = z.shape
    tm = _tile(L, 256)
    H = CONV_HALO
    T = tm + 2 * H

    def body(zb_ref, zbp_ref, zbn_ref, zc_ref, zcp_ref, zcn_ref, zv_ref, zvp_ref, zvn_ref, d_ref, dp_ref, dn_ref,
             w_ref, b_ref, g_ref, dz_ref, dw_ref, db_ref, dg_ref):
        m = pl.program_id(0)
        cat = lambda p, c, n: jnp.concatenate([p[0], c[0], n[0]], axis=0)
        zb, zc, zv, d = cat(zbp_ref, zb_ref, zbn_ref), cat(zcp_ref, zc_ref, zcn_ref), cat(zvp_ref, zv_ref, zvn_ref), cat(dp_ref, d_ref, dn_ref)
        grow = m * tm - H + lax.broadcasted_iota(jnp.int32, (T, C), 0)
        valid, v, v1, v2, cb, ya = _conv_core(zb, zc, zv, w_ref, b_ref[...], grow, L)
        dya, _ = _rms_bwd(ya, g_ref[...], d)
        dc = jnp.where(valid, dya * zb, 0.0)
        dv = w_ref[2:3, :] * dc + w_ref[1:2, :] * pltpu.roll(dc, T - 1, 0) + w_ref[0:1, :] * pltpu.roll(dc, T - 2, 0)
        dz_ref[0] = (dya * cb)[H:H + tm, :].astype(BF16)
        dz_ref[1] = (dv * zv)[H:H + tm, :].astype(BF16)
        dz_ref[2] = (dv * zc)[H:H + tm, :].astype(BF16)
        rs = lambda x: jnp.sum(x[H:H + tm, :], axis=0, keepdims=True)
        yh = ya * _rstd(ya)
        dw = jnp.concatenate([rs(dc * v2), rs(dc * v1), rs(dc * v)], axis=0)
        dbias, dg = rs(dc), rs(d * yh)

        @pl.when(m == 0)
        def _():
            dw_ref[...] = dw
            db_ref[...] = dbias
            dg_ref[...] = dg

        @pl.when(m > 0)
        def _():
            dw_ref[...] += dw
            db_ref[...] += dbias
            dg_ref[...] += dg

    row = lambda r: pl.BlockSpec((r, C), lambda m: (0, 0))
    specs = [*_conv_specs(L, tm, C, 0), *_conv_specs(L, tm, C, 1), *_conv_specs(L, tm, C, 2), *_conv_specs(L, tm, C, 0)]
    return pl.pallas_call(
        body, name="conv_bwd", grid=(L // tm,),
        in_specs=specs + [row(3), row(1), row(1)],
        out_specs=[pl.BlockSpec((3, tm, C), lambda m: (0, m, 0)), row(3), row(1), row(1)],
        out_shape=[S((3, L, C), BF16), S((3, C), F32), S((1, C), F32), S((1, C), F32)],
        compiler_params=_params("arbitrary"),
    )(z, z, z, z, z, z, z, z, z, dyn, dyn, dyn, conv_w, conv_b.reshape(1, C), gnorm.reshape(1, C))


def glu_bwd(y, w, b, gnorm, dn):
    L, C = y.shape
    tm = _tile(L, 256)

    def body(y_ref, w_ref, b_ref, g_ref, d_ref, dy_ref, dpre_ref, zg_ref, db_ref, dg_ref):
        m = pl.program_id(0)
        yv = y_ref[...]
        zg, t = _gelu(yv)
        zgb = zg.astype(BF16)
        sg = _sigmoid(_dot(zgb, w_ref[...], 1, 0) + b_ref[...])
        out = zg * sg
        dout, dg = _rms_bwd(out, g_ref[...], d_ref[...])
        dpre = dout * zg * sg * (1.0 - sg)
        dpre_b = dpre.astype(BF16)
        dzg = dout * sg + _dot(dpre_b, w_ref[...], 1, 1)
        dt = (1.0 - t * t) * _GELU_C * (1.0 + 3.0 * 0.044715 * yv * yv)
        dy_ref[...] = dzg * (0.5 * (1.0 + t) + 0.5 * yv * dt)
        dpre_ref[...] = dpre_b
        zg_ref[...] = zgb
        dbias = jnp.sum(dpre, axis=0, keepdims=True)

        @pl.when(m == 0)
        def _():
            db_ref[...] = dbias
            dg_ref[...] = dg

        @pl.when(m > 0)
        def _():
            db_ref[...] += dbias
            dg_ref[...] += dg

    tile = pl.BlockSpec((tm, C), lambda m: (m, 0))
    row = pl.BlockSpec((1, C), lambda m: (0, 0))
    return pl.pallas_call(
        body, name="glu_bwd", grid=(L // tm,),
        in_specs=[tile, pl.BlockSpec((C, C), lambda m: (0, 0)), row, row, tile],
        out_specs=[tile, tile, tile, row, row],
        out_shape=[S((L, C), F32), S((L, C), BF16), S((L, C), BF16), S((1, C), F32), S((1, C), F32)],
        compiler_params=_params("arbitrary"),
    )(y, w, b.reshape(1, C), gnorm.reshape(1, C), dn)


def _scan_bwd(gr_ref, gi_ref, hr_ref, hi_ref, lr, li, n_steps):
    W = gr_ref.shape[1]
    zero = jnp.zeros((N_SEG, W), F32)
    lic = -li

    def local(i, c):
        r = pl.multiple_of((n_steps - 1 - i) * N_SEG, N_SEG)
        pr, pi = _cmul(lr, lic, c[0], c[1])
        nr = pr + gr_ref[pl.ds(r, N_SEG), :]
        ni = pi + gi_ref[pl.ds(r, N_SEG), :]
        gr_ref[pl.ds(r, N_SEG), :] = nr
        gi_ref[pl.ds(r, N_SEG), :] = ni
        return nr, ni

    fr, fi = lax.fori_loop(0, n_steps, local, (zero, zero))
    qr, qi = _cpow(lr, lic, n_steps)
    row = lax.broadcasted_iota(jnp.int32, (N_SEG, W), 0)
    cr, ci = zero, zero
    for seg in range(N_SEG - 2, -1, -1):
        tr, ti = _cmul(qr, qi, cr, ci)
        sr = pltpu.roll(fr + tr, N_SEG - 1, 0)
        si = pltpu.roll(fi + ti, N_SEG - 1, 0)
        cr = jnp.where(row == seg, sr, cr)
        ci = jnp.where(row == seg, si, ci)

    def fix(i, c):
        pwr, pwi, ar, ai = c
        t = n_steps - 1 - i
        r = pl.multiple_of(t * N_SEG, N_SEG)
        pwr, pwi = _cmul(lr, lic, pwr, pwi)
        xr, xi = _cmul(pwr, pwi, cr, ci)
        g_r = gr_ref[pl.ds(r, N_SEG), :] + xr
        g_i = gi_ref[pl.ds(r, N_SEG), :] + xi
        gr_ref[pl.ds(r, N_SEG), :] = g_r
        gi_ref[pl.ds(r, N_SEG), :] = g_i
        rp = pl.multiple_of(jnp.maximum(t - 1, 0) * N_SEG, N_SEG)
        hpr = hr_ref[pl.ds(rp, N_SEG), :]
        hpi = hi_ref[pl.ds(rp, N_SEG), :]
        live = t > 0
        ar = ar + jnp.where(live, hpr * g_r + hpi * g_i, 0.0)
        ai = ai + jnp.where(live, hpr * g_i - hpi * g_r, 0.0)
        return pwr, pwi, ar, ai

    _, _, ar, ai = lax.fori_loop(0, n_steps, fix, (jnp.ones((N_SEG, W), F32), zero, zero, zero))
    last = pl.ds((n_steps - 1) * N_SEG, N_SEG)
    hpr = jnp.where(row == 0, 0.0, pltpu.roll(hr_ref[last, :], 1, 0))
    hpi = jnp.where(row == 0, 0.0, pltpu.roll(hi_ref[last, :], 1, 0))
    g_r, g_i = gr_ref[pl.ds(0, N_SEG), :], gi_ref[pl.ds(0, N_SEG), :]
    ar = ar + hpr * g_r + hpi * g_i
    ai = ai + hpr * g_i - hpi * g_r
    return jnp.sum(ar, axis=0, keepdims=True), jnp.sum(ai, axis=0, keepdims=True)


def ssm_bwd(us, dy, lam, bre, bim, cre, cim, dvec):
    L = us.shape[0]
    n_steps = L // N_SEG
    sp = _ssm_specs(L)

    def body(u_ref, dy_ref, lam_ref, bre_ref, bim_ref, cre_ref, cim_ref, d_ref,
             du_ref, dlam_ref, dbre_ref, dbim_ref, dcre_ref, dcim_ref, dd_ref, hr, hi, gr, gi):
        u = u_ref[...]
        ub = u.astype(BF16)
        dyv = dy_ref[...]
        dyb = dyv.astype(BF16)
        hr[...] = _dot(ub, bre_ref[0], 1, 0)
        hi[...] = _dot(ub, bim_ref[0], 1, 0)
        lr = jnp.broadcast_to(lam_ref[0:1, :], (N_SEG, 512))
        li = jnp.broadcast_to(lam_ref[1:2, :], (N_SEG, 512))
        _scan_fwd(hr, hi, lr, li, n_steps)
        dcre_ref[0] = _dot(hr[...].astype(BF16), dyb, 0, 0)
        dcim_ref[0] = -_dot(hi[...].astype(BF16), dyb, 0, 0)
        gr[...] = _dot(dyb, cre_ref[0], 1, 1)
        gi[...] = -_dot(dyb, cim_ref[0], 1, 1)
        dlr, dli = _scan_bwd(gr, gi, hr, hi, lr, li, n_steps)
        dlam_ref[...] = jnp.concatenate([dlr, dli], axis=0)
        grb, gib = gr[...].astype(BF16), gi[...].astype(BF16)
        du_ref[...] = _dot(grb, bre_ref[0], 1, 1) + _dot(gib, bim_ref[0], 1, 1) + d_ref[...] * dyv
        dbre_ref[0] = _dot(ub, grb, 0, 0)
        dbim_ref[0] = _dot(ub, gib, 0, 0)
        dd_ref[...] = jnp.sum(dyv * u, axis=0, keepdims=True)

    big = pltpu.VMEM((L, 512), F32)
    return pl.pallas_call(
        body, name="ssm_bwd", grid=(4,),
        in_specs=[sp["u"], sp["u"], sp["lam"], sp["bmat"], sp["bmat"], sp["cmat"], sp["cmat"], sp["d"]],
        out_specs=[sp["u"], sp["lam"], sp["bmat"], sp["bmat"], sp["cmat"], sp["cmat"], sp["d"]],
        out_shape=[S((L, 512), F32), S((2, 2048), F32), S((4, 128, 512), F32), S((4, 128, 512), F32),
                   S((4, 512, 128), F32), S((4, 512, 128), F32), S((1, 512), F32)],
        scratch_shapes=[big, big, big, big], compiler_params=_params("parallel"),
    )(us, dy, lam, bre, bim, cre, cim, dvec)


def _discretize(ar, ai, log_dt, br, bi):
    dt = jnp.exp(log_dt)
    mag = jnp.exp(ar * dt)
    ph = ai * dt
    lr, li = mag * jnp.cos(ph), mag * jnp.sin(ph)
    nr, ni = lr - 1.0, li
    den = ar * ar + ai * ai
    fr = (nr * ar + ni * ai) / den
    fi = (ni * ar - nr * ai) / den
    return lr, li, fr[..., None] * br - fi[..., None] * bi, fr[..., None] * bi + fi[..., None] * br


def ssm_prep(ar, ai, log_dt, br, bi):
    G, P, H = br.shape

    def body(ar_ref, ai_ref, dt_ref, br_ref, bi_ref, lr_ref, li_ref, bbr_ref, bbi_ref):
        lr_ref[...], li_ref[...], bbr_ref[...], bbi_ref[...] = _discretize(
            ar_ref[...], ai_ref[...], dt_ref[...], br_ref[...], bi_ref[...])

    return pl.pallas_call(
        body, name="ssm_prep",
        out_shape=[S((G, P), F32), S((G, P), F32), S((G, P, H), F32), S((G, P, H), F32)],
    )(ar, ai, log_dt.reshape(G, 1), br, bi)


def ssm_prep_bwd(ar, ai, log_dt, br, bi, dlr, dli, dbbr, dbbi):
    G, P, H = br.shape

    def body(ar_ref, ai_ref, dt_ref, br_ref, bi_ref, dlr_ref, dli_ref, dbbr_ref, dbbi_ref,
             dar_ref, dai_ref, ddt_ref, dbr_ref, dbi_ref):
        _, vjp = jax.vjp(_discretize, ar_ref[...], ai_ref[...], dt_ref[...], br_ref[...], bi_ref[...])
        dar_ref[...], dai_ref[...], ddt_ref[...], dbr_ref[...], dbi_ref[...] = vjp(
            (dlr_ref[...], dli_ref[...], dbbr_ref[...], dbbi_ref[...]))

    return pl.pallas_call(
        body, name="ssm_prep_bwd",
        out_shape=[S((G, P), F32), S((G, P), F32), S((G, 1), F32), S((G, P, H), F32), S((G, P, H), F32)],
    )(ar, ai, log_dt.reshape(G, 1), br, bi, dlr, dli, dbbr, dbbi)


def _block_diag(x):
    j, n, R, C = x.shape
    eye = jnp.eye(n, dtype=x.dtype)
    return (x[:, :, :, None, :] * eye[None, :, None, :, None]).reshape(j, n * R, n * C)


def _block_diag_take(x, R, C):
    j = x.shape[0]
    n = x.shape[1] // R
    x5 = x.reshape(j, n, R, n, C)
    return jnp.stack([x5[:, i, :, i, :] for i in range(n)], axis=1)


def _to_segments(x):
    L, C = x.shape
    return x.reshape(N_SEG, L // N_SEG, C).transpose(1, 0, 2).reshape(L, C)


def _from_segments(x):
    L, C = x.shape
    return x.reshape(L // N_SEG, N_SEG, C).transpose(1, 0, 2).reshape(L, C)


BIG = ("ffn1_w_gate", "ffn1_w_up", "ffn1_w_down", "w_in", "glu_w", "w_out",
       "ffn2_w_gate", "ffn2_w_up", "ffn2_w_down", "ple_w_gate", "ple_w_proj")
SMALL = ("ffn1_norm", "mix_norm", "conv_w", "conv_b", "ssm_A_re", "ssm_A_im", "ssm_B_re", "ssm_B_im", "ssm_C_re", "ssm_C_im",
         "ssm_D", "ssm_log_dt", "glu_b", "conv_out_norm", "ssm_out_norm", "ffn2_norm", "ple_norm")


def _ssm_mats(w):
    G, P, H = w["ssm_B_re"].shape
    lr, li, bbr, bbi = ssm_prep(w["ssm_A_re"], w["ssm_A_im"], w["ssm_log_dt"], w["ssm_B_re"], w["ssm_B_im"])
    lam = jnp.stack([lr.reshape(G * P), li.reshape(G * P)])
    bmat = lambda bb: _block_diag(bb.reshape(4, G // 4, P, H).transpose(0, 1, 3, 2)).astype(BF16)
    cmat = lambda c: _block_diag(c.reshape(4, G // 4, H, P).transpose(0, 1, 3, 2)).astype(BF16)
    return lam, bmat(bbr), bmat(bbi), cmat(w["ssm_C_re"]), cmat(w["ssm_C_im"]), w["ssm_D"].reshape(1, G * H)


def layer_fwd(h0, pb, w):
    L, D = h0.shape
    u1 = rmsnorm_fwd(h0, w["ffn1_norm"])
    a1, b1, s1 = ffn_up(u1, w["ffn1_w_gate"], w["ffn1_w_up"])
    h1, u2 = mm_shard_k(s1, w["ffn1_w_down"], h0, 0.5, w["mix_norm"])
    z = mm_shard_n(u2, w["w_in"], F32)
    ya_n = conv_fwd(z, w["conv_w"], w["conv_b"], w["conv_out_norm"])
    us = _to_segments(z[3])
    mats = _ssm_mats(w)
    y = ssm_fwd(us, *mats)
    ys_n = glu_fwd(y, w["glu_w"], w["glu_b"], w["ssm_out_norm"])
    ycat = jnp.stack([ya_n, _from_segments(ys_n)])
    h2, u3 = mm_shard_k(ycat, w["w_out"], h1, 1.0, w["ffn2_norm"])
    a2, b2, s2 = ffn_up(u3, w["ffn2_w_gate"], w["ffn2_w_up"])
    h3, un = mm_shard_k(s2, w["ffn2_w_down"], h2, 0.5, w["ple_norm"])
    h4 = ple_fwd(un, pb, w["ple_w_gate"], w["ple_w_proj"], h3)
    saved = dict(h0=h0, u1=u1, a1=a1, b1=b1, s1=s1, h1=h1, u2=u2, z=z, us=us, mats=mats, y=y, ycat=ycat,
                 h2=h2, u3=u3, a2=a2, b2=b2, s2=s2, h3=h3, un=un)
    return h4, saved


def _ffn_bwd(dh, dhb, h_in, u, a, b, s, wg, wu, wd, gnorm, cast_scale):
    da, db = dact_swiglu(dhb, wd, a, b)
    g_wd = wgrad(s, dhb)
    g_wg = wgrad(da, u)
    g_wu = wgrad(db, u)
    dh_in, dhb_in, g_norm = dx_rms([(da, wg, True), (db, wu, True)], h_in, gnorm, dh, cast_scale)
    return dh_in, dhb_in, g_wg, g_wu, g_wd, g_norm


def layer_bwd(dh, pb, w, sv, token):
    L, D = dh.shape
    G, P, H = w["ssm_B_re"].shape
    dpre, dpp3 = ple_bwd(sv["un"], pb, w["ple_w_gate"], w["ple_w_proj"], dh, token)
    g_wpg = wgrad(sv["un"], dpre).reshape(N_SHARD, D // N_SHARD, D)
    g_wpp = wgrad(pb, dpp3)
    dh3, dhb3, g_nple = dx_rms([(dpre[None], w["ple_w_gate"][None], False)], sv["h3"], w["ple_norm"], dh, 0.5)
    dh2, dhb, g_wg2, g_wu2, g_wd2, g_nffn2 = _ffn_bwd(dh3, dhb3, sv["h2"], sv["u3"], sv["a2"], sv["b2"], sv["s2"],
                                                      w["ffn2_w_gate"], w["ffn2_w_up"], w["ffn2_w_down"], w["ffn2_norm"], 1.0)
    dyn = dact_plain(dhb, w["w_out"])
    g_wout = wgrad(sv["ycat"], dhb).reshape(N_SHARD, -1, D)
    dz_abc, g_convw, g_convb, g_nconv = conv_bwd(sv["z"], w["conv_w"], w["conv_b"], w["conv_out_norm"], dyn)
    dy, dpre_g, zg, g_glub, g_nssm = glu_bwd(sv["y"], w["glu_w"], w["glu_b"], w["ssm_out_norm"], _to_segments(dyn[1]))
    C = zg.shape[1]
    g_gluw = wgrad(zg, dpre_g).reshape(N_SHARD, C // N_SHARD, C)
    dus, dlam, dbre, dbim, dcre, dcim, dd = ssm_bwd(sv["us"], dy, *sv["mats"])
    take_b = lambda m: _block_diag_take(m, H, P).transpose(0, 1, 3, 2).reshape(G, P, H)
    take_c = lambda m: _block_diag_take(m, P, H).transpose(0, 1, 3, 2).reshape(G, H, P)
    g_ar, g_ai, g_dt, g_br, g_bi = ssm_prep_bwd(
        w["ssm_A_re"], w["ssm_A_im"], w["ssm_log_dt"], w["ssm_B_re"], w["ssm_B_im"],
        dlam[0].reshape(G, P), dlam[1].reshape(G, P), take_b(dbre), take_b(dbim))
    dz3 = jnp.concatenate([dz_abc, _from_segments(dus).astype(BF16)[None]], axis=0)
    g_win = wgrad(sv["u2"], dz3)
    dh1, dhb1, g_nmix = dx_rms([(dz3, w["w_in"], False)], sv["h1"], w["mix_norm"], dh2, 0.5)
    dh0, _, g_wg1, g_wu1, g_wd1, g_nffn1 = _ffn_bwd(dh1, dhb1, sv["h0"], sv["u1"], sv["a1"], sv["b1"], sv["s1"],
                                                    w["ffn1_w_gate"], w["ffn1_w_up"], w["ffn1_w_down"], w["ffn1_norm"], 1.0)
    big = [g_wg1, g_wu1, g_wd1, g_win, g_gluw, g_wout, g_wg2, g_wu2, g_wd2, g_wpg, g_wpp]
    small = dict(ffn1_norm=g_nffn1, mix_norm=g_nmix, conv_w=g_convw, conv_b=g_convb, ssm_A_re=g_ar, ssm_A_im=g_ai,
                 ssm_B_re=g_br, ssm_B_im=g_bi, ssm_C_re=take_c(dcre), ssm_C_im=take_c(dcim), ssm_D=dd,
                 ssm_log_dt=g_dt, glu_b=g_glub, conv_out_norm=g_nconv, ssm_out_norm=g_nssm, ffn2_norm=g_nffn2,
                 ple_norm=g_nple)
    return dh0, big, small


def local_step(x, p, target, final_norm, weights_of, on_grads):
    depth = p.shape[0]
    h = x
    layers, saved, pbs = [], [], []
    for i in range(depth):
        w = weights_of(i, h)
        pb = p[i].astype(BF16)
        h, sv = layer_fwd(h, pb, w)
        layers.append(w)
        saved.append(sv)
        pbs.append(pb)
    loss_part, dh, g_final = loss_head(h, final_norm, target)
    smalls = [None] * depth
    token = jnp.zeros((8, 128), F32)
    for i in reversed(range(depth)):
        dh, big, smalls[i] = layer_bwd(dh, pbs[i], layers[i], saved[i], token)
        token = on_grads(i, big, dh)
    return loss_part, dh, smalls, g_final


ROW_TILE_MAX = 512


def _row_tile(rows):
    for t in range(ROW_TILE_MAX, 0, -16):
        if rows % t == 0:
            return t
    return rows


def elementwise(fn, ins, out_dtypes, name):
    rows, cols = ins[0].shape
    tr = _row_tile(rows)
    n_in = len(ins)

    def body(*refs):
        outs = fn(*[r[...] for r in refs[:n_in]])
        for o_ref, o in zip(refs[n_in:], outs):
            o_ref[...] = o.astype(o_ref.dtype)

    spec = pl.BlockSpec((tr, cols), lambda i: (i, 0))
    return pl.pallas_call(
        body, name=name, grid=(rows // tr,), in_specs=[spec] * n_in, out_specs=[spec] * len(out_dtypes),
        out_shape=[S((rows, cols), d) for d in out_dtypes], compiler_params=_params("parallel"),
    )(*ins)


def _adamw(w, g, m, v):
    m = ADAM_B1 * m + (1.0 - ADAM_B1) * g
    v = ADAM_B2 * v + (1.0 - ADAM_B2) * (g * g)
    m_hat = m / (1.0 - ADAM_B1 ** ADAM_STEP)
    v_hat = v / (1.0 - ADAM_B2 ** ADAM_STEP)
    delta = -ADAM_LR * (m_hat / (jnp.sqrt(v_hat) + ADAM_EPS) + ADAM_WD * w)
    return delta, m, v


ANY = pl.BlockSpec(memory_space=pl.ANY)


def _mesh_pos():
    return lax.axis_index("x"), lax.axis_index("y"), lax.axis_index("c")


def _other_chips(x, y):
    return [(1 - x, y), (x, 1 - y), (1 - x, 1 - y)]


def _remote(src, dst, send_sem, recv_sem, device):
    return pltpu.make_async_remote_copy(src_ref=src, dst_ref=dst, send_sem=send_sem, recv_sem=recv_sem,
                                        device_id=device, device_id_type=MESH)


def gather_weights(ws):
    n = len(ws)

    def body(*refs):
        outs = refs[n:2 * n]
        send_sems, recv_sems = refs[2 * n:]
        x, y, c = _mesh_pos()
        me_s = 2 * x + y
        sibling = (x, y, 1 - c)
        chips = _other_chips(x, y)
        n_half = outs[0].shape[0] // 2
        mine, other = pl.ds(c * n_half, n_half), pl.ds((1 - c) * n_half, n_half)
        sent = []
        for t in range(n):
            for j, (cx, cy) in enumerate(chips):
                blk = outs[t].at[mine, me_s]
                cp = _remote(blk, blk, send_sems.at[t, j], recv_sems.at[t, j], (cx, cy, c))
                cp.start()
                sent.append(cp)
        for j, (cx, cy) in enumerate(chips):
            for t in range(n):
                blk = outs[t].at[mine, 2 * cx + cy]
                _remote(blk, blk, send_sems.at[t, j], recv_sems.at[t, j], (cx, cy, c)).wait_recv()
                cp = _remote(blk, blk, send_sems.at[t, 3 + j], recv_sems.at[t, 3 + j], sibling)
                cp.start()
                sent.append(cp)
        for j, (cx, cy) in enumerate(chips):
            for t in range(n):
                blk = outs[t].at[other, 2 * cx + cy]
                _remote(blk, blk, send_sems.at[t, 3 + j], recv_sems.at[t, 3 + j], sibling).wait_recv()
        for cp in sent:
            cp.wait_send()

    return pl.pallas_call(
        body, name="gather_weights", in_specs=[ANY] * n, out_specs=[ANY] * n,
        out_shape=[S(w.shape, w.dtype) for w in ws], input_output_aliases={t: t for t in range(n)},
        scratch_shapes=[pltpu.SemaphoreType.DMA((n, 6)), pltpu.SemaphoreType.DMA((n, 6))],
    )(*ws)


HBM = pl.BlockSpec(memory_space=pltpu.HBM)
SEM = pl.BlockSpec(memory_space=pltpu.SEMAPHORE)
VMEM_WHOLE = pl.BlockSpec(memory_space=pltpu.VMEM)
SPLIT_COPY = pltpu.CompilerParams(has_side_effects=pltpu.SideEffectType.DATAFLOW_SIDE_EFFECTING)


def _hbm(x):
    return pltpu.with_memory_space_constraint(x, pltpu.HBM)


def _half_rows(ref, c):
    r2 = ref.shape[1] // 2
    return pl.ds(pl.multiple_of(c * r2, 8), r2)


def gather_start(bufs, layer, after):
    n, k = len(bufs), len(after)

    def body(*refs):
        ins, send_sems, recv_sems, token = refs[:n], refs[n + k], refs[n + k + 1], refs[2 * n + k + 2]
        x, y, c = _mesh_pos()
        me_s = 2 * x + y
        for t in range(n):
            blk = ins[t].at[me_s, _half_rows(ins[t], c)]
            for j, (cx, cy) in enumerate(_other_chips(x, y)):
                _remote(blk, blk, send_sems.at[3 * t + j], recv_sems.at[3 * t + j], (cx, cy, c)).start()
        token[...] = jnp.zeros_like(token)

    outs = pl.pallas_call(
        body, name=f"gather_start_{layer}", in_specs=[HBM] * n + [ANY] * k, out_specs=[SEM, SEM] + [HBM] * n + [VMEM_WHOLE],
        out_shape=[pltpu.SemaphoreType.DMA((3 * n,)), pltpu.SemaphoreType.DMA((3 * n,))]
        + [pltpu.HBM(b.shape, b.dtype) for b in bufs] + [S((8, 128), F32)],
        input_output_aliases={t: t + 2 for t in range(n)}, compiler_params=SPLIT_COPY,
    )(*[_hbm(b) for b in bufs], *after)
    return outs[0], outs[1], list(outs[2:2 + n]), outs[2 + n]


def gather_wait(bufs, send_sems, recv_sems, after, layer):
    n, n_after = len(bufs), len(after)

    def body(*refs):
        ins, send_ref, recv_ref = refs[:n], refs[n], refs[n + 1]
        x, y, c = _mesh_pos()
        me_s = 2 * x + y
        for t in range(n):
            rows = _half_rows(ins[t], c)
            for j, (cx, cy) in enumerate(_other_chips(x, y)):
                cp = _remote(ins[t].at[me_s, rows], ins[t].at[2 * cx + cy, rows], send_ref.at[3 * t + j], recv_ref.at[3 * t + j],
                             (cx, cy, c))
                cp.wait_send()
                cp.wait_recv()

    outs = pl.pallas_call(
        body, name=f"gather_wait_{layer}", in_specs=[HBM] * n + [SEM, SEM] + [ANY] * n_after, out_specs=[HBM] * n,
        out_shape=[pltpu.HBM(b.shape, b.dtype) for b in bufs],
        input_output_aliases={t: t for t in range(n)}, compiler_params=SPLIT_COPY,
    )(*bufs, send_sems, recv_sems, *after)
    return list(outs)


def gather_forward(bufs):
    n = len(bufs)

    def body(*refs):
        outs = refs[n:2 * n]
        send_sems, recv_sems = refs[2 * n:]
        x, y, c = _mesh_pos()
        copies = []
        for t in range(n):
            rows = _half_rows(outs[t], c)
            for j, (cx, cy) in enumerate(_other_chips(x, y)):
                blk = outs[t].at[2 * cx + cy, rows]
                cp = _remote(blk, blk, send_sems.at[t, j], recv_sems.at[t, j], (x, y, 1 - c))
                cp.start()
                copies.append(cp)
        for cp in copies:
            cp.wait()

    return pl.pallas_call(
        body, name="gather_forward", in_specs=[ANY] * n, out_specs=[ANY] * n,
        out_shape=[S(b.shape, b.dtype) for b in bufs], input_output_aliases={t: t for t in range(n)},
        scratch_shapes=[pltpu.SemaphoreType.DMA((n, 3)), pltpu.SemaphoreType.DMA((n, 3))],
    )(*bufs)


def chips_start(sums, layer):
    n = len(sums)
    lands = [lax.empty((3,) + s.shape[1:], s.dtype) for s in sums]

    def body(*refs):
        a, land, send_sems, recv_sems, token = refs[:n], refs[n:2 * n], refs[2 * n], refs[2 * n + 1], refs[4 * n + 2]
        x, y, c = _mesh_pos()
        for t in range(n):
            for j, (cx, cy) in enumerate(_other_chips(x, y)):
                _remote(a[t].at[2 * cx + cy], land[t].at[j], send_sems.at[3 * t + j], recv_sems.at[3 * t + j], (cx, cy, c)).start()
        token[...] = jnp.zeros_like(token)

    outs = pl.pallas_call(
        body, name=f"chips_start_{layer}", in_specs=[HBM] * (2 * n), out_specs=[SEM, SEM] + [HBM] * (2 * n) + [VMEM_WHOLE],
        out_shape=[pltpu.SemaphoreType.DMA((3 * n,)), pltpu.SemaphoreType.DMA((3 * n,))]
        + [pltpu.HBM(b.shape, b.dtype) for b in sums + lands] + [S((8, 128), F32)],
        input_output_aliases={t: t + 2 for t in range(2 * n)}, compiler_params=SPLIT_COPY,
    )(*[_hbm(b) for b in sums + lands])
    return outs[0], outs[1], list(outs[2:2 + n]), list(outs[2 + n:2 + 2 * n]), outs[2 + 2 * n]


def chips_wait(sums, lands, send_sems, recv_sems, after, layer):
    n, n_after = len(sums), len(after)

    def body(*refs):
        a, land, send_ref, recv_ref = refs[:n], refs[n:2 * n], refs[2 * n], refs[2 * n + 1]
        x, y, c = _mesh_pos()
        for t in range(n):
            for j, (cx, cy) in enumerate(_other_chips(x, y)):
                cp = _remote(a[t].at[2 * cx + cy], land[t].at[j], send_ref.at[3 * t + j], recv_ref.at[3 * t + j], (cx, cy, c))
                cp.wait_send()
                cp.wait_recv()

    outs = pl.pallas_call(
        body, name=f"chips_wait_{layer}", in_specs=[HBM] * (2 * n) + [SEM, SEM] + [ANY] * n_after, out_specs=[HBM] * (2 * n),
        out_shape=[pltpu.HBM(b.shape, b.dtype) for b in sums + lands],
        input_output_aliases={t: t for t in range(2 * n)}, compiler_params=SPLIT_COPY,
    )(*sums, *lands, send_sems, recv_sems, *after)
    return list(outs[n:])


def cast_place_layer(w, layer, pos, dtype):
    _, r, c = w.shape
    tr = _row_tile(r)

    def body(pos_ref, w_ref, o_ref):
        o_ref[0] = w_ref[0].astype(dtype)

    return pl.pallas_call(
        body, name="cast_place_layer",
        grid_spec=pltpu.PrefetchScalarGridSpec(
            num_scalar_prefetch=1, grid=(r // tr,),
            in_specs=[pl.BlockSpec((1, tr, c), lambda i, pos: (layer, i, 0))],
            out_specs=pl.BlockSpec((1, tr, c), lambda i, pos: (pos[1], i, 0))),
        out_shape=S((N_SHARD, r, c), dtype), compiler_params=_params("parallel"),
    )(pos, w)


def cast_place(w, pos, dtype):
    layers, r, c = w.shape
    tr = _row_tile(r)

    def body(pos_ref, w_ref, o_ref):
        o_ref[0, 0] = w_ref[0].astype(dtype)

    return pl.pallas_call(
        body, name="cast_place",
        grid_spec=pltpu.PrefetchScalarGridSpec(
            num_scalar_prefetch=1, grid=(layers, r // tr),
            in_specs=[pl.BlockSpec((1, tr, c), lambda l, i, pos: (l, i, 0))],
            out_specs=pl.BlockSpec((1, 1, tr, c), lambda l, i, pos: (l, pos[1], i, 0))),
        out_shape=S((layers, N_SHARD, r, c), dtype), compiler_params=_params("parallel", "parallel"),
    )(pos, w)


def _pair_copy(g_ref, got_ref, send_sem, recv_sem):
    x, y, c = _mesh_pos()
    r2 = g_ref.shape[1] // 2
    give = pl.ds(pl.multiple_of((1 - c) * r2, 8), r2)
    return _remote(g_ref.at[:, give], got_ref, send_sem, recv_sem, (x, y, 1 - c))


def reduce_pair(gs, after):
    n, k = len(gs), len(after)

    def body(*refs):
        ins, got = refs[:n], refs[n + k:2 * n + k]
        send_sems, recv_sems = refs[2 * n + k:]
        copies = [_pair_copy(ins[t], got[t], send_sems.at[t], recv_sems.at[t]) for t in range(n)]
        for cp in copies:
            cp.start()
        for cp in copies:
            cp.wait()

    return pl.pallas_call(
        body, name="reduce_pair", in_specs=[ANY] * (n + k), out_specs=[ANY] * n,
        out_shape=[S((g.shape[0], g.shape[1] // 2, g.shape[2]), g.dtype) for g in gs],
        scratch_shapes=[pltpu.SemaphoreType.DMA((n,)), pltpu.SemaphoreType.DMA((n,))],
    )(*gs, *after)


def pair_sum(g, got, pos):
    ns, r2, c = got.shape
    tr = _row_tile(r2)
    n_i = r2 // tr

    def body(pos_ref, g_ref, got_ref, sum_ref, own_ref):
        s = pl.program_id(1)
        v = g_ref[0].astype(F32) + got_ref[0].astype(F32)
        sum_ref[0] = v.astype(BF16)

        @pl.when(s == pos_ref[1])
        def _():
            own_ref[...] = v

    return pl.pallas_call(
        body, name="pair_sum",
        grid_spec=pltpu.PrefetchScalarGridSpec(
            num_scalar_prefetch=1, grid=(n_i, ns),
            in_specs=[pl.BlockSpec((1, tr, c), lambda i, s, pos: (s, pos[0] * n_i + i, 0)),
                      pl.BlockSpec((1, tr, c), lambda i, s, pos: (s, i, 0))],
            out_specs=[pl.BlockSpec((1, tr, c), lambda i, s, pos: (s, i, 0)), pl.BlockSpec((tr, c), lambda i, s, pos: (i, 0))]),
        out_shape=[S((ns, r2, c), BF16), S((r2, c), F32)], compiler_params=_params("parallel", "arbitrary"),
    )(pos, g, got)


def chip_sum(own, p2, pos):
    r2, c = own.shape
    tr = _row_tile(r2)
    n_i = r2 // tr

    def body(pos_ref, own_ref, a_ref, b_ref, c_ref, o_ref):
        o_ref[...] = own_ref[...] + a_ref[0].astype(F32) + b_ref[0].astype(F32) + c_ref[0].astype(F32)

    peer = lambda j: pl.BlockSpec((1, tr, c), lambda i, pos: (j, i, 0))
    return pl.pallas_call(
        body, name="chip_sum",
        grid_spec=pltpu.PrefetchScalarGridSpec(
            num_scalar_prefetch=1, grid=(n_i,),
            in_specs=[pl.BlockSpec((tr, c), lambda i, pos: (i, 0)), peer(0), peer(1), peer(2)],
            out_specs=pl.BlockSpec((tr, c), lambda i, pos: (pos[0] * n_i + i, 0))),
        out_shape=S((2 * r2, c), F32), compiler_params=_params("parallel"),
    )(pos, own, p2, p2, p2)


def exchange_halves(rs):
    n = len(rs)

    def body(*refs):
        outs = refs[n:2 * n]
        send_sems, recv_sems = refs[2 * n:]
        x, y, c = _mesh_pos()
        copies = []
        for t in range(n):
            r2 = outs[t].shape[0] // 2
            rows = outs[t].at[pl.ds(pl.multiple_of(c * r2, 8), r2)]
            cp = _remote(rows, rows, send_sems.at[t], recv_sems.at[t], (x, y, 1 - c))
            cp.start()
            copies.append(cp)
        for cp in copies:
            cp.wait()

    return pl.pallas_call(
        body, name="exchange_halves", in_specs=[ANY] * n, out_specs=[ANY] * n,
        out_shape=[S(r.shape, r.dtype) for r in rs], input_output_aliases={t: t for t in range(n)},
        scratch_shapes=[pltpu.SemaphoreType.DMA((n,)), pltpu.SemaphoreType.DMA((n,))],
    )(*rs)


def allreduce_small(vec):
    R = vec.shape[0]
    H = R // 2

    def body(x_ref, o_ref, pair_buf, chip_buf, send_sems, recv_sems):
        x, y, c = _mesh_pos()
        me_s = 2 * x + y
        sibling = (x, y, 1 - c)
        mine = pl.ds(pl.multiple_of(c * H, 8), H)
        give = pl.ds(pl.multiple_of((1 - c) * H, 8), H)
        cp = _remote(x_ref.at[give], pair_buf, send_sems.at[0], recv_sems.at[0], sibling)
        cp.start()
        cp.wait()
        chip_buf[me_s] = x_ref[mine, :] + pair_buf[...]
        copies = []
        for j, (cx, cy) in enumerate(_other_chips(x, y)):
            cp = _remote(chip_buf.at[me_s], chip_buf.at[me_s], send_sems.at[1 + j], recv_sems.at[1 + j], (cx, cy, c))
            cp.start()
            copies.append(cp)
        for cp in copies:
            cp.wait()
        o_ref[mine, :] = (chip_buf[0] + chip_buf[1]) + (chip_buf[2] + chip_buf[3])
        cp = _remote(o_ref.at[mine], o_ref.at[mine], send_sems.at[4], recv_sems.at[4], sibling)
        cp.start()
        cp.wait()

    vm = pl.BlockSpec(memory_space=pltpu.VMEM)
    return pl.pallas_call(
        body, name="allreduce_small", in_specs=[vm], out_specs=vm, out_shape=S((R, 128), F32),
        scratch_shapes=[pltpu.VMEM((H, 128), F32), pltpu.VMEM((N_SHARD, H, 128), F32),
                        pltpu.SemaphoreType.DMA((5,)), pltpu.SemaphoreType.DMA((5,))],
        compiler_params=pltpu.CompilerParams(vmem_limit_bytes=VMEM_LIMIT_BYTES),
    )(vec)


def adamw_layer(w, g, m, v, layer, prev):
    _, r, c = w.shape
    tr = _row_tile(r)

    def body(w_ref, g_ref, m_ref, v_ref, *rest):
        outs = rest[-4:]
        g_val = g_ref[...]
        outs[0][0] = g_val
        outs[1][0], outs[2][0], outs[3][0] = _adamw(w_ref[0], g_val, m_ref[0], v_ref[0])

    lay = pl.BlockSpec((1, tr, c), lambda i: (layer, i, 0))
    prev = list(prev) if prev else []
    return pl.pallas_call(
        body, name="adamw_layer", grid=(r // tr,),
        in_specs=[lay, pl.BlockSpec((tr, c), lambda i: (i, 0)), lay, lay] + [ANY] * len(prev),
        out_specs=[lay] * 4, out_shape=[S(w.shape, F32)] * 4,
        input_output_aliases={4 + k: k for k in range(len(prev))}, compiler_params=_params("parallel"),
    )(w, g, m, v, *prev)


def reduce_begin(gs, pos, layer, after):
    got = reduce_pair(gs, after)
    sums, own = zip(*[pair_sum(g, o, pos) for g, o in zip(gs, got)])
    send_sems, recv_sems, sums, lands, token = chips_start(list(sums), layer)
    return dict(own=own, sums=sums, lands=lands, sems=(send_sems, recv_sems), token=token, layer=layer)


def reduce_end(pending, pos, after):
    lands = chips_wait(pending["sums"], pending["lands"], *pending["sems"], after, pending["layer"])
    return exchange_halves([chip_sum(o, p, pos) for o, p in zip(pending["own"], lands)])


W_NAMES = ("ffn1_norm", "ffn1_w_gate", "ffn1_w_up", "ffn1_w_down", "mix_norm", "w_in", "conv_w", "conv_b", "ssm_A_re", "ssm_A_im",
           "ssm_B_re", "ssm_B_im", "ssm_C_re", "ssm_C_im", "ssm_D", "ssm_log_dt", "glu_w", "glu_b", "conv_out_norm", "ssm_out_norm",
           "w_out", "ffn2_norm", "ffn2_w_gate", "ffn2_w_up", "ffn2_w_down", "ple_norm", "ple_w_gate", "ple_w_proj", "final_norm")
SMALL_ALL = SMALL + ("final_norm",)
TRANSPOSED = ("ffn1_w_gate", "ffn1_w_up", "ffn2_w_gate", "ffn2_w_up")
PACK = ROW_TILE_MAX * 128


def _pack(parts):
    flat = jnp.concatenate([p.reshape(-1) for p in parts])
    pad = (-flat.shape[0]) % PACK
    return jnp.pad(flat, (0, pad)).reshape(-1, 128)


def _unpack(vec, shapes):
    flat = vec.reshape(-1)
    out, off = [], 0
    for shp in shapes:
        size = math.prod(shp)
        out.append(flat[off:off + size].reshape(shp))
        off += size
    return out


def _step(a):
    a = {k: jnp.swapaxes(v, 1, 2) if k.removeprefix("m_").removeprefix("v_") in TRANSPOSED else v for k, v in a.items()}
    x, p, target = a["x"][0], a["p"][:, 0], a["loss_target"][0]
    depth = p.shape[0]
    L, D = x.shape
    me_s = 2 * lax.axis_index("x") + lax.axis_index("y")

    pos = jnp.stack([lax.axis_index("c"), me_s]).astype(jnp.int32)
    conv_w = gather_weights([cast_place(a["conv_w"], pos, F32)])[0]
    started = [gather_start([cast_place_layer(a[n], l, pos, BF16) for n in BIG], l, [conv_w]) for l in range(depth)]

    def weights_of(l, h):
        send_sems, recv_sems, bufs, _ = started[l]
        after = [h] if l else [s[3] for s in started]
        full = gather_forward(gather_wait(bufs, send_sems, recv_sems, after, l))
        w = {n: a[n][l] for n in SMALL if n != "conv_w"}
        w.update(dict(zip(BIG, full)))
        C = w["glu_w"].shape[-1]
        w["glu_w"] = w["glu_w"].reshape(C, C)
        w["w_out"] = w["w_out"].reshape(2, -1, D)
        w["ple_w_gate"] = w["ple_w_gate"].reshape(D, D)
        w["conv_w"] = conv_w[l].transpose(1, 0, 2).reshape(3, -1)
        return w

    pending, first_layer_grads = {}, []

    def on_grads(l, big, dh):
        if l == 0:
            first_layer_grads.extend(big)
            return None
        pending[l] = reduce_begin(big, pos, l, [])
        return pending[l]["token"]

    loss_part, dx, smalls, g_final = local_step(x, p, target, a["final_norm"], weights_of, on_grads)
    small_shapes = [(depth,) + smalls[0][n].shape for n in SMALL] + [g_final.shape, (1,)]
    parts = [smalls[l][n] for n in SMALL for l in range(depth)] + [g_final, loss_part[0, 0:1]]
    summed_vec = allreduce_small(_pack(parts))
    pending[0] = reduce_begin(first_layer_grads, pos, 0, [summed_vec])
    stacked = [None] * len(BIG)
    for l in reversed(range(depth)):
        after = [pending[0]["token"]] if l else [s[3] for s in stacked]
        reduced = reduce_end(pending[l], pos, after)
        stacked = [adamw_layer(a[n], reduced[i], a["m_" + n], a["v_" + n], l, stacked[i]) for i, n in enumerate(BIG)]
    big_out = {n: [jnp.swapaxes(o, 1, 2) for o in outs] if n in TRANSPOSED else outs for n, outs in zip(BIG, stacked)}

    summed = _unpack(summed_vec, small_shapes)
    g_small = dict(zip(SMALL_ALL, summed[:-1]))
    loss = summed[-1][0]
    n_conv = a["conv_w"].shape[-1]
    g_small["conv_w"] = lax.dynamic_slice_in_dim(g_small["conv_w"], me_s * n_conv, n_conv, axis=2)
    g_small = {n: g_small[n].reshape(a[n].shape) for n in SMALL_ALL}
    packed = [_pack([src[n] for n in SMALL_ALL]) for src in
              ({n: a[n] for n in SMALL_ALL}, g_small, {n: a["m_" + n] for n in SMALL_ALL}, {n: a["v_" + n] for n in SMALL_ALL})]
    shapes = [a[n].shape for n in SMALL_ALL]
    d_s, m_s, v_s = [dict(zip(SMALL_ALL, _unpack(o, shapes))) for o in elementwise(_adamw, packed, [F32, F32, F32], "adamw_small")]

    outs = {n: big_out[n] if n in big_out else (g_small[n], d_s[n], m_s[n], v_s[n]) for n in W_NAMES}
    return (loss, dx[None], *[outs[n][0] for n in W_NAMES], *[outs[n][1] for n in W_NAMES],
            *[outs[n][2] for n in W_NAMES], *[outs[n][3] for n in W_NAMES])


def kernel(x, p, ffn1_norm, ffn1_w_gate, ffn1_w_up, ffn1_w_down, mix_norm, w_in, conv_w, conv_b, ssm_A_re, ssm_A_im, ssm_B_re, ssm_B_im, ssm_C_re, ssm_C_im, ssm_D, ssm_log_dt, glu_w, glu_b, conv_out_norm, ssm_out_norm, w_out, ffn2_norm, ffn2_w_gate, ffn2_w_up, ffn2_w_down, ple_norm, ple_w_gate, ple_w_proj, final_norm, loss_target, m_ffn1_norm, m_ffn1_w_gate, m_ffn1_w_up, m_ffn1_w_down, m_mix_norm, m_w_in, m_conv_w, m_conv_b, m_ssm_A_re, m_ssm_A_im, m_ssm_B_re, m_ssm_B_im, m_ssm_C_re, m_ssm_C_im, m_ssm_D, m_ssm_log_dt, m_glu_w, m_glu_b, m_conv_out_norm, m_ssm_out_norm, m_w_out, m_ffn2_norm, m_ffn2_w_gate, m_ffn2_w_up, m_ffn2_w_down, m_ple_norm, m_ple_w_gate, m_ple_w_proj, m_final_norm, v_ffn1_norm, v_ffn1_w_gate, v_ffn1_w_up, v_ffn1_w_down, v_mix_norm, v_w_in, v_conv_w, v_conv_b, v_ssm_A_re, v_ssm_A_im, v_ssm_B_re, v_ssm_B_im, v_ssm_C_re, v_ssm_C_im, v_ssm_D, v_ssm_log_dt, v_glu_w, v_glu_b, v_conv_out_norm, v_ssm_out_norm, v_w_out, v_ffn2_norm, v_ffn2_w_gate, v_ffn2_w_up, v_ffn2_w_down, v_ple_norm, v_ple_w_gate, v_ple_w_proj, v_final_norm):
    return _step(dict(locals()))
```

```python
import functools
import math

import jax
import jax.numpy as jnp
from jax import lax
from jax.experimental import pallas as pl
from jax.experimental.pallas import tpu as pltpu

F32, BF16 = jnp.float32, jnp.bfloat16
S = jax.ShapeDtypeStruct
EPS = 1e-6
N_SEG = 8
N_SHARD = 4
N_DEV = 8
VMEM_LIMIT_BYTES = 56 * 1024 * 1024
ADAM_LR, ADAM_B1, ADAM_B2, ADAM_EPS, ADAM_WD, ADAM_STEP = 0.001, 0.9, 0.999, 1e-08, 0.01, 10
MESH = pl.DeviceIdType.MESH


def _params(*sem):
    return pltpu.CompilerParams(dimension_semantics=sem if sem else None, vmem_limit_bytes=VMEM_LIMIT_BYTES)


def _dot(a, b, ca, cb):
    return lax.dot_general(a, b, (((ca,), (cb,)), ((), ())), preferred_element_type=F32)


def _sigmoid(x):
    return 1.0 / (1.0 + jnp.exp(-x))


def _rstd(x):
    return lax.rsqrt(jnp.mean(x * x, axis=-1, keepdims=True) + EPS)


def _rms_bwd(x, g, dy):
    r = _rstd(x)
    xh = x * r
    dxh = dy * g
    dx = r * (dxh - xh * jnp.mean(dxh * xh, axis=-1, keepdims=True))
    return dx, jnp.sum(dy * xh, axis=0, keepdims=True)


def _tile(n, want):
    return want if n % want == 0 else n


def rmsnorm_fwd(h, g):
    L, D = h.shape
    tm = _tile(L, 512)

    def body(h_ref, g_ref, o_ref):
        x = h_ref[...]
        o_ref[...] = (x * _rstd(x) * g_ref[...]).astype(BF16)

    return pl.pallas_call(
        body, name="rmsnorm_fwd", grid=(L // tm,),
        in_specs=[pl.BlockSpec((tm, D), lambda m: (m, 0)), pl.BlockSpec((1, D), lambda m: (0, 0))],
        out_specs=pl.BlockSpec((tm, D), lambda m: (m, 0)),
        out_shape=S((L, D), BF16), compiler_params=_params("parallel"),
    )(h, g.reshape(1, D))


def ffn_up(u, wg, wu):
    L, D = u.shape
    ns, F, _ = wg.shape
    tm = _tile(L, 512)

    def body(u_ref, wg_ref, wu_ref, a_ref, b_ref, s_ref):
        x = u_ref[...]
        a = _dot(x, wg_ref[0], 1, 1)
        b = _dot(x, wu_ref[0], 1, 1)
        a_ref[0] = a.astype(BF16)
        b_ref[0] = b.astype(BF16)
        s_ref[0] = (a * _sigmoid(a) * b).astype(BF16)

    w_spec = pl.BlockSpec((1, F, D), lambda s, m: (s, 0, 0))
    o_spec = pl.BlockSpec((1, tm, F), lambda s, m: (s, m, 0))
    return pl.pallas_call(
        body, name="ffn_up", grid=(ns, L // tm),
        in_specs=[pl.BlockSpec((tm, D), lambda s, m: (m, 0)), w_spec, w_spec],
        out_specs=[o_spec, o_spec, o_spec],
        out_shape=[S((ns, L, F), BF16)] * 3, compiler_params=_params("parallel", "parallel"),
    )(u, wg, wu)


def mm_shard_n(u, w3, out_dtype):
    L, K = u.shape
    ns, _, N = w3.shape
    tm = _tile(L, 512)

    def body(u_ref, w_ref, o_ref):
        o_ref[0] = _dot(u_ref[...], w_ref[0], 1, 0).astype(out_dtype)

    return pl.pallas_call(
        body, name="mm_shard_n", grid=(ns, L // tm),
        in_specs=[pl.BlockSpec((tm, K), lambda s, m: (m, 0)), pl.BlockSpec((1, K, N), lambda s, m: (s, 0, 0))],
        out_specs=pl.BlockSpec((1, tm, N), lambda s, m: (s, m, 0)),
        out_shape=S((ns, L, N), out_dtype), compiler_params=_params("parallel", "parallel"),
    )(u, w3)


def mm_shard_k(a3, w3, res, scale, g_next):
    nk, L, Kc = a3.shape
    N = w3.shape[2]
    tm = _tile(L, 512)

    def body(a_ref, w_ref, r_ref, g_ref, o_ref, u_ref):
        acc = _dot(a_ref[0], w_ref[0], 1, 0)
        for k in range(1, nk):
            acc += _dot(a_ref[k], w_ref[k], 1, 0)
        h = r_ref[...] + scale * acc
        o_ref[...] = h
        u_ref[...] = (h * _rstd(h) * g_ref[...]).astype(BF16)

    tile = pl.BlockSpec((tm, N), lambda m: (m, 0))
    return pl.pallas_call(
        body, name="mm_shard_k", grid=(L // tm,),
        in_specs=[pl.BlockSpec((nk, tm, Kc), lambda m: (0, m, 0)), pl.BlockSpec((nk, Kc, N), lambda m: (0, 0, 0)),
                  tile, pl.BlockSpec((1, N), lambda m: (0, 0))],
        out_specs=[tile, tile],
        out_shape=[S((L, N), F32), S((L, N), BF16)], compiler_params=_params("parallel"),
    )(a3, w3, res, g_next.reshape(1, N))


CONV_HALO = 8


def _conv_specs(L, tm, C, shard):
    nb = L // CONV_HALO
    per = tm // CONV_HALO
    main = pl.BlockSpec((1, tm, C), lambda m: (shard, m, 0))
    prev = pl.BlockSpec((1, CONV_HALO, C), lambda m: (shard, jnp.maximum(m * per - 1, 0), 0))
    nxt = pl.BlockSpec((1, CONV_HALO, C), lambda m: (shard, jnp.minimum((m + 1) * per, nb - 1), 0))
    return main, prev, nxt


def _conv_core(zb, zc, zv, w_ref, bias, grow, L):
    valid = (grow >= 0) & (grow < L)
    v = jnp.where(valid, zc * zv, 0.0)
    v1 = pltpu.roll(v, 1, 0)
    v2 = pltpu.roll(v, 2, 0)
    cb = w_ref[0:1, :] * v2 + w_ref[1:2, :] * v1 + w_ref[2:3, :] * v + bias
    return valid, v, v1, v2, cb, zb * cb


def conv_fwd(z, conv_w, conv_b, gnorm):
    _, L, C = z.shape
    tm = _tile(L, 256)
    H = CONV_HALO

    def body(zb_ref, zc_ref, zcp_ref, zv_ref, zvp_ref, w_ref, b_ref, g_ref, o_ref):
        m = pl.program_id(0)
        zc = jnp.concatenate([zcp_ref[0], zc_ref[0]], axis=0)
        zv = jnp.concatenate([zvp_ref[0], zv_ref[0]], axis=0)
        grow = m * tm - H + lax.broadcasted_iota(jnp.int32, (tm + H, C), 0)
        valid = grow >= 0
        v = jnp.where(valid, zc * zv, 0.0)
        v1 = pltpu.roll(v, 1, 0)
        v2 = pltpu.roll(v, 2, 0)
        cb = (w_ref[0:1, :] * v2 + w_ref[1:2, :] * v1 + w_ref[2:3, :] * v + b_ref[...])[H:, :]
        ya = zb_ref[0] * cb
        o_ref[...] = (ya * _rstd(ya) * g_ref[...]).astype(BF16)

    zb_m, _, _ = _conv_specs(L, tm, C, 0)
    zc_m, zc_p, _ = _conv_specs(L, tm, C, 1)
    zv_m, zv_p, _ = _conv_specs(L, tm, C, 2)
    row = lambda r: pl.BlockSpec((r, C), lambda m: (0, 0))
    return pl.pallas_call(
        body, name="conv_fwd", grid=(L // tm,),
        in_specs=[zb_m, zc_m, zc_p, zv_m, zv_p, row(3), row(1), row(1)],
        out_specs=pl.BlockSpec((tm, C), lambda m: (m, 0)),
        out_shape=S((L, C), BF16), compiler_params=_params("parallel"),
    )(z, z, z, z, z, conv_w, conv_b.reshape(1, C), gnorm.reshape(1, C))


def _cmul(ar, ai, br, bi):
    return ar * br - ai * bi, ar * bi + ai * br


def _scan_fwd(hr_ref, hi_ref, lr, li, n_steps):
    W = hr_ref.shape[1]
    zero = jnp.zeros((N_SEG, W), F32)

    def local(t, c):
        r = pl.multiple_of(t * N_SEG, N_SEG)
        pr, pi = _cmul(lr, li, c[0], c[1])
        nr = pr + hr_ref[pl.ds(r, N_SEG), :]
        ni = pi + hi_ref[pl.ds(r, N_SEG), :]
        hr_ref[pl.ds(r, N_SEG), :] = nr
        hi_ref[pl.ds(r, N_SEG), :] = ni
        return nr, ni

    fr, fi = lax.fori_loop(0, n_steps, local, (zero, zero))
    qr, qi = _cpow(lr, li, n_steps)
    row = lax.broadcasted_iota(jnp.int32, (N_SEG, W), 0)
    cr, ci = zero, zero
    for seg in range(1, N_SEG):
        tr, ti = _cmul(qr, qi, cr, ci)
        sr = pltpu.roll(fr + tr, 1, 0)
        si = pltpu.roll(fi + ti, 1, 0)
        cr = jnp.where(row == seg, sr, cr)
        ci = jnp.where(row == seg, si, ci)

    def fix(t, c):
        r = pl.multiple_of(t * N_SEG, N_SEG)
        pr, pi = _cmul(lr, li, c[0], c[1])
        ar, ai = _cmul(pr, pi, cr, ci)
        hr_ref[pl.ds(r, N_SEG), :] += ar
        hi_ref[pl.ds(r, N_SEG), :] += ai
        return pr, pi

    lax.fori_loop(0, n_steps, fix, (jnp.ones((N_SEG, W), F32), zero))


def _cpow(lr, li, n):
    rr, ri = None, None
    br, bi = lr, li
    while n:
        if n & 1:
            rr, ri = (br, bi) if rr is None else _cmul(rr, ri, br, bi)
        n >>= 1
        if n:
            br, bi = _cmul(br, bi, br, bi)
    return rr, ri


def _ssm_specs(L):
    col = lambda w: pl.BlockSpec((L, w), lambda j: (0, j))
    return dict(
        u=col(128), lam=pl.BlockSpec((2, 512), lambda j: (0, j)),
        bmat=pl.BlockSpec((1, 128, 512), lambda j: (j, 0, 0)), cmat=pl.BlockSpec((1, 512, 128), lambda j: (j, 0, 0)),
        d=pl.BlockSpec((1, 128), lambda j: (0, j)))


def ssm_fwd(us, lam, bre, bim, cre, cim, dvec):
    L = us.shape[0]
    n_steps = L // N_SEG
    sp = _ssm_specs(L)

    def body(u_ref, lam_ref, bre_ref, bim_ref, cre_ref, cim_ref, d_ref, y_ref, hr, hi):
        u = u_ref[...]
        ub = u.astype(BF16)
        hr[...] = _dot(ub, bre_ref[0], 1, 0)
        hi[...] = _dot(ub, bim_ref[0], 1, 0)
        lr = jnp.broadcast_to(lam_ref[0:1, :], (N_SEG, 512))
        li = jnp.broadcast_to(lam_ref[1:2, :], (N_SEG, 512))
        _scan_fwd(hr, hi, lr, li, n_steps)
        y_ref[...] = (_dot(hr[...].astype(BF16), cre_ref[0], 1, 0) - _dot(hi[...].astype(BF16), cim_ref[0], 1, 0)
                      + d_ref[...] * u)

    return pl.pallas_call(
        body, name="ssm_fwd", grid=(4,),
        in_specs=[sp["u"], sp["lam"], sp["bmat"], sp["bmat"], sp["cmat"], sp["cmat"], sp["d"]],
        out_specs=sp["u"], out_shape=S((L, 512), F32),
        scratch_shapes=[pltpu.VMEM((L, 512), F32), pltpu.VMEM((L, 512), F32)],
        compiler_params=_params("parallel"),
    )(us, lam, bre, bim, cre, cim, dvec)


_GELU_C = math.sqrt(2.0 / math.pi)


def _gelu(y):
    t = jnp.tanh(_GELU_C * (y + 0.044715 * y * y * y))
    return 0.5 * y * (1.0 + t), t


def glu_fwd(y, w, b, gnorm):
    L, C = y.shape
    tm = _tile(L, 512)

    def body(y_ref, w_ref, b_ref, g_ref, o_ref):
        zg, _ = _gelu(y_ref[...])
        out = zg * _sigmoid(_dot(zg.astype(BF16), w_ref[...], 1, 0) + b_ref[...])
        o_ref[...] = (out * _rstd(out) * g_ref[...]).astype(BF16)

    row = pl.BlockSpec((1, C), lambda m: (0, 0))
    return pl.pallas_call(
        body, name="glu_fwd", grid=(L // tm,),
        in_specs=[pl.BlockSpec((tm, C), lambda m: (m, 0)), pl.BlockSpec((C, C), lambda m: (0, 0)), row, row],
        out_specs=pl.BlockSpec((tm, C), lambda m: (m, 0)),
        out_shape=S((L, C), BF16), compiler_params=_params("parallel"),
    )(y, w, b.reshape(1, C), gnorm.reshape(1, C))


def _ple_specs(L, D, P, tm, nb):
    return [pl.BlockSpec((tm, D), lambda n, m: (m, 0)), pl.BlockSpec((tm, P), lambda n, m: (m, 0)),
            pl.BlockSpec((D, nb), lambda n, m: (0, n)), pl.BlockSpec((1, P, nb), lambda n, m: (n, 0, 0)),
            pl.BlockSpec((tm, nb), lambda n, m: (m, n))]


def ple_fwd(un, pb, wpg, wpp, h):
    L, D = un.shape
    ns, P, nb = wpp.shape
    tm = _tile(L, 512)

    def body(un_ref, p_ref, wg_ref, wp_ref, h_ref, o_ref):
        gate = _sigmoid(_dot(un_ref[...], wg_ref[...], 1, 0))
        o_ref[...] = h_ref[...] + _dot(p_ref[...], wp_ref[0], 1, 0) * gate

    return pl.pallas_call(
        body, name="ple_fwd", grid=(ns, L // tm),
        in_specs=_ple_specs(L, D, P, tm, nb),
        out_specs=pl.BlockSpec((tm, nb), lambda n, m: (m, n)),
        out_shape=S((L, D), F32), compiler_params=_params("parallel", "parallel"),
    )(un, pb, wpg, wpp, h)


def loss_head(h, g, target):
    L, D = h.shape
    tm = _tile(L, 256)

    def body(h_ref, g_ref, t_ref, loss_ref, dh_ref, dg_ref):
        m = pl.program_id(0)
        x = h_ref[...]
        gg = g_ref[...]
        e = x * _rstd(x) * gg - t_ref[...]
        dx, dg = _rms_bwd(x, gg, e * (1.0 / D))
        dh_ref[...] = dx
        part = jnp.full((8, 128), 0.5 / D, F32) * jnp.sum(e * e)

        @pl.when(m == 0)
        def _():
            loss_ref[...] = part
            dg_ref[...] = dg

        @pl.when(m > 0)
        def _():
            loss_ref[...] += part
            dg_ref[...] += dg

    return pl.pallas_call(
        body, name="loss_head", grid=(L // tm,),
        in_specs=[pl.BlockSpec((tm, D), lambda m: (m, 0)), pl.BlockSpec((1, D), lambda m: (0, 0)),
                  pl.BlockSpec((tm, D), lambda m: (m, 0))],
        out_specs=[pl.BlockSpec((8, 128), lambda m: (0, 0)), pl.BlockSpec((tm, D), lambda m: (m, 0)),
                   pl.BlockSpec((1, D), lambda m: (0, 0))],
        out_shape=[S((8, 128), F32), S((L, D), F32), S((1, D), F32)],
        compiler_params=_params("arbitrary"),
    )(h, g.reshape(1, D), target)


def ple_bwd(un, pb, wpg, wpp, dh, token):
    L, D = un.shape
    ns, P, nb = wpp.shape
    tm = _tile(L, 512)

    def body(un_ref, p_ref, wg_ref, wp_ref, dh_ref, tok_ref, dpre_ref, dpp_ref):
        gate = _sigmoid(_dot(un_ref[...], wg_ref[...], 1, 0))
        pp = _dot(p_ref[...], wp_ref[0], 1, 0)
        d = dh_ref[...] + tok_ref[0:1, 0:1]
        dpp_ref[0] = (d * gate).astype(BF16)
        dpre_ref[...] = (d * pp * gate * (1.0 - gate)).astype(BF16)

    return pl.pallas_call(
        body, name="ple_bwd", grid=(ns, L // tm),
        in_specs=_ple_specs(L, D, P, tm, nb) + [pl.BlockSpec((8, 128), lambda n, m: (0, 0))],
        out_specs=[pl.BlockSpec((tm, nb), lambda n, m: (m, n)), pl.BlockSpec((1, tm, nb), lambda n, m: (n, m, 0))],
        out_shape=[S((L, D), BF16), S((ns, L, nb), BF16)], compiler_params=_params("parallel", "parallel"),
    )(un, pb, wpg, wpp, dh, token)


def wgrad(a, b):
    a3 = a if a.ndim == 3 else a[None]
    b3 = b if b.ndim == 3 else b[None]
    ns = max(a3.shape[0], b3.shape[0])
    _, L, Ka = a3.shape
    N = b3.shape[2]
    a_map = (lambda s: (s, 0, 0)) if a3.shape[0] > 1 else (lambda s: (0, 0, 0))
    b_map = (lambda s: (s, 0, 0)) if b3.shape[0] > 1 else (lambda s: (0, 0, 0))

    def body(a_ref, b_ref, o_ref):
        o_ref[0] = _dot(a_ref[0], b_ref[0], 0, 0).astype(BF16)

    return pl.pallas_call(
        body, name="wgrad", grid=(ns,),
        in_specs=[pl.BlockSpec((1, L, Ka), a_map), pl.BlockSpec((1, L, N), b_map)],
        out_specs=pl.BlockSpec((1, Ka, N), lambda s: (s, 0, 0)),
        out_shape=S((ns, Ka, N), BF16), compiler_params=_params("parallel"),
    )(a3, b3)


def dx_rms(pairs, h, g, dh_in, cast_scale):
    L, D = h.shape
    nk = pairs[0][0].shape[0]
    n_pairs = len(pairs)
    tm = _tile(L, 512)
    n_m = L // tm
    w_dims = [0 if transposed else 1 for _, _, transposed in pairs]

    def body(*refs):
        ins, (h_ref, g_ref, dhi_ref, dho_ref, dhb_ref, dg_ref) = refs[:2 * n_pairs], refs[2 * n_pairs:]
        m = pl.program_id(0)
        acc = None
        for i in range(n_pairs):
            for k in range(nk):
                part = _dot(ins[2 * i][k], ins[2 * i + 1][k], 1, w_dims[i])
                acc = part if acc is None else acc + part
        dx, dg = _rms_bwd(h_ref[...], g_ref[...], acc)
        dh_out = dhi_ref[...] + dx
        dho_ref[...] = dh_out
        dhb_ref[...] = (cast_scale * dh_out).astype(BF16)

        @pl.when(m == 0)
        def _():
            dg_ref[...] = dg

        @pl.when(m > 0)
        def _():
            dg_ref[...] += dg

    in_specs, args = [], []
    for a3, w3, _ in pairs:
        Kc = a3.shape[2]
        in_specs += [pl.BlockSpec((nk, tm, Kc), lambda m: (0, m, 0)),
                     pl.BlockSpec(w3.shape, lambda m: (0, 0, 0), pipeline_mode=pl.Buffered(1))]
        args += [a3, w3]
    tile = pl.BlockSpec((tm, D), lambda m: (m, 0))
    row = pl.BlockSpec((1, D), lambda m: (0, 0))
    return pl.pallas_call(
        body, name="dx_rms", grid=(n_m,),
        in_specs=in_specs + [tile, row, tile], out_specs=[tile, tile, row],
        out_shape=[S((L, D), F32), S((L, D), BF16), S((1, D), F32)], compiler_params=_params("arbitrary"),
    )(*args, h, g.reshape(1, D), dh_in)


def dact_plain(dhb, w3):
    L, D = dhb.shape
    ns, N, _ = w3.shape
    tm = _tile(L, 512)

    def body(d_ref, w_ref, o_ref):
        o_ref[0] = _dot(d_ref[...], w_ref[0], 1, 1)

    return pl.pallas_call(
        body, name="dact_plain", grid=(ns, L // tm),
        in_specs=[pl.BlockSpec((tm, D), lambda s, m: (m, 0)), pl.BlockSpec((1, N, D), lambda s, m: (s, 0, 0))],
        out_specs=pl.BlockSpec((1, tm, N), lambda s, m: (s, m, 0)),
        out_shape=S((ns, L, N), F32), compiler_params=_params("parallel", "parallel"),
    )(dhb, w3)


def dact_swiglu(dhb, wd, a3, b3):
    L, D = dhb.shape
    ns, F, _ = wd.shape
    tm = _tile(L, 512)

    def body(d_ref, w_ref, a_ref, b_ref, da_ref, db_ref):
        ds = _dot(d_ref[...], w_ref[0], 1, 1)
        a = a_ref[0].astype(F32)
        b = b_ref[0].astype(F32)
        sg = _sigmoid(a)
        da_ref[0] = (ds * b * (sg * (1.0 + a * (1.0 - sg)))).astype(BF16)
        db_ref[0] = (ds * (a * sg)).astype(BF16)

    t_spec = pl.BlockSpec((1, tm, F), lambda s, m: (s, m, 0))
    return pl.pallas_call(
        body, name="dact_swiglu", grid=(ns, L // tm),
        in_specs=[pl.BlockSpec((tm, D), lambda s, m: (m, 0)), pl.BlockSpec((1, F, D), lambda s, m: (s, 0, 0)), t_spec, t_spec],
        out_specs=[t_spec, t_spec], out_shape=[S((ns, L, F), BF16)] * 2,
        compiler_params=_params("parallel", "parallel"),
    )(dhb, wd, a3, b3)


def conv_bwd(z, conv_w, conv_b, gnorm, dyn):
    _, L, C = z.shape
    tm = _tile(L, 256)
    H = CONV_HALO
    T = tm + 2 * H

    def body(zb_ref, zbp_ref, zbn_ref, zc_ref, zcp_ref, zcn_ref, zv_ref, zvp_ref, zvn_ref, d_ref, dp_ref, dn_ref,
             w_ref, b_ref, g_ref, dz_ref, dw_ref, db_ref, dg_ref):
        m = pl.program_id(0)
        cat = lambda p, c, n: jnp.concatenate([p[0], c[0], n[0]], axis=0)
        zb, zc, zv, d = cat(zbp_ref, zb_ref, zbn_ref), cat(zcp_ref, zc_ref, zcn_ref), cat(zvp_ref, zv_ref, zvn_ref), cat(dp_ref, d_ref, dn_ref)
        grow = m * tm - H + lax.broadcasted_iota(jnp.int32, (T, C), 0)
        valid, v, v1, v2, cb, ya = _conv_core(zb, zc, zv, w_ref, b_ref[...], grow, L)
        dya, _ = _rms_bwd(ya, g_ref[...], d)
        dc = jnp.where(valid, dya * zb, 0.0)
        dv = w_ref[2:3, :] * dc + w_ref[1:2, :] * pltpu.roll(dc, T - 1, 0) + w_ref[0:1, :] * pltpu.roll(dc, T - 2, 0)
        dz_ref[0] = (dya * cb)[H:H + tm, :].astype(BF16)
        dz_ref[1] = (dv * zv)[H:H + tm, :].astype(BF16)
        dz_ref[2] = (dv * zc)[H:H + tm, :].astype(BF16)
        rs = lambda x: jnp.sum(x[H:H + tm, :], axis=0, keepdims=True)
        yh = ya * _rstd(ya)
        dw = jnp.concatenate([rs(dc * v2), rs(dc * v1), rs(dc * v)], axis=0)
        dbias, dg = rs(dc), rs(d * yh)

        @pl.when(m == 0)
        def _():
            dw_ref[...] = dw
            db_ref[...] = dbias
            dg_ref[...] = dg

        @pl.when(m > 0)
        def _():
            dw_ref[...] += dw
            db_ref[...] += dbias
            dg_ref[...] += dg

    row = lambda r: pl.BlockSpec((r, C), lambda m: (0, 0))
    specs = [*_conv_specs(L, tm, C, 0), *_conv_specs(L, tm, C, 1), *_conv_specs(L, tm, C, 2), *_conv_specs(L, tm, C, 0)]
    return pl.pallas_call(
        body, name="conv_bwd", grid=(L // tm,),
        in_specs=specs + [row(3), row(1), row(1)],
        out_specs=[pl.BlockSpec((3, tm, C), lambda m: (0, m, 0)), row(3), row(1), row(1)],
        out_shape=[S((3, L, C), BF16), S((3, C), F32), S((1, C), F32), S((1, C), F32)],
        compiler_params=_params("arbitrary"),
    )(z, z, z, z, z, z, z, z, z, dyn, dyn, dyn, conv_w, conv_b.reshape(1, C), gnorm.reshape(1, C))


def glu_bwd(y, w, b, gnorm, dn):
    L, C = y.shape
    tm = _tile(L, 256)

    def body(y_ref, w_ref, b_ref, g_ref, d_ref, dy_ref, dpre_ref, zg_ref, db_ref, dg_ref):
        m = pl.program_id(0)
        yv = y_ref[...]
        zg, t = _gelu(yv)
        zgb = zg.astype(BF16)
        sg = _sigmoid(_dot(zgb, w_ref[...], 1, 0) + b_ref[...])
        out = zg * sg
        dout, dg = _rms_bwd(out, g_ref[...], d_ref[...])
        dpre = dout * zg * sg * (1.0 - sg)
        dpre_b = dpre.astype(BF16)
        dzg = dout * sg + _dot(dpre_b, w_ref[...], 1, 1)
        dt = (1.0 - t * t) * _GELU_C * (1.0 + 3.0 * 0.044715 * yv * yv)
        dy_ref[...] = dzg * (0.5 * (1.0 + t) + 0.5 * yv * dt)
        dpre_ref[...] = dpre_b
        zg_ref[...] = zgb
        dbias = jnp.sum(dpre, axis=0, keepdims=True)

        @pl.when(m == 0)
        def _():
            db_ref[...] = dbias
            dg_ref[...] = dg

        @pl.when(m > 0)
        def _():
            db_ref[...] += dbias
            dg_ref[...] += dg

    tile = pl.BlockSpec((tm, C), lambda m: (m, 0))
    row = pl.BlockSpec((1, C), lambda m: (0, 0))
    return pl.pallas_call(
        body, name="glu_bwd", grid=(L // tm,),
        in_specs=[tile, pl.BlockSpec((C, C), lambda m: (0, 0)), row, row, tile],
        out_specs=[tile, tile, tile, row, row],
        out_shape=[S((L, C), F32), S((L, C), BF16), S((L, C), BF16), S((1, C), F32), S((1, C), F32)],
        compiler_params=_params("arbitrary"),
    )(y, w, b.reshape(1, C), gnorm.reshape(1, C), dn)


def _scan_bwd(gr_ref, gi_ref, hr_ref, hi_ref, lr, li, n_steps):
    W = gr_ref.shape[1]
    zero = jnp.zeros((N_SEG, W), F32)
    lic = -li

    def local(i, c):
        r = pl.multiple_of((n_steps - 1 - i) * N_SEG, N_SEG)
        pr, pi = _cmul(lr, lic, c[0], c[1])
        nr = pr + gr_ref[pl.ds(r, N_SEG), :]
        ni = pi + gi_ref[pl.ds(r, N_SEG), :]
        gr_ref[pl.ds(r, N_SEG), :] = nr
        gi_ref[pl.ds(r, N_SEG), :] = ni
        return nr, ni

    fr, fi = lax.fori_loop(0, n_steps, local, (zero, zero))
    qr, qi = _cpow(lr, lic, n_steps)
    row = lax.broadcasted_iota(jnp.int32, (N_SEG, W), 0)
    cr, ci = zero, zero
    for seg in range(N_SEG - 2, -1, -1):
        tr, ti = _cmul(qr, qi, cr, ci)
        sr = pltpu.roll(fr + tr, N_SEG - 1, 0)
        si = pltpu.roll(fi + ti, N_SEG - 1, 0)
        cr = jnp.where(row == seg, sr, cr)
        ci = jnp.where(row == seg, si, ci)

    def fix(i, c):
        pwr, pwi, ar, ai = c
        t = n_steps - 1 - i
        r = pl.multiple_of(t * N_SEG, N_SEG)
        pwr, pwi = _cmul(lr, lic, pwr, pwi)
        xr, xi = _cmul(pwr, pwi, cr, ci)
        g_r = gr_ref[pl.ds(r, N_SEG), :] + xr
        g_i = gi_ref[pl.ds(r, N_SEG), :] + xi
        gr_ref[pl.ds(r, N_SEG), :] = g_r
        gi_ref[pl.ds(r, N_SEG), :] = g_i
        rp = pl.multiple_of(jnp.maximum(t - 1, 0) * N_SEG, N_SEG)
        hpr = hr_ref[pl.ds(rp, N_SEG), :]
        hpi = hi_ref[pl.ds(rp, N_SEG), :]
        live = t > 0
        ar = ar + jnp.where(live, hpr * g_r + hpi * g_i, 0.0)
        ai = ai + jnp.where(live, hpr * g_i - hpi * g_r, 0.0)
        return pwr, pwi, ar, ai

    _, _, ar, ai = lax.fori_loop(0, n_steps, fix, (jnp.ones((N_SEG, W), F32), zero, zero, zero))
    last = pl.ds((n_steps - 1) * N_SEG, N_SEG)
    hpr = jnp.where(row == 0, 0.0, pltpu.roll(hr_ref[last, :], 1, 0))
    hpi = jnp.where(row == 0, 0.0, pltpu.roll(hi_ref[last, :], 1, 0))
    g_r, g_i = gr_ref[pl.ds(0, N_SEG), :], gi_ref[pl.ds(0, N_SEG), :]
    ar = ar + hpr * g_r + hpi * g_i
    ai = ai + hpr * g_i - hpi * g_r
    return jnp.sum(ar, axis=0, keepdims=True), jnp.sum(ai, axis=0, keepdims=True)


def ssm_bwd(us, dy, lam, bre, bim, cre, cim, dvec):
    L = us.shape[0]
    n_steps = L // N_SEG
    sp = _ssm_specs(L)

    def body(u_ref, dy_ref, lam_ref, bre_ref, bim_ref, cre_ref, cim_ref, d_ref,
             du_ref, dlam_ref, dbre_ref, dbim_ref, dcre_ref, dcim_ref, dd_ref, hr, hi, gr, gi):
        u = u_ref[...]
        ub = u.astype(BF16)
        dyv = dy_ref[...]
        dyb = dyv.astype(BF16)
        hr[...] = _dot(ub, bre_ref[0], 1, 0)
        hi[...] = _dot(ub, bim_ref[0], 1, 0)
        lr = jnp.broadcast_to(lam_ref[0:1, :], (N_SEG, 512))
        li = jnp.broadcast_to(lam_ref[1:2, :], (N_SEG, 512))
        _scan_fwd(hr, hi, lr, li, n_steps)
        dcre_ref[0] = _dot(hr[...].astype(BF16), dyb, 0, 0)
        dcim_ref[0] = -_dot(hi[...].astype(BF16), dyb, 0, 0)
        gr[...] = _dot(dyb, cre_ref[0], 1, 1)
        gi[...] = -_dot(dyb, cim_ref[0], 1, 1)
        dlr, dli = _scan_bwd(gr, gi, hr, hi, lr, li, n_steps)
        dlam_ref[...] = jnp.concatenate([dlr, dli], axis=0)
        grb, gib = gr[...].astype(BF16), gi[...].astype(BF16)
        du_ref[...] = _dot(grb, bre_ref[0], 1, 1) + _dot(gib, bim_ref[0], 1, 1) + d_ref[...] * dyv
        dbre_ref[0] = _dot(ub, grb, 0, 0)
        dbim_ref[0] = _dot(ub, gib, 0, 0)
        dd_ref[...] = jnp.sum(dyv * u, axis=0, keepdims=True)

    big = pltpu.VMEM((L, 512), F32)
    return pl.pallas_call(
        body, name="ssm_bwd", grid=(4,),
        in_specs=[sp["u"], sp["u"], sp["lam"], sp["bmat"], sp["bmat"], sp["cmat"], sp["cmat"], sp["d"]],
        out_specs=[sp["u"], sp["lam"], sp["bmat"], sp["bmat"], sp["cmat"], sp["cmat"], sp["d"]],
        out_shape=[S((L, 512), F32), S((2, 2048), F32), S((4, 128, 512), F32), S((4, 128, 512), F32),
                   S((4, 512, 128), F32), S((4, 512, 128), F32), S((1, 512), F32)],
        scratch_shapes=[big, big, big, big], compiler_params=_params("parallel"),
    )(us, dy, lam, bre, bim, cre, cim, dvec)


def _discretize(ar, ai, log_dt, br, bi):
    dt = jnp.exp(log_dt)
    mag = jnp.exp(ar * dt)
    ph = ai * dt
    lr, li = mag * jnp.cos(ph), mag * jnp.sin(ph)
    nr, ni = lr - 1.0, li
    den = ar * ar + ai * ai
    fr = (nr * ar + ni * ai) / den
    fi = (ni * ar - nr * ai) / den
    return lr, li, fr[..., None] * br - fi[..., None] * bi, fr[..., None] * bi + fi[..., None] * br


def ssm_prep(ar, ai, log_dt, br, bi):
    G, P, H = br.shape

    def body(ar_ref, ai_ref, dt_ref, br_ref, bi_ref, lr_ref, li_ref, bbr_ref, bbi_ref):
        lr_ref[...], li_ref[...], bbr_ref[...], bbi_ref[...] = _discretize(
            ar_ref[...], ai_ref[...], dt_ref[...], br_ref[...], bi_ref[...])

    return pl.pallas_call(
        body, name="ssm_prep",
        out_shape=[S((G, P), F32), S((G, P), F32), S((G, P, H), F32), S((G, P, H), F32)],
    )(ar, ai, log_dt.reshape(G, 1), br, bi)


def ssm_prep_bwd(ar, ai, log_dt, br, bi, dlr, dli, dbbr, dbbi):
    G, P, H = br.shape

    def body(ar_ref, ai_ref, dt_ref, br_ref, bi_ref, dlr_ref, dli_ref, dbbr_ref, dbbi_ref,
             dar_ref, dai_ref, ddt_ref, dbr_ref, dbi_ref):
        _, vjp = jax.vjp(_discretize, ar_ref[...], ai_ref[...], dt_ref[...], br_ref[...], bi_ref[...])
        dar_ref[...], dai_ref[...], ddt_ref[...], dbr_ref[...], dbi_ref[...] = vjp(
            (dlr_ref[...], dli_ref[...], dbbr_ref[...], dbbi_ref[...]))

    return pl.pallas_call(
        body, name="ssm_prep_bwd",
        out_shape=[S((G, P), F32), S((G, P), F32), S((G, 1), F32), S((G, P, H), F32), S((G, P, H), F32)],
    )(ar, ai, log_dt.reshape(G, 1), br, bi, dlr, dli, dbbr, dbbi)


def _block_diag(x):
    j, n, R, C = x.shape
    eye = jnp.eye(n, dtype=x.dtype)
    return (x[:, :, :, None, :] * eye[None, :, None, :, None]).reshape(j, n * R, n * C)


def _block_diag_take(x, R, C):
    j = x.shape[0]
    n = x.shape[1] // R
    x5 = x.reshape(j, n, R, n, C)
    return jnp.stack([x5[:, i, :, i, :] for i in range(n)], axis=1)


def _to_segments(x):
    L, C = x.shape
    return x.reshape(N_SEG, L // N_SEG, C).transpose(1, 0, 2).reshape(L, C)


def _from_segments(x):
    L, C = x.shape
    return x.reshape(L // N_SEG, N_SEG, C).transpose(1, 0, 2).reshape(L, C)


BIG = ("ffn1_w_gate", "ffn1_w_up", "ffn1_w_down", "w_in", "glu_w", "w_out",
       "ffn2_w_gate", "ffn2_w_up", "ffn2_w_down", "ple_w_gate", "ple_w_proj")
SMALL = ("ffn1_norm", "mix_norm", "conv_w", "conv_b", "ssm_A_re", "ssm_A_im", "ssm_B_re", "ssm_B_im", "ssm_C_re", "ssm_C_im",
         "ssm_D", "ssm_log_dt", "glu_b", "conv_out_norm", "ssm_out_norm", "ffn2_norm", "ple_norm")


def _ssm_mats(w):
    G, P, H = w["ssm_B_re"].shape
    lr, li, bbr, bbi = ssm_prep(w["ssm_A_re"], w["ssm_A_im"], w["ssm_log_dt"], w["ssm_B_re"], w["ssm_B_im"])
    lam = jnp.stack([lr.reshape(G * P), li.reshape(G * P)])
    bmat = lambda bb: _block_diag(bb.reshape(4, G // 4, P, H).transpose(0, 1, 3, 2)).astype(BF16)
    cmat = lambda c: _block_diag(c.reshape(4, G // 4, H, P).transpose(0, 1, 3, 2)).astype(BF16)
    return lam, bmat(bbr), bmat(bbi), cmat(w["ssm_C_re"]), cmat(w["ssm_C_im"]), w["ssm_D"].reshape(1, G * H)


def layer_fwd(h0, pb, w):
    L, D = h0.shape
    u1 = rmsnorm_fwd(h0, w["ffn1_norm"])
    a1, b1, s1 = ffn_up(u1, w["ffn1_w_gate"], w["ffn1_w_up"])
    h1, u2 = mm_shard_k(s1, w["ffn1_w_down"], h0, 0.5, w["mix_norm"])
    z = mm_shard_n(u2, w["w_in"], F32)
    ya_n = conv_fwd(z, w["conv_w"], w["conv_b"], w["conv_out_norm"])
    us = _to_segments(z[3])
    mats = _ssm_mats(w)
    y = ssm_fwd(us, *mats)
    ys_n = glu_fwd(y, w["glu_w"], w["glu_b"], w["ssm_out_norm"])
    ycat = jnp.stack([ya_n, _from_segments(ys_n)])
    h2, u3 = mm_shard_k(ycat, w["w_out"], h1, 1.0, w["ffn2_norm"])
    a2, b2, s2 = ffn_up(u3, w["ffn2_w_gate"], w["ffn2_w_up"])
    h3, un = mm_shard_k(s2, w["ffn2_w_down"], h2, 0.5, w["ple_norm"])
    h4 = ple_fwd(un, pb, w["ple_w_gate"], w["ple_w_proj"], h3)
    saved = dict(h0=h0, u1=u1, a1=a1, b1=b1, s1=s1, h1=h1, u2=u2, z=z, us=us, mats=mats, y=y, ycat=ycat,
                 h2=h2, u3=u3, a2=a2, b2=b2, s2=s2, h3=h3, un=un)
    return h4, saved


def _ffn_bwd(dh, dhb, h_in, u, a, b, s, wg, wu, wd, gnorm, cast_scale):
    da, db = dact_swiglu(dhb, wd, a, b)
    g_wd = wgrad(s, dhb)
    g_wg = wgrad(da, u)
    g_wu = wgrad(db, u)
    dh_in, dhb_in, g_norm = dx_rms([(da, wg, True), (db, wu, True)], h_in, gnorm, dh, cast_scale)
    return dh_in, dhb_in, g_wg, g_wu, g_wd, g_norm


def layer_bwd(dh, pb, w, sv, token):
    L, D = dh.shape
    G, P, H = w["ssm_B_re"].shape
    dpre, dpp3 = ple_bwd(sv["un"], pb, w["ple_w_gate"], w["ple_w_proj"], dh, token)
    g_wpg = wgrad(sv["un"], dpre).reshape(N_SHARD, D // N_SHARD, D)
    g_wpp = wgrad(pb, dpp3)
    dh3, dhb3, g_nple = dx_rms([(dpre[None], w["ple_w_gate"][None], False)], sv["h3"], w["ple_norm"], dh, 0.5)
    dh2, dhb, g_wg2, g_wu2, g_wd2, g_nffn2 = _ffn_bwd(dh3, dhb3, sv["h2"], sv["u3"], sv["a2"], sv["b2"], sv["s2"],
                                                      w["ffn2_w_gate"], w["ffn2_w_up"], w["ffn2_w_down"], w["ffn2_norm"], 1.0)
    dyn = dact_plain(dhb, w["w_out"])
    g_wout = wgrad(sv["ycat"], dhb).reshape(N_SHARD, -1, D)
    dz_abc, g_convw, g_convb, g_nconv = conv_bwd(sv["z"], w["conv_w"], w["conv_b"], w["conv_out_norm"], dyn)
    dy, dpre_g, zg, g_glub, g_nssm = glu_bwd(sv["y"], w["glu_w"], w["glu_b"], w["ssm_out_norm"], _to_segments(dyn[1]))
    C = zg.shape[1]
    g_gluw = wgrad(zg, dpre_g).reshape(N_SHARD, C // N_SHARD, C)
    dus, dlam, dbre, dbim, dcre, dcim, dd = ssm_bwd(sv["us"], dy, *sv["mats"])
    take_b = lambda m: _block_diag_take(m, H, P).transpose(0, 1, 3, 2).reshape(G, P, H)
    take_c = lambda m: _block_diag_take(m, P, H).transpose(0, 1, 3, 2).reshape(G, H, P)
    g_ar, g_ai, g_dt, g_br, g_bi = ssm_prep_bwd(
        w["ssm_A_re"], w["ssm_A_im"], w["ssm_log_dt"], w["ssm_B_re"], w["ssm_B_im"],
        dlam[0].reshape(G, P), dlam[1].reshape(G, P), take_b(dbre), take_b(dbim))
    dz3 = jnp.concatenate([dz_abc, _from_segments(dus).astype(BF16)[None]], axis=0)
    g_win = wgrad(sv["u2"], dz3)
    dh1, dhb1, g_nmix = dx_rms([(dz3, w["w_in"], False)], sv["h1"], w["mix_norm"], dh2, 0.5)
    dh0, _, g_wg1, g_wu1, g_wd1, g_nffn1 = _ffn_bwd(dh1, dhb1, sv["h0"], sv["u1"], sv["a1"], sv["b1"], sv["s1"],
                                                    w["ffn1_w_gate"], w["ffn1_w_up"], w["ffn1_w_down"], w["ffn1_norm"], 1.0)
    big = [g_wg1, g_wu1, g_wd1, g_win, g_gluw, g_wout, g_wg2, g_wu2, g_wd2, g_wpg, g_wpp]
    small = dict(ffn1_norm=g_nffn1, mix_norm=g_nmix, conv_w=g_convw, conv_b=g_convb, ssm_A_re=g_ar, ssm_A_im=g_ai,
                 ssm_B_re=g_br, ssm_B_im=g_bi, ssm_C_re=take_c(dcre), ssm_C_im=take_c(dcim), ssm_D=dd,
                 ssm_log_dt=g_dt, glu_b=g_glub, conv_out_norm=g_nconv, ssm_out_norm=g_nssm, ffn2_norm=g_nffn2,
                 ple_norm=g_nple)
    return dh0, big, small


def local_step(x, p, target, final_norm, weights_of, on_grads):
    depth = p.shape[0]
    h = x
    layers, saved, pbs = [], [], []
    for i in range(depth):
        w = weights_of(i, h)
        pb = p[i].astype(BF16)
        h, sv = layer_fwd(h, pb, w)
        layers.append(w)
        saved.append(sv)
        pbs.append(pb)
    loss_part, dh, g_final = loss_head(h, final_norm, target)
    smalls = [None] * depth
    token = jnp.zeros((8, 128), F32)
    for i in reversed(range(depth)):
        dh, big, smalls[i] = layer_bwd(dh, pbs[i], layers[i], saved[i], token)
        token = on_grads(i, big, dh)
    return loss_part, dh, smalls, g_final


ROW_TILE_MAX = 512


def _row_tile(rows):
    for t in range(ROW_TILE_MAX, 0, -16):
        if rows % t == 0:
            return t
    return rows


def elementwise(fn, ins, out_dtypes, name):
    rows, cols = ins[0].shape
    tr = _row_tile(rows)
    n_in = len(ins)

    def body(*refs):
        outs = fn(*[r[...] for r in refs[:n_in]])
        for o_ref, o in zip(refs[n_in:], outs):
            o_ref[...] = o.astype(o_ref.dtype)

    spec = pl.BlockSpec((tr, cols), lambda i: (i, 0))
    return pl.pallas_call(
        body, name=name, grid=(rows // tr,), in_specs=[spec] * n_in, out_specs=[spec] * len(out_dtypes),
        out_shape=[S((rows, cols), d) for d in out_dtypes], compiler_params=_params("parallel"),
    )(*ins)


def _adamw(w, g, m, v):
    m = ADAM_B1 * m + (1.0 - ADAM_B1) * g
    v = ADAM_B2 * v + (1.0 - ADAM_B2) * (g * g)
    m_hat = m / (1.0 - ADAM_B1 ** ADAM_STEP)
    v_hat = v / (1.0 - ADAM_B2 ** ADAM_STEP)
    delta = -ADAM_LR * (m_hat / (jnp.sqrt(v_hat) + ADAM_EPS) + ADAM_WD * w)
    return delta, m, v


ANY = pl.BlockSpec(memory_space=pl.ANY)


def _mesh_pos():
    return lax.axis_index("x"), lax.axis_index("y"), lax.axis_index("c")


def _other_chips(x, y):
    return [(1 - x, y), (x, 1 - y), (1 - x, 1 - y)]


def _remote(src, dst, send_sem, recv_sem, device):
    return pltpu.make_async_remote_copy(src_ref=src, dst_ref=dst, send_sem=send_sem, recv_sem=recv_sem,
                                        device_id=device, device_id_type=MESH)


def gather_weights(ws):
    n = len(ws)

    def body(*refs):
        outs = refs[n:2 * n]
        send_sems, recv_sems = refs[2 * n:]
        x, y, c = _mesh_pos()
        me_s = 2 * x + y
        sibling = (x, y, 1 - c)
        chips = _other_chips(x, y)
        n_half = outs[0].shape[0] // 2
        mine, other = pl.ds(c * n_half, n_half), pl.ds((1 - c) * n_half, n_half)
        sent = []
        for t in range(n):
            for j, (cx, cy) in enumerate(chips):
                blk = outs[t].at[mine, me_s]
                cp = _remote(blk, blk, send_sems.at[t, j], recv_sems.at[t, j], (cx, cy, c))
                cp.start()
                sent.append(cp)
        for j, (cx, cy) in enumerate(chips):
            for t in range(n):
                blk = outs[t].at[mine, 2 * cx + cy]
                _remote(blk, blk, send_sems.at[t, j], recv_sems.at[t, j], (cx, cy, c)).wait_recv()
                cp = _remote(blk, blk, send_sems.at[t, 3 + j], recv_sems.at[t, 3 + j], sibling)
                cp.start()
                sent.append(cp)
        for j, (cx, cy) in enumerate(chips):
            for t in range(n):
                blk = outs[t].at[other, 2 * cx + cy]
                _remote(blk, blk, send_sems.at[t, 3 + j], recv_sems.at[t, 3 + j], sibling).wait_recv()
        for cp in sent:
            cp.wait_send()

    return pl.pallas_call(
        body, name="gather_weights", in_specs=[ANY] * n, out_specs=[ANY] * n,
        out_shape=[S(w.shape, w.dtype) for w in ws], input_output_aliases={t: t for t in range(n)},
        scratch_shapes=[pltpu.SemaphoreType.DMA((n, 6)), pltpu.SemaphoreType.DMA((n, 6))],
    )(*ws)


HBM = pl.BlockSpec(memory_space=pltpu.HBM)
SEM = pl.BlockSpec(memory_space=pltpu.SEMAPHORE)
VMEM_WHOLE = pl.BlockSpec(memory_space=pltpu.VMEM)
SPLIT_COPY = pltpu.CompilerParams(has_side_effects=pltpu.SideEffectType.DATAFLOW_SIDE_EFFECTING)


def _hbm(x):
    return pltpu.with_memory_space_constraint(x, pltpu.HBM)


def _half_rows(ref, c):
    r2 = ref.shape[1] // 2
    return pl.ds(pl.multiple_of(c * r2, 8), r2)


def gather_start(bufs, layer, after):
    n, k = len(bufs), len(after)

    def body(*refs):
        ins, send_sems, recv_sems, token = refs[:n], refs[n + k], refs[n + k + 1], refs[2 * n + k + 2]
        x, y, c = _mesh_pos()
        me_s = 2 * x + y
        for t in range(n):
            blk = ins[t].at[me_s, _half_rows(ins[t], c)]
            for j, (cx, cy) in enumerate(_other_chips(x, y)):
                _remote(blk, blk, send_sems.at[3 * t + j], recv_sems.at[3 * t + j], (cx, cy, c)).start()
        token[...] = jnp.zeros_like(token)

    outs = pl.pallas_call(
        body, name=f"gather_start_{layer}", in_specs=[HBM] * n + [ANY] * k, out_specs=[SEM, SEM] + [HBM] * n + [VMEM_WHOLE],
        out_shape=[pltpu.SemaphoreType.DMA((3 * n,)), pltpu.SemaphoreType.DMA((3 * n,))]
        + [pltpu.HBM(b.shape, b.dtype) for b in bufs] + [S((8, 128), F32)],
        input_output_aliases={t: t + 2 for t in range(n)}, compiler_params=SPLIT_COPY,
    )(*[_hbm(b) for b in bufs], *after)
    return outs[0], outs[1], list(outs[2:2 + n]), outs[2 + n]


def gather_wait(bufs, send_sems, recv_sems, after, layer):
    n, n_after = len(bufs), len(after)

    def body(*refs):
        ins, send_ref, recv_ref = refs[:n], refs[n], refs[n + 1]
        x, y, c = _mesh_pos()
        me_s = 2 * x + y
        for t in range(n):
            rows = _half_rows(ins[t], c)
            for j, (cx, cy) in enumerate(_other_chips(x, y)):
                cp = _remote(ins[t].at[me_s, rows], ins[t].at[2 * cx + cy, rows], send_ref.at[3 * t + j], recv_ref.at[3 * t + j],
                             (cx, cy, c))
                cp.wait_send()
                cp.wait_recv()

    outs = pl.pallas_call(
        body, name=f"gather_wait_{layer}", in_specs=[HBM] * n + [SEM, SEM] + [ANY] * n_after, out_specs=[HBM] * n,
        out_shape=[pltpu.HBM(b.shape, b.dtype) for b in bufs],
        input_output_aliases={t: t for t in range(n)}, compiler_params=SPLIT_COPY,
    )(*bufs, send_sems, recv_sems, *after)
    return list(outs)


def gather_forward(bufs):
    n = len(bufs)

    def body(*refs):
        outs = refs[n:2 * n]
        send_sems, recv_sems = refs[2 * n:]
        x, y, c = _mesh_pos()
        copies = []
        for t in range(n):
            rows = _half_rows(outs[t], c)
            for j, (cx, cy) in enumerate(_other_chips(x, y)):
                blk = outs[t].at[2 * cx + cy, rows]
                cp = _remote(blk, blk, send_sems.at[t, j], recv_sems.at[t, j], (x, y, 1 - c))
                cp.start()
                copies.append(cp)
        for cp in copies:
            cp.wait()

    return pl.pallas_call(
        body, name="gather_forward", in_specs=[ANY] * n, out_specs=[ANY] * n,
        out_shape=[S(b.shape, b.dtype) for b in bufs], input_output_aliases={t: t for t in range(n)},
        scratch_shapes=[pltpu.SemaphoreType.DMA((n, 3)), pltpu.SemaphoreType.DMA((n, 3))],
    )(*bufs)


def chips_start(sums, layer):
    n = len(sums)
    lands = [lax.empty((3,) + s.shape[1:], s.dtype) for s in sums]

    def body(*refs):
        a, land, send_sems, recv_sems, token = refs[:n], refs[n:2 * n], refs[2 * n], refs[2 * n + 1], refs[4 * n + 2]
        x, y, c = _mesh_pos()
        for t in range(n):
            for j, (cx, cy) in enumerate(_other_chips(x, y)):
                _remote(a[t].at[2 * cx + cy], land[t].at[j], send_sems.at[3 * t + j], recv_sems.at[3 * t + j], (cx, cy, c)).start()
        token[...] = jnp.zeros_like(token)

    outs = pl.pallas_call(
        body, name=f"chips_start_{layer}", in_specs=[HBM] * (2 * n), out_specs=[SEM, SEM] + [HBM] * (2 * n) + [VMEM_WHOLE],
        out_shape=[pltpu.SemaphoreType.DMA((3 * n,)), pltpu.SemaphoreType.DMA((3 * n,))]
        + [pltpu.HBM(b.shape, b.dtype) for b in sums + lands] + [S((8, 128), F32)],
        input_output_aliases={t: t + 2 for t in range(2 * n)}, compiler_params=SPLIT_COPY,
    )(*[_hbm(b) for b in sums + lands])
    return outs[0], outs[1], list(outs[2:2 + n]), list(outs[2 + n:2 + 2 * n]), outs[2 + 2 * n]


def chips_wait(sums, lands, send_sems, recv_sems, after, layer):
    n, n_after = len(sums), len(after)

    def body(*refs):
        a, land, send_ref, recv_ref = refs[:n], refs[n:2 * n], refs[2 * n], refs[2 * n + 1]
        x, y, c = _mesh_pos()
        for t in range(n):
            for j, (cx, cy) in enumerate(_other_chips(x, y)):
                cp = _remote(a[t].at[2 * cx + cy], land[t].at[j], send_ref.at[3 * t + j], recv_ref.at[3 * t + j], (cx, cy, c))
                cp.wait_send()
                cp.wait_recv()

    outs = pl.pallas_call(
        body, name=f"chips_wait_{layer}", in_specs=[HBM] * (2 * n) + [SEM, SEM] + [ANY] * n_after, out_specs=[HBM] * (2 * n),
        out_shape=[pltpu.HBM(b.shape, b.dtype) for b in sums + lands],
        input_output_aliases={t: t for t in range(2 * n)}, compiler_params=SPLIT_COPY,
    )(*sums, *lands, send_sems, recv_sems, *after)
    return list(outs[n:])


def cast_place_layer(w, layer, pos, dtype):
    _, r, c = w.shape
    tr = _row_tile(r)

    def body(pos_ref, w_ref, o_ref):
        o_ref[0] = w_ref[0].astype(dtype)

    return pl.pallas_call(
        body, name="cast_place_layer",
        grid_spec=pltpu.PrefetchScalarGridSpec(
            num_scalar_prefetch=1, grid=(r // tr,),
            in_specs=[pl.BlockSpec((1, tr, c), lambda i, pos: (layer, i, 0))],
            out_specs=pl.BlockSpec((1, tr, c), lambda i, pos: (pos[1], i, 0))),
        out_shape=S((N_SHARD, r, c), dtype), compiler_params=_params("parallel"),
    )(pos, w)


def cast_place(w, pos, dtype):
    layers, r, c = w.shape
    tr = _row_tile(r)

    def body(pos_ref, w_ref, o_ref):
        o_ref[0, 0] = w_ref[0].astype(dtype)

    return pl.pallas_call(
        body, name="cast_place",
        grid_spec=pltpu.PrefetchScalarGridSpec(
            num_scalar_prefetch=1, grid=(layers, r // tr),
            in_specs=[pl.BlockSpec((1, tr, c), lambda l, i, pos: (l, i, 0))],
            out_specs=pl.BlockSpec((1, 1, tr, c), lambda l, i, pos: (l, pos[1], i, 0))),
        out_shape=S((layers, N_SHARD, r, c), dtype), compiler_params=_params("parallel", "parallel"),
    )(pos, w)


def _pair_copy(g_ref, got_ref, send_sem, recv_sem):
    x, y, c = _mesh_pos()
    r2 = g_ref.shape[1] // 2
    give = pl.ds(pl.multiple_of((1 - c) * r2, 8), r2)
    return _remote(g_ref.at[:, give], got_ref, send_sem, recv_sem, (x, y, 1 - c))


def reduce_pair(gs, after):
    n, k = len(gs), len(after)

    def body(*refs):
        ins, got = refs[:n], refs[n + k:2 * n + k]
        send_sems, recv_sems = refs[2 * n + k:]
        copies = [_pair_copy(ins[t], got[t], send_sems.at[t], recv_sems.at[t]) for t in range(n)]
        for cp in copies:
            cp.start()
        for cp in copies:
            cp.wait()

    return pl.pallas_call(
        body, name="reduce_pair", in_specs=[ANY] * (n + k), out_specs=[ANY] * n,
        out_shape=[S((g.shape[0], g.shape[1] // 2, g.shape[2]), g.dtype) for g in gs],
        scratch_shapes=[pltpu.SemaphoreType.DMA((n,)), pltpu.SemaphoreType.DMA((n,))],
    )(*gs, *after)


def pair_sum(g, got, pos):
    ns, r2, c = got.shape
    tr = _row_tile(r2)
    n_i = r2 // tr

    def body(pos_ref, g_ref, got_ref, sum_ref, own_ref):
        s = pl.program_id(1)
        v = g_ref[0].astype(F32) + got_ref[0].astype(F32)
        sum_ref[0] = v.astype(BF16)

        @pl.when(s == pos_ref[1])
        def _():
            own_ref[...] = v

    return pl.pallas_call(
        body, name="pair_sum",
        grid_spec=pltpu.PrefetchScalarGridSpec(
            num_scalar_prefetch=1, grid=(n_i, ns),
            in_specs=[pl.BlockSpec((1, tr, c), lambda i, s, pos: (s, pos[0] * n_i + i, 0)),
                      pl.BlockSpec((1, tr, c), lambda i, s, pos: (s, i, 0))],
            out_specs=[pl.BlockSpec((1, tr, c), lambda i, s, pos: (s, i, 0)), pl.BlockSpec((tr, c), lambda i, s, pos: (i, 0))]),
        out_shape=[S((ns, r2, c), BF16), S((r2, c), F32)], compiler_params=_params("parallel", "arbitrary"),
    )(pos, g, got)


def chip_sum(own, p2, pos):
    r2, c = own.shape
    tr = _row_tile(r2)
    n_i = r2 // tr

    def body(pos_ref, own_ref, a_ref, b_ref, c_ref, o_ref):
        o_ref[...] = own_ref[...] + a_ref[0].astype(F32) + b_ref[0].astype(F32) + c_ref[0].astype(F32)

    peer = lambda j: pl.BlockSpec((1, tr, c), lambda i, pos: (j, i, 0))
    return pl.pallas_call(
        body, name="chip_sum",
        grid_spec=pltpu.PrefetchScalarGridSpec(
            num_scalar_prefetch=1, grid=(n_i,),
            in_specs=[pl.BlockSpec((tr, c), lambda i, pos: (i, 0)), peer(0), peer(1), peer(2)],
            out_specs=pl.BlockSpec((tr, c), lambda i, pos: (pos[0] * n_i + i, 0))),
        out_shape=S((2 * r2, c), F32), compiler_params=_params("parallel"),
    )(pos, own, p2, p2, p2)


def exchange_halves(rs):
    n = len(rs)

    def body(*refs):
        outs = refs[n:2 * n]
        send_sems, recv_sems = refs[2 * n:]
        x, y, c = _mesh_pos()
        copies = []
        for t in range(n):
            r2 = outs[t].shape[0] // 2
            rows = outs[t].at[pl.ds(pl.multiple_of(c * r2, 8), r2)]
            cp = _remote(rows, rows, send_sems.at[t], recv_sems.at[t], (x, y, 1 - c))
            cp.start()
            copies.append(cp)
        for cp in copies:
            cp.wait()

    return pl.pallas_call(
        body, name="exchange_halves", in_specs=[ANY] * n, out_specs=[ANY] * n,
        out_shape=[S(r.shape, r.dtype) for r in rs], input_output_aliases={t: t for t in range(n)},
        scratch_shapes=[pltpu.SemaphoreType.DMA((n,)), pltpu.SemaphoreType.DMA((n,))],
    )(*rs)


def allreduce_small(vec):
    R = vec.shape[0]
    H = R // 2

    def body(x_ref, o_ref, pair_buf, chip_buf, send_sems, recv_sems):
        x, y, c = _mesh_pos()
        me_s = 2 * x + y
        sibling = (x, y, 1 - c)
        mine = pl.ds(pl.multiple_of(c * H, 8), H)
        give = pl.ds(pl.multiple_of((1 - c) * H, 8), H)
        cp = _remote(x_ref.at[give], pair_buf, send_sems.at[0], recv_sems.at[0], sibling)
        cp.start()
        cp.wait()
        chip_buf[me_s] = x_ref[mine, :] + pair_buf[...]
        copies = []
        for j, (cx, cy) in enumerate(_other_chips(x, y)):
            cp = _remote(chip_buf.at[me_s], chip_buf.at[me_s], send_sems.at[1 + j], recv_sems.at[1 + j], (cx, cy, c))
            cp.start()
            copies.append(cp)
        for cp in copies:
            cp.wait()
        o_ref[mine, :] = (chip_buf[0] + chip_buf[1]) + (chip_buf[2] + chip_buf[3])
        cp = _remote(o_ref.at[mine], o_ref.at[mine], send_sems.at[4], recv_sems.at[4], sibling)
        cp.start()
        cp.wait()

    vm = pl.BlockSpec(memory_space=pltpu.VMEM)
    return pl.pallas_call(
        body, name="allreduce_small", in_specs=[vm], out_specs=vm, out_shape=S((R, 128), F32),
        scratch_shapes=[pltpu.VMEM((H, 128), F32), pltpu.VMEM((N_SHARD, H, 128), F32),
                        pltpu.SemaphoreType.DMA((5,)), pltpu.SemaphoreType.DMA((5,))],
        compiler_params=pltpu.CompilerParams(vmem_limit_bytes=VMEM_LIMIT_BYTES),
    )(vec)


def adamw_layer(w, g, m, v, layer, prev):
    _, r, c = w.shape
    tr = _row_tile(r)

    def body(w_ref, g_ref, m_ref, v_ref, *rest):
        outs = rest[-4:]
        g_val = g_ref[...]
        outs[0][0] = g_val
        outs[1][0], outs[2][0], outs[3][0] = _adamw(w_ref[0], g_val, m_ref[0], v_ref[0])

    lay = pl.BlockSpec((1, tr, c), lambda i: (layer, i, 0))
    prev = list(prev) if prev else []
    return pl.pallas_call(
        body, name="adamw_layer", grid=(r // tr,),
        in_specs=[lay, pl.BlockSpec((tr, c), lambda i: (i, 0)), lay, lay] + [ANY] * len(prev),
        out_specs=[lay] * 4, out_shape=[S(w.shape, F32)] * 4,
        input_output_aliases={4 + k: k for k in range(len(prev))}, compiler_params=_params("parallel"),
    )(w, g, m, v, *prev)


def reduce_begin(gs, pos, layer, after):
    got = reduce_pair(gs, after)
    sums, own = zip(*[pair_sum(g, o, pos) for g, o in zip(gs, got)])
    send_sems, recv_sems, sums, lands, token = chips_start(list(sums), layer)
    return dict(own=own, sums=sums, lands=lands, sems=(send_sems, recv_sems), token=token, layer=layer)


def reduce_end(pending, pos, after):
    lands = chips_wait(pending["sums"], pending["lands"], *pending["sems"], after, pending["layer"])
    return exchange_halves([chip_sum(o, p, pos) for o, p in zip(pending["own"], lands)])


W_NAMES = ("ffn1_norm", "ffn1_w_gate", "ffn1_w_up", "ffn1_w_down", "mix_norm", "w_in", "conv_w", "conv_b", "ssm_A_re", "ssm_A_im",
           "ssm_B_re", "ssm_B_im", "ssm_C_re", "ssm_C_im", "ssm_D", "ssm_log_dt", "glu_w", "glu_b", "conv_out_norm", "ssm_out_norm",
           "w_out", "ffn2_norm", "ffn2_w_gate", "ffn2_w_up", "ffn2_w_down", "ple_norm", "ple_w_gate", "ple_w_proj", "final_norm")
SMALL_ALL = SMALL + ("final_norm",)
TRANSPOSED = ("ffn1_w_gate", "ffn1_w_up", "ffn2_w_gate", "ffn2_w_up")
PACK = ROW_TILE_MAX * 128


def _pack(parts):
    flat = jnp.concatenate([p.reshape(-1) for p in parts])
    pad = (-flat.shape[0]) % PACK
    return jnp.pad(flat, (0, pad)).reshape(-1, 128)


def _unpack(vec, shapes):
    flat = vec.reshape(-1)
    out, off = [], 0
    for shp in shapes:
        size = math.prod(shp)
        out.append(flat[off:off + size].reshape(shp))
        off += size
    return out


def _step(a):
    a = {k: jnp.swapaxes(v, 1, 2) if k.removeprefix("m_").removeprefix("v_") in TRANSPOSED else v for k, v in a.items()}
    x, p, target = a["x"][0], a["p"][:, 0], a["loss_target"][0]
    depth = p.shape[0]
    L, D = x.shape
    me_s = 2 * lax.axis_index("x") + lax.axis_index("y")

    pos = jnp.stack([lax.axis_index("c"), me_s]).astype(jnp.int32)
    conv_w = gather_weights([cast_place(a["conv_w"], pos, F32)])[0]
    started = [gather_start([cast_place_layer(a[n], l, pos, BF16) for n in BIG], l, [conv_w]) for l in range(depth)]

    def weights_of(l, h):
        send_sems, recv_sems, bufs, _ = started[l]
        after = [h] if l else [s[3] for s in started]
        full = gather_forward(gather_wait(bufs, send_sems, recv_sems, after, l))
        w = {n: a[n][l] for n in SMALL if n != "conv_w"}
        w.update(dict(zip(BIG, full)))
        C = w["glu_w"].shape[-1]
        w["glu_w"] = w["glu_w"].reshape(C, C)
        w["w_out"] = w["w_out"].reshape(2, -1, D)
        w["ple_w_gate"] = w["ple_w_gate"].reshape(D, D)
        w["conv_w"] = conv_w[l].transpose(1, 0, 2).reshape(3, -1)
        return w

    pending, first_layer_grads = {}, []

    def on_grads(l, big, dh):
        if l == 0:
            first_layer_grads.extend(big)
            return None
        pending[l] = reduce_begin(big, pos, l, [])
        return pending[l]["token"]

    loss_part, dx, smalls, g_final = local_step(x, p, target, a["final_norm"], weights_of, on_grads)
    small_shapes = [(depth,) + smalls[0][n].shape for n in SMALL] + [g_final.shape, (1,)]
    parts = [smalls[l][n] for n in SMALL for l in range(depth)] + [g_final, loss_part[0, 0:1]]
    summed_vec = allreduce_small(_pack(parts))
    pending[0] = reduce_begin(first_layer_grads, pos, 0, [summed_vec])
    stacked = [None] * len(BIG)
    for l in reversed(range(depth)):
        after = [pending[0]["token"]] if l else [s[3] for s in stacked]
        reduced = reduce_end(pending[l], pos, after)
        stacked = [adamw_layer(a[n], reduced[i], a["m_" + n], a["v_" + n], l, stacked[i]) for i, n in enumerate(BIG)]
    big_out = {n: [jnp.swapaxes(o, 1, 2) for o in outs] if n in TRANSPOSED else outs for n, outs in zip(BIG, stacked)}

    summed = _unpack(summed_vec, small_shapes)
    g_small = dict(zip(SMALL_ALL, summed[:-1]))
    loss = summed[-1][0]
    n_conv = a["conv_w"].shape[-1]
    g_small["conv_w"] = lax.dynamic_slice_in_dim(g_small["conv_w"], me_s * n_conv, n_conv, axis=2)
    g_small = {n: g_small[n].reshape(a[n].shape) for n in SMALL_ALL}
    packed = [_pack([src[n] for n in SMALL_ALL]) for src in
              ({n: a[n] for n in SMALL_ALL}, g_small, {n: a["m_" + n] for n in SMALL_ALL}, {n: a["v_" + n] for n in SMALL_ALL})]
    shapes = [a[n].shape for n in SMALL_ALL]
    d_s, m_s, v_s = [dict(zip(SMALL_ALL, _unpack(o, shapes))) for o in elementwise(_adamw, packed, [F32, F32, F32], "adamw_small")]

    outs = {n: big_out[n] if n in big_out else (g_small[n], d_s[n], m_s[n], v_s[n]) for n in W_NAMES}
    return (loss, dx[None], *[outs[n][0] for n in W_NAMES], *[outs[n][1] for n in W_NAMES],
            *[outs[n][2] for n in W_NAMES], *[outs[n][3] for n in W_NAMES])


def kernel(x, p, ffn1_norm, ffn1_w_gate, ffn1_w_up, ffn1_w_down, mix_norm, w_in, conv_w, conv_b, ssm_A_re, ssm_A_im, ssm_B_re, ssm_B_im, ssm_C_re, ssm_C_im, ssm_D, ssm_log_dt, glu_w, glu_b, conv_out_norm, ssm_out_norm, w_out, ffn2_norm, ffn2_w_gate, ffn2_w_up, ffn2_w_down, ple_norm, ple_w_gate, ple_w_proj, final_norm, loss_target, m_ffn1_norm, m_ffn1_w_gate, m_ffn1_w_up, m_ffn1_w_down, m_mix_norm, m_w_in, m_conv_w, m_conv_b, m_ssm_A_re, m_ssm_A_im, m_ssm_B_re, m_ssm_B_im, m_ssm_C_re, m_ssm_C_im, m_ssm_D, m_ssm_log_dt, m_glu_w, m_glu_b, m_conv_out_norm, m_ssm_out_norm, m_w_out, m_ffn2_norm, m_ffn2_w_gate, m_ffn2_w_up, m_ffn2_w_down, m_ple_norm, m_ple_w_gate, m_ple_w_proj, m_final_norm, v_ffn1_norm, v_ffn1_w_gate, v_ffn1_w_up, v_ffn1_w_down, v_mix_norm, v_w_in, v_conv_w, v_conv_b, v_ssm_A_re, v_ssm_A_im, v_ssm_B_re, v_ssm_B_im, v_ssm_C_re, v_ssm_C_im, v_ssm_D, v_ssm_log_dt, v_glu_w, v_glu_b, v_conv_out_norm, v_ssm_out_norm, v_w_out, v_ffn2_norm, v_ffn2_w_gate, v_ffn2_w_up, v_ffn2_w_down, v_ple_norm, v_ple_w_gate, v_ple_w_proj, v_final_norm):
    return _step(dict(locals()))
```

```python
import functools
import math

import jax
import jax.numpy as jnp
from jax import lax
from jax.experimental import pallas as pl
from jax.experimental.pallas import tpu as pltpu

F32, BF16 = jnp.float32, jnp.bfloat16
S = jax.ShapeDtypeStruct
EPS = 1e-6
N_SEG = 8
N_SHARD = 4
N_DEV = 8
VMEM_LIMIT_BYTES = 56 * 1024 * 1024
ADAM_LR, ADAM_B1, ADAM_B2, ADAM_EPS, ADAM_WD, ADAM_STEP = 0.001, 0.9, 0.999, 1e-08, 0.01, 10
MESH = pl.DeviceIdType.MESH


def _params(*sem):
    return pltpu.CompilerParams(dimension_semantics=sem if sem else None, vmem_limit_bytes=VMEM_LIMIT_BYTES)


def _dot(a, b, ca, cb):
    return lax.dot_general(a, b, (((ca,), (cb,)), ((), ())), preferred_element_type=F32)


def _sigmoid(x):
    return 1.0 / (1.0 + jnp.exp(-x))


def _rstd(x):
    return lax.rsqrt(jnp.mean(x * x, axis=-1, keepdims=True) + EPS)


def _rms_bwd(x, g, dy):
    r = _rstd(x)
    xh = x * r
    dxh = dy * g
    dx = r * (dxh - xh * jnp.mean(dxh * xh, axis=-1, keepdims=True))
    return dx, jnp.sum(dy * xh, axis=0, keepdims=True)


def _tile(n, want):
    return want if n % want == 0 else n


def rmsnorm_fwd(h, g):
    L, D = h.shape
    tm = _tile(L, 512)

    def body(h_ref, g_ref, o_ref):
        x = h_ref[...]
        o_ref[...] = (x * _rstd(x) * g_ref[...]).astype(BF16)

    return pl.pallas_call(
        body, name="rmsnorm_fwd", grid=(L // tm,),
        in_specs=[pl.BlockSpec((tm, D), lambda m: (m, 0)), pl.BlockSpec((1, D), lambda m: (0, 0))],
        out_specs=pl.BlockSpec((tm, D), lambda m: (m, 0)),
        out_shape=S((L, D), BF16), compiler_params=_params("parallel"),
    )(h, g.reshape(1, D))


def ffn_up(u, wg, wu):
    L, D = u.shape
    ns, F, _ = wg.shape
    tm = _tile(L, 512)

    def body(u_ref, wg_ref, wu_ref, a_ref, b_ref, s_ref):
        x = u_ref[...]
        a = _dot(x, wg_ref[0], 1, 1)
        b = _dot(x, wu_ref[0], 1, 1)
        a_ref[0] = a.astype(BF16)
        b_ref[0] = b.astype(BF16)
        s_ref[0] = (a * _sigmoid(a) * b).astype(BF16)

    w_spec = pl.BlockSpec((1, F, D), lambda s, m: (s, 0, 0))
    o_spec = pl.BlockSpec((1, tm, F), lambda s, m: (s, m, 0))
    return pl.pallas_call(
        body, name="ffn_up", grid=(ns, L // tm),
        in_specs=[pl.BlockSpec((tm, D), lambda s, m: (m, 0)), w_spec, w_spec],
        out_specs=[o_spec, o_spec, o_spec],
        out_shape=[S((ns, L, F), BF16)] * 3, compiler_params=_params("parallel", "parallel"),
    )(u, wg, wu)


def mm_shard_n(u, w3, out_dtype):
    L, K = u.shape
    ns, _, N = w3.shape
    tm = _tile(L, 512)

    def body(u_ref, w_ref, o_ref):
        o_ref[0] = _dot(u_ref[...], w_ref[0], 1, 0).astype(out_dtype)

    return pl.pallas_call(
        body, name="mm_shard_n", grid=(ns, L // tm),
        in_specs=[pl.BlockSpec((tm, K), lambda s, m: (m, 0)), pl.BlockSpec((1, K, N), lambda s, m: (s, 0, 0))],
        out_specs=pl.BlockSpec((1, tm, N), lambda s, m: (s, m, 0)),
        out_shape=S((ns, L, N), out_dtype), compiler_params=_params("parallel", "parallel"),
    )(u, w3)


def mm_shard_k(a3, w3, res, scale, g_next):
    nk, L, Kc = a3.shape
    N = w3.shape[2]
    tm = _tile(L, 512)

    def body(a_ref, w_ref, r_ref, g_ref, o_ref, u_ref):
        acc = _dot(a_ref[0], w_ref[0], 1, 0)
        for k in range(1, nk):
            acc += _dot(a_ref[k], w_ref[k], 1, 0)
        h = r_ref[...] + scale * acc
        o_ref[...] = h
        u_ref[...] = (h * _rstd(h) * g_ref[...]).astype(BF16)

    tile = pl.BlockSpec((tm, N), lambda m: (m, 0))
    return pl.pallas_call(
        body, name="mm_shard_k", grid=(L // tm,),
        in_specs=[pl.BlockSpec((nk, tm, Kc), lambda m: (0, m, 0)), pl.BlockSpec((nk, Kc, N), lambda m: (0, 0, 0)),
                  tile, pl.BlockSpec((1, N), lambda m: (0, 0))],
        out_specs=[tile, tile],
        out_shape=[S((L, N), F32), S((L, N), BF16)], compiler_params=_params("parallel"),
    )(a3, w3, res, g_next.reshape(1, N))


CONV_HALO = 8


def _conv_specs(L, tm, C, shard):
    nb = L // CONV_HALO
    per = tm // CONV_HALO
    main = pl.BlockSpec((1, tm, C), lambda m: (shard, m, 0))
    prev = pl.BlockSpec((1, CONV_HALO, C), lambda m: (shard, jnp.maximum(m * per - 1, 0), 0))
    nxt = pl.BlockSpec((1, CONV_HALO, C), lambda m: (shard, jnp.minimum((m + 1) * per, nb - 1), 0))
    return main, prev, nxt


def _conv_core(zb, zc, zv, w_ref, bias, grow, L):
    valid = (grow >= 0) & (grow < L)
    v = jnp.where(valid, zc * zv, 0.0)
    v1 = pltpu.roll(v, 1, 0)
    v2 = pltpu.roll(v, 2, 0)
    cb = w_ref[0:1, :] * v2 + w_ref[1:2, :] * v1 + w_ref[2:3, :] * v + bias
    return valid, v, v1, v2, cb, zb * cb


def conv_fwd(z, conv_w, conv_b, gnorm):
    _, L, C = z.shape
    tm = _tile(L, 256)
    H = CONV_HALO

    def body(zb_ref, zc_ref, zcp_ref, zv_ref, zvp_ref, w_ref, b_ref, g_ref, o_ref):
        m = pl.program_id(0)
        zc = jnp.concatenate([zcp_ref[0], zc_ref[0]], axis=0)
        zv = jnp.concatenate([zvp_ref[0], zv_ref[0]], axis=0)
        grow = m * tm - H + lax.broadcasted_iota(jnp.int32, (tm + H, C), 0)
        valid = grow >= 0
        v = jnp.where(valid, zc * zv, 0.0)
        v1 = pltpu.roll(v, 1, 0)
        v2 = pltpu.roll(v, 2, 0)
        cb = (w_ref[0:1, :] * v2 + w_ref[1:2, :] * v1 + w_ref[2:3, :] * v + b_ref[...])[H:, :]
        ya = zb_ref[0] * cb
        o_ref[...] = (ya * _rstd(ya) * g_ref[...]).astype(BF16)

    zb_m, _, _ = _conv_specs(L, tm, C, 0)
    zc_m, zc_p, _ = _conv_specs(L, tm, C, 1)
    zv_m, zv_p, _ = _conv_specs(L, tm, C, 2)
    row = lambda r: pl.BlockSpec((r, C), lambda m: (0, 0))
    return pl.pallas_call(
        body, name="conv_fwd", grid=(L // tm,),
        in_specs=[zb_m, zc_m, zc_p, zv_m, zv_p, row(3), row(1), row(1)],
        out_specs=pl.BlockSpec((tm, C), lambda m: (m, 0)),
        out_shape=S((L, C), BF16), compiler_params=_params("parallel"),
    )(z, z, z, z, z, conv_w, conv_b.reshape(1, C), gnorm.reshape(1, C))


def _cmul(ar, ai, br, bi):
    return ar * br - ai * bi, ar * bi + ai * br


def _scan_fwd(hr_ref, hi_ref, lr, li, n_steps):
    W = hr_ref.shape[1]
    zero = jnp.zeros((N_SEG, W), F32)

    def local(t, c):
        r = pl.multiple_of(t * N_SEG, N_SEG)
        pr, pi = _cmul(lr, li, c[0], c[1])
        nr = pr + hr_ref[pl.ds(r, N_SEG), :]
        ni = pi + hi_ref[pl.ds(r, N_SEG), :]
        hr_ref[pl.ds(r, N_SEG), :] = nr
        hi_ref[pl.ds(r, N_SEG), :] = ni
        return nr, ni

    fr, fi = lax.fori_loop(0, n_steps, local, (zero, zero))
    qr, qi = _cpow(lr, li, n_steps)
    row = lax.broadcasted_iota(jnp.int32, (N_SEG, W), 0)
    cr, ci = zero, zero
    for seg in range(1, N_SEG):
        tr, ti = _cmul(qr, qi, cr, ci)
        sr = pltpu.roll(fr + tr, 1, 0)
        si = pltpu.roll(fi + ti, 1, 0)
        cr = jnp.where(row == seg, sr, cr)
        ci = jnp.where(row == seg, si, ci)

    def fix(t, c):
        r = pl.multiple_of(t * N_SEG, N_SEG)
        pr, pi = _cmul(lr, li, c[0], c[1])
        ar, ai = _cmul(pr, pi, cr, ci)
        hr_ref[pl.ds(r, N_SEG), :] += ar
        hi_ref[pl.ds(r, N_SEG), :] += ai
        return pr, pi

    lax.fori_loop(0, n_steps, fix, (jnp.ones((N_SEG, W), F32), zero))


def _cpow(lr, li, n):
    rr, ri = None, None
    br, bi = lr, li
    while n:
        if n & 1:
            rr, ri = (br, bi) if rr is None else _cmul(rr, ri, br, bi)
        n >>= 1
        if n:
            br, bi = _cmul(br, bi, br, bi)
    return rr, ri


def _ssm_specs(L):
    col = lambda w: pl.BlockSpec((L, w), lambda j: (0, j))
    return dict(
        u=col(128), lam=pl.BlockSpec((2, 512), lambda j: (0, j)),
        bmat=pl.BlockSpec((1, 128, 512), lambda j: (j, 0, 0)), cmat=pl.BlockSpec((1, 512, 128), lambda j: (j, 0, 0)),
        d=pl.BlockSpec((1, 128), lambda j: (0, j)))


def ssm_fwd(us, lam, bre, bim, cre, cim, dvec):
    L = us.shape[0]
    n_steps = L // N_SEG
    sp = _ssm_specs(L)

    def body(u_ref, lam_ref, bre_ref, bim_ref, cre_ref, cim_ref, d_ref, y_ref, hr, hi):
        u = u_ref[...]
        ub = u.astype(BF16)
        hr[...] = _dot(ub, bre_ref[0], 1, 0)
        hi[...] = _dot(ub, bim_ref[0], 1, 0)
        lr = jnp.broadcast_to(lam_ref[0:1, :], (N_SEG, 512))
        li = jnp.broadcast_to(lam_ref[1:2, :], (N_SEG, 512))
        _scan_fwd(hr, hi, lr, li, n_steps)
        y_ref[...] = (_dot(hr[...].astype(BF16), cre_ref[0], 1, 0) - _dot(hi[...].astype(BF16), cim_ref[0], 1, 0)
                      + d_ref[...] * u)

    return pl.pallas_call(
        body, name="ssm_fwd", grid=(4,),
        in_specs=[sp["u"], sp["lam"], sp["bmat"], sp["bmat"], sp["cmat"], sp["cmat"], sp["d"]],
        out_specs=sp["u"], out_shape=S((L, 512), F32),
        scratch_shapes=[pltpu.VMEM((L, 512), F32), pltpu.VMEM((L, 512), F32)],
        compiler_params=_params("parallel"),
    )(us, lam, bre, bim, cre, cim, dvec)


_GELU_C = math.sqrt(2.0 / math.pi)


def _gelu(y):
    t = jnp.tanh(_GELU_C * (y + 0.044715 * y * y * y))
    return 0.5 * y * (1.0 + t), t


def glu_fwd(y, w, b, gnorm):
    L, C = y.shape
    tm = _tile(L, 512)

    def body(y_ref, w_ref, b_ref, g_ref, o_ref):
        zg, _ = _gelu(y_ref[...])
        out = zg * _sigmoid(_dot(zg.astype(BF16), w_ref[...], 1, 0) + b_ref[...])
        o_ref[...] = (out * _rstd(out) * g_ref[...]).astype(BF16)

    row = pl.BlockSpec((1, C), lambda m: (0, 0))
    return pl.pallas_call(
        body, name="glu_fwd", grid=(L // tm,),
        in_specs=[pl.BlockSpec((tm, C), lambda m: (m, 0)), pl.BlockSpec((C, C), lambda m: (0, 0)), row, row],
        out_specs=pl.BlockSpec((tm, C), lambda m: (m, 0)),
        out_shape=S((L, C), BF16), compiler_params=_params("parallel"),
    )(y, w, b.reshape(1, C), gnorm.reshape(1, C))


def _ple_specs(L, D, P, tm, nb):
    return [pl.BlockSpec((tm, D), lambda n, m: (m, 0)), pl.BlockSpec((tm, P), lambda n, m: (m, 0)),
            pl.BlockSpec((D, nb), lambda n, m: (0, n)), pl.BlockSpec((1, P, nb), lambda n, m: (n, 0, 0)),
            pl.BlockSpec((tm, nb), lambda n, m: (m, n))]


def ple_fwd(un, pb, wpg, wpp, h, token):
    L, D = un.shape
    ns, P, nb = wpp.shape
    tm = _tile(L, 512)

    def body(un_ref, p_ref, wg_ref, wp_ref, h_ref, tok_ref, o_ref):
        gate = _sigmoid(_dot(un_ref[...], wg_ref[...], 1, 0))
        o_ref[...] = h_ref[...] + tok_ref[0:1, 0:1] + _dot(p_ref[...], wp_ref[0], 1, 0) * gate

    return pl.pallas_call(
        body, name="ple_fwd", grid=(ns, L // tm),
        in_specs=_ple_specs(L, D, P, tm, nb) + [pl.BlockSpec((8, 128), lambda n, m: (0, 0))],
        out_specs=pl.BlockSpec((tm, nb), lambda n, m: (m, n)),
        out_shape=S((L, D), F32), compiler_params=_params("parallel", "parallel"),
    )(un, pb, wpg, wpp, h, token)


def loss_head(h, g, target):
    L, D = h.shape
    tm = _tile(L, 256)

    def body(h_ref, g_ref, t_ref, loss_ref, dh_ref, dg_ref):
        m = pl.program_id(0)
        x = h_ref[...]
        gg = g_ref[...]
        e = x * _rstd(x) * gg - t_ref[...]
        dx, dg = _rms_bwd(x, gg, e * (1.0 / D))
        dh_ref[...] = dx
        part = jnp.full((8, 128), 0.5 / D, F32) * jnp.sum(e * e)

        @pl.when(m == 0)
        def _():
            loss_ref[...] = part
            dg_ref[...] = dg

        @pl.when(m > 0)
        def _():
            loss_ref[...] += part
            dg_ref[...] += dg

    return pl.pallas_call(
        body, name="loss_head", grid=(L // tm,),
        in_specs=[pl.BlockSpec((tm, D), lambda m: (m, 0)), pl.BlockSpec((1, D), lambda m: (0, 0)),
                  pl.BlockSpec((tm, D), lambda m: (m, 0))],
        out_specs=[pl.BlockSpec((8, 128), lambda m: (0, 0)), pl.BlockSpec((tm, D), lambda m: (m, 0)),
                   pl.BlockSpec((1, D), lambda m: (0, 0))],
        out_shape=[S((8, 128), F32), S((L, D), F32), S((1, D), F32)],
        compiler_params=_params("arbitrary"),
    )(h, g.reshape(1, D), target)


def ple_bwd(un, pb, wpg, wpp, dh, token):
    L, D = un.shape
    ns, P, nb = wpp.shape
    tm = _tile(L, 512)

    def body(un_ref, p_ref, wg_ref, wp_ref, dh_ref, tok_ref, dpre_ref, dpp_ref):
        gate = _sigmoid(_dot(un_ref[...], wg_ref[...], 1, 0))
        pp = _dot(p_ref[...], wp_ref[0], 1, 0)
        d = dh_ref[...] + tok_ref[0:1, 0:1]
        dpp_ref[0] = (d * gate).astype(BF16)
        dpre_ref[...] = (d * pp * gate * (1.0 - gate)).astype(BF16)

    return pl.pallas_call(
        body, name="ple_bwd", grid=(ns, L // tm),
        in_specs=_ple_specs(L, D, P, tm, nb) + [pl.BlockSpec((8, 128), lambda n, m: (0, 0))],
        out_specs=[pl.BlockSpec((tm, nb), lambda n, m: (m, n)), pl.BlockSpec((1, tm, nb), lambda n, m: (n, m, 0))],
        out_shape=[S((L, D), BF16), S((ns, L, nb), BF16)], compiler_params=_params("parallel", "parallel"),
    )(un, pb, wpg, wpp, dh, token)


def wgrad(a, b):
    a3 = a if a.ndim == 3 else a[None]
    b3 = b if b.ndim == 3 else b[None]
    ns = max(a3.shape[0], b3.shape[0])
    _, L, Ka = a3.shape
    N = b3.shape[2]
    a_map = (lambda s: (s, 0, 0)) if a3.shape[0] > 1 else (lambda s: (0, 0, 0))
    b_map = (lambda s: (s, 0, 0)) if b3.shape[0] > 1 else (lambda s: (0, 0, 0))

    def body(a_ref, b_ref, o_ref):
        o_ref[0] = _dot(a_ref[0], b_ref[0], 0, 0).astype(BF16)

    return pl.pallas_call(
        body, name="wgrad", grid=(ns,),
        in_specs=[pl.BlockSpec((1, L, Ka), a_map), pl.BlockSpec((1, L, N), b_map)],
        out_specs=pl.BlockSpec((1, Ka, N), lambda s: (s, 0, 0)),
        out_shape=S((ns, Ka, N), BF16), compiler_params=_params("parallel"),
    )(a3, b3)


def dx_rms(pairs, h, g, dh_in, cast_scale):
    L, D = h.shape
    nk = pairs[0][0].shape[0]
    n_pairs = len(pairs)
    tm = _tile(L, 512)
    n_m = L // tm
    w_dims = [0 if transposed else 1 for _, _, transposed in pairs]

    def body(*refs):
        ins, (h_ref, g_ref, dhi_ref, dho_ref, dhb_ref, dg_ref) = refs[:2 * n_pairs], refs[2 * n_pairs:]
        m = pl.program_id(0)
        acc = None
        for i in range(n_pairs):
            for k in range(nk):
                part = _dot(ins[2 * i][k], ins[2 * i + 1][k], 1, w_dims[i])
                acc = part if acc is None else acc + part
        dx, dg = _rms_bwd(h_ref[...], g_ref[...], acc)
        dh_out = dhi_ref[...] + dx
        dho_ref[...] = dh_out
        dhb_ref[...] = (cast_scale * dh_out).astype(BF16)

        @pl.when(m == 0)
        def _():
            dg_ref[...] = dg

        @pl.when(m > 0)
        def _():
            dg_ref[...] += dg

    in_specs, args = [], []
    for a3, w3, _ in pairs:
        Kc = a3.shape[2]
        in_specs += [pl.BlockSpec((nk, tm, Kc), lambda m: (0, m, 0)),
                     pl.BlockSpec(w3.shape, lambda m: (0, 0, 0), pipeline_mode=pl.Buffered(1))]
        args += [a3, w3]
    tile = pl.BlockSpec((tm, D), lambda m: (m, 0))
    row = pl.BlockSpec((1, D), lambda m: (0, 0))
    return pl.pallas_call(
        body, name="dx_rms", grid=(n_m,),
        in_specs=in_specs + [tile, row, tile], out_specs=[tile, tile, row],
        out_shape=[S((L, D), F32), S((L, D), BF16), S((1, D), F32)], compiler_params=_params("arbitrary"),
    )(*args, h, g.reshape(1, D), dh_in)


def dact_plain(dhb, w3):
    L, D = dhb.shape
    ns, N, _ = w3.shape
    tm = _tile(L, 512)

    def body(d_ref, w_ref, o_ref):
        o_ref[0] = _dot(d_ref[...], w_ref[0], 1, 1)

    return pl.pallas_call(
        body, name="dact_plain", grid=(ns, L // tm),
        in_specs=[pl.BlockSpec((tm, D), lambda s, m: (m, 0)), pl.BlockSpec((1, N, D), lambda s, m: (s, 0, 0))],
        out_specs=pl.BlockSpec((1, tm, N), lambda s, m: (s, m, 0)),
        out_shape=S((ns, L, N), F32), compiler_params=_params("parallel", "parallel"),
    )(dhb, w3)


def dact_swiglu(dhb, wd, a3, b3):
    L, D = dhb.shape
    ns, F, _ = wd.shape
    tm = _tile(L, 512)

    def body(d_ref, w_ref, a_ref, b_ref, da_ref, db_ref):
        ds = _dot(d_ref[...], w_ref[0], 1, 1)
        a = a_ref[0].astype(F32)
        b = b_ref[0].astype(F32)
        sg = _sigmoid(a)
        da_ref[0] = (ds * b * (sg * (1.0 + a * (1.0 - sg)))).astype(BF16)
        db_ref[0] = (ds * (a * sg)).astype(BF16)

    t_spec = pl.BlockSpec((1, tm, F), lambda s, m: (s, m, 0))
    return pl.pallas_call(
        body, name="dact_swiglu", grid=(ns, L // tm),
        in_specs=[pl.BlockSpec((tm, D), lambda s, m: (m, 0)), pl.BlockSpec((1, F, D), lambda s, m: (s, 0, 0)), t_spec, t_spec],
        out_specs=[t_spec, t_spec], out_shape=[S((ns, L, F), BF16)] * 2,
        compiler_params=_params("parallel", "parallel"),
    )(dhb, wd, a3, b3)


def conv_bwd(z, conv_w, conv_b, gnorm, dyn):
    _, L, C = z.shape
    tm = _tile(L, 256)
    H = CONV_HALO
    T = tm + 2 * H

    def body(zb_ref, zbp_ref, zbn_ref, zc_ref, zcp_ref, zcn_ref, zv_ref, zvp_ref, zvn_ref, d_ref, dp_ref, dn_ref,
             w_ref, b_ref, g_ref, dz_ref, dw_ref, db_ref, dg_ref):
        m = pl.program_id(0)
        cat = lambda p, c, n: jnp.concatenate([p[0], c[0], n[0]], axis=0)
        zb, zc, zv, d = cat(zbp_ref, zb_ref, zbn_ref), cat(zcp_ref, zc_ref, zcn_ref), cat(zvp_ref, zv_ref, zvn_ref), cat(dp_ref, d_ref, dn_ref)
        grow = m * tm - H + lax.broadcasted_iota(jnp.int32, (T, C), 0)
        valid, v, v1, v2, cb, ya = _conv_core(zb, zc, zv, w_ref, b_ref[...], grow, L)
        dya, _ = _rms_bwd(ya, g_ref[...], d)
        dc = jnp.where(valid, dya * zb, 0.0)
        dv = w_ref[2:3, :] * dc + w_ref[1:2, :] * pltpu.roll(dc, T - 1, 0) + w_ref[0:1, :] * pltpu.roll(dc, T - 2, 0)
        dz_ref[0] = (dya * cb)[H:H + tm, :].astype(BF16)
        dz_ref[1] = (dv * zv)[H:H + tm, :].astype(BF16)
        dz_ref[2] = (dv * zc)[H:H + tm, :].astype(BF16)
        rs = lambda x: jnp.sum(x[H:H + tm, :], axis=0, keepdims=True)
        yh = ya * _rstd(ya)
        dw = jnp.concatenate([rs(dc * v2), rs(dc * v1), rs(dc * v)], axis=0)
        dbias, dg = rs(dc), rs(d * yh)

        @pl.when(m == 0)
        def _():
            dw_ref[...] = dw
            db_ref[...] = dbias
            dg_ref[...] = dg

        @pl.when(m > 0)
        def _():
            dw_ref[...] += dw
            db_ref[...] += dbias
            dg_ref[...] += dg

    row = lambda r: pl.BlockSpec((r, C), lambda m: (0, 0))
    specs = [*_conv_specs(L, tm, C, 0), *_conv_specs(L, tm, C, 1), *_conv_specs(L, tm, C, 2), *_conv_specs(L, tm, C, 0)]
    return pl.pallas_call(
        body, name="conv_bwd", grid=(L // tm,),
        in_specs=specs + [row(3), row(1), row(1)],
        out_specs=[pl.BlockSpec((3, tm, C), lambda m: (0, m, 0)), row(3), row(1), row(1)],
        out_shape=[S((3, L, C), BF16), S((3, C), F32), S((1, C), F32), S((1, C), F32)],
        compiler_params=_params("arbitrary"),
    )(z, z, z, z, z, z, z, z, z, dyn, dyn, dyn, conv_w, conv_b.reshape(1, C), gnorm.reshape(1, C))


def glu_bwd(y, w, b, gnorm, dn):
    L, C = y.shape
    tm = _tile(L, 256)

    def body(y_ref, w_ref, b_ref, g_ref, d_ref, dy_ref, dpre_ref, zg_ref, db_ref, dg_ref):
        m = pl.program_id(0)
        yv = y_ref[...]
        zg, t = _gelu(yv)
        zgb = zg.astype(BF16)
        sg = _sigmoid(_dot(zgb, w_ref[...], 1, 0) + b_ref[...])
        out = zg * sg
        dout, dg = _rms_bwd(out, g_ref[...], d_ref[...])
        dpre = dout * zg * sg * (1.0 - sg)
        dpre_b = dpre.astype(BF16)
        dzg = dout * sg + _dot(dpre_b, w_ref[...], 1, 1)
        dt = (1.0 - t * t) * _GELU_C * (1.0 + 3.0 * 0.044715 * yv * yv)
        dy_ref[...] = dzg * (0.5 * (1.0 + t) + 0.5 * yv * dt)
        dpre_ref[...] = dpre_b
        zg_ref[...] = zgb
        dbias = jnp.sum(dpre, axis=0, keepdims=True)

        @pl.when(m == 0)
        def _():
            db_ref[...] = dbias
            dg_ref[...] = dg

        @pl.when(m > 0)
        def _():
            db_ref[...] += dbias
            dg_ref[...] += dg

    tile = pl.BlockSpec((tm, C), lambda m: (m, 0))
    row = pl.BlockSpec((1, C), lambda m: (0, 0))
    return pl.pallas_call(
        body, name="glu_bwd", grid=(L // tm,),
        in_specs=[tile, pl.BlockSpec((C, C), lambda m: (0, 0)), row, row, tile],
        out_specs=[tile, tile, tile, row, row],
        out_shape=[S((L, C), F32), S((L, C), BF16), S((L, C), BF16), S((1, C), F32), S((1, C), F32)],
        compiler_params=_params("arbitrary"),
    )(y, w, b.reshape(1, C), gnorm.reshape(1, C), dn)


def _scan_bwd(gr_ref, gi_ref, hr_ref, hi_ref, lr, li, n_steps):
    W = gr_ref.shape[1]
    zero = jnp.zeros((N_SEG, W), F32)
    lic = -li

    def local(i, c):
        r = pl.multiple_of((n_steps - 1 - i) * N_SEG, N_SEG)
        pr, pi = _cmul(lr, lic, c[0], c[1])
        nr = pr + gr_ref[pl.ds(r, N_SEG), :]
        ni = pi + gi_ref[pl.ds(r, N_SEG), :]
        gr_ref[pl.ds(r, N_SEG), :] = nr
        gi_ref[pl.ds(r, N_SEG), :] = ni
        return nr, ni

    fr, fi = lax.fori_loop(0, n_steps, local, (zero, zero))
    qr, qi = _cpow(lr, lic, n_steps)
    row = lax.broadcasted_iota(jnp.int32, (N_SEG, W), 0)
    cr, ci = zero, zero
    for seg in range(N_SEG - 2, -1, -1):
        tr, ti = _cmul(qr, qi, cr, ci)
        sr = pltpu.roll(fr + tr, N_SEG - 1, 0)
        si = pltpu.roll(fi + ti, N_SEG - 1, 0)
        cr = jnp.where(row == seg, sr, cr)
        ci = jnp.where(row == seg, si, ci)

    def fix(i, c):
        pwr, pwi, ar, ai = c
        t = n_steps - 1 - i
        r = pl.multiple_of(t * N_SEG, N_SEG)
        pwr, pwi = _cmul(lr, lic, pwr, pwi)
        xr, xi = _cmul(pwr, pwi, cr, ci)
        g_r = gr_ref[pl.ds(r, N_SEG), :] + xr
        g_i = gi_ref[pl.ds(r, N_SEG), :] + xi
        gr_ref[pl.ds(r, N_SEG), :] = g_r
        gi_ref[pl.ds(r, N_SEG), :] = g_i
        rp = pl.multiple_of(jnp.maximum(t - 1, 0) * N_SEG, N_SEG)
        hpr = hr_ref[pl.ds(rp, N_SEG), :]
        hpi = hi_ref[pl.ds(rp, N_SEG), :]
        live = t > 0
        ar = ar + jnp.where(live, hpr * g_r + hpi * g_i, 0.0)
        ai = ai + jnp.where(live, hpr * g_i - hpi * g_r, 0.0)
        return pwr, pwi, ar, ai

    _, _, ar, ai = lax.fori_loop(0, n_steps, fix, (jnp.ones((N_SEG, W), F32), zero, zero, zero))
    last = pl.ds((n_steps - 1) * N_SEG, N_SEG)
    hpr = jnp.where(row == 0, 0.0, pltpu.roll(hr_ref[last, :], 1, 0))
    hpi = jnp.where(row == 0, 0.0, pltpu.roll(hi_ref[last, :], 1, 0))
    g_r, g_i = gr_ref[pl.ds(0, N_SEG), :], gi_ref[pl.ds(0, N_SEG), :]
    ar = ar + hpr * g_r + hpi * g_i
    ai = ai + hpr * g_i - hpi * g_r
    return jnp.sum(ar, axis=0, keepdims=True), jnp.sum(ai, axis=0, keepdims=True)


def ssm_bwd(us, dy, lam, bre, bim, cre, cim, dvec):
    L = us.shape[0]
    n_steps = L // N_SEG
    sp = _ssm_specs(L)

    def body(u_ref, dy_ref, lam_ref, bre_ref, bim_ref, cre_ref, cim_ref, d_ref,
             du_ref, dlam_ref, dbre_ref, dbim_ref, dcre_ref, dcim_ref, dd_ref, hr, hi, gr, gi):
        u = u_ref[...]
        ub = u.astype(BF16)
        dyv = dy_ref[...]
        dyb = dyv.astype(BF16)
        hr[...] = _dot(ub, bre_ref[0], 1, 0)
        hi[...] = _dot(ub, bim_ref[0], 1, 0)
        lr = jnp.broadcast_to(lam_ref[0:1, :], (N_SEG, 512))
        li = jnp.broadcast_to(lam_ref[1:2, :], (N_SEG, 512))
        _scan_fwd(hr, hi, lr, li, n_steps)
        dcre_ref[0] = _dot(hr[...].astype(BF16), dyb, 0, 0)
        dcim_ref[0] = -_dot(hi[...].astype(BF16), dyb, 0, 0)
        gr[...] = _dot(dyb, cre_ref[0], 1, 1)
        gi[...] = -_dot(dyb, cim_ref[0], 1, 1)
        dlr, dli = _scan_bwd(gr, gi, hr, hi, lr, li, n_steps)
        dlam_ref[...] = jnp.concatenate([dlr, dli], axis=0)
        grb, gib = gr[...].astype(BF16), gi[...].astype(BF16)
        du_ref[...] = _dot(grb, bre_ref[0], 1, 1) + _dot(gib, bim_ref[0], 1, 1) + d_ref[...] * dyv
        dbre_ref[0] = _dot(ub, grb, 0, 0)
        dbim_ref[0] = _dot(ub, gib, 0, 0)
        dd_ref[...] = jnp.sum(dyv * u, axis=0, keepdims=True)

    big = pltpu.VMEM((L, 512), F32)
    return pl.pallas_call(
        body, name="ssm_bwd", grid=(4,),
        in_specs=[sp["u"], sp["u"], sp["lam"], sp["bmat"], sp["bmat"], sp["cmat"], sp["cmat"], sp["d"]],
        out_specs=[sp["u"], sp["lam"], sp["bmat"], sp["bmat"], sp["cmat"], sp["cmat"], sp["d"]],
        out_shape=[S((L, 512), F32), S((2, 2048), F32), S((4, 128, 512), F32), S((4, 128, 512), F32),
                   S((4, 512, 128), F32), S((4, 512, 128), F32), S((1, 512), F32)],
        scratch_shapes=[big, big, big, big], compiler_params=_params("parallel"),
    )(us, dy, lam, bre, bim, cre, cim, dvec)


def _discretize(ar, ai, log_dt, br, bi):
    dt = jnp.exp(log_dt)
    mag = jnp.exp(ar * dt)
    ph = ai * dt
    lr, li = mag * jnp.cos(ph), mag * jnp.sin(ph)
    nr, ni = lr - 1.0, li
    den = ar * ar + ai * ai
    fr = (nr * ar + ni * ai) / den
    fi = (ni * ar - nr * ai) / den
    return lr, li, fr[..., None] * br - fi[..., None] * bi, fr[..., None] * bi + fi[..., None] * br


def ssm_prep(ar, ai, log_dt, br, bi):
    G, P, H = br.shape

    def body(ar_ref, ai_ref, dt_ref, br_ref, bi_ref, lr_ref, li_ref, bbr_ref, bbi_ref):
        lr_ref[...], li_ref[...], bbr_ref[...], bbi_ref[...] = _discretize(
            ar_ref[...], ai_ref[...], dt_ref[...], br_ref[...], bi_ref[...])

    return pl.pallas_call(
        body, name="ssm_prep",
        out_shape=[S((G, P), F32), S((G, P), F32), S((G, P, H), F32), S((G, P, H), F32)],
    )(ar, ai, log_dt.reshape(G, 1), br, bi)


def ssm_prep_bwd(ar, ai, log_dt, br, bi, dlr, dli, dbbr, dbbi):
    G, P, H = br.shape

    def body(ar_ref, ai_ref, dt_ref, br_ref, bi_ref, dlr_ref, dli_ref, dbbr_ref, dbbi_ref,
             dar_ref, dai_ref, ddt_ref, dbr_ref, dbi_ref):
        _, vjp = jax.vjp(_discretize, ar_ref[...], ai_ref[...], dt_ref[...], br_ref[...], bi_ref[...])
        dar_ref[...], dai_ref[...], ddt_ref[...], dbr_ref[...], dbi_ref[...] = vjp(
            (dlr_ref[...], dli_ref[...], dbbr_ref[...], dbbi_ref[...]))

    return pl.pallas_call(
        body, name="ssm_prep_bwd",
        out_shape=[S((G, P), F32), S((G, P), F32), S((G, 1), F32), S((G, P, H), F32), S((G, P, H), F32)],
    )(ar, ai, log_dt.reshape(G, 1), br, bi, dlr, dli, dbbr, dbbi)


def _block_diag(x):
    j, n, R, C = x.shape
    eye = jnp.eye(n, dtype=x.dtype)
    return (x[:, :, :, None, :] * eye[None, :, None, :, None]).reshape(j, n * R, n * C)


def _block_diag_take(x, R, C):
    j = x.shape[0]
    n = x.shape[1] // R
    x5 = x.reshape(j, n, R, n, C)
    return jnp.stack([x5[:, i, :, i, :] for i in range(n)], axis=1)


def _to_segments(x):
    L, C = x.shape
    return x.reshape(N_SEG, L // N_SEG, C).transpose(1, 0, 2).reshape(L, C)


def _from_segments(x):
    L, C = x.shape
    return x.reshape(L // N_SEG, N_SEG, C).transpose(1, 0, 2).reshape(L, C)


BIG = ("ffn1_w_gate", "ffn1_w_up", "ffn1_w_down", "w_in", "glu_w", "w_out",
       "ffn2_w_gate", "ffn2_w_up", "ffn2_w_down", "ple_w_gate", "ple_w_proj")
SMALL = ("ffn1_norm", "mix_norm", "conv_w", "conv_b", "ssm_A_re", "ssm_A_im", "ssm_B_re", "ssm_B_im", "ssm_C_re", "ssm_C_im",
         "ssm_D", "ssm_log_dt", "glu_b", "conv_out_norm", "ssm_out_norm", "ffn2_norm", "ple_norm")


def _ssm_mats(w):
    G, P, H = w["ssm_B_re"].shape
    lr, li, bbr, bbi = ssm_prep(w["ssm_A_re"], w["ssm_A_im"], w["ssm_log_dt"], w["ssm_B_re"], w["ssm_B_im"])
    lam = jnp.stack([lr.reshape(G * P), li.reshape(G * P)])
    bmat = lambda bb: _block_diag(bb.reshape(4, G // 4, P, H).transpose(0, 1, 3, 2)).astype(BF16)
    cmat = lambda c: _block_diag(c.reshape(4, G // 4, H, P).transpose(0, 1, 3, 2)).astype(BF16)
    return lam, bmat(bbr), bmat(bbi), cmat(w["ssm_C_re"]), cmat(w["ssm_C_im"]), w["ssm_D"].reshape(1, G * H)


def layer_fwd(h0, pb, w, before_last):
    L, D = h0.shape
    u1 = rmsnorm_fwd(h0, w["ffn1_norm"])
    a1, b1, s1 = ffn_up(u1, w["ffn1_w_gate"], w["ffn1_w_up"])
    h1, u2 = mm_shard_k(s1, w["ffn1_w_down"], h0, 0.5, w["mix_norm"])
    z = mm_shard_n(u2, w["w_in"], F32)
    ya_n = conv_fwd(z, w["conv_w"], w["conv_b"], w["conv_out_norm"])
    us = _to_segments(z[3])
    mats = _ssm_mats(w)
    y = ssm_fwd(us, *mats)
    ys_n = glu_fwd(y, w["glu_w"], w["glu_b"], w["ssm_out_norm"])
    ycat = jnp.stack([ya_n, _from_segments(ys_n)])
    h2, u3 = mm_shard_k(ycat, w["w_out"], h1, 1.0, w["ffn2_norm"])
    a2, b2, s2 = ffn_up(u3, w["ffn2_w_gate"], w["ffn2_w_up"])
    h3, un = mm_shard_k(s2, w["ffn2_w_down"], h2, 0.5, w["ple_norm"])
    h4 = ple_fwd(un, pb, w["ple_w_gate"], w["ple_w_proj"], h3, before_last(h3))
    saved = dict(h0=h0, u1=u1, a1=a1, b1=b1, s1=s1, h1=h1, u2=u2, z=z, us=us, mats=mats, y=y, ycat=ycat,
                 h2=h2, u3=u3, a2=a2, b2=b2, s2=s2, h3=h3, un=un)
    return h4, saved


def _ffn_bwd(dh, dhb, h_in, u, a, b, s, wg, wu, wd, gnorm, cast_scale):
    da, db = dact_swiglu(dhb, wd, a, b)
    g_wd = wgrad(s, dhb)
    g_wg = wgrad(da, u)
    g_wu = wgrad(db, u)
    dh_in, dhb_in, g_norm = dx_rms([(da, wg, True), (db, wu, True)], h_in, gnorm, dh, cast_scale)
    return dh_in, dhb_in, g_wg, g_wu, g_wd, g_norm


def layer_bwd(dh, pb, w, sv, token):
    L, D = dh.shape
    G, P, H = w["ssm_B_re"].shape
    dpre, dpp3 = ple_bwd(sv["un"], pb, w["ple_w_gate"], w["ple_w_proj"], dh, token)
    g_wpg = wgrad(sv["un"], dpre).reshape(N_SHARD, D // N_SHARD, D)
    g_wpp = wgrad(pb, dpp3)
    dh3, dhb3, g_nple = dx_rms([(dpre[None], w["ple_w_gate"][None], False)], sv["h3"], w["ple_norm"], dh, 0.5)
    dh2, dhb, g_wg2, g_wu2, g_wd2, g_nffn2 = _ffn_bwd(dh3, dhb3, sv["h2"], sv["u3"], sv["a2"], sv["b2"], sv["s2"],
                                                      w["ffn2_w_gate"], w["ffn2_w_up"], w["ffn2_w_down"], w["ffn2_norm"], 1.0)
    dyn = dact_plain(dhb, w["w_out"])
    g_wout = wgrad(sv["ycat"], dhb).reshape(N_SHARD, -1, D)
    dz_abc, g_convw, g_convb, g_nconv = conv_bwd(sv["z"], w["conv_w"], w["conv_b"], w["conv_out_norm"], dyn)
    dy, dpre_g, zg, g_glub, g_nssm = glu_bwd(sv["y"], w["glu_w"], w["glu_b"], w["ssm_out_norm"], _to_segments(dyn[1]))
    C = zg.shape[1]
    g_gluw = wgrad(zg, dpre_g).reshape(N_SHARD, C // N_SHARD, C)
    dus, dlam, dbre, dbim, dcre, dcim, dd = ssm_bwd(sv["us"], dy, *sv["mats"])
    take_b = lambda m: _block_diag_take(m, H, P).transpose(0, 1, 3, 2).reshape(G, P, H)
    take_c = lambda m: _block_diag_take(m, P, H).transpose(0, 1, 3, 2).reshape(G, H, P)
    g_ar, g_ai, g_dt, g_br, g_bi = ssm_prep_bwd(
        w["ssm_A_re"], w["ssm_A_im"], w["ssm_log_dt"], w["ssm_B_re"], w["ssm_B_im"],
        dlam[0].reshape(G, P), dlam[1].reshape(G, P), take_b(dbre), take_b(dbim))
    dz3 = jnp.concatenate([dz_abc, _from_segments(dus).astype(BF16)[None]], axis=0)
    g_win = wgrad(sv["u2"], dz3)
    dh1, dhb1, g_nmix = dx_rms([(dz3, w["w_in"], False)], sv["h1"], w["mix_norm"], dh2, 0.5)
    dh0, _, g_wg1, g_wu1, g_wd1, g_nffn1 = _ffn_bwd(dh1, dhb1, sv["h0"], sv["u1"], sv["a1"], sv["b1"], sv["s1"],
                                                    w["ffn1_w_gate"], w["ffn1_w_up"], w["ffn1_w_down"], w["ffn1_norm"], 1.0)
    big = [g_wg1, g_wu1, g_wd1, g_win, g_gluw, g_wout, g_wg2, g_wu2, g_wd2, g_wpg, g_wpp]
    small = dict(ffn1_norm=g_nffn1, mix_norm=g_nmix, conv_w=g_convw, conv_b=g_convb, ssm_A_re=g_ar, ssm_A_im=g_ai,
                 ssm_B_re=g_br, ssm_B_im=g_bi, ssm_C_re=take_c(dcre), ssm_C_im=take_c(dcim), ssm_D=dd,
                 ssm_log_dt=g_dt, glu_b=g_glub, conv_out_norm=g_nconv, ssm_out_norm=g_nssm, ffn2_norm=g_nffn2,
                 ple_norm=g_nple)
    return dh0, big, small


def local_step(x, p, target, final_norm, weights_of, before_last, on_grads):
    depth = p.shape[0]
    h = x
    layers, saved, pbs = [], [], []
    for i in range(depth):
        w = weights_of(i, h)
        pb = p[i].astype(BF16)
        h, sv = layer_fwd(h, pb, w, functools.partial(before_last, i))
        layers.append(w)
        saved.append(sv)
        pbs.append(pb)
    loss_part, dh, g_final = loss_head(h, final_norm, target)
    smalls = [None] * depth
    token = jnp.zeros((8, 128), F32)
    for i in reversed(range(depth)):
        dh, big, smalls[i] = layer_bwd(dh, pbs[i], layers[i], saved[i], token)
        token = on_grads(i, big, dh)
    return loss_part, dh, smalls, g_final


ROW_TILE_MAX = 512


def _row_tile(rows):
    for t in range(ROW_TILE_MAX, 0, -16):
        if rows % t == 0:
            return t
    return rows


def elementwise(fn, ins, out_dtypes, name):
    rows, cols = ins[0].shape
    tr = _row_tile(rows)
    n_in = len(ins)

    def body(*refs):
        outs = fn(*[r[...] for r in refs[:n_in]])
        for o_ref, o in zip(refs[n_in:], outs):
            o_ref[...] = o.astype(o_ref.dtype)

    spec = pl.BlockSpec((tr, cols), lambda i: (i, 0))
    return pl.pallas_call(
        body, name=name, grid=(rows // tr,), in_specs=[spec] * n_in, out_specs=[spec] * len(out_dtypes),
        out_shape=[S((rows, cols), d) for d in out_dtypes], compiler_params=_params("parallel"),
    )(*ins)


def _adamw(w, g, m, v):
    m = ADAM_B1 * m + (1.0 - ADAM_B1) * g
    v = ADAM_B2 * v + (1.0 - ADAM_B2) * (g * g)
    m_hat = m / (1.0 - ADAM_B1 ** ADAM_STEP)
    v_hat = v / (1.0 - ADAM_B2 ** ADAM_STEP)
    delta = -ADAM_LR * (m_hat / (jnp.sqrt(v_hat) + ADAM_EPS) + ADAM_WD * w)
    return delta, m, v


ANY = pl.BlockSpec(memory_space=pl.ANY)


def _mesh_pos():
    return lax.axis_index("x"), lax.axis_index("y"), lax.axis_index("c")


def _other_chips(x, y):
    return [(1 - x, y), (x, 1 - y), (1 - x, 1 - y)]


def _remote(src, dst, send_sem, recv_sem, device):
    return pltpu.make_async_remote_copy(src_ref=src, dst_ref=dst, send_sem=send_sem, recv_sem=recv_sem,
                                        device_id=device, device_id_type=MESH)


def gather_weights(ws):
    n = len(ws)

    def body(*refs):
        outs = refs[n:2 * n]
        send_sems, recv_sems = refs[2 * n:]
        x, y, c = _mesh_pos()
        me_s = 2 * x + y
        sibling = (x, y, 1 - c)
        chips = _other_chips(x, y)
        n_half = outs[0].shape[0] // 2
        mine, other = pl.ds(c * n_half, n_half), pl.ds((1 - c) * n_half, n_half)
        sent = []
        for t in range(n):
            for j, (cx, cy) in enumerate(chips):
                blk = outs[t].at[mine, me_s]
                cp = _remote(blk, blk, send_sems.at[t, j], recv_sems.at[t, j], (cx, cy, c))
                cp.start()
                sent.append(cp)
        for j, (cx, cy) in enumerate(chips):
            for t in range(n):
                blk = outs[t].at[mine, 2 * cx + cy]
                _remote(blk, blk, send_sems.at[t, j], recv_sems.at[t, j], (cx, cy, c)).wait_recv()
                cp = _remote(blk, blk, send_sems.at[t, 3 + j], recv_sems.at[t, 3 + j], sibling)
                cp.start()
                sent.append(cp)
        for j, (cx, cy) in enumerate(chips):
            for t in range(n):
                blk = outs[t].at[other, 2 * cx + cy]
                _remote(blk, blk, send_sems.at[t, 3 + j], recv_sems.at[t, 3 + j], sibling).wait_recv()
        for cp in sent:
            cp.wait_send()

    return pl.pallas_call(
        body, name="gather_weights", in_specs=[ANY] * n, out_specs=[ANY] * n,
        out_shape=[S(w.shape, w.dtype) for w in ws], input_output_aliases={t: t for t in range(n)},
        scratch_shapes=[pltpu.SemaphoreType.DMA((n, 6)), pltpu.SemaphoreType.DMA((n, 6))],
    )(*ws)


HBM = pl.BlockSpec(memory_space=pltpu.HBM)
SEM = pl.BlockSpec(memory_space=pltpu.SEMAPHORE)
VMEM_WHOLE = pl.BlockSpec(memory_space=pltpu.VMEM)
SPLIT_COPY = pltpu.CompilerParams(has_side_effects=pltpu.SideEffectType.DATAFLOW_SIDE_EFFECTING)


def _hbm(x):
    return pltpu.with_memory_space_constraint(x, pltpu.HBM)


def _half_rows(ref, c):
    r2 = ref.shape[1] // 2
    return pl.ds(pl.multiple_of(c * r2, 8), r2)


def _gather_copies(bufs, send_sems, recv_sems, forward):
    x, y, c = _mesh_pos()
    copies = []
    for t in range(len(bufs)):
        rows = _half_rows(bufs[t], c)
        for j, (cx, cy) in enumerate(_other_chips(x, y)):
            blk = bufs[t].at[2 * cx + cy if forward else 2 * x + y, rows]
            peer = (x, y, 1 - c) if forward else (cx, cy, c)
            copies.append(_remote(blk, blk, send_sems.at[3 * t + j], recv_sems.at[3 * t + j], peer))
    return copies


def gather_start(bufs, layer, after, forward=False):
    n, k = len(bufs), len(after)

    def body(*refs):
        ins, send_sems, recv_sems, token = refs[:n], refs[n + k], refs[n + k + 1], refs[2 * n + k + 2]
        for cp in _gather_copies(ins, send_sems, recv_sems, forward):
            cp.start()
        token[...] = jnp.zeros_like(token)

    outs = pl.pallas_call(
        body, name=f"{'forward' if forward else 'gather'}_start_{layer}", in_specs=[HBM] * n + [ANY] * k,
        out_specs=[SEM, SEM] + [HBM] * n + [VMEM_WHOLE],
        out_shape=[pltpu.SemaphoreType.DMA((3 * n,)), pltpu.SemaphoreType.DMA((3 * n,))]
        + [pltpu.HBM(b.shape, b.dtype) for b in bufs] + [S((8, 128), F32)],
        input_output_aliases={t: t + 2 for t in range(n)}, compiler_params=SPLIT_COPY,
    )(*[_hbm(b) for b in bufs], *after)
    return outs[0], outs[1], list(outs[2:2 + n]), outs[2 + n]


def gather_wait(bufs, send_sems, recv_sems, after, layer, forward=False):
    n, n_after = len(bufs), len(after)

    def body(*refs):
        ins, send_ref, recv_ref = refs[:n], refs[n], refs[n + 1]
        for cp in _gather_copies(ins, send_ref, recv_ref, forward):
            cp.wait_send()
            cp.wait_recv()

    outs = pl.pallas_call(
        body, name=f"{'forward' if forward else 'gather'}_wait_{layer}", in_specs=[HBM] * n + [SEM, SEM] + [ANY] * n_after,
        out_specs=[HBM] * n,
        out_shape=[pltpu.HBM(b.shape, b.dtype) for b in bufs],
        input_output_aliases={t: t for t in range(n)}, compiler_params=SPLIT_COPY,
    )(*bufs, send_sems, recv_sems, *after)
    return list(outs)


def gather_forward(bufs):
    n = len(bufs)

    def body(*refs):
        outs = refs[n:2 * n]
        send_sems, recv_sems = refs[2 * n:]
        copies = _gather_copies(outs, send_sems, recv_sems, True)
        for cp in copies:
            cp.start()
        for cp in copies:
            cp.wait()

    return pl.pallas_call(
        body, name="gather_forward", in_specs=[ANY] * n, out_specs=[ANY] * n,
        out_shape=[S(b.shape, b.dtype) for b in bufs], input_output_aliases={t: t for t in range(n)},
        scratch_shapes=[pltpu.SemaphoreType.DMA((3 * n,)), pltpu.SemaphoreType.DMA((3 * n,))],
    )(*bufs)


def chips_start(sums, layer):
    n = len(sums)
    lands = [lax.empty((3,) + s.shape[1:], s.dtype) for s in sums]

    def body(*refs):
        a, land, send_sems, recv_sems, token = refs[:n], refs[n:2 * n], refs[2 * n], refs[2 * n + 1], refs[4 * n + 2]
        x, y, c = _mesh_pos()
        for t in range(n):
            for j, (cx, cy) in enumerate(_other_chips(x, y)):
                _remote(a[t].at[2 * cx + cy], land[t].at[j], send_sems.at[3 * t + j], recv_sems.at[3 * t + j], (cx, cy, c)).start()
        token[...] = jnp.zeros_like(token)

    outs = pl.pallas_call(
        body, name=f"chips_start_{layer}", in_specs=[HBM] * (2 * n), out_specs=[SEM, SEM] + [HBM] * (2 * n) + [VMEM_WHOLE],
        out_shape=[pltpu.SemaphoreType.DMA((3 * n,)), pltpu.SemaphoreType.DMA((3 * n,))]
        + [pltpu.HBM(b.shape, b.dtype) for b in sums + lands] + [S((8, 128), F32)],
        input_output_aliases={t: t + 2 for t in range(2 * n)}, compiler_params=SPLIT_COPY,
    )(*[_hbm(b) for b in sums + lands])
    return outs[0], outs[1], list(outs[2:2 + n]), list(outs[2 + n:2 + 2 * n]), outs[2 + 2 * n]


def chips_wait(sums, lands, send_sems, recv_sems, after, layer):
    n, n_after = len(sums), len(after)

    def body(*refs):
        a, land, send_ref, recv_ref = refs[:n], refs[n:2 * n], refs[2 * n], refs[2 * n + 1]
        x, y, c = _mesh_pos()
        for t in range(n):
            for j, (cx, cy) in enumerate(_other_chips(x, y)):
                cp = _remote(a[t].at[2 * cx + cy], land[t].at[j], send_ref.at[3 * t + j], recv_ref.at[3 * t + j], (cx, cy, c))
                cp.wait_send()
                cp.wait_recv()

    outs = pl.pallas_call(
        body, name=f"chips_wait_{layer}", in_specs=[HBM] * (2 * n) + [SEM, SEM] + [ANY] * n_after, out_specs=[HBM] * (2 * n),
        out_shape=[pltpu.HBM(b.shape, b.dtype) for b in sums + lands],
        input_output_aliases={t: t for t in range(2 * n)}, compiler_params=SPLIT_COPY,
    )(*sums, *lands, send_sems, recv_sems, *after)
    return list(outs[n:])


def cast_place_layer(w, layer, pos, dtype, token):
    _, r, c = w.shape
    tr = _row_tile(r)

    def body(pos_ref, w_ref, tok_ref, o_ref):
        o_ref[0] = (w_ref[0] + tok_ref[0:1, 0:1]).astype(dtype)

    return pl.pallas_call(
        body, name="cast_place_layer",
        grid_spec=pltpu.PrefetchScalarGridSpec(
            num_scalar_prefetch=1, grid=(r // tr,),
            in_specs=[pl.BlockSpec((1, tr, c), lambda i, pos: (layer, i, 0)), pl.BlockSpec((8, 128), lambda i, pos: (0, 0))],
            out_specs=pl.BlockSpec((1, tr, c), lambda i, pos: (pos[1], i, 0))),
        out_shape=S((N_SHARD, r, c), dtype), compiler_params=_params("parallel"),
    )(pos, w, token)


def cast_place(w, pos, dtype):
    layers, r, c = w.shape
    tr = _row_tile(r)

    def body(pos_ref, w_ref, o_ref):
        o_ref[0, 0] = w_ref[0].astype(dtype)

    return pl.pallas_call(
        body, name="cast_place",
        grid_spec=pltpu.PrefetchScalarGridSpec(
            num_scalar_prefetch=1, grid=(layers, r // tr),
            in_specs=[pl.BlockSpec((1, tr, c), lambda l, i, pos: (l, i, 0))],
            out_specs=pl.BlockSpec((1, 1, tr, c), lambda l, i, pos: (l, pos[1], i, 0))),
        out_shape=S((layers, N_SHARD, r, c), dtype), compiler_params=_params("parallel", "parallel"),
    )(pos, w)


def _pair_copy(g_ref, got_ref, send_sem, recv_sem):
    x, y, c = _mesh_pos()
    r2 = g_ref.shape[1] // 2
    give = pl.ds(pl.multiple_of((1 - c) * r2, 8), r2)
    return _remote(g_ref.at[:, give], got_ref, send_sem, recv_sem, (x, y, 1 - c))


def reduce_pair(gs, after):
    n, k = len(gs), len(after)

    def body(*refs):
        ins, got = refs[:n], refs[n + k:2 * n + k]
        send_sems, recv_sems = refs[2 * n + k:]
        copies = [_pair_copy(ins[t], got[t], send_sems.at[t], recv_sems.at[t]) for t in range(n)]
        for cp in copies:
            cp.start()
        for cp in copies:
            cp.wait()

    return pl.pallas_call(
        body, name="reduce_pair", in_specs=[ANY] * (n + k), out_specs=[ANY] * n,
        out_shape=[S((g.shape[0], g.shape[1] // 2, g.shape[2]), g.dtype) for g in gs],
        scratch_shapes=[pltpu.SemaphoreType.DMA((n,)), pltpu.SemaphoreType.DMA((n,))],
    )(*gs, *after)


def pair_sum(g, got, pos):
    ns, r2, c = got.shape
    tr = _row_tile(r2)
    n_i = r2 // tr

    def body(pos_ref, g_ref, got_ref, sum_ref, own_ref):
        s = pl.program_id(1)
        v = g_ref[0].astype(F32) + got_ref[0].astype(F32)
        sum_ref[0] = v.astype(BF16)

        @pl.when(s == pos_ref[1])
        def _():
            own_ref[...] = v

    return pl.pallas_call(
        body, name="pair_sum",
        grid_spec=pltpu.PrefetchScalarGridSpec(
            num_scalar_prefetch=1, grid=(n_i, ns),
            in_specs=[pl.BlockSpec((1, tr, c), lambda i, s, pos: (s, pos[0] * n_i + i, 0)),
                      pl.BlockSpec((1, tr, c), lambda i, s, pos: (s, i, 0))],
            out_specs=[pl.BlockSpec((1, tr, c), lambda i, s, pos: (s, i, 0)), pl.BlockSpec((tr, c), lambda i, s, pos: (i, 0))]),
        out_shape=[S((ns, r2, c), BF16), S((r2, c), F32)], compiler_params=_params("parallel", "arbitrary"),
    )(pos, g, got)


def chip_sum(own, p2, pos):
    r2, c = own.shape
    tr = _row_tile(r2)
    n_i = r2 // tr

    def body(pos_ref, own_ref, a_ref, b_ref, c_ref, o_ref):
        o_ref[...] = own_ref[...] + a_ref[0].astype(F32) + b_ref[0].astype(F32) + c_ref[0].astype(F32)

    peer = lambda j: pl.BlockSpec((1, tr, c), lambda i, pos: (j, i, 0))
    return pl.pallas_call(
        body, name="chip_sum",
        grid_spec=pltpu.PrefetchScalarGridSpec(
            num_scalar_prefetch=1, grid=(n_i,),
            in_specs=[pl.BlockSpec((tr, c), lambda i, pos: (i, 0)), peer(0), peer(1), peer(2)],
            out_specs=pl.BlockSpec((tr, c), lambda i, pos: (pos[0] * n_i + i, 0))),
        out_shape=S((2 * r2, c), F32), compiler_params=_params("parallel"),
    )(pos, own, p2, p2, p2)


def exchange_halves(rs):
    n = len(rs)

    def body(*refs):
        outs = refs[n:2 * n]
        send_sems, recv_sems = refs[2 * n:]
        x, y, c = _mesh_pos()
        copies = []
        for t in range(n):
            r2 = outs[t].shape[0] // 2
            rows = outs[t].at[pl.ds(pl.multiple_of(c * r2, 8), r2)]
            cp = _remote(rows, rows, send_sems.at[t], recv_sems.at[t], (x, y, 1 - c))
            cp.start()
            copies.append(cp)
        for cp in copies:
            cp.wait()

    return pl.pallas_call(
        body, name="exchange_halves", in_specs=[ANY] * n, out_specs=[ANY] * n,
        out_shape=[S(r.shape, r.dtype) for r in rs], input_output_aliases={t: t for t in range(n)},
        scratch_shapes=[pltpu.SemaphoreType.DMA((n,)), pltpu.SemaphoreType.DMA((n,))],
    )(*rs)


def allreduce_small(vec):
    R = vec.shape[0]
    H = R // 2

    def body(x_ref, o_ref, pair_buf, chip_buf, send_sems, recv_sems):
        x, y, c = _mesh_pos()
        me_s = 2 * x + y
        sibling = (x, y, 1 - c)
        mine = pl.ds(pl.multiple_of(c * H, 8), H)
        give = pl.ds(pl.multiple_of((1 - c) * H, 8), H)
        cp = _remote(x_ref.at[give], pair_buf, send_sems.at[0], recv_sems.at[0], sibling)
        cp.start()
        cp.wait()
        chip_buf[me_s] = x_ref[mine, :] + pair_buf[...]
        copies = []
        for j, (cx, cy) in enumerate(_other_chips(x, y)):
            cp = _remote(chip_buf.at[me_s], chip_buf.at[me_s], send_sems.at[1 + j], recv_sems.at[1 + j], (cx, cy, c))
            cp.start()
            copies.append(cp)
        for cp in copies:
            cp.wait()
        o_ref[mine, :] = (chip_buf[0] + chip_buf[1]) + (chip_buf[2] + chip_buf[3])
        cp = _remote(o_ref.at[mine], o_ref.at[mine], send_sems.at[4], recv_sems.at[4], sibling)
        cp.start()
        cp.wait()

    vm = pl.BlockSpec(memory_space=pltpu.VMEM)
    return pl.pallas_call(
        body, name="allreduce_small", in_specs=[vm], out_specs=vm, out_shape=S((R, 128), F32),
        scratch_shapes=[pltpu.VMEM((H, 128), F32), pltpu.VMEM((N_SHARD, H, 128), F32),
                        pltpu.SemaphoreType.DMA((5,)), pltpu.SemaphoreType.DMA((5,))],
        compiler_params=pltpu.CompilerParams(vmem_limit_bytes=VMEM_LIMIT_BYTES),
    )(vec)


def adamw_layer(w, g, m, v, layer, prev):
    _, r, c = w.shape
    tr = _row_tile(r)

    def body(w_ref, g_ref, m_ref, v_ref, *rest):
        outs = rest[-4:]
        g_val = g_ref[...]
        outs[0][0] = g_val
        outs[1][0], outs[2][0], outs[3][0] = _adamw(w_ref[0], g_val, m_ref[0], v_ref[0])

    lay = pl.BlockSpec((1, tr, c), lambda i: (layer, i, 0))
    prev = list(prev) if prev else []
    return pl.pallas_call(
        body, name="adamw_layer", grid=(r // tr,),
        in_specs=[lay, pl.BlockSpec((tr, c), lambda i: (i, 0)), lay, lay] + [ANY] * len(prev),
        out_specs=[lay] * 4, out_shape=[S(w.shape, F32)] * 4,
        input_output_aliases={4 + k: k for k in range(len(prev))}, compiler_params=_params("parallel"),
    )(w, g, m, v, *prev)


def reduce_begin(gs, pos, layer, after):
    got = reduce_pair(gs, after)
    sums, own = zip(*[pair_sum(g, o, pos) for g, o in zip(gs, got)])
    send_sems, recv_sems, sums, lands, token = chips_start(list(sums), layer)
    return dict(own=own, sums=sums, lands=lands, sems=(send_sems, recv_sems), token=token, layer=layer)


def reduce_end(pending, pos, after):
    lands = chips_wait(pending["sums"], pending["lands"], *pending["sems"], after, pending["layer"])
    return exchange_halves([chip_sum(o, p, pos) for o, p in zip(pending["own"], lands)])


W_NAMES = ("ffn1_norm", "ffn1_w_gate", "ffn1_w_up", "ffn1_w_down", "mix_norm", "w_in", "conv_w", "conv_b", "ssm_A_re", "ssm_A_im",
           "ssm_B_re", "ssm_B_im", "ssm_C_re", "ssm_C_im", "ssm_D", "ssm_log_dt", "glu_w", "glu_b", "conv_out_norm", "ssm_out_norm",
           "w_out", "ffn2_norm", "ffn2_w_gate", "ffn2_w_up", "ffn2_w_down", "ple_norm", "ple_w_gate", "ple_w_proj", "final_norm")
SMALL_ALL = SMALL + ("final_norm",)
TRANSPOSED = ("ffn1_w_gate", "ffn1_w_up", "ffn2_w_gate", "ffn2_w_up")
PACK = ROW_TILE_MAX * 128


def _pack(parts):
    flat = jnp.concatenate([p.reshape(-1) for p in parts])
    pad = (-flat.shape[0]) % PACK
    return jnp.pad(flat, (0, pad)).reshape(-1, 128)


def _unpack(vec, shapes):
    flat = vec.reshape(-1)
    out, off = [], 0
    for shp in shapes:
        size = math.prod(shp)
        out.append(flat[off:off + size].reshape(shp))
        off += size
    return out


def _step(a):
    a = {k: jnp.swapaxes(v, 1, 2) if k.removeprefix("m_").removeprefix("v_") in TRANSPOSED else v for k, v in a.items()}
    x, p, target = a["x"][0], a["p"][:, 0], a["loss_target"][0]
    depth = p.shape[0]
    L, D = x.shape
    me_s = 2 * lax.axis_index("x") + lax.axis_index("y")

    pos = jnp.stack([lax.axis_index("c"), me_s]).astype(jnp.int32)
    conv_w = gather_weights([cast_place(a["conv_w"], pos, F32)])[0]
    started, token = [], jnp.zeros((8, 128), F32)
    for l in range(depth):
        started.append(gather_start([cast_place_layer(a[n], l, pos, BF16, token) for n in BIG], l, [conv_w]))
        token = started[-1][3]

    forwarding = {}

    def before_last(l, h3):
        if l + 1 == depth:
            return jnp.zeros((8, 128), F32)
        send_sems, recv_sems, bufs, _ = started[l + 1]
        forwarding[l + 1] = gather_start(gather_wait(bufs, send_sems, recv_sems, [h3], l + 1), l + 1, [], forward=True)
        return forwarding[l + 1][3]

    def weights_of(l, h):
        if l:
            send_sems, recv_sems, bufs, _ = forwarding[l]
            full = gather_wait(bufs, send_sems, recv_sems, [h], l, forward=True)
        else:
            send_sems, recv_sems, bufs, _ = started[0]
            full = gather_forward(gather_wait(bufs, send_sems, recv_sems, [s[3] for s in started], 0))
        w = {n: a[n][l] for n in SMALL if n != "conv_w"}
        w.update(dict(zip(BIG, full)))
        C = w["glu_w"].shape[-1]
        w["glu_w"] = w["glu_w"].reshape(C, C)
        w["w_out"] = w["w_out"].reshape(2, -1, D)
        w["ple_w_gate"] = w["ple_w_gate"].reshape(D, D)
        w["conv_w"] = conv_w[l].transpose(1, 0, 2).reshape(3, -1)
        return w

    pending, first_layer_grads = {}, []

    def on_grads(l, big, dh):
        if l == 0:
            first_layer_grads.extend(big)
            return None
        pending[l] = reduce_begin(big, pos, l, [])
        return pending[l]["token"]

    loss_part, dx, smalls, g_final = local_step(x, p, target, a["final_norm"], weights_of, before_last, on_grads)
    small_shapes = [(depth,) + smalls[0][n].shape for n in SMALL] + [g_final.shape, (1,)]
    parts = [smalls[l][n] for n in SMALL for l in range(depth)] + [g_final, loss_part[0, 0:1]]
    summed_vec = allreduce_small(_pack(parts))
    pending[0] = reduce_begin(first_layer_grads, pos, 0, [summed_vec])
    stacked = [None] * len(BIG)
    for l in reversed(range(depth)):
        after = [pending[0]["token"]] if l else [s[3] for s in stacked]
        reduced = reduce_end(pending[l], pos, after)
        stacked = [adamw_layer(a[n], reduced[i], a["m_" + n], a["v_" + n], l, stacked[i]) for i, n in enumerate(BIG)]
    big_out = {n: [jnp.swapaxes(o, 1, 2) for o in outs] if n in TRANSPOSED else outs for n, outs in zip(BIG, stacked)}

    summed = _unpack(summed_vec, small_shapes)
    g_small = dict(zip(SMALL_ALL, summed[:-1]))
    loss = summed[-1][0]
    n_conv = a["conv_w"].shape[-1]
    g_small["conv_w"] = lax.dynamic_slice_in_dim(g_small["conv_w"], me_s * n_conv, n_conv, axis=2)
    g_small = {n: g_small[n].reshape(a[n].shape) for n in SMALL_ALL}
    packed = [_pack([src[n] for n in SMALL_ALL]) for src in
              ({n: a[n] for n in SMALL_ALL}, g_small, {n: a["m_" + n] for n in SMALL_ALL}, {n: a["v_" + n] for n in SMALL_ALL})]
    shapes = [a[n].shape for n in SMALL_ALL]
    d_s, m_s, v_s = [dict(zip(SMALL_ALL, _unpack(o, shapes))) for o in elementwise(_adamw, packed, [F32, F32, F32], "adamw_small")]

    outs = {n: big_out[n] if n in big_out else (g_small[n], d_s[n], m_s[n], v_s[n]) for n in W_NAMES}
    return (loss, dx[None], *[outs[n][0] for n in W_NAMES], *[outs[n][1] for n in W_NAMES],
            *[outs[n][2] for n in W_NAMES], *[outs[n][3] for n in W_NAMES])


def kernel(x, p, ffn1_norm, ffn1_w_gate, ffn1_w_up, ffn1_w_down, mix_norm, w_in, conv_w, conv_b, ssm_A_re, ssm_A_im, ssm_B_re, ssm_B_im, ssm_C_re, ssm_C_im, ssm_D, ssm_log_dt, glu_w, glu_b, conv_out_norm, ssm_out_norm, w_out, ffn2_norm, ffn2_w_gate, ffn2_w_up, ffn2_w_down, ple_norm, ple_w_gate, ple_w_proj, final_norm, loss_target, m_ffn1_norm, m_ffn1_w_gate, m_ffn1_w_up, m_ffn1_w_down, m_mix_norm, m_w_in, m_conv_w, m_conv_b, m_ssm_A_re, m_ssm_A_im, m_ssm_B_re, m_ssm_B_im, m_ssm_C_re, m_ssm_C_im, m_ssm_D, m_ssm_log_dt, m_glu_w, m_glu_b, m_conv_out_norm, m_ssm_out_norm, m_w_out, m_ffn2_norm, m_ffn2_w_gate, m_ffn2_w_up, m_ffn2_w_down, m_ple_norm, m_ple_w_gate, m_ple_w_proj, m_final_norm, v_ffn1_norm, v_ffn1_w_gate, v_ffn1_w_up, v_ffn1_w_down, v_mix_norm, v_w_in, v_conv_w, v_conv_b, v_ssm_A_re, v_ssm_A_im, v_ssm_B_re, v_ssm_B_im, v_ssm_C_re, v_ssm_C_im, v_ssm_D, v_ssm_log_dt, v_glu_w, v_glu_b, v_conv_out_norm, v_ssm_out_norm, v_w_out, v_ffn2_norm, v_ffn2_w_gate, v_ffn2_w_up, v_ffn2_w_down, v_ple_norm, v_ple_w_gate, v_ple_w_proj, v_final_norm):
    return _step(dict(locals()))
```

```python
import functools
import math

import jax
import jax.numpy as jnp
from jax import lax
from jax.experimental import pallas as pl
from jax.experimental.pallas import tpu as pltpu

F32, BF16 = jnp.float32, jnp.bfloat16
S = jax.ShapeDtypeStruct
EPS = 1e-6
N_SEG = 8
N_SHARD = 4
N_DEV = 8
VMEM_LIMIT_BYTES = 56 * 1024 * 1024
ADAM_LR, ADAM_B1, ADAM_B2, ADAM_EPS, ADAM_WD, ADAM_STEP = 0.001, 0.9, 0.999, 1e-08, 0.01, 10
MESH = pl.DeviceIdType.MESH


def _params(*sem):
    return pltpu.CompilerParams(dimension_semantics=sem if sem else None, vmem_limit_bytes=VMEM_LIMIT_BYTES)


def _dot(a, b, ca, cb):
    return lax.dot_general(a, b, (((ca,), (cb,)), ((), ())), preferred_element_type=F32)


def _sigmoid(x):
    return 1.0 / (1.0 + jnp.exp(-x))


def _rstd(x):
    return lax.rsqrt(jnp.mean(x * x, axis=-1, keepdims=True) + EPS)


def _rms_bwd(x, g, dy):
    r = _rstd(x)
    xh = x * r
    dxh = dy * g
    dx = r * (dxh - xh * jnp.mean(dxh * xh, axis=-1, keepdims=True))
    return dx, jnp.sum(dy * xh, axis=0, keepdims=True)


def _tile(n, want):
    return want if n % want == 0 else n


def rmsnorm_fwd(h, g):
    L, D = h.shape
    tm = _tile(L, 512)

    def body(h_ref, g_ref, o_ref):
        x = h_ref[...]
        o_ref[...] = (x * _rstd(x) * g_ref[...]).astype(BF16)

    return pl.pallas_call(
        body, name="rmsnorm_fwd", grid=(L // tm,),
        in_specs=[pl.BlockSpec((tm, D), lambda m: (m, 0)), pl.BlockSpec((1, D), lambda m: (0, 0))],
        out_specs=pl.BlockSpec((tm, D), lambda m: (m, 0)),
        out_shape=S((L, D), BF16), compiler_params=_params("parallel"),
    )(h, g.reshape(1, D))


def ffn_up(u, wg, wu):
    L, D = u.shape
    ns, F, _ = wg.shape
    tm = _tile(L, 512)

    def body(u_ref, wg_ref, wu_ref, a_ref, b_ref, s_ref):
        x = u_ref[...]
        a = _dot(x, wg_ref[0], 1, 1)
        b = _dot(x, wu_ref[0], 1, 1)
        a_ref[0] = a.astype(BF16)
        b_ref[0] = b.astype(BF16)
        s_ref[0] = (a * _sigmoid(a) * b).astype(BF16)

    w_spec = pl.BlockSpec((1, F, D), lambda s, m: (s, 0, 0))
    o_spec = pl.BlockSpec((1, tm, F), lambda s, m: (s, m, 0))
    return pl.pallas_call(
        body, name="ffn_up", grid=(ns, L // tm),
        in_specs=[pl.BlockSpec((tm, D), lambda s, m: (m, 0)), w_spec, w_spec],
        out_specs=[o_spec, o_spec, o_spec],
        out_shape=[S((ns, L, F), BF16)] * 3, compiler_params=_params("parallel", "parallel"),
    )(u, wg, wu)


def mm_shard_n(u, w3, out_dtype):
    L, K = u.shape
    ns, _, N = w3.shape
    tm = _tile(L, 512)

    def body(u_ref, w_ref, o_ref):
        o_ref[0] = _dot(u_ref[...], w_ref[0], 1, 0).astype(out_dtype)

    return pl.pallas_call(
        body, name="mm_shard_n", grid=(ns, L // tm),
        in_specs=[pl.BlockSpec((tm, K), lambda s, m: (m, 0)), pl.BlockSpec((1, K, N), lambda s, m: (s, 0, 0))],
        out_specs=pl.BlockSpec((1, tm, N), lambda s, m: (s, m, 0)),
        out_shape=S((ns, L, N), out_dtype), compiler_params=_params("parallel", "parallel"),
    )(u, w3)


def mm_shard_k(a3, w3, res, scale, g_next):
    nk, L, Kc = a3.shape
    N = w3.shape[2]
    tm = _tile(L, 512)

    def body(a_ref, w_ref, r_ref, g_ref, o_ref, u_ref):
        acc = _dot(a_ref[0], w_ref[0], 1, 0)
        for k in range(1, nk):
            acc += _dot(a_ref[k], w_ref[k], 1, 0)
        h = r_ref[...] + scale * acc
        o_ref[...] = h
        u_ref[...] = (h * _rstd(h) * g_ref[...]).astype(BF16)

    tile = pl.BlockSpec((tm, N), lambda m: (m, 0))
    return pl.pallas_call(
        body, name="mm_shard_k", grid=(L // tm,),
        in_specs=[pl.BlockSpec((nk, tm, Kc), lambda m: (0, m, 0)), pl.BlockSpec((nk, Kc, N), lambda m: (0, 0, 0)),
                  tile, pl.BlockSpec((1, N), lambda m: (0, 0))],
        out_specs=[tile, tile],
        out_shape=[S((L, N), F32), S((L, N), BF16)], compiler_params=_params("parallel"),
    )(a3, w3, res, g_next.reshape(1, N))


CONV_HALO = 8


def _conv_specs(L, tm, C, shard):
    nb = L // CONV_HALO
    per = tm // CONV_HALO
    main = pl.BlockSpec((1, tm, C), lambda m: (shard, m, 0))
    prev = pl.BlockSpec((1, CONV_HALO, C), lambda m: (shard, jnp.maximum(m * per - 1, 0), 0))
    nxt = pl.BlockSpec((1, CONV_HALO, C), lambda m: (shard, jnp.minimum((m + 1) * per, nb - 1), 0))
    return main, prev, nxt


def _conv_core(zb, zc, zv, w_ref, bias, grow, L):
    valid = (grow >= 0) & (grow < L)
    v = jnp.where(valid, zc * zv, 0.0)
    v1 = pltpu.roll(v, 1, 0)
    v2 = pltpu.roll(v, 2, 0)
    cb = w_ref[0:1, :] * v2 + w_ref[1:2, :] * v1 + w_ref[2:3, :] * v + bias
    return valid, v, v1, v2, cb, zb * cb


def conv_fwd(z, conv_w, conv_b, gnorm):
    _, L, C = z.shape
    tm = _tile(L, 256)
    H = CONV_HALO

    def body(zb_ref, zc_ref, zcp_ref, zv_ref, zvp_ref, w_ref, b_ref, g_ref, o_ref):
        m = pl.program_id(0)
        zc = jnp.concatenate([zcp_ref[0], zc_ref[0]], axis=0)
        zv = jnp.concatenate([zvp_ref[0], zv_ref[0]], axis=0)
        grow = m * tm - H + lax.broadcasted_iota(jnp.int32, (tm + H, C), 0)
        valid = grow >= 0
        v = jnp.where(valid, zc * zv, 0.0)
        v1 = pltpu.roll(v, 1, 0)
        v2 = pltpu.roll(v, 2, 0)
        cb = (w_ref[0:1, :] * v2 + w_ref[1:2, :] * v1 + w_ref[2:3, :] * v + b_ref[...])[H:, :]
        ya = zb_ref[0] * cb
        o_ref[...] = (ya * _rstd(ya) * g_ref[...]).astype(BF16)

    zb_m, _, _ = _conv_specs(L, tm, C, 0)
    zc_m, zc_p, _ = _conv_specs(L, tm, C, 1)
    zv_m, zv_p, _ = _conv_specs(L, tm, C, 2)
    row = lambda r: pl.BlockSpec((r, C), lambda m: (0, 0))
    return pl.pallas_call(
        body, name="conv_fwd", grid=(L // tm,),
        in_specs=[zb_m, zc_m, zc_p, zv_m, zv_p, row(3), row(1), row(1)],
        out_specs=pl.BlockSpec((tm, C), lambda m: (m, 0)),
        out_shape=S((L, C), BF16), compiler_params=_params("parallel"),
    )(z, z, z, z, z, conv_w, conv_b.reshape(1, C), gnorm.reshape(1, C))


def _cmul(ar, ai, br, bi):
    return ar * br - ai * bi, ar * bi + ai * br


def _scan_fwd(hr_ref, hi_ref, lr, li, n_steps):
    W = hr_ref.shape[1]
    zero = jnp.zeros((N_SEG, W), F32)

    def local(t, c):
        r = pl.multiple_of(t * N_SEG, N_SEG)
        pr, pi = _cmul(lr, li, c[0], c[1])
        nr = pr + hr_ref[pl.ds(r, N_SEG), :]
        ni = pi + hi_ref[pl.ds(r, N_SEG), :]
        hr_ref[pl.ds(r, N_SEG), :] = nr
        hi_ref[pl.ds(r, N_SEG), :] = ni
        return nr, ni

    fr, fi = lax.fori_loop(0, n_steps, local, (zero, zero))
    qr, qi = _cpow(lr, li, n_steps)
    row = lax.broadcasted_iota(jnp.int32, (N_SEG, W), 0)
    cr, ci = zero, zero
    for seg in range(1, N_SEG):
        tr, ti = _cmul(qr, qi, cr, ci)
        sr = pltpu.roll(fr + tr, 1, 0)
        si = pltpu.roll(fi + ti, 1, 0)
        cr = jnp.where(row == seg, sr, cr)
        ci = jnp.where(row == seg, si, ci)

    def fix(t, c):
        r = pl.multiple_of(t * N_SEG, N_SEG)
        pr, pi = _cmul(lr, li, c[0], c[1])
        ar, ai = _cmul(pr, pi, cr, ci)
        hr_ref[pl.ds(r, N_SEG), :] += ar
        hi_ref[pl.ds(r, N_SEG), :] += ai
        return pr, pi

    lax.fori_loop(0, n_steps, fix, (jnp.ones((N_SEG, W), F32), zero))


def _cpow(lr, li, n):
    rr, ri = None, None
    br, bi = lr, li
    while n:
        if n & 1:
            rr, ri = (br, bi) if rr is None else _cmul(rr, ri, br, bi)
        n >>= 1
        if n:
            br, bi = _cmul(br, bi, br, bi)
    return rr, ri


def _ssm_specs(L):
    col = lambda w: pl.BlockSpec((L, w), lambda j: (0, j))
    return dict(
        u=col(128), lam=pl.BlockSpec((2, 512), lambda j: (0, j)),
        bmat=pl.BlockSpec((1, 128, 512), lambda j: (j, 0, 0)), cmat=pl.BlockSpec((1, 512, 128), lambda j: (j, 0, 0)),
        d=pl.BlockSpec((1, 128), lambda j: (0, j)))


def ssm_fwd(us, lam, bre, bim, cre, cim, dvec):
    L = us.shape[0]
    n_steps = L // N_SEG
    sp = _ssm_specs(L)

    def body(u_ref, lam_ref, bre_ref, bim_ref, cre_ref, cim_ref, d_ref, y_ref, hr, hi):
        u = u_ref[...]
        ub = u.astype(BF16)
        hr[...] = _dot(ub, bre_ref[0], 1, 0)
        hi[...] = _dot(ub, bim_ref[0], 1, 0)
        lr = jnp.broadcast_to(lam_ref[0:1, :], (N_SEG, 512))
        li = jnp.broadcast_to(lam_ref[1:2, :], (N_SEG, 512))
        _scan_fwd(hr, hi, lr, li, n_steps)
        y_ref[...] = (_dot(hr[...].astype(BF16), cre_ref[0], 1, 0) - _dot(hi[...].astype(BF16), cim_ref[0], 1, 0)
                      + d_ref[...] * u)

    return pl.pallas_call(
        body, name="ssm_fwd", grid=(4,),
        in_specs=[sp["u"], sp["lam"], sp["bmat"], sp["bmat"], sp["cmat"], sp["cmat"], sp["d"]],
        out_specs=sp["u"], out_shape=S((L, 512), F32),
        scratch_shapes=[pltpu.VMEM((L, 512), F32), pltpu.VMEM((L, 512), F32)],
        compiler_params=_params("parallel"),
    )(us, lam, bre, bim, cre, cim, dvec)


_GELU_C = math.sqrt(2.0 / math.pi)


def _gelu(y):
    t = jnp.tanh(_GELU_C * (y + 0.044715 * y * y * y))
    return 0.5 * y * (1.0 + t), t


def glu_fwd(y, w, b, gnorm):
    L, C = y.shape
    tm = _tile(L, 512)

    def body(y_ref, w_ref, b_ref, g_ref, o_ref):
        zg, _ = _gelu(y_ref[...])
        out = zg * _sigmoid(_dot(zg.astype(BF16), w_ref[...], 1, 0) + b_ref[...])
        o_ref[...] = (out * _rstd(out) * g_ref[...]).astype(BF16)

    row = pl.BlockSpec((1, C), lambda m: (0, 0))
    return pl.pallas_call(
        body, name="glu_fwd", grid=(L // tm,),
        in_specs=[pl.BlockSpec((tm, C), lambda m: (m, 0)), pl.BlockSpec((C, C), lambda m: (0, 0)), row, row],
        out_specs=pl.BlockSpec((tm, C), lambda m: (m, 0)),
        out_shape=S((L, C), BF16), compiler_params=_params("parallel"),
    )(y, w, b.reshape(1, C), gnorm.reshape(1, C))


def _ple_specs(L, D, P, tm, nb):
    return [pl.BlockSpec((tm, D), lambda n, m: (m, 0)), pl.BlockSpec((tm, P), lambda n, m: (m, 0)),
            pl.BlockSpec((D, nb), lambda n, m: (0, n)), pl.BlockSpec((1, P, nb), lambda n, m: (n, 0, 0)),
            pl.BlockSpec((tm, nb), lambda n, m: (m, n))]


def ple_fwd(un, pb, wpg, wpp, h, token):
    L, D = un.shape
    ns, P, nb = wpp.shape
    tm = _tile(L, 512)

    def body(un_ref, p_ref, wg_ref, wp_ref, h_ref, tok_ref, o_ref):
        gate = _sigmoid(_dot(un_ref[...], wg_ref[...], 1, 0))
        o_ref[...] = h_ref[...] + tok_ref[0:1, 0:1] + _dot(p_ref[...], wp_ref[0], 1, 0) * gate

    return pl.pallas_call(
        body, name="ple_fwd", grid=(ns, L // tm),
        in_specs=_ple_specs(L, D, P, tm, nb) + [pl.BlockSpec((8, 128), lambda n, m: (0, 0))],
        out_specs=pl.BlockSpec((tm, nb), lambda n, m: (m, n)),
        out_shape=S((L, D), F32), compiler_params=_params("parallel", "parallel"),
    )(un, pb, wpg, wpp, h, token)


def loss_head(h, g, target):
    L, D = h.shape
    tm = _tile(L, 256)

    def body(h_ref, g_ref, t_ref, loss_ref, dh_ref, dg_ref):
        m = pl.program_id(0)
        x = h_ref[...]
        gg = g_ref[...]
        e = x * _rstd(x) * gg - t_ref[...]
        dx, dg = _rms_bwd(x, gg, e * (1.0 / D))
        dh_ref[...] = dx
        part = jnp.full((8, 128), 0.5 / D, F32) * jnp.sum(e * e)

        @pl.when(m == 0)
        def _():
            loss_ref[...] = part
            dg_ref[...] = dg

        @pl.when(m > 0)
        def _():
            loss_ref[...] += part
            dg_ref[...] += dg

    return pl.pallas_call(
        body, name="loss_head", grid=(L // tm,),
        in_specs=[pl.BlockSpec((tm, D), lambda m: (m, 0)), pl.BlockSpec((1, D), lambda m: (0, 0)),
                  pl.BlockSpec((tm, D), lambda m: (m, 0))],
        out_specs=[pl.BlockSpec((8, 128), lambda m: (0, 0)), pl.BlockSpec((tm, D), lambda m: (m, 0)),
                   pl.BlockSpec((1, D), lambda m: (0, 0))],
        out_shape=[S((8, 128), F32), S((L, D), F32), S((1, D), F32)],
        compiler_params=_params("arbitrary"),
    )(h, g.reshape(1, D), target)


def ple_bwd(un, pb, wpg, wpp, dh, token):
    L, D = un.shape
    ns, P, nb = wpp.shape
    tm = _tile(L, 512)

    def body(un_ref, p_ref, wg_ref, wp_ref, dh_ref, tok_ref, dpre_ref, dpp_ref):
        gate = _sigmoid(_dot(un_ref[...], wg_ref[...], 1, 0))
        pp = _dot(p_ref[...], wp_ref[0], 1, 0)
        d = dh_ref[...] + tok_ref[0:1, 0:1]
        dpp_ref[0] = (d * gate).astype(BF16)
        dpre_ref[...] = (d * pp * gate * (1.0 - gate)).astype(BF16)

    return pl.pallas_call(
        body, name="ple_bwd", grid=(ns, L // tm),
        in_specs=_ple_specs(L, D, P, tm, nb) + [pl.BlockSpec((8, 128), lambda n, m: (0, 0))],
        out_specs=[pl.BlockSpec((tm, nb), lambda n, m: (m, n)), pl.BlockSpec((1, tm, nb), lambda n, m: (n, m, 0))],
        out_shape=[S((L, D), BF16), S((ns, L, nb), BF16)], compiler_params=_params("parallel", "parallel"),
    )(un, pb, wpg, wpp, dh, token)


def wgrad(a, b):
    a3 = a if a.ndim == 3 else a[None]
    b3 = b if b.ndim == 3 else b[None]
    ns = max(a3.shape[0], b3.shape[0])
    _, L, Ka = a3.shape
    N = b3.shape[2]
    a_map = (lambda s: (s, 0, 0)) if a3.shape[0] > 1 else (lambda s: (0, 0, 0))
    b_map = (lambda s: (s, 0, 0)) if b3.shape[0] > 1 else (lambda s: (0, 0, 0))

    def body(a_ref, b_ref, o_ref):
        o_ref[0] = _dot(a_ref[0], b_ref[0], 0, 0).astype(BF16)

    return pl.pallas_call(
        body, name="wgrad", grid=(ns,),
        in_specs=[pl.BlockSpec((1, L, Ka), a_map), pl.BlockSpec((1, L, N), b_map)],
        out_specs=pl.BlockSpec((1, Ka, N), lambda s: (s, 0, 0)),
        out_shape=S((ns, Ka, N), BF16), compiler_params=_params("parallel"),
    )(a3, b3)


def dx_rms(pairs, h, g, dh_in, cast_scale):
    L, D = h.shape
    nk = pairs[0][0].shape[0]
    n_pairs = len(pairs)
    tm = _tile(L, 512)
    n_m = L // tm
    w_dims = [0 if transposed else 1 for _, _, transposed in pairs]

    def body(*refs):
        ins, (h_ref, g_ref, dhi_ref, dho_ref, dhb_ref, dg_ref) = refs[:2 * n_pairs], refs[2 * n_pairs:]
        m = pl.program_id(0)
        acc = None
        for i in range(n_pairs):
            for k in range(nk):
                part = _dot(ins[2 * i][k], ins[2 * i + 1][k], 1, w_dims[i])
                acc = part if acc is None else acc + part
        dx, dg = _rms_bwd(h_ref[...], g_ref[...], acc)
        dh_out = dhi_ref[...] + dx
        dho_ref[...] = dh_out
        dhb_ref[...] = (cast_scale * dh_out).astype(BF16)

        @pl.when(m == 0)
        def _():
            dg_ref[...] = dg

        @pl.when(m > 0)
        def _():
            dg_ref[...] += dg

    in_specs, args = [], []
    for a3, w3, _ in pairs:
        Kc = a3.shape[2]
        in_specs += [pl.BlockSpec((nk, tm, Kc), lambda m: (0, m, 0)),
                     pl.BlockSpec(w3.shape, lambda m: (0, 0, 0), pipeline_mode=pl.Buffered(1))]
        args += [a3, w3]
    tile = pl.BlockSpec((tm, D), lambda m: (m, 0))
    row = pl.BlockSpec((1, D), lambda m: (0, 0))
    return pl.pallas_call(
        body, name="dx_rms", grid=(n_m,),
        in_specs=in_specs + [tile, row, tile], out_specs=[tile, tile, row],
        out_shape=[S((L, D), F32), S((L, D), BF16), S((1, D), F32)], compiler_params=_params("arbitrary"),
    )(*args, h, g.reshape(1, D), dh_in)


def dact_plain(dhb, w3):
    L, D = dhb.shape
    ns, N, _ = w3.shape
    tm = _tile(L, 512)

    def body(d_ref, w_ref, o_ref):
        o_ref[0] = _dot(d_ref[...], w_ref[0], 1, 1)

    return pl.pallas_call(
        body, name="dact_plain", grid=(ns, L // tm),
        in_specs=[pl.BlockSpec((tm, D), lambda s, m: (m, 0)), pl.BlockSpec((1, N, D), lambda s, m: (s, 0, 0))],
        out_specs=pl.BlockSpec((1, tm, N), lambda s, m: (s, m, 0)),
        out_shape=S((ns, L, N), F32), compiler_params=_params("parallel", "parallel"),
    )(dhb, w3)


def dact_swiglu(dhb, wd, a3, b3):
    L, D = dhb.shape
    ns, F, _ = wd.shape
    tm = _tile(L, 512)

    def body(d_ref, w_ref, a_ref, b_ref, da_ref, db_ref):
        ds = _dot(d_ref[...], w_ref[0], 1, 1)
        a = a_ref[0].astype(F32)
        b = b_ref[0].astype(F32)
        sg = _sigmoid(a)
        da_ref[0] = (ds * b * (sg * (1.0 + a * (1.0 - sg)))).astype(BF16)
        db_ref[0] = (ds * (a * sg)).astype(BF16)

    t_spec = pl.BlockSpec((1, tm, F), lambda s, m: (s, m, 0))
    return pl.pallas_call(
        body, name="dact_swiglu", grid=(ns, L // tm),
        in_specs=[pl.BlockSpec((tm, D), lambda s, m: (m, 0)), pl.BlockSpec((1, F, D), lambda s, m: (s, 0, 0)), t_spec, t_spec],
        out_specs=[t_spec, t_spec], out_shape=[S((ns, L, F), BF16)] * 2,
        compiler_params=_params("parallel", "parallel"),
    )(dhb, wd, a3, b3)


def conv_bwd(z, conv_w, conv_b, gnorm, dyn):
    _, L, C = z.shape
    tm = _tile(L, 256)
    H = CONV_HALO
    T = tm + 2 * H

    def body(zb_ref, zbp_ref, zbn_ref, zc_ref, zcp_ref, zcn_ref, zv_ref, zvp_ref, zvn_ref, d_ref, dp_ref, dn_ref,
             w_ref, b_ref, g_ref, dz_ref, dw_ref, db_ref, dg_ref):
        m = pl.program_id(0)
        cat = lambda p, c, n: jnp.concatenate([p[0], c[0], n[0]], axis=0)
        zb, zc, zv, d = cat(zbp_ref, zb_ref, zbn_ref), cat(zcp_ref, zc_ref, zcn_ref), cat(zvp_ref, zv_ref, zvn_ref), cat(dp_ref, d_ref, dn_ref)
        grow = m * tm - H + lax.broadcasted_iota(jnp.int32, (T, C), 0)
        valid, v, v1, v2, cb, ya = _conv_core(zb, zc, zv, w_ref, b_ref[...], grow, L)
        dya, _ = _rms_bwd(ya, g_ref[...], d)
        dc = jnp.where(valid, dya * zb, 0.0)
        dv = w_ref[2:3, :] * dc + w_ref[1:2, :] * pltpu.roll(dc, T - 1, 0) + w_ref[0:1, :] * pltpu.roll(dc, T - 2, 0)
        dz_ref[0] = (dya * cb)[H:H + tm, :].astype(BF16)
        dz_ref[1] = (dv * zv)[H:H + tm, :].astype(BF16)
        dz_ref[2] = (dv * zc)[H:H + tm, :].astype(BF16)
        rs = lambda x: jnp.sum(x[H:H + tm, :], axis=0, keepdims=True)
        yh = ya * _rstd(ya)
        dw = jnp.concatenate([rs(dc * v2), rs(dc * v1), rs(dc * v)], axis=0)
        dbias, dg = rs(dc), rs(d * yh)

        @pl.when(m == 0)
        def _():
            dw_ref[...] = dw
            db_ref[...] = dbias
            dg_ref[...] = dg

        @pl.when(m > 0)
        def _():
            dw_ref[...] += dw
            db_ref[...] += dbias
            dg_ref[...] += dg

    row = lambda r: pl.BlockSpec((r, C), lambda m: (0, 0))
    specs = [*_conv_specs(L, tm, C, 0), *_conv_specs(L, tm, C, 1), *_conv_specs(L, tm, C, 2), *_conv_specs(L, tm, C, 0)]
    return pl.pallas_call(
        body, name="conv_bwd", grid=(L // tm,),
        in_specs=specs + [row(3), row(1), row(1)],
        out_specs=[pl.BlockSpec((3, tm, C), lambda m: (0, m, 0)), row(3), row(1), row(1)],
        out_shape=[S((3, L, C), BF16), S((3, C), F32), S((1, C), F32), S((1, C), F32)],
        compiler_params=_params("arbitrary"),
    )(z, z, z, z, z, z, z, z, z, dyn, dyn, dyn, conv_w, conv_b.reshape(1, C), gnorm.reshape(1, C))


def glu_bwd(y, w, b, gnorm, dn):
    L, C = y.shape
    tm = _tile(L, 256)

    def body(y_ref, w_ref, b_ref, g_ref, d_ref, dy_ref, dpre_ref, zg_ref, db_ref, dg_ref):
        m = pl.program_id(0)
        yv = y_ref[...]
        zg, t = _gelu(yv)
        zgb = zg.astype(BF16)
        sg = _sigmoid(_dot(zgb, w_ref[...], 1, 0) + b_ref[...])
        out = zg * sg
        dout, dg = _rms_bwd(out, g_ref[...], d_ref[...])
        dpre = dout * zg * sg * (1.0 - sg)
        dpre_b = dpre.astype(BF16)
        dzg = dout * sg + _dot(dpre_b, w_ref[...], 1, 1)
        dt = (1.0 - t * t) * _GELU_C * (1.0 + 3.0 * 0.044715 * yv * yv)
        dy_ref[...] = dzg * (0.5 * (1.0 + t) + 0.5 * yv * dt)
        dpre_ref[...] = dpre_b
        zg_ref[...] = zgb
        dbias = jnp.sum(dpre, axis=0, keepdims=True)

        @pl.when(m == 0)
        def _():
            db_ref[...] = dbias
            dg_ref[...] = dg

        @pl.when(m > 0)
        def _():
            db_ref[...] += dbias
            dg_ref[...] += dg

    tile = pl.BlockSpec((tm, C), lambda m: (m, 0))
    row = pl.BlockSpec((1, C), lambda m: (0, 0))
    return pl.pallas_call(
        body, name="glu_bwd", grid=(L // tm,),
        in_specs=[tile, pl.BlockSpec((C, C), lambda m: (0, 0)), row, row, tile],
        out_specs=[tile, tile, tile, row, row],
        out_shape=[S((L, C), F32), S((L, C), BF16), S((L, C), BF16), S((1, C), F32), S((1, C), F32)],
        compiler_params=_params("arbitrary"),
    )(y, w, b.reshape(1, C), gnorm.reshape(1, C), dn)


def _scan_bwd(gr_ref, gi_ref, hr_ref, hi_ref, lr, li, n_steps):
    W = gr_ref.shape[1]
    zero = jnp.zeros((N_SEG, W), F32)
    lic = -li

    def local(i, c):
        r = pl.multiple_of((n_steps - 1 - i) * N_SEG, N_SEG)
        pr, pi = _cmul(lr, lic, c[0], c[1])
        nr = pr + gr_ref[pl.ds(r, N_SEG), :]
        ni = pi + gi_ref[pl.ds(r, N_SEG), :]
        gr_ref[pl.ds(r, N_SEG), :] = nr
        gi_ref[pl.ds(r, N_SEG), :] = ni
        return nr, ni

    fr, fi = lax.fori_loop(0, n_steps, local, (zero, zero))
    qr, qi = _cpow(lr, lic, n_steps)
    row = lax.broadcasted_iota(jnp.int32, (N_SEG, W), 0)
    cr, ci = zero, zero
    for seg in range(N_SEG - 2, -1, -1):
        tr, ti = _cmul(qr, qi, cr, ci)
        sr = pltpu.roll(fr + tr, N_SEG - 1, 0)
        si = pltpu.roll(fi + ti, N_SEG - 1, 0)
        cr = jnp.where(row == seg, sr, cr)
        ci = jnp.where(row == seg, si, ci)

    def fix(i, c):
        pwr, pwi, ar, ai = c
        t = n_steps - 1 - i
        r = pl.multiple_of(t * N_SEG, N_SEG)
        pwr, pwi = _cmul(lr, lic, pwr, pwi)
        xr, xi = _cmul(pwr, pwi, cr, ci)
        g_r = gr_ref[pl.ds(r, N_SEG), :] + xr
        g_i = gi_ref[pl.ds(r, N_SEG), :] + xi
        gr_ref[pl.ds(r, N_SEG), :] = g_r
        gi_ref[pl.ds(r, N_SEG), :] = g_i
        rp = pl.multiple_of(jnp.maximum(t - 1, 0) * N_SEG, N_SEG)
        hpr = hr_ref[pl.ds(rp, N_SEG), :]
        hpi = hi_ref[pl.ds(rp, N_SEG), :]
        live = t > 0
        ar = ar + jnp.where(live, hpr * g_r + hpi * g_i, 0.0)
        ai = ai + jnp.where(live, hpr * g_i - hpi * g_r, 0.0)
        return pwr, pwi, ar, ai

    _, _, ar, ai = lax.fori_loop(0, n_steps, fix, (jnp.ones((N_SEG, W), F32), zero, zero, zero))
    last = pl.ds((n_steps - 1) * N_SEG, N_SEG)
    hpr = jnp.where(row == 0, 0.0, pltpu.roll(hr_ref[last, :], 1, 0))
    hpi = jnp.where(row == 0, 0.0, pltpu.roll(hi_ref[last, :], 1, 0))
    g_r, g_i = gr_ref[pl.ds(0, N_SEG), :], gi_ref[pl.ds(0, N_SEG), :]
    ar = ar + hpr * g_r + hpi * g_i
    ai = ai + hpr * g_i - hpi * g_r
    return jnp.sum(ar, axis=0, keepdims=True), jnp.sum(ai, axis=0, keepdims=True)


def ssm_bwd(us, dy, lam, bre, bim, cre, cim, dvec):
    L = us.shape[0]
    n_steps = L // N_SEG
    sp = _ssm_specs(L)

    def body(u_ref, dy_ref, lam_ref, bre_ref, bim_ref, cre_ref, cim_ref, d_ref,
             du_ref, dlam_ref, dbre_ref, dbim_ref, dcre_ref, dcim_ref, dd_ref, hr, hi, gr, gi):
        u = u_ref[...]
        ub = u.astype(BF16)
        dyv = dy_ref[...]
        dyb = dyv.astype(BF16)
        hr[...] = _dot(ub, bre_ref[0], 1, 0)
        hi[...] = _dot(ub, bim_ref[0], 1, 0)
        lr = jnp.broadcast_to(lam_ref[0:1, :], (N_SEG, 512))
        li = jnp.broadcast_to(lam_ref[1:2, :], (N_SEG, 512))
        _scan_fwd(hr, hi, lr, li, n_steps)
        dcre_ref[0] = _dot(hr[...].astype(BF16), dyb, 0, 0)
        dcim_ref[0] = -_dot(hi[...].astype(BF16), dyb, 0, 0)
        gr[...] = _dot(dyb, cre_ref[0], 1, 1)
        gi[...] = -_dot(dyb, cim_ref[0], 1, 1)
        dlr, dli = _scan_bwd(gr, gi, hr, hi, lr, li, n_steps)
        dlam_ref[...] = jnp.concatenate([dlr, dli], axis=0)
        grb, gib = gr[...].astype(BF16), gi[...].astype(BF16)
        du_ref[...] = _dot(grb, bre_ref[0], 1, 1) + _dot(gib, bim_ref[0], 1, 1) + d_ref[...] * dyv
        dbre_ref[0] = _dot(ub, grb, 0, 0)
        dbim_ref[0] = _dot(ub, gib, 0, 0)
        dd_ref[...] = jnp.sum(dyv * u, axis=0, keepdims=True)

    big = pltpu.VMEM((L, 512), F32)
    return pl.pallas_call(
        body, name="ssm_bwd", grid=(4,),
        in_specs=[sp["u"], sp["u"], sp["lam"], sp["bmat"], sp["bmat"], sp["cmat"], sp["cmat"], sp["d"]],
        out_specs=[sp["u"], sp["lam"], sp["bmat"], sp["bmat"], sp["cmat"], sp["cmat"], sp["d"]],
        out_shape=[S((L, 512), F32), S((2, 2048), F32), S((4, 128, 512), F32), S((4, 128, 512), F32),
                   S((4, 512, 128), F32), S((4, 512, 128), F32), S((1, 512), F32)],
        scratch_shapes=[big, big, big, big], compiler_params=_params("parallel"),
    )(us, dy, lam, bre, bim, cre, cim, dvec)


def _discretize(ar, ai, log_dt, br, bi):
    dt = jnp.exp(log_dt)
    mag = jnp.exp(ar * dt)
    ph = ai * dt
    lr, li = mag * jnp.cos(ph), mag * jnp.sin(ph)
    nr, ni = lr - 1.0, li
    den = ar * ar + ai * ai
    fr = (nr * ar + ni * ai) / den
    fi = (ni * ar - nr * ai) / den
    return lr, li, fr[..., None] * br - fi[..., None] * bi, fr[..., None] * bi + fi[..., None] * br


def ssm_prep(ar, ai, log_dt, br, bi):
    G, P, H = br.shape

    def body(ar_ref, ai_ref, dt_ref, br_ref, bi_ref, lr_ref, li_ref, bbr_ref, bbi_ref):
        lr_ref[...], li_ref[...], bbr_ref[...], bbi_ref[...] = _discretize(
            ar_ref[...], ai_ref[...], dt_ref[...], br_ref[...], bi_ref[...])

    return pl.pallas_call(
        body, name="ssm_prep",
        out_shape=[S((G, P), F32), S((G, P), F32), S((G, P, H), F32), S((G, P, H), F32)],
    )(ar, ai, log_dt.reshape(G, 1), br, bi)


def ssm_prep_bwd(ar, ai, log_dt, br, bi, dlr, dli, dbbr, dbbi):
    G, P, H = br.shape

    def body(ar_ref, ai_ref, dt_ref, br_ref, bi_ref, dlr_ref, dli_ref, dbbr_ref, dbbi_ref,
             dar_ref, dai_ref, ddt_ref, dbr_ref, dbi_ref):
        _, vjp = jax.vjp(_discretize, ar_ref[...], ai_ref[...], dt_ref[...], br_ref[...], bi_ref[...])
        dar_ref[...], dai_ref[...], ddt_ref[...], dbr_ref[...], dbi_ref[...] = vjp(
            (dlr_ref[...], dli_ref[...], dbbr_ref[...], dbbi_ref[...]))

    return pl.pallas_call(
        body, name="ssm_prep_bwd",
        out_shape=[S((G, P), F32), S((G, P), F32), S((G, 1), F32), S((G, P, H), F32), S((G, P, H), F32)],
    )(ar, ai, log_dt.reshape(G, 1), br, bi, dlr, dli, dbbr, dbbi)


def _block_diag(x):
    j, n, R, C = x.shape
    eye = jnp.eye(n, dtype=x.dtype)
    return (x[:, :, :, None, :] * eye[None, :, None, :, None]).reshape(j, n * R, n * C)


def _block_diag_take(x, R, C):
    j = x.shape[0]
    n = x.shape[1] // R
    x5 = x.reshape(j, n, R, n, C)
    return jnp.stack([x5[:, i, :, i, :] for i in range(n)], axis=1)


def _to_segments(x):
    L, C = x.shape
    return x.reshape(N_SEG, L // N_SEG, C).transpose(1, 0, 2).reshape(L, C)


def _from_segments(x):
    L, C = x.shape
    return x.reshape(L // N_SEG, N_SEG, C).transpose(1, 0, 2).reshape(L, C)


BIG = ("ffn1_w_gate", "ffn1_w_up", "ffn1_w_down", "w_in", "glu_w", "w_out",
       "ffn2_w_gate", "ffn2_w_up", "ffn2_w_down", "ple_w_gate", "ple_w_proj")
SMALL = ("ffn1_norm", "mix_norm", "conv_w", "conv_b", "ssm_A_re", "ssm_A_im", "ssm_B_re", "ssm_B_im", "ssm_C_re", "ssm_C_im",
         "ssm_D", "ssm_log_dt", "glu_b", "conv_out_norm", "ssm_out_norm", "ffn2_norm", "ple_norm")


def _ssm_mats(w):
    G, P, H = w["ssm_B_re"].shape
    lr, li, bbr, bbi = ssm_prep(w["ssm_A_re"], w["ssm_A_im"], w["ssm_log_dt"], w["ssm_B_re"], w["ssm_B_im"])
    lam = jnp.stack([lr.reshape(G * P), li.reshape(G * P)])
    bmat = lambda bb: _block_diag(bb.reshape(4, G // 4, P, H).transpose(0, 1, 3, 2)).astype(BF16)
    cmat = lambda c: _block_diag(c.reshape(4, G // 4, H, P).transpose(0, 1, 3, 2)).astype(BF16)
    return lam, bmat(bbr), bmat(bbi), cmat(w["ssm_C_re"]), cmat(w["ssm_C_im"]), w["ssm_D"].reshape(1, G * H)


def layer_fwd(h0, pb, w, before_last):
    L, D = h0.shape
    u1 = rmsnorm_fwd(h0, w["ffn1_norm"])
    a1, b1, s1 = ffn_up(u1, w["ffn1_w_gate"], w["ffn1_w_up"])
    h1, u2 = mm_shard_k(s1, w["ffn1_w_down"], h0, 0.5, w["mix_norm"])
    z = mm_shard_n(u2, w["w_in"], F32)
    ya_n = conv_fwd(z, w["conv_w"], w["conv_b"], w["conv_out_norm"])
    us = _to_segments(z[3])
    mats = _ssm_mats(w)
    y = ssm_fwd(us, *mats)
    ys_n = glu_fwd(y, w["glu_w"], w["glu_b"], w["ssm_out_norm"])
    ycat = jnp.stack([ya_n, _from_segments(ys_n)])
    h2, u3 = mm_shard_k(ycat, w["w_out"], h1, 1.0, w["ffn2_norm"])
    a2, b2, s2 = ffn_up(u3, w["ffn2_w_gate"], w["ffn2_w_up"])
    h3, un = mm_shard_k(s2, w["ffn2_w_down"], h2, 0.5, w["ple_norm"])
    h4 = ple_fwd(un, pb, w["ple_w_gate"], w["ple_w_proj"], h3, before_last(h3))
    saved = dict(h0=h0, u1=u1, a1=a1, b1=b1, s1=s1, h1=h1, u2=u2, z=z, us=us, mats=mats, y=y, ycat=ycat,
                 h2=h2, u3=u3, a2=a2, b2=b2, s2=s2, h3=h3, un=un)
    return h4, saved


def _ffn_bwd(dh, dhb, h_in, u, a, b, s, wg, wu, wd, gnorm, cast_scale):
    da, db = dact_swiglu(dhb, wd, a, b)
    g_wd = wgrad(s, dhb)
    g_wg = wgrad(da, u)
    g_wu = wgrad(db, u)
    dh_in, dhb_in, g_norm = dx_rms([(da, wg, True), (db, wu, True)], h_in, gnorm, dh, cast_scale)
    return dh_in, dhb_in, g_wg, g_wu, g_wd, g_norm


def layer_bwd(dh, pb, w, sv, token):
    L, D = dh.shape
    G, P, H = w["ssm_B_re"].shape
    dpre, dpp3 = ple_bwd(sv["un"], pb, w["ple_w_gate"], w["ple_w_proj"], dh, token)
    g_wpg = wgrad(sv["un"], dpre).reshape(N_SHARD, D // N_SHARD, D)
    g_wpp = wgrad(pb, dpp3)
    dh3, dhb3, g_nple = dx_rms([(dpre[None], w["ple_w_gate"][None], False)], sv["h3"], w["ple_norm"], dh, 0.5)
    dh2, dhb, g_wg2, g_wu2, g_wd2, g_nffn2 = _ffn_bwd(dh3, dhb3, sv["h2"], sv["u3"], sv["a2"], sv["b2"], sv["s2"],
                                                      w["ffn2_w_gate"], w["ffn2_w_up"], w["ffn2_w_down"], w["ffn2_norm"], 1.0)
    dyn = dact_plain(dhb, w["w_out"])
    g_wout = wgrad(sv["ycat"], dhb).reshape(N_SHARD, -1, D)
    dz_abc, g_convw, g_convb, g_nconv = conv_bwd(sv["z"], w["conv_w"], w["conv_b"], w["conv_out_norm"], dyn)
    dy, dpre_g, zg, g_glub, g_nssm = glu_bwd(sv["y"], w["glu_w"], w["glu_b"], w["ssm_out_norm"], _to_segments(dyn[1]))
    C = zg.shape[1]
    g_gluw = wgrad(zg, dpre_g).reshape(N_SHARD, C // N_SHARD, C)
    dus, dlam, dbre, dbim, dcre, dcim, dd = ssm_bwd(sv["us"], dy, *sv["mats"])
    take_b = lambda m: _block_diag_take(m, H, P).transpose(0, 1, 3, 2).reshape(G, P, H)
    take_c = lambda m: _block_diag_take(m, P, H).transpose(0, 1, 3, 2).reshape(G, H, P)
    g_ar, g_ai, g_dt, g_br, g_bi = ssm_prep_bwd(
        w["ssm_A_re"], w["ssm_A_im"], w["ssm_log_dt"], w["ssm_B_re"], w["ssm_B_im"],
        dlam[0].reshape(G, P), dlam[1].reshape(G, P), take_b(dbre), take_b(dbim))
    dz3 = jnp.concatenate([dz_abc, _from_segments(dus).astype(BF16)[None]], axis=0)
    g_win = wgrad(sv["u2"], dz3)
    dh1, dhb1, g_nmix = dx_rms([(dz3, w["w_in"], False)], sv["h1"], w["mix_norm"], dh2, 0.5)
    dh0, _, g_wg1, g_wu1, g_wd1, g_nffn1 = _ffn_bwd(dh1, dhb1, sv["h0"], sv["u1"], sv["a1"], sv["b1"], sv["s1"],
                                                    w["ffn1_w_gate"], w["ffn1_w_up"], w["ffn1_w_down"], w["ffn1_norm"], 1.0)
    big = [g_wg1, g_wu1, g_wd1, g_win, g_gluw, g_wout, g_wg2, g_wu2, g_wd2, g_wpg, g_wpp]
    small = dict(ffn1_norm=g_nffn1, mix_norm=g_nmix, conv_w=g_convw, conv_b=g_convb, ssm_A_re=g_ar, ssm_A_im=g_ai,
                 ssm_B_re=g_br, ssm_B_im=g_bi, ssm_C_re=take_c(dcre), ssm_C_im=take_c(dcim), ssm_D=dd,
                 ssm_log_dt=g_dt, glu_b=g_glub, conv_out_norm=g_nconv, ssm_out_norm=g_nssm, ffn2_norm=g_nffn2,
                 ple_norm=g_nple)
    return dh0, big, small


def local_step(x, p, target, final_norm, weights_of, before_last, on_grads):
    depth = p.shape[0]
    h = x
    layers, saved, pbs = [], [], []
    for i in range(depth):
        w = weights_of(i, h)
        pb = p[i].astype(BF16)
        h, sv = layer_fwd(h, pb, w, functools.partial(before_last, i))
        layers.append(w)
        saved.append(sv)
        pbs.append(pb)
    loss_part, dh, g_final = loss_head(h, final_norm, target)
    smalls = [None] * depth
    token = jnp.zeros((8, 128), F32)
    for i in reversed(range(depth)):
        dh, big, smalls[i] = layer_bwd(dh, pbs[i], layers[i], saved[i], token)
        token = on_grads(i, big, dh)
    return loss_part, dh, smalls, g_final


ROW_TILE_MAX = 512


STREAM_TILE_MAX = 176


def _row_tile(rows, cap=ROW_TILE_MAX):
    for t in range(cap - cap % 16, 0, -16):
        if rows % t == 0:
            return t
    return rows


def elementwise(fn, ins, out_dtypes, name):
    rows, cols = ins[0].shape
    tr = _row_tile(rows)
    n_in = len(ins)

    def body(*refs):
        outs = fn(*[r[...] for r in refs[:n_in]])
        for o_ref, o in zip(refs[n_in:], outs):
            o_ref[...] = o.astype(o_ref.dtype)

    spec = pl.BlockSpec((tr, cols), lambda i: (i, 0))
    return pl.pallas_call(
        body, name=name, grid=(rows // tr,), in_specs=[spec] * n_in, out_specs=[spec] * len(out_dtypes),
        out_shape=[S((rows, cols), d) for d in out_dtypes], compiler_params=_params("parallel"),
    )(*ins)


def _adamw(w, g, m, v):
    m = ADAM_B1 * m + (1.0 - ADAM_B1) * g
    v = ADAM_B2 * v + (1.0 - ADAM_B2) * (g * g)
    m_hat = m / (1.0 - ADAM_B1 ** ADAM_STEP)
    v_hat = v / (1.0 - ADAM_B2 ** ADAM_STEP)
    delta = -ADAM_LR * (m_hat / (jnp.sqrt(v_hat) + ADAM_EPS) + ADAM_WD * w)
    return delta, m, v


ANY = pl.BlockSpec(memory_space=pl.ANY)


def _mesh_pos():
    return lax.axis_index("x"), lax.axis_index("y"), lax.axis_index("c")


def _other_chips(x, y):
    return [(1 - x, y), (x, 1 - y), (1 - x, 1 - y)]


def _remote(src, dst, send_sem, recv_sem, device):
    return pltpu.make_async_remote_copy(src_ref=src, dst_ref=dst, send_sem=send_sem, recv_sem=recv_sem,
                                        device_id=device, device_id_type=MESH)


def gather_weights(ws):
    n = len(ws)

    def body(*refs):
        outs = refs[n:2 * n]
        send_sems, recv_sems = refs[2 * n:]
        x, y, c = _mesh_pos()
        me_s = 2 * x + y
        sibling = (x, y, 1 - c)
        chips = _other_chips(x, y)
        n_half = outs[0].shape[0] // 2
        mine, other = pl.ds(c * n_half, n_half), pl.ds((1 - c) * n_half, n_half)
        sent = []
        for t in range(n):
            for j, (cx, cy) in enumerate(chips):
                blk = outs[t].at[mine, me_s]
                cp = _remote(blk, blk, send_sems.at[t, j], recv_sems.at[t, j], (cx, cy, c))
                cp.start()
                sent.append(cp)
        for j, (cx, cy) in enumerate(chips):
            for t in range(n):
                blk = outs[t].at[mine, 2 * cx + cy]
                _remote(blk, blk, send_sems.at[t, j], recv_sems.at[t, j], (cx, cy, c)).wait_recv()
                cp = _remote(blk, blk, send_sems.at[t, 3 + j], recv_sems.at[t, 3 + j], sibling)
                cp.start()
                sent.append(cp)
        for j, (cx, cy) in enumerate(chips):
            for t in range(n):
                blk = outs[t].at[other, 2 * cx + cy]
                _remote(blk, blk, send_sems.at[t, 3 + j], recv_sems.at[t, 3 + j], sibling).wait_recv()
        for cp in sent:
            cp.wait_send()

    return pl.pallas_call(
        body, name="gather_weights", in_specs=[ANY] * n, out_specs=[ANY] * n,
        out_shape=[S(w.shape, w.dtype) for w in ws], input_output_aliases={t: t for t in range(n)},
        scratch_shapes=[pltpu.SemaphoreType.DMA((n, 6)), pltpu.SemaphoreType.DMA((n, 6))],
    )(*ws)


HBM = pl.BlockSpec(memory_space=pltpu.HBM)
SEM = pl.BlockSpec(memory_space=pltpu.SEMAPHORE)
VMEM_WHOLE = pl.BlockSpec(memory_space=pltpu.VMEM)
SPLIT_COPY = pltpu.CompilerParams(has_side_effects=pltpu.SideEffectType.DATAFLOW_SIDE_EFFECTING)


def _hbm(x):
    return pltpu.with_memory_space_constraint(x, pltpu.HBM)


def _half_rows(ref, c):
    r2 = ref.shape[1] // 2
    return pl.ds(pl.multiple_of(c * r2, 8), r2)


def _gather_copies(bufs, send_sems, recv_sems, forward):
    x, y, c = _mesh_pos()
    copies = []
    for t in range(len(bufs)):
        rows = _half_rows(bufs[t], c)
        for j, (cx, cy) in enumerate(_other_chips(x, y)):
            blk = bufs[t].at[2 * cx + cy if forward else 2 * x + y, rows]
            peer = (x, y, 1 - c) if forward else (cx, cy, c)
            copies.append(_remote(blk, blk, send_sems.at[3 * t + j], recv_sems.at[3 * t + j], peer))
    return copies


def gather_start(bufs, layer, after, forward=False):
    n, k = len(bufs), len(after)

    def body(*refs):
        ins, send_sems, recv_sems, token = refs[:n], refs[n + k], refs[n + k + 1], refs[2 * n + k + 2]
        for cp in _gather_copies(ins, send_sems, recv_sems, forward):
            cp.start()
        token[...] = jnp.zeros_like(token)

    outs = pl.pallas_call(
        body, name=f"{'forward' if forward else 'gather'}_start_{layer}", in_specs=[HBM] * n + [ANY] * k,
        out_specs=[SEM, SEM] + [HBM] * n + [VMEM_WHOLE],
        out_shape=[pltpu.SemaphoreType.DMA((3 * n,)), pltpu.SemaphoreType.DMA((3 * n,))]
        + [pltpu.HBM(b.shape, b.dtype) for b in bufs] + [S((8, 128), F32)],
        input_output_aliases={t: t + 2 for t in range(n)}, compiler_params=SPLIT_COPY,
    )(*[_hbm(b) for b in bufs], *after)
    return outs[0], outs[1], list(outs[2:2 + n]), outs[2 + n]


def gather_wait(bufs, send_sems, recv_sems, after, layer, forward=False):
    n, n_after = len(bufs), len(after)

    def body(*refs):
        ins, send_ref, recv_ref = refs[:n], refs[n], refs[n + 1]
        for cp in _gather_copies(ins, send_ref, recv_ref, forward):
            cp.wait_send()
            cp.wait_recv()

    outs = pl.pallas_call(
        body, name=f"{'forward' if forward else 'gather'}_wait_{layer}", in_specs=[HBM] * n + [SEM, SEM] + [ANY] * n_after,
        out_specs=[HBM] * n,
        out_shape=[pltpu.HBM(b.shape, b.dtype) for b in bufs],
        input_output_aliases={t: t for t in range(n)}, compiler_params=SPLIT_COPY,
    )(*bufs, send_sems, recv_sems, *after)
    return list(outs)


def gather_forward(bufs, after):
    n, k = len(bufs), len(after)

    def body(*refs):
        outs = refs[n + k:2 * n + k]
        send_sems, recv_sems = refs[2 * n + k:]
        copies = _gather_copies(outs, send_sems, recv_sems, True)
        for cp in copies:
            cp.start()
        for cp in copies:
            cp.wait()

    return pl.pallas_call(
        body, name="gather_forward", in_specs=[ANY] * (n + k), out_specs=[ANY] * n,
        out_shape=[S(b.shape, b.dtype) for b in bufs], input_output_aliases={t: t for t in range(n)},
        scratch_shapes=[pltpu.SemaphoreType.DMA((3 * n,)), pltpu.SemaphoreType.DMA((3 * n,))],
    )(*bufs, *after)


def chips_start(sums, layer):
    n = len(sums)
    lands = [lax.empty((3,) + s.shape[1:], s.dtype) for s in sums]

    def body(*refs):
        a, land, send_sems, recv_sems, token = refs[:n], refs[n:2 * n], refs[2 * n], refs[2 * n + 1], refs[4 * n + 2]
        x, y, c = _mesh_pos()
        for t in range(n):
            for j, (cx, cy) in enumerate(_other_chips(x, y)):
                _remote(a[t].at[2 * cx + cy], land[t].at[j], send_sems.at[3 * t + j], recv_sems.at[3 * t + j], (cx, cy, c)).start()
        token[...] = jnp.zeros_like(token)

    outs = pl.pallas_call(
        body, name=f"chips_start_{layer}", in_specs=[HBM] * (2 * n), out_specs=[SEM, SEM] + [HBM] * (2 * n) + [VMEM_WHOLE],
        out_shape=[pltpu.SemaphoreType.DMA((3 * n,)), pltpu.SemaphoreType.DMA((3 * n,))]
        + [pltpu.HBM(b.shape, b.dtype) for b in sums + lands] + [S((8, 128), F32)],
        input_output_aliases={t: t + 2 for t in range(2 * n)}, compiler_params=SPLIT_COPY,
    )(*[_hbm(b) for b in sums + lands])
    return outs[0], outs[1], list(outs[2:2 + n]), list(outs[2 + n:2 + 2 * n]), outs[2 + 2 * n]


def chips_wait(sums, lands, send_sems, recv_sems, after, layer):
    n, n_after = len(sums), len(after)

    def body(*refs):
        a, land, send_ref, recv_ref = refs[:n], refs[n:2 * n], refs[2 * n], refs[2 * n + 1]
        x, y, c = _mesh_pos()
        for t in range(n):
            for j, (cx, cy) in enumerate(_other_chips(x, y)):
                cp = _remote(a[t].at[2 * cx + cy], land[t].at[j], send_ref.at[3 * t + j], recv_ref.at[3 * t + j], (cx, cy, c))
                cp.wait_send()
                cp.wait_recv()

    outs = pl.pallas_call(
        body, name=f"chips_wait_{layer}", in_specs=[HBM] * (2 * n) + [SEM, SEM] + [ANY] * n_after, out_specs=[HBM] * (2 * n),
        out_shape=[pltpu.HBM(b.shape, b.dtype) for b in sums + lands],
        input_output_aliases={t: t for t in range(2 * n)}, compiler_params=SPLIT_COPY,
    )(*sums, *lands, send_sems, recv_sems, *after)
    return list(outs[n:])


def cast_place_layer(w, layer, pos, dtype, token):
    _, r, c = w.shape
    tr = _row_tile(r)

    def body(pos_ref, w_ref, tok_ref, o_ref):
        o_ref[0] = (w_ref[0] + tok_ref[0:1, 0:1]).astype(dtype)

    return pl.pallas_call(
        body, name="cast_place_layer",
        grid_spec=pltpu.PrefetchScalarGridSpec(
            num_scalar_prefetch=1, grid=(r // tr,),
            in_specs=[pl.BlockSpec((1, tr, c), lambda i, pos: (layer, i, 0)), pl.BlockSpec((8, 128), lambda i, pos: (0, 0))],
            out_specs=pl.BlockSpec((1, tr, c), lambda i, pos: (pos[1], i, 0))),
        out_shape=S((N_SHARD, r, c), dtype), compiler_params=_params("parallel"),
    )(pos, w, token)


def cast_place(w, pos, dtype):
    layers, r, c = w.shape
    tr = _row_tile(r)

    def body(pos_ref, w_ref, o_ref):
        o_ref[0, 0] = w_ref[0].astype(dtype)

    return pl.pallas_call(
        body, name="cast_place",
        grid_spec=pltpu.PrefetchScalarGridSpec(
            num_scalar_prefetch=1, grid=(layers, r // tr),
            in_specs=[pl.BlockSpec((1, tr, c), lambda l, i, pos: (l, i, 0))],
            out_specs=pl.BlockSpec((1, 1, tr, c), lambda l, i, pos: (l, pos[1], i, 0))),
        out_shape=S((layers, N_SHARD, r, c), dtype), compiler_params=_params("parallel", "parallel"),
    )(pos, w)


def _pair_copy(g_ref, got_ref, send_sem, recv_sem):
    x, y, c = _mesh_pos()
    r2 = g_ref.shape[1] // 2
    give = pl.ds(pl.multiple_of((1 - c) * r2, 8), r2)
    return _remote(g_ref.at[:, give], got_ref, send_sem, recv_sem, (x, y, 1 - c))


def reduce_pair(gs, after):
    n, k = len(gs), len(after)

    def body(*refs):
        ins, got = refs[:n], refs[n + k:2 * n + k]
        send_sems, recv_sems = refs[2 * n + k:]
        copies = [_pair_copy(ins[t], got[t], send_sems.at[t], recv_sems.at[t]) for t in range(n)]
        for cp in copies:
            cp.start()
        for cp in copies:
            cp.wait()

    return pl.pallas_call(
        body, name="reduce_pair", in_specs=[ANY] * (n + k), out_specs=[ANY] * n,
        out_shape=[S((g.shape[0], g.shape[1] // 2, g.shape[2]), g.dtype) for g in gs],
        scratch_shapes=[pltpu.SemaphoreType.DMA((n,)), pltpu.SemaphoreType.DMA((n,))],
    )(*gs, *after)


def pair_sum(g, got, pos):
    ns, r2, c = got.shape
    tr = _row_tile(r2)
    n_i = r2 // tr

    def body(pos_ref, g_ref, got_ref, sum_ref, own_ref):
        s = pl.program_id(1)
        v = g_ref[0].astype(F32) + got_ref[0].astype(F32)
        sum_ref[0] = v.astype(BF16)

        @pl.when(s == pos_ref[1])
        def _():
            own_ref[...] = v

    return pl.pallas_call(
        body, name="pair_sum",
        grid_spec=pltpu.PrefetchScalarGridSpec(
            num_scalar_prefetch=1, grid=(n_i, ns),
            in_specs=[pl.BlockSpec((1, tr, c), lambda i, s, pos: (s, pos[0] * n_i + i, 0)),
                      pl.BlockSpec((1, tr, c), lambda i, s, pos: (s, i, 0))],
            out_specs=[pl.BlockSpec((1, tr, c), lambda i, s, pos: (s, i, 0)), pl.BlockSpec((tr, c), lambda i, s, pos: (i, 0))]),
        out_shape=[S((ns, r2, c), BF16), S((r2, c), F32)], compiler_params=_params("parallel", "arbitrary"),
    )(pos, g, got)


def chip_sum(own, p2, pos):
    r2, c = own.shape
    tr = _row_tile(r2, STREAM_TILE_MAX)
    n_i = r2 // tr

    def body(pos_ref, own_ref, a_ref, b_ref, c_ref, o_ref):
        o_ref[...] = own_ref[...] + a_ref[0].astype(F32) + b_ref[0].astype(F32) + c_ref[0].astype(F32)

    peer = lambda j: pl.BlockSpec((1, tr, c), lambda i, pos: (j, i, 0))
    return pl.pallas_call(
        body, name="chip_sum",
        grid_spec=pltpu.PrefetchScalarGridSpec(
            num_scalar_prefetch=1, grid=(n_i,),
            in_specs=[pl.BlockSpec((tr, c), lambda i, pos: (i, 0)), peer(0), peer(1), peer(2)],
            out_specs=pl.BlockSpec((tr, c), lambda i, pos: (pos[0] * n_i + i, 0))),
        out_shape=S((2 * r2, c), F32), compiler_params=_params("parallel"),
    )(pos, own, p2, p2, p2)


def exchange_halves(rs):
    n = len(rs)

    def body(*refs):
        outs = refs[n:2 * n]
        send_sems, recv_sems = refs[2 * n:]
        x, y, c = _mesh_pos()
        copies = []
        for t in range(n):
            r2 = outs[t].shape[0] // 2
            rows = outs[t].at[pl.ds(pl.multiple_of(c * r2, 8), r2)]
            cp = _remote(rows, rows, send_sems.at[t], recv_sems.at[t], (x, y, 1 - c))
            cp.start()
            copies.append(cp)
        for cp in copies:
            cp.wait()

    return pl.pallas_call(
        body, name="exchange_halves", in_specs=[ANY] * n, out_specs=[ANY] * n,
        out_shape=[S(r.shape, r.dtype) for r in rs], input_output_aliases={t: t for t in range(n)},
        scratch_shapes=[pltpu.SemaphoreType.DMA((n,)), pltpu.SemaphoreType.DMA((n,))],
    )(*rs)


def allreduce_small(vec):
    R = vec.shape[0]
    H = R // 2

    def body(x_ref, o_ref, pair_buf, chip_buf, send_sems, recv_sems):
        x, y, c = _mesh_pos()
        me_s = 2 * x + y
        sibling = (x, y, 1 - c)
        mine = pl.ds(pl.multiple_of(c * H, 8), H)
        give = pl.ds(pl.multiple_of((1 - c) * H, 8), H)
        cp = _remote(x_ref.at[give], pair_buf, send_sems.at[0], recv_sems.at[0], sibling)
        cp.start()
        cp.wait()
        chip_buf[me_s] = x_ref[mine, :] + pair_buf[...]
        copies = []
        for j, (cx, cy) in enumerate(_other_chips(x, y)):
            cp = _remote(chip_buf.at[me_s], chip_buf.at[me_s], send_sems.at[1 + j], recv_sems.at[1 + j], (cx, cy, c))
            cp.start()
            copies.append(cp)
        for cp in copies:
            cp.wait()
        o_ref[mine, :] = (chip_buf[0] + chip_buf[1]) + (chip_buf[2] + chip_buf[3])
        cp = _remote(o_ref.at[mine], o_ref.at[mine], send_sems.at[4], recv_sems.at[4], sibling)
        cp.start()
        cp.wait()

    vm = pl.BlockSpec(memory_space=pltpu.VMEM)
    return pl.pallas_call(
        body, name="allreduce_small", in_specs=[vm], out_specs=vm, out_shape=S((R, 128), F32),
        scratch_shapes=[pltpu.VMEM((H, 128), F32), pltpu.VMEM((N_SHARD, H, 128), F32),
                        pltpu.SemaphoreType.DMA((5,)), pltpu.SemaphoreType.DMA((5,))],
        compiler_params=pltpu.CompilerParams(vmem_limit_bytes=VMEM_LIMIT_BYTES),
    )(vec)


def adamw_layer(w, g, m, v, layer, prev):
    _, r, c = w.shape
    tr = _row_tile(r, STREAM_TILE_MAX)

    def body(w_ref, g_ref, m_ref, v_ref, *rest):
        outs = rest[-4:]
        g_val = g_ref[...]
        outs[0][0] = g_val
        outs[1][0], outs[2][0], outs[3][0] = _adamw(w_ref[0], g_val, m_ref[0], v_ref[0])

    lay = pl.BlockSpec((1, tr, c), lambda i: (layer, i, 0))
    prev = list(prev) if prev else []
    return pl.pallas_call(
        body, name="adamw_layer", grid=(r // tr,),
        in_specs=[lay, pl.BlockSpec((tr, c), lambda i: (i, 0)), lay, lay] + [ANY] * len(prev),
        out_specs=[lay] * 4, out_shape=[S(w.shape, F32)] * 4,
        input_output_aliases={4 + k: k for k in range(len(prev))}, compiler_params=_params("parallel"),
    )(w, g, m, v, *prev)


def reduce_begin(gs, pos, layer, after):
    got = reduce_pair(gs, after)
    sums, own = zip(*[pair_sum(g, o, pos) for g, o in zip(gs, got)])
    send_sems, recv_sems, sums, lands, token = chips_start(list(sums), layer)
    return dict(own=own, sums=sums, lands=lands, sems=(send_sems, recv_sems), token=token, layer=layer)


def reduce_end(pending, pos, after):
    lands = chips_wait(pending["sums"], pending["lands"], *pending["sems"], after, pending["layer"])
    return exchange_halves([chip_sum(o, p, pos) for o, p in zip(pending["own"], lands)])


W_NAMES = ("ffn1_norm", "ffn1_w_gate", "ffn1_w_up", "ffn1_w_down", "mix_norm", "w_in", "conv_w", "conv_b", "ssm_A_re", "ssm_A_im",
           "ssm_B_re", "ssm_B_im", "ssm_C_re", "ssm_C_im", "ssm_D", "ssm_log_dt", "glu_w", "glu_b", "conv_out_norm", "ssm_out_norm",
           "w_out", "ffn2_norm", "ffn2_w_gate", "ffn2_w_up", "ffn2_w_down", "ple_norm", "ple_w_gate", "ple_w_proj", "final_norm")
SMALL_ALL = SMALL + ("final_norm",)
TRANSPOSED = ("ffn1_w_gate", "ffn1_w_up", "ffn2_w_gate", "ffn2_w_up")
PACK = ROW_TILE_MAX * 128


def _pack(parts):
    flat = jnp.concatenate([p.reshape(-1) for p in parts])
    pad = (-flat.shape[0]) % PACK
    return jnp.pad(flat, (0, pad)).reshape(-1, 128)


def _unpack(vec, shapes):
    flat = vec.reshape(-1)
    out, off = [], 0
    for shp in shapes:
        size = math.prod(shp)
        out.append(flat[off:off + size].reshape(shp))
        off += size
    return out


def _step(a):
    a = {k: jnp.swapaxes(v, 1, 2) if k.removeprefix("m_").removeprefix("v_") in TRANSPOSED else v for k, v in a.items()}
    x, p, target = a["x"][0], a["p"][:, 0], a["loss_target"][0]
    depth = p.shape[0]
    L, D = x.shape
    me_s = 2 * lax.axis_index("x") + lax.axis_index("y")

    pos = jnp.stack([lax.axis_index("c"), me_s]).astype(jnp.int32)
    conv_w = gather_weights([cast_place(a["conv_w"], pos, F32)])[0]
    started = {0: gather_start([cast_place_layer(a[n], 0, pos, BF16, jnp.zeros((8, 128), F32)) for n in BIG], 0, [conv_w])}
    placed = {l: [cast_place_layer(a[n], l, pos, BF16, started[0][3]) for n in BIG] for l in range(1, depth)}
    forwarding = {}

    def before_last(l, h3):
        if l + 1 == depth:
            return jnp.zeros((8, 128), F32)
        send_sems, recv_sems, bufs, _ = started[l + 1]
        forwarding[l + 1] = gather_start(gather_wait(bufs, send_sems, recv_sems, [h3], l + 1), l + 1, [], forward=True)
        return forwarding[l + 1][3]

    def weights_of(l, h):
        if l:
            send_sems, recv_sems, bufs, _ = forwarding[l]
            full = gather_wait(bufs, send_sems, recv_sems, [h], l, forward=True)
        else:
            send_sems, recv_sems, bufs, _ = started[0]
            landed = gather_wait(bufs, send_sems, recv_sems, [b for bs in placed.values() for b in bs], 0)
            after = [landed[0]]
            for nxt in range(1, depth):
                started[nxt] = gather_start(placed[nxt], nxt, after)
                after = [started[nxt][3]]
            full = gather_forward(landed, after)
        w = {n: a[n][l] for n in SMALL if n != "conv_w"}
        w.update(dict(zip(BIG, full)))
        C = w["glu_w"].shape[-1]
        w["glu_w"] = w["glu_w"].reshape(C, C)
        w["w_out"] = w["w_out"].reshape(2, -1, D)
        w["ple_w_gate"] = w["ple_w_gate"].reshape(D, D)
        w["conv_w"] = conv_w[l].transpose(1, 0, 2).reshape(3, -1)
        return w

    pending, first_layer_grads = {}, []

    def on_grads(l, big, dh):
        if l == 0:
            first_layer_grads.extend(big)
            return None
        pending[l] = reduce_begin(big, pos, l, [])
        return pending[l]["token"]

    loss_part, dx, smalls, g_final = local_step(x, p, target, a["final_norm"], weights_of, before_last, on_grads)
    small_shapes = [(depth,) + smalls[0][n].shape for n in SMALL] + [g_final.shape, (1,)]
    parts = [smalls[l][n] for n in SMALL for l in range(depth)] + [g_final, loss_part[0, 0:1]]
    summed_vec = allreduce_small(_pack(parts))
    pending[0] = reduce_begin(first_layer_grads, pos, 0, [summed_vec])
    stacked = [None] * len(BIG)
    for l in reversed(range(depth)):
        after = [pending[0]["token"]] if l else [s[3] for s in stacked]
        reduced = reduce_end(pending[l], pos, after)
        stacked = [adamw_layer(a[n], reduced[i], a["m_" + n], a["v_" + n], l, stacked[i]) for i, n in enumerate(BIG)]
    big_out = {n: [jnp.swapaxes(o, 1, 2) for o in outs] if n in TRANSPOSED else outs for n, outs in zip(BIG, stacked)}

    summed = _unpack(summed_vec, small_shapes)
    g_small = dict(zip(SMALL_ALL, summed[:-1]))
    loss = summed[-1][0]
    n_conv = a["conv_w"].shape[-1]
    g_small["conv_w"] = lax.dynamic_slice_in_dim(g_small["conv_w"], me_s * n_conv, n_conv, axis=2)
    g_small = {n: g_small[n].reshape(a[n].shape) for n in SMALL_ALL}
    packed = [_pack([src[n] for n in SMALL_ALL]) for src in
              ({n: a[n] for n in SMALL_ALL}, g_small, {n: a["m_" + n] for n in SMALL_ALL}, {n: a["v_" + n] for n in SMALL_ALL})]
    shapes = [a[n].shape for n in SMALL_ALL]
    d_s, m_s, v_s = [dict(zip(SMALL_ALL, _unpack(o, shapes))) for o in elementwise(_adamw, packed, [F32, F32, F32], "adamw_small")]

    outs = {n: big_out[n] if n in big_out else (g_small[n], d_s[n], m_s[n], v_s[n]) for n in W_NAMES}
    return (loss, dx[None], *[outs[n][0] for n in W_NAMES], *[outs[n][1] for n in W_NAMES],
            *[outs[n][2] for n in W_NAMES], *[outs[n][3] for n in W_NAMES])


def kernel(x, p, ffn1_norm, ffn1_w_gate, ffn1_w_up, ffn1_w_down, mix_norm, w_in, conv_w, conv_b, ssm_A_re, ssm_A_im, ssm_B_re, ssm_B_im, ssm_C_re, ssm_C_im, ssm_D, ssm_log_dt, glu_w, glu_b, conv_out_norm, ssm_out_norm, w_out, ffn2_norm, ffn2_w_gate, ffn2_w_up, ffn2_w_down, ple_norm, ple_w_gate, ple_w_proj, final_norm, loss_target, m_ffn1_norm, m_ffn1_w_gate, m_ffn1_w_up, m_ffn1_w_down, m_mix_norm, m_w_in, m_conv_w, m_conv_b, m_ssm_A_re, m_ssm_A_im, m_ssm_B_re, m_ssm_B_im, m_ssm_C_re, m_ssm_C_im, m_ssm_D, m_ssm_log_dt, m_glu_w, m_glu_b, m_conv_out_norm, m_ssm_out_norm, m_w_out, m_ffn2_norm, m_ffn2_w_gate, m_ffn2_w_up, m_ffn2_w_down, m_ple_norm, m_ple_w_gate, m_ple_w_proj, m_final_norm, v_ffn1_norm, v_ffn1_w_gate, v_ffn1_w_up, v_ffn1_w_down, v_mix_norm, v_w_in, v_conv_w, v_conv_b, v_ssm_A_re, v_ssm_A_im, v_ssm_B_re, v_ssm_B_im, v_ssm_C_re, v_ssm_C_im, v_ssm_D, v_ssm_log_dt, v_glu_w, v_glu_b, v_conv_out_norm, v_ssm_out_norm, v_w_out, v_ffn2_norm, v_ffn2_w_gate, v_ffn2_w_up, v_ffn2_w_down, v_ple_norm, v_ple_w_gate, v_ple_w_proj, v_final_norm):
    return _step(dict(locals()))
```

```python
import functools
import math

import jax
import jax.numpy as jnp
from jax import lax
from jax.experimental import pallas as pl
from jax.experimental.pallas import tpu as pltpu

F32, BF16 = jnp.float32, jnp.bfloat16
S = jax.ShapeDtypeStruct
EPS = 1e-6
N_SEG = 8
N_SHARD = 4
N_DEV = 8
VMEM_LIMIT_BYTES = 56 * 1024 * 1024
ADAM_LR, ADAM_B1, ADAM_B2, ADAM_EPS, ADAM_WD, ADAM_STEP = 0.001, 0.9, 0.999, 1e-08, 0.01, 10
MESH = pl.DeviceIdType.MESH


def _params(*sem):
    return pltpu.CompilerParams(dimension_semantics=sem if sem else None, vmem_limit_bytes=VMEM_LIMIT_BYTES)


def _dot(a, b, ca, cb):
    return lax.dot_general(a, b, (((ca,), (cb,)), ((), ())), preferred_element_type=F32)


def _sigmoid(x):
    return 1.0 / (1.0 + jnp.exp(-x))


def _rstd(x):
    return lax.rsqrt(jnp.mean(x * x, axis=-1, keepdims=True) + EPS)


def _rms_bwd(x, g, dy):
    r = _rstd(x)
    xh = x * r
    dxh = dy * g
    dx = r * (dxh - xh * jnp.mean(dxh * xh, axis=-1, keepdims=True))
    return dx, jnp.sum(dy * xh, axis=0, keepdims=True)


def _tile(n, want):
    return want if n % want == 0 else n


def rmsnorm_fwd(h, g):
    L, D = h.shape
    tm = _tile(L, 512)

    def body(h_ref, g_ref, o_ref):
        x = h_ref[...]
        o_ref[...] = (x * _rstd(x) * g_ref[...]).astype(BF16)

    return pl.pallas_call(
        body, name="rmsnorm_fwd", grid=(L // tm,),
        in_specs=[pl.BlockSpec((tm, D), lambda m: (m, 0)), pl.BlockSpec((1, D), lambda m: (0, 0))],
        out_specs=pl.BlockSpec((tm, D), lambda m: (m, 0)),
        out_shape=S((L, D), BF16), compiler_params=_params("parallel"),
    )(h, g.reshape(1, D))


def ffn_up(u, wg, wu):
    L, D = u.shape
    ns, F, _ = wg.shape
    tm = _tile(L, 512)

    def body(u_ref, wg_ref, wu_ref, a_ref, b_ref, s_ref):
        x = u_ref[...]
        a = _dot(x, wg_ref[0], 1, 1)
        b = _dot(x, wu_ref[0], 1, 1)
        a_ref[0] = a.astype(BF16)
        b_ref[0] = b.astype(BF16)
        s_ref[0] = (a * _sigmoid(a) * b).astype(BF16)

    w_spec = pl.BlockSpec((1, F, D), lambda s, m: (s, 0, 0))
    o_spec = pl.BlockSpec((1, tm, F), lambda s, m: (s, m, 0))
    return pl.pallas_call(
        body, name="ffn_up", grid=(ns, L // tm),
        in_specs=[pl.BlockSpec((tm, D), lambda s, m: (m, 0)), w_spec, w_spec],
        out_specs=[o_spec, o_spec, o_spec],
        out_shape=[S((ns, L, F), BF16)] * 3, compiler_params=_params("parallel", "parallel"),
    )(u, wg, wu)


def mm_shard_n(u, w3, out_dtype):
    L, K = u.shape
    ns, _, N = w3.shape
    tm = _tile(L, 512)

    def body(u_ref, w_ref, o_ref):
        o_ref[0] = _dot(u_ref[...], w_ref[0], 1, 0).astype(out_dtype)

    return pl.pallas_call(
        body, name="mm_shard_n", grid=(ns, L // tm),
        in_specs=[pl.BlockSpec((tm, K), lambda s, m: (m, 0)), pl.BlockSpec((1, K, N), lambda s, m: (s, 0, 0))],
        out_specs=pl.BlockSpec((1, tm, N), lambda s, m: (s, m, 0)),
        out_shape=S((ns, L, N), out_dtype), compiler_params=_params("parallel", "parallel"),
    )(u, w3)


def mm_shard_k(a3, w3, res, scale, g_next):
    nk, L, Kc = a3.shape
    N = w3.shape[2]
    tm = _tile(L, 512)

    def body(a_ref, w_ref, r_ref, g_ref, o_ref, u_ref):
        acc = _dot(a_ref[0], w_ref[0], 1, 0)
        for k in range(1, nk):
            acc += _dot(a_ref[k], w_ref[k], 1, 0)
        h = r_ref[...] + scale * acc
        o_ref[...] = h
        u_ref[...] = (h * _rstd(h) * g_ref[...]).astype(BF16)

    tile = pl.BlockSpec((tm, N), lambda m: (m, 0))
    return pl.pallas_call(
        body, name="mm_shard_k", grid=(L // tm,),
        in_specs=[pl.BlockSpec((nk, tm, Kc), lambda m: (0, m, 0)), pl.BlockSpec((nk, Kc, N), lambda m: (0, 0, 0)),
                  tile, pl.BlockSpec((1, N), lambda m: (0, 0))],
        out_specs=[tile, tile],
        out_shape=[S((L, N), F32), S((L, N), BF16)], compiler_params=_params("parallel"),
    )(a3, w3, res, g_next.reshape(1, N))


CONV_HALO = 8


def _conv_specs(L, tm, C, shard):
    nb = L // CONV_HALO
    per = tm // CONV_HALO
    main = pl.BlockSpec((1, tm, C), lambda m: (shard, m, 0))
    prev = pl.BlockSpec((1, CONV_HALO, C), lambda m: (shard, jnp.maximum(m * per - 1, 0), 0))
    nxt = pl.BlockSpec((1, CONV_HALO, C), lambda m: (shard, jnp.minimum((m + 1) * per, nb - 1), 0))
    return main, prev, nxt


def _conv_core(zb, zc, zv, w_ref, bias, grow, L):
    valid = (grow >= 0) & (grow < L)
    v = jnp.where(valid, zc * zv, 0.0)
    v1 = pltpu.roll(v, 1, 0)
    v2 = pltpu.roll(v, 2, 0)
    cb = w_ref[0:1, :] * v2 + w_ref[1:2, :] * v1 + w_ref[2:3, :] * v + bias
    return valid, v, v1, v2, cb, zb * cb


def conv_fwd(z, conv_w, conv_b, gnorm):
    _, L, C = z.shape
    tm = _tile(L, 256)
    H = CONV_HALO

    def body(zb_ref, zc_ref, zcp_ref, zv_ref, zvp_ref, w_ref, b_ref, g_ref, o_ref):
        m = pl.program_id(0)
        zc = jnp.concatenate([zcp_ref[0], zc_ref[0]], axis=0)
        zv = jnp.concatenate([zvp_ref[0], zv_ref[0]], axis=0)
        grow = m * tm - H + lax.broadcasted_iota(jnp.int32, (tm + H, C), 0)
        valid = grow >= 0
        v = jnp.where(valid, zc * zv, 0.0)
        v1 = pltpu.roll(v, 1, 0)
        v2 = pltpu.roll(v, 2, 0)
        cb = (w_ref[0:1, :] * v2 + w_ref[1:2, :] * v1 + w_ref[2:3, :] * v + b_ref[...])[H:, :]
        ya = zb_ref[0] * cb
        o_ref[...] = (ya * _rstd(ya) * g_ref[...]).astype(BF16)

    zb_m, _, _ = _conv_specs(L, tm, C, 0)
    zc_m, zc_p, _ = _conv_specs(L, tm, C, 1)
    zv_m, zv_p, _ = _conv_specs(L, tm, C, 2)
    row = lambda r: pl.BlockSpec((r, C), lambda m: (0, 0))
    return pl.pallas_call(
        body, name="conv_fwd", grid=(L // tm,),
        in_specs=[zb_m, zc_m, zc_p, zv_m, zv_p, row(3), row(1), row(1)],
        out_specs=pl.BlockSpec((tm, C), lambda m: (m, 0)),
        out_shape=S((L, C), BF16), compiler_params=_params("parallel"),
    )(z, z, z, z, z, conv_w, conv_b.reshape(1, C), gnorm.reshape(1, C))


def _cmul(ar, ai, br, bi):
    return ar * br - ai * bi, ar * bi + ai * br


def _scan_fwd(hr_ref, hi_ref, lr, li, n_steps):
    W = hr_ref.shape[1]
    zero = jnp.zeros((N_SEG, W), F32)

    def local(t, c):
        r = pl.multiple_of(t * N_SEG, N_SEG)
        pr, pi = _cmul(lr, li, c[0], c[1])
        nr = pr + hr_ref[pl.ds(r, N_SEG), :]
        ni = pi + hi_ref[pl.ds(r, N_SEG), :]
        hr_ref[pl.ds(r, N_SEG), :] = nr
        hi_ref[pl.ds(r, N_SEG), :] = ni
        return nr, ni

    fr, fi = lax.fori_loop(0, n_steps, local, (zero, zero))
    qr, qi = _cpow(lr, li, n_steps)
    row = lax.broadcasted_iota(jnp.int32, (N_SEG, W), 0)
    cr, ci = zero, zero
    for seg in range(1, N_SEG):
        tr, ti = _cmul(qr, qi, cr, ci)
        sr = pltpu.roll(fr + tr, 1, 0)
        si = pltpu.roll(fi + ti, 1, 0)
        cr = jnp.where(row == seg, sr, cr)
        ci = jnp.where(row == seg, si, ci)

    def fix(t, c):
        r = pl.multiple_of(t * N_SEG, N_SEG)
        pr, pi = _cmul(lr, li, c[0], c[1])
        ar, ai = _cmul(pr, pi, cr, ci)
        hr_ref[pl.ds(r, N_SEG), :] += ar
        hi_ref[pl.ds(r, N_SEG), :] += ai
        return pr, pi

    lax.fori_loop(0, n_steps, fix, (jnp.ones((N_SEG, W), F32), zero))


def _cpow(lr, li, n):
    rr, ri = None, None
    br, bi = lr, li
    while n:
        if n & 1:
            rr, ri = (br, bi) if rr is None else _cmul(rr, ri, br, bi)
        n >>= 1
        if n:
            br, bi = _cmul(br, bi, br, bi)
    return rr, ri


def _ssm_specs(L):
    col = lambda w: pl.BlockSpec((L, w), lambda j: (0, j))
    return dict(
        u=col(128), lam=pl.BlockSpec((2, 512), lambda j: (0, j)),
        bmat=pl.BlockSpec((1, 128, 512), lambda j: (j, 0, 0)), cmat=pl.BlockSpec((1, 512, 128), lambda j: (j, 0, 0)),
        d=pl.BlockSpec((1, 128), lambda j: (0, j)))


def ssm_fwd(us, lam, bre, bim, cre, cim, dvec):
    L = us.shape[0]
    n_steps = L // N_SEG
    sp = _ssm_specs(L)

    def body(u_ref, lam_ref, bre_ref, bim_ref, cre_ref, cim_ref, d_ref, y_ref, hr, hi):
        u = u_ref[...]
        ub = u.astype(BF16)
        hr[...] = _dot(ub, bre_ref[0], 1, 0)
        hi[...] = _dot(ub, bim_ref[0], 1, 0)
        lr = jnp.broadcast_to(lam_ref[0:1, :], (N_SEG, 512))
        li = jnp.broadcast_to(lam_ref[1:2, :], (N_SEG, 512))
        _scan_fwd(hr, hi, lr, li, n_steps)
        y_ref[...] = (_dot(hr[...].astype(BF16), cre_ref[0], 1, 0) - _dot(hi[...].astype(BF16), cim_ref[0], 1, 0)
                      + d_ref[...] * u)

    return pl.pallas_call(
        body, name="ssm_fwd", grid=(4,),
        in_specs=[sp["u"], sp["lam"], sp["bmat"], sp["bmat"], sp["cmat"], sp["cmat"], sp["d"]],
        out_specs=sp["u"], out_shape=S((L, 512), F32),
        scratch_shapes=[pltpu.VMEM((L, 512), F32), pltpu.VMEM((L, 512), F32)],
        compiler_params=_params("parallel"),
    )(us, lam, bre, bim, cre, cim, dvec)


_GELU_C = math.sqrt(2.0 / math.pi)


def _gelu(y):
    t = jnp.tanh(_GELU_C * (y + 0.044715 * y * y * y))
    return 0.5 * y * (1.0 + t), t


def glu_fwd(y, w, b, gnorm):
    L, C = y.shape
    tm = _tile(L, 512)

    def body(y_ref, w_ref, b_ref, g_ref, o_ref):
        zg, _ = _gelu(y_ref[...])
        out = zg * _sigmoid(_dot(zg.astype(BF16), w_ref[...], 1, 0) + b_ref[...])
        o_ref[...] = (out * _rstd(out) * g_ref[...]).astype(BF16)

    row = pl.BlockSpec((1, C), lambda m: (0, 0))
    return pl.pallas_call(
        body, name="glu_fwd", grid=(L // tm,),
        in_specs=[pl.BlockSpec((tm, C), lambda m: (m, 0)), pl.BlockSpec((C, C), lambda m: (0, 0)), row, row],
        out_specs=pl.BlockSpec((tm, C), lambda m: (m, 0)),
        out_shape=S((L, C), BF16), compiler_params=_params("parallel"),
    )(y, w, b.reshape(1, C), gnorm.reshape(1, C))


def _ple_specs(L, D, P, tm, nb):
    return [pl.BlockSpec((tm, D), lambda n, m: (m, 0)), pl.BlockSpec((tm, P), lambda n, m: (m, 0)),
            pl.BlockSpec((D, nb), lambda n, m: (0, n)), pl.BlockSpec((1, P, nb), lambda n, m: (n, 0, 0)),
            pl.BlockSpec((tm, nb), lambda n, m: (m, n))]


def ple_fwd(un, pb, wpg, wpp, h, token):
    L, D = un.shape
    ns, P, nb = wpp.shape
    tm = _tile(L, 512)

    def body(un_ref, p_ref, wg_ref, wp_ref, h_ref, tok_ref, o_ref):
        gate = _sigmoid(_dot(un_ref[...], wg_ref[...], 1, 0))
        o_ref[...] = h_ref[...] + tok_ref[0:1, 0:1] + _dot(p_ref[...], wp_ref[0], 1, 0) * gate

    return pl.pallas_call(
        body, name="ple_fwd", grid=(ns, L // tm),
        in_specs=_ple_specs(L, D, P, tm, nb) + [pl.BlockSpec((8, 128), lambda n, m: (0, 0))],
        out_specs=pl.BlockSpec((tm, nb), lambda n, m: (m, n)),
        out_shape=S((L, D), F32), compiler_params=_params("parallel", "parallel"),
    )(un, pb, wpg, wpp, h, token)


def loss_head(h, g, target):
    L, D = h.shape
    tm = _tile(L, 256)

    def body(h_ref, g_ref, t_ref, loss_ref, dh_ref, dg_ref):
        m = pl.program_id(0)
        x = h_ref[...]
        gg = g_ref[...]
        e = x * _rstd(x) * gg - t_ref[...]
        dx, dg = _rms_bwd(x, gg, e * (1.0 / D))
        dh_ref[...] = dx
        part = jnp.full((8, 128), 0.5 / D, F32) * jnp.sum(e * e)

        @pl.when(m == 0)
        def _():
            loss_ref[...] = part
            dg_ref[...] = dg

        @pl.when(m > 0)
        def _():
            loss_ref[...] += part
            dg_ref[...] += dg

    return pl.pallas_call(
        body, name="loss_head", grid=(L // tm,),
        in_specs=[pl.BlockSpec((tm, D), lambda m: (m, 0)), pl.BlockSpec((1, D), lambda m: (0, 0)),
                  pl.BlockSpec((tm, D), lambda m: (m, 0))],
        out_specs=[pl.BlockSpec((8, 128), lambda m: (0, 0)), pl.BlockSpec((tm, D), lambda m: (m, 0)),
                   pl.BlockSpec((1, D), lambda m: (0, 0))],
        out_shape=[S((8, 128), F32), S((L, D), F32), S((1, D), F32)],
        compiler_params=_params("arbitrary"),
    )(h, g.reshape(1, D), target)


def ple_bwd(un, pb, wpg, wpp, dh, token):
    L, D = un.shape
    ns, P, nb = wpp.shape
    tm = _tile(L, 512)

    def body(un_ref, p_ref, wg_ref, wp_ref, dh_ref, tok_ref, dpre_ref, dpp_ref):
        gate = _sigmoid(_dot(un_ref[...], wg_ref[...], 1, 0))
        pp = _dot(p_ref[...], wp_ref[0], 1, 0)
        d = dh_ref[...] + tok_ref[0:1, 0:1]
        dpp_ref[0] = (d * gate).astype(BF16)
        dpre_ref[...] = (d * pp * gate * (1.0 - gate)).astype(BF16)

    return pl.pallas_call(
        body, name="ple_bwd", grid=(ns, L // tm),
        in_specs=_ple_specs(L, D, P, tm, nb) + [pl.BlockSpec((8, 128), lambda n, m: (0, 0))],
        out_specs=[pl.BlockSpec((tm, nb), lambda n, m: (m, n)), pl.BlockSpec((1, tm, nb), lambda n, m: (n, m, 0))],
        out_shape=[S((L, D), BF16), S((ns, L, nb), BF16)], compiler_params=_params("parallel", "parallel"),
    )(un, pb, wpg, wpp, dh, token)


def wgrad(a, b):
    a3 = a if a.ndim == 3 else a[None]
    b3 = b if b.ndim == 3 else b[None]
    ns = max(a3.shape[0], b3.shape[0])
    _, L, Ka = a3.shape
    N = b3.shape[2]
    a_map = (lambda s: (s, 0, 0)) if a3.shape[0] > 1 else (lambda s: (0, 0, 0))
    b_map = (lambda s: (s, 0, 0)) if b3.shape[0] > 1 else (lambda s: (0, 0, 0))

    def body(a_ref, b_ref, o_ref):
        o_ref[0] = _dot(a_ref[0], b_ref[0], 0, 0).astype(BF16)

    return pl.pallas_call(
        body, name="wgrad", grid=(ns,),
        in_specs=[pl.BlockSpec((1, L, Ka), a_map), pl.BlockSpec((1, L, N), b_map)],
        out_specs=pl.BlockSpec((1, Ka, N), lambda s: (s, 0, 0)),
        out_shape=S((ns, Ka, N), BF16), compiler_params=_params("parallel"),
    )(a3, b3)


def dx_rms(pairs, h, g, dh_in, cast_scale):
    L, D = h.shape
    nk = pairs[0][0].shape[0]
    n_pairs = len(pairs)
    tm = _tile(L, 512)
    n_m = L // tm
    w_dims = [0 if transposed else 1 for _, _, transposed in pairs]

    def body(*refs):
        ins, (h_ref, g_ref, dhi_ref, dho_ref, dhb_ref, dg_ref) = refs[:2 * n_pairs], refs[2 * n_pairs:]
        m = pl.program_id(0)
        acc = None
        for i in range(n_pairs):
            for k in range(nk):
                part = _dot(ins[2 * i][k], ins[2 * i + 1][k], 1, w_dims[i])
                acc = part if acc is None else acc + part
        dx, dg = _rms_bwd(h_ref[...], g_ref[...], acc)
        dh_out = dhi_ref[...] + dx
        dho_ref[...] = dh_out
        dhb_ref[...] = (cast_scale * dh_out).astype(BF16)

        @pl.when(m == 0)
        def _():
            dg_ref[...] = dg

        @pl.when(m > 0)
        def _():
            dg_ref[...] += dg

    in_specs, args = [], []
    for a3, w3, _ in pairs:
        Kc = a3.shape[2]
        in_specs += [pl.BlockSpec((nk, tm, Kc), lambda m: (0, m, 0)),
                     pl.BlockSpec(w3.shape, lambda m: (0, 0, 0), pipeline_mode=pl.Buffered(1))]
        args += [a3, w3]
    tile = pl.BlockSpec((tm, D), lambda m: (m, 0))
    row = pl.BlockSpec((1, D), lambda m: (0, 0))
    return pl.pallas_call(
        body, name="dx_rms", grid=(n_m,),
        in_specs=in_specs + [tile, row, tile], out_specs=[tile, tile, row],
        out_shape=[S((L, D), F32), S((L, D), BF16), S((1, D), F32)], compiler_params=_params("arbitrary"),
    )(*args, h, g.reshape(1, D), dh_in)


def dact_plain(dhb, w3):
    L, D = dhb.shape
    ns, N, _ = w3.shape
    tm = _tile(L, 512)

    def body(d_ref, w_ref, o_ref):
        o_ref[0] = _dot(d_ref[...], w_ref[0], 1, 1)

    return pl.pallas_call(
        body, name="dact_plain", grid=(ns, L // tm),
        in_specs=[pl.BlockSpec((tm, D), lambda s, m: (m, 0)), pl.BlockSpec((1, N, D), lambda s, m: (s, 0, 0))],
        out_specs=pl.BlockSpec((1, tm, N), lambda s, m: (s, m, 0)),
        out_shape=S((ns, L, N), F32), compiler_params=_params("parallel", "parallel"),
    )(dhb, w3)


def dact_swiglu(dhb, wd, a3, b3):
    L, D = dhb.shape
    ns, F, _ = wd.shape
    tm = _tile(L, 512)

    def body(d_ref, w_ref, a_ref, b_ref, da_ref, db_ref):
        ds = _dot(d_ref[...], w_ref[0], 1, 1)
        a = a_ref[0].astype(F32)
        b = b_ref[0].astype(F32)
        sg = _sigmoid(a)
        da_ref[0] = (ds * b * (sg * (1.0 + a * (1.0 - sg)))).astype(BF16)
        db_ref[0] = (ds * (a * sg)).astype(BF16)

    t_spec = pl.BlockSpec((1, tm, F), lambda s, m: (s, m, 0))
    return pl.pallas_call(
        body, name="dact_swiglu", grid=(ns, L // tm),
        in_specs=[pl.BlockSpec((tm, D), lambda s, m: (m, 0)), pl.BlockSpec((1, F, D), lambda s, m: (s, 0, 0)), t_spec, t_spec],
        out_specs=[t_spec, t_spec], out_shape=[S((ns, L, F), BF16)] * 2,
        compiler_params=_params("parallel", "parallel"),
    )(dhb, wd, a3, b3)


def conv_bwd(z, conv_w, conv_b, gnorm, dyn):
    _, L, C = z.shape
    tm = _tile(L, 256)
    H = CONV_HALO
    T = tm + 2 * H

    def body(zb_ref, zbp_ref, zbn_ref, zc_ref, zcp_ref, zcn_ref, zv_ref, zvp_ref, zvn_ref, d_ref, dp_ref, dn_ref,
             w_ref, b_ref, g_ref, dz_ref, dw_ref, db_ref, dg_ref):
        m = pl.program_id(0)
        cat = lambda p, c, n: jnp.concatenate([p[0], c[0], n[0]], axis=0)
        zb, zc, zv, d = cat(zbp_ref, zb_ref, zbn_ref), cat(zcp_ref, zc_ref, zcn_ref), cat(zvp_ref, zv_ref, zvn_ref), cat(dp_ref, d_ref, dn_ref)
        grow = m * tm - H + lax.broadcasted_iota(jnp.int32, (T, C), 0)
        valid, v, v1, v2, cb, ya = _conv_core(zb, zc, zv, w_ref, b_ref[...], grow, L)
        dya, _ = _rms_bwd(ya, g_ref[...], d)
        dc = jnp.where(valid, dya * zb, 0.0)
        dv = w_ref[2:3, :] * dc + w_ref[1:2, :] * pltpu.roll(dc, T - 1, 0) + w_ref[0:1, :] * pltpu.roll(dc, T - 2, 0)
        dz_ref[0] = (dya * cb)[H:H + tm, :].astype(BF16)
        dz_ref[1] = (dv * zv)[H:H + tm, :].astype(BF16)
        dz_ref[2] = (dv * zc)[H:H + tm, :].astype(BF16)
        rs = lambda x: jnp.sum(x[H:H + tm, :], axis=0, keepdims=True)
        yh = ya * _rstd(ya)
        dw = jnp.concatenate([rs(dc * v2), rs(dc * v1), rs(dc * v)], axis=0)
        dbias, dg = rs(dc), rs(d * yh)

        @pl.when(m == 0)
        def _():
            dw_ref[...] = dw
            db_ref[...] = dbias
            dg_ref[...] = dg

        @pl.when(m > 0)
        def _():
            dw_ref[...] += dw
            db_ref[...] += dbias
            dg_ref[...] += dg

    row = lambda r: pl.BlockSpec((r, C), lambda m: (0, 0))
    specs = [*_conv_specs(L, tm, C, 0), *_conv_specs(L, tm, C, 1), *_conv_specs(L, tm, C, 2), *_conv_specs(L, tm, C, 0)]
    return pl.pallas_call(
        body, name="conv_bwd", grid=(L // tm,),
        in_specs=specs + [row(3), row(1), row(1)],
        out_specs=[pl.BlockSpec((3, tm, C), lambda m: (0, m, 0)), row(3), row(1), row(1)],
        out_shape=[S((3, L, C), BF16), S((3, C), F32), S((1, C), F32), S((1, C), F32)],
        compiler_params=_params("arbitrary"),
    )(z, z, z, z, z, z, z, z, z, dyn, dyn, dyn, conv_w, conv_b.reshape(1, C), gnorm.reshape(1, C))


def glu_bwd(y, w, b, gnorm, dn):
    L, C = y.shape
    tm = _tile(L, 256)

    def body(y_ref, w_ref, b_ref, g_ref, d_ref, dy_ref, dpre_ref, zg_ref, db_ref, dg_ref):
        m = pl.program_id(0)
        yv = y_ref[...]
        zg, t = _gelu(yv)
        zgb = zg.astype(BF16)
        sg = _sigmoid(_dot(zgb, w_ref[...], 1, 0) + b_ref[...])
        out = zg * sg
        dout, dg = _rms_bwd(out, g_ref[...], d_ref[...])
        dpre = dout * zg * sg * (1.0 - sg)
        dpre_b = dpre.astype(BF16)
        dzg = dout * sg + _dot(dpre_b, w_ref[...], 1, 1)
        dt = (1.0 - t * t) * _GELU_C * (1.0 + 3.0 * 0.044715 * yv * yv)
        dy_ref[...] = dzg * (0.5 * (1.0 + t) + 0.5 * yv * dt)
        dpre_ref[...] = dpre_b
        zg_ref[...] = zgb
        dbias = jnp.sum(dpre, axis=0, keepdims=True)

        @pl.when(m == 0)
        def _():
            db_ref[...] = dbias
            dg_ref[...] = dg

        @pl.when(m > 0)
        def _():
            db_ref[...] += dbias
            dg_ref[...] += dg

    tile = pl.BlockSpec((tm, C), lambda m: (m, 0))
    row = pl.BlockSpec((1, C), lambda m: (0, 0))
    return pl.pallas_call(
        body, name="glu_bwd", grid=(L // tm,),
        in_specs=[tile, pl.BlockSpec((C, C), lambda m: (0, 0)), row, row, tile],
        out_specs=[tile, tile, tile, row, row],
        out_shape=[S((L, C), F32), S((L, C), BF16), S((L, C), BF16), S((1, C), F32), S((1, C), F32)],
        compiler_params=_params("arbitrary"),
    )(y, w, b.reshape(1, C), gnorm.reshape(1, C), dn)


def _scan_bwd(gr_ref, gi_ref, hr_ref, hi_ref, lr, li, n_steps):
    W = gr_ref.shape[1]
    zero = jnp.zeros((N_SEG, W), F32)
    lic = -li

    def local(i, c):
        r = pl.multiple_of((n_steps - 1 - i) * N_SEG, N_SEG)
        pr, pi = _cmul(lr, lic, c[0], c[1])
        nr = pr + gr_ref[pl.ds(r, N_SEG), :]
        ni = pi + gi_ref[pl.ds(r, N_SEG), :]
        gr_ref[pl.ds(r, N_SEG), :] = nr
        gi_ref[pl.ds(r, N_SEG), :] = ni
        return nr, ni

    fr, fi = lax.fori_loop(0, n_steps, local, (zero, zero))
    qr, qi = _cpow(lr, lic, n_steps)
    row = lax.broadcasted_iota(jnp.int32, (N_SEG, W), 0)
    cr, ci = zero, zero
    for seg in range(N_SEG - 2, -1, -1):
        tr, ti = _cmul(qr, qi, cr, ci)
        sr = pltpu.roll(fr + tr, N_SEG - 1, 0)
        si = pltpu.roll(fi + ti, N_SEG - 1, 0)
        cr = jnp.where(row == seg, sr, cr)
        ci = jnp.where(row == seg, si, ci)

    def fix(i, c):
        pwr, pwi, ar, ai = c
        t = n_steps - 1 - i
        r = pl.multiple_of(t * N_SEG, N_SEG)
        pwr, pwi = _cmul(lr, lic, pwr, pwi)
        xr, xi = _cmul(pwr, pwi, cr, ci)
        g_r = gr_ref[pl.ds(r, N_SEG), :] + xr
        g_i = gi_ref[pl.ds(r, N_SEG), :] + xi
        gr_ref[pl.ds(r, N_SEG), :] = g_r
        gi_ref[pl.ds(r, N_SEG), :] = g_i
        rp = pl.multiple_of(jnp.maximum(t - 1, 0) * N_SEG, N_SEG)
        hpr = hr_ref[pl.ds(rp, N_SEG), :]
        hpi = hi_ref[pl.ds(rp, N_SEG), :]
        live = t > 0
        ar = ar + jnp.where(live, hpr * g_r + hpi * g_i, 0.0)
        ai = ai + jnp.where(live, hpr * g_i - hpi * g_r, 0.0)
        return pwr, pwi, ar, ai

    _, _, ar, ai = lax.fori_loop(0, n_steps, fix, (jnp.ones((N_SEG, W), F32), zero, zero, zero))
    last = pl.ds((n_steps - 1) * N_SEG, N_SEG)
    hpr = jnp.where(row == 0, 0.0, pltpu.roll(hr_ref[last, :], 1, 0))
    hpi = jnp.where(row == 0, 0.0, pltpu.roll(hi_ref[last, :], 1, 0))
    g_r, g_i = gr_ref[pl.ds(0, N_SEG), :], gi_ref[pl.ds(0, N_SEG), :]
    ar = ar + hpr * g_r + hpi * g_i
    ai = ai + hpr * g_i - hpi * g_r
    return jnp.sum(ar, axis=0, keepdims=True), jnp.sum(ai, axis=0, keepdims=True)


def ssm_bwd(us, dy, lam, bre, bim, cre, cim, dvec):
    L = us.shape[0]
    n_steps = L // N_SEG
    sp = _ssm_specs(L)

    def body(u_ref, dy_ref, lam_ref, bre_ref, bim_ref, cre_ref, cim_ref, d_ref,
             du_ref, dlam_ref, dbre_ref, dbim_ref, dcre_ref, dcim_ref, dd_ref, hr, hi, gr, gi):
        u = u_ref[...]
        ub = u.astype(BF16)
        dyv = dy_ref[...]
        dyb = dyv.astype(BF16)
        hr[...] = _dot(ub, bre_ref[0], 1, 0)
        hi[...] = _dot(ub, bim_ref[0], 1, 0)
        lr = jnp.broadcast_to(lam_ref[0:1, :], (N_SEG, 512))
        li = jnp.broadcast_to(lam_ref[1:2, :], (N_SEG, 512))
        _scan_fwd(hr, hi, lr, li, n_steps)
        dcre_ref[0] = _dot(hr[...].astype(BF16), dyb, 0, 0)
        dcim_ref[0] = -_dot(hi[...].astype(BF16), dyb, 0, 0)
        gr[...] = _dot(dyb, cre_ref[0], 1, 1)
        gi[...] = -_dot(dyb, cim_ref[0], 1, 1)
        dlr, dli = _scan_bwd(gr, gi, hr, hi, lr, li, n_steps)
        dlam_ref[...] = jnp.concatenate([dlr, dli], axis=0)
        grb, gib = gr[...].astype(BF16), gi[...].astype(BF16)
        du_ref[...] = _dot(grb, bre_ref[0], 1, 1) + _dot(gib, bim_ref[0], 1, 1) + d_ref[...] * dyv
        dbre_ref[0] = _dot(ub, grb, 0, 0)
        dbim_ref[0] = _dot(ub, gib, 0, 0)
        dd_ref[...] = jnp.sum(dyv * u, axis=0, keepdims=True)

    big = pltpu.VMEM((L, 512), F32)
    return pl.pallas_call(
        body, name="ssm_bwd", grid=(4,),
        in_specs=[sp["u"], sp["u"], sp["lam"], sp["bmat"], sp["bmat"], sp["cmat"], sp["cmat"], sp["d"]],
        out_specs=[sp["u"], sp["lam"], sp["bmat"], sp["bmat"], sp["cmat"], sp["cmat"], sp["d"]],
        out_shape=[S((L, 512), F32), S((2, 2048), F32), S((4, 128, 512), F32), S((4, 128, 512), F32),
                   S((4, 512, 128), F32), S((4, 512, 128), F32), S((1, 512), F32)],
        scratch_shapes=[big, big, big, big], compiler_params=_params("parallel"),
    )(us, dy, lam, bre, bim, cre, cim, dvec)


def _discretize(ar, ai, log_dt, br, bi):
    dt = jnp.exp(log_dt)
    mag = jnp.exp(ar * dt)
    ph = ai * dt
    lr, li = mag * jnp.cos(ph), mag * jnp.sin(ph)
    nr, ni = lr - 1.0, li
    den = ar * ar + ai * ai
    fr = (nr * ar + ni * ai) / den
    fi = (ni * ar - nr * ai) / den
    return lr, li, fr[..., None] * br - fi[..., None] * bi, fr[..., None] * bi + fi[..., None] * br


def ssm_prep(ar, ai, log_dt, br, bi):
    G, P, H = br.shape

    def body(ar_ref, ai_ref, dt_ref, br_ref, bi_ref, lr_ref, li_ref, bbr_ref, bbi_ref):
        lr_ref[...], li_ref[...], bbr_ref[...], bbi_ref[...] = _discretize(
            ar_ref[...], ai_ref[...], dt_ref[...], br_ref[...], bi_ref[...])

    return pl.pallas_call(
        body, name="ssm_prep",
        out_shape=[S((G, P), F32), S((G, P), F32), S((G, P, H), F32), S((G, P, H), F32)],
    )(ar, ai, log_dt.reshape(G, 1), br, bi)


def ssm_prep_bwd(ar, ai, log_dt, br, bi, dlr, dli, dbbr, dbbi):
    G, P, H = br.shape

    def body(ar_ref, ai_ref, dt_ref, br_ref, bi_ref, dlr_ref, dli_ref, dbbr_ref, dbbi_ref,
             dar_ref, dai_ref, ddt_ref, dbr_ref, dbi_ref):
        _, vjp = jax.vjp(_discretize, ar_ref[...], ai_ref[...], dt_ref[...], br_ref[...], bi_ref[...])
        dar_ref[...], dai_ref[...], ddt_ref[...], dbr_ref[...], dbi_ref[...] = vjp(
            (dlr_ref[...], dli_ref[...], dbbr_ref[...], dbbi_ref[...]))

    return pl.pallas_call(
        body, name="ssm_prep_bwd",
        out_shape=[S((G, P), F32), S((G, P), F32), S((G, 1), F32), S((G, P, H), F32), S((G, P, H), F32)],
    )(ar, ai, log_dt.reshape(G, 1), br, bi, dlr, dli, dbbr, dbbi)


def _block_diag(x):
    j, n, R, C = x.shape
    eye = jnp.eye(n, dtype=x.dtype)
    return (x[:, :, :, None, :] * eye[None, :, None, :, None]).reshape(j, n * R, n * C)


def _block_diag_take(x, R, C):
    j = x.shape[0]
    n = x.shape[1] // R
    x5 = x.reshape(j, n, R, n, C)
    return jnp.stack([x5[:, i, :, i, :] for i in range(n)], axis=1)


def _to_segments(x):
    L, C = x.shape
    return x.reshape(N_SEG, L // N_SEG, C).transpose(1, 0, 2).reshape(L, C)


def _from_segments(x):
    L, C = x.shape
    return x.reshape(L // N_SEG, N_SEG, C).transpose(1, 0, 2).reshape(L, C)


BIG = ("ffn1_w_gate", "ffn1_w_up", "ffn1_w_down", "w_in", "glu_w", "w_out",
       "ffn2_w_gate", "ffn2_w_up", "ffn2_w_down", "ple_w_gate", "ple_w_proj")
SMALL = ("ffn1_norm", "mix_norm", "conv_w", "conv_b", "ssm_A_re", "ssm_A_im", "ssm_B_re", "ssm_B_im", "ssm_C_re", "ssm_C_im",
         "ssm_D", "ssm_log_dt", "glu_b", "conv_out_norm", "ssm_out_norm", "ffn2_norm", "ple_norm")


def _ssm_mats(w):
    G, P, H = w["ssm_B_re"].shape
    lr, li, bbr, bbi = ssm_prep(w["ssm_A_re"], w["ssm_A_im"], w["ssm_log_dt"], w["ssm_B_re"], w["ssm_B_im"])
    lam = jnp.stack([lr.reshape(G * P), li.reshape(G * P)])
    bmat = lambda bb: _block_diag(bb.reshape(4, G // 4, P, H).transpose(0, 1, 3, 2)).astype(BF16)
    cmat = lambda c: _block_diag(c.reshape(4, G // 4, H, P).transpose(0, 1, 3, 2)).astype(BF16)
    return lam, bmat(bbr), bmat(bbi), cmat(w["ssm_C_re"]), cmat(w["ssm_C_im"]), w["ssm_D"].reshape(1, G * H)


def layer_fwd(h0, pb, w, before_last):
    L, D = h0.shape
    u1 = rmsnorm_fwd(h0, w["ffn1_norm"])
    a1, b1, s1 = ffn_up(u1, w["ffn1_w_gate"], w["ffn1_w_up"])
    h1, u2 = mm_shard_k(s1, w["ffn1_w_down"], h0, 0.5, w["mix_norm"])
    z = mm_shard_n(u2, w["w_in"], F32)
    ya_n = conv_fwd(z, w["conv_w"], w["conv_b"], w["conv_out_norm"])
    us = _to_segments(z[3])
    mats = w["mats"]
    y = ssm_fwd(us, *mats)
    ys_n = glu_fwd(y, w["glu_w"], w["glu_b"], w["ssm_out_norm"])
    ycat = jnp.stack([ya_n, _from_segments(ys_n)])
    h2, u3 = mm_shard_k(ycat, w["w_out"], h1, 1.0, w["ffn2_norm"])
    a2, b2, s2 = ffn_up(u3, w["ffn2_w_gate"], w["ffn2_w_up"])
    h3, un = mm_shard_k(s2, w["ffn2_w_down"], h2, 0.5, w["ple_norm"])
    h4 = ple_fwd(un, pb, w["ple_w_gate"], w["ple_w_proj"], h3, before_last(h3))
    saved = dict(h0=h0, u1=u1, a1=a1, b1=b1, s1=s1, h1=h1, u2=u2, z=z, us=us, mats=mats, y=y, ycat=ycat,
                 h2=h2, u3=u3, a2=a2, b2=b2, s2=s2, h3=h3, un=un)
    return h4, saved


def _ffn_bwd(dh, dhb, h_in, u, a, b, s, wg, wu, wd, gnorm, cast_scale):
    da, db = dact_swiglu(dhb, wd, a, b)
    g_wd = wgrad(s, dhb)
    g_wg = wgrad(da, u)
    g_wu = wgrad(db, u)
    dh_in, dhb_in, g_norm = dx_rms([(da, wg, True), (db, wu, True)], h_in, gnorm, dh, cast_scale)
    return dh_in, dhb_in, g_wg, g_wu, g_wd, g_norm


def layer_bwd(dh, pb, w, sv, token):
    L, D = dh.shape
    G, P, H = w["ssm_B_re"].shape
    dpre, dpp3 = ple_bwd(sv["un"], pb, w["ple_w_gate"], w["ple_w_proj"], dh, token)
    g_wpg = wgrad(sv["un"], dpre).reshape(N_SHARD, D // N_SHARD, D)
    g_wpp = wgrad(pb, dpp3)
    dh3, dhb3, g_nple = dx_rms([(dpre[None], w["ple_w_gate"][None], False)], sv["h3"], w["ple_norm"], dh, 0.5)
    dh2, dhb, g_wg2, g_wu2, g_wd2, g_nffn2 = _ffn_bwd(dh3, dhb3, sv["h2"], sv["u3"], sv["a2"], sv["b2"], sv["s2"],
                                                      w["ffn2_w_gate"], w["ffn2_w_up"], w["ffn2_w_down"], w["ffn2_norm"], 1.0)
    dyn = dact_plain(dhb, w["w_out"])
    g_wout = wgrad(sv["ycat"], dhb).reshape(N_SHARD, -1, D)
    dz_abc, g_convw, g_convb, g_nconv = conv_bwd(sv["z"], w["conv_w"], w["conv_b"], w["conv_out_norm"], dyn)
    dy, dpre_g, zg, g_glub, g_nssm = glu_bwd(sv["y"], w["glu_w"], w["glu_b"], w["ssm_out_norm"], _to_segments(dyn[1]))
    C = zg.shape[1]
    g_gluw = wgrad(zg, dpre_g).reshape(N_SHARD, C // N_SHARD, C)
    dus, dlam, dbre, dbim, dcre, dcim, dd = ssm_bwd(sv["us"], dy, *sv["mats"])
    take_b = lambda m: _block_diag_take(m, H, P).transpose(0, 1, 3, 2).reshape(G, P, H)
    take_c = lambda m: _block_diag_take(m, P, H).transpose(0, 1, 3, 2).reshape(G, H, P)
    g_ar, g_ai, g_dt, g_br, g_bi = ssm_prep_bwd(
        w["ssm_A_re"], w["ssm_A_im"], w["ssm_log_dt"], w["ssm_B_re"], w["ssm_B_im"],
        dlam[0].reshape(G, P), dlam[1].reshape(G, P), take_b(dbre), take_b(dbim))
    dz3 = jnp.concatenate([dz_abc, _from_segments(dus).astype(BF16)[None]], axis=0)
    g_win = wgrad(sv["u2"], dz3)
    dh1, dhb1, g_nmix = dx_rms([(dz3, w["w_in"], False)], sv["h1"], w["mix_norm"], dh2, 0.5)
    dh0, _, g_wg1, g_wu1, g_wd1, g_nffn1 = _ffn_bwd(dh1, dhb1, sv["h0"], sv["u1"], sv["a1"], sv["b1"], sv["s1"],
                                                    w["ffn1_w_gate"], w["ffn1_w_up"], w["ffn1_w_down"], w["ffn1_norm"], 1.0)
    big = [g_wg1, g_wu1, g_wd1, g_win, g_gluw, g_wout, g_wg2, g_wu2, g_wd2, g_wpg, g_wpp]
    small = dict(ffn1_norm=g_nffn1, mix_norm=g_nmix, conv_w=g_convw, conv_b=g_convb, ssm_A_re=g_ar, ssm_A_im=g_ai,
                 ssm_B_re=g_br, ssm_B_im=g_bi, ssm_C_re=take_c(dcre), ssm_C_im=take_c(dcim), ssm_D=dd,
                 ssm_log_dt=g_dt, glu_b=g_glub, conv_out_norm=g_nconv, ssm_out_norm=g_nssm, ffn2_norm=g_nffn2,
                 ple_norm=g_nple)
    return dh0, big, small


def local_step(x, p, target, final_norm, weights_of, before_last, on_grads):
    depth = p.shape[0]
    h = x
    layers, saved, pbs = [], [], []
    for i in range(depth):
        w = weights_of(i, h)
        pb = p[i].astype(BF16)
        h, sv = layer_fwd(h, pb, w, functools.partial(before_last, i))
        layers.append(w)
        saved.append(sv)
        pbs.append(pb)
    loss_part, dh, g_final = loss_head(h, final_norm, target)
    smalls = [None] * depth
    token = jnp.zeros((8, 128), F32)
    for i in reversed(range(depth)):
        dh, big, smalls[i] = layer_bwd(dh, pbs[i], layers[i], saved[i], token)
        token = on_grads(i, big, dh)
    return loss_part, dh, smalls, g_final


ROW_TILE_MAX = 512


def _row_tile(rows):
    for t in range(ROW_TILE_MAX, 0, -16):
        if rows % t == 0:
            return t
    return rows


def elementwise(fn, ins, out_dtypes, name):
    rows, cols = ins[0].shape
    tr = _row_tile(rows)
    n_in = len(ins)

    def body(*refs):
        outs = fn(*[r[...] for r in refs[:n_in]])
        for o_ref, o in zip(refs[n_in:], outs):
            o_ref[...] = o.astype(o_ref.dtype)

    spec = pl.BlockSpec((tr, cols), lambda i: (i, 0))
    return pl.pallas_call(
        body, name=name, grid=(rows // tr,), in_specs=[spec] * n_in, out_specs=[spec] * len(out_dtypes),
        out_shape=[S((rows, cols), d) for d in out_dtypes], compiler_params=_params("parallel"),
    )(*ins)


def _adamw(w, g, m, v):
    m = ADAM_B1 * m + (1.0 - ADAM_B1) * g
    v = ADAM_B2 * v + (1.0 - ADAM_B2) * (g * g)
    m_hat = m / (1.0 - ADAM_B1 ** ADAM_STEP)
    v_hat = v / (1.0 - ADAM_B2 ** ADAM_STEP)
    delta = -ADAM_LR * (m_hat / (jnp.sqrt(v_hat) + ADAM_EPS) + ADAM_WD * w)
    return delta, m, v


ANY = pl.BlockSpec(memory_space=pl.ANY)


def _mesh_pos():
    return lax.axis_index("x"), lax.axis_index("y"), lax.axis_index("c")


def _other_chips(x, y):
    return [(1 - x, y), (x, 1 - y), (1 - x, 1 - y)]


def _remote(src, dst, send_sem, recv_sem, device):
    return pltpu.make_async_remote_copy(src_ref=src, dst_ref=dst, send_sem=send_sem, recv_sem=recv_sem,
                                        device_id=device, device_id_type=MESH)


def gather_weights(ws):
    n = len(ws)

    def body(*refs):
        outs = refs[n:2 * n]
        send_sems, recv_sems = refs[2 * n:]
        x, y, c = _mesh_pos()
        me_s = 2 * x + y
        sibling = (x, y, 1 - c)
        chips = _other_chips(x, y)
        n_half = outs[0].shape[0] // 2
        mine, other = pl.ds(c * n_half, n_half), pl.ds((1 - c) * n_half, n_half)
        sent = []
        for t in range(n):
            for j, (cx, cy) in enumerate(chips):
                blk = outs[t].at[mine, me_s]
                cp = _remote(blk, blk, send_sems.at[t, j], recv_sems.at[t, j], (cx, cy, c))
                cp.start()
                sent.append(cp)
        for j, (cx, cy) in enumerate(chips):
            for t in range(n):
                blk = outs[t].at[mine, 2 * cx + cy]
                _remote(blk, blk, send_sems.at[t, j], recv_sems.at[t, j], (cx, cy, c)).wait_recv()
                cp = _remote(blk, blk, send_sems.at[t, 3 + j], recv_sems.at[t, 3 + j], sibling)
                cp.start()
                sent.append(cp)
        for j, (cx, cy) in enumerate(chips):
            for t in range(n):
                blk = outs[t].at[other, 2 * cx + cy]
                _remote(blk, blk, send_sems.at[t, 3 + j], recv_sems.at[t, 3 + j], sibling).wait_recv()
        for cp in sent:
            cp.wait_send()

    return pl.pallas_call(
        body, name="gather_weights", in_specs=[ANY] * n, out_specs=[ANY] * n,
        out_shape=[S(w.shape, w.dtype) for w in ws], input_output_aliases={t: t for t in range(n)},
        scratch_shapes=[pltpu.SemaphoreType.DMA((n, 6)), pltpu.SemaphoreType.DMA((n, 6))],
    )(*ws)


HBM = pl.BlockSpec(memory_space=pltpu.HBM)
SEM = pl.BlockSpec(memory_space=pltpu.SEMAPHORE)
VMEM_WHOLE = pl.BlockSpec(memory_space=pltpu.VMEM)
SPLIT_COPY = pltpu.CompilerParams(has_side_effects=pltpu.SideEffectType.DATAFLOW_SIDE_EFFECTING)


def _hbm(x):
    return pltpu.with_memory_space_constraint(x, pltpu.HBM)


def _half_rows(ref, c):
    r2 = ref.shape[1] // 2
    return pl.ds(pl.multiple_of(c * r2, 8), r2)


def _gather_copies(bufs, send_sems, recv_sems, forward):
    x, y, c = _mesh_pos()
    copies = []
    for t in range(len(bufs)):
        rows = _half_rows(bufs[t], c)
        for j, (cx, cy) in enumerate(_other_chips(x, y)):
            blk = bufs[t].at[2 * cx + cy if forward else 2 * x + y, rows]
            peer = (x, y, 1 - c) if forward else (cx, cy, c)
            copies.append(_remote(blk, blk, send_sems.at[3 * t + j], recv_sems.at[3 * t + j], peer))
    return copies


def gather_start(bufs, layer, after, forward=False):
    n, k = len(bufs), len(after)

    def body(*refs):
        ins, send_sems, recv_sems, token = refs[:n], refs[n + k], refs[n + k + 1], refs[2 * n + k + 2]
        for cp in _gather_copies(ins, send_sems, recv_sems, forward):
            cp.start()
        token[...] = jnp.zeros_like(token)

    outs = pl.pallas_call(
        body, name=f"{'forward' if forward else 'gather'}_start_{layer}", in_specs=[HBM] * n + [ANY] * k,
        out_specs=[SEM, SEM] + [HBM] * n + [VMEM_WHOLE],
        out_shape=[pltpu.SemaphoreType.DMA((3 * n,)), pltpu.SemaphoreType.DMA((3 * n,))]
        + [pltpu.HBM(b.shape, b.dtype) for b in bufs] + [S((8, 128), F32)],
        input_output_aliases={t: t + 2 for t in range(n)}, compiler_params=SPLIT_COPY,
    )(*[_hbm(b) for b in bufs], *after)
    return outs[0], outs[1], list(outs[2:2 + n]), outs[2 + n]


def gather_wait(bufs, send_sems, recv_sems, after, layer, forward=False):
    n, n_after = len(bufs), len(after)

    def body(*refs):
        ins, send_ref, recv_ref = refs[:n], refs[n], refs[n + 1]
        for cp in _gather_copies(ins, send_ref, recv_ref, forward):
            cp.wait_send()
            cp.wait_recv()

    outs = pl.pallas_call(
        body, name=f"{'forward' if forward else 'gather'}_wait_{layer}", in_specs=[HBM] * n + [SEM, SEM] + [ANY] * n_after,
        out_specs=[HBM] * n,
        out_shape=[pltpu.HBM(b.shape, b.dtype) for b in bufs],
        input_output_aliases={t: t for t in range(n)}, compiler_params=SPLIT_COPY,
    )(*bufs, send_sems, recv_sems, *after)
    return list(outs)


def gather_forward(bufs):
    n = len(bufs)

    def body(*refs):
        outs = refs[n:2 * n]
        send_sems, recv_sems = refs[2 * n:]
        copies = _gather_copies(outs, send_sems, recv_sems, True)
        for cp in copies:
            cp.start()
        for cp in copies:
            cp.wait()

    return pl.pallas_call(
        body, name="gather_forward", in_specs=[ANY] * n, out_specs=[ANY] * n,
        out_shape=[S(b.shape, b.dtype) for b in bufs], input_output_aliases={t: t for t in range(n)},
        scratch_shapes=[pltpu.SemaphoreType.DMA((3 * n,)), pltpu.SemaphoreType.DMA((3 * n,))],
    )(*bufs)


def chips_start(sums, layer):
    n = len(sums)
    lands = [lax.empty((3,) + s.shape[1:], s.dtype) for s in sums]

    def body(*refs):
        a, land, send_sems, recv_sems, token = refs[:n], refs[n:2 * n], refs[2 * n], refs[2 * n + 1], refs[4 * n + 2]
        x, y, c = _mesh_pos()
        for t in range(n):
            for j, (cx, cy) in enumerate(_other_chips(x, y)):
                _remote(a[t].at[2 * cx + cy], land[t].at[j], send_sems.at[3 * t + j], recv_sems.at[3 * t + j], (cx, cy, c)).start()
        token[...] = jnp.zeros_like(token)

    outs = pl.pallas_call(
        body, name=f"chips_start_{layer}", in_specs=[HBM] * (2 * n), out_specs=[SEM, SEM] + [HBM] * (2 * n) + [VMEM_WHOLE],
        out_shape=[pltpu.SemaphoreType.DMA((3 * n,)), pltpu.SemaphoreType.DMA((3 * n,))]
        + [pltpu.HBM(b.shape, b.dtype) for b in sums + lands] + [S((8, 128), F32)],
        input_output_aliases={t: t + 2 for t in range(2 * n)}, compiler_params=SPLIT_COPY,
    )(*[_hbm(b) for b in sums + lands])
    return outs[0], outs[1], list(outs[2:2 + n]), list(outs[2 + n:2 + 2 * n]), outs[2 + 2 * n]


def chips_wait(sums, lands, send_sems, recv_sems, after, layer):
    n, n_after = len(sums), len(after)

    def body(*refs):
        a, land, send_ref, recv_ref = refs[:n], refs[n:2 * n], refs[2 * n], refs[2 * n + 1]
        x, y, c = _mesh_pos()
        for t in range(n):
            for j, (cx, cy) in enumerate(_other_chips(x, y)):
                cp = _remote(a[t].at[2 * cx + cy], land[t].at[j], send_ref.at[3 * t + j], recv_ref.at[3 * t + j], (cx, cy, c))
                cp.wait_send()
                cp.wait_recv()

    outs = pl.pallas_call(
        body, name=f"chips_wait_{layer}", in_specs=[HBM] * (2 * n) + [SEM, SEM] + [ANY] * n_after, out_specs=[HBM] * (2 * n),
        out_shape=[pltpu.HBM(b.shape, b.dtype) for b in sums + lands],
        input_output_aliases={t: t for t in range(2 * n)}, compiler_params=SPLIT_COPY,
    )(*sums, *lands, send_sems, recv_sems, *after)
    return list(outs[n:])


def cast_place_layer(w, layer, pos, dtype, token):
    _, r, c = w.shape
    tr = _row_tile(r)

    def body(pos_ref, w_ref, tok_ref, o_ref):
        o_ref[0] = (w_ref[0] + tok_ref[0:1, 0:1]).astype(dtype)

    return pl.pallas_call(
        body, name="cast_place_layer",
        grid_spec=pltpu.PrefetchScalarGridSpec(
            num_scalar_prefetch=1, grid=(r // tr,),
            in_specs=[pl.BlockSpec((1, tr, c), lambda i, pos: (layer, i, 0)), pl.BlockSpec((8, 128), lambda i, pos: (0, 0))],
            out_specs=pl.BlockSpec((1, tr, c), lambda i, pos: (pos[1], i, 0))),
        out_shape=S((N_SHARD, r, c), dtype), compiler_params=_params("parallel"),
    )(pos, w, token)


def cast_place(w, pos, dtype):
    layers, r, c = w.shape
    tr = _row_tile(r)

    def body(pos_ref, w_ref, o_ref):
        o_ref[0, 0] = w_ref[0].astype(dtype)

    return pl.pallas_call(
        body, name="cast_place",
        grid_spec=pltpu.PrefetchScalarGridSpec(
            num_scalar_prefetch=1, grid=(layers, r // tr),
            in_specs=[pl.BlockSpec((1, tr, c), lambda l, i, pos: (l, i, 0))],
            out_specs=pl.BlockSpec((1, 1, tr, c), lambda l, i, pos: (l, pos[1], i, 0))),
        out_shape=S((layers, N_SHARD, r, c), dtype), compiler_params=_params("parallel", "parallel"),
    )(pos, w)


def _pair_copy(g_ref, got_ref, send_sem, recv_sem):
    x, y, c = _mesh_pos()
    r2 = g_ref.shape[1] // 2
    give = pl.ds(pl.multiple_of((1 - c) * r2, 8), r2)
    return _remote(g_ref.at[:, give], got_ref, send_sem, recv_sem, (x, y, 1 - c))


def reduce_pair(gs, after):
    n, k = len(gs), len(after)

    def body(*refs):
        ins, got = refs[:n], refs[n + k:2 * n + k]
        send_sems, recv_sems = refs[2 * n + k:]
        copies = [_pair_copy(ins[t], got[t], send_sems.at[t], recv_sems.at[t]) for t in range(n)]
        for cp in copies:
            cp.start()
        for cp in copies:
            cp.wait()

    return pl.pallas_call(
        body, name="reduce_pair", in_specs=[ANY] * (n + k), out_specs=[ANY] * n,
        out_shape=[S((g.shape[0], g.shape[1] // 2, g.shape[2]), g.dtype) for g in gs],
        scratch_shapes=[pltpu.SemaphoreType.DMA((n,)), pltpu.SemaphoreType.DMA((n,))],
    )(*gs, *after)


def pair_sum(g, got, pos):
    ns, r2, c = got.shape
    tr = _row_tile(r2)
    n_i = r2 // tr

    def body(pos_ref, g_ref, got_ref, sum_ref, own_ref):
        s = pl.program_id(1)
        v = g_ref[0].astype(F32) + got_ref[0].astype(F32)
        sum_ref[0] = v.astype(BF16)

        @pl.when(s == pos_ref[1])
        def _():
            own_ref[...] = v

    return pl.pallas_call(
        body, name="pair_sum",
        grid_spec=pltpu.PrefetchScalarGridSpec(
            num_scalar_prefetch=1, grid=(n_i, ns),
            in_specs=[pl.BlockSpec((1, tr, c), lambda i, s, pos: (s, pos[0] * n_i + i, 0)),
                      pl.BlockSpec((1, tr, c), lambda i, s, pos: (s, i, 0))],
            out_specs=[pl.BlockSpec((1, tr, c), lambda i, s, pos: (s, i, 0)), pl.BlockSpec((tr, c), lambda i, s, pos: (i, 0))]),
        out_shape=[S((ns, r2, c), BF16), S((r2, c), F32)], compiler_params=_params("parallel", "arbitrary"),
    )(pos, g, got)


def chip_sum(own, p2, pos):
    r2, c = own.shape
    tr = _row_tile(r2)
    n_i = r2 // tr

    def body(pos_ref, own_ref, a_ref, b_ref, c_ref, o_ref):
        o_ref[...] = own_ref[...] + a_ref[0].astype(F32) + b_ref[0].astype(F32) + c_ref[0].astype(F32)

    peer = lambda j: pl.BlockSpec((1, tr, c), lambda i, pos: (j, i, 0))
    return pl.pallas_call(
        body, name="chip_sum",
        grid_spec=pltpu.PrefetchScalarGridSpec(
            num_scalar_prefetch=1, grid=(n_i,),
            in_specs=[pl.BlockSpec((tr, c), lambda i, pos: (i, 0)), peer(0), peer(1), peer(2)],
            out_specs=pl.BlockSpec((tr, c), lambda i, pos: (pos[0] * n_i + i, 0))),
        out_shape=S((2 * r2, c), F32), compiler_params=_params("parallel"),
    )(pos, own, p2, p2, p2)


def exchange_halves(rs):
    n = len(rs)

    def body(*refs):
        outs = refs[n:2 * n]
        send_sems, recv_sems = refs[2 * n:]
        x, y, c = _mesh_pos()
        copies = []
        for t in range(n):
            r2 = outs[t].shape[0] // 2
            rows = outs[t].at[pl.ds(pl.multiple_of(c * r2, 8), r2)]
            cp = _remote(rows, rows, send_sems.at[t], recv_sems.at[t], (x, y, 1 - c))
            cp.start()
            copies.append(cp)
        for cp in copies:
            cp.wait()

    return pl.pallas_call(
        body, name="exchange_halves", in_specs=[ANY] * n, out_specs=[ANY] * n,
        out_shape=[S(r.shape, r.dtype) for r in rs], input_output_aliases={t: t for t in range(n)},
        scratch_shapes=[pltpu.SemaphoreType.DMA((n,)), pltpu.SemaphoreType.DMA((n,))],
    )(*rs)


def allreduce_small(vec):
    R = vec.shape[0]
    H = R // 2

    def body(x_ref, o_ref, pair_buf, chip_buf, send_sems, recv_sems):
        x, y, c = _mesh_pos()
        me_s = 2 * x + y
        sibling = (x, y, 1 - c)
        mine = pl.ds(pl.multiple_of(c * H, 8), H)
        give = pl.ds(pl.multiple_of((1 - c) * H, 8), H)
        cp = _remote(x_ref.at[give], pair_buf, send_sems.at[0], recv_sems.at[0], sibling)
        cp.start()
        cp.wait()
        chip_buf[me_s] = x_ref[mine, :] + pair_buf[...]
        copies = []
        for j, (cx, cy) in enumerate(_other_chips(x, y)):
            cp = _remote(chip_buf.at[me_s], chip_buf.at[me_s], send_sems.at[1 + j], recv_sems.at[1 + j], (cx, cy, c))
            cp.start()
            copies.append(cp)
        for cp in copies:
            cp.wait()
        o_ref[mine, :] = (chip_buf[0] + chip_buf[1]) + (chip_buf[2] + chip_buf[3])
        cp = _remote(o_ref.at[mine], o_ref.at[mine], send_sems.at[4], recv_sems.at[4], sibling)
        cp.start()
        cp.wait()

    vm = pl.BlockSpec(memory_space=pltpu.VMEM)
    return pl.pallas_call(
        body, name="allreduce_small", in_specs=[vm], out_specs=vm, out_shape=S((R, 128), F32),
        scratch_shapes=[pltpu.VMEM((H, 128), F32), pltpu.VMEM((N_SHARD, H, 128), F32),
                        pltpu.SemaphoreType.DMA((5,)), pltpu.SemaphoreType.DMA((5,))],
        compiler_params=pltpu.CompilerParams(vmem_limit_bytes=VMEM_LIMIT_BYTES),
    )(vec)


def adamw_layer(w, g, m, v, layer, prev):
    _, r, c = w.shape
    tr = _row_tile(r)

    def body(w_ref, g_ref, m_ref, v_ref, *rest):
        outs = rest[-4:]
        g_val = g_ref[...]
        outs[0][0] = g_val
        outs[1][0], outs[2][0], outs[3][0] = _adamw(w_ref[0], g_val, m_ref[0], v_ref[0])

    lay = pl.BlockSpec((1, tr, c), lambda i: (layer, i, 0))
    prev = list(prev) if prev else []
    return pl.pallas_call(
        body, name="adamw_layer", grid=(r // tr,),
        in_specs=[lay, pl.BlockSpec((tr, c), lambda i: (i, 0)), lay, lay] + [ANY] * len(prev),
        out_specs=[lay] * 4, out_shape=[S(w.shape, F32)] * 4,
        input_output_aliases={4 + k: k for k in range(len(prev))}, compiler_params=_params("parallel"),
    )(w, g, m, v, *prev)


def reduce_begin(gs, pos, layer, after):
    got = reduce_pair(gs, after)
    sums, own = zip(*[pair_sum(g, o, pos) for g, o in zip(gs, got)])
    send_sems, recv_sems, sums, lands, token = chips_start(list(sums), layer)
    return dict(own=own, sums=sums, lands=lands, sems=(send_sems, recv_sems), token=token, layer=layer)


def reduce_end(pending, pos, after):
    lands = chips_wait(pending["sums"], pending["lands"], *pending["sems"], after, pending["layer"])
    return exchange_halves([chip_sum(o, p, pos) for o, p in zip(pending["own"], lands)])


W_NAMES = ("ffn1_norm", "ffn1_w_gate", "ffn1_w_up", "ffn1_w_down", "mix_norm", "w_in", "conv_w", "conv_b", "ssm_A_re", "ssm_A_im",
           "ssm_B_re", "ssm_B_im", "ssm_C_re", "ssm_C_im", "ssm_D", "ssm_log_dt", "glu_w", "glu_b", "conv_out_norm", "ssm_out_norm",
           "w_out", "ffn2_norm", "ffn2_w_gate", "ffn2_w_up", "ffn2_w_down", "ple_norm", "ple_w_gate", "ple_w_proj", "final_norm")
SMALL_ALL = SMALL + ("final_norm",)
TRANSPOSED = ("ffn1_w_gate", "ffn1_w_up", "ffn2_w_gate", "ffn2_w_up")
PACK = ROW_TILE_MAX * 128


def _pack(parts):
    flat = jnp.concatenate([p.reshape(-1) for p in parts])
    pad = (-flat.shape[0]) % PACK
    return jnp.pad(flat, (0, pad)).reshape(-1, 128)


def _unpack(vec, shapes):
    flat = vec.reshape(-1)
    out, off = [], 0
    for shp in shapes:
        size = math.prod(shp)
        out.append(flat[off:off + size].reshape(shp))
        off += size
    return out


def _step(a):
    a = {k: jnp.swapaxes(v, 1, 2) if k.removeprefix("m_").removeprefix("v_") in TRANSPOSED else v for k, v in a.items()}
    x, p, target = a["x"][0], a["p"][:, 0], a["loss_target"][0]
    depth = p.shape[0]
    L, D = x.shape
    me_s = 2 * lax.axis_index("x") + lax.axis_index("y")

    pos = jnp.stack([lax.axis_index("c"), me_s]).astype(jnp.int32)
    conv_w = gather_weights([cast_place(a["conv_w"], pos, F32)])[0]
    started, token = [], jnp.zeros((8, 128), F32)
    for l in range(depth):
        started.append(gather_start([cast_place_layer(a[n], l, pos, BF16, token) for n in BIG], l, [conv_w]))
        token = started[-1][3]

    mats = [_ssm_mats({n: a[n][l] for n in SMALL}) for l in range(depth)]
    packed_state = [_pack([a[prefix + n] for n in SMALL_ALL]) for prefix in ("", "m_", "v_")]
    early_work = [leaf for m in mats for leaf in m] + packed_state
    forwarding = {}

    def before_last(l, h3):
        if l + 1 == depth:
            return jnp.zeros((8, 128), F32)
        send_sems, recv_sems, bufs, _ = started[l + 1]
        forwarding[l + 1] = gather_start(gather_wait(bufs, send_sems, recv_sems, [h3], l + 1), l + 1, [], forward=True)
        return forwarding[l + 1][3]

    def weights_of(l, h):
        if l:
            send_sems, recv_sems, bufs, _ = forwarding[l]
            full = gather_wait(bufs, send_sems, recv_sems, [h], l, forward=True)
        else:
            send_sems, recv_sems, bufs, _ = started[0]
            full = gather_forward(gather_wait(bufs, send_sems, recv_sems, [s[3] for s in started] + early_work, 0))
        w = {n: a[n][l] for n in SMALL if n != "conv_w"}
        w["mats"] = mats[l]
        w.update(dict(zip(BIG, full)))
        C = w["glu_w"].shape[-1]
        w["glu_w"] = w["glu_w"].reshape(C, C)
        w["w_out"] = w["w_out"].reshape(2, -1, D)
        w["ple_w_gate"] = w["ple_w_gate"].reshape(D, D)
        w["conv_w"] = conv_w[l].transpose(1, 0, 2).reshape(3, -1)
        return w

    pending, first_layer_grads = {}, []

    def on_grads(l, big, dh):
        if l == 0:
            first_layer_grads.extend(big)
            return None
        pending[l] = reduce_begin(big, pos, l, [])
        return pending[l]["token"]

    loss_part, dx, smalls, g_final = local_step(x, p, target, a["final_norm"], weights_of, before_last, on_grads)
    small_shapes = [(depth,) + smalls[0][n].shape for n in SMALL] + [g_final.shape, (1,)]
    parts = [smalls[l][n] for n in SMALL for l in range(depth)] + [g_final, loss_part[0, 0:1]]
    summed_vec = allreduce_small(_pack(parts))
    pending[0] = reduce_begin(first_layer_grads, pos, 0, [summed_vec])
    stacked = [None] * len(BIG)
    for l in reversed(range(depth)):
        after = [pending[0]["token"]] if l else [s[3] for s in stacked]
        reduced = reduce_end(pending[l], pos, after)
        stacked = [adamw_layer(a[n], reduced[i], a["m_" + n], a["v_" + n], l, stacked[i]) for i, n in enumerate(BIG)]
    big_out = {n: [jnp.swapaxes(o, 1, 2) for o in outs] if n in TRANSPOSED else outs for n, outs in zip(BIG, stacked)}

    summed = _unpack(summed_vec, small_shapes)
    g_small = dict(zip(SMALL_ALL, summed[:-1]))
    loss = summed[-1][0]
    n_conv = a["conv_w"].shape[-1]
    g_small["conv_w"] = lax.dynamic_slice_in_dim(g_small["conv_w"], me_s * n_conv, n_conv, axis=2)
    g_small = {n: g_small[n].reshape(a[n].shape) for n in SMALL_ALL}
    packed = [packed_state[0], _pack([g_small[n] for n in SMALL_ALL]), packed_state[1], packed_state[2]]
    shapes = [a[n].shape for n in SMALL_ALL]
    d_s, m_s, v_s = [dict(zip(SMALL_ALL, _unpack(o, shapes))) for o in elementwise(_adamw, packed, [F32, F32, F32], "adamw_small")]

    outs = {n: big_out[n] if n in big_out else (g_small[n], d_s[n], m_s[n], v_s[n]) for n in W_NAMES}
    return (loss, dx[None], *[outs[n][0] for n in W_NAMES], *[outs[n][1] for n in W_NAMES],
            *[outs[n][2] for n in W_NAMES], *[outs[n][3] for n in W_NAMES])


def kernel(x, p, ffn1_norm, ffn1_w_gate, ffn1_w_up, ffn1_w_down, mix_norm, w_in, conv_w, conv_b, ssm_A_re, ssm_A_im, ssm_B_re, ssm_B_im, ssm_C_re, ssm_C_im, ssm_D, ssm_log_dt, glu_w, glu_b, conv_out_norm, ssm_out_norm, w_out, ffn2_norm, ffn2_w_gate, ffn2_w_up, ffn2_w_down, ple_norm, ple_w_gate, ple_w_proj, final_norm, loss_target, m_ffn1_norm, m_ffn1_w_gate, m_ffn1_w_up, m_ffn1_w_down, m_mix_norm, m_w_in, m_conv_w, m_conv_b, m_ssm_A_re, m_ssm_A_im, m_ssm_B_re, m_ssm_B_im, m_ssm_C_re, m_ssm_C_im, m_ssm_D, m_ssm_log_dt, m_glu_w, m_glu_b, m_conv_out_norm, m_ssm_out_norm, m_w_out, m_ffn2_norm, m_ffn2_w_gate, m_ffn2_w_up, m_ffn2_w_down, m_ple_norm, m_ple_w_gate, m_ple_w_proj, m_final_norm, v_ffn1_norm, v_ffn1_w_gate, v_ffn1_w_up, v_ffn1_w_down, v_mix_norm, v_w_in, v_conv_w, v_conv_b, v_ssm_A_re, v_ssm_A_im, v_ssm_B_re, v_ssm_B_im, v_ssm_C_re, v_ssm_C_im, v_ssm_D, v_ssm_log_dt, v_glu_w, v_glu_b, v_conv_out_norm, v_ssm_out_norm, v_w_out, v_ffn2_norm, v_ffn2_w_gate, v_ffn2_w_up, v_ffn2_w_down, v_ple_norm, v_ple_w_gate, v_ple_w_proj, v_final_norm):
    return _step(dict(locals()))
```

```python
import functools
import math

import jax
import jax.numpy as jnp
from jax import lax
from jax.experimental import pallas as pl
from jax.experimental.pallas import tpu as pltpu

F32, BF16 = jnp.float32, jnp.bfloat16
S = jax.ShapeDtypeStruct
EPS = 1e-6
N_SEG = 8
N_SHARD = 4
N_DEV = 8
VMEM_LIMIT_BYTES = 56 * 1024 * 1024
ADAM_LR, ADAM_B1, ADAM_B2, ADAM_EPS, ADAM_WD, ADAM_STEP = 0.001, 0.9, 0.999, 1e-08, 0.01, 10
MESH = pl.DeviceIdType.MESH


def _params(*sem):
    return pltpu.CompilerParams(dimension_semantics=sem if sem else None, vmem_limit_bytes=VMEM_LIMIT_BYTES)


def _dot(a, b, ca, cb):
    return lax.dot_general(a, b, (((ca,), (cb,)), ((), ())), preferred_element_type=F32)


def _sigmoid(x):
    return 1.0 / (1.0 + jnp.exp(-x))


def _rstd(x):
    return lax.rsqrt(jnp.mean(x * x, axis=-1, keepdims=True) + EPS)


def _rms_bwd(x, g, dy):
    r = _rstd(x)
    xh = x * r
    dxh = dy * g
    dx = r * (dxh - xh * jnp.mean(dxh * xh, axis=-1, keepdims=True))
    return dx, jnp.sum(dy * xh, axis=0, keepdims=True)


def _tile(n, want):
    return want if n % want == 0 else n


def rmsnorm_fwd(h, g):
    L, D = h.shape
    tm = _tile(L, 512)

    def body(h_ref, g_ref, o_ref):
        x = h_ref[...]
        o_ref[...] = (x * _rstd(x) * g_ref[...]).astype(BF16)

    return pl.pallas_call(
        body, name="rmsnorm_fwd", grid=(L // tm,),
        in_specs=[pl.BlockSpec((tm, D), lambda m: (m, 0)), pl.BlockSpec((1, D), lambda m: (0, 0))],
        out_specs=pl.BlockSpec((tm, D), lambda m: (m, 0)),
        out_shape=S((L, D), BF16), compiler_params=_params("parallel"),
    )(h, g.reshape(1, D))


def ffn_up(u, wg, wu):
    L, D = u.shape
    ns, F, _ = wg.shape
    tm = _tile(L, 512)

    def body(u_ref, wg_ref, wu_ref, a_ref, b_ref, s_ref):
        x = u_ref[...]
        a = _dot(x, wg_ref[0], 1, 1)
        b = _dot(x, wu_ref[0], 1, 1)
        a_ref[0] = a.astype(BF16)
        b_ref[0] = b.astype(BF16)
        s_ref[0] = (a * _sigmoid(a) * b).astype(BF16)

    w_spec = pl.BlockSpec((1, F, D), lambda s, m: (s, 0, 0))
    o_spec = pl.BlockSpec((1, tm, F), lambda s, m: (s, m, 0))
    return pl.pallas_call(
        body, name="ffn_up", grid=(ns, L // tm),
        in_specs=[pl.BlockSpec((tm, D), lambda s, m: (m, 0)), w_spec, w_spec],
        out_specs=[o_spec, o_spec, o_spec],
        out_shape=[S((ns, L, F), BF16)] * 3, compiler_params=_params("parallel", "parallel"),
    )(u, wg, wu)


def mm_shard_n(u, w3, out_dtype):
    L, K = u.shape
    ns, _, N = w3.shape
    tm = _tile(L, 512)

    def body(u_ref, w_ref, o_ref):
        o_ref[0] = _dot(u_ref[...], w_ref[0], 1, 0).astype(out_dtype)

    return pl.pallas_call(
        body, name="mm_shard_n", grid=(ns, L // tm),
        in_specs=[pl.BlockSpec((tm, K), lambda s, m: (m, 0)), pl.BlockSpec((1, K, N), lambda s, m: (s, 0, 0))],
        out_specs=pl.BlockSpec((1, tm, N), lambda s, m: (s, m, 0)),
        out_shape=S((ns, L, N), out_dtype), compiler_params=_params("parallel", "parallel"),
    )(u, w3)


def mm_shard_k(a3, w3, res, scale, g_next):
    nk, L, Kc = a3.shape
    N = w3.shape[2]
    tm = _tile(L, 512)

    def body(a_ref, w_ref, r_ref, g_ref, o_ref, u_ref):
        acc = _dot(a_ref[0], w_ref[0], 1, 0)
        for k in range(1, nk):
            acc += _dot(a_ref[k], w_ref[k], 1, 0)
        h = r_ref[...] + scale * acc
        o_ref[...] = h
        u_ref[...] = (h * _rstd(h) * g_ref[...]).astype(BF16)

    tile = pl.BlockSpec((tm, N), lambda m: (m, 0))
    return pl.pallas_call(
        body, name="mm_shard_k", grid=(L // tm,),
        in_specs=[pl.BlockSpec((nk, tm, Kc), lambda m: (0, m, 0)), pl.BlockSpec((nk, Kc, N), lambda m: (0, 0, 0)),
                  tile, pl.BlockSpec((1, N), lambda m: (0, 0))],
        out_specs=[tile, tile],
        out_shape=[S((L, N), F32), S((L, N), BF16)], compiler_params=_params("parallel"),
    )(a3, w3, res, g_next.reshape(1, N))


CONV_HALO = 8


def _conv_specs(L, tm, C, shard):
    nb = L // CONV_HALO
    per = tm // CONV_HALO
    main = pl.BlockSpec((1, tm, C), lambda m: (shard, m, 0))
    prev = pl.BlockSpec((1, CONV_HALO, C), lambda m: (shard, jnp.maximum(m * per - 1, 0), 0))
    nxt = pl.BlockSpec((1, CONV_HALO, C), lambda m: (shard, jnp.minimum((m + 1) * per, nb - 1), 0))
    return main, prev, nxt


def _conv_core(zb, zc, zv, w_ref, bias, grow, L):
    valid = (grow >= 0) & (grow < L)
    v = jnp.where(valid, zc * zv, 0.0)
    v1 = pltpu.roll(v, 1, 0)
    v2 = pltpu.roll(v, 2, 0)
    cb = w_ref[0:1, :] * v2 + w_ref[1:2, :] * v1 + w_ref[2:3, :] * v + bias
    return valid, v, v1, v2, cb, zb * cb


def conv_fwd(z, conv_w, conv_b, gnorm):
    _, L, C = z.shape
    tm = _tile(L, 256)
    H = CONV_HALO

    def body(zb_ref, zc_ref, zcp_ref, zv_ref, zvp_ref, w_ref, b_ref, g_ref, o_ref):
        m = pl.program_id(0)
        zc = jnp.concatenate([zcp_ref[0], zc_ref[0]], axis=0)
        zv = jnp.concatenate([zvp_ref[0], zv_ref[0]], axis=0)
        grow = m * tm - H + lax.broadcasted_iota(jnp.int32, (tm + H, C), 0)
        valid = grow >= 0
        v = jnp.where(valid, zc * zv, 0.0)
        v1 = pltpu.roll(v, 1, 0)
        v2 = pltpu.roll(v, 2, 0)
        cb = (w_ref[0:1, :] * v2 + w_ref[1:2, :] * v1 + w_ref[2:3, :] * v + b_ref[...])[H:, :]
        ya = zb_ref[0] * cb
        o_ref[...] = (ya * _rstd(ya) * g_ref[...]).astype(BF16)

    zb_m, _, _ = _conv_specs(L, tm, C, 0)
    zc_m, zc_p, _ = _conv_specs(L, tm, C, 1)
    zv_m, zv_p, _ = _conv_specs(L, tm, C, 2)
    row = lambda r: pl.BlockSpec((r, C), lambda m: (0, 0))
    return pl.pallas_call(
        body, name="conv_fwd", grid=(L // tm,),
        in_specs=[zb_m, zc_m, zc_p, zv_m, zv_p, row(3), row(1), row(1)],
        out_specs=pl.BlockSpec((tm, C), lambda m: (m, 0)),
        out_shape=S((L, C), BF16), compiler_params=_params("parallel"),
    )(z, z, z, z, z, conv_w, conv_b.reshape(1, C), gnorm.reshape(1, C))


def _cmul(ar, ai, br, bi):
    return ar * br - ai * bi, ar * bi + ai * br


def _scan_fwd(hr_ref, hi_ref, lr, li, n_steps):
    W = hr_ref.shape[1]
    zero = jnp.zeros((N_SEG, W), F32)

    def local(t, c):
        r = pl.multiple_of(t * N_SEG, N_SEG)
        pr, pi = _cmul(lr, li, c[0], c[1])
        nr = pr + hr_ref[pl.ds(r, N_SEG), :]
        ni = pi + hi_ref[pl.ds(r, N_SEG), :]
        hr_ref[pl.ds(r, N_SEG), :] = nr
        hi_ref[pl.ds(r, N_SEG), :] = ni
        return nr, ni

    fr, fi = lax.fori_loop(0, n_steps, local, (zero, zero))
    qr, qi = _cpow(lr, li, n_steps)
    row = lax.broadcasted_iota(jnp.int32, (N_SEG, W), 0)
    cr, ci = zero, zero
    for seg in range(1, N_SEG):
        tr, ti = _cmul(qr, qi, cr, ci)
        sr = pltpu.roll(fr + tr, 1, 0)
        si = pltpu.roll(fi + ti, 1, 0)
        cr = jnp.where(row == seg, sr, cr)
        ci = jnp.where(row == seg, si, ci)

    def fix(t, c):
        r = pl.multiple_of(t * N_SEG, N_SEG)
        pr, pi = _cmul(lr, li, c[0], c[1])
        ar, ai = _cmul(pr, pi, cr, ci)
        hr_ref[pl.ds(r, N_SEG), :] += ar
        hi_ref[pl.ds(r, N_SEG), :] += ai
        return pr, pi

    lax.fori_loop(0, n_steps, fix, (jnp.ones((N_SEG, W), F32), zero))


def _cpow(lr, li, n):
    rr, ri = None, None
    br, bi = lr, li
    while n:
        if n & 1:
            rr, ri = (br, bi) if rr is None else _cmul(rr, ri, br, bi)
        n >>= 1
        if n:
            br, bi = _cmul(br, bi, br, bi)
    return rr, ri


def _ssm_specs(L):
    col = lambda w: pl.BlockSpec((L, w), lambda j: (0, j))
    return dict(
        u=col(128), lam=pl.BlockSpec((2, 512), lambda j: (0, j)),
        bmat=pl.BlockSpec((1, 128, 512), lambda j: (j, 0, 0)), cmat=pl.BlockSpec((1, 512, 128), lambda j: (j, 0, 0)),
        d=pl.BlockSpec((1, 128), lambda j: (0, j)))


def ssm_fwd(us, lam, bre, bim, cre, cim, dvec):
    L = us.shape[0]
    n_steps = L // N_SEG
    sp = _ssm_specs(L)

    def body(u_ref, lam_ref, bre_ref, bim_ref, cre_ref, cim_ref, d_ref, y_ref, hr, hi):
        u = u_ref[...]
        ub = u.astype(BF16)
        hr[...] = _dot(ub, bre_ref[0], 1, 0)
        hi[...] = _dot(ub, bim_ref[0], 1, 0)
        lr = jnp.broadcast_to(lam_ref[0:1, :], (N_SEG, 512))
        li = jnp.broadcast_to(lam_ref[1:2, :], (N_SEG, 512))
        _scan_fwd(hr, hi, lr, li, n_steps)
        y_ref[...] = (_dot(hr[...].astype(BF16), cre_ref[0], 1, 0) - _dot(hi[...].astype(BF16), cim_ref[0], 1, 0)
                      + d_ref[...] * u)

    return pl.pallas_call(
        body, name="ssm_fwd", grid=(4,),
        in_specs=[sp["u"], sp["lam"], sp["bmat"], sp["bmat"], sp["cmat"], sp["cmat"], sp["d"]],
        out_specs=sp["u"], out_shape=S((L, 512), F32),
        scratch_shapes=[pltpu.VMEM((L, 512), F32), pltpu.VMEM((L, 512), F32)],
        compiler_params=_params("parallel"),
    )(us, lam, bre, bim, cre, cim, dvec)


_GELU_C = math.sqrt(2.0 / math.pi)


def _gelu(y):
    t = jnp.tanh(_GELU_C * (y + 0.044715 * y * y * y))
    return 0.5 * y * (1.0 + t), t


def glu_fwd(y, w, b, gnorm):
    L, C = y.shape
    tm = _tile(L, 512)

    def body(y_ref, w_ref, b_ref, g_ref, o_ref):
        zg, _ = _gelu(y_ref[...])
        out = zg * _sigmoid(_dot(zg.astype(BF16), w_ref[...], 1, 0) + b_ref[...])
        o_ref[...] = (out * _rstd(out) * g_ref[...]).astype(BF16)

    row = pl.BlockSpec((1, C), lambda m: (0, 0))
    return pl.pallas_call(
        body, name="glu_fwd", grid=(L // tm,),
        in_specs=[pl.BlockSpec((tm, C), lambda m: (m, 0)), pl.BlockSpec((C, C), lambda m: (0, 0)), row, row],
        out_specs=pl.BlockSpec((tm, C), lambda m: (m, 0)),
        out_shape=S((L, C), BF16), compiler_params=_params("parallel"),
    )(y, w, b.reshape(1, C), gnorm.reshape(1, C))


def _ple_specs(L, D, P, tm, nb):
    return [pl.BlockSpec((tm, D), lambda n, m: (m, 0)), pl.BlockSpec((tm, P), lambda n, m: (m, 0)),
            pl.BlockSpec((D, nb), lambda n, m: (0, n)), pl.BlockSpec((1, P, nb), lambda n, m: (n, 0, 0)),
            pl.BlockSpec((tm, nb), lambda n, m: (m, n))]


def ple_fwd(un, pb, wpg, wpp, h, token):
    L, D = un.shape
    ns, P, nb = wpp.shape
    tm = _tile(L, 512)

    def body(un_ref, p_ref, wg_ref, wp_ref, h_ref, tok_ref, o_ref):
        gate = _sigmoid(_dot(un_ref[...], wg_ref[...], 1, 0))
        o_ref[...] = h_ref[...] + tok_ref[0:1, 0:1] + _dot(p_ref[...], wp_ref[0], 1, 0) * gate

    return pl.pallas_call(
        body, name="ple_fwd", grid=(ns, L // tm),
        in_specs=_ple_specs(L, D, P, tm, nb) + [pl.BlockSpec((8, 128), lambda n, m: (0, 0))],
        out_specs=pl.BlockSpec((tm, nb), lambda n, m: (m, n)),
        out_shape=S((L, D), F32), compiler_params=_params("parallel", "parallel"),
    )(un, pb, wpg, wpp, h, token)


def loss_head(h, g, target):
    L, D = h.shape
    tm = _tile(L, 256)

    def body(h_ref, g_ref, t_ref, loss_ref, dh_ref, dg_ref):
        m = pl.program_id(0)
        x = h_ref[...]
        gg = g_ref[...]
        e = x * _rstd(x) * gg - t_ref[...]
        dx, dg = _rms_bwd(x, gg, e * (1.0 / D))
        dh_ref[...] = dx
        part = jnp.full((8, 128), 0.5 / D, F32) * jnp.sum(e * e)

        @pl.when(m == 0)
        def _():
            loss_ref[...] = part
            dg_ref[...] = dg

        @pl.when(m > 0)
        def _():
            loss_ref[...] += part
            dg_ref[...] += dg

    return pl.pallas_call(
        body, name="loss_head", grid=(L // tm,),
        in_specs=[pl.BlockSpec((tm, D), lambda m: (m, 0)), pl.BlockSpec((1, D), lambda m: (0, 0)),
                  pl.BlockSpec((tm, D), lambda m: (m, 0))],
        out_specs=[pl.BlockSpec((8, 128), lambda m: (0, 0)), pl.BlockSpec((tm, D), lambda m: (m, 0)),
                   pl.BlockSpec((1, D), lambda m: (0, 0))],
        out_shape=[S((8, 128), F32), S((L, D), F32), S((1, D), F32)],
        compiler_params=_params("arbitrary"),
    )(h, g.reshape(1, D), target)


def ple_bwd(un, pb, wpg, wpp, dh, token):
    L, D = un.shape
    ns, P, nb = wpp.shape
    tm = _tile(L, 512)

    def body(un_ref, p_ref, wg_ref, wp_ref, dh_ref, tok_ref, dpre_ref, dpp_ref):
        gate = _sigmoid(_dot(un_ref[...], wg_ref[...], 1, 0))
        pp = _dot(p_ref[...], wp_ref[0], 1, 0)
        d = dh_ref[...] + tok_ref[0:1, 0:1]
        dpp_ref[0] = (d * gate).astype(BF16)
        dpre_ref[...] = (d * pp * gate * (1.0 - gate)).astype(BF16)

    return pl.pallas_call(
        body, name="ple_bwd", grid=(ns, L // tm),
        in_specs=_ple_specs(L, D, P, tm, nb) + [pl.BlockSpec((8, 128), lambda n, m: (0, 0))],
        out_specs=[pl.BlockSpec((tm, nb), lambda n, m: (m, n)), pl.BlockSpec((1, tm, nb), lambda n, m: (n, m, 0))],
        out_shape=[S((L, D), BF16), S((ns, L, nb), BF16)], compiler_params=_params("parallel", "parallel"),
    )(un, pb, wpg, wpp, dh, token)


def wgrad(a, b):
    a3 = a if a.ndim == 3 else a[None]
    b3 = b if b.ndim == 3 else b[None]
    ns = max(a3.shape[0], b3.shape[0])
    _, L, Ka = a3.shape
    N = b3.shape[2]
    a_map = (lambda s: (s, 0, 0)) if a3.shape[0] > 1 else (lambda s: (0, 0, 0))
    b_map = (lambda s: (s, 0, 0)) if b3.shape[0] > 1 else (lambda s: (0, 0, 0))

    def body(a_ref, b_ref, o_ref):
        o_ref[0] = _dot(a_ref[0], b_ref[0], 0, 0).astype(BF16)

    return pl.pallas_call(
        body, name="wgrad", grid=(ns,),
        in_specs=[pl.BlockSpec((1, L, Ka), a_map), pl.BlockSpec((1, L, N), b_map)],
        out_specs=pl.BlockSpec((1, Ka, N), lambda s: (s, 0, 0)),
        out_shape=S((ns, Ka, N), BF16), compiler_params=_params("parallel"),
    )(a3, b3)


def wgrad_sharded(pairs):
    n = len(pairs)
    ns, L, Ka = pairs[0][0].shape
    N = pairs[0][1].shape[1]

    def body(*refs):
        for i in range(n):
            refs[2 * n + i][0] = _dot(refs[2 * i][0], refs[2 * i + 1][...], 0, 0).astype(BF16)

    in_specs, args = [], []
    for a3, b in pairs:
        in_specs += [pl.BlockSpec((1, L, Ka), lambda s: (s, 0, 0)), pl.BlockSpec((L, N), lambda s: (0, 0), pipeline_mode=pl.Buffered(1))]
        args += [a3, b]
    out = pl.BlockSpec((1, Ka, N), lambda s: (s, 0, 0))
    return pl.pallas_call(
        body, name="wgrad_sharded", grid=(ns,), in_specs=in_specs, out_specs=[out] * n,
        out_shape=[S((ns, Ka, N), BF16)] * n, compiler_params=_params("parallel"),
    )(*args)


def dx_rms(pairs, h, g, dh_in, cast_scale):
    L, D = h.shape
    nk = pairs[0][0].shape[0]
    n_pairs = len(pairs)
    tm = _tile(L, 512)
    n_m = L // tm
    w_dims = [0 if transposed else 1 for _, _, transposed in pairs]

    def body(*refs):
        ins, (h_ref, g_ref, dhi_ref, dho_ref, dhb_ref, dg_ref) = refs[:2 * n_pairs], refs[2 * n_pairs:]
        m = pl.program_id(0)
        acc = None
        for i in range(n_pairs):
            for k in range(nk):
                part = _dot(ins[2 * i][k], ins[2 * i + 1][k], 1, w_dims[i])
                acc = part if acc is None else acc + part
        dx, dg = _rms_bwd(h_ref[...], g_ref[...], acc)
        dh_out = dhi_ref[...] + dx
        dho_ref[...] = dh_out
        dhb_ref[...] = (cast_scale * dh_out).astype(BF16)

        @pl.when(m == 0)
        def _():
            dg_ref[...] = dg

        @pl.when(m > 0)
        def _():
            dg_ref[...] += dg

    in_specs, args = [], []
    for a3, w3, _ in pairs:
        Kc = a3.shape[2]
        in_specs += [pl.BlockSpec((nk, tm, Kc), lambda m: (0, m, 0)),
                     pl.BlockSpec(w3.shape, lambda m: (0, 0, 0), pipeline_mode=pl.Buffered(1))]
        args += [a3, w3]
    tile = pl.BlockSpec((tm, D), lambda m: (m, 0))
    row = pl.BlockSpec((1, D), lambda m: (0, 0))
    return pl.pallas_call(
        body, name="dx_rms", grid=(n_m,),
        in_specs=in_specs + [tile, row, tile], out_specs=[tile, tile, row],
        out_shape=[S((L, D), F32), S((L, D), BF16), S((1, D), F32)], compiler_params=_params("arbitrary"),
    )(*args, h, g.reshape(1, D), dh_in)


def dact_plain(dhb, w3):
    L, D = dhb.shape
    ns, N, _ = w3.shape
    tm = _tile(L, 512)

    def body(d_ref, w_ref, o_ref):
        o_ref[0] = _dot(d_ref[...], w_ref[0], 1, 1)

    return pl.pallas_call(
        body, name="dact_plain", grid=(ns, L // tm),
        in_specs=[pl.BlockSpec((tm, D), lambda s, m: (m, 0)), pl.BlockSpec((1, N, D), lambda s, m: (s, 0, 0))],
        out_specs=pl.BlockSpec((1, tm, N), lambda s, m: (s, m, 0)),
        out_shape=S((ns, L, N), F32), compiler_params=_params("parallel", "parallel"),
    )(dhb, w3)


def dact_swiglu(dhb, wd, a3, b3):
    L, D = dhb.shape
    ns, F, _ = wd.shape
    tm = _tile(L, 512)

    def body(d_ref, w_ref, a_ref, b_ref, da_ref, db_ref):
        ds = _dot(d_ref[...], w_ref[0], 1, 1)
        a = a_ref[0].astype(F32)
        b = b_ref[0].astype(F32)
        sg = _sigmoid(a)
        da_ref[0] = (ds * b * (sg * (1.0 + a * (1.0 - sg)))).astype(BF16)
        db_ref[0] = (ds * (a * sg)).astype(BF16)

    t_spec = pl.BlockSpec((1, tm, F), lambda s, m: (s, m, 0))
    return pl.pallas_call(
        body, name="dact_swiglu", grid=(ns, L // tm),
        in_specs=[pl.BlockSpec((tm, D), lambda s, m: (m, 0)), pl.BlockSpec((1, F, D), lambda s, m: (s, 0, 0)), t_spec, t_spec],
        out_specs=[t_spec, t_spec], out_shape=[S((ns, L, F), BF16)] * 2,
        compiler_params=_params("parallel", "parallel"),
    )(dhb, wd, a3, b3)


def conv_bwd(z, conv_w, conv_b, gnorm, dyn):
    _, L, C = z.shape
    tm = _tile(L, 256)
    H = CONV_HALO
    T = tm + 2 * H

    def body(zb_ref, zbp_ref, zbn_ref, zc_ref, zcp_ref, zcn_ref, zv_ref, zvp_ref, zvn_ref, d_ref, dp_ref, dn_ref,
             w_ref, b_ref, g_ref, dz_ref, dw_ref, db_ref, dg_ref):
        m = pl.program_id(0)
        cat = lambda p, c, n: jnp.concatenate([p[0], c[0], n[0]], axis=0)
        zb, zc, zv, d = cat(zbp_ref, zb_ref, zbn_ref), cat(zcp_ref, zc_ref, zcn_ref), cat(zvp_ref, zv_ref, zvn_ref), cat(dp_ref, d_ref, dn_ref)
        grow = m * tm - H + lax.broadcasted_iota(jnp.int32, (T, C), 0)
        valid, v, v1, v2, cb, ya = _conv_core(zb, zc, zv, w_ref, b_ref[...], grow, L)
        dya, _ = _rms_bwd(ya, g_ref[...], d)
        dc = jnp.where(valid, dya * zb, 0.0)
        dv = w_ref[2:3, :] * dc + w_ref[1:2, :] * pltpu.roll(dc, T - 1, 0) + w_ref[0:1, :] * pltpu.roll(dc, T - 2, 0)
        dz_ref[0] = (dya * cb)[H:H + tm, :].astype(BF16)
        dz_ref[1] = (dv * zv)[H:H + tm, :].astype(BF16)
        dz_ref[2] = (dv * zc)[H:H + tm, :].astype(BF16)
        rs = lambda x: jnp.sum(x[H:H + tm, :], axis=0, keepdims=True)
        yh = ya * _rstd(ya)
        dw = jnp.concatenate([rs(dc * v2), rs(dc * v1), rs(dc * v)], axis=0)
        dbias, dg = rs(dc), rs(d * yh)

        @pl.when(m == 0)
        def _():
            dw_ref[...] = dw
            db_ref[...] = dbias
            dg_ref[...] = dg

        @pl.when(m > 0)
        def _():
            dw_ref[...] += dw
            db_ref[...] += dbias
            dg_ref[...] += dg

    row = lambda r: pl.BlockSpec((r, C), lambda m: (0, 0))
    specs = [*_conv_specs(L, tm, C, 0), *_conv_specs(L, tm, C, 1), *_conv_specs(L, tm, C, 2), *_conv_specs(L, tm, C, 0)]
    return pl.pallas_call(
        body, name="conv_bwd", grid=(L // tm,),
        in_specs=specs + [row(3), row(1), row(1)],
        out_specs=[pl.BlockSpec((3, tm, C), lambda m: (0, m, 0)), row(3), row(1), row(1)],
        out_shape=[S((3, L, C), BF16), S((3, C), F32), S((1, C), F32), S((1, C), F32)],
        compiler_params=_params("arbitrary"),
    )(z, z, z, z, z, z, z, z, z, dyn, dyn, dyn, conv_w, conv_b.reshape(1, C), gnorm.reshape(1, C))


def glu_bwd(y, w, b, gnorm, dn):
    L, C = y.shape
    tm = _tile(L, 256)

    def body(y_ref, w_ref, b_ref, g_ref, d_ref, dy_ref, dpre_ref, zg_ref, db_ref, dg_ref):
        m = pl.program_id(0)
        yv = y_ref[...]
        zg, t = _gelu(yv)
        zgb = zg.astype(BF16)
        sg = _sigmoid(_dot(zgb, w_ref[...], 1, 0) + b_ref[...])
        out = zg * sg
        dout, dg = _rms_bwd(out, g_ref[...], d_ref[...])
        dpre = dout * zg * sg * (1.0 - sg)
        dpre_b = dpre.astype(BF16)
        dzg = dout * sg + _dot(dpre_b, w_ref[...], 1, 1)
        dt = (1.0 - t * t) * _GELU_C * (1.0 + 3.0 * 0.044715 * yv * yv)
        dy_ref[...] = dzg * (0.5 * (1.0 + t) + 0.5 * yv * dt)
        dpre_ref[...] = dpre_b
        zg_ref[...] = zgb
        dbias = jnp.sum(dpre, axis=0, keepdims=True)

        @pl.when(m == 0)
        def _():
            db_ref[...] = dbias
            dg_ref[...] = dg

        @pl.when(m > 0)
        def _():
            db_ref[...] += dbias
            dg_ref[...] += dg

    tile = pl.BlockSpec((tm, C), lambda m: (m, 0))
    row = pl.BlockSpec((1, C), lambda m: (0, 0))
    return pl.pallas_call(
        body, name="glu_bwd", grid=(L // tm,),
        in_specs=[tile, pl.BlockSpec((C, C), lambda m: (0, 0)), row, row, tile],
        out_specs=[tile, tile, tile, row, row],
        out_shape=[S((L, C), F32), S((L, C), BF16), S((L, C), BF16), S((1, C), F32), S((1, C), F32)],
        compiler_params=_params("arbitrary"),
    )(y, w, b.reshape(1, C), gnorm.reshape(1, C), dn)


def _scan_bwd(gr_ref, gi_ref, hr_ref, hi_ref, lr, li, n_steps):
    W = gr_ref.shape[1]
    zero = jnp.zeros((N_SEG, W), F32)
    lic = -li

    def local(i, c):
        r = pl.multiple_of((n_steps - 1 - i) * N_SEG, N_SEG)
        pr, pi = _cmul(lr, lic, c[0], c[1])
        nr = pr + gr_ref[pl.ds(r, N_SEG), :]
        ni = pi + gi_ref[pl.ds(r, N_SEG), :]
        gr_ref[pl.ds(r, N_SEG), :] = nr
        gi_ref[pl.ds(r, N_SEG), :] = ni
        return nr, ni

    fr, fi = lax.fori_loop(0, n_steps, local, (zero, zero))
    qr, qi = _cpow(lr, lic, n_steps)
    row = lax.broadcasted_iota(jnp.int32, (N_SEG, W), 0)
    cr, ci = zero, zero
    for seg in range(N_SEG - 2, -1, -1):
        tr, ti = _cmul(qr, qi, cr, ci)
        sr = pltpu.roll(fr + tr, N_SEG - 1, 0)
        si = pltpu.roll(fi + ti, N_SEG - 1, 0)
        cr = jnp.where(row == seg, sr, cr)
        ci = jnp.where(row == seg, si, ci)

    def fix(i, c):
        pwr, pwi, ar, ai = c
        t = n_steps - 1 - i
        r = pl.multiple_of(t * N_SEG, N_SEG)
        pwr, pwi = _cmul(lr, lic, pwr, pwi)
        xr, xi = _cmul(pwr, pwi, cr, ci)
        g_r = gr_ref[pl.ds(r, N_SEG), :] + xr
        g_i = gi_ref[pl.ds(r, N_SEG), :] + xi
        gr_ref[pl.ds(r, N_SEG), :] = g_r
        gi_ref[pl.ds(r, N_SEG), :] = g_i
        rp = pl.multiple_of(jnp.maximum(t - 1, 0) * N_SEG, N_SEG)
        hpr = hr_ref[pl.ds(rp, N_SEG), :]
        hpi = hi_ref[pl.ds(rp, N_SEG), :]
        live = t > 0
        ar = ar + jnp.where(live, hpr * g_r + hpi * g_i, 0.0)
        ai = ai + jnp.where(live, hpr * g_i - hpi * g_r, 0.0)
        return pwr, pwi, ar, ai

    _, _, ar, ai = lax.fori_loop(0, n_steps, fix, (jnp.ones((N_SEG, W), F32), zero, zero, zero))
    last = pl.ds((n_steps - 1) * N_SEG, N_SEG)
    hpr = jnp.where(row == 0, 0.0, pltpu.roll(hr_ref[last, :], 1, 0))
    hpi = jnp.where(row == 0, 0.0, pltpu.roll(hi_ref[last, :], 1, 0))
    g_r, g_i = gr_ref[pl.ds(0, N_SEG), :], gi_ref[pl.ds(0, N_SEG), :]
    ar = ar + hpr * g_r + hpi * g_i
    ai = ai + hpr * g_i - hpi * g_r
    return jnp.sum(ar, axis=0, keepdims=True), jnp.sum(ai, axis=0, keepdims=True)


def ssm_bwd(us, dy, lam, bre, bim, cre, cim, dvec):
    L = us.shape[0]
    n_steps = L // N_SEG
    sp = _ssm_specs(L)

    def body(u_ref, dy_ref, lam_ref, bre_ref, bim_ref, cre_ref, cim_ref, d_ref,
             du_ref, dlam_ref, dbre_ref, dbim_ref, dcre_ref, dcim_ref, dd_ref, hr, hi, gr, gi):
        u = u_ref[...]
        ub = u.astype(BF16)
        dyv = dy_ref[...]
        dyb = dyv.astype(BF16)
        hr[...] = _dot(ub, bre_ref[0], 1, 0)
        hi[...] = _dot(ub, bim_ref[0], 1, 0)
        lr = jnp.broadcast_to(lam_ref[0:1, :], (N_SEG, 512))
        li = jnp.broadcast_to(lam_ref[1:2, :], (N_SEG, 512))
        _scan_fwd(hr, hi, lr, li, n_steps)
        dcre_ref[0] = _dot(hr[...].astype(BF16), dyb, 0, 0)
        dcim_ref[0] = -_dot(hi[...].astype(BF16), dyb, 0, 0)
        gr[...] = _dot(dyb, cre_ref[0], 1, 1)
        gi[...] = -_dot(dyb, cim_ref[0], 1, 1)
        dlr, dli = _scan_bwd(gr, gi, hr, hi, lr, li, n_steps)
        dlam_ref[...] = jnp.concatenate([dlr, dli], axis=0)
        grb, gib = gr[...].astype(BF16), gi[...].astype(BF16)
        du_ref[...] = _dot(grb, bre_ref[0], 1, 1) + _dot(gib, bim_ref[0], 1, 1) + d_ref[...] * dyv
        dbre_ref[0] = _dot(ub, grb, 0, 0)
        dbim_ref[0] = _dot(ub, gib, 0, 0)
        dd_ref[...] = jnp.sum(dyv * u, axis=0, keepdims=True)

    big = pltpu.VMEM((L, 512), F32)
    return pl.pallas_call(
        body, name="ssm_bwd", grid=(4,),
        in_specs=[sp["u"], sp["u"], sp["lam"], sp["bmat"], sp["bmat"], sp["cmat"], sp["cmat"], sp["d"]],
        out_specs=[sp["u"], sp["lam"], sp["bmat"], sp["bmat"], sp["cmat"], sp["cmat"], sp["d"]],
        out_shape=[S((L, 512), F32), S((2, 2048), F32), S((4, 128, 512), F32), S((4, 128, 512), F32),
                   S((4, 512, 128), F32), S((4, 512, 128), F32), S((1, 512), F32)],
        scratch_shapes=[big, big, big, big], compiler_params=_params("parallel"),
    )(us, dy, lam, bre, bim, cre, cim, dvec)


def _discretize(ar, ai, log_dt, br, bi):
    dt = jnp.exp(log_dt)
    mag = jnp.exp(ar * dt)
    ph = ai * dt
    lr, li = mag * jnp.cos(ph), mag * jnp.sin(ph)
    nr, ni = lr - 1.0, li
    den = ar * ar + ai * ai
    fr = (nr * ar + ni * ai) / den
    fi = (ni * ar - nr * ai) / den
    return lr, li, fr[..., None] * br - fi[..., None] * bi, fr[..., None] * bi + fi[..., None] * br


def ssm_prep(ar, ai, log_dt, br, bi):
    G, P, H = br.shape

    def body(ar_ref, ai_ref, dt_ref, br_ref, bi_ref, lr_ref, li_ref, bbr_ref, bbi_ref):
        lr_ref[...], li_ref[...], bbr_ref[...], bbi_ref[...] = _discretize(
            ar_ref[...], ai_ref[...], dt_ref[...], br_ref[...], bi_ref[...])

    return pl.pallas_call(
        body, name="ssm_prep",
        out_shape=[S((G, P), F32), S((G, P), F32), S((G, P, H), F32), S((G, P, H), F32)],
    )(ar, ai, log_dt.reshape(G, 1), br, bi)


def ssm_prep_bwd(ar, ai, log_dt, br, bi, dlr, dli, dbbr, dbbi):
    G, P, H = br.shape

    def body(ar_ref, ai_ref, dt_ref, br_ref, bi_ref, dlr_ref, dli_ref, dbbr_ref, dbbi_ref,
             dar_ref, dai_ref, ddt_ref, dbr_ref, dbi_ref):
        _, vjp = jax.vjp(_discretize, ar_ref[...], ai_ref[...], dt_ref[...], br_ref[...], bi_ref[...])
        dar_ref[...], dai_ref[...], ddt_ref[...], dbr_ref[...], dbi_ref[...] = vjp(
            (dlr_ref[...], dli_ref[...], dbbr_ref[...], dbbi_ref[...]))

    return pl.pallas_call(
        body, name="ssm_prep_bwd",
        out_shape=[S((G, P), F32), S((G, P), F32), S((G, 1), F32), S((G, P, H), F32), S((G, P, H), F32)],
    )(ar, ai, log_dt.reshape(G, 1), br, bi, dlr, dli, dbbr, dbbi)


def _block_diag(x):
    j, n, R, C = x.shape
    eye = jnp.eye(n, dtype=x.dtype)
    return (x[:, :, :, None, :] * eye[None, :, None, :, None]).reshape(j, n * R, n * C)


def _block_diag_take(x, R, C):
    j = x.shape[0]
    n = x.shape[1] // R
    x5 = x.reshape(j, n, R, n, C)
    return jnp.stack([x5[:, i, :, i, :] for i in range(n)], axis=1)


def _to_segments(x):
    L, C = x.shape
    return x.reshape(N_SEG, L // N_SEG, C).transpose(1, 0, 2).reshape(L, C)


def _from_segments(x):
    L, C = x.shape
    return x.reshape(L // N_SEG, N_SEG, C).transpose(1, 0, 2).reshape(L, C)


BIG = ("ffn1_w_gate", "ffn1_w_up", "ffn1_w_down", "w_in", "glu_w", "w_out",
       "ffn2_w_gate", "ffn2_w_up", "ffn2_w_down", "ple_w_gate", "ple_w_proj")
SMALL = ("ffn1_norm", "mix_norm", "conv_w", "conv_b", "ssm_A_re", "ssm_A_im", "ssm_B_re", "ssm_B_im", "ssm_C_re", "ssm_C_im",
         "ssm_D", "ssm_log_dt", "glu_b", "conv_out_norm", "ssm_out_norm", "ffn2_norm", "ple_norm")


def _ssm_mats(w):
    G, P, H = w["ssm_B_re"].shape
    lr, li, bbr, bbi = ssm_prep(w["ssm_A_re"], w["ssm_A_im"], w["ssm_log_dt"], w["ssm_B_re"], w["ssm_B_im"])
    lam = jnp.stack([lr.reshape(G * P), li.reshape(G * P)])
    bmat = lambda bb: _block_diag(bb.reshape(4, G // 4, P, H).transpose(0, 1, 3, 2)).astype(BF16)
    cmat = lambda c: _block_diag(c.reshape(4, G // 4, H, P).transpose(0, 1, 3, 2)).astype(BF16)
    return lam, bmat(bbr), bmat(bbi), cmat(w["ssm_C_re"]), cmat(w["ssm_C_im"]), w["ssm_D"].reshape(1, G * H)


def layer_fwd(h0, pb, w, before_last):
    L, D = h0.shape
    u1 = rmsnorm_fwd(h0, w["ffn1_norm"])
    a1, b1, s1 = ffn_up(u1, w["ffn1_w_gate"], w["ffn1_w_up"])
    h1, u2 = mm_shard_k(s1, w["ffn1_w_down"], h0, 0.5, w["mix_norm"])
    z = mm_shard_n(u2, w["w_in"], F32)
    ya_n = conv_fwd(z, w["conv_w"], w["conv_b"], w["conv_out_norm"])
    us = _to_segments(z[3])
    mats = w["mats"]
    y = ssm_fwd(us, *mats)
    ys_n = glu_fwd(y, w["glu_w"], w["glu_b"], w["ssm_out_norm"])
    ycat = jnp.stack([ya_n, _from_segments(ys_n)])
    h2, u3 = mm_shard_k(ycat, w["w_out"], h1, 1.0, w["ffn2_norm"])
    a2, b2, s2 = ffn_up(u3, w["ffn2_w_gate"], w["ffn2_w_up"])
    h3, un = mm_shard_k(s2, w["ffn2_w_down"], h2, 0.5, w["ple_norm"])
    h4 = ple_fwd(un, pb, w["ple_w_gate"], w["ple_w_proj"], h3, before_last(h3))
    saved = dict(h0=h0, u1=u1, a1=a1, b1=b1, s1=s1, h1=h1, u2=u2, z=z, us=us, mats=mats, y=y, ycat=ycat,
                 h2=h2, u3=u3, a2=a2, b2=b2, s2=s2, h3=h3, un=un)
    return h4, saved


def _ffn_bwd(dh, dhb, h_in, u, a, b, s, wg, wu, wd, gnorm, cast_scale):
    da, db = dact_swiglu(dhb, wd, a, b)
    g_wd, g_wg, g_wu = wgrad_sharded([(s, dhb), (da, u), (db, u)])
    dh_in, dhb_in, g_norm = dx_rms([(da, wg, True), (db, wu, True)], h_in, gnorm, dh, cast_scale)
    return dh_in, dhb_in, g_wg, g_wu, g_wd, g_norm


def layer_bwd(dh, pb, w, sv, token):
    L, D = dh.shape
    G, P, H = w["ssm_B_re"].shape
    dpre, dpp3 = ple_bwd(sv["un"], pb, w["ple_w_gate"], w["ple_w_proj"], dh, token)
    g_wpg = wgrad(sv["un"], dpre).reshape(N_SHARD, D // N_SHARD, D)
    g_wpp = wgrad(pb, dpp3)
    dh3, dhb3, g_nple = dx_rms([(dpre[None], w["ple_w_gate"][None], False)], sv["h3"], w["ple_norm"], dh, 0.5)
    dh2, dhb, g_wg2, g_wu2, g_wd2, g_nffn2 = _ffn_bwd(dh3, dhb3, sv["h2"], sv["u3"], sv["a2"], sv["b2"], sv["s2"],
                                                      w["ffn2_w_gate"], w["ffn2_w_up"], w["ffn2_w_down"], w["ffn2_norm"], 1.0)
    dyn = dact_plain(dhb, w["w_out"])
    g_wout = wgrad(sv["ycat"], dhb).reshape(N_SHARD, -1, D)
    dz_abc, g_convw, g_convb, g_nconv = conv_bwd(sv["z"], w["conv_w"], w["conv_b"], w["conv_out_norm"], dyn)
    dy, dpre_g, zg, g_glub, g_nssm = glu_bwd(sv["y"], w["glu_w"], w["glu_b"], w["ssm_out_norm"], _to_segments(dyn[1]))
    C = zg.shape[1]
    g_gluw = wgrad(zg, dpre_g).reshape(N_SHARD, C // N_SHARD, C)
    dus, dlam, dbre, dbim, dcre, dcim, dd = ssm_bwd(sv["us"], dy, *sv["mats"])
    take_b = lambda m: _block_diag_take(m, H, P).transpose(0, 1, 3, 2).reshape(G, P, H)
    take_c = lambda m: _block_diag_take(m, P, H).transpose(0, 1, 3, 2).reshape(G, H, P)
    g_ar, g_ai, g_dt, g_br, g_bi = ssm_prep_bwd(
        w["ssm_A_re"], w["ssm_A_im"], w["ssm_log_dt"], w["ssm_B_re"], w["ssm_B_im"],
        dlam[0].reshape(G, P), dlam[1].reshape(G, P), take_b(dbre), take_b(dbim))
    dz3 = jnp.concatenate([dz_abc, _from_segments(dus).astype(BF16)[None]], axis=0)
    g_win = wgrad(sv["u2"], dz3)
    dh1, dhb1, g_nmix = dx_rms([(dz3, w["w_in"], False)], sv["h1"], w["mix_norm"], dh2, 0.5)
    dh0, _, g_wg1, g_wu1, g_wd1, g_nffn1 = _ffn_bwd(dh1, dhb1, sv["h0"], sv["u1"], sv["a1"], sv["b1"], sv["s1"],
                                                    w["ffn1_w_gate"], w["ffn1_w_up"], w["ffn1_w_down"], w["ffn1_norm"], 1.0)
    big = [g_wg1, g_wu1, g_wd1, g_win, g_gluw, g_wout, g_wg2, g_wu2, g_wd2, g_wpg, g_wpp]
    small = dict(ffn1_norm=g_nffn1, mix_norm=g_nmix, conv_w=g_convw, conv_b=g_convb, ssm_A_re=g_ar, ssm_A_im=g_ai,
                 ssm_B_re=g_br, ssm_B_im=g_bi, ssm_C_re=take_c(dcre), ssm_C_im=take_c(dcim), ssm_D=dd,
                 ssm_log_dt=g_dt, glu_b=g_glub, conv_out_norm=g_nconv, ssm_out_norm=g_nssm, ffn2_norm=g_nffn2,
                 ple_norm=g_nple)
    return dh0, big, small


def local_step(x, p, target, final_norm, weights_of, before_last, on_grads):
    depth = p.shape[0]
    h = x
    layers, saved, pbs = [], [], []
    for i in range(depth):
        w = weights_of(i, h)
        pb = p[i].astype(BF16)
        h, sv = layer_fwd(h, pb, w, functools.partial(before_last, i))
        layers.append(w)
        saved.append(sv)
        pbs.append(pb)
    loss_part, dh, g_final = loss_head(h, final_norm, target)
    smalls = [None] * depth
    token = jnp.zeros((8, 128), F32)
    for i in reversed(range(depth)):
        dh, big, smalls[i] = layer_bwd(dh, pbs[i], layers[i], saved[i], token)
        token = on_grads(i, big, dh)
    return loss_part, dh, smalls, g_final


ROW_TILE_MAX = 512


def _row_tile(rows):
    for t in range(ROW_TILE_MAX, 0, -16):
        if rows % t == 0:
            return t
    return rows


def elementwise(fn, ins, out_dtypes, name):
    rows, cols = ins[0].shape
    tr = _row_tile(rows)
    n_in = len(ins)

    def body(*refs):
        outs = fn(*[r[...] for r in refs[:n_in]])
        for o_ref, o in zip(refs[n_in:], outs):
            o_ref[...] = o.astype(o_ref.dtype)

    spec = pl.BlockSpec((tr, cols), lambda i: (i, 0))
    return pl.pallas_call(
        body, name=name, grid=(rows // tr,), in_specs=[spec] * n_in, out_specs=[spec] * len(out_dtypes),
        out_shape=[S((rows, cols), d) for d in out_dtypes], compiler_params=_params("parallel"),
    )(*ins)


def _adamw(w, g, m, v):
    m = ADAM_B1 * m + (1.0 - ADAM_B1) * g
    v = ADAM_B2 * v + (1.0 - ADAM_B2) * (g * g)
    m_hat = m / (1.0 - ADAM_B1 ** ADAM_STEP)
    v_hat = v / (1.0 - ADAM_B2 ** ADAM_STEP)
    delta = -ADAM_LR * (m_hat / (jnp.sqrt(v_hat) + ADAM_EPS) + ADAM_WD * w)
    return delta, m, v


ANY = pl.BlockSpec(memory_space=pl.ANY)


def _mesh_pos():
    return lax.axis_index("x"), lax.axis_index("y"), lax.axis_index("c")


def _other_chips(x, y):
    return [(1 - x, y), (x, 1 - y), (1 - x, 1 - y)]


def _remote(src, dst, send_sem, recv_sem, device):
    return pltpu.make_async_remote_copy(src_ref=src, dst_ref=dst, send_sem=send_sem, recv_sem=recv_sem,
                                        device_id=device, device_id_type=MESH)


def gather_weights(ws):
    n = len(ws)

    def body(*refs):
        outs = refs[n:2 * n]
        send_sems, recv_sems = refs[2 * n:]
        x, y, c = _mesh_pos()
        me_s = 2 * x + y
        sibling = (x, y, 1 - c)
        chips = _other_chips(x, y)
        n_half = outs[0].shape[0] // 2
        mine, other = pl.ds(c * n_half, n_half), pl.ds((1 - c) * n_half, n_half)
        sent = []
        for t in range(n):
            for j, (cx, cy) in enumerate(chips):
                blk = outs[t].at[mine, me_s]
                cp = _remote(blk, blk, send_sems.at[t, j], recv_sems.at[t, j], (cx, cy, c))
                cp.start()
                sent.append(cp)
        for j, (cx, cy) in enumerate(chips):
            for t in range(n):
                blk = outs[t].at[mine, 2 * cx + cy]
                _remote(blk, blk, send_sems.at[t, j], recv_sems.at[t, j], (cx, cy, c)).wait_recv()
                cp = _remote(blk, blk, send_sems.at[t, 3 + j], recv_sems.at[t, 3 + j], sibling)
                cp.start()
                sent.append(cp)
        for j, (cx, cy) in enumerate(chips):
            for t in range(n):
                blk = outs[t].at[other, 2 * cx + cy]
                _remote(blk, blk, send_sems.at[t, 3 + j], recv_sems.at[t, 3 + j], sibling).wait_recv()
        for cp in sent:
            cp.wait_send()

    return pl.pallas_call(
        body, name="gather_weights", in_specs=[ANY] * n, out_specs=[ANY] * n,
        out_shape=[S(w.shape, w.dtype) for w in ws], input_output_aliases={t: t for t in range(n)},
        scratch_shapes=[pltpu.SemaphoreType.DMA((n, 6)), pltpu.SemaphoreType.DMA((n, 6))],
    )(*ws)


HBM = pl.BlockSpec(memory_space=pltpu.HBM)
SEM = pl.BlockSpec(memory_space=pltpu.SEMAPHORE)
VMEM_WHOLE = pl.BlockSpec(memory_space=pltpu.VMEM)
SPLIT_COPY = pltpu.CompilerParams(has_side_effects=pltpu.SideEffectType.DATAFLOW_SIDE_EFFECTING)


def _hbm(x):
    return pltpu.with_memory_space_constraint(x, pltpu.HBM)


def _half_rows(ref, c):
    r2 = ref.shape[1] // 2
    return pl.ds(pl.multiple_of(c * r2, 8), r2)


def _gather_copies(bufs, send_sems, recv_sems, forward):
    x, y, c = _mesh_pos()
    copies = []
    for t in range(len(bufs)):
        rows = _half_rows(bufs[t], c)
        for j, (cx, cy) in enumerate(_other_chips(x, y)):
            blk = bufs[t].at[2 * cx + cy if forward else 2 * x + y, rows]
            peer = (x, y, 1 - c) if forward else (cx, cy, c)
            copies.append(_remote(blk, blk, send_sems.at[3 * t + j], recv_sems.at[3 * t + j], peer))
    return copies


def gather_start(bufs, layer, after, forward=False):
    n, k = len(bufs), len(after)

    def body(*refs):
        ins, send_sems, recv_sems, token = refs[:n], refs[n + k], refs[n + k + 1], refs[2 * n + k + 2]
        for cp in _gather_copies(ins, send_sems, recv_sems, forward):
            cp.start()
        token[...] = jnp.zeros_like(token)

    outs = pl.pallas_call(
        body, name=f"{'forward' if forward else 'gather'}_start_{layer}", in_specs=[HBM] * n + [ANY] * k,
        out_specs=[SEM, SEM] + [HBM] * n + [VMEM_WHOLE],
        out_shape=[pltpu.SemaphoreType.DMA((3 * n,)), pltpu.SemaphoreType.DMA((3 * n,))]
        + [pltpu.HBM(b.shape, b.dtype) for b in bufs] + [S((8, 128), F32)],
        input_output_aliases={t: t + 2 for t in range(n)}, compiler_params=SPLIT_COPY,
    )(*[_hbm(b) for b in bufs], *after)
    return outs[0], outs[1], list(outs[2:2 + n]), outs[2 + n]


def gather_wait(bufs, send_sems, recv_sems, after, layer, forward=False):
    n, n_after = len(bufs), len(after)

    def body(*refs):
        ins, send_ref, recv_ref = refs[:n], refs[n], refs[n + 1]
        for cp in _gather_copies(ins, send_ref, recv_ref, forward):
            cp.wait_send()
            cp.wait_recv()

    outs = pl.pallas_call(
        body, name=f"{'forward' if forward else 'gather'}_wait_{layer}", in_specs=[HBM] * n + [SEM, SEM] + [ANY] * n_after,
        out_specs=[HBM] * n,
        out_shape=[pltpu.HBM(b.shape, b.dtype) for b in bufs],
        input_output_aliases={t: t for t in range(n)}, compiler_params=SPLIT_COPY,
    )(*bufs, send_sems, recv_sems, *after)
    return list(outs)


def gather_forward(bufs):
    n = len(bufs)

    def body(*refs):
        outs = refs[n:2 * n]
        send_sems, recv_sems = refs[2 * n:]
        copies = _gather_copies(outs, send_sems, recv_sems, True)
        for cp in copies:
            cp.start()
        for cp in copies:
            cp.wait()

    return pl.pallas_call(
        body, name="gather_forward", in_specs=[ANY] * n, out_specs=[ANY] * n,
        out_shape=[S(b.shape, b.dtype) for b in bufs], input_output_aliases={t: t for t in range(n)},
        scratch_shapes=[pltpu.SemaphoreType.DMA((3 * n,)), pltpu.SemaphoreType.DMA((3 * n,))],
    )(*bufs)


def chips_start(sums, layer):
    n = len(sums)
    lands = [lax.empty((3,) + s.shape[1:], s.dtype) for s in sums]

    def body(*refs):
        a, land, send_sems, recv_sems, token = refs[:n], refs[n:2 * n], refs[2 * n], refs[2 * n + 1], refs[4 * n + 2]
        x, y, c = _mesh_pos()
        for t in range(n):
            for j, (cx, cy) in enumerate(_other_chips(x, y)):
                _remote(a[t].at[2 * cx + cy], land[t].at[j], send_sems.at[3 * t + j], recv_sems.at[3 * t + j], (cx, cy, c)).start()
        token[...] = jnp.zeros_like(token)

    outs = pl.pallas_call(
        body, name=f"chips_start_{layer}", in_specs=[HBM] * (2 * n), out_specs=[SEM, SEM] + [HBM] * (2 * n) + [VMEM_WHOLE],
        out_shape=[pltpu.SemaphoreType.DMA((3 * n,)), pltpu.SemaphoreType.DMA((3 * n,))]
        + [pltpu.HBM(b.shape, b.dtype) for b in sums + lands] + [S((8, 128), F32)],
        input_output_aliases={t: t + 2 for t in range(2 * n)}, compiler_params=SPLIT_COPY,
    )(*[_hbm(b) for b in sums + lands])
    return outs[0], outs[1], list(outs[2:2 + n]), list(outs[2 + n:2 + 2 * n]), outs[2 + 2 * n]


def chips_wait(sums, lands, send_sems, recv_sems, after, layer):
    n, n_after = len(sums), len(after)

    def body(*refs):
        a, land, send_ref, recv_ref = refs[:n], refs[n:2 * n], refs[2 * n], refs[2 * n + 1]
        x, y, c = _mesh_pos()
        for t in range(n):
            for j, (cx, cy) in enumerate(_other_chips(x, y)):
                cp = _remote(a[t].at[2 * cx + cy], land[t].at[j], send_ref.at[3 * t + j], recv_ref.at[3 * t + j], (cx, cy, c))
                cp.wait_send()
                cp.wait_recv()

    outs = pl.pallas_call(
        body, name=f"chips_wait_{layer}", in_specs=[HBM] * (2 * n) + [SEM, SEM] + [ANY] * n_after, out_specs=[HBM] * (2 * n),
        out_shape=[pltpu.HBM(b.shape, b.dtype) for b in sums + lands],
        input_output_aliases={t: t for t in range(2 * n)}, compiler_params=SPLIT_COPY,
    )(*sums, *lands, send_sems, recv_sems, *after)
    return list(outs[n:])


def cast_place_layer(w, layer, pos, dtype, token):
    _, r, c = w.shape
    tr = _row_tile(r)

    def body(pos_ref, w_ref, tok_ref, o_ref):
        o_ref[0] = (w_ref[0] + tok_ref[0:1, 0:1]).astype(dtype)

    return pl.pallas_call(
        body, name="cast_place_layer",
        grid_spec=pltpu.PrefetchScalarGridSpec(
            num_scalar_prefetch=1, grid=(r // tr,),
            in_specs=[pl.BlockSpec((1, tr, c), lambda i, pos: (layer, i, 0)), pl.BlockSpec((8, 128), lambda i, pos: (0, 0))],
            out_specs=pl.BlockSpec((1, tr, c), lambda i, pos: (pos[1], i, 0))),
        out_shape=S((N_SHARD, r, c), dtype), compiler_params=_params("parallel"),
    )(pos, w, token)


def cast_place(w, pos, dtype):
    layers, r, c = w.shape
    tr = _row_tile(r)

    def body(pos_ref, w_ref, o_ref):
        o_ref[0, 0] = w_ref[0].astype(dtype)

    return pl.pallas_call(
        body, name="cast_place",
        grid_spec=pltpu.PrefetchScalarGridSpec(
            num_scalar_prefetch=1, grid=(layers, r // tr),
            in_specs=[pl.BlockSpec((1, tr, c), lambda l, i, pos: (l, i, 0))],
            out_specs=pl.BlockSpec((1, 1, tr, c), lambda l, i, pos: (l, pos[1], i, 0))),
        out_shape=S((layers, N_SHARD, r, c), dtype), compiler_params=_params("parallel", "parallel"),
    )(pos, w)


def _pair_copy(g_ref, got_ref, send_sem, recv_sem):
    x, y, c = _mesh_pos()
    r2 = g_ref.shape[1] // 2
    give = pl.ds(pl.multiple_of((1 - c) * r2, 8), r2)
    return _remote(g_ref.at[:, give], got_ref, send_sem, recv_sem, (x, y, 1 - c))


def reduce_pair(gs, after):
    n, k = len(gs), len(after)

    def body(*refs):
        ins, got = refs[:n], refs[n + k:2 * n + k]
        send_sems, recv_sems = refs[2 * n + k:]
        copies = [_pair_copy(ins[t], got[t], send_sems.at[t], recv_sems.at[t]) for t in range(n)]
        for cp in copies:
            cp.start()
        for cp in copies:
            cp.wait()

    return pl.pallas_call(
        body, name="reduce_pair", in_specs=[ANY] * (n + k), out_specs=[ANY] * n,
        out_shape=[S((g.shape[0], g.shape[1] // 2, g.shape[2]), g.dtype) for g in gs],
        scratch_shapes=[pltpu.SemaphoreType.DMA((n,)), pltpu.SemaphoreType.DMA((n,))],
    )(*gs, *after)


def pair_sum(g, got, pos):
    ns, r2, c = got.shape
    tr = _row_tile(r2)
    n_i = r2 // tr

    def body(pos_ref, g_ref, got_ref, sum_ref, own_ref):
        s = pl.program_id(1)
        v = g_ref[0].astype(F32) + got_ref[0].astype(F32)
        sum_ref[0] = v.astype(BF16)

        @pl.when(s == pos_ref[1])
        def _():
            own_ref[...] = v

    return pl.pallas_call(
        body, name="pair_sum",
        grid_spec=pltpu.PrefetchScalarGridSpec(
            num_scalar_prefetch=1, grid=(n_i, ns),
            in_specs=[pl.BlockSpec((1, tr, c), lambda i, s, pos: (s, pos[0] * n_i + i, 0)),
                      pl.BlockSpec((1, tr, c), lambda i, s, pos: (s, i, 0))],
            out_specs=[pl.BlockSpec((1, tr, c), lambda i, s, pos: (s, i, 0)), pl.BlockSpec((tr, c), lambda i, s, pos: (i, 0))]),
        out_shape=[S((ns, r2, c), BF16), S((r2, c), F32)], compiler_params=_params("parallel", "arbitrary"),
    )(pos, g, got)


def chip_sum(own, p2, pos):
    r2, c = own.shape
    tr = _row_tile(r2)
    n_i = r2 // tr

    def body(pos_ref, own_ref, a_ref, b_ref, c_ref, o_ref):
        o_ref[...] = own_ref[...] + a_ref[0].astype(F32) + b_ref[0].astype(F32) + c_ref[0].astype(F32)

    peer = lambda j: pl.BlockSpec((1, tr, c), lambda i, pos: (j, i, 0))
    return pl.pallas_call(
        body, name="chip_sum",
        grid_spec=pltpu.PrefetchScalarGridSpec(
            num_scalar_prefetch=1, grid=(n_i,),
            in_specs=[pl.BlockSpec((tr, c), lambda i, pos: (i, 0)), peer(0), peer(1), peer(2)],
            out_specs=pl.BlockSpec((tr, c), lambda i, pos: (pos[0] * n_i + i, 0))),
        out_shape=S((2 * r2, c), F32), compiler_params=_params("parallel"),
    )(pos, own, p2, p2, p2)


def exchange_halves(rs):
    n = len(rs)

    def body(*refs):
        outs = refs[n:2 * n]
        send_sems, recv_sems = refs[2 * n:]
        x, y, c = _mesh_pos()
        copies = []
        for t in range(n):
            r2 = outs[t].shape[0] // 2
            rows = outs[t].at[pl.ds(pl.multiple_of(c * r2, 8), r2)]
            cp = _remote(rows, rows, send_sems.at[t], recv_sems.at[t], (x, y, 1 - c))
            cp.start()
            copies.append(cp)
        for cp in copies:
            cp.wait()

    return pl.pallas_call(
        body, name="exchange_halves", in_specs=[ANY] * n, out_specs=[ANY] * n,
        out_shape=[S(r.shape, r.dtype) for r in rs], input_output_aliases={t: t for t in range(n)},
        scratch_shapes=[pltpu.SemaphoreType.DMA((n,)), pltpu.SemaphoreType.DMA((n,))],
    )(*rs)


def allreduce_small(vec):
    R = vec.shape[0]
    H = R // 2

    def body(x_ref, o_ref, pair_buf, chip_buf, send_sems, recv_sems):
        x, y, c = _mesh_pos()
        me_s = 2 * x + y
        sibling = (x, y, 1 - c)
        mine = pl.ds(pl.multiple_of(c * H, 8), H)
        give = pl.ds(pl.multiple_of((1 - c) * H, 8), H)
        cp = _remote(x_ref.at[give], pair_buf, send_sems.at[0], recv_sems.at[0], sibling)
        cp.start()
        cp.wait()
        chip_buf[me_s] = x_ref[mine, :] + pair_buf[...]
        copies = []
        for j, (cx, cy) in enumerate(_other_chips(x, y)):
            cp = _remote(chip_buf.at[me_s], chip_buf.at[me_s], send_sems.at[1 + j], recv_sems.at[1 + j], (cx, cy, c))
            cp.start()
            copies.append(cp)
        for cp in copies:
            cp.wait()
        o_ref[mine, :] = (chip_buf[0] + chip_buf[1]) + (chip_buf[2] + chip_buf[3])
        cp = _remote(o_ref.at[mine], o_ref.at[mine], send_sems.at[4], recv_sems.at[4], sibling)
        cp.start()
        cp.wait()

    vm = pl.BlockSpec(memory_space=pltpu.VMEM)
    return pl.pallas_call(
        body, name="allreduce_small", in_specs=[vm], out_specs=vm, out_shape=S((R, 128), F32),
        scratch_shapes=[pltpu.VMEM((H, 128), F32), pltpu.VMEM((N_SHARD, H, 128), F32),
                        pltpu.SemaphoreType.DMA((5,)), pltpu.SemaphoreType.DMA((5,))],
        compiler_params=pltpu.CompilerParams(vmem_limit_bytes=VMEM_LIMIT_BYTES),
    )(vec)


def adamw_layer(w, g, m, v, layer, prev):
    _, r, c = w.shape
    tr = _row_tile(r)

    def body(w_ref, g_ref, m_ref, v_ref, *rest):
        outs = rest[-4:]
        g_val = g_ref[...]
        outs[0][0] = g_val
        outs[1][0], outs[2][0], outs[3][0] = _adamw(w_ref[0], g_val, m_ref[0], v_ref[0])

    lay = pl.BlockSpec((1, tr, c), lambda i: (layer, i, 0))
    prev = list(prev) if prev else []
    return pl.pallas_call(
        body, name="adamw_layer", grid=(r // tr,),
        in_specs=[lay, pl.BlockSpec((tr, c), lambda i: (i, 0)), lay, lay] + [ANY] * len(prev),
        out_specs=[lay] * 4, out_shape=[S(w.shape, F32)] * 4,
        input_output_aliases={4 + k: k for k in range(len(prev))}, compiler_params=_params("parallel"),
    )(w, g, m, v, *prev)


def reduce_begin(gs, pos, layer, after):
    got = reduce_pair(gs, after)
    sums, own = zip(*[pair_sum(g, o, pos) for g, o in zip(gs, got)])
    send_sems, recv_sems, sums, lands, token = chips_start(list(sums), layer)
    return dict(own=own, sums=sums, lands=lands, sems=(send_sems, recv_sems), token=token, layer=layer)


def reduce_end(pending, pos, after):
    lands = chips_wait(pending["sums"], pending["lands"], *pending["sems"], after, pending["layer"])
    return exchange_halves([chip_sum(o, p, pos) for o, p in zip(pending["own"], lands)])


W_NAMES = ("ffn1_norm", "ffn1_w_gate", "ffn1_w_up", "ffn1_w_down", "mix_norm", "w_in", "conv_w", "conv_b", "ssm_A_re", "ssm_A_im",
           "ssm_B_re", "ssm_B_im", "ssm_C_re", "ssm_C_im", "ssm_D", "ssm_log_dt", "glu_w", "glu_b", "conv_out_norm", "ssm_out_norm",
           "w_out", "ffn2_norm", "ffn2_w_gate", "ffn2_w_up", "ffn2_w_down", "ple_norm", "ple_w_gate", "ple_w_proj", "final_norm")
SMALL_ALL = SMALL + ("final_norm",)
TRANSPOSED = ("ffn1_w_gate", "ffn1_w_up", "ffn2_w_gate", "ffn2_w_up")
PACK = ROW_TILE_MAX * 128


def _pack(parts):
    flat = jnp.concatenate([p.reshape(-1) for p in parts])
    pad = (-flat.shape[0]) % PACK
    return jnp.pad(flat, (0, pad)).reshape(-1, 128)


def _unpack(vec, shapes):
    flat = vec.reshape(-1)
    out, off = [], 0
    for shp in shapes:
        size = math.prod(shp)
        out.append(flat[off:off + size].reshape(shp))
        off += size
    return out


def _step(a):
    a = {k: jnp.swapaxes(v, 1, 2) if k.removeprefix("m_").removeprefix("v_") in TRANSPOSED else v for k, v in a.items()}
    x, p, target = a["x"][0], a["p"][:, 0], a["loss_target"][0]
    depth = p.shape[0]
    L, D = x.shape
    me_s = 2 * lax.axis_index("x") + lax.axis_index("y")

    pos = jnp.stack([lax.axis_index("c"), me_s]).astype(jnp.int32)
    conv_w = gather_weights([cast_place(a["conv_w"], pos, F32)])[0]
    started, token = [], jnp.zeros((8, 128), F32)
    for l in range(depth):
        started.append(gather_start([cast_place_layer(a[n], l, pos, BF16, token) for n in BIG], l, [conv_w]))
        token = started[-1][3]

    mats = [_ssm_mats({n: a[n][l] for n in SMALL}) for l in range(depth)]
    packed_state = [_pack([a[prefix + n] for n in SMALL_ALL]) for prefix in ("", "m_", "v_")]
    early_work = [leaf for m in mats for leaf in m] + packed_state
    forwarding = {}

    def before_last(l, h3):
        if l + 1 == depth:
            return jnp.zeros((8, 128), F32)
        send_sems, recv_sems, bufs, _ = started[l + 1]
        forwarding[l + 1] = gather_start(gather_wait(bufs, send_sems, recv_sems, [h3], l + 1), l + 1, [], forward=True)
        return forwarding[l + 1][3]

    def weights_of(l, h):
        if l:
            send_sems, recv_sems, bufs, _ = forwarding[l]
            full = gather_wait(bufs, send_sems, recv_sems, [h], l, forward=True)
        else:
            send_sems, recv_sems, bufs, _ = started[0]
            full = gather_forward(gather_wait(bufs, send_sems, recv_sems, [s[3] for s in started] + early_work, 0))
        w = {n: a[n][l] for n in SMALL if n != "conv_w"}
        w["mats"] = mats[l]
        w.update(dict(zip(BIG, full)))
        C = w["glu_w"].shape[-1]
        w["glu_w"] = w["glu_w"].reshape(C, C)
        w["w_out"] = w["w_out"].reshape(2, -1, D)
        w["ple_w_gate"] = w["ple_w_gate"].reshape(D, D)
        w["conv_w"] = conv_w[l].transpose(1, 0, 2).reshape(3, -1)
        return w

    pending, first_layer_grads = {}, []

    def on_grads(l, big, dh):
        if l == 0:
            first_layer_grads.extend(big)
            return None
        pending[l] = reduce_begin(big, pos, l, [])
        return pending[l]["token"]

    loss_part, dx, smalls, g_final = local_step(x, p, target, a["final_norm"], weights_of, before_last, on_grads)
    small_shapes = [(depth,) + smalls[0][n].shape for n in SMALL] + [g_final.shape, (1,)]
    parts = [smalls[l][n] for n in SMALL for l in range(depth)] + [g_final, loss_part[0, 0:1]]
    summed_vec = allreduce_small(_pack(parts))
    pending[0] = reduce_begin(first_layer_grads, pos, 0, [summed_vec])
    stacked = [None] * len(BIG)
    for l in reversed(range(depth)):
        after = [pending[0]["token"]] if l else [s[3] for s in stacked]
        reduced = reduce_end(pending[l], pos, after)
        stacked = [adamw_layer(a[n], reduced[i], a["m_" + n], a["v_" + n], l, stacked[i]) for i, n in enumerate(BIG)]
    big_out = {n: [jnp.swapaxes(o, 1, 2) for o in outs] if n in TRANSPOSED else outs for n, outs in zip(BIG, stacked)}

    summed = _unpack(summed_vec, small_shapes)
    g_small = dict(zip(SMALL_ALL, summed[:-1]))
    loss = summed[-1][0]
    n_conv = a["conv_w"].shape[-1]
    g_small["conv_w"] = lax.dynamic_slice_in_dim(g_small["conv_w"], me_s * n_conv, n_conv, axis=2)
    g_small = {n: g_small[n].reshape(a[n].shape) for n in SMALL_ALL}
    packed = [packed_state[0], _pack([g_small[n] for n in SMALL_ALL]), packed_state[1], packed_state[2]]
    shapes = [a[n].shape for n in SMALL_ALL]
    d_s, m_s, v_s = [dict(zip(SMALL_ALL, _unpack(o, shapes))) for o in elementwise(_adamw, packed, [F32, F32, F32], "adamw_small")]

    outs = {n: big_out[n] if n in big_out else (g_small[n], d_s[n], m_s[n], v_s[n]) for n in W_NAMES}
    return (loss, dx[None], *[outs[n][0] for n in W_NAMES], *[outs[n][1] for n in W_NAMES],
            *[outs[n][2] for n in W_NAMES], *[outs[n][3] for n in W_NAMES])


def kernel(x, p, ffn1_norm, ffn1_w_gate, ffn1_w_up, ffn1_w_down, mix_norm, w_in, conv_w, conv_b, ssm_A_re, ssm_A_im, ssm_B_re, ssm_B_im, ssm_C_re, ssm_C_im, ssm_D, ssm_log_dt, glu_w, glu_b, conv_out_norm, ssm_out_norm, w_out, ffn2_norm, ffn2_w_gate, ffn2_w_up, ffn2_w_down, ple_norm, ple_w_gate, ple_w_proj, final_norm, loss_target, m_ffn1_norm, m_ffn1_w_gate, m_ffn1_w_up, m_ffn1_w_down, m_mix_norm, m_w_in, m_conv_w, m_conv_b, m_ssm_A_re, m_ssm_A_im, m_ssm_B_re, m_ssm_B_im, m_ssm_C_re, m_ssm_C_im, m_ssm_D, m_ssm_log_dt, m_glu_w, m_glu_b, m_conv_out_norm, m_ssm_out_norm, m_w_out, m_ffn2_norm, m_ffn2_w_gate, m_ffn2_w_up, m_ffn2_w_down, m_ple_norm, m_ple_w_gate, m_ple_w_proj, m_final_norm, v_ffn1_norm, v_ffn1_w_gate, v_ffn1_w_up, v_ffn1_w_down, v_mix_norm, v_w_in, v_conv_w, v_conv_b, v_ssm_A_re, v_ssm_A_im, v_ssm_B_re, v_ssm_B_im, v_ssm_C_re, v_ssm_C_im, v_ssm_D, v_ssm_log_dt, v_glu_w, v_glu_b, v_conv_out_norm, v_ssm_out_norm, v_w_out, v_ffn2_norm, v_ffn2_w_gate, v_ffn2_w_up, v_ffn2_w_down, v_ple_norm, v_ple_w_gate, v_ple_w_proj, v_final_norm):
    return _step(dict(locals()))
```

```python
import functools
import math

import jax
import jax.numpy as jnp
from jax import lax
from jax.experimental import pallas as pl
from jax.experimental.pallas import tpu as pltpu

F32, BF16 = jnp.float32, jnp.bfloat16
S = jax.ShapeDtypeStruct
EPS = 1e-6
N_SEG = 8
N_SHARD = 4
N_DEV = 8
VMEM_LIMIT_BYTES = 56 * 1024 * 1024
ADAM_LR, ADAM_B1, ADAM_B2, ADAM_EPS, ADAM_WD, ADAM_STEP = 0.001, 0.9, 0.999, 1e-08, 0.01, 10
MESH = pl.DeviceIdType.MESH


def _params(*sem):
    return pltpu.CompilerParams(dimension_semantics=sem if sem else None, vmem_limit_bytes=VMEM_LIMIT_BYTES)


def _dot(a, b, ca, cb):
    return lax.dot_general(a, b, (((ca,), (cb,)), ((), ())), preferred_element_type=F32)


def _sigmoid(x):
    return 1.0 / (1.0 + jnp.exp(-x))


def _rstd(x):
    return lax.rsqrt(jnp.mean(x * x, axis=-1, keepdims=True) + EPS)


def _rms_bwd(x, g, dy):
    r = _rstd(x)
    xh = x * r
    dxh = dy * g
    dx = r * (dxh - xh * jnp.mean(dxh * xh, axis=-1, keepdims=True))
    return dx, jnp.sum(dy * xh, axis=0, keepdims=True)


def _tile(n, want):
    return want if n % want == 0 else n


def rmsnorm_fwd(h, g):
    L, D = h.shape
    tm = _tile(L, 512)

    def body(h_ref, g_ref, o_ref):
        x = h_ref[...]
        o_ref[...] = (x * _rstd(x) * g_ref[...]).astype(BF16)

    return pl.pallas_call(
        body, name="rmsnorm_fwd", grid=(L // tm,),
        in_specs=[pl.BlockSpec((tm, D), lambda m: (m, 0)), pl.BlockSpec((1, D), lambda m: (0, 0))],
        out_specs=pl.BlockSpec((tm, D), lambda m: (m, 0)),
        out_shape=S((L, D), BF16), compiler_params=_params("parallel"),
    )(h, g.reshape(1, D))


def ffn_up(u, wg, wu):
    L, D = u.shape
    ns, F, _ = wg.shape
    tm = _tile(L, 512)

    def body(u_ref, wg_ref, wu_ref, a_ref, b_ref, s_ref):
        x = u_ref[...]
        a = _dot(x, wg_ref[0], 1, 1)
        b = _dot(x, wu_ref[0], 1, 1)
        a_ref[0] = a.astype(BF16)
        b_ref[0] = b.astype(BF16)
        s_ref[0] = (a * _sigmoid(a) * b).astype(BF16)

    w_spec = pl.BlockSpec((1, F, D), lambda s, m: (s, 0, 0))
    o_spec = pl.BlockSpec((1, tm, F), lambda s, m: (s, m, 0))
    return pl.pallas_call(
        body, name="ffn_up", grid=(ns, L // tm),
        in_specs=[pl.BlockSpec((tm, D), lambda s, m: (m, 0)), w_spec, w_spec],
        out_specs=[o_spec, o_spec, o_spec],
        out_shape=[S((ns, L, F), BF16)] * 3, compiler_params=_params("parallel", "parallel"),
    )(u, wg, wu)


def mm_shard_n(u, w3, out_dtype):
    L, K = u.shape
    ns, _, N = w3.shape
    tm = _tile(L, 512)

    def body(u_ref, w_ref, o_ref):
        o_ref[0] = _dot(u_ref[...], w_ref[0], 1, 0).astype(out_dtype)

    return pl.pallas_call(
        body, name="mm_shard_n", grid=(ns, L // tm),
        in_specs=[pl.BlockSpec((tm, K), lambda s, m: (m, 0)), pl.BlockSpec((1, K, N), lambda s, m: (s, 0, 0))],
        out_specs=pl.BlockSpec((1, tm, N), lambda s, m: (s, m, 0)),
        out_shape=S((ns, L, N), out_dtype), compiler_params=_params("parallel", "parallel"),
    )(u, w3)


def mm_shard_k(a3, w3, res, scale, g_next):
    nk, L, Kc = a3.shape
    N = w3.shape[2]
    tm = _tile(L, 512)

    def body(a_ref, w_ref, r_ref, g_ref, o_ref, u_ref):
        acc = _dot(a_ref[0], w_ref[0], 1, 0)
        for k in range(1, nk):
            acc += _dot(a_ref[k], w_ref[k], 1, 0)
        h = r_ref[...] + scale * acc
        o_ref[...] = h
        u_ref[...] = (h * _rstd(h) * g_ref[...]).astype(BF16)

    tile = pl.BlockSpec((tm, N), lambda m: (m, 0))
    return pl.pallas_call(
        body, name="mm_shard_k", grid=(L // tm,),
        in_specs=[pl.BlockSpec((nk, tm, Kc), lambda m: (0, m, 0)), pl.BlockSpec((nk, Kc, N), lambda m: (0, 0, 0)),
                  tile, pl.BlockSpec((1, N), lambda m: (0, 0))],
        out_specs=[tile, tile],
        out_shape=[S((L, N), F32), S((L, N), BF16)], compiler_params=_params("parallel"),
    )(a3, w3, res, g_next.reshape(1, N))


CONV_HALO = 8


def _conv_specs(L, tm, C, shard):
    nb = L // CONV_HALO
    per = tm // CONV_HALO
    main = pl.BlockSpec((1, tm, C), lambda m: (shard, m, 0))
    prev = pl.BlockSpec((1, CONV_HALO, C), lambda m: (shard, jnp.maximum(m * per - 1, 0), 0))
    nxt = pl.BlockSpec((1, CONV_HALO, C), lambda m: (shard, jnp.minimum((m + 1) * per, nb - 1), 0))
    return main, prev, nxt


def _conv_core(zb, zc, zv, w_ref, bias, grow, L):
    valid = (grow >= 0) & (grow < L)
    v = jnp.where(valid, zc * zv, 0.0)
    v1 = pltpu.roll(v, 1, 0)
    v2 = pltpu.roll(v, 2, 0)
    cb = w_ref[0:1, :] * v2 + w_ref[1:2, :] * v1 + w_ref[2:3, :] * v + bias
    return valid, v, v1, v2, cb, zb * cb


def conv_fwd(z, conv_w, conv_b, gnorm):
    _, L, C = z.shape
    tm = _tile(L, 256)
    H = CONV_HALO

    def body(zb_ref, zc_ref, zcp_ref, zv_ref, zvp_ref, w_ref, b_ref, g_ref, o_ref):
        m = pl.program_id(0)
        zc = jnp.concatenate([zcp_ref[0], zc_ref[0]], axis=0)
        zv = jnp.concatenate([zvp_ref[0], zv_ref[0]], axis=0)
        grow = m * tm - H + lax.broadcasted_iota(jnp.int32, (tm + H, C), 0)
        valid = grow >= 0
        v = jnp.where(valid, zc * zv, 0.0)
        v1 = pltpu.roll(v, 1, 0)
        v2 = pltpu.roll(v, 2, 0)
        cb = (w_ref[0:1, :] * v2 + w_ref[1:2, :] * v1 + w_ref[2:3, :] * v + b_ref[...])[H:, :]
        ya = zb_ref[0] * cb
        o_ref[...] = (ya * _rstd(ya) * g_ref[...]).astype(BF16)

    zb_m, _, _ = _conv_specs(L, tm, C, 0)
    zc_m, zc_p, _ = _conv_specs(L, tm, C, 1)
    zv_m, zv_p, _ = _conv_specs(L, tm, C, 2)
    row = lambda r: pl.BlockSpec((r, C), lambda m: (0, 0))
    return pl.pallas_call(
        body, name="conv_fwd", grid=(L // tm,),
        in_specs=[zb_m, zc_m, zc_p, zv_m, zv_p, row(3), row(1), row(1)],
        out_specs=pl.BlockSpec((tm, C), lambda m: (m, 0)),
        out_shape=S((L, C), BF16), compiler_params=_params("parallel"),
    )(z, z, z, z, z, conv_w, conv_b.reshape(1, C), gnorm.reshape(1, C))


def _cmul(ar, ai, br, bi):
    return ar * br - ai * bi, ar * bi + ai * br


def _scan_fwd(hr_ref, hi_ref, lr, li, n_steps):
    W = hr_ref.shape[1]
    zero = jnp.zeros((N_SEG, W), F32)

    def local(t, c):
        r = pl.multiple_of(t * N_SEG, N_SEG)
        pr, pi = _cmul(lr, li, c[0], c[1])
        nr = pr + hr_ref[pl.ds(r, N_SEG), :]
        ni = pi + hi_ref[pl.ds(r, N_SEG), :]
        hr_ref[pl.ds(r, N_SEG), :] = nr
        hi_ref[pl.ds(r, N_SEG), :] = ni
        return nr, ni

    fr, fi = lax.fori_loop(0, n_steps, local, (zero, zero))
    qr, qi = _cpow(lr, li, n_steps)
    row = lax.broadcasted_iota(jnp.int32, (N_SEG, W), 0)
    cr, ci = zero, zero
    for seg in range(1, N_SEG):
        tr, ti = _cmul(qr, qi, cr, ci)
        sr = pltpu.roll(fr + tr, 1, 0)
        si = pltpu.roll(fi + ti, 1, 0)
        cr = jnp.where(row == seg, sr, cr)
        ci = jnp.where(row == seg, si, ci)

    def fix(t, c):
        r = pl.multiple_of(t * N_SEG, N_SEG)
        pr, pi = _cmul(lr, li, c[0], c[1])
        ar, ai = _cmul(pr, pi, cr, ci)
        hr_ref[pl.ds(r, N_SEG), :] += ar
        hi_ref[pl.ds(r, N_SEG), :] += ai
        return pr, pi

    lax.fori_loop(0, n_steps, fix, (jnp.ones((N_SEG, W), F32), zero))


def _cpow(lr, li, n):
    rr, ri = None, None
    br, bi = lr, li
    while n:
        if n & 1:
            rr, ri = (br, bi) if rr is None else _cmul(rr, ri, br, bi)
        n >>= 1
        if n:
            br, bi = _cmul(br, bi, br, bi)
    return rr, ri


def _ssm_specs(L):
    col = lambda w: pl.BlockSpec((L, w), lambda j: (0, j))
    return dict(
        u=col(128), lam=pl.BlockSpec((2, 512), lambda j: (0, j)),
        bmat=pl.BlockSpec((1, 128, 512), lambda j: (j, 0, 0)), cmat=pl.BlockSpec((1, 512, 128), lambda j: (j, 0, 0)),
        d=pl.BlockSpec((1, 128), lambda j: (0, j)))


def ssm_fwd(us, lam, bre, bim, cre, cim, dvec):
    L = us.shape[0]
    n_steps = L // N_SEG
    sp = _ssm_specs(L)

    def body(u_ref, lam_ref, bre_ref, bim_ref, cre_ref, cim_ref, d_ref, y_ref, hr, hi):
        u = u_ref[...]
        ub = u.astype(BF16)
        hr[...] = _dot(ub, bre_ref[0], 1, 0)
        hi[...] = _dot(ub, bim_ref[0], 1, 0)
        lr = jnp.broadcast_to(lam_ref[0:1, :], (N_SEG, 512))
        li = jnp.broadcast_to(lam_ref[1:2, :], (N_SEG, 512))
        _scan_fwd(hr, hi, lr, li, n_steps)
        y_ref[...] = (_dot(hr[...].astype(BF16), cre_ref[0], 1, 0) - _dot(hi[...].astype(BF16), cim_ref[0], 1, 0)
                      + d_ref[...] * u)

    return pl.pallas_call(
        body, name="ssm_fwd", grid=(4,),
        in_specs=[sp["u"], sp["lam"], sp["bmat"], sp["bmat"], sp["cmat"], sp["cmat"], sp["d"]],
        out_specs=sp["u"], out_shape=S((L, 512), F32),
        scratch_shapes=[pltpu.VMEM((L, 512), F32), pltpu.VMEM((L, 512), F32)],
        compiler_params=_params("parallel"),
    )(us, lam, bre, bim, cre, cim, dvec)


_GELU_C = math.sqrt(2.0 / math.pi)


def _gelu(y):
    t = jnp.tanh(_GELU_C * (y + 0.044715 * y * y * y))
    return 0.5 * y * (1.0 + t), t


def glu_fwd(y, w, b, gnorm):
    L, C = y.shape
    tm = _tile(L, 512)

    def body(y_ref, w_ref, b_ref, g_ref, o_ref):
        zg, _ = _gelu(y_ref[...])
        out = zg * _sigmoid(_dot(zg.astype(BF16), w_ref[...], 1, 0) + b_ref[...])
        o_ref[...] = (out * _rstd(out) * g_ref[...]).astype(BF16)

    row = pl.BlockSpec((1, C), lambda m: (0, 0))
    return pl.pallas_call(
        body, name="glu_fwd", grid=(L // tm,),
        in_specs=[pl.BlockSpec((tm, C), lambda m: (m, 0)), pl.BlockSpec((C, C), lambda m: (0, 0)), row, row],
        out_specs=pl.BlockSpec((tm, C), lambda m: (m, 0)),
        out_shape=S((L, C), BF16), compiler_params=_params("parallel"),
    )(y, w, b.reshape(1, C), gnorm.reshape(1, C))


def _ple_specs(L, D, P, tm, nb):
    return [pl.BlockSpec((tm, D), lambda n, m: (m, 0)), pl.BlockSpec((tm, P), lambda n, m: (m, 0)),
            pl.BlockSpec((D, nb), lambda n, m: (0, n)), pl.BlockSpec((1, P, nb), lambda n, m: (n, 0, 0)),
            pl.BlockSpec((tm, nb), lambda n, m: (m, n))]


def ple_fwd(un, pb, wpg, wpp, h, token):
    L, D = un.shape
    ns, P, nb = wpp.shape
    tm = _tile(L, 512)

    def body(un_ref, p_ref, wg_ref, wp_ref, h_ref, tok_ref, o_ref):
        gate = _sigmoid(_dot(un_ref[...], wg_ref[...], 1, 0))
        o_ref[...] = h_ref[...] + tok_ref[0:1, 0:1] + _dot(p_ref[...], wp_ref[0], 1, 0) * gate

    return pl.pallas_call(
        body, name="ple_fwd", grid=(ns, L // tm),
        in_specs=_ple_specs(L, D, P, tm, nb) + [pl.BlockSpec((8, 128), lambda n, m: (0, 0))],
        out_specs=pl.BlockSpec((tm, nb), lambda n, m: (m, n)),
        out_shape=S((L, D), F32), compiler_params=_params("parallel", "parallel"),
    )(un, pb, wpg, wpp, h, token)


def loss_head(h, g, target):
    L, D = h.shape
    tm = _tile(L, 256)

    def body(h_ref, g_ref, t_ref, loss_ref, dh_ref, dg_ref):
        m = pl.program_id(0)
        x = h_ref[...]
        gg = g_ref[...]
        e = x * _rstd(x) * gg - t_ref[...]
        dx, dg = _rms_bwd(x, gg, e * (1.0 / D))
        dh_ref[...] = dx
        part = jnp.full((8, 128), 0.5 / D, F32) * jnp.sum(e * e)

        @pl.when(m == 0)
        def _():
            loss_ref[...] = part
            dg_ref[...] = dg

        @pl.when(m > 0)
        def _():
            loss_ref[...] += part
            dg_ref[...] += dg

    return pl.pallas_call(
        body, name="loss_head", grid=(L // tm,),
        in_specs=[pl.BlockSpec((tm, D), lambda m: (m, 0)), pl.BlockSpec((1, D), lambda m: (0, 0)),
                  pl.BlockSpec((tm, D), lambda m: (m, 0))],
        out_specs=[pl.BlockSpec((8, 128), lambda m: (0, 0)), pl.BlockSpec((tm, D), lambda m: (m, 0)),
                   pl.BlockSpec((1, D), lambda m: (0, 0))],
        out_shape=[S((8, 128), F32), S((L, D), F32), S((1, D), F32)],
        compiler_params=_params("arbitrary"),
    )(h, g.reshape(1, D), target)


def ple_bwd(un, pb, wpg, wpp, dh, token):
    L, D = un.shape
    ns, P, nb = wpp.shape
    tm = _tile(L, 512)

    def body(un_ref, p_ref, wg_ref, wp_ref, dh_ref, tok_ref, dpre_ref, dpp_ref):
        gate = _sigmoid(_dot(un_ref[...], wg_ref[...], 1, 0))
        pp = _dot(p_ref[...], wp_ref[0], 1, 0)
        d = dh_ref[...] + tok_ref[0:1, 0:1]
        dpp_ref[0] = (d * gate).astype(BF16)
        dpre_ref[...] = (d * pp * gate * (1.0 - gate)).astype(BF16)

    return pl.pallas_call(
        body, name="ple_bwd", grid=(ns, L // tm),
        in_specs=_ple_specs(L, D, P, tm, nb) + [pl.BlockSpec((8, 128), lambda n, m: (0, 0))],
        out_specs=[pl.BlockSpec((tm, nb), lambda n, m: (m, n)), pl.BlockSpec((1, tm, nb), lambda n, m: (n, m, 0))],
        out_shape=[S((L, D), BF16), S((ns, L, nb), BF16)], compiler_params=_params("parallel", "parallel"),
    )(un, pb, wpg, wpp, dh, token)


def wgrad(a, b):
    a3 = a if a.ndim == 3 else a[None]
    b3 = b if b.ndim == 3 else b[None]
    ns = max(a3.shape[0], b3.shape[0])
    _, L, Ka = a3.shape
    N = b3.shape[2]
    a_map = (lambda s: (s, 0, 0)) if a3.shape[0] > 1 else (lambda s: (0, 0, 0))
    b_map = (lambda s: (s, 0, 0)) if b3.shape[0] > 1 else (lambda s: (0, 0, 0))

    def body(a_ref, b_ref, o_ref):
        o_ref[0] = _dot(a_ref[0], b_ref[0], 0, 0).astype(BF16)

    return pl.pallas_call(
        body, name="wgrad", grid=(ns,),
        in_specs=[pl.BlockSpec((1, L, Ka), a_map), pl.BlockSpec((1, L, N), b_map)],
        out_specs=pl.BlockSpec((1, Ka, N), lambda s: (s, 0, 0)),
        out_shape=S((ns, Ka, N), BF16), compiler_params=_params("parallel"),
    )(a3, b3)


def wgrad_sharded(pairs):
    n = len(pairs)
    ns, L, Ka = pairs[0][0].shape
    N = pairs[0][1].shape[1]

    def body(*refs):
        for i in range(n):
            refs[2 * n + i][0] = _dot(refs[2 * i][0], refs[2 * i + 1][...], 0, 0).astype(BF16)

    in_specs, args = [], []
    for a3, b in pairs:
        in_specs += [pl.BlockSpec((1, L, Ka), lambda s: (s, 0, 0)), pl.BlockSpec((L, N), lambda s: (0, 0), pipeline_mode=pl.Buffered(1))]
        args += [a3, b]
    out = pl.BlockSpec((1, Ka, N), lambda s: (s, 0, 0))
    return pl.pallas_call(
        body, name="wgrad_sharded", grid=(ns,), in_specs=in_specs, out_specs=[out] * n,
        out_shape=[S((ns, Ka, N), BF16)] * n, compiler_params=_params("parallel"),
    )(*args)


def dx_rms(pairs, h, g, dh_in, cast_scale):
    L, D = h.shape
    nk = pairs[0][0].shape[0]
    n_pairs = len(pairs)
    tm = _tile(L, 512)
    n_m = L // tm
    w_dims = [0 if transposed else 1 for _, _, transposed in pairs]

    def body(*refs):
        ins, (h_ref, g_ref, dhi_ref, dho_ref, dhb_ref, dg_ref) = refs[:2 * n_pairs], refs[2 * n_pairs:]
        m = pl.program_id(0)
        acc = None
        for i in range(n_pairs):
            for k in range(nk):
                part = _dot(ins[2 * i][k], ins[2 * i + 1][k], 1, w_dims[i])
                acc = part if acc is None else acc + part
        dx, dg = _rms_bwd(h_ref[...], g_ref[...], acc)
        dh_out = dhi_ref[...] + dx
        dho_ref[...] = dh_out
        dhb_ref[...] = (cast_scale * dh_out).astype(BF16)

        @pl.when(m == 0)
        def _():
            dg_ref[...] = dg

        @pl.when(m > 0)
        def _():
            dg_ref[...] += dg

    in_specs, args = [], []
    for a3, w3, _ in pairs:
        Kc = a3.shape[2]
        in_specs += [pl.BlockSpec((nk, tm, Kc), lambda m: (0, m, 0)),
                     pl.BlockSpec(w3.shape, lambda m: (0, 0, 0), pipeline_mode=pl.Buffered(1))]
        args += [a3, w3]
    tile = pl.BlockSpec((tm, D), lambda m: (m, 0))
    row = pl.BlockSpec((1, D), lambda m: (0, 0))
    return pl.pallas_call(
        body, name="dx_rms", grid=(n_m,),
        in_specs=in_specs + [tile, row, tile], out_specs=[tile, tile, row],
        out_shape=[S((L, D), F32), S((L, D), BF16), S((1, D), F32)], compiler_params=_params("arbitrary"),
    )(*args, h, g.reshape(1, D), dh_in)


def dact_plain(dhb, w3):
    L, D = dhb.shape
    ns, N, _ = w3.shape
    tm = _tile(L, 512)

    def body(d_ref, w_ref, o_ref):
        o_ref[0] = _dot(d_ref[...], w_ref[0], 1, 1)

    return pl.pallas_call(
        body, name="dact_plain", grid=(ns, L // tm),
        in_specs=[pl.BlockSpec((tm, D), lambda s, m: (m, 0)), pl.BlockSpec((1, N, D), lambda s, m: (s, 0, 0))],
        out_specs=pl.BlockSpec((1, tm, N), lambda s, m: (s, m, 0)),
        out_shape=S((ns, L, N), F32), compiler_params=_params("parallel", "parallel"),
    )(dhb, w3)


def dact_swiglu(dhb, wd, a3, b3):
    L, D = dhb.shape
    ns, F, _ = wd.shape
    tm = _tile(L, 512)

    def body(d_ref, w_ref, a_ref, b_ref, da_ref, db_ref):
        ds = _dot(d_ref[...], w_ref[0], 1, 1)
        a = a_ref[0].astype(F32)
        b = b_ref[0].astype(F32)
        sg = _sigmoid(a)
        da_ref[0] = (ds * b * (sg * (1.0 + a * (1.0 - sg)))).astype(BF16)
        db_ref[0] = (ds * (a * sg)).astype(BF16)

    t_spec = pl.BlockSpec((1, tm, F), lambda s, m: (s, m, 0))
    return pl.pallas_call(
        body, name="dact_swiglu", grid=(ns, L // tm),
        in_specs=[pl.BlockSpec((tm, D), lambda s, m: (m, 0)), pl.BlockSpec((1, F, D), lambda s, m: (s, 0, 0)), t_spec, t_spec],
        out_specs=[t_spec, t_spec], out_shape=[S((ns, L, F), BF16)] * 2,
        compiler_params=_params("parallel", "parallel"),
    )(dhb, wd, a3, b3)


def conv_bwd(z, conv_w, conv_b, gnorm, dyn):
    _, L, C = z.shape
    tm = _tile(L, 256)
    H = CONV_HALO
    T = tm + 2 * H

    def body(zb_ref, zbp_ref, zbn_ref, zc_ref, zcp_ref, zcn_ref, zv_ref, zvp_ref, zvn_ref, d_ref, dp_ref, dn_ref,
             w_ref, b_ref, g_ref, dz_ref, dw_ref, db_ref, dg_ref):
        m = pl.program_id(0)
        cat = lambda p, c, n: jnp.concatenate([p[0], c[0], n[0]], axis=0)
        zb, zc, zv, d = cat(zbp_ref, zb_ref, zbn_ref), cat(zcp_ref, zc_ref, zcn_ref), cat(zvp_ref, zv_ref, zvn_ref), cat(dp_ref, d_ref, dn_ref)
        grow = m * tm - H + lax.broadcasted_iota(jnp.int32, (T, C), 0)
        valid, v, v1, v2, cb, ya = _conv_core(zb, zc, zv, w_ref, b_ref[...], grow, L)
        dya, _ = _rms_bwd(ya, g_ref[...], d)
        dc = jnp.where(valid, dya * zb, 0.0)
        dv = w_ref[2:3, :] * dc + w_ref[1:2, :] * pltpu.roll(dc, T - 1, 0) + w_ref[0:1, :] * pltpu.roll(dc, T - 2, 0)
        dz_ref[0] = (dya * cb)[H:H + tm, :].astype(BF16)
        dz_ref[1] = (dv * zv)[H:H + tm, :].astype(BF16)
        dz_ref[2] = (dv * zc)[H:H + tm, :].astype(BF16)
        rs = lambda x: jnp.sum(x[H:H + tm, :], axis=0, keepdims=True)
        yh = ya * _rstd(ya)
        dw = jnp.concatenate([rs(dc * v2), rs(dc * v1), rs(dc * v)], axis=0)
        dbias, dg = rs(dc), rs(d * yh)

        @pl.when(m == 0)
        def _():
            dw_ref[...] = dw
            db_ref[...] = dbias
            dg_ref[...] = dg

        @pl.when(m > 0)
        def _():
            dw_ref[...] += dw
            db_ref[...] += dbias
            dg_ref[...] += dg

    row = lambda r: pl.BlockSpec((r, C), lambda m: (0, 0))
    specs = [*_conv_specs(L, tm, C, 0), *_conv_specs(L, tm, C, 1), *_conv_specs(L, tm, C, 2), *_conv_specs(L, tm, C, 0)]
    return pl.pallas_call(
        body, name="conv_bwd", grid=(L // tm,),
        in_specs=specs + [row(3), row(1), row(1)],
        out_specs=[pl.BlockSpec((3, tm, C), lambda m: (0, m, 0)), row(3), row(1), row(1)],
        out_shape=[S((3, L, C), BF16), S((3, C), F32), S((1, C), F32), S((1, C), F32)],
        compiler_params=_params("arbitrary"),
    )(z, z, z, z, z, z, z, z, z, dyn, dyn, dyn, conv_w, conv_b.reshape(1, C), gnorm.reshape(1, C))


def glu_bwd(y, w, b, gnorm, dn):
    L, C = y.shape
    tm = _tile(L, 256)

    def body(y_ref, w_ref, b_ref, g_ref, d_ref, dy_ref, dpre_ref, zg_ref, db_ref, dg_ref):
        m = pl.program_id(0)
        yv = y_ref[...]
        zg, t = _gelu(yv)
        zgb = zg.astype(BF16)
        sg = _sigmoid(_dot(zgb, w_ref[...], 1, 0) + b_ref[...])
        out = zg * sg
        dout, dg = _rms_bwd(out, g_ref[...], d_ref[...])
        dpre = dout * zg * sg * (1.0 - sg)
        dpre_b = dpre.astype(BF16)
        dzg = dout * sg + _dot(dpre_b, w_ref[...], 1, 1)
        dt = (1.0 - t * t) * _GELU_C * (1.0 + 3.0 * 0.044715 * yv * yv)
        dy_ref[...] = dzg * (0.5 * (1.0 + t) + 0.5 * yv * dt)
        dpre_ref[...] = dpre_b
        zg_ref[...] = zgb
        dbias = jnp.sum(dpre, axis=0, keepdims=True)

        @pl.when(m == 0)
        def _():
            db_ref[...] = dbias
            dg_ref[...] = dg

        @pl.when(m > 0)
        def _():
            db_ref[...] += dbias
            dg_ref[...] += dg

    tile = pl.BlockSpec((tm, C), lambda m: (m, 0))
    row = pl.BlockSpec((1, C), lambda m: (0, 0))
    return pl.pallas_call(
        body, name="glu_bwd", grid=(L // tm,),
        in_specs=[tile, pl.BlockSpec((C, C), lambda m: (0, 0)), row, row, tile],
        out_specs=[tile, tile, tile, row, row],
        out_shape=[S((L, C), F32), S((L, C), BF16), S((L, C), BF16), S((1, C), F32), S((1, C), F32)],
        compiler_params=_params("arbitrary"),
    )(y, w, b.reshape(1, C), gnorm.reshape(1, C), dn)


def _scan_bwd(gr_ref, gi_ref, hr_ref, hi_ref, lr, li, n_steps):
    W = gr_ref.shape[1]
    zero = jnp.zeros((N_SEG, W), F32)
    lic = -li

    def local(i, c):
        r = pl.multiple_of((n_steps - 1 - i) * N_SEG, N_SEG)
        pr, pi = _cmul(lr, lic, c[0], c[1])
        nr = pr + gr_ref[pl.ds(r, N_SEG), :]
        ni = pi + gi_ref[pl.ds(r, N_SEG), :]
        gr_ref[pl.ds(r, N_SEG), :] = nr
        gi_ref[pl.ds(r, N_SEG), :] = ni
        return nr, ni

    fr, fi = lax.fori_loop(0, n_steps, local, (zero, zero))
    qr, qi = _cpow(lr, lic, n_steps)
    row = lax.broadcasted_iota(jnp.int32, (N_SEG, W), 0)
    cr, ci = zero, zero
    for seg in range(N_SEG - 2, -1, -1):
        tr, ti = _cmul(qr, qi, cr, ci)
        sr = pltpu.roll(fr + tr, N_SEG - 1, 0)
        si = pltpu.roll(fi + ti, N_SEG - 1, 0)
        cr = jnp.where(row == seg, sr, cr)
        ci = jnp.where(row == seg, si, ci)

    def fix(i, c):
        pwr, pwi, ar, ai = c
        t = n_steps - 1 - i
        r = pl.multiple_of(t * N_SEG, N_SEG)
        pwr, pwi = _cmul(lr, lic, pwr, pwi)
        xr, xi = _cmul(pwr, pwi, cr, ci)
        g_r = gr_ref[pl.ds(r, N_SEG), :] + xr
        g_i = gi_ref[pl.ds(r, N_SEG), :] + xi
        gr_ref[pl.ds(r, N_SEG), :] = g_r
        gi_ref[pl.ds(r, N_SEG), :] = g_i
        rp = pl.multiple_of(jnp.maximum(t - 1, 0) * N_SEG, N_SEG)
        hpr = hr_ref[pl.ds(rp, N_SEG), :]
        hpi = hi_ref[pl.ds(rp, N_SEG), :]
        live = t > 0
        ar = ar + jnp.where(live, hpr * g_r + hpi * g_i, 0.0)
        ai = ai + jnp.where(live, hpr * g_i - hpi * g_r, 0.0)
        return pwr, pwi, ar, ai

    _, _, ar, ai = lax.fori_loop(0, n_steps, fix, (jnp.ones((N_SEG, W), F32), zero, zero, zero))
    last = pl.ds((n_steps - 1) * N_SEG, N_SEG)
    hpr = jnp.where(row == 0, 0.0, pltpu.roll(hr_ref[last, :], 1, 0))
    hpi = jnp.where(row == 0, 0.0, pltpu.roll(hi_ref[last, :], 1, 0))
    g_r, g_i = gr_ref[pl.ds(0, N_SEG), :], gi_ref[pl.ds(0, N_SEG), :]
    ar = ar + hpr * g_r + hpi * g_i
    ai = ai + hpr * g_i - hpi * g_r
    return jnp.sum(ar, axis=0, keepdims=True), jnp.sum(ai, axis=0, keepdims=True)


def ssm_bwd(us, dy, lam, bre, bim, cre, cim, dvec):
    L = us.shape[0]
    n_steps = L // N_SEG
    sp = _ssm_specs(L)

    def body(u_ref, dy_ref, lam_ref, bre_ref, bim_ref, cre_ref, cim_ref, d_ref,
             du_ref, dlam_ref, dbre_ref, dbim_ref, dcre_ref, dcim_ref, dd_ref, hr, hi, gr, gi):
        u = u_ref[...]
        ub = u.astype(BF16)
        dyv = dy_ref[...]
        dyb = dyv.astype(BF16)
        hr[...] = _dot(ub, bre_ref[0], 1, 0)
        hi[...] = _dot(ub, bim_ref[0], 1, 0)
        lr = jnp.broadcast_to(lam_ref[0:1, :], (N_SEG, 512))
        li = jnp.broadcast_to(lam_ref[1:2, :], (N_SEG, 512))
        _scan_fwd(hr, hi, lr, li, n_steps)
        dcre_ref[0] = _dot(hr[...].astype(BF16), dyb, 0, 0)
        dcim_ref[0] = -_dot(hi[...].astype(BF16), dyb, 0, 0)
        gr[...] = _dot(dyb, cre_ref[0], 1, 1)
        gi[...] = -_dot(dyb, cim_ref[0], 1, 1)
        dlr, dli = _scan_bwd(gr, gi, hr, hi, lr, li, n_steps)
        dlam_ref[...] = jnp.concatenate([dlr, dli], axis=0)
        grb, gib = gr[...].astype(BF16), gi[...].astype(BF16)
        du_ref[...] = _dot(grb, bre_ref[0], 1, 1) + _dot(gib, bim_ref[0], 1, 1) + d_ref[...] * dyv
        dbre_ref[0] = _dot(ub, grb, 0, 0)
        dbim_ref[0] = _dot(ub, gib, 0, 0)
        dd_ref[...] = jnp.sum(dyv * u, axis=0, keepdims=True)

    big = pltpu.VMEM((L, 512), F32)
    return pl.pallas_call(
        body, name="ssm_bwd", grid=(4,),
        in_specs=[sp["u"], sp["u"], sp["lam"], sp["bmat"], sp["bmat"], sp["cmat"], sp["cmat"], sp["d"]],
        out_specs=[sp["u"], sp["lam"], sp["bmat"], sp["bmat"], sp["cmat"], sp["cmat"], sp["d"]],
        out_shape=[S((L, 512), F32), S((2, 2048), F32), S((4, 128, 512), F32), S((4, 128, 512), F32),
                   S((4, 512, 128), F32), S((4, 512, 128), F32), S((1, 512), F32)],
        scratch_shapes=[big, big, big, big], compiler_params=_params("parallel"),
    )(us, dy, lam, bre, bim, cre, cim, dvec)


def _discretize(ar, ai, log_dt, br, bi):
    dt = jnp.exp(log_dt)
    mag = jnp.exp(ar * dt)
    ph = ai * dt
    lr, li = mag * jnp.cos(ph), mag * jnp.sin(ph)
    nr, ni = lr - 1.0, li
    den = ar * ar + ai * ai
    fr = (nr * ar + ni * ai) / den
    fi = (ni * ar - nr * ai) / den
    return lr, li, fr[..., None] * br - fi[..., None] * bi, fr[..., None] * bi + fi[..., None] * br


def ssm_prep(ar, ai, log_dt, br, bi):
    G, P, H = br.shape

    def body(ar_ref, ai_ref, dt_ref, br_ref, bi_ref, lr_ref, li_ref, bbr_ref, bbi_ref):
        lr_ref[...], li_ref[...], bbr_ref[...], bbi_ref[...] = _discretize(
            ar_ref[...], ai_ref[...], dt_ref[...], br_ref[...], bi_ref[...])

    return pl.pallas_call(
        body, name="ssm_prep",
        out_shape=[S((G, P), F32), S((G, P), F32), S((G, P, H), F32), S((G, P, H), F32)],
    )(ar, ai, log_dt.reshape(G, 1), br, bi)


def ssm_prep_bwd(ar, ai, log_dt, br, bi, dlr, dli, dbbr, dbbi):
    G, P, H = br.shape

    def body(ar_ref, ai_ref, dt_ref, br_ref, bi_ref, dlr_ref, dli_ref, dbbr_ref, dbbi_ref,
             dar_ref, dai_ref, ddt_ref, dbr_ref, dbi_ref):
        _, vjp = jax.vjp(_discretize, ar_ref[...], ai_ref[...], dt_ref[...], br_ref[...], bi_ref[...])
        dar_ref[...], dai_ref[...], ddt_ref[...], dbr_ref[...], dbi_ref[...] = vjp(
            (dlr_ref[...], dli_ref[...], dbbr_ref[...], dbbi_ref[...]))

    return pl.pallas_call(
        body, name="ssm_prep_bwd",
        out_shape=[S((G, P), F32), S((G, P), F32), S((G, 1), F32), S((G, P, H), F32), S((G, P, H), F32)],
    )(ar, ai, log_dt.reshape(G, 1), br, bi, dlr, dli, dbbr, dbbi)


def _block_diag(x):
    j, n, R, C = x.shape
    eye = jnp.eye(n, dtype=x.dtype)
    return (x[:, :, :, None, :] * eye[None, :, None, :, None]).reshape(j, n * R, n * C)


def _block_diag_take(x, R, C):
    j = x.shape[0]
    n = x.shape[1] // R
    x5 = x.reshape(j, n, R, n, C)
    return jnp.stack([x5[:, i, :, i, :] for i in range(n)], axis=1)


def _to_segments(x):
    L, C = x.shape
    return x.reshape(N_SEG, L // N_SEG, C).transpose(1, 0, 2).reshape(L, C)


def _from_segments(x):
    L, C = x.shape
    return x.reshape(L // N_SEG, N_SEG, C).transpose(1, 0, 2).reshape(L, C)


BIG = ("ffn1_w_gate", "ffn1_w_up", "ffn1_w_down", "w_in", "glu_w", "w_out",
       "ffn2_w_gate", "ffn2_w_up", "ffn2_w_down", "ple_w_gate", "ple_w_proj")
SMALL = ("ffn1_norm", "mix_norm", "conv_w", "conv_b", "ssm_A_re", "ssm_A_im", "ssm_B_re", "ssm_B_im", "ssm_C_re", "ssm_C_im",
         "ssm_D", "ssm_log_dt", "glu_b", "conv_out_norm", "ssm_out_norm", "ffn2_norm", "ple_norm")


def _ssm_mats(w):
    G, P, H = w["ssm_B_re"].shape
    lr, li, bbr, bbi = ssm_prep(w["ssm_A_re"], w["ssm_A_im"], w["ssm_log_dt"], w["ssm_B_re"], w["ssm_B_im"])
    lam = jnp.stack([lr.reshape(G * P), li.reshape(G * P)])
    bmat = lambda bb: _block_diag(bb.reshape(4, G // 4, P, H).transpose(0, 1, 3, 2)).astype(BF16)
    cmat = lambda c: _block_diag(c.reshape(4, G // 4, H, P).transpose(0, 1, 3, 2)).astype(BF16)
    return lam, bmat(bbr), bmat(bbi), cmat(w["ssm_C_re"]), cmat(w["ssm_C_im"]), w["ssm_D"].reshape(1, G * H)


def layer_fwd(h0, pb, w, before_last):
    L, D = h0.shape
    u1 = rmsnorm_fwd(h0, w["ffn1_norm"])
    a1, b1, s1 = ffn_up(u1, w["ffn1_w_gate"], w["ffn1_w_up"])
    h1, u2 = mm_shard_k(s1, w["ffn1_w_down"], h0, 0.5, w["mix_norm"])
    z = mm_shard_n(u2, w["w_in"], F32)
    ya_n = conv_fwd(z, w["conv_w"], w["conv_b"], w["conv_out_norm"])
    us = _to_segments(z[3])
    mats = w["mats"]
    y = ssm_fwd(us, *mats)
    ys_n = glu_fwd(y, w["glu_w"], w["glu_b"], w["ssm_out_norm"])
    ycat = jnp.stack([ya_n, _from_segments(ys_n)])
    h2, u3 = mm_shard_k(ycat, w["w_out"], h1, 1.0, w["ffn2_norm"])
    a2, b2, s2 = ffn_up(u3, w["ffn2_w_gate"], w["ffn2_w_up"])
    h3, un = mm_shard_k(s2, w["ffn2_w_down"], h2, 0.5, w["ple_norm"])
    h4 = ple_fwd(un, pb, w["ple_w_gate"], w["ple_w_proj"], h3, before_last(h3))
    saved = dict(h0=h0, u1=u1, a1=a1, b1=b1, s1=s1, h1=h1, u2=u2, z=z, us=us, mats=mats, y=y, ycat=ycat,
                 h2=h2, u3=u3, a2=a2, b2=b2, s2=s2, h3=h3, un=un)
    return h4, saved


def _ffn_bwd(dh, dhb, h_in, u, a, b, s, wg, wu, wd, gnorm, cast_scale):
    da, db = dact_swiglu(dhb, wd, a, b)
    g_wd, g_wg, g_wu = wgrad_sharded([(s, dhb), (da, u), (db, u)])
    dh_in, dhb_in, g_norm = dx_rms([(da, wg, True), (db, wu, True)], h_in, gnorm, dh, cast_scale)
    return dh_in, dhb_in, g_wg, g_wu, g_wd, g_norm


def layer_bwd(dh, pb, w, sv, token):
    L, D = dh.shape
    G, P, H = w["ssm_B_re"].shape
    dpre, dpp3 = ple_bwd(sv["un"], pb, w["ple_w_gate"], w["ple_w_proj"], dh, token)
    g_wpg = wgrad(sv["un"], dpre).reshape(N_SHARD, D // N_SHARD, D)
    g_wpp = wgrad(pb, dpp3)
    dh3, dhb3, g_nple = dx_rms([(dpre[None], w["ple_w_gate"][None], False)], sv["h3"], w["ple_norm"], dh, 0.5)
    dh2, dhb, g_wg2, g_wu2, g_wd2, g_nffn2 = _ffn_bwd(dh3, dhb3, sv["h2"], sv["u3"], sv["a2"], sv["b2"], sv["s2"],
                                                      w["ffn2_w_gate"], w["ffn2_w_up"], w["ffn2_w_down"], w["ffn2_norm"], 1.0)
    dyn = dact_plain(dhb, w["w_out"])
    g_wout = wgrad(sv["ycat"], dhb).reshape(N_SHARD, -1, D)
    dz_abc, g_convw, g_convb, g_nconv = conv_bwd(sv["z"], w["conv_w"], w["conv_b"], w["conv_out_norm"], dyn)
    dy, dpre_g, zg, g_glub, g_nssm = glu_bwd(sv["y"], w["glu_w"], w["glu_b"], w["ssm_out_norm"], _to_segments(dyn[1]))
    C = zg.shape[1]
    g_gluw = wgrad(zg, dpre_g).reshape(N_SHARD, C // N_SHARD, C)
    dus, dlam, dbre, dbim, dcre, dcim, dd = ssm_bwd(sv["us"], dy, *sv["mats"])
    take_b = lambda m: _block_diag_take(m, H, P).transpose(0, 1, 3, 2).reshape(G, P, H)
    take_c = lambda m: _block_diag_take(m, P, H).transpose(0, 1, 3, 2).reshape(G, H, P)
    g_ar, g_ai, g_dt, g_br, g_bi = ssm_prep_bwd(
        w["ssm_A_re"], w["ssm_A_im"], w["ssm_log_dt"], w["ssm_B_re"], w["ssm_B_im"],
        dlam[0].reshape(G, P), dlam[1].reshape(G, P), take_b(dbre), take_b(dbim))
    dz3 = jnp.concatenate([dz_abc, _from_segments(dus).astype(BF16)[None]], axis=0)
    g_win = wgrad(sv["u2"], dz3)
    dh1, dhb1, g_nmix = dx_rms([(dz3, w["w_in"], False)], sv["h1"], w["mix_norm"], dh2, 0.5)
    dh0, _, g_wg1, g_wu1, g_wd1, g_nffn1 = _ffn_bwd(dh1, dhb1, sv["h0"], sv["u1"], sv["a1"], sv["b1"], sv["s1"],
                                                    w["ffn1_w_gate"], w["ffn1_w_up"], w["ffn1_w_down"], w["ffn1_norm"], 1.0)
    big = [g_wg1, g_wu1, g_wd1, g_win, g_gluw, g_wout, g_wg2, g_wu2, g_wd2, g_wpg, g_wpp]
    small = dict(ffn1_norm=g_nffn1, mix_norm=g_nmix, conv_w=g_convw, conv_b=g_convb, ssm_A_re=g_ar, ssm_A_im=g_ai,
                 ssm_B_re=g_br, ssm_B_im=g_bi, ssm_C_re=take_c(dcre), ssm_C_im=take_c(dcim), ssm_D=dd,
                 ssm_log_dt=g_dt, glu_b=g_glub, conv_out_norm=g_nconv, ssm_out_norm=g_nssm, ffn2_norm=g_nffn2,
                 ple_norm=g_nple)
    return dh0, big, small


def local_step(x, p, target, final_norm, weights_of, before_last, on_grads):
    depth = p.shape[0]
    h = x
    layers, saved, pbs = [], [], []
    for i in range(depth):
        w = weights_of(i, h)
        pb = p[i].astype(BF16)
        h, sv = layer_fwd(h, pb, w, functools.partial(before_last, i))
        layers.append(w)
        saved.append(sv)
        pbs.append(pb)
    loss_part, dh, g_final = loss_head(h, final_norm, target)
    smalls = [None] * depth
    token = jnp.zeros((8, 128), F32)
    for i in reversed(range(depth)):
        dh, big, smalls[i] = layer_bwd(dh, pbs[i], layers[i], saved[i], token)
        token = on_grads(i, big, dh)
    return loss_part, dh, smalls, g_final


ROW_TILE_MAX = 512


def _row_tile(rows):
    for t in range(ROW_TILE_MAX, 0, -16):
        if rows % t == 0:
            return t
    return rows


def elementwise(fn, ins, out_dtypes, name):
    rows, cols = ins[0].shape
    tr = _row_tile(rows)
    n_in = len(ins)

    def body(*refs):
        outs = fn(*[r[...] for r in refs[:n_in]])
        for o_ref, o in zip(refs[n_in:], outs):
            o_ref[...] = o.astype(o_ref.dtype)

    spec = pl.BlockSpec((tr, cols), lambda i: (i, 0))
    return pl.pallas_call(
        body, name=name, grid=(rows // tr,), in_specs=[spec] * n_in, out_specs=[spec] * len(out_dtypes),
        out_shape=[S((rows, cols), d) for d in out_dtypes], compiler_params=_params("parallel"),
    )(*ins)


def _adamw(w, g, m, v):
    m = ADAM_B1 * m + (1.0 - ADAM_B1) * g
    v = ADAM_B2 * v + (1.0 - ADAM_B2) * (g * g)
    m_hat = m / (1.0 - ADAM_B1 ** ADAM_STEP)
    v_hat = v / (1.0 - ADAM_B2 ** ADAM_STEP)
    delta = -ADAM_LR * (m_hat / (jnp.sqrt(v_hat) + ADAM_EPS) + ADAM_WD * w)
    return delta, m, v


ANY = pl.BlockSpec(memory_space=pl.ANY)


def _mesh_pos():
    return lax.axis_index("x"), lax.axis_index("y"), lax.axis_index("c")


def _other_chips(x, y):
    return [(1 - x, y), (x, 1 - y), (1 - x, 1 - y)]


def _remote(src, dst, send_sem, recv_sem, device):
    return pltpu.make_async_remote_copy(src_ref=src, dst_ref=dst, send_sem=send_sem, recv_sem=recv_sem,
                                        device_id=device, device_id_type=MESH)


def gather_weights(ws):
    n = len(ws)

    def body(*refs):
        outs = refs[n:2 * n]
        send_sems, recv_sems = refs[2 * n:]
        x, y, c = _mesh_pos()
        me_s = 2 * x + y
        sibling = (x, y, 1 - c)
        chips = _other_chips(x, y)
        n_half = outs[0].shape[0] // 2
        mine, other = pl.ds(c * n_half, n_half), pl.ds((1 - c) * n_half, n_half)
        sent = []
        for t in range(n):
            for j, (cx, cy) in enumerate(chips):
                blk = outs[t].at[mine, me_s]
                cp = _remote(blk, blk, send_sems.at[t, j], recv_sems.at[t, j], (cx, cy, c))
                cp.start()
                sent.append(cp)
        for j, (cx, cy) in enumerate(chips):
            for t in range(n):
                blk = outs[t].at[mine, 2 * cx + cy]
                _remote(blk, blk, send_sems.at[t, j], recv_sems.at[t, j], (cx, cy, c)).wait_recv()
                cp = _remote(blk, blk, send_sems.at[t, 3 + j], recv_sems.at[t, 3 + j], sibling)
                cp.start()
                sent.append(cp)
        for j, (cx, cy) in enumerate(chips):
            for t in range(n):
                blk = outs[t].at[other, 2 * cx + cy]
                _remote(blk, blk, send_sems.at[t, 3 + j], recv_sems.at[t, 3 + j], sibling).wait_recv()
        for cp in sent:
            cp.wait_send()

    return pl.pallas_call(
        body, name="gather_weights", in_specs=[ANY] * n, out_specs=[ANY] * n,
        out_shape=[S(w.shape, w.dtype) for w in ws], input_output_aliases={t: t for t in range(n)},
        scratch_shapes=[pltpu.SemaphoreType.DMA((n, 6)), pltpu.SemaphoreType.DMA((n, 6))],
    )(*ws)


HBM = pl.BlockSpec(memory_space=pltpu.HBM)
SEM = pl.BlockSpec(memory_space=pltpu.SEMAPHORE)
VMEM_WHOLE = pl.BlockSpec(memory_space=pltpu.VMEM)
SPLIT_COPY = pltpu.CompilerParams(has_side_effects=pltpu.SideEffectType.DATAFLOW_SIDE_EFFECTING)


def _hbm(x):
    return pltpu.with_memory_space_constraint(x, pltpu.HBM)


def _half_rows(ref, c):
    r2 = ref.shape[1] // 2
    return pl.ds(pl.multiple_of(c * r2, 8), r2)


def _gather_copies(bufs, send_sems, recv_sems, forward):
    x, y, c = _mesh_pos()
    copies = []
    for t in range(len(bufs)):
        rows = _half_rows(bufs[t], c)
        for j, (cx, cy) in enumerate(_other_chips(x, y)):
            blk = bufs[t].at[2 * cx + cy if forward else 2 * x + y, rows]
            peer = (x, y, 1 - c) if forward else (cx, cy, c)
            copies.append(_remote(blk, blk, send_sems.at[3 * t + j], recv_sems.at[3 * t + j], peer))
    return copies


def gather_start(bufs, layer, after, forward=False):
    n, k = len(bufs), len(after)

    def body(*refs):
        ins, send_sems, recv_sems, token = refs[:n], refs[n + k], refs[n + k + 1], refs[2 * n + k + 2]
        for cp in _gather_copies(ins, send_sems, recv_sems, forward):
            cp.start()
        token[...] = jnp.zeros_like(token)

    outs = pl.pallas_call(
        body, name=f"{'forward' if forward else 'gather'}_start_{layer}", in_specs=[HBM] * n + [ANY] * k,
        out_specs=[SEM, SEM] + [HBM] * n + [VMEM_WHOLE],
        out_shape=[pltpu.SemaphoreType.DMA((3 * n,)), pltpu.SemaphoreType.DMA((3 * n,))]
        + [pltpu.HBM(b.shape, b.dtype) for b in bufs] + [S((8, 128), F32)],
        input_output_aliases={t: t + 2 for t in range(n)}, compiler_params=SPLIT_COPY,
    )(*[_hbm(b) for b in bufs], *after)
    return outs[0], outs[1], list(outs[2:2 + n]), outs[2 + n]


def gather_wait(bufs, send_sems, recv_sems, after, layer, forward=False):
    n, n_after = len(bufs), len(after)

    def body(*refs):
        ins, send_ref, recv_ref = refs[:n], refs[n], refs[n + 1]
        for cp in _gather_copies(ins, send_ref, recv_ref, forward):
            cp.wait_send()
            cp.wait_recv()

    outs = pl.pallas_call(
        body, name=f"{'forward' if forward else 'gather'}_wait_{layer}", in_specs=[HBM] * n + [SEM, SEM] + [ANY] * n_after,
        out_specs=[HBM] * n,
        out_shape=[pltpu.HBM(b.shape, b.dtype) for b in bufs],
        input_output_aliases={t: t for t in range(n)}, compiler_params=SPLIT_COPY,
    )(*bufs, send_sems, recv_sems, *after)
    return list(outs)


def gather_forward(bufs):
    n = len(bufs)

    def body(*refs):
        outs = refs[n:2 * n]
        send_sems, recv_sems = refs[2 * n:]
        copies = _gather_copies(outs, send_sems, recv_sems, True)
        for cp in copies:
            cp.start()
        for cp in copies:
            cp.wait()

    return pl.pallas_call(
        body, name="gather_forward", in_specs=[ANY] * n, out_specs=[ANY] * n,
        out_shape=[S(b.shape, b.dtype) for b in bufs], input_output_aliases={t: t for t in range(n)},
        scratch_shapes=[pltpu.SemaphoreType.DMA((3 * n,)), pltpu.SemaphoreType.DMA((3 * n,))],
    )(*bufs)


def chips_start(sums, layer):
    n = len(sums)
    lands = [lax.empty((3,) + s.shape[1:], s.dtype) for s in sums]

    def body(*refs):
        a, land, send_sems, recv_sems, token = refs[:n], refs[n:2 * n], refs[2 * n], refs[2 * n + 1], refs[4 * n + 2]
        x, y, c = _mesh_pos()
        for t in range(n):
            for j, (cx, cy) in enumerate(_other_chips(x, y)):
                _remote(a[t].at[2 * cx + cy], land[t].at[j], send_sems.at[3 * t + j], recv_sems.at[3 * t + j], (cx, cy, c)).start()
        token[...] = jnp.zeros_like(token)

    outs = pl.pallas_call(
        body, name=f"chips_start_{layer}", in_specs=[HBM] * (2 * n), out_specs=[SEM, SEM] + [HBM] * (2 * n) + [VMEM_WHOLE],
        out_shape=[pltpu.SemaphoreType.DMA((3 * n,)), pltpu.SemaphoreType.DMA((3 * n,))]
        + [pltpu.HBM(b.shape, b.dtype) for b in sums + lands] + [S((8, 128), F32)],
        input_output_aliases={t: t + 2 for t in range(2 * n)}, compiler_params=SPLIT_COPY,
    )(*[_hbm(b) for b in sums + lands])
    return outs[0], outs[1], list(outs[2:2 + n]), list(outs[2 + n:2 + 2 * n]), outs[2 + 2 * n]


def chips_wait(sums, lands, send_sems, recv_sems, after, layer):
    n, n_after = len(sums), len(after)

    def body(*refs):
        a, land, send_ref, recv_ref = refs[:n], refs[n:2 * n], refs[2 * n], refs[2 * n + 1]
        x, y, c = _mesh_pos()
        for t in range(n):
            for j, (cx, cy) in enumerate(_other_chips(x, y)):
                cp = _remote(a[t].at[2 * cx + cy], land[t].at[j], send_ref.at[3 * t + j], recv_ref.at[3 * t + j], (cx, cy, c))
                cp.wait_send()
                cp.wait_recv()

    outs = pl.pallas_call(
        body, name=f"chips_wait_{layer}", in_specs=[HBM] * (2 * n) + [SEM, SEM] + [ANY] * n_after, out_specs=[HBM] * (2 * n),
        out_shape=[pltpu.HBM(b.shape, b.dtype) for b in sums + lands],
        input_output_aliases={t: t for t in range(2 * n)}, compiler_params=SPLIT_COPY,
    )(*sums, *lands, send_sems, recv_sems, *after)
    return list(outs[n:])


def cast_place_layer(ws, layer, pos, dtype, token):
    n = len(ws)
    _, r, c = ws[0].shape
    tr = _row_tile(r)

    def body(pos_ref, *refs):
        tok = refs[n][0:1, 0:1]
        for t in range(n):
            refs[n + 1 + t][0] = (refs[t][0] + tok).astype(dtype)

    return pl.pallas_call(
        body, name="cast_place_layer",
        grid_spec=pltpu.PrefetchScalarGridSpec(
            num_scalar_prefetch=1, grid=(r // tr,),
            in_specs=[pl.BlockSpec((1, tr, c), lambda i, pos: (layer, i, 0))] * n + [pl.BlockSpec((8, 128), lambda i, pos: (0, 0))],
            out_specs=[pl.BlockSpec((1, tr, c), lambda i, pos: (pos[1], i, 0))] * n),
        out_shape=[S((N_SHARD, r, c), dtype)] * n, compiler_params=_params("parallel"),
    )(pos, *ws, token)


def cast_place(w, pos, dtype):
    layers, r, c = w.shape
    tr = _row_tile(r)

    def body(pos_ref, w_ref, o_ref):
        o_ref[0, 0] = w_ref[0].astype(dtype)

    return pl.pallas_call(
        body, name="cast_place",
        grid_spec=pltpu.PrefetchScalarGridSpec(
            num_scalar_prefetch=1, grid=(layers, r // tr),
            in_specs=[pl.BlockSpec((1, tr, c), lambda l, i, pos: (l, i, 0))],
            out_specs=pl.BlockSpec((1, 1, tr, c), lambda l, i, pos: (l, pos[1], i, 0))),
        out_shape=S((layers, N_SHARD, r, c), dtype), compiler_params=_params("parallel", "parallel"),
    )(pos, w)


def _pair_copy(g_ref, got_ref, send_sem, recv_sem):
    x, y, c = _mesh_pos()
    r2 = g_ref.shape[1] // 2
    give = pl.ds(pl.multiple_of((1 - c) * r2, 8), r2)
    return _remote(g_ref.at[:, give], got_ref, send_sem, recv_sem, (x, y, 1 - c))


def reduce_pair(gs, after):
    n, k = len(gs), len(after)

    def body(*refs):
        ins, got = refs[:n], refs[n + k:2 * n + k]
        send_sems, recv_sems = refs[2 * n + k:]
        copies = [_pair_copy(ins[t], got[t], send_sems.at[t], recv_sems.at[t]) for t in range(n)]
        for cp in copies:
            cp.start()
        for cp in copies:
            cp.wait()

    return pl.pallas_call(
        body, name="reduce_pair", in_specs=[ANY] * (n + k), out_specs=[ANY] * n,
        out_shape=[S((g.shape[0], g.shape[1] // 2, g.shape[2]), g.dtype) for g in gs],
        scratch_shapes=[pltpu.SemaphoreType.DMA((n,)), pltpu.SemaphoreType.DMA((n,))],
    )(*gs, *after)


def pair_sum(g, got, pos):
    ns, r2, c = got.shape
    tr = _row_tile(r2)
    n_i = r2 // tr

    def body(pos_ref, g_ref, got_ref, sum_ref, own_ref):
        s = pl.program_id(1)
        v = g_ref[0].astype(F32) + got_ref[0].astype(F32)
        sum_ref[0] = v.astype(BF16)

        @pl.when(s == pos_ref[1])
        def _():
            own_ref[...] = v

    return pl.pallas_call(
        body, name="pair_sum",
        grid_spec=pltpu.PrefetchScalarGridSpec(
            num_scalar_prefetch=1, grid=(n_i, ns),
            in_specs=[pl.BlockSpec((1, tr, c), lambda i, s, pos: (s, pos[0] * n_i + i, 0)),
                      pl.BlockSpec((1, tr, c), lambda i, s, pos: (s, i, 0))],
            out_specs=[pl.BlockSpec((1, tr, c), lambda i, s, pos: (s, i, 0)), pl.BlockSpec((tr, c), lambda i, s, pos: (i, 0))]),
        out_shape=[S((ns, r2, c), BF16), S((r2, c), F32)], compiler_params=_params("parallel", "arbitrary"),
    )(pos, g, got)


def chip_sum(own, p2, pos):
    r2, c = own.shape
    tr = _row_tile(r2)
    n_i = r2 // tr

    def body(pos_ref, own_ref, a_ref, b_ref, c_ref, o_ref):
        o_ref[...] = own_ref[...] + a_ref[0].astype(F32) + b_ref[0].astype(F32) + c_ref[0].astype(F32)

    peer = lambda j: pl.BlockSpec((1, tr, c), lambda i, pos: (j, i, 0))
    return pl.pallas_call(
        body, name="chip_sum",
        grid_spec=pltpu.PrefetchScalarGridSpec(
            num_scalar_prefetch=1, grid=(n_i,),
            in_specs=[pl.BlockSpec((tr, c), lambda i, pos: (i, 0)), peer(0), peer(1), peer(2)],
            out_specs=pl.BlockSpec((tr, c), lambda i, pos: (pos[0] * n_i + i, 0))),
        out_shape=S((2 * r2, c), F32), compiler_params=_params("parallel"),
    )(pos, own, p2, p2, p2)


def exchange_halves(rs):
    n = len(rs)

    def body(*refs):
        outs = refs[n:2 * n]
        send_sems, recv_sems = refs[2 * n:]
        x, y, c = _mesh_pos()
        copies = []
        for t in range(n):
            r2 = outs[t].shape[0] // 2
            rows = outs[t].at[pl.ds(pl.multiple_of(c * r2, 8), r2)]
            cp = _remote(rows, rows, send_sems.at[t], recv_sems.at[t], (x, y, 1 - c))
            cp.start()
            copies.append(cp)
        for cp in copies:
            cp.wait()

    return pl.pallas_call(
        body, name="exchange_halves", in_specs=[ANY] * n, out_specs=[ANY] * n,
        out_shape=[S(r.shape, r.dtype) for r in rs], input_output_aliases={t: t for t in range(n)},
        scratch_shapes=[pltpu.SemaphoreType.DMA((n,)), pltpu.SemaphoreType.DMA((n,))],
    )(*rs)


def allreduce_small(vec):
    R = vec.shape[0]
    H = R // 2

    def body(x_ref, o_ref, pair_buf, chip_buf, send_sems, recv_sems):
        x, y, c = _mesh_pos()
        me_s = 2 * x + y
        sibling = (x, y, 1 - c)
        mine = pl.ds(pl.multiple_of(c * H, 8), H)
        give = pl.ds(pl.multiple_of((1 - c) * H, 8), H)
        cp = _remote(x_ref.at[give], pair_buf, send_sems.at[0], recv_sems.at[0], sibling)
        cp.start()
        cp.wait()
        chip_buf[me_s] = x_ref[mine, :] + pair_buf[...]
        copies = []
        for j, (cx, cy) in enumerate(_other_chips(x, y)):
            cp = _remote(chip_buf.at[me_s], chip_buf.at[me_s], send_sems.at[1 + j], recv_sems.at[1 + j], (cx, cy, c))
            cp.start()
            copies.append(cp)
        for cp in copies:
            cp.wait()
        o_ref[mine, :] = (chip_buf[0] + chip_buf[1]) + (chip_buf[2] + chip_buf[3])
        cp = _remote(o_ref.at[mine], o_ref.at[mine], send_sems.at[4], recv_sems.at[4], sibling)
        cp.start()
        cp.wait()

    vm = pl.BlockSpec(memory_space=pltpu.VMEM)
    return pl.pallas_call(
        body, name="allreduce_small", in_specs=[vm], out_specs=vm, out_shape=S((R, 128), F32),
        scratch_shapes=[pltpu.VMEM((H, 128), F32), pltpu.VMEM((N_SHARD, H, 128), F32),
                        pltpu.SemaphoreType.DMA((5,)), pltpu.SemaphoreType.DMA((5,))],
        compiler_params=pltpu.CompilerParams(vmem_limit_bytes=VMEM_LIMIT_BYTES),
    )(vec)


def adamw_layer(w, g, m, v, layer, prev):
    _, r, c = w.shape
    tr = _row_tile(r)

    def body(w_ref, g_ref, m_ref, v_ref, *rest):
        outs = rest[-4:]
        g_val = g_ref[...]
        outs[0][0] = g_val
        outs[1][0], outs[2][0], outs[3][0] = _adamw(w_ref[0], g_val, m_ref[0], v_ref[0])

    lay = pl.BlockSpec((1, tr, c), lambda i: (layer, i, 0))
    prev = list(prev) if prev else []
    return pl.pallas_call(
        body, name="adamw_layer", grid=(r // tr,),
        in_specs=[lay, pl.BlockSpec((tr, c), lambda i: (i, 0)), lay, lay] + [ANY] * len(prev),
        out_specs=[lay] * 4, out_shape=[S(w.shape, F32)] * 4,
        input_output_aliases={4 + k: k for k in range(len(prev))}, compiler_params=_params("parallel"),
    )(w, g, m, v, *prev)


def reduce_begin(gs, pos, layer, after):
    got = reduce_pair(gs, after)
    sums, own = zip(*[pair_sum(g, o, pos) for g, o in zip(gs, got)])
    send_sems, recv_sems, sums, lands, token = chips_start(list(sums), layer)
    return dict(own=own, sums=sums, lands=lands, sems=(send_sems, recv_sems), token=token, layer=layer)


def reduce_end(pending, pos, after):
    lands = chips_wait(pending["sums"], pending["lands"], *pending["sems"], after, pending["layer"])
    return exchange_halves([chip_sum(o, p, pos) for o, p in zip(pending["own"], lands)])


W_NAMES = ("ffn1_norm", "ffn1_w_gate", "ffn1_w_up", "ffn1_w_down", "mix_norm", "w_in", "conv_w", "conv_b", "ssm_A_re", "ssm_A_im",
           "ssm_B_re", "ssm_B_im", "ssm_C_re", "ssm_C_im", "ssm_D", "ssm_log_dt", "glu_w", "glu_b", "conv_out_norm", "ssm_out_norm",
           "w_out", "ffn2_norm", "ffn2_w_gate", "ffn2_w_up", "ffn2_w_down", "ple_norm", "ple_w_gate", "ple_w_proj", "final_norm")
SMALL_ALL = SMALL + ("final_norm",)
TRANSPOSED = ("ffn1_w_gate", "ffn1_w_up", "ffn2_w_gate", "ffn2_w_up")
PACK = ROW_TILE_MAX * 128


def _pack(parts):
    flat = jnp.concatenate([p.reshape(-1) for p in parts])
    pad = (-flat.shape[0]) % PACK
    return jnp.pad(flat, (0, pad)).reshape(-1, 128)


def _unpack(vec, shapes):
    flat = vec.reshape(-1)
    out, off = [], 0
    for shp in shapes:
        size = math.prod(shp)
        out.append(flat[off:off + size].reshape(shp))
        off += size
    return out


def _step(a):
    a = {k: jnp.swapaxes(v, 1, 2) if k.removeprefix("m_").removeprefix("v_") in TRANSPOSED else v for k, v in a.items()}
    x, p, target = a["x"][0], a["p"][:, 0], a["loss_target"][0]
    depth = p.shape[0]
    L, D = x.shape
    me_s = 2 * lax.axis_index("x") + lax.axis_index("y")

    pos = jnp.stack([lax.axis_index("c"), me_s]).astype(jnp.int32)
    conv_w = gather_weights([cast_place(a["conv_w"], pos, F32)])[0]
    started, token = [], jnp.zeros((8, 128), F32)
    for l in range(depth):
        placed = {}
        for shape in dict.fromkeys(a[n].shape for n in BIG):
            names = [n for n in BIG if a[n].shape == shape]
            placed.update(zip(names, cast_place_layer([a[n] for n in names], l, pos, BF16, token)))
        started.append(gather_start([placed[n] for n in BIG], l, [conv_w]))
        token = started[-1][3]

    mats = [_ssm_mats({n: a[n][l] for n in SMALL}) for l in range(depth)]
    packed_state = [_pack([a[prefix + n] for n in SMALL_ALL]) for prefix in ("", "m_", "v_")]
    early_work = [leaf for m in mats for leaf in m] + packed_state
    forwarding = {}

    def before_last(l, h3):
        if l + 1 == depth:
            return jnp.zeros((8, 128), F32)
        send_sems, recv_sems, bufs, _ = started[l + 1]
        forwarding[l + 1] = gather_start(gather_wait(bufs, send_sems, recv_sems, [h3], l + 1), l + 1, [], forward=True)
        return forwarding[l + 1][3]

    def weights_of(l, h):
        if l:
            send_sems, recv_sems, bufs, _ = forwarding[l]
            full = gather_wait(bufs, send_sems, recv_sems, [h], l, forward=True)
        else:
            send_sems, recv_sems, bufs, _ = started[0]
            full = gather_forward(gather_wait(bufs, send_sems, recv_sems, [s[3] for s in started] + early_work, 0))
        w = {n: a[n][l] for n in SMALL if n != "conv_w"}
        w["mats"] = mats[l]
        w.update(dict(zip(BIG, full)))
        C = w["glu_w"].shape[-1]
        w["glu_w"] = w["glu_w"].reshape(C, C)
        w["w_out"] = w["w_out"].reshape(2, -1, D)
        w["ple_w_gate"] = w["ple_w_gate"].reshape(D, D)
        w["conv_w"] = conv_w[l].transpose(1, 0, 2).reshape(3, -1)
        return w

    pending, first_layer_grads = {}, []

    def on_grads(l, big, dh):
        if l == 0:
            first_layer_grads.extend(big)
            return None
        pending[l] = reduce_begin(big, pos, l, [])
        return pending[l]["token"]

    loss_part, dx, smalls, g_final = local_step(x, p, target, a["final_norm"], weights_of, before_last, on_grads)
    small_shapes = [(depth,) + smalls[0][n].shape for n in SMALL] + [g_final.shape, (1,)]
    parts = [smalls[l][n] for n in SMALL for l in range(depth)] + [g_final, loss_part[0, 0:1]]
    summed_vec = allreduce_small(_pack(parts))
    pending[0] = reduce_begin(first_layer_grads, pos, 0, [summed_vec])
    stacked = [None] * len(BIG)
    for l in reversed(range(depth)):
        after = [pending[0]["token"]] if l else [s[3] for s in stacked]
        reduced = reduce_end(pending[l], pos, after)
        stacked = [adamw_layer(a[n], reduced[i], a["m_" + n], a["v_" + n], l, stacked[i]) for i, n in enumerate(BIG)]
    big_out = {n: [jnp.swapaxes(o, 1, 2) for o in outs] if n in TRANSPOSED else outs for n, outs in zip(BIG, stacked)}

    summed = _unpack(summed_vec, small_shapes)
    g_small = dict(zip(SMALL_ALL, summed[:-1]))
    loss = summed[-1][0]
    n_conv = a["conv_w"].shape[-1]
    g_small["conv_w"] = lax.dynamic_slice_in_dim(g_small["conv_w"], me_s * n_conv, n_conv, axis=2)
    g_small = {n: g_small[n].reshape(a[n].shape) for n in SMALL_ALL}
    packed = [packed_state[0], _pack([g_small[n] for n in SMALL_ALL]), packed_state[1], packed_state[2]]
    shapes = [a[n].shape for n in SMALL_ALL]
    d_s, m_s, v_s = [dict(zip(SMALL_ALL, _unpack(o, shapes))) for o in elementwise(_adamw, packed, [F32, F32, F32], "adamw_small")]

    outs = {n: big_out[n] if n in big_out else (g_small[n], d_s[n], m_s[n], v_s[n]) for n in W_NAMES}
    return (loss, dx[None], *[outs[n][0] for n in W_NAMES], *[outs[n][1] for n in W_NAMES],
            *[outs[n][2] for n in W_NAMES], *[outs[n][3] for n in W_NAMES])


def kernel(x, p, ffn1_norm, ffn1_w_gate, ffn1_w_up, ffn1_w_down, mix_norm, w_in, conv_w, conv_b, ssm_A_re, ssm_A_im, ssm_B_re, ssm_B_im, ssm_C_re, ssm_C_im, ssm_D, ssm_log_dt, glu_w, glu_b, conv_out_norm, ssm_out_norm, w_out, ffn2_norm, ffn2_w_gate, ffn2_w_up, ffn2_w_down, ple_norm, ple_w_gate, ple_w_proj, final_norm, loss_target, m_ffn1_norm, m_ffn1_w_gate, m_ffn1_w_up, m_ffn1_w_down, m_mix_norm, m_w_in, m_conv_w, m_conv_b, m_ssm_A_re, m_ssm_A_im, m_ssm_B_re, m_ssm_B_im, m_ssm_C_re, m_ssm_C_im, m_ssm_D, m_ssm_log_dt, m_glu_w, m_glu_b, m_conv_out_norm, m_ssm_out_norm, m_w_out, m_ffn2_norm, m_ffn2_w_gate, m_ffn2_w_up, m_ffn2_w_down, m_ple_norm, m_ple_w_gate, m_ple_w_proj, m_final_norm, v_ffn1_norm, v_ffn1_w_gate, v_ffn1_w_up, v_ffn1_w_down, v_mix_norm, v_w_in, v_conv_w, v_conv_b, v_ssm_A_re, v_ssm_A_im, v_ssm_B_re, v_ssm_B_im, v_ssm_C_re, v_ssm_C_im, v_ssm_D, v_ssm_log_dt, v_glu_w, v_glu_b, v_conv_out_norm, v_ssm_out_norm, v_w_out, v_ffn2_norm, v_ffn2_w_gate, v_ffn2_w_up, v_ffn2_w_down, v_ple_norm, v_ple_w_gate, v_ple_w_proj, v_final_norm):
    return _step(dict(locals()))
```
